```python
import jax, jax.numpy as jnp
from jax import lax
import numpy as np

D_MODEL = 1024
BATCH = 8
SEQ = 16384
DEPTH = 1

PLE_DIM = 256
N_HEADS = 8
QK_NOPE = 64
QK_ROPE = 32
V_HEAD = 64
Q_LORA = 384
KV_LORA = 256
POOL_WINDOWS = (2, 4, 8, 16)
POOL_GROUP = 128
POOL_WIDTH = POOL_GROUP * len(POOL_WINDOWS)
N_BRANCH = 2
D_FF = 4 * D_MODEL
ROPE_THETA = 10000.0
EPS = 1e-6
Q_BLOCK = 128
IN_SPLITS = (Q_LORA, KV_LORA, QK_ROPE, POOL_WIDTH, D_MODEL, D_MODEL)
IN_WIDTH = sum(IN_SPLITS)

kernel_name = "hybrid_mla_multiscale_pool_gated_block"


def rmsnorm(x, g):
    xf = x.astype(jnp.float32)
    y = xf * lax.rsqrt(jnp.mean(xf * xf, axis=-1, keepdims=True) + EPS)
    return (y * g.astype(jnp.float32)).astype(x.dtype)


def rope_cos_sin(positions, dim, dtype):
    inv_freq = ROPE_THETA ** (-jnp.arange(0, dim, 2, dtype=jnp.float32) / dim)
    ang = positions.astype(jnp.float32)[..., None] * inv_freq
    return jnp.cos(ang).astype(dtype), jnp.sin(ang).astype(dtype)


def apply_rope(x, cos, sin):
    half = x.shape[-1] // 2
    x1, x2 = x[..., :half], x[..., half:]
    return jnp.concatenate([x1 * cos - x2 * sin, x2 * cos + x1 * sin], axis=-1)


def mla_attention(q_nope, q_rope, k_nope, k_rope, v):
    B, S, H, _ = q_nope.shape
    nb = S // Q_BLOCK
    scale = (QK_NOPE + QK_ROPE) ** -0.5
    kpos = jnp.arange(S)

    def block(args):
        qn, qr, i = args
        s = jnp.einsum('bqhd,bkhd->bhqk', qn, k_nope, preferred_element_type=jnp.float32)
        s = s + jnp.einsum('bqhr,bkr->bhqk', qr, k_rope, preferred_element_type=jnp.float32)
        qpos = i * Q_BLOCK + jnp.arange(Q_BLOCK)
        mask = kpos[None, :] <= qpos[:, None]
        s = jnp.where(mask, s * scale, -jnp.inf)
        pr = jax.nn.softmax(s, axis=-1).astype(v.dtype)
        return jnp.einsum('bhqk,bkhd->bqhd', pr, v)

    qn_b = q_nope.reshape(B, nb, Q_BLOCK, H, QK_NOPE).transpose(1, 0, 2, 3, 4)
    qr_b = q_rope.reshape(B, nb, Q_BLOCK, H, QK_ROPE).transpose(1, 0, 2, 3, 4)
    out = lax.map(block, (qn_b, qr_b, jnp.arange(nb)))
    return out.transpose(1, 0, 2, 3, 4).reshape(B, S, H * V_HEAD)


def multiscale_pool(u, w_pool, pool_scale):
    B, S, _ = u.shape
    uf = u.reshape(B, S, len(POOL_WINDOWS), POOL_GROUP).astype(jnp.float32)
    cs = jnp.cumsum(uf, axis=1)
    t = jnp.arange(S)
    outs = []
    for g, w in enumerate(POOL_WINDOWS):
        c = cs[:, :, g]
        prev = jnp.pad(c, ((0, 0), (w, 0), (0, 0)))[:, :S]
        cnt = jnp.minimum(t + 1, w).astype(jnp.float32)[None, :, None]
        outs.append((c - prev) / cnt - uf[:, :, g])
    d = jnp.stack(outs, axis=2).astype(u.dtype)
    y = jnp.einsum('bsgc,gcd->bsgd', d, w_pool).reshape(B, S, POOL_WIDTH)
    return y * pool_scale


def _fwd_setup_inputs(seed: int = 0) -> dict:
    key = jax.random.key(seed)
    ks = jax.random.split(key, 24)

    def w(k, shape, fan_in):
        return jax.random.normal(k, shape, jnp.float32) * (fan_in ** -0.5)

    def gain(k, shape):
        return 1.0 + 0.05 * jax.random.normal(k, shape, jnp.float32)

    L = DEPTH
    x = jax.random.normal(ks[0], (BATCH, SEQ, D_MODEL), jnp.float32)
    p = jax.random.normal(ks[1], (DEPTH, BATCH, SEQ, PLE_DIM), jnp.float32)
    offset = jax.random.randint(ks[2], (BATCH, 1), 0, 4096, dtype=jnp.int32)
    positions = offset + jnp.arange(SEQ, dtype=jnp.int32)[None, :]
    return {
        "x": x,
        "p": p,
        "positions": positions,
        "g_pre_mix": gain(ks[3], (L, D_MODEL)),
        "w_in": w(ks[4], (L, D_MODEL, IN_WIDTH), D_MODEL),
        "b_gate": 0.01 * jax.random.normal(ks[5], (L, N_BRANCH * D_MODEL), jnp.float32),
        "g_q": gain(ks[6], (L, Q_LORA)),
        "w_uq": w(ks[7], (L, Q_LORA, N_HEADS * (QK_NOPE + QK_ROPE)), Q_LORA),
        "g_kv": gain(ks[8], (L, KV_LORA)),
        "w_ukv": w(ks[9], (L, KV_LORA, N_HEADS * (QK_NOPE + V_HEAD)), KV_LORA),
        "w_pool": w(ks[10], (L, len(POOL_WINDOWS), POOL_GROUP, POOL_GROUP), POOL_GROUP),
        "pool_scale": gain(ks[11], (L, POOL_WIDTH)),
        "w_branch_attn": w(ks[12], (L, N_HEADS * V_HEAD, D_MODEL), N_HEADS * V_HEAD),
        "w_branch_pool": w(ks[13], (L, POOL_WIDTH, D_MODEL), POOL_WIDTH),
        "w_out": w(ks[14], (L, D_MODEL, D_MODEL), D_MODEL),
        "g_post_mix": gain(ks[15], (L, D_MODEL)),
        "g_pre_mlp": gain(ks[16], (L, D_MODEL)),
        "w_ff1": w(ks[17], (L, D_MODEL, D_FF), D_MODEL),
        "w_ff2": w(ks[18], (L, D_FF, D_MODEL), D_FF),
        "g_post_mlp": gain(ks[19], (L, D_MODEL)),
        "w_ple_proj": w(ks[20], (L, PLE_DIM, D_MODEL), PLE_DIM),
        "w_ple_gate": w(ks[21], (L, D_MODEL, D_MODEL), D_MODEL),
        "g_ple": gain(ks[22], (L, D_MODEL)),
    }


def _fwd_reference(x, p, positions, g_pre_mix, w_in, b_gate, g_q, w_uq, g_kv, w_ukv,
              w_pool, pool_scale, w_branch_attn, w_branch_pool, w_out, g_post_mix,
              g_pre_mlp, w_ff1, w_ff2, g_post_mlp, w_ple_proj, w_ple_gate, g_ple):
    B, S, _ = x.shape
    cos, sin = rope_cos_sin(positions, QK_ROPE, x.dtype)
    split_idx = list(np.cumsum(IN_SPLITS)[:-1])
    h = x
    for i in range(DEPTH):
        a = rmsnorm(h, g_pre_mix[i])
        proj = a @ w_in[i]
        q_down, kv_down, k_rope, pool_in, gates = (
            *jnp.split(proj, split_idx, axis=-1)[:4],
            proj[..., sum(IN_SPLITS[:4]):])
        gates = jax.nn.sigmoid(gates + b_gate[i])
        gate_attn, gate_pool = gates[..., :D_MODEL], gates[..., D_MODEL:]

        q = (rmsnorm(q_down, g_q[i]) @ w_uq[i]).reshape(B, S, N_HEADS, QK_NOPE + QK_ROPE)
        q_nope = q[..., :QK_NOPE]
        q_rope = apply_rope(q[..., QK_NOPE:], cos[:, :, None, :], sin[:, :, None, :])
        kv = (rmsnorm(kv_down, g_kv[i]) @ w_ukv[i]).reshape(B, S, N_HEADS, QK_NOPE + V_HEAD)
        k_nope, v = kv[..., :QK_NOPE], kv[..., QK_NOPE:]
        k_rope = apply_rope(k_rope, cos, sin)
        attn = mla_attention(q_nope, q_rope, k_nope, k_rope, v)

        pooled = multiscale_pool(pool_in, w_pool[i], pool_scale[i])

        merged = gate_attn * (attn @ w_branch_attn[i]) + gate_pool * (pooled @ w_branch_pool[i])
        h = h + rmsnorm(merged @ w_out[i], g_post_mix[i])

        m = rmsnorm(h, g_pre_mlp[i])
        f = jnp.square(jax.nn.relu(m @ w_ff1[i])) @ w_ff2[i]
        h = h + rmsnorm(f, g_post_mlp[i])

        e = p[i] @ w_ple_proj[i]
        pg = jax.nn.sigmoid(h @ w_ple_gate[i])
        h = h + rmsnorm(pg * e, g_ple[i])
    return h


import jax as _jax
import jax.numpy as _jnp

TWIN_FORMAT = 'train_step'
FWD_PARAMS = ['x', 'p', 'positions', 'g_pre_mix', 'w_in', 'b_gate', 'g_q', 'w_uq', 'g_kv', 'w_ukv', 'w_pool', 'pool_scale', 'w_branch_attn', 'w_branch_pool', 'w_out', 'g_post_mix', 'g_pre_mlp', 'w_ff1', 'w_ff2', 'g_post_mlp', 'w_ple_proj', 'w_ple_gate', 'g_ple']
TWIN_WEIGHTS = ['g_pre_mix', 'w_in', 'b_gate', 'g_q', 'w_uq', 'g_kv', 'w_ukv', 'w_pool', 'pool_scale', 'w_branch_attn', 'w_branch_pool', 'w_out', 'g_post_mix', 'g_pre_mlp', 'w_ff1', 'w_ff2', 'g_post_mlp', 'w_ple_proj', 'w_ple_gate', 'g_ple']
TWIN_DIFF_INPUT = 'x'
TWIN_INPUTS = ['x', 'p', 'positions', 'g_pre_mix', 'w_in', 'b_gate', 'g_q', 'w_uq', 'g_kv', 'w_ukv', 'w_pool', 'pool_scale', 'w_branch_attn', 'w_branch_pool', 'w_out', 'g_post_mix', 'g_pre_mlp', 'w_ff1', 'w_ff2', 'g_post_mlp', 'w_ple_proj', 'w_ple_gate', 'g_ple', 'loss_target', 'm_g_pre_mix', 'm_w_in', 'm_b_gate', 'm_g_q', 'm_w_uq', 'm_g_kv', 'm_w_ukv', 'm_w_pool', 'm_pool_scale', 'm_w_branch_attn', 'm_w_branch_pool', 'm_w_out', 'm_g_post_mix', 'm_g_pre_mlp', 'm_w_ff1', 'm_w_ff2', 'm_g_post_mlp', 'm_w_ple_proj', 'm_w_ple_gate', 'm_g_ple', 'v_g_pre_mix', 'v_w_in', 'v_b_gate', 'v_g_q', 'v_w_uq', 'v_g_kv', 'v_w_ukv', 'v_w_pool', 'v_pool_scale', 'v_w_branch_attn', 'v_w_branch_pool', 'v_w_out', 'v_g_post_mix', 'v_g_pre_mlp', 'v_w_ff1', 'v_w_ff2', 'v_g_post_mlp', 'v_w_ple_proj', 'v_w_ple_gate', 'v_g_ple']
TWIN_OUTPUTS = ['loss', 'grad_x', 'grad_g_pre_mix', 'grad_w_in', 'grad_b_gate', 'grad_g_q', 'grad_w_uq', 'grad_g_kv', 'grad_w_ukv', 'grad_w_pool', 'grad_pool_scale', 'grad_w_branch_attn', 'grad_w_branch_pool', 'grad_w_out', 'grad_g_post_mix', 'grad_g_pre_mlp', 'grad_w_ff1', 'grad_w_ff2', 'grad_g_post_mlp', 'grad_w_ple_proj', 'grad_w_ple_gate', 'grad_g_ple', 'delta_g_pre_mix', 'delta_w_in', 'delta_b_gate', 'delta_g_q', 'delta_w_uq', 'delta_g_kv', 'delta_w_ukv', 'delta_w_pool', 'delta_pool_scale', 'delta_w_branch_attn', 'delta_w_branch_pool', 'delta_w_out', 'delta_g_post_mix', 'delta_g_pre_mlp', 'delta_w_ff1', 'delta_w_ff2', 'delta_g_post_mlp', 'delta_w_ple_proj', 'delta_w_ple_gate', 'delta_g_ple', 'new_m_g_pre_mix', 'new_m_w_in', 'new_m_b_gate', 'new_m_g_q', 'new_m_w_uq', 'new_m_g_kv', 'new_m_w_ukv', 'new_m_w_pool', 'new_m_pool_scale', 'new_m_w_branch_attn', 'new_m_w_branch_pool', 'new_m_w_out', 'new_m_g_post_mix', 'new_m_g_pre_mlp', 'new_m_w_ff1', 'new_m_w_ff2', 'new_m_g_post_mlp', 'new_m_w_ple_proj', 'new_m_w_ple_gate', 'new_m_g_ple', 'new_v_g_pre_mix', 'new_v_w_in', 'new_v_b_gate', 'new_v_g_q', 'new_v_w_uq', 'new_v_g_kv', 'new_v_w_ukv', 'new_v_w_pool', 'new_v_pool_scale', 'new_v_w_branch_attn', 'new_v_w_branch_pool', 'new_v_w_out', 'new_v_g_post_mix', 'new_v_g_pre_mlp', 'new_v_w_ff1', 'new_v_w_ff2', 'new_v_g_post_mlp', 'new_v_w_ple_proj', 'new_v_w_ple_gate', 'new_v_g_ple']
TWIN_LEAF_KINDS = {'loss': 'loss', 'grad_x': 'grad_x', 'grad_g_pre_mix': 'grad_w', 'grad_w_in': 'grad_w', 'grad_b_gate': 'grad_w', 'grad_g_q': 'grad_w', 'grad_w_uq': 'grad_w', 'grad_g_kv': 'grad_w', 'grad_w_ukv': 'grad_w', 'grad_w_pool': 'grad_w', 'grad_pool_scale': 'grad_w', 'grad_w_branch_attn': 'grad_w', 'grad_w_branch_pool': 'grad_w', 'grad_w_out': 'grad_w', 'grad_g_post_mix': 'grad_w', 'grad_g_pre_mlp': 'grad_w', 'grad_w_ff1': 'grad_w', 'grad_w_ff2': 'grad_w', 'grad_g_post_mlp': 'grad_w', 'grad_w_ple_proj': 'grad_w', 'grad_w_ple_gate': 'grad_w', 'grad_g_ple': 'grad_w', 'delta_g_pre_mix': 'delta_w', 'delta_w_in': 'delta_w', 'delta_b_gate': 'delta_w', 'delta_g_q': 'delta_w', 'delta_w_uq': 'delta_w', 'delta_g_kv': 'delta_w', 'delta_w_ukv': 'delta_w', 'delta_w_pool': 'delta_w', 'delta_pool_scale': 'delta_w', 'delta_w_branch_attn': 'delta_w', 'delta_w_branch_pool': 'delta_w', 'delta_w_out': 'delta_w', 'delta_g_post_mix': 'delta_w', 'delta_g_pre_mlp': 'delta_w', 'delta_w_ff1': 'delta_w', 'delta_w_ff2': 'delta_w', 'delta_g_post_mlp': 'delta_w', 'delta_w_ple_proj': 'delta_w', 'delta_w_ple_gate': 'delta_w', 'delta_g_ple': 'delta_w', 'new_m_g_pre_mix': 'new_m', 'new_m_w_in': 'new_m', 'new_m_b_gate': 'new_m', 'new_m_g_q': 'new_m', 'new_m_w_uq': 'new_m', 'new_m_g_kv': 'new_m', 'new_m_w_ukv': 'new_m', 'new_m_w_pool': 'new_m', 'new_m_pool_scale': 'new_m', 'new_m_w_branch_attn': 'new_m', 'new_m_w_branch_pool': 'new_m', 'new_m_w_out': 'new_m', 'new_m_g_post_mix': 'new_m', 'new_m_g_pre_mlp': 'new_m', 'new_m_w_ff1': 'new_m', 'new_m_w_ff2': 'new_m', 'new_m_g_post_mlp': 'new_m', 'new_m_w_ple_proj': 'new_m', 'new_m_w_ple_gate': 'new_m', 'new_m_g_ple': 'new_m', 'new_v_g_pre_mix': 'new_v', 'new_v_w_in': 'new_v', 'new_v_b_gate': 'new_v', 'new_v_g_q': 'new_v', 'new_v_w_uq': 'new_v', 'new_v_g_kv': 'new_v', 'new_v_w_ukv': 'new_v', 'new_v_w_pool': 'new_v', 'new_v_pool_scale': 'new_v', 'new_v_w_branch_attn': 'new_v', 'new_v_w_branch_pool': 'new_v', 'new_v_w_out': 'new_v', 'new_v_g_post_mix': 'new_v', 'new_v_g_pre_mlp': 'new_v', 'new_v_w_ff1': 'new_v', 'new_v_w_ff2': 'new_v', 'new_v_g_post_mlp': 'new_v', 'new_v_w_ple_proj': 'new_v', 'new_v_w_ple_gate': 'new_v', 'new_v_g_ple': 'new_v'}


def _forward(args):
    return _fwd_reference(*[args[k] for k in FWD_PARAMS])


def _output_shape():
    def fwd():
        inp = _fwd_setup_inputs(0)
        return _fwd_reference(*[inp[k] for k in FWD_PARAMS])
    out = _jax.eval_shape(fwd)
    return out.shape, out.dtype

N_MICROBATCH = 1
ADAM_LR = 0.001
ADAM_B1 = 0.9
ADAM_B2 = 0.999
ADAM_EPS = 1e-08
ADAM_WD = 0.01
ADAM_STEP = 10
PER_EXAMPLE_BATCH_AXIS = {'x': 0, 'p': 1, 'positions': 0, 'loss_target': 0}
SHARED_INPUTS = []
_WEIGHT_DTYPES = {'g_pre_mix': _jnp.float32, 'w_in': _jnp.float32, 'b_gate': _jnp.float32, 'g_q': _jnp.float32, 'w_uq': _jnp.float32, 'g_kv': _jnp.float32, 'w_ukv': _jnp.float32, 'w_pool': _jnp.float32, 'pool_scale': _jnp.float32, 'w_branch_attn': _jnp.float32, 'w_branch_pool': _jnp.float32, 'w_out': _jnp.float32, 'g_post_mix': _jnp.float32, 'g_pre_mlp': _jnp.float32, 'w_ff1': _jnp.float32, 'w_ff2': _jnp.float32, 'g_post_mlp': _jnp.float32, 'w_ple_proj': _jnp.float32, 'w_ple_gate': _jnp.float32, 'g_ple': _jnp.float32}
MOMENT_SCALE = {'g_pre_mix': 2.967037e+00, 'w_in': 1.404468e+00, 'b_gate': 1.505169e+00, 'g_q': 3.258500e-01, 'w_uq': 2.272325e-01, 'g_kv': 7.309705e-01, 'w_ukv': 3.680539e-01, 'w_pool': 4.897936e+00, 'pool_scale': 7.066242e+00, 'w_branch_attn': 3.183332e-01, 'w_branch_pool': 4.239753e+00, 'w_out': 4.581110e+00, 'g_post_mix': 1.281461e+02, 'g_pre_mlp': 4.154768e+00, 'w_ff1': 1.907401e+00, 'w_ff2': 4.429317e+00, 'g_post_mlp': 1.318631e+02, 'w_ple_proj': 7.855090e-01, 'w_ple_gate': 7.287729e-01, 'g_ple': 1.309590e+02}


def _to_microbatches(a, axis):
    t = _jnp.moveaxis(a, axis, 0)
    t = t.reshape((N_MICROBATCH, t.shape[0] // N_MICROBATCH) + t.shape[1:])
    return _jnp.moveaxis(t, 1, axis + 1)


def setup_inputs(seed: int = 0) -> dict:
    inp = _fwd_setup_inputs(seed)
    key = _jax.random.fold_in(_jax.random.key(seed), 7919)
    shape, _ = _output_shape()
    out = dict(inp)
    out["loss_target"] = _jax.random.normal(_jax.random.fold_in(key, 0), shape, _jnp.float32)
    for i, name in enumerate(TWIN_WEIGHTS):
        w = inp[name].astype(_jnp.float32)
        if MOMENT_SCALE is None:
            s = _jnp.sqrt(_jnp.mean(_jnp.square(w)) + 1e-30)
        else:
            s = MOMENT_SCALE[name]
        km, kv = _jax.random.split(_jax.random.fold_in(key, i + 1))
        out[name] = w
        out["m_" + name] = s * _jax.random.normal(km, w.shape, _jnp.float32)
        out["v_" + name] = (s * s) * _jax.random.uniform(kv, w.shape, _jnp.float32, 0.5, 1.5)
    if N_MICROBATCH > 1:
        for name, axis in PER_EXAMPLE_BATCH_AXIS.items():
            out[name] = _to_microbatches(out[name], axis)
    return {'x': out['x'], 'p': out['p'], 'positions': out['positions'], 'g_pre_mix': out['g_pre_mix'], 'w_in': out['w_in'], 'b_gate': out['b_gate'], 'g_q': out['g_q'], 'w_uq': out['w_uq'], 'g_kv': out['g_kv'], 'w_ukv': out['w_ukv'], 'w_pool': out['w_pool'], 'pool_scale': out['pool_scale'], 'w_branch_attn': out['w_branch_attn'], 'w_branch_pool': out['w_branch_pool'], 'w_out': out['w_out'], 'g_post_mix': out['g_post_mix'], 'g_pre_mlp': out['g_pre_mlp'], 'w_ff1': out['w_ff1'], 'w_ff2': out['w_ff2'], 'g_post_mlp': out['g_post_mlp'], 'w_ple_proj': out['w_ple_proj'], 'w_ple_gate': out['w_ple_gate'], 'g_ple': out['g_ple'], 'loss_target': out['loss_target'], 'm_g_pre_mix': out['m_g_pre_mix'], 'm_w_in': out['m_w_in'], 'm_b_gate': out['m_b_gate'], 'm_g_q': out['m_g_q'], 'm_w_uq': out['m_w_uq'], 'm_g_kv': out['m_g_kv'], 'm_w_ukv': out['m_w_ukv'], 'm_w_pool': out['m_w_pool'], 'm_pool_scale': out['m_pool_scale'], 'm_w_branch_attn': out['m_w_branch_attn'], 'm_w_branch_pool': out['m_w_branch_pool'], 'm_w_out': out['m_w_out'], 'm_g_post_mix': out['m_g_post_mix'], 'm_g_pre_mlp': out['m_g_pre_mlp'], 'm_w_ff1': out['m_w_ff1'], 'm_w_ff2': out['m_w_ff2'], 'm_g_post_mlp': out['m_g_post_mlp'], 'm_w_ple_proj': out['m_w_ple_proj'], 'm_w_ple_gate': out['m_w_ple_gate'], 'm_g_ple': out['m_g_ple'], 'v_g_pre_mix': out['v_g_pre_mix'], 'v_w_in': out['v_w_in'], 'v_b_gate': out['v_b_gate'], 'v_g_q': out['v_g_q'], 'v_w_uq': out['v_w_uq'], 'v_g_kv': out['v_g_kv'], 'v_w_ukv': out['v_w_ukv'], 'v_w_pool': out['v_w_pool'], 'v_pool_scale': out['v_pool_scale'], 'v_w_branch_attn': out['v_w_branch_attn'], 'v_w_branch_pool': out['v_w_branch_pool'], 'v_w_out': out['v_w_out'], 'v_g_post_mix': out['v_g_post_mix'], 'v_g_pre_mlp': out['v_g_pre_mlp'], 'v_w_ff1': out['v_w_ff1'], 'v_w_ff2': out['v_w_ff2'], 'v_g_post_mlp': out['v_g_post_mlp'], 'v_w_ple_proj': out['v_w_ple_proj'], 'v_w_ple_gate': out['v_w_ple_gate'], 'v_g_ple': out['v_g_ple']}


def _loss(weights, diff, rest, loss_target):
    with _jax.named_scope("forward"):
        args = {**rest, TWIN_DIFF_INPUT: diff, **{k: w.astype(_WEIGHT_DTYPES[k]) for k, w in weights.items()}}
        y = _forward(args)
    with _jax.named_scope("loss_head"):
        err = _jnp.square(y.astype(_jnp.float32) - loss_target)
        return 0.5 * _jnp.sum(_jnp.mean(err, axis=-1)) if err.ndim else 0.5 * err


def _adamw(w, g, m, v):
    m = ADAM_B1 * m + (1.0 - ADAM_B1) * g
    v = ADAM_B2 * v + (1.0 - ADAM_B2) * _jnp.square(g)
    m_hat = m / (1.0 - ADAM_B1 ** ADAM_STEP)
    v_hat = v / (1.0 - ADAM_B2 ** ADAM_STEP)
    delta = -ADAM_LR * (m_hat / (_jnp.sqrt(v_hat) + ADAM_EPS) + ADAM_WD * w)
    return delta, m, v


def reference(x, p, positions, g_pre_mix, w_in, b_gate, g_q, w_uq, g_kv, w_ukv, w_pool, pool_scale, w_branch_attn, w_branch_pool, w_out, g_post_mix, g_pre_mlp, w_ff1, w_ff2, g_post_mlp, w_ple_proj, w_ple_gate, g_ple, loss_target, m_g_pre_mix, m_w_in, m_b_gate, m_g_q, m_w_uq, m_g_kv, m_w_ukv, m_w_pool, m_pool_scale, m_w_branch_attn, m_w_branch_pool, m_w_out, m_g_post_mix, m_g_pre_mlp, m_w_ff1, m_w_ff2, m_g_post_mlp, m_w_ple_proj, m_w_ple_gate, m_g_ple, v_g_pre_mix, v_w_in, v_b_gate, v_g_q, v_w_uq, v_g_kv, v_w_ukv, v_w_pool, v_pool_scale, v_w_branch_attn, v_w_branch_pool, v_w_out, v_g_post_mix, v_g_pre_mlp, v_w_ff1, v_w_ff2, v_g_post_mlp, v_w_ple_proj, v_w_ple_gate, v_g_ple):
    given = dict(x=x, p=p, positions=positions, g_pre_mix=g_pre_mix, w_in=w_in, b_gate=b_gate, g_q=g_q, w_uq=w_uq, g_kv=g_kv, w_ukv=w_ukv, w_pool=w_pool, pool_scale=pool_scale, w_branch_attn=w_branch_attn, w_branch_pool=w_branch_pool, w_out=w_out, g_post_mix=g_post_mix, g_pre_mlp=g_pre_mlp, w_ff1=w_ff1, w_ff2=w_ff2, g_post_mlp=g_post_mlp, w_ple_proj=w_ple_proj, w_ple_gate=w_ple_gate, g_ple=g_ple, loss_target=loss_target, m_g_pre_mix=m_g_pre_mix, m_w_in=m_w_in, m_b_gate=m_b_gate, m_g_q=m_g_q, m_w_uq=m_w_uq, m_g_kv=m_g_kv, m_w_ukv=m_w_ukv, m_w_pool=m_w_pool, m_pool_scale=m_pool_scale, m_w_branch_attn=m_w_branch_attn, m_w_branch_pool=m_w_branch_pool, m_w_out=m_w_out, m_g_post_mix=m_g_post_mix, m_g_pre_mlp=m_g_pre_mlp, m_w_ff1=m_w_ff1, m_w_ff2=m_w_ff2, m_g_post_mlp=m_g_post_mlp, m_w_ple_proj=m_w_ple_proj, m_w_ple_gate=m_w_ple_gate, m_g_ple=m_g_ple, v_g_pre_mix=v_g_pre_mix, v_w_in=v_w_in, v_b_gate=v_b_gate, v_g_q=v_g_q, v_w_uq=v_w_uq, v_g_kv=v_g_kv, v_w_ukv=v_w_ukv, v_w_pool=v_w_pool, v_pool_scale=v_pool_scale, v_w_branch_attn=v_w_branch_attn, v_w_branch_pool=v_w_branch_pool, v_w_out=v_w_out, v_g_post_mix=v_g_post_mix, v_g_pre_mlp=v_g_pre_mlp, v_w_ff1=v_w_ff1, v_w_ff2=v_w_ff2, v_g_post_mlp=v_g_post_mlp, v_w_ple_proj=v_w_ple_proj, v_w_ple_gate=v_w_ple_gate, v_g_ple=v_g_ple)
    weights = {n: given[n] for n in TWIN_WEIGHTS}
    shared = {n: given[n] for n in SHARED_INPUTS}
    per_example = {n: given[n] for n in ['x', 'p', 'positions']}
    grad_fn = _jax.value_and_grad(_loss, argnums=(0, 1))

    def one_microbatch(ex, loss_target):
        ex = dict(ex)
        diff = ex.pop(TWIN_DIFF_INPUT)
        return grad_fn(weights, diff, {**shared, **ex}, loss_target)

    if N_MICROBATCH == 1:
        loss, (grad_w, grad_x) = one_microbatch(per_example, given["loss_target"])
    else:
        def body(carry, xs):
            loss_sum, grad_sum = carry
            l_k, (gw_k, gx_k) = one_microbatch(xs[0], xs[1])
            with _jax.named_scope("update"):
                return (loss_sum + l_k, _jax.tree.map(_jnp.add, grad_sum, gw_k)), gx_k

        init = (_jnp.zeros((), _jnp.float32), _jax.tree.map(_jnp.zeros_like, weights))
        (loss, grad_w), grad_x = _jax.lax.scan(body, init, (per_example, given["loss_target"]))
    with _jax.named_scope("update"):
        delta_w, new_m, new_v = {}, {}, {}
        for n in TWIN_WEIGHTS:
            delta_w[n], new_m[n], new_v[n] = _adamw(weights[n], grad_w[n], given["m_" + n], given["v_" + n])
    return (loss, grad_x, *[grad_w[n] for n in TWIN_WEIGHTS], *[delta_w[n] for n in TWIN_WEIGHTS],
            *[new_m[n] for n in TWIN_WEIGHTS], *[new_v[n] for n in TWIN_WEIGHTS])
```

```python
import jax
import jax.numpy as jnp
from jax import lax
from jax.experimental import pallas as pl
from jax.experimental.pallas import tpu as pltpu

F32 = jnp.float32
BF16 = jnp.bfloat16

D_MODEL = 1024
PLE_DIM = 256
N_HEADS = 8
QK_NOPE = 64
QK_ROPE = 32
V_HEAD = 64
Q_LORA = 384
KV_LORA = 256
POOL_WINDOWS = (2, 4, 8, 16)
POOL_GROUP = 128
POOL_WIDTH = 512
D_FF = 4096
ROPE_THETA = 10000.0
EPS = 1e-6
HEAD_PAD = 128
ATTN_SCALE = (QK_NOPE + QK_ROPE) ** -0.5

ADAM_LR = 0.001
ADAM_B1 = 0.9
ADAM_B2 = 0.999
ADAM_EPS = 1e-08
ADAM_WD = 0.01
ADAM_STEP = 10

N_DEV = 8
LANES = 1024
PACK_ROW_TILE = 480
POOL_HALO = 16
MIB = 2 ** 20

IN_Q0, IN_KV0, IN_POOL0, IN_GATE0, IN_KR0, IN_R = 0, 384, 640, 1152, 3200, 3328

SHARDED = (("w_in", "col"), ("w_uq", "col"), ("w_ukv", "col"), ("w_branch_attn", "col"),
           ("w_branch_pool", "col"), ("w_out", "row"), ("w_ff1", "col"), ("w_ff2", "row"),
           ("w_ple_proj", "col"), ("w_ple_gate", "row"))
REPLICATED = ("g_pre_mix", "b_gate", "g_q", "g_kv", "w_pool", "pool_scale", "g_post_mix",
              "g_pre_mlp", "g_post_mlp", "g_ple")
WEIGHT_ORDER = ("g_pre_mix", "w_in", "b_gate", "g_q", "w_uq", "g_kv", "w_ukv", "w_pool", "pool_scale",
                "w_branch_attn", "w_branch_pool", "w_out", "g_post_mix", "g_pre_mlp", "w_ff1", "w_ff2",
                "g_post_mlp", "w_ple_proj", "w_ple_gate", "g_ple")

NT = (((1,), (1,)), ((), ()))
TN = (((0,), (0,)), ((), ()))
MESH = pl.DeviceIdType.MESH
ANY = pl.BlockSpec(memory_space=pl.ANY)


def _params(n_axes, vmem_mib):
    return pltpu.CompilerParams(dimension_semantics=("arbitrary",) * n_axes, vmem_limit_bytes=vmem_mib * MIB)


def _row(tm, n):
    return pl.BlockSpec((tm, n), lambda i: (i, 0))


def _fix(shape):
    zeros = (0,) * len(shape)
    return pl.BlockSpec(shape, lambda i: zeros)


def _sds(shape, dtype):
    return jax.ShapeDtypeStruct(shape, dtype)


def _rms_r(v):
    return lax.rsqrt(jnp.mean(v * v, axis=-1, keepdims=True) + EPS)


def _rms_bwd(vhat, r, g, dy):
    gdy = dy * g
    return r * (gdy - vhat * jnp.mean(gdy * vhat, axis=-1, keepdims=True))


def _colsum(v):
    return jnp.sum(v, axis=0, keepdims=True)


def _sigmoid(v):
    return 1.0 / (1.0 + jnp.exp(-v))


def _mm(a, b):
    return jnp.dot(a, b, preferred_element_type=F32)


def _mm_nt(a, b):
    return lax.dot_general(a, b, NT, preferred_element_type=F32)


def _rope(c, cf, s1, s2):
    return c * cf + pltpu.roll(c, HEAD_PAD - 16, 1) * s1 + pltpu.roll(c, 16, 1) * s2


def _rope_t(c, cf, s1, s2):
    return c * cf + pltpu.roll(c * s1, 16, 1) + pltpu.roll(c * s2, HEAD_PAD - 16, 1)


def _load_once(pairs):
    @pl.when(pl.program_id(0) == 0)
    def _():
        for src, dst in pairs:
            pltpu.sync_copy(src, dst)


def _proj_fwd(x, g_pre, b_gate, g_q, g_kv, cf, s1, s2, w_in_r, w_uq_r, w_k_exp, w_v, tm):
    seq = x.shape[0]

    def body(x_ref, gpre_ref, bg_ref, gq_ref, gkv_ref, cf_ref, s1_ref, s2_ref, win_hbm, wuq_hbm, wk_hbm, wv_hbm,
             a_ref, qd_ref, kvd_ref, pin_ref, gates_ref, q_ref, k_ref, v_ref, win, wuq, wk, wv):
        _load_once(((win_hbm, win), (wuq_hbm, wuq), (wk_hbm, wk), (wv_hbm, wv)))
        xv = x_ref[...]
        a = (xv * _rms_r(xv) * gpre_ref[...]).astype(BF16)
        a_ref[...] = a
        proj = _mm(a, win[...])
        qd = proj[:, IN_Q0:IN_KV0]
        kvd = proj[:, IN_KV0:IN_POOL0]
        qd_ref[...] = qd
        kvd_ref[...] = kvd
        pin_ref[...] = proj[:, IN_POOL0:IN_GATE0]
        gates_ref[...] = _sigmoid(proj[:, IN_GATE0:IN_KR0] + bg_ref[...]).astype(BF16)
        cfv, s1v, s2v = cf_ref[...], s1_ref[...], s2_ref[...]
        krr = _rope(proj[:, IN_KR0:IN_R], cfv, s1v, s2v)
        qn = (qd * _rms_r(qd) * gq_ref[...]).astype(BF16)
        qf = _mm(qn, wuq[...])
        kvn = (kvd * _rms_r(kvd) * gkv_ref[...]).astype(BF16)
        kf = _mm(kvn, wk[...])
        for h in range(N_HEADS):
            lanes = slice(HEAD_PAD * h, HEAD_PAD * (h + 1))
            q_ref[h] = _rope(qf[:, lanes], cfv, s1v, s2v).astype(BF16)
            k_ref[h] = (kf[:, lanes] + krr).astype(BF16)
        v_ref[...] = _mm(kvn, wv[...]).astype(BF16)

    heads = pl.BlockSpec((N_HEADS, tm, HEAD_PAD), lambda i: (0, i, 0))
    return pl.pallas_call(
        body, name="proj_fwd", grid=(seq // tm,),
        in_specs=[_row(tm, D_MODEL), _fix((1, D_MODEL)), _fix((1, 2 * D_MODEL)), _fix((1, Q_LORA)), _fix((1, KV_LORA)),
                  _row(tm, HEAD_PAD), _row(tm, HEAD_PAD), _row(tm, HEAD_PAD), ANY, ANY, ANY, ANY],
        out_specs=[_row(tm, D_MODEL), _row(tm, Q_LORA), _row(tm, KV_LORA), _row(tm, POOL_WIDTH), _row(tm, 2 * D_MODEL),
                   heads, heads, _row(tm, N_HEADS * V_HEAD)],
        out_shape=[_sds((seq, D_MODEL), BF16), _sds((seq, Q_LORA), F32), _sds((seq, KV_LORA), F32),
                   _sds((seq, POOL_WIDTH), F32), _sds((seq, 2 * D_MODEL), BF16),
                   _sds((N_HEADS, seq, HEAD_PAD), BF16), _sds((N_HEADS, seq, HEAD_PAD), BF16),
                   _sds((seq, N_HEADS * V_HEAD), BF16)],
        scratch_shapes=[pltpu.VMEM(w_in_r.shape, BF16), pltpu.VMEM(w_uq_r.shape, BF16),
                        pltpu.VMEM(w_k_exp.shape, BF16), pltpu.VMEM(w_v.shape, BF16)],
        compiler_params=_params(1, 48),
    )(x, g_pre, b_gate, g_q, g_kv, cf, s1, s2, w_in_r, w_uq_r, w_k_exp, w_v)


def _window_count(row0, n_rows):
    t = row0 + lax.broadcasted_iota(jnp.int32, (n_rows, POOL_GROUP), 0)
    return [jnp.minimum(t + 1, w).astype(F32) for w in POOL_WINDOWS]


def _pool_fwd(pin, w_pool_bf, pool_scale, tm):
    seq = pin.shape[0]
    ext_rows = tm + POOL_HALO

    def body(prev_ref, u_ref, wp_ref, ps_ref, d_ref, pooled_ref):
        i = pl.program_id(0)
        prev = jnp.where(i == 0, 0.0, prev_ref[...])
        u = u_ref[...]
        level = jnp.concatenate([prev, u], axis=0)
        counts = _window_count(i * tm, tm)
        shift = 1
        for g in range(len(POOL_WINDOWS)):
            level = level + pltpu.roll(level, shift, 0)
            shift *= 2
            lanes = slice(POOL_GROUP * g, POOL_GROUP * (g + 1))
            d = (level[POOL_HALO:, lanes] / counts[g] - u[:, lanes]).astype(BF16)
            d_ref[:, lanes] = d
            pooled_ref[:, lanes] = (_mm(d, wp_ref[g]) * ps_ref[:, lanes]).astype(BF16)

    halo = tm // POOL_HALO
    return pl.pallas_call(
        body, name="pool_fwd", grid=(seq // tm,),
        in_specs=[pl.BlockSpec((POOL_HALO, POOL_WIDTH), lambda i: (jnp.maximum(i * halo - 1, 0), 0)),
                  _row(tm, POOL_WIDTH), _fix(w_pool_bf.shape), _fix((1, POOL_WIDTH))],
        out_specs=[_row(tm, POOL_WIDTH), _row(tm, POOL_WIDTH)],
        out_shape=[_sds((seq, POOL_WIDTH), BF16), _sds((seq, POOL_WIDTH), BF16)],
        compiler_params=_params(1, 32),
    )(pin, pin, w_pool_bf, pool_scale)


def _pool_bwd_window(dd, tm):
    seq = dd.shape[0]
    n_tiles = seq // tm
    ext_rows = tm + POOL_HALO

    def body(dd_ref, next_ref, dpin_ref):
        i = pl.program_id(0)
        nxt = jnp.where(i == n_tiles - 1, 0.0, next_ref[...])
        dd_t = dd_ref[...]
        ext = jnp.concatenate([dd_t, nxt], axis=0)
        counts = _window_count(i * tm, ext_rows)
        shift = 1
        for g in range(len(POOL_WINDOWS)):
            lanes = slice(POOL_GROUP * g, POOL_GROUP * (g + 1))
            level = ext[:, lanes] / counts[g]
            s = 1
            while s <= shift:
                level = level + pltpu.roll(level, ext_rows - s, 0)
                s *= 2
            shift *= 2
            dpin_ref[:, lanes] = (level[:tm] - dd_t[:, lanes]).astype(BF16)

    halo = tm // POOL_HALO
    return pl.pallas_call(
        body, name="pool_bwd_window", grid=(n_tiles,),
        in_specs=[_row(tm, POOL_WIDTH),
                  pl.BlockSpec((POOL_HALO, POOL_WIDTH), lambda i: (jnp.minimum((i + 1) * halo, seq // POOL_HALO - 1), 0))],
        out_specs=_row(tm, POOL_WIDTH),
        out_shape=_sds((seq, POOL_WIDTH), BF16),
        compiler_params=_params(1, 32),
    )(dd, dd)


def _col_to_row(col, n):
    return jnp.transpose(jnp.broadcast_to(col, (n, HEAD_PAD)))[0:1, :]


def _attn_fwd(q, k, v, tq):
    heads, seq, _ = q.shape
    nq = seq // tq

    def body(q_ref, k_ref, v_ref, o_ref, lse_ref):
        qi = pl.program_id(1)
        qt = q_ref[...]

        def step(j, carry, diagonal):
            m, l, acc = carry
            start = pl.multiple_of(j * tq, tq)
            kt = k_ref[pl.ds(start, tq), :]
            vt = v_ref[pl.ds(start, tq), :]
            s = _mm_nt(qt, kt) * ATTN_SCALE
            if diagonal:
                row = lax.broadcasted_iota(jnp.int32, (tq, tq), 0)
                col = lax.broadcasted_iota(jnp.int32, (tq, tq), 1)
                s = jnp.where(col <= row, s, -jnp.inf)
            m_new = jnp.maximum(m, jnp.max(s, axis=1, keepdims=True))
            alpha = jnp.exp(m - m_new)
            p = jnp.exp(s - m_new)
            l = alpha * l + jnp.sum(p, axis=1, keepdims=True)
            acc = alpha * acc + _mm(p.astype(BF16), vt)
            return m_new, l, acc

        init = (jnp.full((tq, 1), -jnp.inf, F32), jnp.zeros((tq, 1), F32), jnp.zeros((tq, V_HEAD), F32))
        carry = lax.fori_loop(0, qi, lambda j, c: step(j, c, False), init)
        m, l, acc = step(qi, carry, True)
        o_ref[...] = (acc / l).astype(BF16)
        lse_ref[...] = _col_to_row(m + jnp.log(l), tq)

    return pl.pallas_call(
        body, name="attn_fwd", grid=(heads, nq),
        in_specs=[pl.BlockSpec((None, tq, HEAD_PAD), lambda h, i: (h, i, 0)),
                  pl.BlockSpec((None, seq, HEAD_PAD), lambda h, i: (h, 0, 0)),
                  pl.BlockSpec((None, seq, V_HEAD), lambda h, i: (h, 0, 0))],
        out_specs=[pl.BlockSpec((None, tq, V_HEAD), lambda h, i: (h, i, 0)),
                   pl.BlockSpec((None, None, 1, tq), lambda h, i: (h, i, 0, 0))],
        out_shape=[_sds((heads, seq, V_HEAD), BF16), _sds((heads, nq, 1, tq), F32)],
        compiler_params=_params(2, 48),
    )(q, k, v)


def _attn_delta(o, do, tq):
    heads, seq, _ = o.shape
    nq = seq // tq

    def body(o_ref, do_ref, delta_ref):
        prod = o_ref[...].astype(F32) * do_ref[...].astype(F32)
        delta_ref[...] = _col_to_row(jnp.sum(prod, axis=1, keepdims=True), tq)

    tile = pl.BlockSpec((None, tq, V_HEAD), lambda h, i: (h, i, 0))
    return pl.pallas_call(
        body, name="attn_delta", grid=(heads, nq),
        in_specs=[tile, tile],
        out_specs=pl.BlockSpec((None, None, 1, tq), lambda h, i: (h, i, 0, 0)),
        out_shape=_sds((heads, nq, 1, tq), F32),
        compiler_params=_params(2, 32),
    )(o, do)


def _attn_bwd(q, k, v, do, lse, delta, tq):
    heads, seq, _ = q.shape
    nq = seq // tq

    def body(q_ref, k_ref, v_ref, do_ref, lse_ref, delta_ref, dq_ref, dk_ref, dv_ref):
        j = pl.program_id(1)

        @pl.when(j == 0)
        def _():
            dq_ref[...] = jnp.zeros_like(dq_ref)

        kt = k_ref[...]
        vt = v_ref[...]

        def step(i, carry, diagonal):
            dk, dv = carry
            start = pl.multiple_of(i * tq, tq)
            qt = q_ref[pl.ds(start, tq), :]
            dot = do_ref[pl.ds(start, tq), :]
            p_t = jnp.exp(_mm_nt(kt, qt) * ATTN_SCALE - lse_ref[i])
            if diagonal:
                key = lax.broadcasted_iota(jnp.int32, (tq, tq), 0)
                query = lax.broadcasted_iota(jnp.int32, (tq, tq), 1)
                p_t = jnp.where(key <= query, p_t, 0.0)
            dv = dv + _mm(p_t.astype(BF16), dot)
            ds_t = (p_t * (_mm_nt(vt, dot) - delta_ref[i])).astype(BF16)
            dk = dk + _mm(ds_t, qt)
            dq_ref[pl.ds(start, tq), :] += lax.dot_general(ds_t, kt, TN, preferred_element_type=F32)
            return dk, dv

        carry = step(j, (jnp.zeros((tq, HEAD_PAD), F32), jnp.zeros((tq, V_HEAD), F32)), True)
        dk, dv = lax.fori_loop(j + 1, nq, lambda i, c: step(i, c, False), carry)
        dk_ref[...] = (dk * ATTN_SCALE).astype(BF16)
        dv_ref[...] = dv.astype(BF16)

    stats = pl.BlockSpec((None, nq, 1, tq), lambda h, j: (h, 0, 0, 0))
    return pl.pallas_call(
        body, name="attn_bwd", grid=(heads, nq),
        in_specs=[pl.BlockSpec((None, seq, HEAD_PAD), lambda h, j: (h, 0, 0)),
                  pl.BlockSpec((None, tq, HEAD_PAD), lambda h, j: (h, j, 0)),
                  pl.BlockSpec((None, tq, V_HEAD), lambda h, j: (h, j, 0)),
                  pl.BlockSpec((None, seq, V_HEAD), lambda h, j: (h, 0, 0)),
                  stats, stats],
        out_specs=[pl.BlockSpec((None, seq, HEAD_PAD), lambda h, j: (h, 0, 0)),
                   pl.BlockSpec((None, tq, HEAD_PAD), lambda h, j: (h, j, 0)),
                   pl.BlockSpec((None, tq, V_HEAD), lambda h, j: (h, j, 0))],
        out_shape=[_sds((heads, seq, HEAD_PAD), F32), _sds((heads, seq, HEAD_PAD), BF16),
                   _sds((heads, seq, V_HEAD), BF16)],
        compiler_params=_params(2, 56),
    )(q, k, v, do, lse, delta)


def _merge_fwd(attn, pooled, gates, x, g_post_mix, w_ba, w_bb, w_out, tm):
    seq = x.shape[0]

    def body(attn_ref, pooled_ref, gates_ref, x_ref, g_ref, wba_ref, wbb_ref, wout_ref,
             merged_ref, ba_ref, bb_ref, y_ref, h1_ref):
        ba = _mm(attn_ref[...], wba_ref[...])
        bb = _mm(pooled_ref[...], wbb_ref[...])
        ba_ref[...] = ba.astype(BF16)
        bb_ref[...] = bb.astype(BF16)
        merged = (gates_ref[:, :D_MODEL].astype(F32) * ba + gates_ref[:, D_MODEL:].astype(F32) * bb).astype(BF16)
        merged_ref[...] = merged
        y = _mm(merged, wout_ref[...])
        y_ref[...] = y
        h1_ref[...] = x_ref[...] + y * _rms_r(y) * g_ref[...]

    return pl.pallas_call(
        body, name="merge_fwd", grid=(seq // tm,),
        in_specs=[_row(tm, N_HEADS * V_HEAD), _row(tm, POOL_WIDTH), _row(tm, 2 * D_MODEL), _row(tm, D_MODEL),
                  _fix((1, D_MODEL)), _fix(w_ba.shape), _fix(w_bb.shape), _fix(w_out.shape)],
        out_specs=[_row(tm, D_MODEL)] * 5,
        out_shape=[_sds((seq, D_MODEL), BF16), _sds((seq, D_MODEL), BF16), _sds((seq, D_MODEL), BF16),
                   _sds((seq, D_MODEL), F32), _sds((seq, D_MODEL), F32)],
        compiler_params=_params(1, 48),
    )(attn, pooled, gates, x, g_post_mix, w_ba, w_bb, w_out)


def _tail_fwd(h1, target, p, g_pre_mlp, g_post_mlp, g_ple, w_ff1, w_ff2, w_pe, w_pg, tm):
    seq = h1.shape[0]

    def body(h1_ref, tgt_ref, p_ref, gm_ref, gf_ref, gp_ref, w1_hbm, w2_hbm, wpe_hbm, wpg_hbm,
             m_ref, zr_ref, f_ref, h2b_ref, pb_ref, de_ref, dpre_ref, dh2_ref, loss_ref, dgple_ref,
             w1, w2, wpe, wpg):
        _load_once(((w1_hbm, w1), (w2_hbm, w2), (wpe_hbm, wpe), (wpg_hbm, wpg)))

        @pl.when(pl.program_id(0) == 0)
        def _():
            loss_ref[...] = jnp.zeros_like(loss_ref)
            dgple_ref[...] = jnp.zeros_like(dgple_ref)

        h1v = h1_ref[...]
        m = (h1v * _rms_r(h1v) * gm_ref[...]).astype(BF16)
        m_ref[...] = m
        zr = jnp.maximum(_mm(m, w1[...]), 0.0)
        zr_ref[...] = zr.astype(BF16)
        f = _mm((zr * zr).astype(BF16), w2[...])
        f_ref[...] = f
        h2 = h1v + f * _rms_r(f) * gf_ref[...]
        h2b = h2.astype(BF16)
        h2b_ref[...] = h2b
        pb = p_ref[...].astype(BF16)
        pb_ref[...] = pb
        e = _mm(pb, wpe[...])
        pg = _sigmoid(_mm(h2b, wpg[...]))
        t3 = pg * e
        r3 = _rms_r(t3)
        t3hat = t3 * r3
        diff = h2 + t3hat * gp_ref[...] - tgt_ref[...]
        loss_ref[...] += jnp.sum(diff * diff) * (0.5 / D_MODEL)
        dh3 = diff * (1.0 / D_MODEL)
        dgple_ref[...] += _colsum(dh3 * t3hat)
        dt3 = _rms_bwd(t3hat, r3, gp_ref[...], dh3)
        de_ref[...] = (dt3 * pg).astype(BF16)
        dpre = (dt3 * e * pg * (1.0 - pg)).astype(BF16)
        dpre_ref[...] = dpre
        dh2_ref[...] = dh3 + _mm_nt(dpre, wpg[...])

    return pl.pallas_call(
        body, name="tail_fwd", grid=(seq // tm,),
        in_specs=[_row(tm, D_MODEL), _row(tm, D_MODEL), _row(tm, PLE_DIM), _fix((1, D_MODEL)), _fix((1, D_MODEL)),
                  _fix((1, D_MODEL)), ANY, ANY, ANY, ANY],
        out_specs=[_row(tm, D_MODEL), _row(tm, D_FF), _row(tm, D_MODEL), _row(tm, D_MODEL), _row(tm, PLE_DIM),
                   _row(tm, D_MODEL), _row(tm, D_MODEL), _row(tm, D_MODEL), _fix((8, 128)), _fix((1, D_MODEL))],
        out_shape=[_sds((seq, D_MODEL), BF16), _sds((seq, D_FF), BF16), _sds((seq, D_MODEL), F32),
                   _sds((seq, D_MODEL), BF16), _sds((seq, PLE_DIM), BF16), _sds((seq, D_MODEL), BF16),
                   _sds((seq, D_MODEL), BF16), _sds((seq, D_MODEL), F32), _sds((8, 128), F32), _sds((1, D_MODEL), F32)],
        scratch_shapes=[pltpu.VMEM(w_ff1.shape, BF16), pltpu.VMEM(w_ff2.shape, BF16),
                        pltpu.VMEM(w_pe.shape, BF16), pltpu.VMEM(w_pg.shape, BF16)],
        compiler_params=_params(1, 56),
    )(h1, target, p, g_pre_mlp, g_post_mlp, g_ple, w_ff1, w_ff2, w_pe, w_pg)


def _mlp_bwd(h1, f, zr, dh2, g_pre_mlp, g_post_mlp, w_ff1, w_ff2, tm):
    seq = h1.shape[0]

    def body(h1_ref, f_ref, zr_ref, dh2_ref, gm_ref, gf_ref, w1_hbm, w2_hbm,
             df_ref, dz_ref, dh1_ref, dgm_ref, dgf_ref, w1, w2):
        _load_once(((w1_hbm, w1), (w2_hbm, w2)))

        @pl.when(pl.program_id(0) == 0)
        def _():
            dgm_ref[...] = jnp.zeros_like(dgm_ref)
            dgf_ref[...] = jnp.zeros_like(dgf_ref)

        dh2 = dh2_ref[...]
        fv = f_ref[...]
        rf = _rms_r(fv)
        fhat = fv * rf
        dgf_ref[...] += _colsum(dh2 * fhat)
        df = _rms_bwd(fhat, rf, gf_ref[...], dh2).astype(BF16)
        df_ref[...] = df
        dz = (_mm_nt(df, w2[...]) * (2.0 * zr_ref[...].astype(F32))).astype(BF16)
        dz_ref[...] = dz
        dm = _mm_nt(dz, w1[...])
        h1v = h1_ref[...]
        r1 = _rms_r(h1v)
        h1hat = h1v * r1
        dgm_ref[...] += _colsum(dm * h1hat)
        dh1_ref[...] = dh2 + _rms_bwd(h1hat, r1, gm_ref[...], dm)

    return pl.pallas_call(
        body, name="mlp_bwd", grid=(seq // tm,),
        in_specs=[_row(tm, D_MODEL), _row(tm, D_MODEL), _row(tm, D_FF), _row(tm, D_MODEL),
                  _fix((1, D_MODEL)), _fix((1, D_MODEL)), ANY, ANY],
        out_specs=[_row(tm, D_MODEL), _row(tm, D_FF), _row(tm, D_MODEL), _fix((1, D_MODEL)), _fix((1, D_MODEL))],
        out_shape=[_sds((seq, D_MODEL), BF16), _sds((seq, D_FF), BF16), _sds((seq, D_MODEL), F32),
                   _sds((1, D_MODEL), F32), _sds((1, D_MODEL), F32)],
        scratch_shapes=[pltpu.VMEM(w_ff1.shape, BF16), pltpu.VMEM(w_ff2.shape, BF16)],
        compiler_params=_params(1, 56),
    )(h1, f, zr, dh2, g_pre_mlp, g_post_mlp, w_ff1, w_ff2)


def _merge_bwd(dh1, y, gates, ba, bb, d_pool, g_post_mix, pool_scale, w_out, w_ba, w_bb, w_pool_bf, tm):
    seq = dh1.shape[0]

    def body(dh1_ref, y_ref, gates_ref, ba_ref, bb_ref, d_ref, g_ref, ps_ref, wout_ref, wba_ref, wbb_ref, wp_ref,
             dy_ref, dba_ref, dbb_ref, dgpre_ref, dattn_ref, dyp_ref, dd_ref, dg_ref, dbg_ref, dps_ref):
        @pl.when(pl.program_id(0) == 0)
        def _():
            dg_ref[...] = jnp.zeros_like(dg_ref)
            dbg_ref[...] = jnp.zeros_like(dbg_ref)
            dps_ref[...] = jnp.zeros_like(dps_ref)

        dh1v = dh1_ref[...]
        yv = y_ref[...]
        r = _rms_r(yv)
        yhat = yv * r
        dg_ref[...] += _colsum(dh1v * yhat)
        dy = _rms_bwd(yhat, r, g_ref[...], dh1v).astype(BF16)
        dy_ref[...] = dy
        dmerged = _mm_nt(dy, wout_ref[...])
        for half, branch_ref, dbranch_ref in ((0, ba_ref, dba_ref), (1, bb_ref, dbb_ref)):
            lanes = slice(D_MODEL * half, D_MODEL * (half + 1))
            gate = gates_ref[:, lanes].astype(F32)
            dpre = dmerged * branch_ref[...].astype(F32) * gate * (1.0 - gate)
            dbg_ref[:, lanes] += _colsum(dpre)
            dgpre_ref[:, lanes] = dpre.astype(BF16)
            dbranch_ref[...] = (dmerged * gate).astype(BF16)
        dattn_ref[...] = _mm_nt(dba_ref[...], wba_ref[...]).astype(BF16)
        dpooled = _mm_nt(dbb_ref[...], wbb_ref[...])
        for g in range(len(POOL_WINDOWS)):
            lanes = slice(POOL_GROUP * g, POOL_GROUP * (g + 1))
            dpl = dpooled[:, lanes]
            dps_ref[:, lanes] += _colsum(dpl * _mm(d_ref[:, lanes], wp_ref[g]))
            dyp = (dpl * ps_ref[:, lanes]).astype(BF16)
            dyp_ref[:, lanes] = dyp
            dd_ref[:, lanes] = _mm_nt(dyp, wp_ref[g])

    return pl.pallas_call(
        body, name="merge_bwd", grid=(seq // tm,),
        in_specs=[_row(tm, D_MODEL), _row(tm, D_MODEL), _row(tm, 2 * D_MODEL), _row(tm, D_MODEL), _row(tm, D_MODEL),
                  _row(tm, POOL_WIDTH), _fix((1, D_MODEL)), _fix((1, POOL_WIDTH)),
                  _fix(w_out.shape), _fix(w_ba.shape), _fix(w_bb.shape), _fix(w_pool_bf.shape)],
        out_specs=[_row(tm, D_MODEL), _row(tm, D_MODEL), _row(tm, D_MODEL), _row(tm, 2 * D_MODEL),
                   _row(tm, N_HEADS * V_HEAD), _row(tm, POOL_WIDTH), _row(tm, POOL_WIDTH),
                   _fix((1, D_MODEL)), _fix((1, 2 * D_MODEL)), _fix((1, POOL_WIDTH))],
        out_shape=[_sds((seq, D_MODEL), BF16), _sds((seq, D_MODEL), BF16), _sds((seq, D_MODEL), BF16),
                   _sds((seq, 2 * D_MODEL), BF16), _sds((seq, N_HEADS * V_HEAD), BF16), _sds((seq, POOL_WIDTH), BF16),
                   _sds((seq, POOL_WIDTH), F32), _sds((1, D_MODEL), F32), _sds((1, 2 * D_MODEL), F32),
                   _sds((1, POOL_WIDTH), F32)],
        compiler_params=_params(1, 48),
    )(dh1, y, gates, ba, bb, d_pool, g_post_mix, pool_scale, w_out, w_ba, w_bb, w_pool_bf)


def _proj_bwd(dq, dk, dv, qd, kvd, x, dh1, dgpre, dpin, cf, s1, s2, g_pre, g_q, g_kv,
              w_in_r, w_uq_r, w_k_exp, w_v, tm):
    seq = x.shape[0]

    def body(dq_ref, dk_ref, dv_ref, qd_ref, kvd_ref, x_ref, dh1_ref, dgpre_ref, dpin_ref, cf_ref, s1_ref, s2_ref,
             gpre_ref, gq_ref, gkv_ref, win_hbm, wuq_hbm, wk_hbm, wv_hbm,
             gx_ref, dproj_ref, dqb_ref, qn_ref, kvn_ref, dkb_ref, dgpre_acc, dgq_acc, dgkv_acc,
             win, wuq, wk, wv):
        _load_once(((win_hbm, win), (wuq_hbm, wuq), (wk_hbm, wk), (wv_hbm, wv)))

        @pl.when(pl.program_id(0) == 0)
        def _():
            dgpre_acc[...] = jnp.zeros_like(dgpre_acc)
            dgq_acc[...] = jnp.zeros_like(dgq_acc)
            dgkv_acc[...] = jnp.zeros_like(dgkv_acc)

        cfv, s1v, s2v = cf_ref[...], s1_ref[...], s2_ref[...]
        ksum = jnp.zeros((tm, HEAD_PAD), F32)
        for h in range(N_HEADS):
            lanes = slice(HEAD_PAD * h, HEAD_PAD * (h + 1))
            dqb_ref[:, lanes] = (_rope_t(dq_ref[h], cfv, s1v, s2v) * ATTN_SCALE).astype(BF16)
            dkh = dk_ref[h]
            dkb_ref[:, lanes] = dkh
            ksum = ksum + dkh.astype(F32)
        lane = lax.broadcasted_iota(jnp.int32, (tm, HEAD_PAD), 1)
        rope_lanes = (lane >= QK_NOPE) & (lane < QK_NOPE + QK_ROPE)
        dkr = _rope_t(jnp.where(rope_lanes, ksum, 0.0), cfv, s1v, s2v)

        qdv = qd_ref[...]
        rq = _rms_r(qdv)
        qhat = qdv * rq
        qn_ref[...] = (qhat * gq_ref[...]).astype(BF16)
        dqn = _mm_nt(dqb_ref[...], wuq[...])
        dgq_acc[...] += _colsum(dqn * qhat)
        dproj_ref[:, IN_Q0:IN_KV0] = _rms_bwd(qhat, rq, gq_ref[...], dqn).astype(BF16)

        kvdv = kvd_ref[...]
        rkv = _rms_r(kvdv)
        kvhat = kvdv * rkv
        kvn_ref[...] = (kvhat * gkv_ref[...]).astype(BF16)
        dkvn = _mm_nt(dkb_ref[...], wk[...]) + _mm_nt(dv_ref[...], wv[...])
        dgkv_acc[...] += _colsum(dkvn * kvhat)
        dproj_ref[:, IN_KV0:IN_POOL0] = _rms_bwd(kvhat, rkv, gkv_ref[...], dkvn).astype(BF16)

        dproj_ref[:, IN_POOL0:IN_GATE0] = dpin_ref[...]
        dproj_ref[:, IN_GATE0:IN_KR0] = dgpre_ref[...]
        dproj_ref[:, IN_KR0:IN_R] = dkr.astype(BF16)

        da = _mm_nt(dproj_ref[...], win[...])
        xv = x_ref[...]
        r0 = _rms_r(xv)
        xhat = xv * r0
        dgpre_acc[...] += _colsum(da * xhat)
        gx_ref[...] = dh1_ref[...] + _rms_bwd(xhat, r0, gpre_ref[...], da)

    heads = pl.BlockSpec((N_HEADS, tm, HEAD_PAD), lambda i: (0, i, 0))
    return pl.pallas_call(
        body, name="proj_bwd", grid=(seq // tm,),
        in_specs=[heads, heads, _row(tm, N_HEADS * V_HEAD), _row(tm, Q_LORA), _row(tm, KV_LORA), _row(tm, D_MODEL),
                  _row(tm, D_MODEL), _row(tm, 2 * D_MODEL), _row(tm, POOL_WIDTH),
                  _row(tm, HEAD_PAD), _row(tm, HEAD_PAD), _row(tm, HEAD_PAD),
                  _fix((1, D_MODEL)), _fix((1, Q_LORA)), _fix((1, KV_LORA)), ANY, ANY, ANY, ANY],
        out_specs=[_row(tm, D_MODEL), _row(tm, IN_R), _row(tm, N_HEADS * HEAD_PAD), _row(tm, Q_LORA), _row(tm, KV_LORA),
                   _row(tm, N_HEADS * HEAD_PAD), _fix((1, D_MODEL)), _fix((1, Q_LORA)), _fix((1, KV_LORA))],
        out_shape=[_sds((seq, D_MODEL), F32), _sds((seq, IN_R), BF16), _sds((seq, N_HEADS * HEAD_PAD), BF16),
                   _sds((seq, Q_LORA), BF16), _sds((seq, KV_LORA), BF16), _sds((seq, N_HEADS * HEAD_PAD), BF16),
                   _sds((1, D_MODEL), F32), _sds((1, Q_LORA), F32), _sds((1, KV_LORA), F32)],
        scratch_shapes=[pltpu.VMEM(w_in_r.shape, BF16), pltpu.VMEM(w_uq_r.shape, BF16),
                        pltpu.VMEM(w_k_exp.shape, BF16), pltpu.VMEM(w_v.shape, BF16)],
        compiler_params=_params(1, 48),
    )(dq, dk, dv, qd, kvd, x, dh1, dgpre, dpin, cf, s1, s2, g_pre, g_q, g_kv, w_in_r, w_uq_r, w_k_exp, w_v)


def _grad_w(a, b, name, square_a=False):
    seq, k_dim = a.shape
    n_dim = b.shape[1]
    tk = min(k_dim, 1024)
    tn = n_dim // 2 if n_dim == IN_R else min(n_dim, 1024)
    ts = min(seq, 1024)

    def body(a_ref, b_ref, o_ref):
        @pl.when(pl.program_id(2) == 0)
        def _():
            o_ref[...] = jnp.zeros_like(o_ref)

        at = a_ref[...]
        if square_a:
            at = at * at
        o_ref[...] += lax.dot_general(at, b_ref[...], TN, preferred_element_type=F32)

    return pl.pallas_call(
        body, name=name, grid=(k_dim // tk, n_dim // tn, seq // ts),
        in_specs=[pl.BlockSpec((ts, tk), lambda i, j, s: (s, i)), pl.BlockSpec((ts, tn), lambda i, j, s: (s, j))],
        out_specs=pl.BlockSpec((tk, tn), lambda i, j, s: (i, j)),
        out_shape=_sds((k_dim, n_dim), F32),
        compiler_params=_params(3, 48),
    )(a, b)


def _grad_w_pool(d, dyp):
    seq = d.shape[0]
    ts = min(seq, 1024)

    def body(a_ref, b_ref, o_ref):
        @pl.when(pl.program_id(1) == 0)
        def _():
            o_ref[...] = jnp.zeros_like(o_ref)

        o_ref[...] += lax.dot_general(a_ref[...], b_ref[...], TN, preferred_element_type=F32)

    tile = pl.BlockSpec((ts, POOL_GROUP), lambda g, s: (s, g))
    return pl.pallas_call(
        body, name="grad_w_pool", grid=(len(POOL_WINDOWS), seq // ts),
        in_specs=[tile, tile],
        out_specs=pl.BlockSpec((None, POOL_GROUP, POOL_GROUP), lambda g, s: (g, 0, 0)),
        out_shape=_sds((len(POOL_WINDOWS), POOL_GROUP, POOL_GROUP), F32),
        compiler_params=_params(2, 32),
    )(d, dyp)


def _position():
    return lax.axis_index("x"), lax.axis_index("y"), lax.axis_index("c")


def _gather_copies(x_ref, slot, send_sems, recv_sems, local_sem):
    x, y, c = _position()
    me, sibling = (x, y, c), (x, y, 1 - c)
    chips = [(1 - x, y), (x, 1 - y), (1 - x, 1 - y)]

    def copy(k, block, to, src=None):
        return pltpu.make_async_remote_copy(
            src_ref=slot(*block) if src is None else src, dst_ref=slot(*block),
            send_sem=send_sems.at[k], recv_sem=recv_sems.at[k], device_id=to, device_id_type=MESH)

    mine = pltpu.make_async_copy(x_ref, slot(*me), local_sem)
    mine.start()
    first = [copy(0, me, sibling, src=x_ref)]
    first += [copy(1 + j, me, (*chip, c), src=x_ref) for j, chip in enumerate(chips)]
    for cp in first:
        cp.start()
    passed = [copy(4 + j, (*chip, c), sibling) for j, chip in enumerate(chips)]
    for j, chip in enumerate(chips):
        copy(1 + j, (*chip, c), me).wait_recv()
        passed[j].start()
    copy(0, sibling, me).wait_recv()
    for j, chip in enumerate(chips):
        copy(4 + j, (*chip, 1 - c), me).wait_recv()
    for cp in first + passed:
        cp.wait_send()
    mine.wait()


def _all_gather_hbm(block):
    def body(x_ref, out_ref, send_sems, recv_sems, local_sem):
        _gather_copies(x_ref, lambda px, py, pc: out_ref.at[4 * px + 2 * py + pc], send_sems, recv_sems, local_sem)

    return pl.pallas_call(
        body, name="gather_weights",
        in_specs=[ANY], out_specs=ANY,
        out_shape=_sds((N_DEV,) + block.shape, block.dtype),
        scratch_shapes=[pltpu.SemaphoreType.DMA((7,)), pltpu.SemaphoreType.DMA((7,)), pltpu.SemaphoreType.DMA],
    )(block)


def _all_reduce_small(block):
    def body(x_ref, out_ref, buf, send_sems, recv_sems, local_sem):
        _gather_copies(x_ref, lambda px, py, pc: buf.at[4 * px + 2 * py + pc], send_sems, recv_sems, local_sem)
        acc = buf[0]
        for k in range(1, N_DEV):
            acc = acc + buf[k]
        out_ref[...] = acc

    vmem = pl.BlockSpec(memory_space=pltpu.VMEM)
    return pl.pallas_call(
        body, name="all_reduce_small",
        in_specs=[vmem], out_specs=vmem,
        out_shape=_sds(block.shape, F32),
        scratch_shapes=[pltpu.VMEM((N_DEV,) + block.shape, F32), pltpu.SemaphoreType.DMA((7,)),
                        pltpu.SemaphoreType.DMA((7,)), pltpu.SemaphoreType.DMA],
        compiler_params=pltpu.CompilerParams(vmem_limit_bytes=32 * MIB),
    )(block)


def _exchange_pair(g):
    def body(g_ref, out_ref, send_sems, recv_sems):
        x, y, c = _position()
        copies = []
        for chip in range(4):
            cp = pltpu.make_async_remote_copy(
                src_ref=g_ref.at[2 * chip + (1 - c)], dst_ref=out_ref.at[chip],
                send_sem=send_sems.at[chip], recv_sem=recv_sems.at[chip], device_id=(x, y, 1 - c), device_id_type=MESH)
            cp.start()
            copies.append(cp)
        for cp in copies:
            cp.wait_recv()
        for cp in copies:
            cp.wait_send()

    return pl.pallas_call(
        body, name="exchange_pair",
        in_specs=[ANY], out_specs=ANY,
        out_shape=_sds((4,) + g.shape[1:], g.dtype),
        scratch_shapes=[pltpu.SemaphoreType.DMA((4,)), pltpu.SemaphoreType.DMA((4,))],
    )(g)


def _exchange_chips(part):
    def body(p_ref, out_ref, send_sems, recv_sems):
        x, y, c = _position()
        chips = [(1 - x, y), (x, 1 - y), (1 - x, 1 - y)]
        copies = []
        for k, (px, py) in enumerate(chips):
            cp = pltpu.make_async_remote_copy(
                src_ref=p_ref.at[2 * px + py], dst_ref=out_ref.at[k],
                send_sem=send_sems.at[k], recv_sem=recv_sems.at[k], device_id=(px, py, c), device_id_type=MESH)
            cp.start()
            copies.append(cp)
        for cp in copies:
            cp.wait_recv()
        for cp in copies:
            cp.wait_send()

    return pl.pallas_call(
        body, name="exchange_chips",
        in_specs=[ANY], out_specs=ANY,
        out_shape=_sds((3,) + part.shape[1:], part.dtype),
        scratch_shapes=[pltpu.SemaphoreType.DMA((3,)), pltpu.SemaphoreType.DMA((3,))],
    )(part)


def _pair_sum(g, recv, core):
    rows = g.shape[1]
    tr = PACK_ROW_TILE
    g4 = g.reshape(4, 2, rows, LANES)

    def body(s_ref, g_ref, r_ref, o_ref):
        o_ref[...] = g_ref[...] + r_ref[...]

    spec = pltpu.PrefetchScalarGridSpec(
        num_scalar_prefetch=1, grid=(4, rows // tr),
        in_specs=[pl.BlockSpec((None, None, tr, LANES), lambda j, i, s: (j, s[0], i, 0)),
                  pl.BlockSpec((None, tr, LANES), lambda j, i, s: (j, i, 0))],
        out_specs=pl.BlockSpec((None, tr, LANES), lambda j, i, s: (j, i, 0)))
    return pl.pallas_call(
        body, name="pair_sum", grid_spec=spec, out_shape=_sds((4, rows, LANES), F32),
        compiler_params=_params(2, 32),
    )(core, g4, recv)


def _adamw_math(g, w, m, v):
    m = ADAM_B1 * m + (1.0 - ADAM_B1) * g
    v = ADAM_B2 * v + (1.0 - ADAM_B2) * (g * g)
    m_hat = m / (1.0 - ADAM_B1 ** ADAM_STEP)
    v_hat = v / (1.0 - ADAM_B2 ** ADAM_STEP)
    delta = -ADAM_LR * (m_hat / (jnp.sqrt(v_hat) + ADAM_EPS) + ADAM_WD * w)
    return delta, m, v


def _adamw_sharded(part, recv, chip, w, m, v):
    rows = w.shape[0]
    tr = PACK_ROW_TILE

    def body(s_ref, p_ref, r0_ref, r1_ref, r2_ref, w_ref, m_ref, v_ref, g_ref, d_ref, nm_ref, nv_ref):
        g = ((p_ref[...] + r0_ref[...]) + r1_ref[...]) + r2_ref[...]
        g_ref[...] = g
        d_ref[...], nm_ref[...], nv_ref[...] = _adamw_math(g, w_ref[...], m_ref[...], v_ref[...])

    tile = pl.BlockSpec((tr, LANES), lambda i, s: (i, 0))

    def slot(k):
        return pl.BlockSpec((None, tr, LANES), lambda i, s: (k, i, 0))

    spec = pltpu.PrefetchScalarGridSpec(
        num_scalar_prefetch=1, grid=(rows // tr,),
        in_specs=[pl.BlockSpec((None, tr, LANES), lambda i, s: (s[0], i, 0)), slot(0), slot(1), slot(2), tile, tile, tile],
        out_specs=[tile] * 4)
    return pl.pallas_call(
        body, name="adamw_sharded", grid_spec=spec, out_shape=[_sds((rows, LANES), F32)] * 4,
        compiler_params=_params(1, 48),
    )(chip, part, recv, recv, recv, w, m, v)


def _adamw_small(g, w, m, v):
    def body(g_ref, w_ref, m_ref, v_ref, d_ref, nm_ref, nv_ref):
        d_ref[...], nm_ref[...], nv_ref[...] = _adamw_math(g_ref[...], w_ref[...], m_ref[...], v_ref[...])

    return pl.pallas_call(body, name="adamw_small", out_shape=[_sds(g.shape, F32)] * 3)(g, w, m, v)


def _pack_rows(parts):
    flat = jnp.concatenate([a.reshape(-1, LANES) for a in parts], axis=0)
    pad = (-flat.shape[0]) % PACK_ROW_TILE
    return jnp.pad(flat, ((0, pad), (0, 0)))


def _pack_small(parts):
    flat = jnp.concatenate([a.reshape(-1) for a in parts])
    pad = (-flat.shape[0]) % (8 * LANES)
    return jnp.pad(flat, (0, pad)).reshape(-1, LANES)


def _unpack_small(packed, shapes):
    flat = packed.reshape(-1)
    out, off = [], 0
    for shape in shapes:
        size = 1
        for n in shape:
            size *= n
        out.append(flat[off:off + size].reshape(shape))
        off += size
    return out


def _full_from_gathered(gathered, shard_shapes):
    out, off = {}, 0
    for (name, kind), (k, n) in zip(SHARDED, shard_shapes):
        rows = k * n // LANES
        seg = gathered[:, off:off + rows].reshape(N_DEV, k, n)
        out[name] = jnp.transpose(seg, (1, 0, 2)).reshape(k, N_DEV * n) if kind == "col" else seg.reshape(N_DEV * k, n)
        off += rows
    return out


def _shards_from_full(full, shard_shapes):
    parts = []
    for (name, kind), (k, n) in zip(SHARDED, shard_shapes):
        a = full[name]
        seg = jnp.transpose(a.reshape(k, N_DEV, n), (1, 0, 2)) if kind == "col" else a.reshape(N_DEV, k, n)
        parts.append(seg.reshape(N_DEV, k * n // LANES, LANES))
    flat = jnp.concatenate(parts, axis=1)
    pad = (-flat.shape[1]) % PACK_ROW_TILE
    return jnp.pad(flat, ((0, 0), (0, pad), (0, 0)))


def _unpack_rows(packed, shard_shapes):
    out, off = {}, 0
    for (name, _), (k, n) in zip(SHARDED, shard_shapes):
        rows = k * n // LANES
        out[name] = packed[off:off + rows].reshape(1, k, n)
        off += rows
    return out


def _rope_tables(positions):
    inv_freq = ROPE_THETA ** (-jnp.arange(0, QK_ROPE, 2, dtype=F32) / QK_ROPE)
    ang = positions.astype(F32)[:, None] * inv_freq
    cos, sin = jnp.cos(ang), jnp.sin(ang)
    seq = positions.shape[0]
    zeros = lambda n: jnp.zeros((seq, n), F32)
    cf = jnp.concatenate([jnp.ones((seq, QK_NOPE), F32), cos, cos, zeros(HEAD_PAD - QK_NOPE - QK_ROPE)], axis=1)
    s1 = jnp.concatenate([zeros(QK_NOPE), -sin, zeros(HEAD_PAD - QK_NOPE - QK_ROPE // 2)], axis=1)
    s2 = jnp.concatenate([zeros(QK_NOPE + QK_ROPE // 2), sin, zeros(HEAD_PAD - QK_NOPE - QK_ROPE)], axis=1)
    return cf, s1, s2


def _arrange_w_in(w):
    k = w.shape[0]
    zeros = lambda n: jnp.zeros((k, n), w.dtype)
    kr0 = Q_LORA + KV_LORA
    pool0 = kr0 + QK_ROPE
    return jnp.concatenate([w[:, :kr0], w[:, pool0:], zeros(QK_NOPE), w[:, kr0:pool0],
                            zeros(HEAD_PAD - QK_NOPE - QK_ROPE)], axis=1)


def _restore_w_in(d):
    kr = d[:, IN_KR0 + QK_NOPE:IN_KR0 + QK_NOPE + QK_ROPE]
    return jnp.concatenate([d[:, :IN_POOL0], kr, d[:, IN_POOL0:IN_KR0]], axis=1)


def _pad_heads(w, width):
    k = w.shape[0]
    w = w.reshape(k, N_HEADS, width)
    return jnp.pad(w, ((0, 0), (0, 0), (0, HEAD_PAD - width))).reshape(k, N_HEADS * HEAD_PAD)


def _unpad_heads(d, width):
    k = d.shape[0]
    return d.reshape(k, N_HEADS, HEAD_PAD)[:, :, :width]


def kernel(x, p, positions, g_pre_mix, w_in, b_gate, g_q, w_uq, g_kv, w_ukv, w_pool, pool_scale, w_branch_attn, w_branch_pool, w_out, g_post_mix, g_pre_mlp, w_ff1, w_ff2, g_post_mlp, w_ple_proj, w_ple_gate, g_ple, loss_target, m_g_pre_mix, m_w_in, m_b_gate, m_g_q, m_w_uq, m_g_kv, m_w_ukv, m_w_pool, m_pool_scale, m_w_branch_attn, m_w_branch_pool, m_w_out, m_g_post_mix, m_g_pre_mlp, m_w_ff1, m_w_ff2, m_g_post_mlp, m_w_ple_proj, m_w_ple_gate, m_g_ple, v_g_pre_mix, v_w_in, v_b_gate, v_g_q, v_w_uq, v_g_kv, v_w_ukv, v_w_pool, v_pool_scale, v_w_branch_attn, v_w_branch_pool, v_w_out, v_g_post_mix, v_g_pre_mlp, v_w_ff1, v_w_ff2, v_g_post_mlp, v_w_ple_proj, v_w_ple_gate, v_g_ple):
    given = dict(locals())
    weights = {n: given[n] for n in WEIGHT_ORDER}
    moments_m = {n: given["m_" + n] for n in WEIGHT_ORDER}
    moments_v = {n: given["v_" + n] for n in WEIGHT_ORDER}
    xs, ps, target = x[0], p[0, 0], loss_target[0]
    seq = xs.shape[0]
    tm = min(256, seq)
    tq = min(512, seq)
    core = lax.axis_index("c")
    chip = 2 * lax.axis_index("x") + lax.axis_index("y")

    shard_shapes = [weights[n].shape[1:] for n, _ in SHARDED]
    gathered = _all_gather_hbm(_pack_rows([weights[n][0] for n, _ in SHARDED]).astype(BF16))
    full = _full_from_gathered(gathered, shard_shapes)
    w_in_r = _arrange_w_in(full["w_in"])
    w_uq_r = _pad_heads(full["w_uq"], QK_NOPE + QK_ROPE)
    ukv = full["w_ukv"].reshape(KV_LORA, N_HEADS, QK_NOPE + V_HEAD)
    w_k_exp = _pad_heads(ukv[:, :, :QK_NOPE].reshape(KV_LORA, N_HEADS * QK_NOPE), QK_NOPE)
    w_v = ukv[:, :, QK_NOPE:].reshape(KV_LORA, N_HEADS * V_HEAD)
    w_pool_bf = w_pool[0].astype(BF16)
    cf, s1, s2 = _rope_tables(positions[0])

    a_bf, qd, kvd, pin, gates, q, k, v = _proj_fwd(xs, g_pre_mix, b_gate, g_q, g_kv, cf, s1, s2,
                                                   w_in_r, w_uq_r, w_k_exp, w_v, tm)
    d_pool, pooled = _pool_fwd(pin, w_pool_bf, pool_scale, tm)
    v_heads = jnp.transpose(v.reshape(seq, N_HEADS, V_HEAD), (1, 0, 2))
    o_heads, lse = _attn_fwd(q, k, v_heads, tq)
    attn = jnp.transpose(o_heads, (1, 0, 2)).reshape(seq, N_HEADS * V_HEAD)
    merged, ba, bb, y, h1 = _merge_fwd(attn, pooled, gates, xs, g_post_mix, full["w_branch_attn"],
                                       full["w_branch_pool"], full["w_out"], tm)
    (m_bf, zr, f, h2_bf, p_bf, de, dpre, dh2, loss_acc, dg_ple) = _tail_fwd(
        h1, target, ps, g_pre_mlp, g_post_mlp, g_ple, full["w_ff1"], full["w_ff2"], full["w_ple_proj"],
        full["w_ple_gate"], tm)

    df, dz, dh1, dg_pre_mlp, dg_post_mlp = _mlp_bwd(h1, f, zr, dh2, g_pre_mlp, g_post_mlp, full["w_ff1"],
                                                    full["w_ff2"], tm)
    (dy, dba, dbb, dgpre, dattn, dyp, dd, dg_post_mix, db_gate, dpool_scale) = _merge_bwd(
        dh1, y, gates, ba, bb, d_pool, g_post_mix, pool_scale, full["w_out"], full["w_branch_attn"],
        full["w_branch_pool"], w_pool_bf, tm)
    dpin = _pool_bwd_window(dd, tm)
    do_heads = jnp.transpose(dattn.reshape(seq, N_HEADS, V_HEAD), (1, 0, 2))
    delta = _attn_delta(o_heads, do_heads, tq)
    dq, dk, dv_heads = _attn_bwd(q, k, v_heads, do_heads, lse, delta, tq)
    dv = jnp.transpose(dv_heads, (1, 0, 2)).reshape(seq, N_HEADS * V_HEAD)
    (grad_x, dproj, dq_bf, qn_bf, kvn_bf, dk_bf, dg_pre_mix, dg_q, dg_kv) = _proj_bwd(
        dq, dk, dv, qd, kvd, xs, dh1, dgpre, dpin, cf, s1, s2, g_pre_mix, g_q, g_kv, w_in_r, w_uq_r, w_k_exp, w_v, tm)

    d_k_exp = _unpad_heads(_grad_w(kvn_bf, dk_bf, "grad_w_uk"), QK_NOPE)
    d_w_v = _grad_w(kvn_bf, dv, "grad_w_uv").reshape(KV_LORA, N_HEADS, V_HEAD)
    grads_full = {
        "w_in": _restore_w_in(_grad_w(a_bf, dproj, "grad_w_in")),
        "w_uq": _unpad_heads(_grad_w(qn_bf, dq_bf, "grad_w_uq"), QK_NOPE + QK_ROPE).reshape(Q_LORA, -1),
        "w_ukv": jnp.concatenate([d_k_exp, d_w_v], axis=2).reshape(KV_LORA, -1),
        "w_branch_attn": _grad_w(attn, dba, "grad_w_branch_attn"),
        "w_branch_pool": _grad_w(pooled, dbb, "grad_w_branch_pool"),
        "w_out": _grad_w(merged, dy, "grad_w_out"),
        "w_ff1": _grad_w(m_bf, dz, "grad_w_ff1"),
        "w_ff2": _grad_w(zr, df, "grad_w_ff2", square_a=True),
        "w_ple_proj": _grad_w(p_bf, de, "grad_w_ple_proj"),
        "w_ple_gate": _grad_w(h2_bf, dpre, "grad_w_ple_gate"),
    }
    grads_small = {
        "g_pre_mix": dg_pre_mix, "b_gate": db_gate, "g_q": dg_q, "g_kv": dg_kv,
        "w_pool": _grad_w_pool(d_pool, dyp), "pool_scale": dpool_scale, "g_post_mix": dg_post_mix,
        "g_pre_mlp": dg_pre_mlp, "g_post_mlp": dg_post_mlp, "g_ple": dg_ple,
    }

    by_device = _shards_from_full(grads_full, shard_shapes)
    from_sibling = _exchange_pair(by_device)
    pair = _pair_sum(by_device, from_sibling, jnp.reshape(core, (1,)).astype(jnp.int32))
    from_chips = _exchange_chips(pair)
    g_sh, d_sh, m_sh, v_sh = _adamw_sharded(
        pair, from_chips, jnp.reshape(chip, (1,)).astype(jnp.int32),
        _pack_rows([weights[n][0] for n, _ in SHARDED]),
        _pack_rows([moments_m[n][0] for n, _ in SHARDED]),
        _pack_rows([moments_v[n][0] for n, _ in SHARDED]))

    g_sm = _all_reduce_small(_pack_small([grads_small[n] for n in REPLICATED]))
    d_sm, m_sm, v_sm = _adamw_small(g_sm, _pack_small([weights[n] for n in REPLICATED]),
                                    _pack_small([moments_m[n] for n in REPLICATED]),
                                    _pack_small([moments_v[n] for n in REPLICATED]))

    small_shapes = [weights[n].shape for n in REPLICATED]
    results = []
    for sharded, small in ((g_sh, g_sm), (d_sh, d_sm), (m_sh, m_sm), (v_sh, v_sm)):
        named = _unpack_rows(sharded, shard_shapes)
        named.update(zip(REPLICATED, _unpack_small(small, small_shapes)))
        results.append([named[n] for n in WEIGHT_ORDER])

    loss = lax.psum(loss_acc[0, 0], ("x", "y", "c"))
    return (loss, grad_x[None], *results[0], *results[1], *results[2], *results[3])
```

```python
import jax
import jax.numpy as jnp
from jax import lax
from jax.experimental import pallas as pl
from jax.experimental.pallas import tpu as pltpu

F32 = jnp.float32
BF16 = jnp.bfloat16

D_MODEL = 1024
PLE_DIM = 256
N_HEADS = 8
QK_NOPE = 64
QK_ROPE = 32
V_HEAD = 64
Q_LORA = 384
KV_LORA = 256
POOL_WINDOWS = (2, 4, 8, 16)
POOL_GROUP = 128
POOL_WIDTH = 512
D_FF = 4096
ROPE_THETA = 10000.0
EPS = 1e-6
HEAD_PAD = 128
ATTN_SCALE = (QK_NOPE + QK_ROPE) ** -0.5
LOG2E = 1.4426950408889634
Q_PRESCALE = ATTN_SCALE * LOG2E
ATTN_TILE = 512

ADAM_LR = 0.001
ADAM_B1 = 0.9
ADAM_B2 = 0.999
ADAM_EPS = 1e-08
ADAM_WD = 0.01
ADAM_STEP = 10

N_DEV = 8
LANES = 1024
PACK_ROW_TILE = 480
POOL_HALO = 16
MIB = 2 ** 20

IN_Q0, IN_KV0, IN_POOL0, IN_GATE0, IN_KR0, IN_R = 0, 384, 640, 1152, 3200, 3328

SHARDED = (("w_in", "col"), ("w_uq", "col"), ("w_ukv", "col"), ("w_branch_attn", "col"),
           ("w_branch_pool", "col"), ("w_out", "row"), ("w_ff1", "col"), ("w_ff2", "row"),
           ("w_ple_proj", "col"), ("w_ple_gate", "row"))
REPLICATED = ("g_pre_mix", "b_gate", "g_q", "g_kv", "w_pool", "pool_scale", "g_post_mix",
              "g_pre_mlp", "g_post_mlp", "g_ple")
WEIGHT_ORDER = ("g_pre_mix", "w_in", "b_gate", "g_q", "w_uq", "g_kv", "w_ukv", "w_pool", "pool_scale",
                "w_branch_attn", "w_branch_pool", "w_out", "g_post_mix", "g_pre_mlp", "w_ff1", "w_ff2",
                "g_post_mlp", "w_ple_proj", "w_ple_gate", "g_ple")

NT = (((1,), (1,)), ((), ()))
TN = (((0,), (0,)), ((), ()))
MESH = pl.DeviceIdType.MESH
ANY = pl.BlockSpec(memory_space=pl.ANY)


def _params(n_axes, vmem_mib):
    return pltpu.CompilerParams(dimension_semantics=("arbitrary",) * n_axes, vmem_limit_bytes=vmem_mib * MIB)


def _row(tm, n):
    return pl.BlockSpec((tm, n), lambda i: (i, 0))


def _fix(shape):
    zeros = (0,) * len(shape)
    return pl.BlockSpec(shape, lambda i: zeros)


def _sds(shape, dtype):
    return jax.ShapeDtypeStruct(shape, dtype)


def _rms_r(v):
    return lax.rsqrt(jnp.mean(v * v, axis=-1, keepdims=True) + EPS)


def _rms_bwd(vhat, r, g, dy):
    gdy = dy * g
    return r * (gdy - vhat * jnp.mean(gdy * vhat, axis=-1, keepdims=True))


def _colsum(v):
    return jnp.sum(v, axis=0, keepdims=True)


def _sigmoid(v):
    return 1.0 / (1.0 + jnp.exp(-v))


def _mm(a, b):
    return jnp.dot(a, b, preferred_element_type=F32)


def _mm_nt(a, b):
    return lax.dot_general(a, b, NT, preferred_element_type=F32)


def _rope(c, cf, s1, s2):
    return c * cf + pltpu.roll(c, HEAD_PAD - 16, 1) * s1 + pltpu.roll(c, 16, 1) * s2


def _rope_t(c, cf, s1, s2):
    return c * cf + pltpu.roll(c * s1, 16, 1) + pltpu.roll(c * s2, HEAD_PAD - 16, 1)


def _load_once(pairs):
    @pl.when(pl.program_id(0) == 0)
    def _():
        for src, dst in pairs:
            pltpu.sync_copy(src, dst)


def _proj_fwd(x, g_pre, b_gate, g_q, g_kv, cf, s1, s2, w_in_r, w_uq_r, w_k_exp, w_v, tm):
    seq = x.shape[0]

    def body(x_ref, gpre_ref, bg_ref, gq_ref, gkv_ref, cf_ref, s1_ref, s2_ref, win_hbm, wuq_hbm, wk_hbm, wv_hbm,
             a_ref, qd_ref, kvd_ref, pin_ref, gates_ref, q_ref, k_ref, v_ref, win, wuq, wk, wv):
        _load_once(((win_hbm, win), (wuq_hbm, wuq), (wk_hbm, wk), (wv_hbm, wv)))
        xv = x_ref[...]
        a = (xv * _rms_r(xv) * gpre_ref[...]).astype(BF16)
        a_ref[...] = a
        proj = _mm(a, win[...])
        qd = proj[:, IN_Q0:IN_KV0]
        kvd = proj[:, IN_KV0:IN_POOL0]
        qd_ref[...] = qd
        kvd_ref[...] = kvd
        pin_ref[...] = proj[:, IN_POOL0:IN_GATE0]
        gates_ref[...] = _sigmoid(proj[:, IN_GATE0:IN_KR0] + bg_ref[...]).astype(BF16)
        cfv, s1v, s2v = cf_ref[...], s1_ref[...], s2_ref[...]
        krr = _rope(proj[:, IN_KR0:IN_R], cfv, s1v, s2v)
        qn = (qd * _rms_r(qd) * gq_ref[...]).astype(BF16)
        qf = _mm(qn, wuq[...])
        kvn = (kvd * _rms_r(kvd) * gkv_ref[...]).astype(BF16)
        kf = _mm(kvn, wk[...])
        vf = _mm(kvn, wv[...])
        one_lane = (lax.broadcasted_iota(jnp.int32, (tm, HEAD_PAD), 1) == V_HEAD).astype(F32)
        for h in range(N_HEADS):
            lanes = slice(HEAD_PAD * h, HEAD_PAD * (h + 1))
            q_ref[h] = (_rope(qf[:, lanes], cfv, s1v, s2v) * Q_PRESCALE).astype(BF16)
            k_ref[h] = (kf[:, lanes] + krr).astype(BF16)
            v_ref[h] = (vf[:, lanes] + one_lane).astype(BF16)

    heads = pl.BlockSpec((N_HEADS, tm, HEAD_PAD), lambda i: (0, i, 0))
    return pl.pallas_call(
        body, name="proj_fwd", grid=(seq // tm,),
        in_specs=[_row(tm, D_MODEL), _fix((1, D_MODEL)), _fix((1, 2 * D_MODEL)), _fix((1, Q_LORA)), _fix((1, KV_LORA)),
                  _row(tm, HEAD_PAD), _row(tm, HEAD_PAD), _row(tm, HEAD_PAD), ANY, ANY, ANY, ANY],
        out_specs=[_row(tm, D_MODEL), _row(tm, Q_LORA), _row(tm, KV_LORA), _row(tm, POOL_WIDTH), _row(tm, 2 * D_MODEL),
                   heads, heads, heads],
        out_shape=[_sds((seq, D_MODEL), BF16), _sds((seq, Q_LORA), F32), _sds((seq, KV_LORA), F32),
                   _sds((seq, POOL_WIDTH), F32), _sds((seq, 2 * D_MODEL), BF16),
                   _sds((N_HEADS, seq, HEAD_PAD), BF16), _sds((N_HEADS, seq, HEAD_PAD), BF16),
                   _sds((N_HEADS, seq, HEAD_PAD), BF16)],
        scratch_shapes=[pltpu.VMEM(w_in_r.shape, BF16), pltpu.VMEM(w_uq_r.shape, BF16),
                        pltpu.VMEM(w_k_exp.shape, BF16), pltpu.VMEM(w_v.shape, BF16)],
        compiler_params=_params(1, 48),
    )(x, g_pre, b_gate, g_q, g_kv, cf, s1, s2, w_in_r, w_uq_r, w_k_exp, w_v)


def _window_count(row0, n_rows):
    t = row0 + lax.broadcasted_iota(jnp.int32, (n_rows, POOL_GROUP), 0)
    return [jnp.minimum(t + 1, w).astype(F32) for w in POOL_WINDOWS]


def _pool_fwd(pin, w_pool_bf, pool_scale, tm):
    seq = pin.shape[0]
    ext_rows = tm + POOL_HALO

    def body(prev_ref, u_ref, wp_ref, ps_ref, d_ref, pooled_ref):
        i = pl.program_id(0)
        prev = jnp.where(i == 0, 0.0, prev_ref[...])
        u = u_ref[...]
        level = jnp.concatenate([prev, u], axis=0)
        counts = _window_count(i * tm, tm)
        shift = 1
        for g in range(len(POOL_WINDOWS)):
            level = level + pltpu.roll(level, shift, 0)
            shift *= 2
            lanes = slice(POOL_GROUP * g, POOL_GROUP * (g + 1))
            d = (level[POOL_HALO:, lanes] / counts[g] - u[:, lanes]).astype(BF16)
            d_ref[:, lanes] = d
            pooled_ref[:, lanes] = (_mm(d, wp_ref[g]) * ps_ref[:, lanes]).astype(BF16)

    halo = tm // POOL_HALO
    return pl.pallas_call(
        body, name="pool_fwd", grid=(seq // tm,),
        in_specs=[pl.BlockSpec((POOL_HALO, POOL_WIDTH), lambda i: (jnp.maximum(i * halo - 1, 0), 0)),
                  _row(tm, POOL_WIDTH), _fix(w_pool_bf.shape), _fix((1, POOL_WIDTH))],
        out_specs=[_row(tm, POOL_WIDTH), _row(tm, POOL_WIDTH)],
        out_shape=[_sds((seq, POOL_WIDTH), BF16), _sds((seq, POOL_WIDTH), BF16)],
        compiler_params=_params(1, 32),
    )(pin, pin, w_pool_bf, pool_scale)


def _pool_bwd_window(dd, tm):
    seq = dd.shape[0]
    n_tiles = seq // tm
    ext_rows = tm + POOL_HALO

    def body(dd_ref, next_ref, dpin_ref):
        i = pl.program_id(0)
        nxt = jnp.where(i == n_tiles - 1, 0.0, next_ref[...])
        dd_t = dd_ref[...]
        ext = jnp.concatenate([dd_t, nxt], axis=0)
        counts = _window_count(i * tm, ext_rows)
        shift = 1
        for g in range(len(POOL_WINDOWS)):
            lanes = slice(POOL_GROUP * g, POOL_GROUP * (g + 1))
            level = ext[:, lanes] / counts[g]
            s = 1
            while s <= shift:
                level = level + pltpu.roll(level, ext_rows - s, 0)
                s *= 2
            shift *= 2
            dpin_ref[:, lanes] = (level[:tm] - dd_t[:, lanes]).astype(BF16)

    halo = tm // POOL_HALO
    return pl.pallas_call(
        body, name="pool_bwd_window", grid=(n_tiles,),
        in_specs=[_row(tm, POOL_WIDTH),
                  pl.BlockSpec((POOL_HALO, POOL_WIDTH), lambda i: (jnp.minimum((i + 1) * halo, seq // POOL_HALO - 1), 0))],
        out_specs=_row(tm, POOL_WIDTH),
        out_shape=_sds((seq, POOL_WIDTH), BF16),
        compiler_params=_params(1, 32),
    )(dd, dd)


def _col_to_row(col, n):
    return jnp.transpose(jnp.broadcast_to(col, (n, HEAD_PAD)))[0:1, :]


def _attn_fwd(q, k, v):
    heads, seq, _ = q.shape
    t = ATTN_TILE
    n_pairs = seq // (2 * t)

    def body(q_ref, k_ref, v_ref, o_ref, lse_ref):
        qi = pl.program_id(1)
        q_a = q_ref[0:t, :]
        q_b = q_ref[t:2 * t, :]

        def tile(qt, j, m, acc, diagonal):
            start = pl.multiple_of(j * t, t)
            s = _mm_nt(qt, k_ref[pl.ds(start, t), :])
            if diagonal:
                row = lax.broadcasted_iota(jnp.int32, (t, t), 0)
                col = lax.broadcasted_iota(jnp.int32, (t, t), 1)
                s = jnp.where(col <= row, s, -jnp.inf)
            m_new = jnp.maximum(m, jnp.max(s, axis=1, keepdims=True))
            p = jnp.exp2(s - m_new).astype(BF16)
            acc = jnp.exp2(m - m_new) * acc + _mm(p, v_ref[pl.ds(start, t), :])
            return m_new, acc

        def both(j, carry):
            m_a, acc_a, m_b, acc_b = carry
            m_a, acc_a = tile(q_a, j, m_a, acc_a, False)
            m_b, acc_b = tile(q_b, j, m_b, acc_b, False)
            return m_a, acc_a, m_b, acc_b

        m0 = jnp.full((t, 1), -jnp.inf, F32)
        acc0 = jnp.zeros((t, HEAD_PAD), F32)
        m_a, acc_a, m_b, acc_b = lax.fori_loop(0, 2 * qi, both, (m0, acc0, m0, acc0))
        m_a, acc_a = tile(q_a, 2 * qi, m_a, acc_a, True)
        m_b, acc_b = tile(q_b, 2 * qi, m_b, acc_b, False)
        m_b, acc_b = tile(q_b, 2 * qi + 1, m_b, acc_b, True)
        for half, (m, acc) in enumerate(((m_a, acc_a), (m_b, acc_b))):
            l = acc[:, V_HEAD:V_HEAD + 1]
            o_ref[half * t:(half + 1) * t, :] = (acc / l).astype(BF16)
            lse_ref[half] = _col_to_row(m + jnp.log2(l), t)

    return pl.pallas_call(
        body, name="attn_fwd", grid=(heads, n_pairs),
        in_specs=[pl.BlockSpec((None, 2 * t, HEAD_PAD), lambda h, i: (h, i, 0)),
                  pl.BlockSpec((None, seq, HEAD_PAD), lambda h, i: (h, 0, 0)),
                  pl.BlockSpec((None, seq, HEAD_PAD), lambda h, i: (h, 0, 0))],
        out_specs=[pl.BlockSpec((None, 2 * t, HEAD_PAD), lambda h, i: (h, i, 0)),
                   pl.BlockSpec((None, 2, 1, t), lambda h, i: (h, i, 0, 0))],
        out_shape=[_sds((heads, seq, HEAD_PAD), BF16), _sds((heads, seq // t, 1, t), F32)],
        compiler_params=_params(2, 48),
    )(q, k, v)


def _attn_delta(o, do):
    heads, seq, _ = o.shape
    tq = ATTN_TILE
    nq = seq // tq

    def body(o_ref, do_ref, delta_ref):
        prod = o_ref[...].astype(F32) * do_ref[...].astype(F32)
        delta_ref[...] = _col_to_row(jnp.sum(prod, axis=1, keepdims=True), tq)

    tile = pl.BlockSpec((None, tq, HEAD_PAD), lambda h, i: (h, i, 0))
    return pl.pallas_call(
        body, name="attn_delta", grid=(heads, nq),
        in_specs=[tile, tile],
        out_specs=pl.BlockSpec((None, None, 1, tq), lambda h, i: (h, i, 0, 0)),
        out_shape=_sds((heads, nq, 1, tq), F32),
        compiler_params=_params(2, 32),
    )(o, do)


def _attn_bwd(q, k, v, do, lse, delta):
    heads, seq, _ = q.shape
    t = ATTN_TILE
    nq = seq // t

    def body(q_ref, k_ref, v_ref, do_ref, lse_ref, delta_ref, dq_ref, dk_ref, dv_ref):
        jp = pl.program_id(1)

        @pl.when(jp == 0)
        def _():
            dq_ref[...] = jnp.zeros_like(dq_ref)

        k_a, k_b = k_ref[0:t, :], k_ref[t:2 * t, :]
        v_a, v_b = v_ref[0:t, :], v_ref[t:2 * t, :]

        def tile(kt, vt, i, dk, dv, diagonal):
            start = pl.multiple_of(i * t, t)
            qt = q_ref[pl.ds(start, t), :]
            dot = do_ref[pl.ds(start, t), :]
            p_t = jnp.exp2(_mm_nt(kt, qt) - lse_ref[i])
            if diagonal:
                key = lax.broadcasted_iota(jnp.int32, (t, t), 0)
                query = lax.broadcasted_iota(jnp.int32, (t, t), 1)
                p_t = jnp.where(key <= query, p_t, 0.0)
            dv = dv + _mm(p_t.astype(BF16), dot)
            ds_t = (p_t * (_mm_nt(vt, dot) - delta_ref[i])).astype(BF16)
            dk = dk + _mm(ds_t, qt)
            return dk, dv, lax.dot_general(ds_t, kt, TN, preferred_element_type=F32)

        def add_dq(i, dq):
            start = pl.multiple_of(i * t, t)
            dq_ref[pl.ds(start, t), :] += dq

        def both(i, carry):
            dk_a, dv_a, dk_b, dv_b = carry
            dk_a, dv_a, dq_a = tile(k_a, v_a, i, dk_a, dv_a, False)
            dk_b, dv_b, dq_b = tile(k_b, v_b, i, dk_b, dv_b, False)
            add_dq(i, dq_a + dq_b)
            return dk_a, dv_a, dk_b, dv_b

        zero = jnp.zeros((t, HEAD_PAD), F32)
        dk_a, dv_a, dq_a = tile(k_a, v_a, 2 * jp, zero, zero, True)
        add_dq(2 * jp, dq_a)
        dk_a, dv_a, dq_a = tile(k_a, v_a, 2 * jp + 1, dk_a, dv_a, False)
        dk_b, dv_b, dq_b = tile(k_b, v_b, 2 * jp + 1, zero, zero, True)
        add_dq(2 * jp + 1, dq_a + dq_b)
        dk_a, dv_a, dk_b, dv_b = lax.fori_loop(2 * jp + 2, nq, both, (dk_a, dv_a, dk_b, dv_b))
        dk_ref[0:t, :] = (dk_a * (1.0 / LOG2E)).astype(BF16)
        dk_ref[t:2 * t, :] = (dk_b * (1.0 / LOG2E)).astype(BF16)
        dv_ref[0:t, :] = dv_a.astype(BF16)
        dv_ref[t:2 * t, :] = dv_b.astype(BF16)

    whole = pl.BlockSpec((None, seq, HEAD_PAD), lambda h, j: (h, 0, 0))
    pair = pl.BlockSpec((None, 2 * t, HEAD_PAD), lambda h, j: (h, j, 0))
    stats = pl.BlockSpec((None, nq, 1, t), lambda h, j: (h, 0, 0, 0))
    return pl.pallas_call(
        body, name="attn_bwd", grid=(heads, nq // 2),
        in_specs=[whole, pair, pair, whole, stats, stats],
        out_specs=[whole, pair, pair],
        out_shape=[_sds((heads, seq, HEAD_PAD), F32), _sds((heads, seq, HEAD_PAD), BF16),
                   _sds((heads, seq, HEAD_PAD), BF16)],
        compiler_params=_params(2, 56),
    )(q, k, v, do, lse, delta)


def _merge_fwd(attn, pooled, gates, x, g_post_mix, w_ba, w_bb, w_out, tm):
    seq = x.shape[0]

    def body(attn_ref, pooled_ref, gates_ref, x_ref, g_ref, wba_ref, wbb_ref, wout_ref,
             merged_ref, ba_ref, bb_ref, y_ref, h1_ref):
        attn = jnp.concatenate([attn_ref[h] for h in range(N_HEADS)], axis=1)
        ba = _mm(attn, wba_ref[...])
        bb = _mm(pooled_ref[...], wbb_ref[...])
        ba_ref[...] = ba.astype(BF16)
        bb_ref[...] = bb.astype(BF16)
        merged = (gates_ref[:, :D_MODEL].astype(F32) * ba + gates_ref[:, D_MODEL:].astype(F32) * bb).astype(BF16)
        merged_ref[...] = merged
        y = _mm(merged, wout_ref[...])
        y_ref[...] = y
        h1_ref[...] = x_ref[...] + y * _rms_r(y) * g_ref[...]

    return pl.pallas_call(
        body, name="merge_fwd", grid=(seq // tm,),
        in_specs=[pl.BlockSpec((N_HEADS, tm, HEAD_PAD), lambda i: (0, i, 0)), _row(tm, POOL_WIDTH),
                  _row(tm, 2 * D_MODEL), _row(tm, D_MODEL),
                  _fix((1, D_MODEL)), _fix(w_ba.shape), _fix(w_bb.shape), _fix(w_out.shape)],
        out_specs=[_row(tm, D_MODEL)] * 5,
        out_shape=[_sds((seq, D_MODEL), BF16), _sds((seq, D_MODEL), BF16), _sds((seq, D_MODEL), BF16),
                   _sds((seq, D_MODEL), F32), _sds((seq, D_MODEL), F32)],
        compiler_params=_params(1, 48),
    )(attn, pooled, gates, x, g_post_mix, w_ba, w_bb, w_out)


def _tail_fwd(h1, target, p, g_pre_mlp, g_post_mlp, g_ple, w_ff1, w_ff2, w_pe, w_pg, tm):
    seq = h1.shape[0]

    def body(h1_ref, tgt_ref, p_ref, gm_ref, gf_ref, gp_ref, w1_hbm, w2_hbm, wpe_hbm, wpg_hbm,
             m_ref, zr_ref, f_ref, h2b_ref, pb_ref, de_ref, dpre_ref, dh2_ref, loss_ref, dgple_ref,
             w1, w2, wpe, wpg):
        _load_once(((w1_hbm, w1), (w2_hbm, w2), (wpe_hbm, wpe), (wpg_hbm, wpg)))

        @pl.when(pl.program_id(0) == 0)
        def _():
            loss_ref[...] = jnp.zeros_like(loss_ref)
            dgple_ref[...] = jnp.zeros_like(dgple_ref)

        h1v = h1_ref[...]
        m = (h1v * _rms_r(h1v) * gm_ref[...]).astype(BF16)
        m_ref[...] = m
        zr = jnp.maximum(_mm(m, w1[...]), 0.0)
        zr_ref[...] = zr.astype(BF16)
        f = _mm((zr * zr).astype(BF16), w2[...])
        f_ref[...] = f
        h2 = h1v + f * _rms_r(f) * gf_ref[...]
        h2b = h2.astype(BF16)
        h2b_ref[...] = h2b
        pb = p_ref[...].astype(BF16)
        pb_ref[...] = pb
        e = _mm(pb, wpe[...])
        pg = _sigmoid(_mm(h2b, wpg[...]))
        t3 = pg * e
        r3 = _rms_r(t3)
        t3hat = t3 * r3
        diff = h2 + t3hat * gp_ref[...] - tgt_ref[...]
        loss_ref[...] += jnp.sum(diff * diff) * (0.5 / D_MODEL)
        dh3 = diff * (1.0 / D_MODEL)
        dgple_ref[...] += _colsum(dh3 * t3hat)
        dt3 = _rms_bwd(t3hat, r3, gp_ref[...], dh3)
        de_ref[...] = (dt3 * pg).astype(BF16)
        dpre = (dt3 * e * pg * (1.0 - pg)).astype(BF16)
        dpre_ref[...] = dpre
        dh2_ref[...] = dh3 + _mm_nt(dpre, wpg[...])

    return pl.pallas_call(
        body, name="tail_fwd", grid=(seq // tm,),
        in_specs=[_row(tm, D_MODEL), _row(tm, D_MODEL), _row(tm, PLE_DIM), _fix((1, D_MODEL)), _fix((1, D_MODEL)),
                  _fix((1, D_MODEL)), ANY, ANY, ANY, ANY],
        out_specs=[_row(tm, D_MODEL), _row(tm, D_FF), _row(tm, D_MODEL), _row(tm, D_MODEL), _row(tm, PLE_DIM),
                   _row(tm, D_MODEL), _row(tm, D_MODEL), _row(tm, D_MODEL), _fix((8, 128)), _fix((1, D_MODEL))],
        out_shape=[_sds((seq, D_MODEL), BF16), _sds((seq, D_FF), BF16), _sds((seq, D_MODEL), F32),
                   _sds((seq, D_MODEL), BF16), _sds((seq, PLE_DIM), BF16), _sds((seq, D_MODEL), BF16),
                   _sds((seq, D_MODEL), BF16), _sds((seq, D_MODEL), F32), _sds((8, 128), F32), _sds((1, D_MODEL), F32)],
        scratch_shapes=[pltpu.VMEM(w_ff1.shape, BF16), pltpu.VMEM(w_ff2.shape, BF16),
                        pltpu.VMEM(w_pe.shape, BF16), pltpu.VMEM(w_pg.shape, BF16)],
        compiler_params=_params(1, 56),
    )(h1, target, p, g_pre_mlp, g_post_mlp, g_ple, w_ff1, w_ff2, w_pe, w_pg)


def _mlp_bwd(h1, f, zr, dh2, g_pre_mlp, g_post_mlp, w_ff1, w_ff2, tm):
    seq = h1.shape[0]

    def body(h1_ref, f_ref, zr_ref, dh2_ref, gm_ref, gf_ref, w1_hbm, w2_hbm,
             df_ref, dz_ref, dh1_ref, dgm_ref, dgf_ref, w1, w2):
        _load_once(((w1_hbm, w1), (w2_hbm, w2)))

        @pl.when(pl.program_id(0) == 0)
        def _():
            dgm_ref[...] = jnp.zeros_like(dgm_ref)
            dgf_ref[...] = jnp.zeros_like(dgf_ref)

        dh2 = dh2_ref[...]
        fv = f_ref[...]
        rf = _rms_r(fv)
        fhat = fv * rf
        dgf_ref[...] += _colsum(dh2 * fhat)
        df = _rms_bwd(fhat, rf, gf_ref[...], dh2).astype(BF16)
        df_ref[...] = df
        dz = (_mm_nt(df, w2[...]) * (2.0 * zr_ref[...].astype(F32))).astype(BF16)
        dz_ref[...] = dz
        dm = _mm_nt(dz, w1[...])
        h1v = h1_ref[...]
        r1 = _rms_r(h1v)
        h1hat = h1v * r1
        dgm_ref[...] += _colsum(dm * h1hat)
        dh1_ref[...] = dh2 + _rms_bwd(h1hat, r1, gm_ref[...], dm)

    return pl.pallas_call(
        body, name="mlp_bwd", grid=(seq // tm,),
        in_specs=[_row(tm, D_MODEL), _row(tm, D_MODEL), _row(tm, D_FF), _row(tm, D_MODEL),
                  _fix((1, D_MODEL)), _fix((1, D_MODEL)), ANY, ANY],
        out_specs=[_row(tm, D_MODEL), _row(tm, D_FF), _row(tm, D_MODEL), _fix((1, D_MODEL)), _fix((1, D_MODEL))],
        out_shape=[_sds((seq, D_MODEL), BF16), _sds((seq, D_FF), BF16), _sds((seq, D_MODEL), F32),
                   _sds((1, D_MODEL), F32), _sds((1, D_MODEL), F32)],
        scratch_shapes=[pltpu.VMEM(w_ff1.shape, BF16), pltpu.VMEM(w_ff2.shape, BF16)],
        compiler_params=_params(1, 56),
    )(h1, f, zr, dh2, g_pre_mlp, g_post_mlp, w_ff1, w_ff2)


def _merge_bwd(dh1, y, gates, ba, bb, d_pool, g_post_mix, pool_scale, w_out, w_ba, w_bb, w_pool_bf, tm):
    seq = dh1.shape[0]

    def body(dh1_ref, y_ref, gates_ref, ba_ref, bb_ref, d_ref, g_ref, ps_ref, wout_ref, wba_ref, wbb_ref, wp_ref,
             dy_ref, dba_ref, dbb_ref, dgpre_ref, dattn_ref, dyp_ref, dd_ref, dg_ref, dbg_ref, dps_ref):
        @pl.when(pl.program_id(0) == 0)
        def _():
            dg_ref[...] = jnp.zeros_like(dg_ref)
            dbg_ref[...] = jnp.zeros_like(dbg_ref)
            dps_ref[...] = jnp.zeros_like(dps_ref)

        dh1v = dh1_ref[...]
        yv = y_ref[...]
        r = _rms_r(yv)
        yhat = yv * r
        dg_ref[...] += _colsum(dh1v * yhat)
        dy = _rms_bwd(yhat, r, g_ref[...], dh1v).astype(BF16)
        dy_ref[...] = dy
        dmerged = _mm_nt(dy, wout_ref[...])
        for half, branch_ref, dbranch_ref in ((0, ba_ref, dba_ref), (1, bb_ref, dbb_ref)):
            lanes = slice(D_MODEL * half, D_MODEL * (half + 1))
            gate = gates_ref[:, lanes].astype(F32)
            dpre = dmerged * branch_ref[...].astype(F32) * gate * (1.0 - gate)
            dbg_ref[:, lanes] += _colsum(dpre)
            dgpre_ref[:, lanes] = dpre.astype(BF16)
            dbranch_ref[...] = (dmerged * gate).astype(BF16)
        dattn = _mm_nt(dba_ref[...], wba_ref[...]).astype(BF16)
        for h in range(N_HEADS):
            dattn_ref[h] = dattn[:, HEAD_PAD * h:HEAD_PAD * (h + 1)]
        dpooled = _mm_nt(dbb_ref[...], wbb_ref[...])
        for g in range(len(POOL_WINDOWS)):
            lanes = slice(POOL_GROUP * g, POOL_GROUP * (g + 1))
            dpl = dpooled[:, lanes]
            dps_ref[:, lanes] += _colsum(dpl * _mm(d_ref[:, lanes], wp_ref[g]))
            dyp = (dpl * ps_ref[:, lanes]).astype(BF16)
            dyp_ref[:, lanes] = dyp
            dd_ref[:, lanes] = _mm_nt(dyp, wp_ref[g])

    return pl.pallas_call(
        body, name="merge_bwd", grid=(seq // tm,),
        in_specs=[_row(tm, D_MODEL), _row(tm, D_MODEL), _row(tm, 2 * D_MODEL), _row(tm, D_MODEL), _row(tm, D_MODEL),
                  _row(tm, POOL_WIDTH), _fix((1, D_MODEL)), _fix((1, POOL_WIDTH)),
                  _fix(w_out.shape), _fix(w_ba.shape), _fix(w_bb.shape), _fix(w_pool_bf.shape)],
        out_specs=[_row(tm, D_MODEL), _row(tm, D_MODEL), _row(tm, D_MODEL), _row(tm, 2 * D_MODEL),
                   pl.BlockSpec((N_HEADS, tm, HEAD_PAD), lambda i: (0, i, 0)), _row(tm, POOL_WIDTH), _row(tm, POOL_WIDTH),
                   _fix((1, D_MODEL)), _fix((1, 2 * D_MODEL)), _fix((1, POOL_WIDTH))],
        out_shape=[_sds((seq, D_MODEL), BF16), _sds((seq, D_MODEL), BF16), _sds((seq, D_MODEL), BF16),
                   _sds((seq, 2 * D_MODEL), BF16), _sds((N_HEADS, seq, HEAD_PAD), BF16), _sds((seq, POOL_WIDTH), BF16),
                   _sds((seq, POOL_WIDTH), F32), _sds((1, D_MODEL), F32), _sds((1, 2 * D_MODEL), F32),
                   _sds((1, POOL_WIDTH), F32)],
        compiler_params=_params(1, 48),
    )(dh1, y, gates, ba, bb, d_pool, g_post_mix, pool_scale, w_out, w_ba, w_bb, w_pool_bf)


def _proj_bwd(dq, dk, dv, qd, kvd, x, dh1, dgpre, dpin, cf, s1, s2, g_pre, g_q, g_kv,
              w_in_r, w_uq_r, w_k_exp, w_v, tm):
    seq = x.shape[0]

    def body(dq_ref, dk_ref, dv_ref, qd_ref, kvd_ref, x_ref, dh1_ref, dgpre_ref, dpin_ref, cf_ref, s1_ref, s2_ref,
             gpre_ref, gq_ref, gkv_ref, win_hbm, wuq_hbm, wk_hbm, wv_hbm,
             gx_ref, dproj_ref, dqb_ref, qn_ref, kvn_ref, dkb_ref, dvb_ref, dgpre_acc, dgq_acc, dgkv_acc,
             win, wuq, wk, wv):
        _load_once(((win_hbm, win), (wuq_hbm, wuq), (wk_hbm, wk), (wv_hbm, wv)))

        @pl.when(pl.program_id(0) == 0)
        def _():
            dgpre_acc[...] = jnp.zeros_like(dgpre_acc)
            dgq_acc[...] = jnp.zeros_like(dgq_acc)
            dgkv_acc[...] = jnp.zeros_like(dgkv_acc)

        cfv, s1v, s2v = cf_ref[...], s1_ref[...], s2_ref[...]
        ksum = jnp.zeros((tm, HEAD_PAD), F32)
        for h in range(N_HEADS):
            lanes = slice(HEAD_PAD * h, HEAD_PAD * (h + 1))
            dqb_ref[:, lanes] = (_rope_t(dq_ref[h], cfv, s1v, s2v) * ATTN_SCALE).astype(BF16)
            dkh = dk_ref[h]
            dkb_ref[:, lanes] = dkh
            dvb_ref[:, lanes] = dv_ref[h]
            ksum = ksum + dkh.astype(F32)
        lane = lax.broadcasted_iota(jnp.int32, (tm, HEAD_PAD), 1)
        rope_lanes = (lane >= QK_NOPE) & (lane < QK_NOPE + QK_ROPE)
        dkr = _rope_t(jnp.where(rope_lanes, ksum, 0.0), cfv, s1v, s2v)

        qdv = qd_ref[...]
        rq = _rms_r(qdv)
        qhat = qdv * rq
        qn_ref[...] = (qhat * gq_ref[...]).astype(BF16)
        dqn = _mm_nt(dqb_ref[...], wuq[...])
        dgq_acc[...] += _colsum(dqn * qhat)
        dproj_ref[:, IN_Q0:IN_KV0] = _rms_bwd(qhat, rq, gq_ref[...], dqn).astype(BF16)

        kvdv = kvd_ref[...]
        rkv = _rms_r(kvdv)
        kvhat = kvdv * rkv
        kvn_ref[...] = (kvhat * gkv_ref[...]).astype(BF16)
        dkvn = _mm_nt(dkb_ref[...], wk[...]) + _mm_nt(dvb_ref[...], wv[...])
        dgkv_acc[...] += _colsum(dkvn * kvhat)
        dproj_ref[:, IN_KV0:IN_POOL0] = _rms_bwd(kvhat, rkv, gkv_ref[...], dkvn).astype(BF16)

        dproj_ref[:, IN_POOL0:IN_GATE0] = dpin_ref[...]
        dproj_ref[:, IN_GATE0:IN_KR0] = dgpre_ref[...]
        dproj_ref[:, IN_KR0:IN_R] = dkr.astype(BF16)

        da = _mm_nt(dproj_ref[...], win[...])
        xv = x_ref[...]
        r0 = _rms_r(xv)
        xhat = xv * r0
        dgpre_acc[...] += _colsum(da * xhat)
        gx_ref[...] = dh1_ref[...] + _rms_bwd(xhat, r0, gpre_ref[...], da)

    heads = pl.BlockSpec((N_HEADS, tm, HEAD_PAD), lambda i: (0, i, 0))
    return pl.pallas_call(
        body, name="proj_bwd", grid=(seq // tm,),
        in_specs=[heads, heads, heads, _row(tm, Q_LORA), _row(tm, KV_LORA), _row(tm, D_MODEL),
                  _row(tm, D_MODEL), _row(tm, 2 * D_MODEL), _row(tm, POOL_WIDTH),
                  _row(tm, HEAD_PAD), _row(tm, HEAD_PAD), _row(tm, HEAD_PAD),
                  _fix((1, D_MODEL)), _fix((1, Q_LORA)), _fix((1, KV_LORA)), ANY, ANY, ANY, ANY],
        out_specs=[_row(tm, D_MODEL), _row(tm, IN_R), _row(tm, N_HEADS * HEAD_PAD), _row(tm, Q_LORA), _row(tm, KV_LORA),
                   _row(tm, N_HEADS * HEAD_PAD), _row(tm, N_HEADS * HEAD_PAD),
                   _fix((1, D_MODEL)), _fix((1, Q_LORA)), _fix((1, KV_LORA))],
        out_shape=[_sds((seq, D_MODEL), F32), _sds((seq, IN_R), BF16), _sds((seq, N_HEADS * HEAD_PAD), BF16),
                   _sds((seq, Q_LORA), BF16), _sds((seq, KV_LORA), BF16), _sds((seq, N_HEADS * HEAD_PAD), BF16),
                   _sds((seq, N_HEADS * HEAD_PAD), BF16),
                   _sds((1, D_MODEL), F32), _sds((1, Q_LORA), F32), _sds((1, KV_LORA), F32)],
        scratch_shapes=[pltpu.VMEM(w_in_r.shape, BF16), pltpu.VMEM(w_uq_r.shape, BF16),
                        pltpu.VMEM(w_k_exp.shape, BF16), pltpu.VMEM(w_v.shape, BF16)],
        compiler_params=_params(1, 48),
    )(dq, dk, dv, qd, kvd, x, dh1, dgpre, dpin, cf, s1, s2, g_pre, g_q, g_kv, w_in_r, w_uq_r, w_k_exp, w_v)


def _grad_w(a, b, name, square_a=False):
    seq, k_dim = a.shape
    n_dim = b.shape[1]
    tk = min(k_dim, 1024)
    tn = n_dim // 2 if n_dim == IN_R else min(n_dim, 1024)
    ts = min(seq, 1024)

    def body(a_ref, b_ref, o_ref):
        @pl.when(pl.program_id(2) == 0)
        def _():
            o_ref[...] = jnp.zeros_like(o_ref)

        at = a_ref[...]
        if square_a:
            at = at * at
        o_ref[...] += lax.dot_general(at, b_ref[...], TN, preferred_element_type=F32)

    return pl.pallas_call(
        body, name=name, grid=(k_dim // tk, n_dim // tn, seq // ts),
        in_specs=[pl.BlockSpec((ts, tk), lambda i, j, s: (s, i)), pl.BlockSpec((ts, tn), lambda i, j, s: (s, j))],
        out_specs=pl.BlockSpec((tk, tn), lambda i, j, s: (i, j)),
        out_shape=_sds((k_dim, n_dim), F32),
        compiler_params=_params(3, 48),
    )(a, b)


def _grad_w_heads(a, b, name):
    heads, seq, _ = a.shape
    n_dim = b.shape[1]
    ts = min(seq, 1024)

    def body(a_ref, b_ref, o_ref):
        @pl.when(pl.program_id(1) == 0)
        def _():
            o_ref[...] = jnp.zeros_like(o_ref)

        o_ref[...] += lax.dot_general(a_ref[...], b_ref[...], TN, preferred_element_type=F32)

    return pl.pallas_call(
        body, name=name, grid=(heads, seq // ts),
        in_specs=[pl.BlockSpec((None, ts, HEAD_PAD), lambda h, s: (h, s, 0)), pl.BlockSpec((ts, n_dim), lambda h, s: (s, 0))],
        out_specs=pl.BlockSpec((HEAD_PAD, n_dim), lambda h, s: (h, 0)),
        out_shape=_sds((heads * HEAD_PAD, n_dim), F32),
        compiler_params=_params(2, 32),
    )(a, b)


def _grad_w_pool(d, dyp):
    seq = d.shape[0]
    ts = min(seq, 1024)

    def body(a_ref, b_ref, o_ref):
        @pl.when(pl.program_id(1) == 0)
        def _():
            o_ref[...] = jnp.zeros_like(o_ref)

        o_ref[...] += lax.dot_general(a_ref[...], b_ref[...], TN, preferred_element_type=F32)

    tile = pl.BlockSpec((ts, POOL_GROUP), lambda g, s: (s, g))
    return pl.pallas_call(
        body, name="grad_w_pool", grid=(len(POOL_WINDOWS), seq // ts),
        in_specs=[tile, tile],
        out_specs=pl.BlockSpec((None, POOL_GROUP, POOL_GROUP), lambda g, s: (g, 0, 0)),
        out_shape=_sds((len(POOL_WINDOWS), POOL_GROUP, POOL_GROUP), F32),
        compiler_params=_params(2, 32),
    )(d, dyp)


def _position():
    return lax.axis_index("x"), lax.axis_index("y"), lax.axis_index("c")


def _gather_copies(x_ref, slot, send_sems, recv_sems, local_sem):
    x, y, c = _position()
    me, sibling = (x, y, c), (x, y, 1 - c)
    chips = [(1 - x, y), (x, 1 - y), (1 - x, 1 - y)]

    def copy(k, block, to, src=None):
        return pltpu.make_async_remote_copy(
            src_ref=slot(*block) if src is None else src, dst_ref=slot(*block),
            send_sem=send_sems.at[k], recv_sem=recv_sems.at[k], device_id=to, device_id_type=MESH)

    mine = pltpu.make_async_copy(x_ref, slot(*me), local_sem)
    mine.start()
    first = [copy(0, me, sibling, src=x_ref)]
    first += [copy(1 + j, me, (*chip, c), src=x_ref) for j, chip in enumerate(chips)]
    for cp in first:
        cp.start()
    passed = [copy(4 + j, (*chip, c), sibling) for j, chip in enumerate(chips)]
    for j, chip in enumerate(chips):
        copy(1 + j, (*chip, c), me).wait_recv()
        passed[j].start()
    copy(0, sibling, me).wait_recv()
    for j, chip in enumerate(chips):
        copy(4 + j, (*chip, 1 - c), me).wait_recv()
    for cp in first + passed:
        cp.wait_send()
    mine.wait()


def _all_gather_hbm(block):
    def body(x_ref, out_ref, send_sems, recv_sems, local_sem):
        _gather_copies(x_ref, lambda px, py, pc: out_ref.at[4 * px + 2 * py + pc], send_sems, recv_sems, local_sem)

    return pl.pallas_call(
        body, name="gather_weights",
        in_specs=[ANY], out_specs=ANY,
        out_shape=_sds((N_DEV,) + block.shape, block.dtype),
        scratch_shapes=[pltpu.SemaphoreType.DMA((7,)), pltpu.SemaphoreType.DMA((7,)), pltpu.SemaphoreType.DMA],
    )(block)


def _all_reduce_small(block):
    def body(x_ref, out_ref, buf, send_sems, recv_sems, local_sem):
        _gather_copies(x_ref, lambda px, py, pc: buf.at[4 * px + 2 * py + pc], send_sems, recv_sems, local_sem)
        acc = buf[0]
        for k in range(1, N_DEV):
            acc = acc + buf[k]
        out_ref[...] = acc

    vmem = pl.BlockSpec(memory_space=pltpu.VMEM)
    return pl.pallas_call(
        body, name="all_reduce_small",
        in_specs=[vmem], out_specs=vmem,
        out_shape=_sds(block.shape, F32),
        scratch_shapes=[pltpu.VMEM((N_DEV,) + block.shape, F32), pltpu.SemaphoreType.DMA((7,)),
                        pltpu.SemaphoreType.DMA((7,)), pltpu.SemaphoreType.DMA],
        compiler_params=pltpu.CompilerParams(vmem_limit_bytes=32 * MIB),
    )(block)


def _exchange_pair(g):
    def body(g_ref, out_ref, send_sems, recv_sems):
        x, y, c = _position()
        copies = []
        for chip in range(4):
            cp = pltpu.make_async_remote_copy(
                src_ref=g_ref.at[2 * chip + (1 - c)], dst_ref=out_ref.at[chip],
                send_sem=send_sems.at[chip], recv_sem=recv_sems.at[chip], device_id=(x, y, 1 - c), device_id_type=MESH)
            cp.start()
            copies.append(cp)
        for cp in copies:
            cp.wait_recv()
        for cp in copies:
            cp.wait_send()

    return pl.pallas_call(
        body, name="exchange_pair",
        in_specs=[ANY], out_specs=ANY,
        out_shape=_sds((4,) + g.shape[1:], g.dtype),
        scratch_shapes=[pltpu.SemaphoreType.DMA((4,)), pltpu.SemaphoreType.DMA((4,))],
    )(g)


def _exchange_chips(part):
    def body(p_ref, out_ref, send_sems, recv_sems):
        x, y, c = _position()
        chips = [(1 - x, y), (x, 1 - y), (1 - x, 1 - y)]
        copies = []
        for k, (px, py) in enumerate(chips):
            cp = pltpu.make_async_remote_copy(
                src_ref=p_ref.at[2 * px + py], dst_ref=out_ref.at[k],
                send_sem=send_sems.at[k], recv_sem=recv_sems.at[k], device_id=(px, py, c), device_id_type=MESH)
            cp.start()
            copies.append(cp)
        for cp in copies:
            cp.wait_recv()
        for cp in copies:
            cp.wait_send()

    return pl.pallas_call(
        body, name="exchange_chips",
        in_specs=[ANY], out_specs=ANY,
        out_shape=_sds((3,) + part.shape[1:], part.dtype),
        scratch_shapes=[pltpu.SemaphoreType.DMA((3,)), pltpu.SemaphoreType.DMA((3,))],
    )(part)


def _pair_sum(g, recv, core):
    rows = g.shape[1]
    tr = PACK_ROW_TILE
    g4 = g.reshape(4, 2, rows, LANES)

    def body(s_ref, g_ref, r_ref, o_ref):
        o_ref[...] = g_ref[...] + r_ref[...]

    spec = pltpu.PrefetchScalarGridSpec(
        num_scalar_prefetch=1, grid=(4, rows // tr),
        in_specs=[pl.BlockSpec((None, None, tr, LANES), lambda j, i, s: (j, s[0], i, 0)),
                  pl.BlockSpec((None, tr, LANES), lambda j, i, s: (j, i, 0))],
        out_specs=pl.BlockSpec((None, tr, LANES), lambda j, i, s: (j, i, 0)))
    return pl.pallas_call(
        body, name="pair_sum", grid_spec=spec, out_shape=_sds((4, rows, LANES), F32),
        compiler_params=_params(2, 32),
    )(core, g4, recv)


def _adamw_math(g, w, m, v):
    m = ADAM_B1 * m + (1.0 - ADAM_B1) * g
    v = ADAM_B2 * v + (1.0 - ADAM_B2) * (g * g)
    m_hat = m / (1.0 - ADAM_B1 ** ADAM_STEP)
    v_hat = v / (1.0 - ADAM_B2 ** ADAM_STEP)
    delta = -ADAM_LR * (m_hat / (jnp.sqrt(v_hat) + ADAM_EPS) + ADAM_WD * w)
    return delta, m, v


def _adamw_sharded(part, recv, chip, w, m, v):
    rows = w.shape[0]
    tr = PACK_ROW_TILE

    def body(s_ref, p_ref, r0_ref, r1_ref, r2_ref, w_ref, m_ref, v_ref, g_ref, d_ref, nm_ref, nv_ref):
        g = ((p_ref[...] + r0_ref[...]) + r1_ref[...]) + r2_ref[...]
        g_ref[...] = g
        d_ref[...], nm_ref[...], nv_ref[...] = _adamw_math(g, w_ref[...], m_ref[...], v_ref[...])

    tile = pl.BlockSpec((tr, LANES), lambda i, s: (i, 0))

    def slot(k):
        return pl.BlockSpec((None, tr, LANES), lambda i, s: (k, i, 0))

    spec = pltpu.PrefetchScalarGridSpec(
        num_scalar_prefetch=1, grid=(rows // tr,),
        in_specs=[pl.BlockSpec((None, tr, LANES), lambda i, s: (s[0], i, 0)), slot(0), slot(1), slot(2), tile, tile, tile],
        out_specs=[tile] * 4)
    return pl.pallas_call(
        body, name="adamw_sharded", grid_spec=spec, out_shape=[_sds((rows, LANES), F32)] * 4,
        compiler_params=_params(1, 48),
    )(chip, part, recv, recv, recv, w, m, v)


def _adamw_small(g, w, m, v):
    def body(g_ref, w_ref, m_ref, v_ref, d_ref, nm_ref, nv_ref):
        d_ref[...], nm_ref[...], nv_ref[...] = _adamw_math(g_ref[...], w_ref[...], m_ref[...], v_ref[...])

    return pl.pallas_call(body, name="adamw_small", out_shape=[_sds(g.shape, F32)] * 3)(g, w, m, v)


def _pack_rows(parts):
    parts = [a.reshape(-1, LANES) for a in parts]
    pad = (-sum(a.shape[0] for a in parts)) % PACK_ROW_TILE
    return jnp.concatenate(parts + [jnp.zeros((pad, LANES), parts[0].dtype)], axis=0)


def _pack_small(parts):
    flat = jnp.concatenate([a.reshape(-1) for a in parts])
    pad = (-flat.shape[0]) % (8 * LANES)
    return jnp.pad(flat, (0, pad)).reshape(-1, LANES)


def _unpack_small(packed, shapes):
    flat = packed.reshape(-1)
    out, off = [], 0
    for shape in shapes:
        size = 1
        for n in shape:
            size *= n
        out.append(flat[off:off + size].reshape(shape))
        off += size
    return out


def _full_from_gathered(gathered, shard_shapes):
    out, off = {}, 0
    for (name, kind), (k, n) in zip(SHARDED, shard_shapes):
        rows = k * n // LANES
        seg = gathered[:, off:off + rows].reshape(N_DEV, k, n)
        out[name] = jnp.transpose(seg, (1, 0, 2)).reshape(k, N_DEV * n) if kind == "col" else seg.reshape(N_DEV * k, n)
        off += rows
    return out


def _shards_from_full(full, shard_shapes):
    parts = []
    for (name, kind), (k, n) in zip(SHARDED, shard_shapes):
        a = full[name]
        seg = jnp.transpose(a.reshape(k, N_DEV, n), (1, 0, 2)) if kind == "col" else a.reshape(N_DEV, k, n)
        parts.append(seg.reshape(N_DEV, k * n // LANES, LANES))
    pad = (-sum(a.shape[1] for a in parts)) % PACK_ROW_TILE
    return jnp.concatenate(parts + [jnp.zeros((N_DEV, pad, LANES), F32)], axis=1)


def _unpack_rows(packed, shard_shapes):
    out, off = {}, 0
    for (name, _), (k, n) in zip(SHARDED, shard_shapes):
        rows = k * n // LANES
        out[name] = packed[off:off + rows].reshape(1, k, n)
        off += rows
    return out


def _rope_tables(positions):
    inv_freq = ROPE_THETA ** (-jnp.arange(0, QK_ROPE, 2, dtype=F32) / QK_ROPE)
    ang = positions.astype(F32)[:, None] * inv_freq
    cos, sin = jnp.cos(ang), jnp.sin(ang)
    seq = positions.shape[0]
    zeros = lambda n: jnp.zeros((seq, n), F32)
    cf = jnp.concatenate([jnp.ones((seq, QK_NOPE), F32), cos, cos, zeros(HEAD_PAD - QK_NOPE - QK_ROPE)], axis=1)
    s1 = jnp.concatenate([zeros(QK_NOPE), -sin, zeros(HEAD_PAD - QK_NOPE - QK_ROPE // 2)], axis=1)
    s2 = jnp.concatenate([zeros(QK_NOPE + QK_ROPE // 2), sin, zeros(HEAD_PAD - QK_NOPE - QK_ROPE)], axis=1)
    return cf, s1, s2


def _arrange_w_in(w):
    k = w.shape[0]
    zeros = lambda n: jnp.zeros((k, n), w.dtype)
    kr0 = Q_LORA + KV_LORA
    pool0 = kr0 + QK_ROPE
    return jnp.concatenate([w[:, :kr0], w[:, pool0:], zeros(QK_NOPE), w[:, kr0:pool0],
                            zeros(HEAD_PAD - QK_NOPE - QK_ROPE)], axis=1)


def _restore_w_in(d):
    kr = d[:, IN_KR0 + QK_NOPE:IN_KR0 + QK_NOPE + QK_ROPE]
    return jnp.concatenate([d[:, :IN_POOL0], kr, d[:, IN_POOL0:IN_KR0]], axis=1)


def _pad_heads(w, width):
    k = w.shape[0]
    w = w.reshape(k, N_HEADS, width)
    return jnp.pad(w, ((0, 0), (0, 0), (0, HEAD_PAD - width))).reshape(k, N_HEADS * HEAD_PAD)


def _unpad_heads(d, width):
    k = d.shape[0]
    return d.reshape(k, N_HEADS, HEAD_PAD)[:, :, :width]


def kernel(x, p, positions, g_pre_mix, w_in, b_gate, g_q, w_uq, g_kv, w_ukv, w_pool, pool_scale, w_branch_attn, w_branch_pool, w_out, g_post_mix, g_pre_mlp, w_ff1, w_ff2, g_post_mlp, w_ple_proj, w_ple_gate, g_ple, loss_target, m_g_pre_mix, m_w_in, m_b_gate, m_g_q, m_w_uq, m_g_kv, m_w_ukv, m_w_pool, m_pool_scale, m_w_branch_attn, m_w_branch_pool, m_w_out, m_g_post_mix, m_g_pre_mlp, m_w_ff1, m_w_ff2, m_g_post_mlp, m_w_ple_proj, m_w_ple_gate, m_g_ple, v_g_pre_mix, v_w_in, v_b_gate, v_g_q, v_w_uq, v_g_kv, v_w_ukv, v_w_pool, v_pool_scale, v_w_branch_attn, v_w_branch_pool, v_w_out, v_g_post_mix, v_g_pre_mlp, v_w_ff1, v_w_ff2, v_g_post_mlp, v_w_ple_proj, v_w_ple_gate, v_g_ple):
    given = dict(locals())
    weights = {n: given[n] for n in WEIGHT_ORDER}
    moments_m = {n: given["m_" + n] for n in WEIGHT_ORDER}
    moments_v = {n: given["v_" + n] for n in WEIGHT_ORDER}
    xs, ps, target = x[0], p[0, 0], loss_target[0]
    seq = xs.shape[0]
    tm = min(256, seq)
    core = lax.axis_index("c")
    chip = 2 * lax.axis_index("x") + lax.axis_index("y")

    shard_shapes = [weights[n].shape[1:] for n, _ in SHARDED]
    gathered = _all_gather_hbm(_pack_rows([weights[n][0] for n, _ in SHARDED]).astype(BF16))
    full = _full_from_gathered(gathered, shard_shapes)
    w_in_r = _arrange_w_in(full["w_in"])
    w_uq_r = _pad_heads(full["w_uq"], QK_NOPE + QK_ROPE)
    ukv = full["w_ukv"].reshape(KV_LORA, N_HEADS, QK_NOPE + V_HEAD)
    w_k_exp = _pad_heads(ukv[:, :, :QK_NOPE].reshape(KV_LORA, N_HEADS * QK_NOPE), QK_NOPE)
    w_v = _pad_heads(ukv[:, :, QK_NOPE:].reshape(KV_LORA, N_HEADS * V_HEAD), V_HEAD)
    w_ba = jnp.pad(full["w_branch_attn"].reshape(N_HEADS, V_HEAD, D_MODEL),
                   ((0, 0), (0, HEAD_PAD - V_HEAD), (0, 0))).reshape(N_HEADS * HEAD_PAD, D_MODEL)
    w_pool_bf = w_pool[0].astype(BF16)
    cf, s1, s2 = _rope_tables(positions[0])

    a_bf, qd, kvd, pin, gates, q, k, v = _proj_fwd(xs, g_pre_mix, b_gate, g_q, g_kv, cf, s1, s2,
                                                   w_in_r, w_uq_r, w_k_exp, w_v, tm)
    d_pool, pooled = _pool_fwd(pin, w_pool_bf, pool_scale, tm)
    o_heads, lse = _attn_fwd(q, k, v)
    merged, ba, bb, y, h1 = _merge_fwd(o_heads, pooled, gates, xs, g_post_mix, w_ba,
                                       full["w_branch_pool"], full["w_out"], tm)
    (m_bf, zr, f, h2_bf, p_bf, de, dpre, dh2, loss_acc, dg_ple) = _tail_fwd(
        h1, target, ps, g_pre_mlp, g_post_mlp, g_ple, full["w_ff1"], full["w_ff2"], full["w_ple_proj"],
        full["w_ple_gate"], tm)

    df, dz, dh1, dg_pre_mlp, dg_post_mlp = _mlp_bwd(h1, f, zr, dh2, g_pre_mlp, g_post_mlp, full["w_ff1"],
                                                    full["w_ff2"], tm)
    (dy, dba, dbb, dgpre, do_heads, dyp, dd, dg_post_mix, db_gate, dpool_scale) = _merge_bwd(
        dh1, y, gates, ba, bb, d_pool, g_post_mix, pool_scale, full["w_out"], w_ba,
        full["w_branch_pool"], w_pool_bf, tm)
    dpin = _pool_bwd_window(dd, tm)
    delta = _attn_delta(o_heads, do_heads)
    dq, dk, dv = _attn_bwd(q, k, v, do_heads, lse, delta)
    (grad_x, dproj, dq_bf, qn_bf, kvn_bf, dk_bf, dv_bf, dg_pre_mix, dg_q, dg_kv) = _proj_bwd(
        dq, dk, dv, qd, kvd, xs, dh1, dgpre, dpin, cf, s1, s2, g_pre_mix, g_q, g_kv, w_in_r, w_uq_r, w_k_exp, w_v, tm)

    d_k_exp = _unpad_heads(_grad_w(kvn_bf, dk_bf, "grad_w_uk"), QK_NOPE)
    d_w_v = _unpad_heads(_grad_w(kvn_bf, dv_bf, "grad_w_uv"), V_HEAD)
    d_w_ba = _grad_w_heads(o_heads, dba, "grad_w_branch_attn").reshape(N_HEADS, HEAD_PAD, D_MODEL)
    grads_full = {
        "w_in": _restore_w_in(_grad_w(a_bf, dproj, "grad_w_in")),
        "w_uq": _unpad_heads(_grad_w(qn_bf, dq_bf, "grad_w_uq"), QK_NOPE + QK_ROPE).reshape(Q_LORA, -1),
        "w_ukv": jnp.concatenate([d_k_exp, d_w_v], axis=2).reshape(KV_LORA, -1),
        "w_branch_attn": d_w_ba[:, :V_HEAD].reshape(N_HEADS * V_HEAD, D_MODEL),
        "w_branch_pool": _grad_w(pooled, dbb, "grad_w_branch_pool"),
        "w_out": _grad_w(merged, dy, "grad_w_out"),
        "w_ff1": _grad_w(m_bf, dz, "grad_w_ff1"),
        "w_ff2": _grad_w(zr, df, "grad_w_ff2", square_a=True),
        "w_ple_proj": _grad_w(p_bf, de, "grad_w_ple_proj"),
        "w_ple_gate": _grad_w(h2_bf, dpre, "grad_w_ple_gate"),
    }
    grads_small = {
        "g_pre_mix": dg_pre_mix, "b_gate": db_gate, "g_q": dg_q, "g_kv": dg_kv,
        "w_pool": _grad_w_pool(d_pool, dyp), "pool_scale": dpool_scale, "g_post_mix": dg_post_mix,
        "g_pre_mlp": dg_pre_mlp, "g_post_mlp": dg_post_mlp, "g_ple": dg_ple,
    }

    by_device = _shards_from_full(grads_full, shard_shapes)
    from_sibling = _exchange_pair(by_device)
    pair = _pair_sum(by_device, from_sibling, jnp.reshape(core, (1,)).astype(jnp.int32))
    from_chips = _exchange_chips(pair)
    g_sh, d_sh, m_sh, v_sh = _adamw_sharded(
        pair, from_chips, jnp.reshape(chip, (1,)).astype(jnp.int32),
        _pack_rows([weights[n][0] for n, _ in SHARDED]),
        _pack_rows([moments_m[n][0] for n, _ in SHARDED]),
        _pack_rows([moments_v[n][0] for n, _ in SHARDED]))

    g_sm = _all_reduce_small(_pack_small([grads_small[n] for n in REPLICATED] + [loss_acc[0:1, 0:1]]))
    n_small = sum(weights[n].size for n in REPLICATED)
    d_sm, m_sm, v_sm = _adamw_small(g_sm, _pack_small([weights[n] for n in REPLICATED]),
                                    _pack_small([moments_m[n] for n in REPLICATED]),
                                    _pack_small([moments_v[n] for n in REPLICATED]))

    small_shapes = [weights[n].shape for n in REPLICATED]
    results = []
    for sharded, small in ((g_sh, g_sm), (d_sh, d_sm), (m_sh, m_sm), (v_sh, v_sm)):
        named = _unpack_rows(sharded, shard_shapes)
        named.update(zip(REPLICATED, _unpack_small(small, small_shapes)))
        results.append([named[n] for n in WEIGHT_ORDER])

    loss = g_sm.reshape(-1)[n_small]
    return (loss, grad_x[None], *results[0], *results[1], *results[2], *results[3])
```

```python
import jax
import jax.numpy as jnp
from jax import lax
from jax.experimental import pallas as pl
from jax.experimental.pallas import tpu as pltpu

F32 = jnp.float32
BF16 = jnp.bfloat16

D_MODEL = 1024
PLE_DIM = 256
N_HEADS = 8
QK_NOPE = 64
QK_ROPE = 32
V_HEAD = 64
Q_LORA = 384
KV_LORA = 256
POOL_WINDOWS = (2, 4, 8, 16)
POOL_GROUP = 128
POOL_WIDTH = 512
D_FF = 4096
ROPE_THETA = 10000.0
EPS = 1e-6
HEAD_PAD = 128
ATTN_SCALE = (QK_NOPE + QK_ROPE) ** -0.5
LOG2E = 1.4426950408889634
Q_PRESCALE = ATTN_SCALE * LOG2E
ATTN_TILE = 512
FWD_ROWS = 512
FWD_CHAINS = 2

ADAM_LR = 0.001
ADAM_B1 = 0.9
ADAM_B2 = 0.999
ADAM_EPS = 1e-08
ADAM_WD = 0.01
ADAM_STEP = 10

N_DEV = 8
LANES = 1024
PACK_ROW_TILE = 480
POOL_HALO = 16
MIB = 2 ** 20

IN_Q0, IN_KV0, IN_POOL0, IN_GATE0, IN_KR0, IN_R = 0, 384, 640, 1152, 3200, 3328

SHARDED = (("w_in", "col"), ("w_uq", "col"), ("w_ukv", "col"), ("w_branch_attn", "col"),
           ("w_branch_pool", "col"), ("w_out", "row"), ("w_ff1", "col"), ("w_ff2", "row"),
           ("w_ple_proj", "col"), ("w_ple_gate", "row"))
REPLICATED = ("g_pre_mix", "b_gate", "g_q", "g_kv", "w_pool", "pool_scale", "g_post_mix",
              "g_pre_mlp", "g_post_mlp", "g_ple")
WEIGHT_ORDER = ("g_pre_mix", "w_in", "b_gate", "g_q", "w_uq", "g_kv", "w_ukv", "w_pool", "pool_scale",
                "w_branch_attn", "w_branch_pool", "w_out", "g_post_mix", "g_pre_mlp", "w_ff1", "w_ff2",
                "g_post_mlp", "w_ple_proj", "w_ple_gate", "g_ple")

NT = (((1,), (1,)), ((), ()))
TN = (((0,), (0,)), ((), ()))
MESH = pl.DeviceIdType.MESH
ANY = pl.BlockSpec(memory_space=pl.ANY)


def _params(n_axes, vmem_mib):
    return pltpu.CompilerParams(dimension_semantics=("arbitrary",) * n_axes, vmem_limit_bytes=vmem_mib * MIB)


def _row(tm, n):
    return pl.BlockSpec((tm, n), lambda i: (i, 0))


def _fix(shape):
    zeros = (0,) * len(shape)
    return pl.BlockSpec(shape, lambda i: zeros)


def _sds(shape, dtype):
    return jax.ShapeDtypeStruct(shape, dtype)


def _rms_r(v):
    return lax.rsqrt(jnp.mean(v * v, axis=-1, keepdims=True) + EPS)


def _rms_bwd(vhat, r, g, dy):
    gdy = dy * g
    return r * (gdy - vhat * jnp.mean(gdy * vhat, axis=-1, keepdims=True))


def _colsum(v):
    return jnp.sum(v, axis=0, keepdims=True)


def _sigmoid(v):
    return 1.0 / (1.0 + jnp.exp(-v))


def _mm(a, b):
    return jnp.dot(a, b, preferred_element_type=F32)


def _mm_nt(a, b):
    return lax.dot_general(a, b, NT, preferred_element_type=F32)


def _rope(c, cf, s1, s2):
    return c * cf + pltpu.roll(c, HEAD_PAD - 16, 1) * s1 + pltpu.roll(c, 16, 1) * s2


def _rope_t(c, cf, s1, s2):
    return c * cf + pltpu.roll(c * s1, 16, 1) + pltpu.roll(c * s2, HEAD_PAD - 16, 1)


def _load_once(pairs):
    @pl.when(pl.program_id(0) == 0)
    def _():
        for src, dst in pairs:
            pltpu.sync_copy(src, dst)


def _proj_fwd(x, g_pre, b_gate, g_q, g_kv, cf, s1, s2, w_in_r, w_uq_r, w_k_exp, w_v, tm):
    seq = x.shape[0]

    def body(x_ref, gpre_ref, bg_ref, gq_ref, gkv_ref, cf_ref, s1_ref, s2_ref, win_hbm, wuq_hbm, wk_hbm, wv_hbm,
             a_ref, qd_ref, kvd_ref, pin_ref, gates_ref, q_ref, k_ref, v_ref, kt_ref, win, wuq, wk, wv):
        _load_once(((win_hbm, win), (wuq_hbm, wuq), (wk_hbm, wk), (wv_hbm, wv)))
        xv = x_ref[...]
        a = (xv * _rms_r(xv) * gpre_ref[...]).astype(BF16)
        a_ref[...] = a
        proj = _mm(a, win[...])
        qd = proj[:, IN_Q0:IN_KV0]
        kvd = proj[:, IN_KV0:IN_POOL0]
        qd_ref[...] = qd
        kvd_ref[...] = kvd
        pin_ref[...] = proj[:, IN_POOL0:IN_GATE0]
        gates_ref[...] = _sigmoid(proj[:, IN_GATE0:IN_KR0] + bg_ref[...]).astype(BF16)
        cfv, s1v, s2v = cf_ref[...], s1_ref[...], s2_ref[...]
        krr = _rope(proj[:, IN_KR0:IN_R], cfv, s1v, s2v)
        qn = (qd * _rms_r(qd) * gq_ref[...]).astype(BF16)
        qf = _mm(qn, wuq[...])
        kvn = (kvd * _rms_r(kvd) * gkv_ref[...]).astype(BF16)
        kf = _mm(kvn, wk[...])
        vf = _mm(kvn, wv[...])
        one_lane = (lax.broadcasted_iota(jnp.int32, (tm, HEAD_PAD), 1) == V_HEAD).astype(F32)
        for h in range(N_HEADS):
            lanes = slice(HEAD_PAD * h, HEAD_PAD * (h + 1))
            q_ref[h] = (_rope(qf[:, lanes], cfv, s1v, s2v) * Q_PRESCALE).astype(BF16)
            kh = kf[:, lanes] + krr
            vh = vf[:, lanes] + one_lane
            k_ref[h] = kh.astype(BF16)
            v_ref[h] = vh.astype(BF16)
            kt_ref[h] = jnp.transpose(kh).astype(BF16)

    per_tile = ATTN_TILE // tm
    heads = pl.BlockSpec((N_HEADS, tm, HEAD_PAD), lambda i: (0, i, 0))
    heads_t = pl.BlockSpec((N_HEADS, None, HEAD_PAD, tm), lambda i: (0, i // per_tile, 0, i % per_tile))
    heads_t_shape = _sds((N_HEADS, seq // ATTN_TILE, HEAD_PAD, ATTN_TILE), BF16)
    return pl.pallas_call(
        body, name="proj_fwd", grid=(seq // tm,),
        in_specs=[_row(tm, D_MODEL), _fix((1, D_MODEL)), _fix((1, 2 * D_MODEL)), _fix((1, Q_LORA)), _fix((1, KV_LORA)),
                  _row(tm, HEAD_PAD), _row(tm, HEAD_PAD), _row(tm, HEAD_PAD), ANY, ANY, ANY, ANY],
        out_specs=[_row(tm, D_MODEL), _row(tm, Q_LORA), _row(tm, KV_LORA), _row(tm, POOL_WIDTH), _row(tm, 2 * D_MODEL),
                   heads, heads, heads, heads_t],
        out_shape=[_sds((seq, D_MODEL), BF16), _sds((seq, Q_LORA), F32), _sds((seq, KV_LORA), F32),
                   _sds((seq, POOL_WIDTH), F32), _sds((seq, 2 * D_MODEL), BF16),
                   _sds((N_HEADS, seq, HEAD_PAD), BF16), _sds((N_HEADS, seq, HEAD_PAD), BF16),
                   _sds((N_HEADS, seq, HEAD_PAD), BF16), heads_t_shape],
        scratch_shapes=[pltpu.VMEM(w_in_r.shape, BF16), pltpu.VMEM(w_uq_r.shape, BF16),
                        pltpu.VMEM(w_k_exp.shape, BF16), pltpu.VMEM(w_v.shape, BF16)],
        compiler_params=_params(1, 48),
    )(x, g_pre, b_gate, g_q, g_kv, cf, s1, s2, w_in_r, w_uq_r, w_k_exp, w_v)


def _window_count(row0, n_rows):
    t = row0 + lax.broadcasted_iota(jnp.int32, (n_rows, POOL_GROUP), 0)
    return [jnp.minimum(t + 1, w).astype(F32) for w in POOL_WINDOWS]


def _pool_fwd(pin, w_pool_bf, pool_scale, tm):
    seq = pin.shape[0]
    ext_rows = tm + POOL_HALO

    def body(prev_ref, u_ref, wp_ref, ps_ref, d_ref, pooled_ref):
        i = pl.program_id(0)
        prev = jnp.where(i == 0, 0.0, prev_ref[...])
        u = u_ref[...]
        level = jnp.concatenate([prev, u], axis=0)
        counts = _window_count(i * tm, tm)
        shift = 1
        for g in range(len(POOL_WINDOWS)):
            level = level + pltpu.roll(level, shift, 0)
            shift *= 2
            lanes = slice(POOL_GROUP * g, POOL_GROUP * (g + 1))
            d = (level[POOL_HALO:, lanes] / counts[g] - u[:, lanes]).astype(BF16)
            d_ref[:, lanes] = d
            pooled_ref[:, lanes] = (_mm(d, wp_ref[g]) * ps_ref[:, lanes]).astype(BF16)

    halo = tm // POOL_HALO
    return pl.pallas_call(
        body, name="pool_fwd", grid=(seq // tm,),
        in_specs=[pl.BlockSpec((POOL_HALO, POOL_WIDTH), lambda i: (jnp.maximum(i * halo - 1, 0), 0)),
                  _row(tm, POOL_WIDTH), _fix(w_pool_bf.shape), _fix((1, POOL_WIDTH))],
        out_specs=[_row(tm, POOL_WIDTH), _row(tm, POOL_WIDTH)],
        out_shape=[_sds((seq, POOL_WIDTH), BF16), _sds((seq, POOL_WIDTH), BF16)],
        compiler_params=_params(1, 32),
    )(pin, pin, w_pool_bf, pool_scale)


def _pool_bwd_window(dd, tm):
    seq = dd.shape[0]
    n_tiles = seq // tm
    ext_rows = tm + POOL_HALO

    def body(dd_ref, next_ref, dpin_ref):
        i = pl.program_id(0)
        nxt = jnp.where(i == n_tiles - 1, 0.0, next_ref[...])
        dd_t = dd_ref[...]
        ext = jnp.concatenate([dd_t, nxt], axis=0)
        counts = _window_count(i * tm, ext_rows)
        shift = 1
        for g in range(len(POOL_WINDOWS)):
            lanes = slice(POOL_GROUP * g, POOL_GROUP * (g + 1))
            level = ext[:, lanes] / counts[g]
            s = 1
            while s <= shift:
                level = level + pltpu.roll(level, ext_rows - s, 0)
                s *= 2
            shift *= 2
            dpin_ref[:, lanes] = (level[:tm] - dd_t[:, lanes]).astype(BF16)

    halo = tm // POOL_HALO
    return pl.pallas_call(
        body, name="pool_bwd_window", grid=(n_tiles,),
        in_specs=[_row(tm, POOL_WIDTH),
                  pl.BlockSpec((POOL_HALO, POOL_WIDTH), lambda i: (jnp.minimum((i + 1) * halo, seq // POOL_HALO - 1), 0))],
        out_specs=_row(tm, POOL_WIDTH),
        out_shape=_sds((seq, POOL_WIDTH), BF16),
        compiler_params=_params(1, 32),
    )(dd, dd)


def _col_to_row(col, n):
    return jnp.transpose(jnp.broadcast_to(col, (n, HEAD_PAD)))[0:1, :]


def _attn_fwd(q, k, v):
    heads, seq, _ = q.shape
    r, n = FWD_ROWS, FWD_CHAINS
    block = r * n

    def body(q_ref, k_ref, v_ref, o_ref, lse_ref):
        qi = pl.program_id(1)
        q_tiles = [q_ref[c * r:(c + 1) * r, :] for c in range(n)]

        def tile(qt, j, m, acc, diagonal):
            start = pl.multiple_of(j * r, r)
            s = _mm_nt(qt, k_ref[pl.ds(start, r), :])
            if diagonal:
                row = lax.broadcasted_iota(jnp.int32, (r, r), 0)
                col = lax.broadcasted_iota(jnp.int32, (r, r), 1)
                s = jnp.where(col <= row, s, -jnp.inf)
            m_new = jnp.maximum(m, jnp.max(s, axis=1, keepdims=True))
            p = jnp.exp2(s - m_new).astype(BF16)
            acc = jnp.exp2(m - m_new) * acc + _mm(p, v_ref[pl.ds(start, r), :])
            return m_new, acc

        def all_chains(j, carry):
            return tuple(tile(q_tiles[c], j, *carry[c], False) for c in range(n))

        init = tuple((jnp.full((r, 1), -jnp.inf, F32), jnp.zeros((r, HEAD_PAD), F32)) for _ in range(n))
        state = list(lax.fori_loop(0, n * qi, all_chains, init))
        for d in range(n):
            for c in range(d, n):
                state[c] = tile(q_tiles[c], n * qi + d, *state[c], c == d)
        for c, (m, acc) in enumerate(state):
            l = acc[:, V_HEAD:V_HEAD + 1]
            o_ref[c * r:(c + 1) * r, :] = (acc / l).astype(BF16)
            row0 = c * r
            lse_ref[row0 // ATTN_TILE, :, row0 % ATTN_TILE:row0 % ATTN_TILE + r] = _col_to_row(m + jnp.log2(l), r)

    return pl.pallas_call(
        body, name="attn_fwd", grid=(heads, seq // block),
        in_specs=[pl.BlockSpec((None, block, HEAD_PAD), lambda h, i: (h, i, 0)),
                  pl.BlockSpec((None, seq, HEAD_PAD), lambda h, i: (h, 0, 0)),
                  pl.BlockSpec((None, seq, HEAD_PAD), lambda h, i: (h, 0, 0))],
        out_specs=[pl.BlockSpec((None, block, HEAD_PAD), lambda h, i: (h, i, 0)),
                   pl.BlockSpec((None, block // ATTN_TILE, 1, ATTN_TILE), lambda h, i: (h, i, 0, 0))],
        out_shape=[_sds((heads, seq, HEAD_PAD), BF16), _sds((heads, seq // ATTN_TILE, 1, ATTN_TILE), F32)],
        compiler_params=_params(2, 48),
    )(q, k, v)


def _attn_delta(o, do):
    heads, seq, _ = o.shape
    tq = ATTN_TILE
    nq = seq // tq

    def body(o_ref, do_ref, delta_ref):
        prod = o_ref[...].astype(F32) * do_ref[...].astype(F32)
        delta_ref[...] = _col_to_row(jnp.sum(prod, axis=1, keepdims=True), tq)

    tile = pl.BlockSpec((None, tq, HEAD_PAD), lambda h, i: (h, i, 0))
    return pl.pallas_call(
        body, name="attn_delta", grid=(heads, nq),
        in_specs=[tile, tile],
        out_specs=pl.BlockSpec((None, None, 1, tq), lambda h, i: (h, i, 0, 0)),
        out_shape=_sds((heads, nq, 1, tq), F32),
        compiler_params=_params(2, 32),
    )(o, do)


def _attn_bwd(q, k, k_t, v, do, lse, delta):
    heads, seq, _ = q.shape
    t = ATTN_TILE
    nq = seq // t

    def body(q_ref, k_ref, kt_ref, v_ref, do_ref, lse_ref, delta_ref, dq_ref, dk_ref, dv_ref):
        jp = pl.program_id(1)

        @pl.when(jp == 0)
        def _():
            dq_ref[...] = jnp.zeros_like(dq_ref)

        k_a, k_b = k_ref[0:t, :], k_ref[t:2 * t, :]
        v_a, v_b = v_ref[0:t, :], v_ref[t:2 * t, :]

        def tile(kt, k_tr, vt, i, dk, dv, diagonal):
            start = pl.multiple_of(i * t, t)
            qt = q_ref[pl.ds(start, t), :]
            dot = do_ref[pl.ds(start, t), :]
            p_t = jnp.exp2(_mm_nt(kt, qt) - lse_ref[i])
            if diagonal:
                key = lax.broadcasted_iota(jnp.int32, (t, t), 0)
                query = lax.broadcasted_iota(jnp.int32, (t, t), 1)
                p_t = jnp.where(key <= query, p_t, 0.0)
            dv = dv + _mm(p_t.astype(BF16), dot)
            ds_t = (p_t * (_mm_nt(vt, dot) - delta_ref[i])).astype(BF16)
            dk = dk + _mm(ds_t, qt)
            return dk, dv, _mm(k_tr, ds_t)

        def add_dq(i, dq):
            dq_ref[i] += dq

        def both(i, carry):
            dk_a, dv_a, dk_b, dv_b = carry
            dk_a, dv_a, dq_a = tile(k_a, kt_ref[0], v_a, i, dk_a, dv_a, False)
            dk_b, dv_b, dq_b = tile(k_b, kt_ref[1], v_b, i, dk_b, dv_b, False)
            add_dq(i, dq_a + dq_b)
            return dk_a, dv_a, dk_b, dv_b

        zero = jnp.zeros((t, HEAD_PAD), F32)
        dk_a, dv_a, dq_a = tile(k_a, kt_ref[0], v_a, 2 * jp, zero, zero, True)
        add_dq(2 * jp, dq_a)
        dk_a, dv_a, dq_a = tile(k_a, kt_ref[0], v_a, 2 * jp + 1, dk_a, dv_a, False)
        dk_b, dv_b, dq_b = tile(k_b, kt_ref[1], v_b, 2 * jp + 1, zero, zero, True)
        add_dq(2 * jp + 1, dq_a + dq_b)
        dk_a, dv_a, dk_b, dv_b = lax.fori_loop(2 * jp + 2, nq, both, (dk_a, dv_a, dk_b, dv_b))
        dk_ref[0:t, :] = (dk_a * (1.0 / LOG2E)).astype(BF16)
        dk_ref[t:2 * t, :] = (dk_b * (1.0 / LOG2E)).astype(BF16)
        dv_ref[0:t, :] = dv_a.astype(BF16)
        dv_ref[t:2 * t, :] = dv_b.astype(BF16)

    whole = pl.BlockSpec((None, seq, HEAD_PAD), lambda h, j: (h, 0, 0))
    whole_t = pl.BlockSpec((None, nq, HEAD_PAD, t), lambda h, j: (h, 0, 0, 0))
    pair = pl.BlockSpec((None, 2 * t, HEAD_PAD), lambda h, j: (h, j, 0))
    pair_t = pl.BlockSpec((None, 2, HEAD_PAD, t), lambda h, j: (h, j, 0, 0))
    stats = pl.BlockSpec((None, nq, 1, t), lambda h, j: (h, 0, 0, 0))
    return pl.pallas_call(
        body, name="attn_bwd", grid=(heads, nq // 2),
        in_specs=[whole, pair, pair_t, pair, whole, stats, stats],
        out_specs=[whole_t, pair, pair],
        out_shape=[_sds((heads, nq, HEAD_PAD, t), F32), _sds((heads, seq, HEAD_PAD), BF16),
                   _sds((heads, seq, HEAD_PAD), BF16)],
        compiler_params=_params(2, 56),
    )(q, k, k_t, v, do, lse, delta)


def _merge_fwd(attn, pooled, gates, x, g_post_mix, w_ba, w_bb, w_out, tm):
    seq = x.shape[0]

    def body(attn_ref, pooled_ref, gates_ref, x_ref, g_ref, wba_ref, wbb_ref, wout_ref,
             merged_ref, ba_ref, bb_ref, y_ref, h1_ref):
        attn = jnp.concatenate([attn_ref[h] for h in range(N_HEADS)], axis=1)
        ba = _mm(attn, wba_ref[...])
        bb = _mm(pooled_ref[...], wbb_ref[...])
        ba_ref[...] = ba.astype(BF16)
        bb_ref[...] = bb.astype(BF16)
        merged = (gates_ref[:, :D_MODEL].astype(F32) * ba + gates_ref[:, D_MODEL:].astype(F32) * bb).astype(BF16)
        merged_ref[...] = merged
        y = _mm(merged, wout_ref[...])
        y_ref[...] = y
        h1_ref[...] = x_ref[...] + y * _rms_r(y) * g_ref[...]

    return pl.pallas_call(
        body, name="merge_fwd", grid=(seq // tm,),
        in_specs=[pl.BlockSpec((N_HEADS, tm, HEAD_PAD), lambda i: (0, i, 0)), _row(tm, POOL_WIDTH),
                  _row(tm, 2 * D_MODEL), _row(tm, D_MODEL),
                  _fix((1, D_MODEL)), _fix(w_ba.shape), _fix(w_bb.shape), _fix(w_out.shape)],
        out_specs=[_row(tm, D_MODEL)] * 5,
        out_shape=[_sds((seq, D_MODEL), BF16), _sds((seq, D_MODEL), BF16), _sds((seq, D_MODEL), BF16),
                   _sds((seq, D_MODEL), F32), _sds((seq, D_MODEL), F32)],
        compiler_params=_params(1, 48),
    )(attn, pooled, gates, x, g_post_mix, w_ba, w_bb, w_out)


def _tail_fwd(h1, target, p, g_pre_mlp, g_post_mlp, g_ple, w_ff1, w_ff2, w_pe, w_pg, tm):
    seq = h1.shape[0]

    def body(h1_ref, tgt_ref, p_ref, gm_ref, gf_ref, gp_ref, w1_hbm, w2_hbm, wpe_hbm, wpg_hbm,
             m_ref, zr_ref, f_ref, h2b_ref, pb_ref, de_ref, dpre_ref, dh2_ref, loss_ref, dgple_ref,
             w1, w2, wpe, wpg):
        _load_once(((w1_hbm, w1), (w2_hbm, w2), (wpe_hbm, wpe), (wpg_hbm, wpg)))

        @pl.when(pl.program_id(0) == 0)
        def _():
            loss_ref[...] = jnp.zeros_like(loss_ref)
            dgple_ref[...] = jnp.zeros_like(dgple_ref)

        h1v = h1_ref[...]
        m = (h1v * _rms_r(h1v) * gm_ref[...]).astype(BF16)
        m_ref[...] = m
        zr = jnp.maximum(_mm(m, w1[...]), 0.0)
        zr_ref[...] = zr.astype(BF16)
        f = _mm((zr * zr).astype(BF16), w2[...])
        f_ref[...] = f
        h2 = h1v + f * _rms_r(f) * gf_ref[...]
        h2b = h2.astype(BF16)
        h2b_ref[...] = h2b
        pb = p_ref[...].astype(BF16)
        pb_ref[...] = pb
        e = _mm(pb, wpe[...])
        pg = _sigmoid(_mm(h2b, wpg[...]))
        t3 = pg * e
        r3 = _rms_r(t3)
        t3hat = t3 * r3
        diff = h2 + t3hat * gp_ref[...] - tgt_ref[...]
        loss_ref[...] += jnp.sum(diff * diff) * (0.5 / D_MODEL)
        dh3 = diff * (1.0 / D_MODEL)
        dgple_ref[...] += _colsum(dh3 * t3hat)
        dt3 = _rms_bwd(t3hat, r3, gp_ref[...], dh3)
        de_ref[...] = (dt3 * pg).astype(BF16)
        dpre = (dt3 * e * pg * (1.0 - pg)).astype(BF16)
        dpre_ref[...] = dpre
        dh2_ref[...] = dh3 + _mm_nt(dpre, wpg[...])

    return pl.pallas_call(
        body, name="tail_fwd", grid=(seq // tm,),
        in_specs=[_row(tm, D_MODEL), _row(tm, D_MODEL), _row(tm, PLE_DIM), _fix((1, D_MODEL)), _fix((1, D_MODEL)),
                  _fix((1, D_MODEL)), ANY, ANY, ANY, ANY],
        out_specs=[_row(tm, D_MODEL), _row(tm, D_FF), _row(tm, D_MODEL), _row(tm, D_MODEL), _row(tm, PLE_DIM),
                   _row(tm, D_MODEL), _row(tm, D_MODEL), _row(tm, D_MODEL), _fix((8, 128)), _fix((1, D_MODEL))],
        out_shape=[_sds((seq, D_MODEL), BF16), _sds((seq, D_FF), BF16), _sds((seq, D_MODEL), F32),
                   _sds((seq, D_MODEL), BF16), _sds((seq, PLE_DIM), BF16), _sds((seq, D_MODEL), BF16),
                   _sds((seq, D_MODEL), BF16), _sds((seq, D_MODEL), F32), _sds((8, 128), F32), _sds((1, D_MODEL), F32)],
        scratch_shapes=[pltpu.VMEM(w_ff1.shape, BF16), pltpu.VMEM(w_ff2.shape, BF16),
                        pltpu.VMEM(w_pe.shape, BF16), pltpu.VMEM(w_pg.shape, BF16)],
        compiler_params=_params(1, 56),
    )(h1, target, p, g_pre_mlp, g_post_mlp, g_ple, w_ff1, w_ff2, w_pe, w_pg)


def _mlp_bwd(h1, f, zr, dh2, g_pre_mlp, g_post_mlp, w_ff1, w_ff2, tm):
    seq = h1.shape[0]

    def body(h1_ref, f_ref, zr_ref, dh2_ref, gm_ref, gf_ref, w1_hbm, w2_hbm,
             df_ref, dz_ref, dh1_ref, dgm_ref, dgf_ref, w1, w2):
        _load_once(((w1_hbm, w1), (w2_hbm, w2)))

        @pl.when(pl.program_id(0) == 0)
        def _():
            dgm_ref[...] = jnp.zeros_like(dgm_ref)
            dgf_ref[...] = jnp.zeros_like(dgf_ref)

        dh2 = dh2_ref[...]
        fv = f_ref[...]
        rf = _rms_r(fv)
        fhat = fv * rf
        dgf_ref[...] += _colsum(dh2 * fhat)
        df = _rms_bwd(fhat, rf, gf_ref[...], dh2).astype(BF16)
        df_ref[...] = df
        dz = (_mm_nt(df, w2[...]) * (2.0 * zr_ref[...].astype(F32))).astype(BF16)
        dz_ref[...] = dz
        dm = _mm_nt(dz, w1[...])
        h1v = h1_ref[...]
        r1 = _rms_r(h1v)
        h1hat = h1v * r1
        dgm_ref[...] += _colsum(dm * h1hat)
        dh1_ref[...] = dh2 + _rms_bwd(h1hat, r1, gm_ref[...], dm)

    return pl.pallas_call(
        body, name="mlp_bwd", grid=(seq // tm,),
        in_specs=[_row(tm, D_MODEL), _row(tm, D_MODEL), _row(tm, D_FF), _row(tm, D_MODEL),
                  _fix((1, D_MODEL)), _fix((1, D_MODEL)), ANY, ANY],
        out_specs=[_row(tm, D_MODEL), _row(tm, D_FF), _row(tm, D_MODEL), _fix((1, D_MODEL)), _fix((1, D_MODEL))],
        out_shape=[_sds((seq, D_MODEL), BF16), _sds((seq, D_FF), BF16), _sds((seq, D_MODEL), F32),
                   _sds((1, D_MODEL), F32), _sds((1, D_MODEL), F32)],
        scratch_shapes=[pltpu.VMEM(w_ff1.shape, BF16), pltpu.VMEM(w_ff2.shape, BF16)],
        compiler_params=_params(1, 56),
    )(h1, f, zr, dh2, g_pre_mlp, g_post_mlp, w_ff1, w_ff2)


def _merge_bwd(dh1, y, gates, ba, bb, d_pool, g_post_mix, pool_scale, w_out, w_ba, w_bb, w_pool_bf, tm):
    seq = dh1.shape[0]

    def body(dh1_ref, y_ref, gates_ref, ba_ref, bb_ref, d_ref, g_ref, ps_ref, wout_ref, wba_ref, wbb_ref, wp_ref,
             dy_ref, dba_ref, dbb_ref, dgpre_ref, dattn_ref, dyp_ref, dd_ref, dg_ref, dbg_ref, dps_ref):
        @pl.when(pl.program_id(0) == 0)
        def _():
            dg_ref[...] = jnp.zeros_like(dg_ref)
            dbg_ref[...] = jnp.zeros_like(dbg_ref)
            dps_ref[...] = jnp.zeros_like(dps_ref)

        dh1v = dh1_ref[...]
        yv = y_ref[...]
        r = _rms_r(yv)
        yhat = yv * r
        dg_ref[...] += _colsum(dh1v * yhat)
        dy = _rms_bwd(yhat, r, g_ref[...], dh1v).astype(BF16)
        dy_ref[...] = dy
        dmerged = _mm_nt(dy, wout_ref[...])
        for half, branch_ref, dbranch_ref in ((0, ba_ref, dba_ref), (1, bb_ref, dbb_ref)):
            lanes = slice(D_MODEL * half, D_MODEL * (half + 1))
            gate = gates_ref[:, lanes].astype(F32)
            dpre = dmerged * branch_ref[...].astype(F32) * gate * (1.0 - gate)
            dbg_ref[:, lanes] += _colsum(dpre)
            dgpre_ref[:, lanes] = dpre.astype(BF16)
            dbranch_ref[...] = (dmerged * gate).astype(BF16)
        dattn = _mm_nt(dba_ref[...], wba_ref[...]).astype(BF16)
        for h in range(N_HEADS):
            dattn_ref[h] = dattn[:, HEAD_PAD * h:HEAD_PAD * (h + 1)]
        dpooled = _mm_nt(dbb_ref[...], wbb_ref[...])
        for g in range(len(POOL_WINDOWS)):
            lanes = slice(POOL_GROUP * g, POOL_GROUP * (g + 1))
            dpl = dpooled[:, lanes]
            dps_ref[:, lanes] += _colsum(dpl * _mm(d_ref[:, lanes], wp_ref[g]))
            dyp = (dpl * ps_ref[:, lanes]).astype(BF16)
            dyp_ref[:, lanes] = dyp
            dd_ref[:, lanes] = _mm_nt(dyp, wp_ref[g])

    return pl.pallas_call(
        body, name="merge_bwd", grid=(seq // tm,),
        in_specs=[_row(tm, D_MODEL), _row(tm, D_MODEL), _row(tm, 2 * D_MODEL), _row(tm, D_MODEL), _row(tm, D_MODEL),
                  _row(tm, POOL_WIDTH), _fix((1, D_MODEL)), _fix((1, POOL_WIDTH)),
                  _fix(w_out.shape), _fix(w_ba.shape), _fix(w_bb.shape), _fix(w_pool_bf.shape)],
        out_specs=[_row(tm, D_MODEL), _row(tm, D_MODEL), _row(tm, D_MODEL), _row(tm, 2 * D_MODEL),
                   pl.BlockSpec((N_HEADS, tm, HEAD_PAD), lambda i: (0, i, 0)), _row(tm, POOL_WIDTH), _row(tm, POOL_WIDTH),
                   _fix((1, D_MODEL)), _fix((1, 2 * D_MODEL)), _fix((1, POOL_WIDTH))],
        out_shape=[_sds((seq, D_MODEL), BF16), _sds((seq, D_MODEL), BF16), _sds((seq, D_MODEL), BF16),
                   _sds((seq, 2 * D_MODEL), BF16), _sds((N_HEADS, seq, HEAD_PAD), BF16), _sds((seq, POOL_WIDTH), BF16),
                   _sds((seq, POOL_WIDTH), F32), _sds((1, D_MODEL), F32), _sds((1, 2 * D_MODEL), F32),
                   _sds((1, POOL_WIDTH), F32)],
        compiler_params=_params(1, 48),
    )(dh1, y, gates, ba, bb, d_pool, g_post_mix, pool_scale, w_out, w_ba, w_bb, w_pool_bf)


def _proj_bwd(dq, dk, dv, qd, kvd, x, dh1, dgpre, dpin, cf, s1, s2, g_pre, g_q, g_kv,
              w_in_r, w_uq_r, w_k_exp, w_v, tm):
    seq = x.shape[0]

    def body(dq_ref, dk_ref, dv_ref, qd_ref, kvd_ref, x_ref, dh1_ref, dgpre_ref, dpin_ref, cf_ref, s1_ref, s2_ref,
             gpre_ref, gq_ref, gkv_ref, win_hbm, wuq_hbm, wk_hbm, wv_hbm,
             gx_ref, dproj_ref, dqb_ref, qn_ref, kvn_ref, dkb_ref, dvb_ref, dgpre_acc, dgq_acc, dgkv_acc,
             win, wuq, wk, wv):
        _load_once(((win_hbm, win), (wuq_hbm, wuq), (wk_hbm, wk), (wv_hbm, wv)))

        @pl.when(pl.program_id(0) == 0)
        def _():
            dgpre_acc[...] = jnp.zeros_like(dgpre_acc)
            dgq_acc[...] = jnp.zeros_like(dgq_acc)
            dgkv_acc[...] = jnp.zeros_like(dgkv_acc)

        cfv, s1v, s2v = cf_ref[...], s1_ref[...], s2_ref[...]
        ksum = jnp.zeros((tm, HEAD_PAD), F32)
        for h in range(N_HEADS):
            lanes = slice(HEAD_PAD * h, HEAD_PAD * (h + 1))
            dqb_ref[:, lanes] = (_rope_t(jnp.transpose(dq_ref[h]), cfv, s1v, s2v) * ATTN_SCALE).astype(BF16)
            dkh = dk_ref[h]
            dkb_ref[:, lanes] = dkh
            dvb_ref[:, lanes] = dv_ref[h]
            ksum = ksum + dkh.astype(F32)
        lane = lax.broadcasted_iota(jnp.int32, (tm, HEAD_PAD), 1)
        rope_lanes = (lane >= QK_NOPE) & (lane < QK_NOPE + QK_ROPE)
        dkr = _rope_t(jnp.where(rope_lanes, ksum, 0.0), cfv, s1v, s2v)

        qdv = qd_ref[...]
        rq = _rms_r(qdv)
        qhat = qdv * rq
        qn_ref[...] = (qhat * gq_ref[...]).astype(BF16)
        dqn = _mm_nt(dqb_ref[...], wuq[...])
        dgq_acc[...] += _colsum(dqn * qhat)
        dproj_ref[:, IN_Q0:IN_KV0] = _rms_bwd(qhat, rq, gq_ref[...], dqn).astype(BF16)

        kvdv = kvd_ref[...]
        rkv = _rms_r(kvdv)
        kvhat = kvdv * rkv
        kvn_ref[...] = (kvhat * gkv_ref[...]).astype(BF16)
        dkvn = _mm_nt(dkb_ref[...], wk[...]) + _mm_nt(dvb_ref[...], wv[...])
        dgkv_acc[...] += _colsum(dkvn * kvhat)
        dproj_ref[:, IN_KV0:IN_POOL0] = _rms_bwd(kvhat, rkv, gkv_ref[...], dkvn).astype(BF16)

        dproj_ref[:, IN_POOL0:IN_GATE0] = dpin_ref[...]
        dproj_ref[:, IN_GATE0:IN_KR0] = dgpre_ref[...]
        dproj_ref[:, IN_KR0:IN_R] = dkr.astype(BF16)

        da = _mm_nt(dproj_ref[...], win[...])
        xv = x_ref[...]
        r0 = _rms_r(xv)
        xhat = xv * r0
        dgpre_acc[...] += _colsum(da * xhat)
        gx_ref[...] = dh1_ref[...] + _rms_bwd(xhat, r0, gpre_ref[...], da)

    per_tile = ATTN_TILE // tm
    heads = pl.BlockSpec((N_HEADS, tm, HEAD_PAD), lambda i: (0, i, 0))
    heads_t = pl.BlockSpec((N_HEADS, None, HEAD_PAD, tm), lambda i: (0, i // per_tile, 0, i % per_tile))
    return pl.pallas_call(
        body, name="proj_bwd", grid=(seq // tm,),
        in_specs=[heads_t, heads, heads, _row(tm, Q_LORA), _row(tm, KV_LORA), _row(tm, D_MODEL),
                  _row(tm, D_MODEL), _row(tm, 2 * D_MODEL), _row(tm, POOL_WIDTH),
                  _row(tm, HEAD_PAD), _row(tm, HEAD_PAD), _row(tm, HEAD_PAD),
                  _fix((1, D_MODEL)), _fix((1, Q_LORA)), _fix((1, KV_LORA)), ANY, ANY, ANY, ANY],
        out_specs=[_row(tm, D_MODEL), _row(tm, IN_R), _row(tm, N_HEADS * HEAD_PAD), _row(tm, Q_LORA), _row(tm, KV_LORA),
                   _row(tm, N_HEADS * HEAD_PAD), _row(tm, N_HEADS * HEAD_PAD),
                   _fix((1, D_MODEL)), _fix((1, Q_LORA)), _fix((1, KV_LORA))],
        out_shape=[_sds((seq, D_MODEL), F32), _sds((seq, IN_R), BF16), _sds((seq, N_HEADS * HEAD_PAD), BF16),
                   _sds((seq, Q_LORA), BF16), _sds((seq, KV_LORA), BF16), _sds((seq, N_HEADS * HEAD_PAD), BF16),
                   _sds((seq, N_HEADS * HEAD_PAD), BF16),
                   _sds((1, D_MODEL), F32), _sds((1, Q_LORA), F32), _sds((1, KV_LORA), F32)],
        scratch_shapes=[pltpu.VMEM(w_in_r.shape, BF16), pltpu.VMEM(w_uq_r.shape, BF16),
                        pltpu.VMEM(w_k_exp.shape, BF16), pltpu.VMEM(w_v.shape, BF16)],
        compiler_params=_params(1, 48),
    )(dq, dk, dv, qd, kvd, x, dh1, dgpre, dpin, cf, s1, s2, g_pre, g_q, g_kv, w_in_r, w_uq_r, w_k_exp, w_v)


def _grad_w(a, b, name, square_a=False, by_device=False):
    seq, k_dim = a.shape
    n_dim = b.shape[1]
    tk = min(k_dim, 1024)
    tn = n_dim // 2 if n_dim == IN_R else min(n_dim, 1024)
    if by_device:
        tn = n_dim // N_DEV
    ts = min(seq, 1024)
    if by_device:
        out_spec = pl.BlockSpec((None, tk, tn), lambda i, j, s: (j, i, 0))
        out_shape = _sds((N_DEV, k_dim, tn), F32)
    else:
        out_spec = pl.BlockSpec((tk, tn), lambda i, j, s: (i, j))
        out_shape = _sds((k_dim, n_dim), F32)

    def body(a_ref, b_ref, o_ref):
        @pl.when(pl.program_id(2) == 0)
        def _():
            o_ref[...] = jnp.zeros_like(o_ref)

        at = a_ref[...]
        if square_a:
            at = at * at
        o_ref[...] += lax.dot_general(at, b_ref[...], TN, preferred_element_type=F32)

    return pl.pallas_call(
        body, name=name, grid=(k_dim // tk, n_dim // tn, seq // ts),
        in_specs=[pl.BlockSpec((ts, tk), lambda i, j, s: (s, i)), pl.BlockSpec((ts, tn), lambda i, j, s: (s, j))],
        out_specs=out_spec, out_shape=out_shape,
        compiler_params=_params(3, 48),
    )(a, b)


def _grad_w_heads(a, b, name):
    heads, seq, _ = a.shape
    n_dim = b.shape[1]
    ts = min(seq, 1024)

    def body(a_ref, b_ref, o_ref):
        @pl.when(pl.program_id(1) == 0)
        def _():
            o_ref[...] = jnp.zeros_like(o_ref)

        o_ref[...] += lax.dot_general(a_ref[...], b_ref[...], TN, preferred_element_type=F32)

    return pl.pallas_call(
        body, name=name, grid=(heads, seq // ts),
        in_specs=[pl.BlockSpec((None, ts, HEAD_PAD), lambda h, s: (h, s, 0)), pl.BlockSpec((ts, n_dim), lambda h, s: (s, 0))],
        out_specs=pl.BlockSpec((HEAD_PAD, n_dim), lambda h, s: (h, 0)),
        out_shape=_sds((heads * HEAD_PAD, n_dim), F32),
        compiler_params=_params(2, 32),
    )(a, b)


def _grad_w_pool(d, dyp):
    seq = d.shape[0]
    ts = min(seq, 1024)

    def body(a_ref, b_ref, o_ref):
        @pl.when(pl.program_id(1) == 0)
        def _():
            o_ref[...] = jnp.zeros_like(o_ref)

        o_ref[...] += lax.dot_general(a_ref[...], b_ref[...], TN, preferred_element_type=F32)

    tile = pl.BlockSpec((ts, POOL_GROUP), lambda g, s: (s, g))
    return pl.pallas_call(
        body, name="grad_w_pool", grid=(len(POOL_WINDOWS), seq // ts),
        in_specs=[tile, tile],
        out_specs=pl.BlockSpec((None, POOL_GROUP, POOL_GROUP), lambda g, s: (g, 0, 0)),
        out_shape=_sds((len(POOL_WINDOWS), POOL_GROUP, POOL_GROUP), F32),
        compiler_params=_params(2, 32),
    )(d, dyp)


def _position():
    return lax.axis_index("x"), lax.axis_index("y"), lax.axis_index("c")


def _gather_copies(x_ref, slot, send_sems, recv_sems, local_sem):
    x, y, c = _position()
    me, sibling = (x, y, c), (x, y, 1 - c)
    chips = [(1 - x, y), (x, 1 - y), (1 - x, 1 - y)]

    def copy(k, block, to, src=None):
        return pltpu.make_async_remote_copy(
            src_ref=slot(*block) if src is None else src, dst_ref=slot(*block),
            send_sem=send_sems.at[k], recv_sem=recv_sems.at[k], device_id=to, device_id_type=MESH)

    mine = pltpu.make_async_copy(x_ref, slot(*me), local_sem)
    mine.start()
    first = [copy(0, me, sibling, src=x_ref)]
    first += [copy(1 + j, me, (*chip, c), src=x_ref) for j, chip in enumerate(chips)]
    for cp in first:
        cp.start()
    passed = [copy(4 + j, (*chip, c), sibling) for j, chip in enumerate(chips)]
    for j, chip in enumerate(chips):
        copy(1 + j, (*chip, c), me).wait_recv()
        passed[j].start()
    copy(0, sibling, me).wait_recv()
    for j, chip in enumerate(chips):
        copy(4 + j, (*chip, 1 - c), me).wait_recv()
    for cp in first + passed:
        cp.wait_send()
    mine.wait()


def _all_gather_hbm(block):
    def body(x_ref, out_ref, send_sems, recv_sems, local_sem):
        _gather_copies(x_ref, lambda px, py, pc: out_ref.at[4 * px + 2 * py + pc], send_sems, recv_sems, local_sem)

    return pl.pallas_call(
        body, name="gather_weights",
        in_specs=[ANY], out_specs=ANY,
        out_shape=_sds((N_DEV,) + block.shape, block.dtype),
        scratch_shapes=[pltpu.SemaphoreType.DMA((7,)), pltpu.SemaphoreType.DMA((7,)), pltpu.SemaphoreType.DMA],
    )(block)


def _all_reduce_small(block):
    def body(x_ref, out_ref, buf, send_sems, recv_sems, local_sem):
        _gather_copies(x_ref, lambda px, py, pc: buf.at[4 * px + 2 * py + pc], send_sems, recv_sems, local_sem)
        acc = buf[0]
        for k in range(1, N_DEV):
            acc = acc + buf[k]
        out_ref[...] = acc

    vmem = pl.BlockSpec(memory_space=pltpu.VMEM)
    return pl.pallas_call(
        body, name="all_reduce_small",
        in_specs=[vmem], out_specs=vmem,
        out_shape=_sds(block.shape, F32),
        scratch_shapes=[pltpu.VMEM((N_DEV,) + block.shape, F32), pltpu.SemaphoreType.DMA((7,)),
                        pltpu.SemaphoreType.DMA((7,)), pltpu.SemaphoreType.DMA],
        compiler_params=pltpu.CompilerParams(vmem_limit_bytes=32 * MIB),
    )(block)


def _exchange_pair(gs):
    n_w = len(gs)

    def body(*refs):
        g_refs, out_refs = refs[:n_w], refs[n_w:2 * n_w]
        send_sems, recv_sems = refs[2 * n_w:]
        x, y, c = _position()
        copies = []
        for w in range(n_w):
            for chip in range(4):
                cp = pltpu.make_async_remote_copy(
                    src_ref=g_refs[w].at[2 * chip + (1 - c)], dst_ref=out_refs[w].at[chip],
                    send_sem=send_sems.at[4 * w + chip], recv_sem=recv_sems.at[4 * w + chip],
                    device_id=(x, y, 1 - c), device_id_type=MESH)
                cp.start()
                copies.append(cp)
        for cp in copies:
            cp.wait_recv()
        for cp in copies:
            cp.wait_send()

    return pl.pallas_call(
        body, name="exchange_pair",
        in_specs=[ANY] * n_w, out_specs=[ANY] * n_w,
        out_shape=[_sds((4,) + g.shape[1:], g.dtype) for g in gs],
        scratch_shapes=[pltpu.SemaphoreType.DMA((4 * n_w,)), pltpu.SemaphoreType.DMA((4 * n_w,))],
    )(*gs)


def _exchange_chips(parts):
    n_w = len(parts)

    def body(*refs):
        p_refs, out_refs = refs[:n_w], refs[n_w:2 * n_w]
        send_sems, recv_sems = refs[2 * n_w:]
        x, y, c = _position()
        chips = [(1 - x, y), (x, 1 - y), (1 - x, 1 - y)]
        copies = []
        for w in range(n_w):
            for k, (px, py) in enumerate(chips):
                cp = pltpu.make_async_remote_copy(
                    src_ref=p_refs[w].at[2 * px + py], dst_ref=out_refs[w].at[k],
                    send_sem=send_sems.at[3 * w + k], recv_sem=recv_sems.at[3 * w + k],
                    device_id=(px, py, c), device_id_type=MESH)
                cp.start()
                copies.append(cp)
        for cp in copies:
            cp.wait_recv()
        for cp in copies:
            cp.wait_send()

    return pl.pallas_call(
        body, name="exchange_chips",
        in_specs=[ANY] * n_w, out_specs=[ANY] * n_w,
        out_shape=[_sds((3,) + p.shape[1:], p.dtype) for p in parts],
        scratch_shapes=[pltpu.SemaphoreType.DMA((3 * n_w,)), pltpu.SemaphoreType.DMA((3 * n_w,))],
    )(*parts)


def _row_tile(k):
    return 256 if k % 256 == 0 else 128


def _pair_sum(g, recv, place, name):
    _, k, n = g.shape
    tr = _row_tile(k)
    g4 = g.reshape(4, 2, k, n)

    def body(s_ref, g_ref, r_ref, o_ref):
        o_ref[...] = (g_ref[...] + r_ref[...]).astype(BF16)

    spec = pltpu.PrefetchScalarGridSpec(
        num_scalar_prefetch=1, grid=(4, k // tr),
        in_specs=[pl.BlockSpec((None, None, tr, n), lambda j, i, s: (j, s[2], i, 0)),
                  pl.BlockSpec((None, tr, n), lambda j, i, s: (j, i, 0))],
        out_specs=pl.BlockSpec((None, tr, n), lambda j, i, s: (j, i, 0)))
    return pl.pallas_call(
        body, name=name, grid_spec=spec, out_shape=_sds((4, k, n), BF16),
        compiler_params=_params(2, 32),
    )(place, g4, recv)


def _adamw_math(g, w, m, v):
    m = ADAM_B1 * m + (1.0 - ADAM_B1) * g
    v = ADAM_B2 * v + (1.0 - ADAM_B2) * (g * g)
    m_hat = m / (1.0 - ADAM_B1 ** ADAM_STEP)
    v_hat = v / (1.0 - ADAM_B2 ** ADAM_STEP)
    delta = -ADAM_LR * (m_hat / (jnp.sqrt(v_hat) + ADAM_EPS) + ADAM_WD * w)
    return delta, m, v


def _adamw_sharded(g, from_sibling, from_chips, place, w, m, v, name):
    _, k, n = g.shape
    tr = _row_tile(k)

    def body(s_ref, g_ref, sib_ref, r0_ref, r1_ref, r2_ref, w_ref, m_ref, v_ref, grad_ref, d_ref, nm_ref, nv_ref):
        grad = g_ref[...] + sib_ref[...]
        for r_ref in (r0_ref, r1_ref, r2_ref):
            grad = grad + r_ref[...].astype(F32)
        grad_ref[...] = grad
        d_ref[...], nm_ref[...], nv_ref[...] = _adamw_math(grad, w_ref[...], m_ref[...], v_ref[...])

    tile = pl.BlockSpec((None, tr, n), lambda i, s: (0, i, 0))

    def slot(j):
        return pl.BlockSpec((None, tr, n), lambda i, s: (j, i, 0))

    spec = pltpu.PrefetchScalarGridSpec(
        num_scalar_prefetch=1, grid=(k // tr,),
        in_specs=[pl.BlockSpec((None, tr, n), lambda i, s: (s[0], i, 0)),
                  pl.BlockSpec((None, tr, n), lambda i, s: (s[1], i, 0)),
                  slot(0), slot(1), slot(2), tile, tile, tile],
        out_specs=[tile] * 4)
    return pl.pallas_call(
        body, name=name, grid_spec=spec, out_shape=[_sds((1, k, n), F32)] * 4,
        compiler_params=_params(1, 48),
    )(place, g, from_sibling, from_chips, from_chips, from_chips, w, m, v)


def _adamw_small(g, w, m, v):
    def body(g_ref, w_ref, m_ref, v_ref, d_ref, nm_ref, nv_ref):
        d_ref[...], nm_ref[...], nv_ref[...] = _adamw_math(g_ref[...], w_ref[...], m_ref[...], v_ref[...])

    return pl.pallas_call(body, name="adamw_small", out_shape=[_sds(g.shape, F32)] * 3)(g, w, m, v)


def _pack_rows(parts):
    parts = [a.reshape(-1, LANES) for a in parts]
    pad = (-sum(a.shape[0] for a in parts)) % PACK_ROW_TILE
    return jnp.concatenate(parts + [jnp.zeros((pad, LANES), parts[0].dtype)], axis=0)


def _pack_small(parts):
    flat = jnp.concatenate([a.reshape(-1) for a in parts])
    pad = (-flat.shape[0]) % (8 * LANES)
    return jnp.pad(flat, (0, pad)).reshape(-1, LANES)


def _unpack_small(packed, shapes):
    flat = packed.reshape(-1)
    out, off = [], 0
    for shape in shapes:
        size = 1
        for n in shape:
            size *= n
        out.append(flat[off:off + size].reshape(shape))
        off += size
    return out


def _full_from_gathered(gathered, shard_shapes):
    out, off = {}, 0
    for (name, kind), (k, n) in zip(SHARDED, shard_shapes):
        rows = k * n // LANES
        seg = gathered[:, off:off + rows].reshape(N_DEV, k, n)
        out[name] = jnp.transpose(seg, (1, 0, 2)).reshape(k, N_DEV * n) if kind == "col" else seg.reshape(N_DEV * k, n)
        off += rows
    return out


def _columns_by_device(a):
    k, n_all = a.shape
    return jnp.transpose(a.reshape(k, N_DEV, n_all // N_DEV), (1, 0, 2))


def _rows_by_device(a):
    k_all, n = a.shape
    return a.reshape(N_DEV, k_all // N_DEV, n)


def _rope_tables(positions):
    inv_freq = ROPE_THETA ** (-jnp.arange(0, QK_ROPE, 2, dtype=F32) / QK_ROPE)
    ang = positions.astype(F32)[:, None] * inv_freq
    cos, sin = jnp.cos(ang), jnp.sin(ang)
    seq = positions.shape[0]
    zeros = lambda n: jnp.zeros((seq, n), F32)
    cf = jnp.concatenate([jnp.ones((seq, QK_NOPE), F32), cos, cos, zeros(HEAD_PAD - QK_NOPE - QK_ROPE)], axis=1)
    s1 = jnp.concatenate([zeros(QK_NOPE), -sin, zeros(HEAD_PAD - QK_NOPE - QK_ROPE // 2)], axis=1)
    s2 = jnp.concatenate([zeros(QK_NOPE + QK_ROPE // 2), sin, zeros(HEAD_PAD - QK_NOPE - QK_ROPE)], axis=1)
    return cf, s1, s2


def _arrange_w_in(w):
    k = w.shape[0]
    zeros = lambda n: jnp.zeros((k, n), w.dtype)
    kr0 = Q_LORA + KV_LORA
    pool0 = kr0 + QK_ROPE
    return jnp.concatenate([w[:, :kr0], w[:, pool0:], zeros(QK_NOPE), w[:, kr0:pool0],
                            zeros(HEAD_PAD - QK_NOPE - QK_ROPE)], axis=1)


def _restore_w_in(d):
    kr = d[:, IN_KR0 + QK_NOPE:IN_KR0 + QK_NOPE + QK_ROPE]
    return jnp.concatenate([d[:, :IN_POOL0], kr, d[:, IN_POOL0:IN_KR0]], axis=1)


def _pad_heads(w, width):
    k = w.shape[0]
    w = w.reshape(k, N_HEADS, width)
    return jnp.pad(w, ((0, 0), (0, 0), (0, HEAD_PAD - width))).reshape(k, N_HEADS * HEAD_PAD)


def _unpad_heads(d, width):
    k = d.shape[0]
    return d.reshape(k, N_HEADS, HEAD_PAD)[:, :, :width]


def kernel(x, p, positions, g_pre_mix, w_in, b_gate, g_q, w_uq, g_kv, w_ukv, w_pool, pool_scale, w_branch_attn, w_branch_pool, w_out, g_post_mix, g_pre_mlp, w_ff1, w_ff2, g_post_mlp, w_ple_proj, w_ple_gate, g_ple, loss_target, m_g_pre_mix, m_w_in, m_b_gate, m_g_q, m_w_uq, m_g_kv, m_w_ukv, m_w_pool, m_pool_scale, m_w_branch_attn, m_w_branch_pool, m_w_out, m_g_post_mix, m_g_pre_mlp, m_w_ff1, m_w_ff2, m_g_post_mlp, m_w_ple_proj, m_w_ple_gate, m_g_ple, v_g_pre_mix, v_w_in, v_b_gate, v_g_q, v_w_uq, v_g_kv, v_w_ukv, v_w_pool, v_pool_scale, v_w_branch_attn, v_w_branch_pool, v_w_out, v_g_post_mix, v_g_pre_mlp, v_w_ff1, v_w_ff2, v_g_post_mlp, v_w_ple_proj, v_w_ple_gate, v_g_ple):
    given = dict(locals())
    weights = {n: given[n] for n in WEIGHT_ORDER}
    moments_m = {n: given["m_" + n] for n in WEIGHT_ORDER}
    moments_v = {n: given["v_" + n] for n in WEIGHT_ORDER}
    xs, ps, target = x[0], p[0, 0], loss_target[0]
    seq = xs.shape[0]
    tm = min(256, seq)
    core = lax.axis_index("c")
    chip = 2 * lax.axis_index("x") + lax.axis_index("y")

    shard_shapes = [weights[n].shape[1:] for n, _ in SHARDED]
    gathered = _all_gather_hbm(_pack_rows([weights[n][0] for n, _ in SHARDED]).astype(BF16))
    full = _full_from_gathered(gathered, shard_shapes)
    w_in_r = _arrange_w_in(full["w_in"])
    w_uq_r = _pad_heads(full["w_uq"], QK_NOPE + QK_ROPE)
    ukv = full["w_ukv"].reshape(KV_LORA, N_HEADS, QK_NOPE + V_HEAD)
    w_k_exp = _pad_heads(ukv[:, :, :QK_NOPE].reshape(KV_LORA, N_HEADS * QK_NOPE), QK_NOPE)
    w_v = _pad_heads(ukv[:, :, QK_NOPE:].reshape(KV_LORA, N_HEADS * V_HEAD), V_HEAD)
    w_ba = jnp.pad(full["w_branch_attn"].reshape(N_HEADS, V_HEAD, D_MODEL),
                   ((0, 0), (0, HEAD_PAD - V_HEAD), (0, 0))).reshape(N_HEADS * HEAD_PAD, D_MODEL)
    w_pool_bf = w_pool[0].astype(BF16)
    cf, s1, s2 = _rope_tables(positions[0])

    a_bf, qd, kvd, pin, gates, q, k, v, k_t = _proj_fwd(xs, g_pre_mix, b_gate, g_q, g_kv, cf, s1, s2,
                                                             w_in_r, w_uq_r, w_k_exp, w_v, tm)
    d_pool, pooled = _pool_fwd(pin, w_pool_bf, pool_scale, tm)
    o_heads, lse = _attn_fwd(q, k, v)
    merged, ba, bb, y, h1 = _merge_fwd(o_heads, pooled, gates, xs, g_post_mix, w_ba,
                                       full["w_branch_pool"], full["w_out"], tm)
    (m_bf, zr, f, h2_bf, p_bf, de, dpre, dh2, loss_acc, dg_ple) = _tail_fwd(
        h1, target, ps, g_pre_mlp, g_post_mlp, g_ple, full["w_ff1"], full["w_ff2"], full["w_ple_proj"],
        full["w_ple_gate"], tm)

    df, dz, dh1, dg_pre_mlp, dg_post_mlp = _mlp_bwd(h1, f, zr, dh2, g_pre_mlp, g_post_mlp, full["w_ff1"],
                                                    full["w_ff2"], tm)
    (dy, dba, dbb, dgpre, do_heads, dyp, dd, dg_post_mix, db_gate, dpool_scale) = _merge_bwd(
        dh1, y, gates, ba, bb, d_pool, g_post_mix, pool_scale, full["w_out"], w_ba,
        full["w_branch_pool"], w_pool_bf, tm)
    dpin = _pool_bwd_window(dd, tm)
    delta = _attn_delta(o_heads, do_heads)
    dq, dk, dv = _attn_bwd(q, k, k_t, v, do_heads, lse, delta)
    (grad_x, dproj, dq_bf, qn_bf, kvn_bf, dk_bf, dv_bf, dg_pre_mix, dg_q, dg_kv) = _proj_bwd(
        dq, dk, dv, qd, kvd, xs, dh1, dgpre, dpin, cf, s1, s2, g_pre_mix, g_q, g_kv, w_in_r, w_uq_r, w_k_exp, w_v, tm)

    d_k_exp = _unpad_heads(_grad_w(kvn_bf, dk_bf, "grad_w_uk"), QK_NOPE)
    d_w_v = _unpad_heads(_grad_w(kvn_bf, dv_bf, "grad_w_uv"), V_HEAD)
    d_w_ba = _grad_w_heads(o_heads, dba, "grad_w_branch_attn").reshape(N_HEADS, HEAD_PAD, D_MODEL)
    by_device = {
        "w_in": _columns_by_device(_restore_w_in(_grad_w(a_bf, dproj, "grad_w_in"))),
        "w_uq": _columns_by_device(
            _unpad_heads(_grad_w(qn_bf, dq_bf, "grad_w_uq"), QK_NOPE + QK_ROPE).reshape(Q_LORA, -1)),
        "w_ukv": _columns_by_device(jnp.concatenate([d_k_exp, d_w_v], axis=2).reshape(KV_LORA, -1)),
        "w_branch_attn": _columns_by_device(d_w_ba[:, :V_HEAD].reshape(N_HEADS * V_HEAD, D_MODEL)),
        "w_branch_pool": _grad_w(pooled, dbb, "grad_w_branch_pool", by_device=True),
        "w_out": _rows_by_device(_grad_w(merged, dy, "grad_w_out")),
        "w_ff1": _grad_w(m_bf, dz, "grad_w_ff1", by_device=True),
        "w_ff2": _rows_by_device(_grad_w(zr, df, "grad_w_ff2", square_a=True)),
        "w_ple_proj": _grad_w(p_bf, de, "grad_w_ple_proj", by_device=True),
        "w_ple_gate": _rows_by_device(_grad_w(h2_bf, dpre, "grad_w_ple_gate")),
    }
    grads_small = {
        "g_pre_mix": dg_pre_mix, "b_gate": db_gate, "g_q": dg_q, "g_kv": dg_kv,
        "w_pool": _grad_w_pool(d_pool, dyp), "pool_scale": dpool_scale, "g_post_mix": dg_post_mix,
        "g_pre_mlp": dg_pre_mlp, "g_post_mlp": dg_post_mlp, "g_ple": dg_ple,
    }

    names = [n for n, _ in SHARDED]
    place = jnp.stack([2 * chip + core, chip, core]).astype(jnp.int32)
    own = [by_device[n] for n in names]
    from_sibling = _exchange_pair(own)
    pair = [_pair_sum(g, r, place, "pair_sum_" + n) for n, g, r in zip(names, own, from_sibling)]
    from_chips = _exchange_chips(pair)
    sharded = {n: _adamw_sharded(g, r, rc, place, weights[n], moments_m[n], moments_v[n], "adamw_" + n)
               for n, g, r, rc in zip(names, own, from_sibling, from_chips)}

    g_sm = _all_reduce_small(_pack_small([grads_small[n] for n in REPLICATED] + [loss_acc[0:1, 0:1]]))
    n_small = sum(weights[n].size for n in REPLICATED)
    d_sm, m_sm, v_sm = _adamw_small(g_sm, _pack_small([weights[n] for n in REPLICATED]),
                                    _pack_small([moments_m[n] for n in REPLICATED]),
                                    _pack_small([moments_v[n] for n in REPLICATED]))

    small_shapes = [weights[n].shape for n in REPLICATED]
    results = []
    for which, small in enumerate((g_sm, d_sm, m_sm, v_sm)):
        named = {n: sharded[n][which] for n in names}
        named.update(zip(REPLICATED, _unpack_small(small, small_shapes)))
        results.append([named[n] for n in WEIGHT_ORDER])

    loss = g_sm.reshape(-1)[n_small]
    return (loss, grad_x[None], *results[0], *results[1], *results[2], *results[3])
```

```python
import jax
import jax.numpy as jnp
from jax import lax
from jax.experimental import pallas as pl
from jax.experimental.pallas import tpu as pltpu

F32 = jnp.float32
BF16 = jnp.bfloat16

D_MODEL = 1024
PLE_DIM = 256
N_HEADS = 8
QK_NOPE = 64
QK_ROPE = 32
V_HEAD = 64
Q_LORA = 384
KV_LORA = 256
POOL_WINDOWS = (2, 4, 8, 16)
POOL_GROUP = 128
POOL_WIDTH = 512
D_FF = 4096
ROPE_THETA = 10000.0
EPS = 1e-6
HEAD_PAD = 128
ATTN_SCALE = (QK_NOPE + QK_ROPE) ** -0.5
LOG2E = 1.4426950408889634
Q_PRESCALE = ATTN_SCALE * LOG2E
ATTN_TILE = 512
FWD_ROWS = 512
FWD_CHAINS = 2

ADAM_LR = 0.001
ADAM_B1 = 0.9
ADAM_B2 = 0.999
ADAM_EPS = 1e-08
ADAM_WD = 0.01
ADAM_STEP = 10

N_DEV = 8
LANES = 1024
PACK_ROW_TILE = 480
POOL_HALO = 16
MIB = 2 ** 20

IN_Q0, IN_KV0, IN_POOL0, IN_GATE0, IN_KR0, IN_R = 0, 384, 640, 1152, 3200, 3328

SHARDED = (("w_in", "col"), ("w_uq", "col"), ("w_ukv", "col"), ("w_branch_attn", "col"),
           ("w_branch_pool", "col"), ("w_out", "row"), ("w_ff1", "col"), ("w_ff2", "row"),
           ("w_ple_proj", "col"), ("w_ple_gate", "row"))
N_EARLY = 3
REPLICATED = ("g_pre_mix", "b_gate", "g_q", "g_kv", "w_pool", "pool_scale", "g_post_mix",
              "g_pre_mlp", "g_post_mlp", "g_ple")
WEIGHT_ORDER = ("g_pre_mix", "w_in", "b_gate", "g_q", "w_uq", "g_kv", "w_ukv", "w_pool", "pool_scale",
                "w_branch_attn", "w_branch_pool", "w_out", "g_post_mix", "g_pre_mlp", "w_ff1", "w_ff2",
                "g_post_mlp", "w_ple_proj", "w_ple_gate", "g_ple")

NT = (((1,), (1,)), ((), ()))
TN = (((0,), (0,)), ((), ()))
MESH = pl.DeviceIdType.MESH
ANY = pl.BlockSpec(memory_space=pl.ANY)


def _params(n_axes, vmem_mib):
    return pltpu.CompilerParams(dimension_semantics=("arbitrary",) * n_axes, vmem_limit_bytes=vmem_mib * MIB)


def _row(tm, n):
    return pl.BlockSpec((tm, n), lambda i: (i, 0))


def _fix(shape):
    zeros = (0,) * len(shape)
    return pl.BlockSpec(shape, lambda i: zeros)


def _sds(shape, dtype):
    return jax.ShapeDtypeStruct(shape, dtype)


def _rms_r(v):
    return lax.rsqrt(jnp.mean(v * v, axis=-1, keepdims=True) + EPS)


def _rms_bwd(vhat, r, g, dy):
    gdy = dy * g
    return r * (gdy - vhat * jnp.mean(gdy * vhat, axis=-1, keepdims=True))


def _colsum(v):
    return jnp.sum(v, axis=0, keepdims=True)


def _sigmoid(v):
    return 1.0 / (1.0 + jnp.exp(-v))


def _mm(a, b):
    return jnp.dot(a, b, preferred_element_type=F32)


def _mm_nt(a, b):
    return lax.dot_general(a, b, NT, preferred_element_type=F32)


def _rope(c, cf, s1, s2):
    return c * cf + pltpu.roll(c, HEAD_PAD - 16, 1) * s1 + pltpu.roll(c, 16, 1) * s2


def _rope_t(c, cf, s1, s2):
    return c * cf + pltpu.roll(c * s1, 16, 1) + pltpu.roll(c * s2, HEAD_PAD - 16, 1)


def _load_once(pairs):
    @pl.when(pl.program_id(0) == 0)
    def _():
        for src, dst in pairs:
            pltpu.sync_copy(src, dst)


def _proj_fwd(x, g_pre, b_gate, g_q, g_kv, cf, s1, s2, w_in_r, w_uq_r, w_k_exp, w_v, later_shards, tm):
    seq = x.shape[0]
    n_steps = seq // tm
    forward_step = (3 * n_steps) // 4

    def body(x_ref, gpre_ref, bg_ref, gq_ref, gkv_ref, cf_ref, s1_ref, s2_ref, win_hbm, wuq_hbm, wk_hbm, wv_hbm,
             later_ref, a_ref, qd_ref, kvd_ref, pin_ref, gates_ref, q_ref, k_ref, v_ref, kt_ref, gathered_ref,
             win, wuq, wk, wv, send_sems, recv_sems, local_sem):
        step = pl.program_id(0)

        def gather(phase):
            _gather_copies(later_ref, lambda px, py, pc: gathered_ref.at[4 * px + 2 * py + pc],
                           send_sems, recv_sems, local_sem, phases=(phase,))

        pl.when(step == 0)(lambda: gather("send"))
        pl.when(step == forward_step)(lambda: gather("forward"))
        _load_once(((win_hbm, win), (wuq_hbm, wuq), (wk_hbm, wk), (wv_hbm, wv)))
        xv = x_ref[...]
        a = (xv * _rms_r(xv) * gpre_ref[...]).astype(BF16)
        a_ref[...] = a
        proj = _mm(a, win[...])
        qd = proj[:, IN_Q0:IN_KV0]
        kvd = proj[:, IN_KV0:IN_POOL0]
        qd_ref[...] = qd
        kvd_ref[...] = kvd
        pin_ref[...] = proj[:, IN_POOL0:IN_GATE0]
        gates_ref[...] = _sigmoid(proj[:, IN_GATE0:IN_KR0] + bg_ref[...]).astype(BF16)
        cfv, s1v, s2v = cf_ref[...], s1_ref[...], s2_ref[...]
        krr = _rope(proj[:, IN_KR0:IN_R], cfv, s1v, s2v)
        qn = (qd * _rms_r(qd) * gq_ref[...]).astype(BF16)
        qf = _mm(qn, wuq[...])
        kvn = (kvd * _rms_r(kvd) * gkv_ref[...]).astype(BF16)
        kf = _mm(kvn, wk[...])
        vf = _mm(kvn, wv[...])
        one_lane = (lax.broadcasted_iota(jnp.int32, (tm, HEAD_PAD), 1) == V_HEAD).astype(F32)
        for h in range(N_HEADS):
            lanes = slice(HEAD_PAD * h, HEAD_PAD * (h + 1))
            q_ref[h] = (_rope(qf[:, lanes], cfv, s1v, s2v) * Q_PRESCALE).astype(BF16)
            kh = kf[:, lanes] + krr
            vh = vf[:, lanes] + one_lane
            k_ref[h] = kh.astype(BF16)
            v_ref[h] = vh.astype(BF16)
            kt_ref[h] = jnp.transpose(kh).astype(BF16)
        pl.when(step == n_steps - 1)(lambda: gather("finish"))

    per_tile = ATTN_TILE // tm
    heads = pl.BlockSpec((N_HEADS, tm, HEAD_PAD), lambda i: (0, i, 0))
    heads_t = pl.BlockSpec((N_HEADS, None, HEAD_PAD, tm), lambda i: (0, i // per_tile, 0, i % per_tile))
    heads_t_shape = _sds((N_HEADS, seq // ATTN_TILE, HEAD_PAD, ATTN_TILE), BF16)
    return pl.pallas_call(
        body, name="proj_fwd", grid=(seq // tm,),
        in_specs=[_row(tm, D_MODEL), _fix((1, D_MODEL)), _fix((1, 2 * D_MODEL)), _fix((1, Q_LORA)), _fix((1, KV_LORA)),
                  _row(tm, HEAD_PAD), _row(tm, HEAD_PAD), _row(tm, HEAD_PAD), ANY, ANY, ANY, ANY, ANY],
        out_specs=[_row(tm, D_MODEL), _row(tm, Q_LORA), _row(tm, KV_LORA), _row(tm, POOL_WIDTH), _row(tm, 2 * D_MODEL),
                   heads, heads, heads, heads_t, ANY],
        out_shape=[_sds((seq, D_MODEL), BF16), _sds((seq, Q_LORA), F32), _sds((seq, KV_LORA), F32),
                   _sds((seq, POOL_WIDTH), F32), _sds((seq, 2 * D_MODEL), BF16),
                   _sds((N_HEADS, seq, HEAD_PAD), BF16), _sds((N_HEADS, seq, HEAD_PAD), BF16),
                   _sds((N_HEADS, seq, HEAD_PAD), BF16), heads_t_shape,
                   _sds((N_DEV,) + later_shards.shape, later_shards.dtype)],
        scratch_shapes=[pltpu.VMEM(w_in_r.shape, BF16), pltpu.VMEM(w_uq_r.shape, BF16),
                        pltpu.VMEM(w_k_exp.shape, BF16), pltpu.VMEM(w_v.shape, BF16),
                        pltpu.SemaphoreType.DMA((7,)), pltpu.SemaphoreType.DMA((7,)), pltpu.SemaphoreType.DMA],
        compiler_params=_params(1, 48),
    )(x, g_pre, b_gate, g_q, g_kv, cf, s1, s2, w_in_r, w_uq_r, w_k_exp, w_v, later_shards)


def _window_count(row0, n_rows):
    t = row0 + lax.broadcasted_iota(jnp.int32, (n_rows, POOL_GROUP), 0)
    return [jnp.minimum(t + 1, w).astype(F32) for w in POOL_WINDOWS]


def _pool_fwd(pin, w_pool_bf, pool_scale, tm):
    seq = pin.shape[0]
    ext_rows = tm + POOL_HALO

    def body(prev_ref, u_ref, wp_ref, ps_ref, d_ref, pooled_ref):
        i = pl.program_id(0)
        prev = jnp.where(i == 0, 0.0, prev_ref[...])
        u = u_ref[...]
        level = jnp.concatenate([prev, u], axis=0)
        counts = _window_count(i * tm, tm)
        shift = 1
        for g in range(len(POOL_WINDOWS)):
            level = level + pltpu.roll(level, shift, 0)
            shift *= 2
            lanes = slice(POOL_GROUP * g, POOL_GROUP * (g + 1))
            d = (level[POOL_HALO:, lanes] / counts[g] - u[:, lanes]).astype(BF16)
            d_ref[:, lanes] = d
            pooled_ref[:, lanes] = (_mm(d, wp_ref[g]) * ps_ref[:, lanes]).astype(BF16)

    halo = tm // POOL_HALO
    return pl.pallas_call(
        body, name="pool_fwd", grid=(seq // tm,),
        in_specs=[pl.BlockSpec((POOL_HALO, POOL_WIDTH), lambda i: (jnp.maximum(i * halo - 1, 0), 0)),
                  _row(tm, POOL_WIDTH), _fix(w_pool_bf.shape), _fix((1, POOL_WIDTH))],
        out_specs=[_row(tm, POOL_WIDTH), _row(tm, POOL_WIDTH)],
        out_shape=[_sds((seq, POOL_WIDTH), BF16), _sds((seq, POOL_WIDTH), BF16)],
        compiler_params=_params(1, 32),
    )(pin, pin, w_pool_bf, pool_scale)


def _pool_bwd_window(dd, tm):
    seq = dd.shape[0]
    n_tiles = seq // tm
    ext_rows = tm + POOL_HALO

    def body(dd_ref, next_ref, dpin_ref):
        i = pl.program_id(0)
        nxt = jnp.where(i == n_tiles - 1, 0.0, next_ref[...])
        dd_t = dd_ref[...]
        ext = jnp.concatenate([dd_t, nxt], axis=0)
        counts = _window_count(i * tm, ext_rows)
        shift = 1
        for g in range(len(POOL_WINDOWS)):
            lanes = slice(POOL_GROUP * g, POOL_GROUP * (g + 1))
            level = ext[:, lanes] / counts[g]
            s = 1
            while s <= shift:
                level = level + pltpu.roll(level, ext_rows - s, 0)
                s *= 2
            shift *= 2
            dpin_ref[:, lanes] = (level[:tm] - dd_t[:, lanes]).astype(BF16)

    halo = tm // POOL_HALO
    return pl.pallas_call(
        body, name="pool_bwd_window", grid=(n_tiles,),
        in_specs=[_row(tm, POOL_WIDTH),
                  pl.BlockSpec((POOL_HALO, POOL_WIDTH), lambda i: (jnp.minimum((i + 1) * halo, seq // POOL_HALO - 1), 0))],
        out_specs=_row(tm, POOL_WIDTH),
        out_shape=_sds((seq, POOL_WIDTH), BF16),
        compiler_params=_params(1, 32),
    )(dd, dd)


def _col_to_row(col, n):
    return jnp.transpose(jnp.broadcast_to(col, (n, HEAD_PAD)))[0:1, :]


def _attn_fwd(q, k, v):
    heads, seq, _ = q.shape
    r, n = FWD_ROWS, FWD_CHAINS
    block = r * n

    def body(q_ref, k_ref, v_ref, o_ref, lse_ref):
        qi = pl.program_id(1)
        q_tiles = [q_ref[c * r:(c + 1) * r, :] for c in range(n)]

        def tile(qt, j, m, acc, diagonal):
            start = pl.multiple_of(j * r, r)
            s = _mm_nt(qt, k_ref[pl.ds(start, r), :])
            if diagonal:
                row = lax.broadcasted_iota(jnp.int32, (r, r), 0)
                col = lax.broadcasted_iota(jnp.int32, (r, r), 1)
                s = jnp.where(col <= row, s, -jnp.inf)
            m_new = jnp.maximum(m, jnp.max(s, axis=1, keepdims=True))
            p = jnp.exp2(s - m_new).astype(BF16)
            acc = jnp.exp2(m - m_new) * acc + _mm(p, v_ref[pl.ds(start, r), :])
            return m_new, acc

        def all_chains(jj, carry):
            for u in range(n):
                carry = tuple(tile(q_tiles[c], n * jj + u, *carry[c], False) for c in range(n))
            return carry

        init = tuple((jnp.full((r, 1), -jnp.inf, F32), jnp.zeros((r, HEAD_PAD), F32)) for _ in range(n))
        state = list(lax.fori_loop(0, qi, all_chains, init))
        for d in range(n):
            for c in range(d, n):
                state[c] = tile(q_tiles[c], n * qi + d, *state[c], c == d)
        for c, (m, acc) in enumerate(state):
            l = acc[:, V_HEAD:V_HEAD + 1]
            o_ref[c * r:(c + 1) * r, :] = (acc / l).astype(BF16)
            row0 = c * r
            lse_ref[row0 // ATTN_TILE, :, row0 % ATTN_TILE:row0 % ATTN_TILE + r] = _col_to_row(m + jnp.log2(l), r)

    return pl.pallas_call(
        body, name="attn_fwd", grid=(heads, seq // block),
        in_specs=[pl.BlockSpec((None, block, HEAD_PAD), lambda h, i: (h, i, 0)),
                  pl.BlockSpec((None, seq, HEAD_PAD), lambda h, i: (h, 0, 0)),
                  pl.BlockSpec((None, seq, HEAD_PAD), lambda h, i: (h, 0, 0))],
        out_specs=[pl.BlockSpec((None, block, HEAD_PAD), lambda h, i: (h, i, 0)),
                   pl.BlockSpec((None, block // ATTN_TILE, 1, ATTN_TILE), lambda h, i: (h, i, 0, 0))],
        out_shape=[_sds((heads, seq, HEAD_PAD), BF16), _sds((heads, seq // ATTN_TILE, 1, ATTN_TILE), F32)],
        compiler_params=_params(2, 48),
    )(q, k, v)


def _attn_delta(o, do):
    heads, seq, _ = o.shape
    tq = ATTN_TILE
    nq = seq // tq

    per_step = min(4, nq)

    def body(o_ref, do_ref, delta_ref):
        for u in range(per_step):
            rows = slice(u * tq, (u + 1) * tq)
            prod = o_ref[rows, :].astype(F32) * do_ref[rows, :].astype(F32)
            delta_ref[u] = _col_to_row(jnp.sum(prod, axis=1, keepdims=True), tq)

    tile = pl.BlockSpec((None, per_step * tq, HEAD_PAD), lambda h, i: (h, i, 0))
    return pl.pallas_call(
        body, name="attn_delta", grid=(heads, nq // per_step),
        in_specs=[tile, tile],
        out_specs=pl.BlockSpec((None, per_step, 1, tq), lambda h, i: (h, i, 0, 0)),
        out_shape=_sds((heads, nq, 1, tq), F32),
        compiler_params=_params(2, 32),
    )(o, do)


def _attn_bwd(q, k, k_t, v, do, lse, delta):
    heads, seq, _ = q.shape
    t = ATTN_TILE
    nq = seq // t

    def body(q_ref, k_ref, kt_ref, v_ref, do_ref, lse_ref, delta_ref, dq_ref, dk_ref, dv_ref):
        jp = pl.program_id(1)

        @pl.when(jp == 0)
        def _():
            dq_ref[...] = jnp.zeros_like(dq_ref)

        k_a, k_b = k_ref[0:t, :], k_ref[t:2 * t, :]
        v_a, v_b = v_ref[0:t, :], v_ref[t:2 * t, :]

        def tile(kt, k_tr, vt, i, dk, dv, diagonal):
            start = pl.multiple_of(i * t, t)
            qt = q_ref[pl.ds(start, t), :]
            dot = do_ref[pl.ds(start, t), :]
            p_t = jnp.exp2(_mm_nt(kt, qt) - lse_ref[i])
            if diagonal:
                key = lax.broadcasted_iota(jnp.int32, (t, t), 0)
                query = lax.broadcasted_iota(jnp.int32, (t, t), 1)
                p_t = jnp.where(key <= query, p_t, 0.0)
            dv = dv + _mm(p_t.astype(BF16), dot)
            ds_t = (p_t * (_mm_nt(vt, dot) - delta_ref[i])).astype(BF16)
            dk = dk + _mm(ds_t, qt)
            return dk, dv, _mm(k_tr, ds_t)

        def add_dq(i, dq):
            dq_ref[i] += dq

        def both(ip, carry):
            dk_a, dv_a, dk_b, dv_b = carry
            for i in (2 * ip, 2 * ip + 1):
                dk_a, dv_a, dq_a = tile(k_a, kt_ref[0], v_a, i, dk_a, dv_a, False)
                dk_b, dv_b, dq_b = tile(k_b, kt_ref[1], v_b, i, dk_b, dv_b, False)
                add_dq(i, dq_a + dq_b)
            return dk_a, dv_a, dk_b, dv_b

        zero = jnp.zeros((t, HEAD_PAD), F32)
        dk_a, dv_a, dq_a = tile(k_a, kt_ref[0], v_a, 2 * jp, zero, zero, True)
        add_dq(2 * jp, dq_a)
        dk_a, dv_a, dq_a = tile(k_a, kt_ref[0], v_a, 2 * jp + 1, dk_a, dv_a, False)
        dk_b, dv_b, dq_b = tile(k_b, kt_ref[1], v_b, 2 * jp + 1, zero, zero, True)
        add_dq(2 * jp + 1, dq_a + dq_b)
        dk_a, dv_a, dk_b, dv_b = lax.fori_loop(jp + 1, nq // 2, both, (dk_a, dv_a, dk_b, dv_b))
        dk_ref[0:t, :] = (dk_a * (1.0 / LOG2E)).astype(BF16)
        dk_ref[t:2 * t, :] = (dk_b * (1.0 / LOG2E)).astype(BF16)
        dv_ref[0:t, :] = dv_a.astype(BF16)
        dv_ref[t:2 * t, :] = dv_b.astype(BF16)

    whole = pl.BlockSpec((None, seq, HEAD_PAD), lambda h, j: (h, 0, 0))
    whole_t = pl.BlockSpec((None, nq, HEAD_PAD, t), lambda h, j: (h, 0, 0, 0))
    pair = pl.BlockSpec((None, 2 * t, HEAD_PAD), lambda h, j: (h, j, 0))
    pair_t = pl.BlockSpec((None, 2, HEAD_PAD, t), lambda h, j: (h, j, 0, 0))
    stats = pl.BlockSpec((None, nq, 1, t), lambda h, j: (h, 0, 0, 0))
    return pl.pallas_call(
        body, name="attn_bwd", grid=(heads, nq // 2),
        in_specs=[whole, pair, pair_t, pair, whole, stats, stats],
        out_specs=[whole_t, pair, pair],
        out_shape=[_sds((heads, nq, HEAD_PAD, t), F32), _sds((heads, seq, HEAD_PAD), BF16),
                   _sds((heads, seq, HEAD_PAD), BF16)],
        compiler_params=_params(2, 56),
    )(q, k, k_t, v, do, lse, delta)


def _merge_fwd(attn, pooled, gates, x, g_post_mix, w_ba, w_bb, w_out, tm):
    seq = x.shape[0]

    def body(attn_ref, pooled_ref, gates_ref, x_ref, g_ref, wba_ref, wbb_ref, wout_ref,
             merged_ref, ba_ref, bb_ref, y_ref, h1_ref):
        attn = jnp.concatenate([attn_ref[h] for h in range(N_HEADS)], axis=1)
        ba = _mm(attn, wba_ref[...])
        bb = _mm(pooled_ref[...], wbb_ref[...])
        ba_ref[...] = ba.astype(BF16)
        bb_ref[...] = bb.astype(BF16)
        merged = (gates_ref[:, :D_MODEL].astype(F32) * ba + gates_ref[:, D_MODEL:].astype(F32) * bb).astype(BF16)
        merged_ref[...] = merged
        y = _mm(merged, wout_ref[...])
        y_ref[...] = y
        h1_ref[...] = x_ref[...] + y * _rms_r(y) * g_ref[...]

    return pl.pallas_call(
        body, name="merge_fwd", grid=(seq // tm,),
        in_specs=[pl.BlockSpec((N_HEADS, tm, HEAD_PAD), lambda i: (0, i, 0)), _row(tm, POOL_WIDTH),
                  _row(tm, 2 * D_MODEL), _row(tm, D_MODEL),
                  _fix((1, D_MODEL)), _fix(w_ba.shape), _fix(w_bb.shape), _fix(w_out.shape)],
        out_specs=[_row(tm, D_MODEL)] * 5,
        out_shape=[_sds((seq, D_MODEL), BF16), _sds((seq, D_MODEL), BF16), _sds((seq, D_MODEL), BF16),
                   _sds((seq, D_MODEL), F32), _sds((seq, D_MODEL), F32)],
        compiler_params=_params(1, 48),
    )(attn, pooled, gates, x, g_post_mix, w_ba, w_bb, w_out)


def _tail_fwd(h1, target, p, g_pre_mlp, g_post_mlp, g_ple, w_ff1, w_ff2, w_pe, w_pg, tm):
    seq = h1.shape[0]

    def body(h1_ref, tgt_ref, p_ref, gm_ref, gf_ref, gp_ref, w1_hbm, w2_hbm, wpe_hbm, wpg_hbm,
             m_ref, zr_ref, f_ref, h2b_ref, pb_ref, de_ref, dpre_ref, dh2_ref, loss_ref, dgple_ref,
             w1, w2, wpe, wpg):
        _load_once(((w1_hbm, w1), (w2_hbm, w2), (wpe_hbm, wpe), (wpg_hbm, wpg)))

        @pl.when(pl.program_id(0) == 0)
        def _():
            loss_ref[...] = jnp.zeros_like(loss_ref)
            dgple_ref[...] = jnp.zeros_like(dgple_ref)

        h1v = h1_ref[...]
        m = (h1v * _rms_r(h1v) * gm_ref[...]).astype(BF16)
        m_ref[...] = m
        zr = jnp.maximum(_mm(m, w1[...]), 0.0)
        zr_ref[...] = zr.astype(BF16)
        f = _mm((zr * zr).astype(BF16), w2[...])
        f_ref[...] = f
        h2 = h1v + f * _rms_r(f) * gf_ref[...]
        h2b = h2.astype(BF16)
        h2b_ref[...] = h2b
        pb = p_ref[...].astype(BF16)
        pb_ref[...] = pb
        e = _mm(pb, wpe[...])
        pg = _sigmoid(_mm(h2b, wpg[...]))
        t3 = pg * e
        r3 = _rms_r(t3)
        t3hat = t3 * r3
        diff = h2 + t3hat * gp_ref[...] - tgt_ref[...]
        loss_ref[...] += jnp.sum(diff * diff) * (0.5 / D_MODEL)
        dh3 = diff * (1.0 / D_MODEL)
        dgple_ref[...] += _colsum(dh3 * t3hat)
        dt3 = _rms_bwd(t3hat, r3, gp_ref[...], dh3)
        de_ref[...] = (dt3 * pg).astype(BF16)
        dpre = (dt3 * e * pg * (1.0 - pg)).astype(BF16)
        dpre_ref[...] = dpre
        dh2_ref[...] = dh3 + _mm_nt(dpre, wpg[...])

    return pl.pallas_call(
        body, name="tail_fwd", grid=(seq // tm,),
        in_specs=[_row(tm, D_MODEL), _row(tm, D_MODEL), _row(tm, PLE_DIM), _fix((1, D_MODEL)), _fix((1, D_MODEL)),
                  _fix((1, D_MODEL)), ANY, ANY, ANY, ANY],
        out_specs=[_row(tm, D_MODEL), _row(tm, D_FF), _row(tm, D_MODEL), _row(tm, D_MODEL), _row(tm, PLE_DIM),
                   _row(tm, D_MODEL), _row(tm, D_MODEL), _row(tm, D_MODEL), _fix((8, 128)), _fix((1, D_MODEL))],
        out_shape=[_sds((seq, D_MODEL), BF16), _sds((seq, D_FF), BF16), _sds((seq, D_MODEL), F32),
                   _sds((seq, D_MODEL), BF16), _sds((seq, PLE_DIM), BF16), _sds((seq, D_MODEL), BF16),
                   _sds((seq, D_MODEL), BF16), _sds((seq, D_MODEL), F32), _sds((8, 128), F32), _sds((1, D_MODEL), F32)],
        scratch_shapes=[pltpu.VMEM(w_ff1.shape, BF16), pltpu.VMEM(w_ff2.shape, BF16),
                        pltpu.VMEM(w_pe.shape, BF16), pltpu.VMEM(w_pg.shape, BF16)],
        compiler_params=_params(1, 56),
    )(h1, target, p, g_pre_mlp, g_post_mlp, g_ple, w_ff1, w_ff2, w_pe, w_pg)


def _mlp_bwd(h1, f, zr, dh2, g_pre_mlp, g_post_mlp, w_ff1, w_ff2, tm):
    seq = h1.shape[0]

    def body(h1_ref, f_ref, zr_ref, dh2_ref, gm_ref, gf_ref, w1_hbm, w2_hbm,
             df_ref, dz_ref, dh1_ref, dgm_ref, dgf_ref, w1, w2):
        _load_once(((w1_hbm, w1), (w2_hbm, w2)))

        @pl.when(pl.program_id(0) == 0)
        def _():
            dgm_ref[...] = jnp.zeros_like(dgm_ref)
            dgf_ref[...] = jnp.zeros_like(dgf_ref)

        dh2 = dh2_ref[...]
        fv = f_ref[...]
        rf = _rms_r(fv)
        fhat = fv * rf
        dgf_ref[...] += _colsum(dh2 * fhat)
        df = _rms_bwd(fhat, rf, gf_ref[...], dh2).astype(BF16)
        df_ref[...] = df
        dz = (_mm_nt(df, w2[...]) * (2.0 * zr_ref[...].astype(F32))).astype(BF16)
        dz_ref[...] = dz
        dm = _mm_nt(dz, w1[...])
        h1v = h1_ref[...]
        r1 = _rms_r(h1v)
        h1hat = h1v * r1
        dgm_ref[...] += _colsum(dm * h1hat)
        dh1_ref[...] = dh2 + _rms_bwd(h1hat, r1, gm_ref[...], dm)

    return pl.pallas_call(
        body, name="mlp_bwd", grid=(seq // tm,),
        in_specs=[_row(tm, D_MODEL), _row(tm, D_MODEL), _row(tm, D_FF), _row(tm, D_MODEL),
                  _fix((1, D_MODEL)), _fix((1, D_MODEL)), ANY, ANY],
        out_specs=[_row(tm, D_MODEL), _row(tm, D_FF), _row(tm, D_MODEL), _fix((1, D_MODEL)), _fix((1, D_MODEL))],
        out_shape=[_sds((seq, D_MODEL), BF16), _sds((seq, D_FF), BF16), _sds((seq, D_MODEL), F32),
                   _sds((1, D_MODEL), F32), _sds((1, D_MODEL), F32)],
        scratch_shapes=[pltpu.VMEM(w_ff1.shape, BF16), pltpu.VMEM(w_ff2.shape, BF16)],
        compiler_params=_params(1, 56),
    )(h1, f, zr, dh2, g_pre_mlp, g_post_mlp, w_ff1, w_ff2)


def _merge_bwd(dh1, y, gates, ba, bb, d_pool, g_post_mix, pool_scale, w_out, w_ba, w_bb, w_pool_bf, tm):
    seq = dh1.shape[0]

    def body(dh1_ref, y_ref, gates_ref, ba_ref, bb_ref, d_ref, g_ref, ps_ref, wout_ref, wba_ref, wbb_ref, wp_ref,
             dy_ref, dba_ref, dbb_ref, dgpre_ref, dattn_ref, dyp_ref, dd_ref, dg_ref, dbg_ref, dps_ref):
        @pl.when(pl.program_id(0) == 0)
        def _():
            dg_ref[...] = jnp.zeros_like(dg_ref)
            dbg_ref[...] = jnp.zeros_like(dbg_ref)
            dps_ref[...] = jnp.zeros_like(dps_ref)

        dh1v = dh1_ref[...]
        yv = y_ref[...]
        r = _rms_r(yv)
        yhat = yv * r
        dg_ref[...] += _colsum(dh1v * yhat)
        dy = _rms_bwd(yhat, r, g_ref[...], dh1v).astype(BF16)
        dy_ref[...] = dy
        dmerged = _mm_nt(dy, wout_ref[...])
        for half, branch_ref, dbranch_ref in ((0, ba_ref, dba_ref), (1, bb_ref, dbb_ref)):
            lanes = slice(D_MODEL * half, D_MODEL * (half + 1))
            gate = gates_ref[:, lanes].astype(F32)
            dpre = dmerged * branch_ref[...].astype(F32) * gate * (1.0 - gate)
            dbg_ref[:, lanes] += _colsum(dpre)
            dgpre_ref[:, lanes] = dpre.astype(BF16)
            dbranch_ref[...] = (dmerged * gate).astype(BF16)
        dattn = _mm_nt(dba_ref[...], wba_ref[...]).astype(BF16)
        for h in range(N_HEADS):
            dattn_ref[h] = dattn[:, HEAD_PAD * h:HEAD_PAD * (h + 1)]
        dpooled = _mm_nt(dbb_ref[...], wbb_ref[...])
        for g in range(len(POOL_WINDOWS)):
            lanes = slice(POOL_GROUP * g, POOL_GROUP * (g + 1))
            dpl = dpooled[:, lanes]
            dps_ref[:, lanes] += _colsum(dpl * _mm(d_ref[:, lanes], wp_ref[g]))
            dyp = (dpl * ps_ref[:, lanes]).astype(BF16)
            dyp_ref[:, lanes] = dyp
            dd_ref[:, lanes] = _mm_nt(dyp, wp_ref[g])

    return pl.pallas_call(
        body, name="merge_bwd", grid=(seq // tm,),
        in_specs=[_row(tm, D_MODEL), _row(tm, D_MODEL), _row(tm, 2 * D_MODEL), _row(tm, D_MODEL), _row(tm, D_MODEL),
                  _row(tm, POOL_WIDTH), _fix((1, D_MODEL)), _fix((1, POOL_WIDTH)),
                  _fix(w_out.shape), _fix(w_ba.shape), _fix(w_bb.shape), _fix(w_pool_bf.shape)],
        out_specs=[_row(tm, D_MODEL), _row(tm, D_MODEL), _row(tm, D_MODEL), _row(tm, 2 * D_MODEL),
                   pl.BlockSpec((N_HEADS, tm, HEAD_PAD), lambda i: (0, i, 0)), _row(tm, POOL_WIDTH), _row(tm, POOL_WIDTH),
                   _fix((1, D_MODEL)), _fix((1, 2 * D_MODEL)), _fix((1, POOL_WIDTH))],
        out_shape=[_sds((seq, D_MODEL), BF16), _sds((seq, D_MODEL), BF16), _sds((seq, D_MODEL), BF16),
                   _sds((seq, 2 * D_MODEL), BF16), _sds((N_HEADS, seq, HEAD_PAD), BF16), _sds((seq, POOL_WIDTH), BF16),
                   _sds((seq, POOL_WIDTH), F32), _sds((1, D_MODEL), F32), _sds((1, 2 * D_MODEL), F32),
                   _sds((1, POOL_WIDTH), F32)],
        compiler_params=_params(1, 48),
    )(dh1, y, gates, ba, bb, d_pool, g_post_mix, pool_scale, w_out, w_ba, w_bb, w_pool_bf)


def _proj_bwd(dq, dk, dv, qd, kvd, x, dh1, dgpre, dpin, cf, s1, s2, g_pre, g_q, g_kv,
              w_in_r, w_uq_r, w_k_exp, w_v, tm):
    seq = x.shape[0]

    def body(dq_ref, dk_ref, dv_ref, qd_ref, kvd_ref, x_ref, dh1_ref, dgpre_ref, dpin_ref, cf_ref, s1_ref, s2_ref,
             gpre_ref, gq_ref, gkv_ref, win_hbm, wuq_hbm, wk_hbm, wv_hbm,
             gx_ref, dproj_ref, dqb_ref, qn_ref, kvn_ref, dkb_ref, dvb_ref, dgpre_acc, dgq_acc, dgkv_acc,
             win, wuq, wk, wv):
        _load_once(((win_hbm, win), (wuq_hbm, wuq), (wk_hbm, wk), (wv_hbm, wv)))

        @pl.when(pl.program_id(0) == 0)
        def _():
            dgpre_acc[...] = jnp.zeros_like(dgpre_acc)
            dgq_acc[...] = jnp.zeros_like(dgq_acc)
            dgkv_acc[...] = jnp.zeros_like(dgkv_acc)

        cfv, s1v, s2v = cf_ref[...], s1_ref[...], s2_ref[...]
        ksum = jnp.zeros((tm, HEAD_PAD), F32)
        for h in range(N_HEADS):
            lanes = slice(HEAD_PAD * h, HEAD_PAD * (h + 1))
            dqb_ref[:, lanes] = (_rope_t(jnp.transpose(dq_ref[h]), cfv, s1v, s2v) * ATTN_SCALE).astype(BF16)
            dkh = dk_ref[h]
            dkb_ref[:, lanes] = dkh
            dvb_ref[:, lanes] = dv_ref[h]
            ksum = ksum + dkh.astype(F32)
        lane = lax.broadcasted_iota(jnp.int32, (tm, HEAD_PAD), 1)
        rope_lanes = (lane >= QK_NOPE) & (lane < QK_NOPE + QK_ROPE)
        dkr = _rope_t(jnp.where(rope_lanes, ksum, 0.0), cfv, s1v, s2v)

        qdv = qd_ref[...]
        rq = _rms_r(qdv)
        qhat = qdv * rq
        qn_ref[...] = (qhat * gq_ref[...]).astype(BF16)
        dqn = _mm_nt(dqb_ref[...], wuq[...])
        dgq_acc[...] += _colsum(dqn * qhat)
        dproj_ref[:, IN_Q0:IN_KV0] = _rms_bwd(qhat, rq, gq_ref[...], dqn).astype(BF16)

        kvdv = kvd_ref[...]
        rkv = _rms_r(kvdv)
        kvhat = kvdv * rkv
        kvn_ref[...] = (kvhat * gkv_ref[...]).astype(BF16)
        dkvn = _mm_nt(dkb_ref[...], wk[...]) + _mm_nt(dvb_ref[...], wv[...])
        dgkv_acc[...] += _colsum(dkvn * kvhat)
        dproj_ref[:, IN_KV0:IN_POOL0] = _rms_bwd(kvhat, rkv, gkv_ref[...], dkvn).astype(BF16)

        dproj_ref[:, IN_POOL0:IN_GATE0] = dpin_ref[...]
        dproj_ref[:, IN_GATE0:IN_KR0] = dgpre_ref[...]
        dproj_ref[:, IN_KR0:IN_R] = dkr.astype(BF16)

        da = _mm_nt(dproj_ref[...], win[...])
        xv = x_ref[...]
        r0 = _rms_r(xv)
        xhat = xv * r0
        dgpre_acc[...] += _colsum(da * xhat)
        gx_ref[...] = dh1_ref[...] + _rms_bwd(xhat, r0, gpre_ref[...], da)

    per_tile = ATTN_TILE // tm
    heads = pl.BlockSpec((N_HEADS, tm, HEAD_PAD), lambda i: (0, i, 0))
    heads_t = pl.BlockSpec((N_HEADS, None, HEAD_PAD, tm), lambda i: (0, i // per_tile, 0, i % per_tile))
    return pl.pallas_call(
        body, name="proj_bwd", grid=(seq // tm,),
        in_specs=[heads_t, heads, heads, _row(tm, Q_LORA), _row(tm, KV_LORA), _row(tm, D_MODEL),
                  _row(tm, D_MODEL), _row(tm, 2 * D_MODEL), _row(tm, POOL_WIDTH),
                  _row(tm, HEAD_PAD), _row(tm, HEAD_PAD), _row(tm, HEAD_PAD),
                  _fix((1, D_MODEL)), _fix((1, Q_LORA)), _fix((1, KV_LORA)), ANY, ANY, ANY, ANY],
        out_specs=[_row(tm, D_MODEL), _row(tm, IN_R), _row(tm, N_HEADS * HEAD_PAD), _row(tm, Q_LORA), _row(tm, KV_LORA),
                   _row(tm, N_HEADS * HEAD_PAD), _row(tm, N_HEADS * HEAD_PAD),
                   _fix((1, D_MODEL)), _fix((1, Q_LORA)), _fix((1, KV_LORA))],
        out_shape=[_sds((seq, D_MODEL), F32), _sds((seq, IN_R), BF16), _sds((seq, N_HEADS * HEAD_PAD), BF16),
                   _sds((seq, Q_LORA), BF16), _sds((seq, KV_LORA), BF16), _sds((seq, N_HEADS * HEAD_PAD), BF16),
                   _sds((seq, N_HEADS * HEAD_PAD), BF16),
                   _sds((1, D_MODEL), F32), _sds((1, Q_LORA), F32), _sds((1, KV_LORA), F32)],
        scratch_shapes=[pltpu.VMEM(w_in_r.shape, BF16), pltpu.VMEM(w_uq_r.shape, BF16),
                        pltpu.VMEM(w_k_exp.shape, BF16), pltpu.VMEM(w_v.shape, BF16)],
        compiler_params=_params(1, 48),
    )(dq, dk, dv, qd, kvd, x, dh1, dgpre, dpin, cf, s1, s2, g_pre, g_q, g_kv, w_in_r, w_uq_r, w_k_exp, w_v)


def _grad_w(a, b, name, square_a=False, by_device=False):
    seq, k_dim = a.shape
    n_dim = b.shape[1]
    tk = min(k_dim, 1024)
    tn = n_dim // 2 if n_dim == IN_R else min(n_dim, 1024)
    ts = min(seq, 1024)
    shard = n_dim // N_DEV
    per_tile = tn // shard
    if by_device:
        out_spec = pl.BlockSpec((per_tile, tk, shard), lambda i, j, s: (j, i, 0))
        out_shape = _sds((N_DEV, k_dim, shard), F32)
    else:
        out_spec = pl.BlockSpec((tk, tn), lambda i, j, s: (i, j))
        out_shape = _sds((k_dim, n_dim), F32)

    def body(a_ref, b_ref, o_ref):
        @pl.when(pl.program_id(2) == 0)
        def _():
            o_ref[...] = jnp.zeros_like(o_ref)

        at = a_ref[...]
        if square_a:
            at = at * at
        part = lax.dot_general(at, b_ref[...], TN, preferred_element_type=F32)
        if by_device:
            for d in range(per_tile):
                o_ref[d] += part[:, d * shard:(d + 1) * shard]
        else:
            o_ref[...] += part

    return pl.pallas_call(
        body, name=name, grid=(k_dim // tk, n_dim // tn, seq // ts),
        in_specs=[pl.BlockSpec((ts, tk), lambda i, j, s: (s, i)), pl.BlockSpec((ts, tn), lambda i, j, s: (s, j))],
        out_specs=out_spec, out_shape=out_shape,
        compiler_params=_params(3, 48),
    )(a, b)


def _grad_w_heads(a, b, name):
    heads, seq, _ = a.shape
    n_dim = b.shape[1]
    ts = min(seq, 1024)

    def body(a_ref, b_ref, o_ref):
        @pl.when(pl.program_id(1) == 0)
        def _():
            o_ref[...] = jnp.zeros_like(o_ref)

        o_ref[...] += lax.dot_general(a_ref[...], b_ref[...], TN, preferred_element_type=F32)

    return pl.pallas_call(
        body, name=name, grid=(heads, seq // ts),
        in_specs=[pl.BlockSpec((None, ts, HEAD_PAD), lambda h, s: (h, s, 0)), pl.BlockSpec((ts, n_dim), lambda h, s: (s, 0))],
        out_specs=pl.BlockSpec((HEAD_PAD, n_dim), lambda h, s: (h, 0)),
        out_shape=_sds((heads * HEAD_PAD, n_dim), F32),
        compiler_params=_params(2, 32),
    )(a, b)


def _grad_w_pool(d, dyp):
    seq = d.shape[0]
    ts = min(seq, 1024)

    def body(a_ref, b_ref, o_ref):
        @pl.when(pl.program_id(1) == 0)
        def _():
            o_ref[...] = jnp.zeros_like(o_ref)

        o_ref[...] += lax.dot_general(a_ref[...], b_ref[...], TN, preferred_element_type=F32)

    tile = pl.BlockSpec((ts, POOL_GROUP), lambda g, s: (s, g))
    return pl.pallas_call(
        body, name="grad_w_pool", grid=(len(POOL_WINDOWS), seq // ts),
        in_specs=[tile, tile],
        out_specs=pl.BlockSpec((None, POOL_GROUP, POOL_GROUP), lambda g, s: (g, 0, 0)),
        out_shape=_sds((len(POOL_WINDOWS), POOL_GROUP, POOL_GROUP), F32),
        compiler_params=_params(2, 32),
    )(d, dyp)


def _position():
    return lax.axis_index("x"), lax.axis_index("y"), lax.axis_index("c")


def _gather_copies(x_ref, slot, send_sems, recv_sems, local_sem, phases=("send", "forward", "finish")):
    x, y, c = _position()
    me, sibling = (x, y, c), (x, y, 1 - c)
    chips = [(1 - x, y), (x, 1 - y), (1 - x, 1 - y)]

    def copy(k, block, to, src=None):
        return pltpu.make_async_remote_copy(
            src_ref=slot(*block) if src is None else src, dst_ref=slot(*block),
            send_sem=send_sems.at[k], recv_sem=recv_sems.at[k], device_id=to, device_id_type=MESH)

    mine = pltpu.make_async_copy(x_ref, slot(*me), local_sem)
    first = [copy(0, me, sibling, src=x_ref)]
    first += [copy(1 + j, me, (*chip, c), src=x_ref) for j, chip in enumerate(chips)]
    passed = [copy(4 + j, (*chip, c), sibling) for j, chip in enumerate(chips)]
    if "send" in phases:
        mine.start()
        for cp in first:
            cp.start()
    if "forward" in phases:
        for j, chip in enumerate(chips):
            copy(1 + j, (*chip, c), me).wait_recv()
            passed[j].start()
    if "finish" in phases:
        copy(0, sibling, me).wait_recv()
        for j, chip in enumerate(chips):
            copy(4 + j, (*chip, 1 - c), me).wait_recv()
        for cp in first + passed:
            cp.wait_send()
        mine.wait()


def _all_gather_hbm(block):
    def body(x_ref, out_ref, send_sems, recv_sems, local_sem):
        _gather_copies(x_ref, lambda px, py, pc: out_ref.at[4 * px + 2 * py + pc], send_sems, recv_sems, local_sem)

    return pl.pallas_call(
        body, name="gather_weights",
        in_specs=[ANY], out_specs=ANY,
        out_shape=_sds((N_DEV,) + block.shape, block.dtype),
        scratch_shapes=[pltpu.SemaphoreType.DMA((7,)), pltpu.SemaphoreType.DMA((7,)), pltpu.SemaphoreType.DMA],
    )(block)


def _all_reduce_small(block):
    def body(x_ref, out_ref, buf, send_sems, recv_sems, local_sem):
        _gather_copies(x_ref, lambda px, py, pc: buf.at[4 * px + 2 * py + pc], send_sems, recv_sems, local_sem)
        acc = buf[0]
        for k in range(1, N_DEV):
            acc = acc + buf[k]
        out_ref[...] = acc

    vmem = pl.BlockSpec(memory_space=pltpu.VMEM)
    return pl.pallas_call(
        body, name="all_reduce_small",
        in_specs=[vmem], out_specs=vmem,
        out_shape=_sds(block.shape, F32),
        scratch_shapes=[pltpu.VMEM((N_DEV,) + block.shape, F32), pltpu.SemaphoreType.DMA((7,)),
                        pltpu.SemaphoreType.DMA((7,)), pltpu.SemaphoreType.DMA],
        compiler_params=pltpu.CompilerParams(vmem_limit_bytes=32 * MIB),
    )(block)


def _exchange_pair(gs):
    n_w = len(gs)

    def body(*refs):
        g_refs, out_refs = refs[:n_w], refs[n_w:2 * n_w]
        send_sems, recv_sems = refs[2 * n_w:]
        x, y, c = _position()
        copies = []
        for w in range(n_w):
            for chip in range(4):
                cp = pltpu.make_async_remote_copy(
                    src_ref=g_refs[w].at[2 * chip + (1 - c)], dst_ref=out_refs[w].at[chip],
                    send_sem=send_sems.at[4 * w + chip], recv_sem=recv_sems.at[4 * w + chip],
                    device_id=(x, y, 1 - c), device_id_type=MESH)
                cp.start()
                copies.append(cp)
        for cp in copies:
            cp.wait_recv()
        for cp in copies:
            cp.wait_send()

    return pl.pallas_call(
        body, name="exchange_pair",
        in_specs=[ANY] * n_w, out_specs=[ANY] * n_w,
        out_shape=[_sds((4,) + g.shape[1:], g.dtype) for g in gs],
        scratch_shapes=[pltpu.SemaphoreType.DMA((4 * n_w,)), pltpu.SemaphoreType.DMA((4 * n_w,))],
    )(*gs)


def _exchange_chips(parts):
    n_w = len(parts)

    def body(*refs):
        p_refs, out_refs = refs[:n_w], refs[n_w:2 * n_w]
        send_sems, recv_sems = refs[2 * n_w:]
        x, y, c = _position()
        chips = [(1 - x, y), (x, 1 - y), (1 - x, 1 - y)]
        copies = []
        for w in range(n_w):
            for k, (px, py) in enumerate(chips):
                cp = pltpu.make_async_remote_copy(
                    src_ref=p_refs[w].at[2 * px + py], dst_ref=out_refs[w].at[k],
                    send_sem=send_sems.at[3 * w + k], recv_sem=recv_sems.at[3 * w + k],
                    device_id=(px, py, c), device_id_type=MESH)
                cp.start()
                copies.append(cp)
        for cp in copies:
            cp.wait_recv()
        for cp in copies:
            cp.wait_send()

    return pl.pallas_call(
        body, name="exchange_chips",
        in_specs=[ANY] * n_w, out_specs=[ANY] * n_w,
        out_shape=[_sds((3,) + p.shape[1:], p.dtype) for p in parts],
        scratch_shapes=[pltpu.SemaphoreType.DMA((3 * n_w,)), pltpu.SemaphoreType.DMA((3 * n_w,))],
    )(*parts)


def _row_tile(k):
    return 256 if k % 256 == 0 else 128


def _pair_sum(g, recv, place, name):
    _, k, n = g.shape
    tr = _row_tile(k)
    g4 = g.reshape(4, 2, k, n)

    def body(s_ref, g_ref, r_ref, o_ref):
        o_ref[...] = (g_ref[...] + r_ref[...]).astype(BF16)

    spec = pltpu.PrefetchScalarGridSpec(
        num_scalar_prefetch=1, grid=(4, k // tr),
        in_specs=[pl.BlockSpec((None, None, tr, n), lambda j, i, s: (j, s[2], i, 0)),
                  pl.BlockSpec((None, tr, n), lambda j, i, s: (j, i, 0))],
        out_specs=pl.BlockSpec((None, tr, n), lambda j, i, s: (j, i, 0)))
    return pl.pallas_call(
        body, name=name, grid_spec=spec, out_shape=_sds((4, k, n), BF16),
        compiler_params=_params(2, 32),
    )(place, g4, recv)


def _adamw_math(g, w, m, v):
    m = ADAM_B1 * m + (1.0 - ADAM_B1) * g
    v = ADAM_B2 * v + (1.0 - ADAM_B2) * (g * g)
    m_hat = m / (1.0 - ADAM_B1 ** ADAM_STEP)
    v_hat = v / (1.0 - ADAM_B2 ** ADAM_STEP)
    delta = -ADAM_LR * (m_hat / (jnp.sqrt(v_hat) + ADAM_EPS) + ADAM_WD * w)
    return delta, m, v


def _adamw_sharded(g, from_sibling, from_chips, place, w, m, v, name):
    _, k, n = g.shape
    tr = _row_tile(k)

    def body(s_ref, g_ref, sib_ref, r0_ref, r1_ref, r2_ref, w_ref, m_ref, v_ref, grad_ref, d_ref, nm_ref, nv_ref):
        grad = g_ref[...] + sib_ref[...]
        for r_ref in (r0_ref, r1_ref, r2_ref):
            grad = grad + r_ref[...].astype(F32)
        grad_ref[...] = grad
        d_ref[...], nm_ref[...], nv_ref[...] = _adamw_math(grad, w_ref[...], m_ref[...], v_ref[...])

    tile = pl.BlockSpec((None, tr, n), lambda i, s: (0, i, 0))

    def slot(j):
        return pl.BlockSpec((None, tr, n), lambda i, s: (j, i, 0))

    spec = pltpu.PrefetchScalarGridSpec(
        num_scalar_prefetch=1, grid=(k // tr,),
        in_specs=[pl.BlockSpec((None, tr, n), lambda i, s: (s[0], i, 0)),
                  pl.BlockSpec((None, tr, n), lambda i, s: (s[1], i, 0)),
                  slot(0), slot(1), slot(2), tile, tile, tile],
        out_specs=[tile] * 4)
    return pl.pallas_call(
        body, name=name, grid_spec=spec, out_shape=[_sds((1, k, n), F32)] * 4,
        compiler_params=_params(1, 48),
    )(place, g, from_sibling, from_chips, from_chips, from_chips, w, m, v)


def _adamw_small(g, w, m, v):
    def body(g_ref, w_ref, m_ref, v_ref, d_ref, nm_ref, nv_ref):
        d_ref[...], nm_ref[...], nv_ref[...] = _adamw_math(g_ref[...], w_ref[...], m_ref[...], v_ref[...])

    return pl.pallas_call(body, name="adamw_small", out_shape=[_sds(g.shape, F32)] * 3)(g, w, m, v)


def _pack_rows(parts):
    parts = [a.reshape(-1, LANES) for a in parts]
    pad = (-sum(a.shape[0] for a in parts)) % PACK_ROW_TILE
    return jnp.concatenate(parts + [jnp.zeros((pad, LANES), parts[0].dtype)], axis=0)


def _pack_small(parts):
    flat = jnp.concatenate([a.reshape(-1) for a in parts])
    pad = (-flat.shape[0]) % (8 * LANES)
    return jnp.pad(flat, (0, pad)).reshape(-1, LANES)


def _unpack_small(packed, shapes):
    flat = packed.reshape(-1)
    out, off = [], 0
    for shape in shapes:
        size = 1
        for n in shape:
            size *= n
        out.append(flat[off:off + size].reshape(shape))
        off += size
    return out


def _full_from_gathered(gathered, entries, shard_shapes):
    out, off = {}, 0
    for (name, kind), (k, n) in zip(entries, shard_shapes):
        rows = k * n // LANES
        seg = gathered[:, off:off + rows].reshape(N_DEV, k, n)
        out[name] = jnp.transpose(seg, (1, 0, 2)).reshape(k, N_DEV * n) if kind == "col" else seg.reshape(N_DEV * k, n)
        off += rows
    return out


def _columns_by_device(a):
    k, n_all = a.shape
    return jnp.transpose(a.reshape(k, N_DEV, n_all // N_DEV), (1, 0, 2))


def _rows_by_device(a):
    k_all, n = a.shape
    return a.reshape(N_DEV, k_all // N_DEV, n)


def _rope_tables(positions):
    inv_freq = ROPE_THETA ** (-jnp.arange(0, QK_ROPE, 2, dtype=F32) / QK_ROPE)
    ang = positions.astype(F32)[:, None] * inv_freq
    cos, sin = jnp.cos(ang), jnp.sin(ang)
    seq = positions.shape[0]
    zeros = lambda n: jnp.zeros((seq, n), F32)
    cf = jnp.concatenate([jnp.ones((seq, QK_NOPE), F32), cos, cos, zeros(HEAD_PAD - QK_NOPE - QK_ROPE)], axis=1)
    s1 = jnp.concatenate([zeros(QK_NOPE), -sin, zeros(HEAD_PAD - QK_NOPE - QK_ROPE // 2)], axis=1)
    s2 = jnp.concatenate([zeros(QK_NOPE + QK_ROPE // 2), sin, zeros(HEAD_PAD - QK_NOPE - QK_ROPE)], axis=1)
    return cf, s1, s2


def _arrange_w_in(w):
    k = w.shape[0]
    zeros = lambda n: jnp.zeros((k, n), w.dtype)
    kr0 = Q_LORA + KV_LORA
    pool0 = kr0 + QK_ROPE
    return jnp.concatenate([w[:, :kr0], w[:, pool0:], zeros(QK_NOPE), w[:, kr0:pool0],
                            zeros(HEAD_PAD - QK_NOPE - QK_ROPE)], axis=1)


def _restore_w_in(d):
    kr = d[:, IN_KR0 + QK_NOPE:IN_KR0 + QK_NOPE + QK_ROPE]
    return jnp.concatenate([d[:, :IN_POOL0], kr, d[:, IN_POOL0:IN_KR0]], axis=1)


def _pad_heads(w, width):
    k = w.shape[0]
    w = w.reshape(k, N_HEADS, width)
    return jnp.pad(w, ((0, 0), (0, 0), (0, HEAD_PAD - width))).reshape(k, N_HEADS * HEAD_PAD)


def _unpad_heads(d, width):
    k = d.shape[0]
    return d.reshape(k, N_HEADS, HEAD_PAD)[:, :, :width]


def kernel(x, p, positions, g_pre_mix, w_in, b_gate, g_q, w_uq, g_kv, w_ukv, w_pool, pool_scale, w_branch_attn, w_branch_pool, w_out, g_post_mix, g_pre_mlp, w_ff1, w_ff2, g_post_mlp, w_ple_proj, w_ple_gate, g_ple, loss_target, m_g_pre_mix, m_w_in, m_b_gate, m_g_q, m_w_uq, m_g_kv, m_w_ukv, m_w_pool, m_pool_scale, m_w_branch_attn, m_w_branch_pool, m_w_out, m_g_post_mix, m_g_pre_mlp, m_w_ff1, m_w_ff2, m_g_post_mlp, m_w_ple_proj, m_w_ple_gate, m_g_ple, v_g_pre_mix, v_w_in, v_b_gate, v_g_q, v_w_uq, v_g_kv, v_w_ukv, v_w_pool, v_pool_scale, v_w_branch_attn, v_w_branch_pool, v_w_out, v_g_post_mix, v_g_pre_mlp, v_w_ff1, v_w_ff2, v_g_post_mlp, v_w_ple_proj, v_w_ple_gate, v_g_ple):
    given = dict(locals())
    weights = {n: given[n] for n in WEIGHT_ORDER}
    moments_m = {n: given["m_" + n] for n in WEIGHT_ORDER}
    moments_v = {n: given["v_" + n] for n in WEIGHT_ORDER}
    xs, ps, target = x[0], p[0, 0], loss_target[0]
    seq = xs.shape[0]
    tm = min(256, seq)
    core = lax.axis_index("c")
    chip = 2 * lax.axis_index("x") + lax.axis_index("y")

    early, later = SHARDED[:N_EARLY], SHARDED[N_EARLY:]
    shapes_of = lambda entries: [weights[n].shape[1:] for n, _ in entries]
    pack_bf16 = lambda entries: _pack_rows([weights[n][0].astype(BF16) for n, _ in entries])
    full = _full_from_gathered(_all_gather_hbm(pack_bf16(early)), early, shapes_of(early))
    w_in_r = _arrange_w_in(full["w_in"])
    w_uq_r = _pad_heads(full["w_uq"], QK_NOPE + QK_ROPE)
    ukv = full["w_ukv"].reshape(KV_LORA, N_HEADS, QK_NOPE + V_HEAD)
    w_k_exp = _pad_heads(ukv[:, :, :QK_NOPE].reshape(KV_LORA, N_HEADS * QK_NOPE), QK_NOPE)
    w_v = _pad_heads(ukv[:, :, QK_NOPE:].reshape(KV_LORA, N_HEADS * V_HEAD), V_HEAD)
    w_pool_bf = w_pool[0].astype(BF16)
    cf, s1, s2 = _rope_tables(positions[0])

    a_bf, qd, kvd, pin, gates, q, k, v, k_t, gathered_later = _proj_fwd(
        xs, g_pre_mix, b_gate, g_q, g_kv, cf, s1, s2, w_in_r, w_uq_r, w_k_exp, w_v, pack_bf16(later), tm)
    full.update(_full_from_gathered(gathered_later, later, shapes_of(later)))
    w_ba = jnp.pad(full["w_branch_attn"].reshape(N_HEADS, V_HEAD, D_MODEL),
                   ((0, 0), (0, HEAD_PAD - V_HEAD), (0, 0))).reshape(N_HEADS * HEAD_PAD, D_MODEL)
    d_pool, pooled = _pool_fwd(pin, w_pool_bf, pool_scale, tm)
    o_heads, lse = _attn_fwd(q, k, v)
    merged, ba, bb, y, h1 = _merge_fwd(o_heads, pooled, gates, xs, g_post_mix, w_ba,
                                       full["w_branch_pool"], full["w_out"], tm)
    (m_bf, zr, f, h2_bf, p_bf, de, dpre, dh2, loss_acc, dg_ple) = _tail_fwd(
        h1, target, ps, g_pre_mlp, g_post_mlp, g_ple, full["w_ff1"], full["w_ff2"], full["w_ple_proj"],
        full["w_ple_gate"], tm)

    df, dz, dh1, dg_pre_mlp, dg_post_mlp = _mlp_bwd(h1, f, zr, dh2, g_pre_mlp, g_post_mlp, full["w_ff1"],
                                                    full["w_ff2"], tm)
    (dy, dba, dbb, dgpre, do_heads, dyp, dd, dg_post_mix, db_gate, dpool_scale) = _merge_bwd(
        dh1, y, gates, ba, bb, d_pool, g_post_mix, pool_scale, full["w_out"], w_ba,
        full["w_branch_pool"], w_pool_bf, tm)
    dpin = _pool_bwd_window(dd, tm)
    delta = _attn_delta(o_heads, do_heads)
    dq, dk, dv = _attn_bwd(q, k, k_t, v, do_heads, lse, delta)
    (grad_x, dproj, dq_bf, qn_bf, kvn_bf, dk_bf, dv_bf, dg_pre_mix, dg_q, dg_kv) = _proj_bwd(
        dq, dk, dv, qd, kvd, xs, dh1, dgpre, dpin, cf, s1, s2, g_pre_mix, g_q, g_kv, w_in_r, w_uq_r, w_k_exp, w_v, tm)

    d_k_exp = _unpad_heads(_grad_w(kvn_bf, dk_bf, "grad_w_uk"), QK_NOPE)
    d_w_v = _unpad_heads(_grad_w(kvn_bf, dv_bf, "grad_w_uv"), V_HEAD)
    d_w_ba = _grad_w_heads(o_heads, dba, "grad_w_branch_attn").reshape(N_HEADS, HEAD_PAD, D_MODEL)
    by_device = {
        "w_in": _columns_by_device(_restore_w_in(_grad_w(a_bf, dproj, "grad_w_in"))),
        "w_uq": _columns_by_device(
            _unpad_heads(_grad_w(qn_bf, dq_bf, "grad_w_uq"), QK_NOPE + QK_ROPE).reshape(Q_LORA, -1)),
        "w_ukv": _columns_by_device(jnp.concatenate([d_k_exp, d_w_v], axis=2).reshape(KV_LORA, -1)),
        "w_branch_attn": _columns_by_device(d_w_ba[:, :V_HEAD].reshape(N_HEADS * V_HEAD, D_MODEL)),
        "w_branch_pool": _grad_w(pooled, dbb, "grad_w_branch_pool", by_device=True),
        "w_out": _rows_by_device(_grad_w(merged, dy, "grad_w_out")),
        "w_ff1": _grad_w(m_bf, dz, "grad_w_ff1", by_device=True),
        "w_ff2": _rows_by_device(_grad_w(zr, df, "grad_w_ff2", square_a=True)),
        "w_ple_proj": _grad_w(p_bf, de, "grad_w_ple_proj", by_device=True),
        "w_ple_gate": _rows_by_device(_grad_w(h2_bf, dpre, "grad_w_ple_gate")),
    }
    grads_small = {
        "g_pre_mix": dg_pre_mix, "b_gate": db_gate, "g_q": dg_q, "g_kv": dg_kv,
        "w_pool": _grad_w_pool(d_pool, dyp), "pool_scale": dpool_scale, "g_post_mix": dg_post_mix,
        "g_pre_mlp": dg_pre_mlp, "g_post_mlp": dg_post_mlp, "g_ple": dg_ple,
    }

    names = [n for n, _ in SHARDED]
    place = jnp.stack([2 * chip + core, chip, core]).astype(jnp.int32)
    own = [by_device[n] for n in names]
    from_sibling = _exchange_pair(own)
    pair = [_pair_sum(g, r, place, "pair_sum_" + n) for n, g, r in zip(names, own, from_sibling)]
    from_chips = _exchange_chips(pair)
    sharded = {n: _adamw_sharded(g, r, rc, place, weights[n], moments_m[n], moments_v[n], "adamw_" + n)
               for n, g, r, rc in zip(names, own, from_sibling, from_chips)}

    g_sm = _all_reduce_small(_pack_small([grads_small[n] for n in REPLICATED] + [loss_acc[0:1, 0:1]]))
    n_small = sum(weights[n].size for n in REPLICATED)
    d_sm, m_sm, v_sm = _adamw_small(g_sm, _pack_small([weights[n] for n in REPLICATED]),
                                    _pack_small([moments_m[n] for n in REPLICATED]),
                                    _pack_small([moments_v[n] for n in REPLICATED]))

    small_shapes = [weights[n].shape for n in REPLICATED]
    results = []
    for which, small in enumerate((g_sm, d_sm, m_sm, v_sm)):
        named = {n: sharded[n][which] for n in names}
        named.update(zip(REPLICATED, _unpack_small(small, small_shapes)))
        results.append([named[n] for n in WEIGHT_ORDER])

    loss = g_sm.reshape(-1)[n_small]
    return (loss, grad_x[None], *results[0], *results[1], *results[2], *results[3])
```

```python
import jax
import jax.numpy as jnp
from jax import lax
from jax.experimental import pallas as pl
from jax.experimental.pallas import tpu as pltpu

F32 = jnp.float32
BF16 = jnp.bfloat16

D_MODEL = 1024
PLE_DIM = 256
N_HEADS = 8
QK_NOPE = 64
QK_ROPE = 32
V_HEAD = 64
Q_LORA = 384
KV_LORA = 256
POOL_WINDOWS = (2, 4, 8, 16)
POOL_GROUP = 128
POOL_WIDTH = 512
D_FF = 4096
ROPE_THETA = 10000.0
EPS = 1e-6
HEAD_PAD = 128
ATTN_SCALE = (QK_NOPE + QK_ROPE) ** -0.5
LOG2E = 1.4426950408889634
Q_PRESCALE = ATTN_SCALE * LOG2E
ATTN_TILE = 512
FWD_ROWS = 512
FWD_CHAINS = 2

ADAM_LR = 0.001
ADAM_B1 = 0.9
ADAM_B2 = 0.999
ADAM_EPS = 1e-08
ADAM_WD = 0.01
ADAM_STEP = 10

N_DEV = 8
LANES = 1024
PACK_ROW_TILE = 480
POOL_HALO = 16
MIB = 2 ** 20

IN_Q0, IN_KV0, IN_POOL0, IN_GATE0, IN_KR0, IN_R = 0, 384, 640, 1152, 3200, 3328

SHARDED = (("w_in", "col"), ("w_uq", "col"), ("w_ukv", "col"), ("w_branch_attn", "col"),
           ("w_branch_pool", "col"), ("w_out", "row"), ("w_ff1", "col"), ("w_ff2", "row"),
           ("w_ple_proj", "col"), ("w_ple_gate", "row"))
N_EARLY = 3
REPLICATED = ("g_pre_mix", "b_gate", "g_q", "g_kv", "w_pool", "pool_scale", "g_post_mix",
              "g_pre_mlp", "g_post_mlp", "g_ple")
WEIGHT_ORDER = ("g_pre_mix", "w_in", "b_gate", "g_q", "w_uq", "g_kv", "w_ukv", "w_pool", "pool_scale",
                "w_branch_attn", "w_branch_pool", "w_out", "g_post_mix", "g_pre_mlp", "w_ff1", "w_ff2",
                "g_post_mlp", "w_ple_proj", "w_ple_gate", "g_ple")

NT = (((1,), (1,)), ((), ()))
TN = (((0,), (0,)), ((), ()))
MESH = pl.DeviceIdType.MESH
ANY = pl.BlockSpec(memory_space=pl.ANY)


def _params(n_axes, vmem_mib):
    return pltpu.CompilerParams(dimension_semantics=("arbitrary",) * n_axes, vmem_limit_bytes=vmem_mib * MIB)


def _row(tm, n):
    return pl.BlockSpec((tm, n), lambda i: (i, 0))


def _fix(shape):
    zeros = (0,) * len(shape)
    return pl.BlockSpec(shape, lambda i: zeros)


def _sds(shape, dtype):
    return jax.ShapeDtypeStruct(shape, dtype)


def _rms_r(v):
    return lax.rsqrt(jnp.mean(v * v, axis=-1, keepdims=True) + EPS)


def _rms_bwd(vhat, r, g, dy):
    gdy = dy * g
    return r * (gdy - vhat * jnp.mean(gdy * vhat, axis=-1, keepdims=True))


def _colsum(v):
    return jnp.sum(v, axis=0, keepdims=True)


def _sigmoid(v):
    return 1.0 / (1.0 + jnp.exp(-v))


def _mm(a, b):
    return jnp.dot(a, b, preferred_element_type=F32)


def _mm_nt(a, b):
    return lax.dot_general(a, b, NT, preferred_element_type=F32)


def _rope(c, cf, s1, s2):
    return c * cf + pltpu.roll(c, HEAD_PAD - 16, 1) * s1 + pltpu.roll(c, 16, 1) * s2


def _rope_t(c, cf, s1, s2):
    return c * cf + pltpu.roll(c * s1, 16, 1) + pltpu.roll(c * s2, HEAD_PAD - 16, 1)


def _load_once(pairs):
    @pl.when(pl.program_id(0) == 0)
    def _():
        for src, dst in pairs:
            pltpu.sync_copy(src, dst)


def _proj_fwd(x, g_pre, b_gate, g_q, g_kv, cf, s1, s2, w_in_r, w_uq_r, w_k_exp, w_v, later_shards, tm):
    seq = x.shape[0]
    n_steps = seq // tm
    forward_step = (3 * n_steps) // 4

    def body(x_ref, gpre_ref, bg_ref, gq_ref, gkv_ref, cf_ref, s1_ref, s2_ref, win_hbm, wuq_hbm, wk_hbm, wv_hbm,
             later_ref, a_ref, qd_ref, kvd_ref, pin_ref, gates_ref, q_ref, k_ref, v_ref, kt_ref, gathered_ref,
             win, wuq, wk, wv, send_sems, recv_sems, local_sem):
        step = pl.program_id(0)

        def gather(phase):
            _gather_copies(later_ref, lambda px, py, pc: gathered_ref.at[4 * px + 2 * py + pc],
                           send_sems, recv_sems, local_sem, phases=(phase,))

        pl.when(step == 0)(lambda: gather("send"))
        pl.when(step == forward_step)(lambda: gather("forward"))
        _load_once(((win_hbm, win), (wuq_hbm, wuq), (wk_hbm, wk), (wv_hbm, wv)))
        xv = x_ref[...]
        a = (xv * _rms_r(xv) * gpre_ref[...]).astype(BF16)
        a_ref[...] = a
        proj = _mm(a, win[...])
        qd = proj[:, IN_Q0:IN_KV0]
        kvd = proj[:, IN_KV0:IN_POOL0]
        qd_ref[...] = qd
        kvd_ref[...] = kvd
        pin_ref[...] = proj[:, IN_POOL0:IN_GATE0]
        gates_ref[...] = _sigmoid(proj[:, IN_GATE0:IN_KR0] + bg_ref[...]).astype(BF16)
        cfv, s1v, s2v = cf_ref[...], s1_ref[...], s2_ref[...]
        krr = _rope(proj[:, IN_KR0:IN_R], cfv, s1v, s2v)
        qn = (qd * _rms_r(qd) * gq_ref[...]).astype(BF16)
        qf = _mm(qn, wuq[...])
        kvn = (kvd * _rms_r(kvd) * gkv_ref[...]).astype(BF16)
        kf = _mm(kvn, wk[...])
        vf = _mm(kvn, wv[...])
        one_lane = (lax.broadcasted_iota(jnp.int32, (tm, HEAD_PAD), 1) == V_HEAD).astype(F32)
        for h in range(N_HEADS):
            lanes = slice(HEAD_PAD * h, HEAD_PAD * (h + 1))
            q_ref[h] = (_rope(qf[:, lanes], cfv, s1v, s2v) * Q_PRESCALE).astype(BF16)
            kh = kf[:, lanes] + krr
            vh = vf[:, lanes] + one_lane
            k_ref[h] = kh.astype(BF16)
            v_ref[h] = vh.astype(BF16)
            kt_ref[h] = jnp.transpose(kh).astype(BF16)
        pl.when(step == n_steps - 1)(lambda: gather("finish"))

    per_tile = ATTN_TILE // tm
    heads = pl.BlockSpec((N_HEADS, tm, HEAD_PAD), lambda i: (0, i, 0))
    heads_t = pl.BlockSpec((N_HEADS, None, HEAD_PAD, tm), lambda i: (0, i // per_tile, 0, i % per_tile))
    heads_t_shape = _sds((N_HEADS, seq // ATTN_TILE, HEAD_PAD, ATTN_TILE), BF16)
    return pl.pallas_call(
        body, name="proj_fwd", grid=(seq // tm,),
        in_specs=[_row(tm, D_MODEL), _fix((1, D_MODEL)), _fix((1, 2 * D_MODEL)), _fix((1, Q_LORA)), _fix((1, KV_LORA)),
                  _row(tm, HEAD_PAD), _row(tm, HEAD_PAD), _row(tm, HEAD_PAD), ANY, ANY, ANY, ANY, ANY],
        out_specs=[_row(tm, D_MODEL), _row(tm, Q_LORA), _row(tm, KV_LORA), _row(tm, POOL_WIDTH), _row(tm, 2 * D_MODEL),
                   heads, heads, heads, heads_t, ANY],
        out_shape=[_sds((seq, D_MODEL), BF16), _sds((seq, Q_LORA), F32), _sds((seq, KV_LORA), F32),
                   _sds((seq, POOL_WIDTH), F32), _sds((seq, 2 * D_MODEL), BF16),
                   _sds((N_HEADS, seq, HEAD_PAD), BF16), _sds((N_HEADS, seq, HEAD_PAD), BF16),
                   _sds((N_HEADS, seq, HEAD_PAD), BF16), heads_t_shape,
                   _sds((N_DEV,) + later_shards.shape, later_shards.dtype)],
        scratch_shapes=[pltpu.VMEM(w_in_r.shape, BF16), pltpu.VMEM(w_uq_r.shape, BF16),
                        pltpu.VMEM(w_k_exp.shape, BF16), pltpu.VMEM(w_v.shape, BF16),
                        pltpu.SemaphoreType.DMA((7,)), pltpu.SemaphoreType.DMA((7,)), pltpu.SemaphoreType.DMA],
        compiler_params=_params(1, 48),
    )(x, g_pre, b_gate, g_q, g_kv, cf, s1, s2, w_in_r, w_uq_r, w_k_exp, w_v, later_shards)


def _window_count(row0, n_rows):
    t = row0 + lax.broadcasted_iota(jnp.int32, (n_rows, POOL_GROUP), 0)
    return [jnp.minimum(t + 1, w).astype(F32) for w in POOL_WINDOWS]


def _pool_fwd(pin, w_pool_bf, pool_scale, tm):
    seq = pin.shape[0]
    ext_rows = tm + POOL_HALO

    def body(prev_ref, u_ref, wp_ref, ps_ref, d_ref, pooled_ref):
        i = pl.program_id(0)
        prev = jnp.where(i == 0, 0.0, prev_ref[...])
        u = u_ref[...]
        level = jnp.concatenate([prev, u], axis=0)
        counts = _window_count(i * tm, tm)
        shift = 1
        for g in range(len(POOL_WINDOWS)):
            level = level + pltpu.roll(level, shift, 0)
            shift *= 2
            lanes = slice(POOL_GROUP * g, POOL_GROUP * (g + 1))
            d = (level[POOL_HALO:, lanes] / counts[g] - u[:, lanes]).astype(BF16)
            d_ref[:, lanes] = d
            pooled_ref[:, lanes] = (_mm(d, wp_ref[g]) * ps_ref[:, lanes]).astype(BF16)

    halo = tm // POOL_HALO
    return pl.pallas_call(
        body, name="pool_fwd", grid=(seq // tm,),
        in_specs=[pl.BlockSpec((POOL_HALO, POOL_WIDTH), lambda i: (jnp.maximum(i * halo - 1, 0), 0)),
                  _row(tm, POOL_WIDTH), _fix(w_pool_bf.shape), _fix((1, POOL_WIDTH))],
        out_specs=[_row(tm, POOL_WIDTH), _row(tm, POOL_WIDTH)],
        out_shape=[_sds((seq, POOL_WIDTH), BF16), _sds((seq, POOL_WIDTH), BF16)],
        compiler_params=_params(1, 32),
    )(pin, pin, w_pool_bf, pool_scale)


def _pool_bwd_window(dd, tm):
    seq = dd.shape[0]
    n_tiles = seq // tm
    ext_rows = tm + POOL_HALO

    def body(dd_ref, next_ref, dpin_ref):
        i = pl.program_id(0)
        nxt = jnp.where(i == n_tiles - 1, 0.0, next_ref[...])
        dd_t = dd_ref[...]
        ext = jnp.concatenate([dd_t, nxt], axis=0)
        counts = _window_count(i * tm, ext_rows)
        shift = 1
        for g in range(len(POOL_WINDOWS)):
            lanes = slice(POOL_GROUP * g, POOL_GROUP * (g + 1))
            level = ext[:, lanes] / counts[g]
            s = 1
            while s <= shift:
                level = level + pltpu.roll(level, ext_rows - s, 0)
                s *= 2
            shift *= 2
            dpin_ref[:, lanes] = (level[:tm] - dd_t[:, lanes]).astype(BF16)

    halo = tm // POOL_HALO
    return pl.pallas_call(
        body, name="pool_bwd_window", grid=(n_tiles,),
        in_specs=[_row(tm, POOL_WIDTH),
                  pl.BlockSpec((POOL_HALO, POOL_WIDTH), lambda i: (jnp.minimum((i + 1) * halo, seq // POOL_HALO - 1), 0))],
        out_specs=_row(tm, POOL_WIDTH),
        out_shape=_sds((seq, POOL_WIDTH), BF16),
        compiler_params=_params(1, 32),
    )(dd, dd)


def _col_to_row(col, n):
    return jnp.transpose(jnp.broadcast_to(col, (n, HEAD_PAD)))[0:1, :]


def _attn_fwd(q, k, v):
    heads, seq, _ = q.shape
    r, n = FWD_ROWS, FWD_CHAINS
    block = r * n

    def body(q_ref, k_ref, v_ref, o_ref, lse_ref):
        qi = pl.program_id(1)
        q_tiles = [q_ref[c * r:(c + 1) * r, :] for c in range(n)]

        def tile(qt, j, m, acc, diagonal):
            start = pl.multiple_of(j * r, r)
            s = _mm_nt(qt, k_ref[pl.ds(start, r), :])
            if diagonal:
                row = lax.broadcasted_iota(jnp.int32, (r, r), 0)
                col = lax.broadcasted_iota(jnp.int32, (r, r), 1)
                s = jnp.where(col <= row, s, -jnp.inf)
            m_new = jnp.maximum(m, jnp.max(s, axis=1, keepdims=True))
            p = jnp.exp2(s - m_new).astype(BF16)
            acc = jnp.exp2(m - m_new) * acc + _mm(p, v_ref[pl.ds(start, r), :])
            return m_new, acc

        def all_chains(jj, carry):
            for u in range(n):
                carry = tuple(tile(q_tiles[c], n * jj + u, *carry[c], False) for c in range(n))
            return carry

        init = tuple((jnp.full((r, 1), -jnp.inf, F32), jnp.zeros((r, HEAD_PAD), F32)) for _ in range(n))
        state = list(lax.fori_loop(0, qi, all_chains, init))
        for d in range(n):
            for c in range(d, n):
                state[c] = tile(q_tiles[c], n * qi + d, *state[c], c == d)
        for c, (m, acc) in enumerate(state):
            l = acc[:, V_HEAD:V_HEAD + 1]
            o_ref[c * r:(c + 1) * r, :] = (acc / l).astype(BF16)
            row0 = c * r
            lse_ref[row0 // ATTN_TILE, :, row0 % ATTN_TILE:row0 % ATTN_TILE + r] = _col_to_row(m + jnp.log2(l), r)

    return pl.pallas_call(
        body, name="attn_fwd", grid=(heads, seq // block),
        in_specs=[pl.BlockSpec((None, block, HEAD_PAD), lambda h, i: (h, i, 0)),
                  pl.BlockSpec((None, seq, HEAD_PAD), lambda h, i: (h, 0, 0)),
                  pl.BlockSpec((None, seq, HEAD_PAD), lambda h, i: (h, 0, 0))],
        out_specs=[pl.BlockSpec((None, block, HEAD_PAD), lambda h, i: (h, i, 0)),
                   pl.BlockSpec((None, block // ATTN_TILE, 1, ATTN_TILE), lambda h, i: (h, i, 0, 0))],
        out_shape=[_sds((heads, seq, HEAD_PAD), BF16), _sds((heads, seq // ATTN_TILE, 1, ATTN_TILE), F32)],
        compiler_params=_params(2, 48),
    )(q, k, v)


def _attn_delta(o, do):
    heads, seq, _ = o.shape
    tq = ATTN_TILE
    nq = seq // tq

    per_step = min(4, nq)

    def body(o_ref, do_ref, delta_ref):
        for u in range(per_step):
            rows = slice(u * tq, (u + 1) * tq)
            prod = o_ref[rows, :].astype(F32) * do_ref[rows, :].astype(F32)
            delta_ref[u] = _col_to_row(jnp.sum(prod, axis=1, keepdims=True), tq)

    tile = pl.BlockSpec((None, per_step * tq, HEAD_PAD), lambda h, i: (h, i, 0))
    return pl.pallas_call(
        body, name="attn_delta", grid=(heads, nq // per_step),
        in_specs=[tile, tile],
        out_specs=pl.BlockSpec((None, per_step, 1, tq), lambda h, i: (h, i, 0, 0)),
        out_shape=_sds((heads, nq, 1, tq), F32),
        compiler_params=_params(2, 32),
    )(o, do)


def _peer_copies(src_refs, dst_refs, send_sems, recv_sems):
    x, y, c = _position()
    copies = []
    for w, (src, dst) in enumerate(zip(src_refs, dst_refs)):
        for r in range(1, N_DEV):
            px = 1 - x if r & 4 else x
            py = 1 - y if r & 2 else y
            pc = 1 - c if r & 1 else c
            copies.append(pltpu.make_async_remote_copy(
                src_ref=src.at[4 * px + 2 * py + pc], dst_ref=dst.at[r - 1],
                send_sem=send_sems.at[(N_DEV - 1) * w + r - 1], recv_sem=recv_sems.at[(N_DEV - 1) * w + r - 1],
                device_id=(px, py, pc), device_id_type=MESH))
    return copies


def _attn_bwd(q, k, k_t, v, do, lse, delta, early_grads):
    heads, seq, _ = q.shape
    t = ATTN_TILE
    nq = seq // t
    n_w = len(early_grads)

    def body(q_ref, k_ref, kt_ref, v_ref, do_ref, lse_ref, delta_ref, *rest):
        grad_refs, rest = rest[:n_w], rest[n_w:]
        dq_ref, dk_ref, dv_ref = rest[:3]
        recv_refs, (send_sems, recv_sems) = rest[3:3 + n_w], rest[3 + n_w:]
        jp = pl.program_id(1)
        head = pl.program_id(0)

        @pl.when((head == 0) & (jp == 0))
        def _():
            for cp in _peer_copies(grad_refs, recv_refs, send_sems, recv_sems):
                cp.start()

        @pl.when(jp == 0)
        def _():
            dq_ref[...] = jnp.zeros_like(dq_ref)

        k_a, k_b = k_ref[0:t, :], k_ref[t:2 * t, :]
        v_a, v_b = v_ref[0:t, :], v_ref[t:2 * t, :]

        def tile(kt, k_tr, vt, i, dk, dv, diagonal):
            start = pl.multiple_of(i * t, t)
            qt = q_ref[pl.ds(start, t), :]
            dot = do_ref[pl.ds(start, t), :]
            p_t = jnp.exp2(_mm_nt(kt, qt) - lse_ref[i])
            if diagonal:
                key = lax.broadcasted_iota(jnp.int32, (t, t), 0)
                query = lax.broadcasted_iota(jnp.int32, (t, t), 1)
                p_t = jnp.where(key <= query, p_t, 0.0)
            dv = dv + _mm(p_t.astype(BF16), dot)
            ds_t = (p_t * (_mm_nt(vt, dot) - delta_ref[i])).astype(BF16)
            dk = dk + _mm(ds_t, qt)
            return dk, dv, _mm(k_tr, ds_t)

        def add_dq(i, dq):
            dq_ref[i] += dq

        def both(ip, carry):
            dk_a, dv_a, dk_b, dv_b = carry
            for i in (2 * ip, 2 * ip + 1):
                dk_a, dv_a, dq_a = tile(k_a, kt_ref[0], v_a, i, dk_a, dv_a, False)
                dk_b, dv_b, dq_b = tile(k_b, kt_ref[1], v_b, i, dk_b, dv_b, False)
                add_dq(i, dq_a + dq_b)
            return dk_a, dv_a, dk_b, dv_b

        zero = jnp.zeros((t, HEAD_PAD), F32)
        dk_a, dv_a, dq_a = tile(k_a, kt_ref[0], v_a, 2 * jp, zero, zero, True)
        add_dq(2 * jp, dq_a)
        dk_a, dv_a, dq_a = tile(k_a, kt_ref[0], v_a, 2 * jp + 1, dk_a, dv_a, False)
        dk_b, dv_b, dq_b = tile(k_b, kt_ref[1], v_b, 2 * jp + 1, zero, zero, True)
        add_dq(2 * jp + 1, dq_a + dq_b)
        dk_a, dv_a, dk_b, dv_b = lax.fori_loop(jp + 1, nq // 2, both, (dk_a, dv_a, dk_b, dv_b))
        dk_ref[0:t, :] = (dk_a * (1.0 / LOG2E)).astype(BF16)
        dk_ref[t:2 * t, :] = (dk_b * (1.0 / LOG2E)).astype(BF16)
        dv_ref[0:t, :] = dv_a.astype(BF16)
        dv_ref[t:2 * t, :] = dv_b.astype(BF16)

        @pl.when((head == heads - 1) & (jp == nq // 2 - 1))
        def _():
            copies = _peer_copies(grad_refs, recv_refs, send_sems, recv_sems)
            for cp in copies:
                cp.wait_recv()
            for cp in copies:
                cp.wait_send()

    whole = pl.BlockSpec((None, seq, HEAD_PAD), lambda h, j: (h, 0, 0))
    whole_t = pl.BlockSpec((None, nq, HEAD_PAD, t), lambda h, j: (h, 0, 0, 0))
    pair = pl.BlockSpec((None, 2 * t, HEAD_PAD), lambda h, j: (h, j, 0))
    pair_t = pl.BlockSpec((None, 2, HEAD_PAD, t), lambda h, j: (h, j, 0, 0))
    stats = pl.BlockSpec((None, nq, 1, t), lambda h, j: (h, 0, 0, 0))
    return pl.pallas_call(
        body, name="attn_bwd", grid=(heads, nq // 2),
        in_specs=[whole, pair, pair_t, pair, whole, stats, stats] + [ANY] * n_w,
        out_specs=[whole_t, pair, pair] + [ANY] * n_w,
        out_shape=[_sds((heads, nq, HEAD_PAD, t), F32), _sds((heads, seq, HEAD_PAD), BF16),
                   _sds((heads, seq, HEAD_PAD), BF16)]
                  + [_sds((N_DEV - 1,) + g.shape[1:], g.dtype) for g in early_grads],
        scratch_shapes=[pltpu.SemaphoreType.DMA(((N_DEV - 1) * n_w,)), pltpu.SemaphoreType.DMA(((N_DEV - 1) * n_w,))],
        compiler_params=_params(2, 56),
    )(q, k, k_t, v, do, lse, delta, *early_grads)


def _merge_fwd(attn, pooled, gates, x, g_post_mix, w_ba, w_bb, w_out, tm):
    seq = x.shape[0]

    def body(attn_ref, pooled_ref, gates_ref, x_ref, g_ref, wba_ref, wbb_ref, wout_ref,
             merged_ref, ba_ref, bb_ref, y_ref, h1_ref, attn_rows_ref):
        attn = jnp.concatenate([attn_ref[h] for h in range(N_HEADS)], axis=1)
        attn_rows_ref[...] = attn
        ba = _mm(attn, wba_ref[...])
        bb = _mm(pooled_ref[...], wbb_ref[...])
        ba_ref[...] = ba.astype(BF16)
        bb_ref[...] = bb.astype(BF16)
        merged = (gates_ref[:, :D_MODEL].astype(F32) * ba + gates_ref[:, D_MODEL:].astype(F32) * bb).astype(BF16)
        merged_ref[...] = merged
        y = _mm(merged, wout_ref[...])
        y_ref[...] = y
        h1_ref[...] = x_ref[...] + y * _rms_r(y) * g_ref[...]

    return pl.pallas_call(
        body, name="merge_fwd", grid=(seq // tm,),
        in_specs=[pl.BlockSpec((N_HEADS, tm, HEAD_PAD), lambda i: (0, i, 0)), _row(tm, POOL_WIDTH),
                  _row(tm, 2 * D_MODEL), _row(tm, D_MODEL),
                  _fix((1, D_MODEL)), _fix(w_ba.shape), _fix(w_bb.shape), _fix(w_out.shape)],
        out_specs=[_row(tm, D_MODEL)] * 5 + [_row(tm, N_HEADS * HEAD_PAD)],
        out_shape=[_sds((seq, D_MODEL), BF16), _sds((seq, D_MODEL), BF16), _sds((seq, D_MODEL), BF16),
                   _sds((seq, D_MODEL), F32), _sds((seq, D_MODEL), F32), _sds((seq, N_HEADS * HEAD_PAD), BF16)],
        compiler_params=_params(1, 48),
    )(attn, pooled, gates, x, g_post_mix, w_ba, w_bb, w_out)


def _tail_fwd(h1, target, p, g_pre_mlp, g_post_mlp, g_ple, w_ff1, w_ff2, w_pe, w_pg, tm):
    seq = h1.shape[0]

    def body(h1_ref, tgt_ref, p_ref, gm_ref, gf_ref, gp_ref, w1_hbm, w2_hbm, wpe_hbm, wpg_hbm,
             m_ref, zr_ref, f_ref, h2b_ref, pb_ref, de_ref, dpre_ref, dh2_ref, loss_ref, dgple_ref,
             w1, w2, wpe, wpg):
        _load_once(((w1_hbm, w1), (w2_hbm, w2), (wpe_hbm, wpe), (wpg_hbm, wpg)))

        @pl.when(pl.program_id(0) == 0)
        def _():
            loss_ref[...] = jnp.zeros_like(loss_ref)
            dgple_ref[...] = jnp.zeros_like(dgple_ref)

        h1v = h1_ref[...]
        m = (h1v * _rms_r(h1v) * gm_ref[...]).astype(BF16)
        m_ref[...] = m
        zr = jnp.maximum(_mm(m, w1[...]), 0.0)
        zr_ref[...] = zr.astype(BF16)
        f = _mm((zr * zr).astype(BF16), w2[...])
        f_ref[...] = f
        h2 = h1v + f * _rms_r(f) * gf_ref[...]
        h2b = h2.astype(BF16)
        h2b_ref[...] = h2b
        pb = p_ref[...].astype(BF16)
        pb_ref[...] = pb
        e = _mm(pb, wpe[...])
        pg = _sigmoid(_mm(h2b, wpg[...]))
        t3 = pg * e
        r3 = _rms_r(t3)
        t3hat = t3 * r3
        diff = h2 + t3hat * gp_ref[...] - tgt_ref[...]
        loss_ref[...] += jnp.sum(diff * diff) * (0.5 / D_MODEL)
        dh3 = diff * (1.0 / D_MODEL)
        dgple_ref[...] += _colsum(dh3 * t3hat)
        dt3 = _rms_bwd(t3hat, r3, gp_ref[...], dh3)
        de_ref[...] = (dt3 * pg).astype(BF16)
        dpre = (dt3 * e * pg * (1.0 - pg)).astype(BF16)
        dpre_ref[...] = dpre
        dh2_ref[...] = dh3 + _mm_nt(dpre, wpg[...])

    return pl.pallas_call(
        body, name="tail_fwd", grid=(seq // tm,),
        in_specs=[_row(tm, D_MODEL), _row(tm, D_MODEL), _row(tm, PLE_DIM), _fix((1, D_MODEL)), _fix((1, D_MODEL)),
                  _fix((1, D_MODEL)), ANY, ANY, ANY, ANY],
        out_specs=[_row(tm, D_MODEL), _row(tm, D_FF), _row(tm, D_MODEL), _row(tm, D_MODEL), _row(tm, PLE_DIM),
                   _row(tm, D_MODEL), _row(tm, D_MODEL), _row(tm, D_MODEL), _fix((8, 128)), _fix((1, D_MODEL))],
        out_shape=[_sds((seq, D_MODEL), BF16), _sds((seq, D_FF), BF16), _sds((seq, D_MODEL), F32),
                   _sds((seq, D_MODEL), BF16), _sds((seq, PLE_DIM), BF16), _sds((seq, D_MODEL), BF16),
                   _sds((seq, D_MODEL), BF16), _sds((seq, D_MODEL), F32), _sds((8, 128), F32), _sds((1, D_MODEL), F32)],
        scratch_shapes=[pltpu.VMEM(w_ff1.shape, BF16), pltpu.VMEM(w_ff2.shape, BF16),
                        pltpu.VMEM(w_pe.shape, BF16), pltpu.VMEM(w_pg.shape, BF16)],
        compiler_params=_params(1, 56),
    )(h1, target, p, g_pre_mlp, g_post_mlp, g_ple, w_ff1, w_ff2, w_pe, w_pg)


def _mlp_bwd(h1, f, zr, dh2, g_pre_mlp, g_post_mlp, w_ff1, w_ff2, tm):
    seq = h1.shape[0]

    def body(h1_ref, f_ref, zr_ref, dh2_ref, gm_ref, gf_ref, w1_hbm, w2_hbm,
             df_ref, dz_ref, dh1_ref, dgm_ref, dgf_ref, w1, w2):
        _load_once(((w1_hbm, w1), (w2_hbm, w2)))

        @pl.when(pl.program_id(0) == 0)
        def _():
            dgm_ref[...] = jnp.zeros_like(dgm_ref)
            dgf_ref[...] = jnp.zeros_like(dgf_ref)

        dh2 = dh2_ref[...]
        fv = f_ref[...]
        rf = _rms_r(fv)
        fhat = fv * rf
        dgf_ref[...] += _colsum(dh2 * fhat)
        df = _rms_bwd(fhat, rf, gf_ref[...], dh2).astype(BF16)
        df_ref[...] = df
        dz = (_mm_nt(df, w2[...]) * (2.0 * zr_ref[...].astype(F32))).astype(BF16)
        dz_ref[...] = dz
        dm = _mm_nt(dz, w1[...])
        h1v = h1_ref[...]
        r1 = _rms_r(h1v)
        h1hat = h1v * r1
        dgm_ref[...] += _colsum(dm * h1hat)
        dh1_ref[...] = dh2 + _rms_bwd(h1hat, r1, gm_ref[...], dm)

    return pl.pallas_call(
        body, name="mlp_bwd", grid=(seq // tm,),
        in_specs=[_row(tm, D_MODEL), _row(tm, D_MODEL), _row(tm, D_FF), _row(tm, D_MODEL),
                  _fix((1, D_MODEL)), _fix((1, D_MODEL)), ANY, ANY],
        out_specs=[_row(tm, D_MODEL), _row(tm, D_FF), _row(tm, D_MODEL), _fix((1, D_MODEL)), _fix((1, D_MODEL))],
        out_shape=[_sds((seq, D_MODEL), BF16), _sds((seq, D_FF), BF16), _sds((seq, D_MODEL), F32),
                   _sds((1, D_MODEL), F32), _sds((1, D_MODEL), F32)],
        scratch_shapes=[pltpu.VMEM(w_ff1.shape, BF16), pltpu.VMEM(w_ff2.shape, BF16)],
        compiler_params=_params(1, 56),
    )(h1, f, zr, dh2, g_pre_mlp, g_post_mlp, w_ff1, w_ff2)


def _merge_bwd(dh1, y, gates, ba, bb, d_pool, g_post_mix, pool_scale, w_out, w_ba, w_bb, w_pool_bf, tm):
    seq = dh1.shape[0]

    def body(dh1_ref, y_ref, gates_ref, ba_ref, bb_ref, d_ref, g_ref, ps_ref, wout_ref, wba_ref, wbb_ref, wp_ref,
             dy_ref, dba_ref, dbb_ref, dgpre_ref, dattn_ref, dyp_ref, dd_ref, dg_ref, dbg_ref, dps_ref):
        @pl.when(pl.program_id(0) == 0)
        def _():
            dg_ref[...] = jnp.zeros_like(dg_ref)
            dbg_ref[...] = jnp.zeros_like(dbg_ref)
            dps_ref[...] = jnp.zeros_like(dps_ref)

        dh1v = dh1_ref[...]
        yv = y_ref[...]
        r = _rms_r(yv)
        yhat = yv * r
        dg_ref[...] += _colsum(dh1v * yhat)
        dy = _rms_bwd(yhat, r, g_ref[...], dh1v).astype(BF16)
        dy_ref[...] = dy
        dmerged = _mm_nt(dy, wout_ref[...])
        for half, branch_ref, dbranch_ref in ((0, ba_ref, dba_ref), (1, bb_ref, dbb_ref)):
            lanes = slice(D_MODEL * half, D_MODEL * (half + 1))
            gate = gates_ref[:, lanes].astype(F32)
            dpre = dmerged * branch_ref[...].astype(F32) * gate * (1.0 - gate)
            dbg_ref[:, lanes] += _colsum(dpre)
            dgpre_ref[:, lanes] = dpre.astype(BF16)
            dbranch_ref[...] = (dmerged * gate).astype(BF16)
        dattn = _mm_nt(dba_ref[...], wba_ref[...]).astype(BF16)
        for h in range(N_HEADS):
            dattn_ref[h] = dattn[:, HEAD_PAD * h:HEAD_PAD * (h + 1)]
        dpooled = _mm_nt(dbb_ref[...], wbb_ref[...])
        for g in range(len(POOL_WINDOWS)):
            lanes = slice(POOL_GROUP * g, POOL_GROUP * (g + 1))
            dpl = dpooled[:, lanes]
            dps_ref[:, lanes] += _colsum(dpl * _mm(d_ref[:, lanes], wp_ref[g]))
            dyp = (dpl * ps_ref[:, lanes]).astype(BF16)
            dyp_ref[:, lanes] = dyp
            dd_ref[:, lanes] = _mm_nt(dyp, wp_ref[g])

    return pl.pallas_call(
        body, name="merge_bwd", grid=(seq // tm,),
        in_specs=[_row(tm, D_MODEL), _row(tm, D_MODEL), _row(tm, 2 * D_MODEL), _row(tm, D_MODEL), _row(tm, D_MODEL),
                  _row(tm, POOL_WIDTH), _fix((1, D_MODEL)), _fix((1, POOL_WIDTH)),
                  _fix(w_out.shape), _fix(w_ba.shape), _fix(w_bb.shape), _fix(w_pool_bf.shape)],
        out_specs=[_row(tm, D_MODEL), _row(tm, D_MODEL), _row(tm, D_MODEL), _row(tm, 2 * D_MODEL),
                   pl.BlockSpec((N_HEADS, tm, HEAD_PAD), lambda i: (0, i, 0)), _row(tm, POOL_WIDTH), _row(tm, POOL_WIDTH),
                   _fix((1, D_MODEL)), _fix((1, 2 * D_MODEL)), _fix((1, POOL_WIDTH))],
        out_shape=[_sds((seq, D_MODEL), BF16), _sds((seq, D_MODEL), BF16), _sds((seq, D_MODEL), BF16),
                   _sds((seq, 2 * D_MODEL), BF16), _sds((N_HEADS, seq, HEAD_PAD), BF16), _sds((seq, POOL_WIDTH), BF16),
                   _sds((seq, POOL_WIDTH), F32), _sds((1, D_MODEL), F32), _sds((1, 2 * D_MODEL), F32),
                   _sds((1, POOL_WIDTH), F32)],
        compiler_params=_params(1, 48),
    )(dh1, y, gates, ba, bb, d_pool, g_post_mix, pool_scale, w_out, w_ba, w_bb, w_pool_bf)


def _proj_bwd(dq, dk, dv, qd, kvd, x, dh1, dgpre, dpin, cf, s1, s2, g_pre, g_q, g_kv,
              w_in_r, w_uq_r, w_k_exp, w_v, tm):
    seq = x.shape[0]

    def body(dq_ref, dk_ref, dv_ref, qd_ref, kvd_ref, x_ref, dh1_ref, dgpre_ref, dpin_ref, cf_ref, s1_ref, s2_ref,
             gpre_ref, gq_ref, gkv_ref, win_hbm, wuq_hbm, wk_hbm, wv_hbm,
             gx_ref, dproj_ref, dqb_ref, qn_ref, kvn_ref, dkb_ref, dvb_ref, dgpre_acc, dgq_acc, dgkv_acc,
             win, wuq, wk, wv):
        _load_once(((win_hbm, win), (wuq_hbm, wuq), (wk_hbm, wk), (wv_hbm, wv)))

        @pl.when(pl.program_id(0) == 0)
        def _():
            dgpre_acc[...] = jnp.zeros_like(dgpre_acc)
            dgq_acc[...] = jnp.zeros_like(dgq_acc)
            dgkv_acc[...] = jnp.zeros_like(dgkv_acc)

        cfv, s1v, s2v = cf_ref[...], s1_ref[...], s2_ref[...]
        ksum = jnp.zeros((tm, HEAD_PAD), F32)
        for h in range(N_HEADS):
            lanes = slice(HEAD_PAD * h, HEAD_PAD * (h + 1))
            dqb_ref[:, lanes] = (_rope_t(jnp.transpose(dq_ref[h]), cfv, s1v, s2v) * ATTN_SCALE).astype(BF16)
            dkh = dk_ref[h]
            dkb_ref[:, lanes] = dkh
            dvb_ref[:, lanes] = dv_ref[h]
            ksum = ksum + dkh.astype(F32)
        lane = lax.broadcasted_iota(jnp.int32, (tm, HEAD_PAD), 1)
        rope_lanes = (lane >= QK_NOPE) & (lane < QK_NOPE + QK_ROPE)
        dkr = _rope_t(jnp.where(rope_lanes, ksum, 0.0), cfv, s1v, s2v)

        qdv = qd_ref[...]
        rq = _rms_r(qdv)
        qhat = qdv * rq
        qn_ref[...] = (qhat * gq_ref[...]).astype(BF16)
        dqn = _mm_nt(dqb_ref[...], wuq[...])
        dgq_acc[...] += _colsum(dqn * qhat)
        dproj_ref[:, IN_Q0:IN_KV0] = _rms_bwd(qhat, rq, gq_ref[...], dqn).astype(BF16)

        kvdv = kvd_ref[...]
        rkv = _rms_r(kvdv)
        kvhat = kvdv * rkv
        kvn_ref[...] = (kvhat * gkv_ref[...]).astype(BF16)
        dkvn = _mm_nt(dkb_ref[...], wk[...]) + _mm_nt(dvb_ref[...], wv[...])
        dgkv_acc[...] += _colsum(dkvn * kvhat)
        dproj_ref[:, IN_KV0:IN_POOL0] = _rms_bwd(kvhat, rkv, gkv_ref[...], dkvn).astype(BF16)

        dproj_ref[:, IN_POOL0:IN_GATE0] = dpin_ref[...]
        dproj_ref[:, IN_GATE0:IN_KR0] = dgpre_ref[...]
        dproj_ref[:, IN_KR0:IN_R] = dkr.astype(BF16)

        da = _mm_nt(dproj_ref[...], win[...])
        xv = x_ref[...]
        r0 = _rms_r(xv)
        xhat = xv * r0
        dgpre_acc[...] += _colsum(da * xhat)
        gx_ref[...] = dh1_ref[...] + _rms_bwd(xhat, r0, gpre_ref[...], da)

    per_tile = ATTN_TILE // tm
    heads = pl.BlockSpec((N_HEADS, tm, HEAD_PAD), lambda i: (0, i, 0))
    heads_t = pl.BlockSpec((N_HEADS, None, HEAD_PAD, tm), lambda i: (0, i // per_tile, 0, i % per_tile))
    return pl.pallas_call(
        body, name="proj_bwd", grid=(seq // tm,),
        in_specs=[heads_t, heads, heads, _row(tm, Q_LORA), _row(tm, KV_LORA), _row(tm, D_MODEL),
                  _row(tm, D_MODEL), _row(tm, 2 * D_MODEL), _row(tm, POOL_WIDTH),
                  _row(tm, HEAD_PAD), _row(tm, HEAD_PAD), _row(tm, HEAD_PAD),
                  _fix((1, D_MODEL)), _fix((1, Q_LORA)), _fix((1, KV_LORA)), ANY, ANY, ANY, ANY],
        out_specs=[_row(tm, D_MODEL), _row(tm, IN_R), _row(tm, N_HEADS * HEAD_PAD), _row(tm, Q_LORA), _row(tm, KV_LORA),
                   _row(tm, N_HEADS * HEAD_PAD), _row(tm, N_HEADS * HEAD_PAD),
                   _fix((1, D_MODEL)), _fix((1, Q_LORA)), _fix((1, KV_LORA))],
        out_shape=[_sds((seq, D_MODEL), F32), _sds((seq, IN_R), BF16), _sds((seq, N_HEADS * HEAD_PAD), BF16),
                   _sds((seq, Q_LORA), BF16), _sds((seq, KV_LORA), BF16), _sds((seq, N_HEADS * HEAD_PAD), BF16),
                   _sds((seq, N_HEADS * HEAD_PAD), BF16),
                   _sds((1, D_MODEL), F32), _sds((1, Q_LORA), F32), _sds((1, KV_LORA), F32)],
        scratch_shapes=[pltpu.VMEM(w_in_r.shape, BF16), pltpu.VMEM(w_uq_r.shape, BF16),
                        pltpu.VMEM(w_k_exp.shape, BF16), pltpu.VMEM(w_v.shape, BF16)],
        compiler_params=_params(1, 48),
    )(dq, dk, dv, qd, kvd, x, dh1, dgpre, dpin, cf, s1, s2, g_pre, g_q, g_kv, w_in_r, w_uq_r, w_k_exp, w_v)


def _grad_w(a, b, name, square_a=False, by_device=False):
    seq, k_dim = a.shape
    n_dim = b.shape[1]
    tk = min(k_dim, 1024)
    tn = n_dim // 2 if n_dim == IN_R else min(n_dim, 1024)
    ts = min(seq, 1024)
    shard = n_dim // N_DEV
    per_tile = tn // shard
    if by_device:
        out_spec = pl.BlockSpec((per_tile, tk, shard), lambda i, j, s: (j, i, 0))
        out_shape = _sds((N_DEV, k_dim, shard), F32)
    else:
        out_spec = pl.BlockSpec((tk, tn), lambda i, j, s: (i, j))
        out_shape = _sds((k_dim, n_dim), F32)

    def body(a_ref, b_ref, o_ref):
        @pl.when(pl.program_id(2) == 0)
        def _():
            o_ref[...] = jnp.zeros_like(o_ref)

        at = a_ref[...]
        if square_a:
            at = at * at
        part = lax.dot_general(at, b_ref[...], TN, preferred_element_type=F32)
        if by_device:
            for d in range(per_tile):
                o_ref[d] += part[:, d * shard:(d + 1) * shard]
        else:
            o_ref[...] += part

    return pl.pallas_call(
        body, name=name, grid=(k_dim // tk, n_dim // tn, seq // ts),
        in_specs=[pl.BlockSpec((ts, tk), lambda i, j, s: (s, i)), pl.BlockSpec((ts, tn), lambda i, j, s: (s, j))],
        out_specs=out_spec, out_shape=out_shape,
        compiler_params=_params(3, 48),
    )(a, b)


def _grad_w_pool(d, dyp):
    seq = d.shape[0]
    ts = min(seq, 1024)

    def body(a_ref, b_ref, o_ref):
        @pl.when(pl.program_id(1) == 0)
        def _():
            o_ref[...] = jnp.zeros_like(o_ref)

        o_ref[...] += lax.dot_general(a_ref[...], b_ref[...], TN, preferred_element_type=F32)

    tile = pl.BlockSpec((ts, POOL_GROUP), lambda g, s: (s, g))
    return pl.pallas_call(
        body, name="grad_w_pool", grid=(len(POOL_WINDOWS), seq // ts),
        in_specs=[tile, tile],
        out_specs=pl.BlockSpec((None, POOL_GROUP, POOL_GROUP), lambda g, s: (g, 0, 0)),
        out_shape=_sds((len(POOL_WINDOWS), POOL_GROUP, POOL_GROUP), F32),
        compiler_params=_params(2, 32),
    )(d, dyp)


def _position():
    return lax.axis_index("x"), lax.axis_index("y"), lax.axis_index("c")


def _gather_copies(x_ref, slot, send_sems, recv_sems, local_sem, phases=("send", "forward", "finish")):
    x, y, c = _position()
    me, sibling = (x, y, c), (x, y, 1 - c)
    chips = [(1 - x, y), (x, 1 - y), (1 - x, 1 - y)]

    def copy(k, block, to, src=None):
        return pltpu.make_async_remote_copy(
            src_ref=slot(*block) if src is None else src, dst_ref=slot(*block),
            send_sem=send_sems.at[k], recv_sem=recv_sems.at[k], device_id=to, device_id_type=MESH)

    mine = pltpu.make_async_copy(x_ref, slot(*me), local_sem)
    first = [copy(0, me, sibling, src=x_ref)]
    first += [copy(1 + j, me, (*chip, c), src=x_ref) for j, chip in enumerate(chips)]
    passed = [copy(4 + j, (*chip, c), sibling) for j, chip in enumerate(chips)]
    if "send" in phases:
        mine.start()
        for cp in first:
            cp.start()
    if "forward" in phases:
        for j, chip in enumerate(chips):
            copy(1 + j, (*chip, c), me).wait_recv()
            passed[j].start()
    if "finish" in phases:
        copy(0, sibling, me).wait_recv()
        for j, chip in enumerate(chips):
            copy(4 + j, (*chip, 1 - c), me).wait_recv()
        for cp in first + passed:
            cp.wait_send()
        mine.wait()


def _all_gather_hbm(block):
    def body(x_ref, out_ref, send_sems, recv_sems, local_sem):
        _gather_copies(x_ref, lambda px, py, pc: out_ref.at[4 * px + 2 * py + pc], send_sems, recv_sems, local_sem)

    return pl.pallas_call(
        body, name="gather_weights",
        in_specs=[ANY], out_specs=ANY,
        out_shape=_sds((N_DEV,) + block.shape, block.dtype),
        scratch_shapes=[pltpu.SemaphoreType.DMA((7,)), pltpu.SemaphoreType.DMA((7,)), pltpu.SemaphoreType.DMA],
    )(block)


def _all_reduce_small(block):
    def body(x_ref, out_ref, buf, send_sems, recv_sems, local_sem):
        _gather_copies(x_ref, lambda px, py, pc: buf.at[4 * px + 2 * py + pc], send_sems, recv_sems, local_sem)
        acc = buf[0]
        for k in range(1, N_DEV):
            acc = acc + buf[k]
        out_ref[...] = acc

    vmem = pl.BlockSpec(memory_space=pltpu.VMEM)
    return pl.pallas_call(
        body, name="all_reduce_small",
        in_specs=[vmem], out_specs=vmem,
        out_shape=_sds(block.shape, F32),
        scratch_shapes=[pltpu.VMEM((N_DEV,) + block.shape, F32), pltpu.SemaphoreType.DMA((7,)),
                        pltpu.SemaphoreType.DMA((7,)), pltpu.SemaphoreType.DMA],
        compiler_params=pltpu.CompilerParams(vmem_limit_bytes=32 * MIB),
    )(block)


def _exchange_pair(gs):
    n_w = len(gs)

    def body(*refs):
        g_refs, out_refs = refs[:n_w], refs[n_w:2 * n_w]
        send_sems, recv_sems = refs[2 * n_w:]
        x, y, c = _position()
        copies = []
        for w in range(n_w):
            for chip in range(4):
                cp = pltpu.make_async_remote_copy(
                    src_ref=g_refs[w].at[2 * chip + (1 - c)], dst_ref=out_refs[w].at[chip],
                    send_sem=send_sems.at[4 * w + chip], recv_sem=recv_sems.at[4 * w + chip],
                    device_id=(x, y, 1 - c), device_id_type=MESH)
                cp.start()
                copies.append(cp)
        for cp in copies:
            cp.wait_recv()
        for cp in copies:
            cp.wait_send()

    return pl.pallas_call(
        body, name="exchange_pair",
        in_specs=[ANY] * n_w, out_specs=[ANY] * n_w,
        out_shape=[_sds((4,) + g.shape[1:], g.dtype) for g in gs],
        scratch_shapes=[pltpu.SemaphoreType.DMA((4 * n_w,)), pltpu.SemaphoreType.DMA((4 * n_w,))],
    )(*gs)


def _exchange_chips(parts):
    n_w = len(parts)

    def body(*refs):
        p_refs, out_refs = refs[:n_w], refs[n_w:2 * n_w]
        send_sems, recv_sems = refs[2 * n_w:]
        x, y, c = _position()
        chips = [(1 - x, y), (x, 1 - y), (1 - x, 1 - y)]
        copies = []
        for w in range(n_w):
            for k, (px, py) in enumerate(chips):
                cp = pltpu.make_async_remote_copy(
                    src_ref=p_refs[w].at[2 * px + py], dst_ref=out_refs[w].at[k],
                    send_sem=send_sems.at[3 * w + k], recv_sem=recv_sems.at[3 * w + k],
                    device_id=(px, py, c), device_id_type=MESH)
                cp.start()
                copies.append(cp)
        for cp in copies:
            cp.wait_recv()
        for cp in copies:
            cp.wait_send()

    return pl.pallas_call(
        body, name="exchange_chips",
        in_specs=[ANY] * n_w, out_specs=[ANY] * n_w,
        out_shape=[_sds((3,) + p.shape[1:], p.dtype) for p in parts],
        scratch_shapes=[pltpu.SemaphoreType.DMA((3 * n_w,)), pltpu.SemaphoreType.DMA((3 * n_w,))],
    )(*parts)


def _row_tile(k):
    return 256 if k % 256 == 0 else 128


def _pair_sum(g, recv, place, name):
    _, k, n = g.shape
    tr = _row_tile(k)
    g4 = g.reshape(4, 2, k, n)

    def body(s_ref, g_ref, r_ref, o_ref):
        o_ref[...] = (g_ref[...] + r_ref[...]).astype(BF16)

    spec = pltpu.PrefetchScalarGridSpec(
        num_scalar_prefetch=1, grid=(4, k // tr),
        in_specs=[pl.BlockSpec((None, None, tr, n), lambda j, i, s: (j, s[2], i, 0)),
                  pl.BlockSpec((None, tr, n), lambda j, i, s: (j, i, 0))],
        out_specs=pl.BlockSpec((None, tr, n), lambda j, i, s: (j, i, 0)))
    return pl.pallas_call(
        body, name=name, grid_spec=spec, out_shape=_sds((4, k, n), BF16),
        compiler_params=_params(2, 32),
    )(place, g4, recv)


def _adamw_math(g, w, m, v):
    m = ADAM_B1 * m + (1.0 - ADAM_B1) * g
    v = ADAM_B2 * v + (1.0 - ADAM_B2) * (g * g)
    m_hat = m / (1.0 - ADAM_B1 ** ADAM_STEP)
    v_hat = v / (1.0 - ADAM_B2 ** ADAM_STEP)
    delta = -ADAM_LR * (m_hat / (jnp.sqrt(v_hat) + ADAM_EPS) + ADAM_WD * w)
    return delta, m, v


def _adamw_sharded(g, from_sibling, from_chips, place, w, m, v, name):
    _, k, n = g.shape
    tr = _row_tile(k)

    def body(s_ref, g_ref, sib_ref, r0_ref, r1_ref, r2_ref, w_ref, m_ref, v_ref, grad_ref, d_ref, nm_ref, nv_ref):
        grad = g_ref[...] + sib_ref[...]
        for r_ref in (r0_ref, r1_ref, r2_ref):
            grad = grad + r_ref[...].astype(F32)
        grad_ref[...] = grad
        d_ref[...], nm_ref[...], nv_ref[...] = _adamw_math(grad, w_ref[...], m_ref[...], v_ref[...])

    tile = pl.BlockSpec((None, tr, n), lambda i, s: (0, i, 0))

    def slot(j):
        return pl.BlockSpec((None, tr, n), lambda i, s: (j, i, 0))

    spec = pltpu.PrefetchScalarGridSpec(
        num_scalar_prefetch=1, grid=(k // tr,),
        in_specs=[pl.BlockSpec((None, tr, n), lambda i, s: (s[0], i, 0)),
                  pl.BlockSpec((None, tr, n), lambda i, s: (s[1], i, 0)),
                  slot(0), slot(1), slot(2), tile, tile, tile],
        out_specs=[tile] * 4)
    return pl.pallas_call(
        body, name=name, grid_spec=spec, out_shape=[_sds((1, k, n), F32)] * 4,
        compiler_params=_params(1, 48),
    )(place, g, from_sibling, from_chips, from_chips, from_chips, w, m, v)


def _adamw_direct(g, received, place, w, m, v, name):
    _, k, n = g.shape
    tr = _row_tile(k)

    def body(s_ref, g_ref, r_ref, w_ref, m_ref, v_ref, grad_ref, d_ref, nm_ref, nv_ref):
        grad = g_ref[...]
        for r in range(N_DEV - 1):
            grad = grad + r_ref[r].astype(F32)
        grad_ref[...] = grad
        d_ref[...], nm_ref[...], nv_ref[...] = _adamw_math(grad, w_ref[...], m_ref[...], v_ref[...])

    tile = pl.BlockSpec((None, tr, n), lambda i, s: (0, i, 0))
    spec = pltpu.PrefetchScalarGridSpec(
        num_scalar_prefetch=1, grid=(k // tr,),
        in_specs=[pl.BlockSpec((None, tr, n), lambda i, s: (s[0], i, 0)),
                  pl.BlockSpec((N_DEV - 1, tr, n), lambda i, s: (0, i, 0)), tile, tile, tile],
        out_specs=[tile] * 4)
    return pl.pallas_call(
        body, name=name, grid_spec=spec, out_shape=[_sds((1, k, n), F32)] * 4,
        compiler_params=_params(1, 48),
    )(place, g, received, w, m, v)


def _adamw_small(g, w, m, v):
    def body(g_ref, w_ref, m_ref, v_ref, d_ref, nm_ref, nv_ref):
        d_ref[...], nm_ref[...], nv_ref[...] = _adamw_math(g_ref[...], w_ref[...], m_ref[...], v_ref[...])

    return pl.pallas_call(body, name="adamw_small", out_shape=[_sds(g.shape, F32)] * 3)(g, w, m, v)


def _pack_rows(parts):
    parts = [a.reshape(-1, LANES) for a in parts]
    pad = (-sum(a.shape[0] for a in parts)) % PACK_ROW_TILE
    return jnp.concatenate(parts + [jnp.zeros((pad, LANES), parts[0].dtype)], axis=0)


def _pack_small(parts):
    flat = jnp.concatenate([a.reshape(-1) for a in parts])
    pad = (-flat.shape[0]) % (8 * LANES)
    return jnp.pad(flat, (0, pad)).reshape(-1, LANES)


def _unpack_small(packed, shapes):
    flat = packed.reshape(-1)
    out, off = [], 0
    for shape in shapes:
        size = 1
        for n in shape:
            size *= n
        out.append(flat[off:off + size].reshape(shape))
        off += size
    return out


def _full_from_gathered(gathered, entries, shard_shapes):
    out, off = {}, 0
    for (name, kind), (k, n) in zip(entries, shard_shapes):
        rows = k * n // LANES
        seg = gathered[:, off:off + rows].reshape(N_DEV, k, n)
        out[name] = jnp.transpose(seg, (1, 0, 2)).reshape(k, N_DEV * n) if kind == "col" else seg.reshape(N_DEV * k, n)
        off += rows
    return out


def _columns_by_device(a):
    k, n_all = a.shape
    return jnp.transpose(a.reshape(k, N_DEV, n_all // N_DEV), (1, 0, 2))


def _rows_by_device(a):
    k_all, n = a.shape
    return a.reshape(N_DEV, k_all // N_DEV, n)


def _rope_tables(positions):
    inv_freq = ROPE_THETA ** (-jnp.arange(0, QK_ROPE, 2, dtype=F32) / QK_ROPE)
    ang = positions.astype(F32)[:, None] * inv_freq
    cos, sin = jnp.cos(ang), jnp.sin(ang)
    seq = positions.shape[0]
    zeros = lambda n: jnp.zeros((seq, n), F32)
    cf = jnp.concatenate([jnp.ones((seq, QK_NOPE), F32), cos, cos, zeros(HEAD_PAD - QK_NOPE - QK_ROPE)], axis=1)
    s1 = jnp.concatenate([zeros(QK_NOPE), -sin, zeros(HEAD_PAD - QK_NOPE - QK_ROPE // 2)], axis=1)
    s2 = jnp.concatenate([zeros(QK_NOPE + QK_ROPE // 2), sin, zeros(HEAD_PAD - QK_NOPE - QK_ROPE)], axis=1)
    return cf, s1, s2


def _arrange_w_in(w):
    k = w.shape[0]
    zeros = lambda n: jnp.zeros((k, n), w.dtype)
    kr0 = Q_LORA + KV_LORA
    pool0 = kr0 + QK_ROPE
    return jnp.concatenate([w[:, :kr0], w[:, pool0:], zeros(QK_NOPE), w[:, kr0:pool0],
                            zeros(HEAD_PAD - QK_NOPE - QK_ROPE)], axis=1)


def _restore_w_in(d):
    kr = d[:, IN_KR0 + QK_NOPE:IN_KR0 + QK_NOPE + QK_ROPE]
    return jnp.concatenate([d[:, :IN_POOL0], kr, d[:, IN_POOL0:IN_KR0]], axis=1)


def _pad_heads(w, width):
    k = w.shape[0]
    w = w.reshape(k, N_HEADS, width)
    return jnp.pad(w, ((0, 0), (0, 0), (0, HEAD_PAD - width))).reshape(k, N_HEADS * HEAD_PAD)


def _unpad_heads(d, width):
    k = d.shape[0]
    return d.reshape(k, N_HEADS, HEAD_PAD)[:, :, :width]


def kernel(x, p, positions, g_pre_mix, w_in, b_gate, g_q, w_uq, g_kv, w_ukv, w_pool, pool_scale, w_branch_attn, w_branch_pool, w_out, g_post_mix, g_pre_mlp, w_ff1, w_ff2, g_post_mlp, w_ple_proj, w_ple_gate, g_ple, loss_target, m_g_pre_mix, m_w_in, m_b_gate, m_g_q, m_w_uq, m_g_kv, m_w_ukv, m_w_pool, m_pool_scale, m_w_branch_attn, m_w_branch_pool, m_w_out, m_g_post_mix, m_g_pre_mlp, m_w_ff1, m_w_ff2, m_g_post_mlp, m_w_ple_proj, m_w_ple_gate, m_g_ple, v_g_pre_mix, v_w_in, v_b_gate, v_g_q, v_w_uq, v_g_kv, v_w_ukv, v_w_pool, v_pool_scale, v_w_branch_attn, v_w_branch_pool, v_w_out, v_g_post_mix, v_g_pre_mlp, v_w_ff1, v_w_ff2, v_g_post_mlp, v_w_ple_proj, v_w_ple_gate, v_g_ple):
    given = dict(locals())
    weights = {n: given[n] for n in WEIGHT_ORDER}
    moments_m = {n: given["m_" + n] for n in WEIGHT_ORDER}
    moments_v = {n: given["v_" + n] for n in WEIGHT_ORDER}
    xs, ps, target = x[0], p[0, 0], loss_target[0]
    seq = xs.shape[0]
    tm = min(256, seq)
    core = lax.axis_index("c")
    chip = 2 * lax.axis_index("x") + lax.axis_index("y")

    early, later = SHARDED[:N_EARLY], SHARDED[N_EARLY:]
    shapes_of = lambda entries: [weights[n].shape[1:] for n, _ in entries]
    pack_bf16 = lambda entries: _pack_rows([weights[n][0].astype(BF16) for n, _ in entries])
    full = _full_from_gathered(_all_gather_hbm(pack_bf16(early)), early, shapes_of(early))
    w_in_r = _arrange_w_in(full["w_in"])
    w_uq_r = _pad_heads(full["w_uq"], QK_NOPE + QK_ROPE)
    ukv = full["w_ukv"].reshape(KV_LORA, N_HEADS, QK_NOPE + V_HEAD)
    w_k_exp = _pad_heads(ukv[:, :, :QK_NOPE].reshape(KV_LORA, N_HEADS * QK_NOPE), QK_NOPE)
    w_v = _pad_heads(ukv[:, :, QK_NOPE:].reshape(KV_LORA, N_HEADS * V_HEAD), V_HEAD)
    w_pool_bf = w_pool[0].astype(BF16)
    cf, s1, s2 = _rope_tables(positions[0])

    a_bf, qd, kvd, pin, gates, q, k, v, k_t, gathered_later = _proj_fwd(
        xs, g_pre_mix, b_gate, g_q, g_kv, cf, s1, s2, w_in_r, w_uq_r, w_k_exp, w_v, pack_bf16(later), tm)
    full.update(_full_from_gathered(gathered_later, later, shapes_of(later)))
    w_ba = jnp.pad(full["w_branch_attn"].reshape(N_HEADS, V_HEAD, D_MODEL),
                   ((0, 0), (0, HEAD_PAD - V_HEAD), (0, 0))).reshape(N_HEADS * HEAD_PAD, D_MODEL)
    d_pool, pooled = _pool_fwd(pin, w_pool_bf, pool_scale, tm)
    o_heads, lse = _attn_fwd(q, k, v)
    merged, ba, bb, y, h1, attn_rows = _merge_fwd(o_heads, pooled, gates, xs, g_post_mix, w_ba,
                                                  full["w_branch_pool"], full["w_out"], tm)
    (m_bf, zr, f, h2_bf, p_bf, de, dpre, dh2, loss_acc, dg_ple) = _tail_fwd(
        h1, target, ps, g_pre_mlp, g_post_mlp, g_ple, full["w_ff1"], full["w_ff2"], full["w_ple_proj"],
        full["w_ple_gate"], tm)

    by_device = {
        "w_ple_proj": _grad_w(p_bf, de, "grad_w_ple_proj", by_device=True),
        "w_ple_gate": _rows_by_device(_grad_w(h2_bf, dpre, "grad_w_ple_gate")),
    }
    df, dz, dh1, dg_pre_mlp, dg_post_mlp = _mlp_bwd(h1, f, zr, dh2, g_pre_mlp, g_post_mlp, full["w_ff1"],
                                                    full["w_ff2"], tm)
    by_device["w_ff1"] = _grad_w(m_bf, dz, "grad_w_ff1", by_device=True)
    by_device["w_ff2"] = _rows_by_device(_grad_w(zr, df, "grad_w_ff2", square_a=True))
    (dy, dba, dbb, dgpre, do_heads, dyp, dd, dg_post_mix, db_gate, dpool_scale) = _merge_bwd(
        dh1, y, gates, ba, bb, d_pool, g_post_mix, pool_scale, full["w_out"], w_ba,
        full["w_branch_pool"], w_pool_bf, tm)
    d_w_ba = _grad_w(attn_rows, dba, "grad_w_branch_attn").reshape(N_HEADS, HEAD_PAD, D_MODEL)
    by_device["w_branch_attn"] = _columns_by_device(d_w_ba[:, :V_HEAD].reshape(N_HEADS * V_HEAD, D_MODEL))
    by_device["w_branch_pool"] = _grad_w(pooled, dbb, "grad_w_branch_pool", by_device=True)
    by_device["w_out"] = _rows_by_device(_grad_w(merged, dy, "grad_w_out"))
    dpin = _pool_bwd_window(dd, tm)
    delta = _attn_delta(o_heads, do_heads)
    direct = [n for n, _ in SHARDED[N_EARLY:]]
    outs = _attn_bwd(q, k, k_t, v, do_heads, lse, delta, [by_device[n].astype(BF16) for n in direct])
    dq, dk, dv = outs[:3]
    received = dict(zip(direct, outs[3:]))
    (grad_x, dproj, dq_bf, qn_bf, kvn_bf, dk_bf, dv_bf, dg_pre_mix, dg_q, dg_kv) = _proj_bwd(
        dq, dk, dv, qd, kvd, xs, dh1, dgpre, dpin, cf, s1, s2, g_pre_mix, g_q, g_kv, w_in_r, w_uq_r, w_k_exp, w_v, tm)
    d_k_exp = _unpad_heads(_grad_w(kvn_bf, dk_bf, "grad_w_uk"), QK_NOPE)
    d_w_v = _unpad_heads(_grad_w(kvn_bf, dv_bf, "grad_w_uv"), V_HEAD)
    by_device["w_in"] = _columns_by_device(_restore_w_in(_grad_w(a_bf, dproj, "grad_w_in")))
    by_device["w_uq"] = _columns_by_device(
        _unpad_heads(_grad_w(qn_bf, dq_bf, "grad_w_uq"), QK_NOPE + QK_ROPE).reshape(Q_LORA, -1))
    by_device["w_ukv"] = _columns_by_device(jnp.concatenate([d_k_exp, d_w_v], axis=2).reshape(KV_LORA, -1))
    grads_small = {
        "g_pre_mix": dg_pre_mix, "b_gate": db_gate, "g_q": dg_q, "g_kv": dg_kv,
        "w_pool": _grad_w_pool(d_pool, dyp), "pool_scale": dpool_scale, "g_post_mix": dg_post_mix,
        "g_pre_mlp": dg_pre_mlp, "g_post_mlp": dg_post_mlp, "g_ple": dg_ple,
    }

    names = [n for n, _ in SHARDED]
    place = jnp.stack([2 * chip + core, chip, core]).astype(jnp.int32)
    sharded = {n: _adamw_direct(by_device[n], received[n], place, weights[n], moments_m[n], moments_v[n],
                                "adamw_" + n) for n in direct}
    last = [n for n, _ in SHARDED[:N_EARLY]]
    own = [by_device[n] for n in last]
    from_sibling = _exchange_pair(own)
    pair = [_pair_sum(g, r, place, "pair_sum_" + n) for n, g, r in zip(last, own, from_sibling)]
    from_chips = _exchange_chips(pair)
    sharded.update({n: _adamw_sharded(g, r, rc, place, weights[n], moments_m[n], moments_v[n], "adamw_" + n)
                    for n, g, r, rc in zip(last, own, from_sibling, from_chips)})

    g_sm = _all_reduce_small(_pack_small([grads_small[n] for n in REPLICATED] + [loss_acc[0:1, 0:1]]))
    n_small = sum(weights[n].size for n in REPLICATED)
    d_sm, m_sm, v_sm = _adamw_small(g_sm, _pack_small([weights[n] for n in REPLICATED]),
                                    _pack_small([moments_m[n] for n in REPLICATED]),
                                    _pack_small([moments_v[n] for n in REPLICATED]))

    small_shapes = [weights[n].shape for n in REPLICATED]
    results = []
    for which, small in enumerate((g_sm, d_sm, m_sm, v_sm)):
        named = {n: sharded[n][which] for n in names}
        named.update(zip(REPLICATED, _unpack_small(small, small_shapes)))
        results.append([named[n] for n in WEIGHT_ORDER])

    loss = g_sm.reshape(-1)[n_small]
    return (loss, grad_x[None], *results[0], *results[1], *results[2], *results[3])
```

```python
import jax
import jax.numpy as jnp
from jax import lax
from jax.experimental import pallas as pl
from jax.experimental.pallas import tpu as pltpu

F32 = jnp.float32
BF16 = jnp.bfloat16

D_MODEL = 1024
PLE_DIM = 256
N_HEADS = 8
QK_NOPE = 64
QK_ROPE = 32
V_HEAD = 64
Q_LORA = 384
KV_LORA = 256
POOL_WINDOWS = (2, 4, 8, 16)
POOL_GROUP = 128
POOL_WIDTH = 512
D_FF = 4096
ROPE_THETA = 10000.0
EPS = 1e-6
HEAD_PAD = 128
ATTN_SCALE = (QK_NOPE + QK_ROPE) ** -0.5
LOG2E = 1.4426950408889634
Q_PRESCALE = ATTN_SCALE * LOG2E
ATTN_TILE = 512
FWD_ROWS = 512
FWD_CHAINS = 2

ADAM_LR = 0.001
ADAM_B1 = 0.9
ADAM_B2 = 0.999
ADAM_EPS = 1e-08
ADAM_WD = 0.01
ADAM_STEP = 10

N_DEV = 8
LANES = 1024
PACK_ROW_TILE = 480
POOL_HALO = 16
MIB = 2 ** 20

IN_Q0, IN_KV0, IN_POOL0, IN_GATE0, IN_KR0, IN_R = 0, 384, 640, 1152, 3200, 3328

SHARDED = (("w_in", "col"), ("w_uq", "col"), ("w_ukv", "col"), ("w_branch_attn", "col"),
           ("w_branch_pool", "col"), ("w_out", "row"), ("w_ff1", "col"), ("w_ff2", "row"),
           ("w_ple_proj", "col"), ("w_ple_gate", "row"))
N_EARLY = 3
REPLICATED = ("g_pre_mix", "b_gate", "g_q", "g_kv", "w_pool", "pool_scale", "g_post_mix",
              "g_pre_mlp", "g_post_mlp", "g_ple")
WEIGHT_ORDER = ("g_pre_mix", "w_in", "b_gate", "g_q", "w_uq", "g_kv", "w_ukv", "w_pool", "pool_scale",
                "w_branch_attn", "w_branch_pool", "w_out", "g_post_mix", "g_pre_mlp", "w_ff1", "w_ff2",
                "g_post_mlp", "w_ple_proj", "w_ple_gate", "g_ple")

NT = (((1,), (1,)), ((), ()))
TN = (((0,), (0,)), ((), ()))
MESH = pl.DeviceIdType.MESH
ANY = pl.BlockSpec(memory_space=pl.ANY)


def _params(n_axes, vmem_mib):
    return pltpu.CompilerParams(dimension_semantics=("arbitrary",) * n_axes, vmem_limit_bytes=vmem_mib * MIB)


def _row(tm, n):
    return pl.BlockSpec((tm, n), lambda i: (i, 0))


def _fix(shape):
    zeros = (0,) * len(shape)
    return pl.BlockSpec(shape, lambda i: zeros)


def _sds(shape, dtype):
    return jax.ShapeDtypeStruct(shape, dtype)


def _rms_r(v):
    return lax.rsqrt(jnp.mean(v * v, axis=-1, keepdims=True) + EPS)


def _rms_bwd(vhat, r, g, dy):
    gdy = dy * g
    return r * (gdy - vhat * jnp.mean(gdy * vhat, axis=-1, keepdims=True))


def _colsum(v):
    return jnp.sum(v, axis=0, keepdims=True)


def _sigmoid(v):
    return 1.0 / (1.0 + jnp.exp(-v))


def _mm(a, b):
    return jnp.dot(a, b, preferred_element_type=F32)


def _mm_nt(a, b):
    return lax.dot_general(a, b, NT, preferred_element_type=F32)


def _rope(c, cf, s1, s2):
    return c * cf + pltpu.roll(c, HEAD_PAD - 16, 1) * s1 + pltpu.roll(c, 16, 1) * s2


def _rope_t(c, cf, s1, s2):
    return c * cf + pltpu.roll(c * s1, 16, 1) + pltpu.roll(c * s2, HEAD_PAD - 16, 1)


def _load_once(pairs):
    @pl.when(pl.program_id(0) == 0)
    def _():
        for src, dst in pairs:
            pltpu.sync_copy(src, dst)


def _proj_fwd(x, g_pre, b_gate, g_q, g_kv, positions, w_in_r, w_uq_r, w_k_exp, w_v, later_shards, tm):
    seq = x.shape[0]
    n_steps = seq // tm
    forward_step = (3 * n_steps) // 4

    def body(x_ref, gpre_ref, bg_ref, gq_ref, gkv_ref, pos_ref, freq_ref, win_hbm, wuq_hbm, wk_hbm, wv_hbm,
             later_ref, a_ref, qd_ref, kvd_ref, pin_ref, gates_ref, q_ref, k_ref, v_ref, kt_ref,
             cf_ref, s1_ref, s2_ref, gathered_ref, win, wuq, wk, wv, send_sems, recv_sems, local_sem):
        step = pl.program_id(0)

        def gather(phase):
            _gather_copies(later_ref, lambda px, py, pc: gathered_ref.at[4 * px + 2 * py + pc],
                           send_sems, recv_sems, local_sem, phases=(phase,))

        pl.when(step == 0)(lambda: gather("send"))
        pl.when(step == forward_step)(lambda: gather("forward"))
        _load_once(((win_hbm, win), (wuq_hbm, wuq), (wk_hbm, wk), (wv_hbm, wv)))
        xv = x_ref[...]
        a = (xv * _rms_r(xv) * gpre_ref[...]).astype(BF16)
        a_ref[...] = a
        proj = _mm(a, win[...])
        qd = proj[:, IN_Q0:IN_KV0]
        kvd = proj[:, IN_KV0:IN_POOL0]
        qd_ref[...] = qd
        kvd_ref[...] = kvd
        pin_ref[...] = proj[:, IN_POOL0:IN_GATE0]
        gates_ref[...] = _sigmoid(proj[:, IN_GATE0:IN_KR0] + bg_ref[...]).astype(BF16)
        cfv, s1v, s2v = _rope_tables(pos_ref[...], freq_ref[...], tm)
        cf_ref[...], s1_ref[...], s2_ref[...] = cfv, s1v, s2v
        krr = _rope(proj[:, IN_KR0:IN_R], cfv, s1v, s2v)
        qn = (qd * _rms_r(qd) * gq_ref[...]).astype(BF16)
        qf = _mm(qn, wuq[...])
        kvn = (kvd * _rms_r(kvd) * gkv_ref[...]).astype(BF16)
        kf = _mm(kvn, wk[...])
        vf = _mm(kvn, wv[...])
        one_lane = (lax.broadcasted_iota(jnp.int32, (tm, HEAD_PAD), 1) == V_HEAD).astype(F32)
        for h in range(N_HEADS):
            lanes = slice(HEAD_PAD * h, HEAD_PAD * (h + 1))
            q_ref[h] = (_rope(qf[:, lanes], cfv, s1v, s2v) * Q_PRESCALE).astype(BF16)
            kh = kf[:, lanes] + krr
            vh = vf[:, lanes] + one_lane
            k_ref[h] = kh.astype(BF16)
            v_ref[h] = vh.astype(BF16)
            kt_ref[h] = jnp.transpose(kh).astype(BF16)
        pl.when(step == n_steps - 1)(lambda: gather("finish"))

    per_tile = ATTN_TILE // tm
    heads = pl.BlockSpec((N_HEADS, tm, HEAD_PAD), lambda i: (0, i, 0))
    heads_t = pl.BlockSpec((N_HEADS, None, HEAD_PAD, tm), lambda i: (0, i // per_tile, 0, i % per_tile))
    heads_t_shape = _sds((N_HEADS, seq // ATTN_TILE, HEAD_PAD, ATTN_TILE), BF16)
    return pl.pallas_call(
        body, name="proj_fwd", grid=(seq // tm,),
        in_specs=[_row(tm, D_MODEL), _fix((1, D_MODEL)), _fix((1, 2 * D_MODEL)), _fix((1, Q_LORA)), _fix((1, KV_LORA)),
                  pl.BlockSpec((1, tm), lambda i: (0, i)), _fix((1, HEAD_PAD)), ANY, ANY, ANY, ANY, ANY],
        out_specs=[_row(tm, D_MODEL), _row(tm, Q_LORA), _row(tm, KV_LORA), _row(tm, POOL_WIDTH), _row(tm, 2 * D_MODEL),
                   heads, heads, heads, heads_t, _row(tm, HEAD_PAD), _row(tm, HEAD_PAD), _row(tm, HEAD_PAD), ANY],
        out_shape=[_sds((seq, D_MODEL), BF16), _sds((seq, Q_LORA), F32), _sds((seq, KV_LORA), F32),
                   _sds((seq, POOL_WIDTH), F32), _sds((seq, 2 * D_MODEL), BF16),
                   _sds((N_HEADS, seq, HEAD_PAD), BF16), _sds((N_HEADS, seq, HEAD_PAD), BF16),
                   _sds((N_HEADS, seq, HEAD_PAD), BF16), heads_t_shape,
                   _sds((seq, HEAD_PAD), F32), _sds((seq, HEAD_PAD), F32), _sds((seq, HEAD_PAD), F32),
                   _sds((N_DEV,) + later_shards.shape, later_shards.dtype)],
        scratch_shapes=[pltpu.VMEM(w_in_r.shape, BF16), pltpu.VMEM(w_uq_r.shape, BF16),
                        pltpu.VMEM(w_k_exp.shape, BF16), pltpu.VMEM(w_v.shape, BF16),
                        pltpu.SemaphoreType.DMA((7,)), pltpu.SemaphoreType.DMA((7,)), pltpu.SemaphoreType.DMA],
        compiler_params=_params(1, 48),
    )(x, g_pre, b_gate, g_q, g_kv, positions, _rope_lane_frequencies(), w_in_r, w_uq_r, w_k_exp, w_v, later_shards)


def _window_count(row0, n_rows):
    t = row0 + lax.broadcasted_iota(jnp.int32, (n_rows, POOL_GROUP), 0)
    return [jnp.minimum(t + 1, w).astype(F32) for w in POOL_WINDOWS]


def _pool_fwd(pin, w_pool_bf, pool_scale, tm):
    seq = pin.shape[0]
    ext_rows = tm + POOL_HALO

    def body(prev_ref, u_ref, wp_ref, ps_ref, d_ref, pooled_ref):
        i = pl.program_id(0)
        prev = jnp.where(i == 0, 0.0, prev_ref[...])
        u = u_ref[...]
        level = jnp.concatenate([prev, u], axis=0)
        counts = _window_count(i * tm, tm)
        shift = 1
        for g in range(len(POOL_WINDOWS)):
            level = level + pltpu.roll(level, shift, 0)
            shift *= 2
            lanes = slice(POOL_GROUP * g, POOL_GROUP * (g + 1))
            d = (level[POOL_HALO:, lanes] / counts[g] - u[:, lanes]).astype(BF16)
            d_ref[:, lanes] = d
            pooled_ref[:, lanes] = (_mm(d, wp_ref[g]) * ps_ref[:, lanes]).astype(BF16)

    halo = tm // POOL_HALO
    return pl.pallas_call(
        body, name="pool_fwd", grid=(seq // tm,),
        in_specs=[pl.BlockSpec((POOL_HALO, POOL_WIDTH), lambda i: (jnp.maximum(i * halo - 1, 0), 0)),
                  _row(tm, POOL_WIDTH), _fix(w_pool_bf.shape), _fix((1, POOL_WIDTH))],
        out_specs=[_row(tm, POOL_WIDTH), _row(tm, POOL_WIDTH)],
        out_shape=[_sds((seq, POOL_WIDTH), BF16), _sds((seq, POOL_WIDTH), BF16)],
        compiler_params=_params(1, 32),
    )(pin, pin, w_pool_bf, pool_scale)


def _pool_bwd_window(dd, tm):
    seq = dd.shape[0]
    n_tiles = seq // tm
    ext_rows = tm + POOL_HALO

    def body(dd_ref, next_ref, dpin_ref):
        i = pl.program_id(0)
        nxt = jnp.where(i == n_tiles - 1, 0.0, next_ref[...])
        dd_t = dd_ref[...]
        ext = jnp.concatenate([dd_t, nxt], axis=0)
        counts = _window_count(i * tm, ext_rows)
        shift = 1
        for g in range(len(POOL_WINDOWS)):
            lanes = slice(POOL_GROUP * g, POOL_GROUP * (g + 1))
            level = ext[:, lanes] / counts[g]
            s = 1
            while s <= shift:
                level = level + pltpu.roll(level, ext_rows - s, 0)
                s *= 2
            shift *= 2
            dpin_ref[:, lanes] = (level[:tm] - dd_t[:, lanes]).astype(BF16)

    halo = tm // POOL_HALO
    return pl.pallas_call(
        body, name="pool_bwd_window", grid=(n_tiles,),
        in_specs=[_row(tm, POOL_WIDTH),
                  pl.BlockSpec((POOL_HALO, POOL_WIDTH), lambda i: (jnp.minimum((i + 1) * halo, seq // POOL_HALO - 1), 0))],
        out_specs=_row(tm, POOL_WIDTH),
        out_shape=_sds((seq, POOL_WIDTH), BF16),
        compiler_params=_params(1, 32),
    )(dd, dd)


def _col_to_row(col, n):
    return jnp.transpose(jnp.broadcast_to(col, (n, HEAD_PAD)))[0:1, :]


def _attn_fwd(q, k, v):
    heads, seq, _ = q.shape
    r, n = FWD_ROWS, FWD_CHAINS
    block = r * n

    def body(q_ref, k_ref, v_ref, o_ref, lse_ref):
        qi = pl.program_id(1)
        q_tiles = [q_ref[c * r:(c + 1) * r, :] for c in range(n)]

        def tile(qt, j, m, acc, diagonal):
            start = pl.multiple_of(j * r, r)
            s = _mm_nt(qt, k_ref[pl.ds(start, r), :])
            if diagonal:
                row = lax.broadcasted_iota(jnp.int32, (r, r), 0)
                col = lax.broadcasted_iota(jnp.int32, (r, r), 1)
                s = jnp.where(col <= row, s, -jnp.inf)
            m_new = jnp.maximum(m, jnp.max(s, axis=1, keepdims=True))
            p = jnp.exp2((s - m_new).astype(BF16))
            acc = jnp.exp2(m - m_new) * acc + _mm(p, v_ref[pl.ds(start, r), :])
            return m_new, acc

        def all_chains(jj, carry):
            for u in range(n):
                carry = tuple(tile(q_tiles[c], n * jj + u, *carry[c], False) for c in range(n))
            return carry

        init = tuple((jnp.full((r, 1), -jnp.inf, F32), jnp.zeros((r, HEAD_PAD), F32)) for _ in range(n))
        state = list(lax.fori_loop(0, qi, all_chains, init))
        for d in range(n):
            for c in range(d, n):
                state[c] = tile(q_tiles[c], n * qi + d, *state[c], c == d)
        for c, (m, acc) in enumerate(state):
            l = acc[:, V_HEAD:V_HEAD + 1]
            o_ref[c * r:(c + 1) * r, :] = (acc / l).astype(BF16)
            row0 = c * r
            lse_ref[row0 // ATTN_TILE, :, row0 % ATTN_TILE:row0 % ATTN_TILE + r] = _col_to_row(m + jnp.log2(l), r)

    return pl.pallas_call(
        body, name="attn_fwd", grid=(heads, seq // block),
        in_specs=[pl.BlockSpec((None, block, HEAD_PAD), lambda h, i: (h, i, 0)),
                  pl.BlockSpec((None, seq, HEAD_PAD), lambda h, i: (h, 0, 0)),
                  pl.BlockSpec((None, seq, HEAD_PAD), lambda h, i: (h, 0, 0))],
        out_specs=[pl.BlockSpec((None, block, HEAD_PAD), lambda h, i: (h, i, 0)),
                   pl.BlockSpec((None, block // ATTN_TILE, 1, ATTN_TILE), lambda h, i: (h, i, 0, 0))],
        out_shape=[_sds((heads, seq, HEAD_PAD), BF16), _sds((heads, seq // ATTN_TILE, 1, ATTN_TILE), F32)],
        compiler_params=_params(2, 48),
    )(q, k, v)


def _attn_delta(o, do):
    heads, seq, _ = o.shape
    tq = ATTN_TILE
    nq = seq // tq

    per_step = min(4, nq)

    def body(o_ref, do_ref, delta_ref):
        for u in range(per_step):
            rows = slice(u * tq, (u + 1) * tq)
            prod = o_ref[rows, :].astype(F32) * do_ref[rows, :].astype(F32)
            delta_ref[u] = _col_to_row(jnp.sum(prod, axis=1, keepdims=True), tq)

    tile = pl.BlockSpec((None, per_step * tq, HEAD_PAD), lambda h, i: (h, i, 0))
    return pl.pallas_call(
        body, name="attn_delta", grid=(heads, nq // per_step),
        in_specs=[tile, tile],
        out_specs=pl.BlockSpec((None, per_step, 1, tq), lambda h, i: (h, i, 0, 0)),
        out_shape=_sds((heads, nq, 1, tq), F32),
        compiler_params=_params(2, 32),
    )(o, do)


def _peer_copies(src_refs, dst_refs, send_sems, recv_sems):
    x, y, c = _position()
    copies = []
    for w, (src, dst) in enumerate(zip(src_refs, dst_refs)):
        for r in range(1, N_DEV):
            px = 1 - x if r & 4 else x
            py = 1 - y if r & 2 else y
            pc = 1 - c if r & 1 else c
            copies.append(pltpu.make_async_remote_copy(
                src_ref=src.at[4 * px + 2 * py + pc], dst_ref=dst.at[r - 1],
                send_sem=send_sems.at[(N_DEV - 1) * w + r - 1], recv_sem=recv_sems.at[(N_DEV - 1) * w + r - 1],
                device_id=(px, py, pc), device_id_type=MESH))
    return copies


def _attn_bwd(q, k, k_t, v, do, lse, delta, early_grads):
    heads, seq, _ = q.shape
    t = ATTN_TILE
    nq = seq // t
    n_w = len(early_grads)

    def body(q_ref, k_ref, kt_ref, v_ref, do_ref, lse_ref, delta_ref, *rest):
        grad_refs, rest = rest[:n_w], rest[n_w:]
        dq_ref, dk_ref, dv_ref = rest[:3]
        recv_refs, (send_sems, recv_sems) = rest[3:3 + n_w], rest[3 + n_w:]
        jp = pl.program_id(1)
        head = pl.program_id(0)

        @pl.when((head == 0) & (jp == 0))
        def _():
            for cp in _peer_copies(grad_refs, recv_refs, send_sems, recv_sems):
                cp.start()

        @pl.when(jp == 0)
        def _():
            dq_ref[...] = jnp.zeros_like(dq_ref)

        k_a, k_b = k_ref[0:t, :], k_ref[t:2 * t, :]
        v_a, v_b = v_ref[0:t, :], v_ref[t:2 * t, :]

        def tile(kt, k_tr, vt, i, dk, dv, diagonal):
            start = pl.multiple_of(i * t, t)
            qt = q_ref[pl.ds(start, t), :]
            dot = do_ref[pl.ds(start, t), :]
            p_t = jnp.exp2(_mm_nt(kt, qt) - lse_ref[i])
            if diagonal:
                key = lax.broadcasted_iota(jnp.int32, (t, t), 0)
                query = lax.broadcasted_iota(jnp.int32, (t, t), 1)
                p_t = jnp.where(key <= query, p_t, 0.0)
            dv = dv + _mm(p_t.astype(BF16), dot)
            ds_t = (p_t * (_mm_nt(vt, dot) - delta_ref[i])).astype(BF16)
            dk = dk + _mm(ds_t, qt)
            return dk, dv, _mm(k_tr, ds_t)

        def add_dq(i, dq):
            dq_ref[i] += dq

        def both(ip, carry):
            dk_a, dv_a, dk_b, dv_b = carry
            for i in (2 * ip, 2 * ip + 1):
                dk_a, dv_a, dq_a = tile(k_a, kt_ref[0], v_a, i, dk_a, dv_a, False)
                dk_b, dv_b, dq_b = tile(k_b, kt_ref[1], v_b, i, dk_b, dv_b, False)
                add_dq(i, dq_a + dq_b)
            return dk_a, dv_a, dk_b, dv_b

        zero = jnp.zeros((t, HEAD_PAD), F32)
        dk_a, dv_a, dq_a = tile(k_a, kt_ref[0], v_a, 2 * jp, zero, zero, True)
        add_dq(2 * jp, dq_a)
        dk_a, dv_a, dq_a = tile(k_a, kt_ref[0], v_a, 2 * jp + 1, dk_a, dv_a, False)
        dk_b, dv_b, dq_b = tile(k_b, kt_ref[1], v_b, 2 * jp + 1, zero, zero, True)
        add_dq(2 * jp + 1, dq_a + dq_b)
        dk_a, dv_a, dk_b, dv_b = lax.fori_loop(jp + 1, nq // 2, both, (dk_a, dv_a, dk_b, dv_b))
        dk_ref[0:t, :] = (dk_a * (1.0 / LOG2E)).astype(BF16)
        dk_ref[t:2 * t, :] = (dk_b * (1.0 / LOG2E)).astype(BF16)
        dv_ref[0:t, :] = dv_a.astype(BF16)
        dv_ref[t:2 * t, :] = dv_b.astype(BF16)

        @pl.when((head == heads - 1) & (jp == nq // 2 - 1))
        def _():
            copies = _peer_copies(grad_refs, recv_refs, send_sems, recv_sems)
            for cp in copies:
                cp.wait_recv()
            for cp in copies:
                cp.wait_send()

    whole = pl.BlockSpec((None, seq, HEAD_PAD), lambda h, j: (h, 0, 0))
    whole_t = pl.BlockSpec((None, nq, HEAD_PAD, t), lambda h, j: (h, 0, 0, 0))
    pair = pl.BlockSpec((None, 2 * t, HEAD_PAD), lambda h, j: (h, j, 0))
    pair_t = pl.BlockSpec((None, 2, HEAD_PAD, t), lambda h, j: (h, j, 0, 0))
    stats = pl.BlockSpec((None, nq, 1, t), lambda h, j: (h, 0, 0, 0))
    return pl.pallas_call(
        body, name="attn_bwd", grid=(heads, nq // 2),
        in_specs=[whole, pair, pair_t, pair, whole, stats, stats] + [ANY] * n_w,
        out_specs=[whole_t, pair, pair] + [ANY] * n_w,
        out_shape=[_sds((heads, nq, HEAD_PAD, t), F32), _sds((heads, seq, HEAD_PAD), BF16),
                   _sds((heads, seq, HEAD_PAD), BF16)]
                  + [_sds((N_DEV - 1,) + g.shape[1:], g.dtype) for g in early_grads],
        scratch_shapes=[pltpu.SemaphoreType.DMA(((N_DEV - 1) * n_w,)), pltpu.SemaphoreType.DMA(((N_DEV - 1) * n_w,))],
        compiler_params=_params(2, 56),
    )(q, k, k_t, v, do, lse, delta, *early_grads)


def _merge_fwd(attn, pooled, gates, x, g_post_mix, w_ba, w_bb, w_out, tm):
    seq = x.shape[0]

    def body(attn_ref, pooled_ref, gates_ref, x_ref, g_ref, wba_ref, wbb_ref, wout_ref,
             merged_ref, ba_ref, bb_ref, y_ref, h1_ref, attn_rows_ref):
        attn = jnp.concatenate([attn_ref[h] for h in range(N_HEADS)], axis=1)
        attn_rows_ref[...] = attn
        ba = _mm(attn, wba_ref[...])
        bb = _mm(pooled_ref[...], wbb_ref[...])
        ba_ref[...] = ba.astype(BF16)
        bb_ref[...] = bb.astype(BF16)
        merged = (gates_ref[:, :D_MODEL].astype(F32) * ba + gates_ref[:, D_MODEL:].astype(F32) * bb).astype(BF16)
        merged_ref[...] = merged
        y = _mm(merged, wout_ref[...])
        y_ref[...] = y
        h1_ref[...] = x_ref[...] + y * _rms_r(y) * g_ref[...]

    return pl.pallas_call(
        body, name="merge_fwd", grid=(seq // tm,),
        in_specs=[pl.BlockSpec((N_HEADS, tm, HEAD_PAD), lambda i: (0, i, 0)), _row(tm, POOL_WIDTH),
                  _row(tm, 2 * D_MODEL), _row(tm, D_MODEL),
                  _fix((1, D_MODEL)), _fix(w_ba.shape), _fix(w_bb.shape), _fix(w_out.shape)],
        out_specs=[_row(tm, D_MODEL)] * 5 + [_row(tm, N_HEADS * HEAD_PAD)],
        out_shape=[_sds((seq, D_MODEL), BF16), _sds((seq, D_MODEL), BF16), _sds((seq, D_MODEL), BF16),
                   _sds((seq, D_MODEL), F32), _sds((seq, D_MODEL), F32), _sds((seq, N_HEADS * HEAD_PAD), BF16)],
        compiler_params=_params(1, 48),
    )(attn, pooled, gates, x, g_post_mix, w_ba, w_bb, w_out)


def _tail_fwd(h1, target, p, g_pre_mlp, g_post_mlp, g_ple, w_ff1, w_ff2, w_pe, w_pg, tm):
    seq = h1.shape[0]

    def body(h1_ref, tgt_ref, p_ref, gm_ref, gf_ref, gp_ref, w1_hbm, w2_hbm, wpe_hbm, wpg_hbm,
             m_ref, zr_ref, f_ref, h2b_ref, pb_ref, de_ref, dpre_ref, dh2_ref, loss_ref, dgple_ref,
             w1, w2, wpe, wpg):
        _load_once(((w1_hbm, w1), (w2_hbm, w2), (wpe_hbm, wpe), (wpg_hbm, wpg)))

        @pl.when(pl.program_id(0) == 0)
        def _():
            loss_ref[...] = jnp.zeros_like(loss_ref)
            dgple_ref[...] = jnp.zeros_like(dgple_ref)

        h1v = h1_ref[...]
        m = (h1v * _rms_r(h1v) * gm_ref[...]).astype(BF16)
        m_ref[...] = m
        zr = jnp.maximum(_mm(m, w1[...]), 0.0)
        zr_ref[...] = zr.astype(BF16)
        f = _mm((zr * zr).astype(BF16), w2[...])
        f_ref[...] = f
        h2 = h1v + f * _rms_r(f) * gf_ref[...]
        h2b = h2.astype(BF16)
        h2b_ref[...] = h2b
        pb = p_ref[...].astype(BF16)
        pb_ref[...] = pb
        e = _mm(pb, wpe[...])
        pg = _sigmoid(_mm(h2b, wpg[...]))
        t3 = pg * e
        r3 = _rms_r(t3)
        t3hat = t3 * r3
        diff = h2 + t3hat * gp_ref[...] - tgt_ref[...]
        loss_ref[...] += jnp.sum(diff * diff) * (0.5 / D_MODEL)
        dh3 = diff * (1.0 / D_MODEL)
        dgple_ref[...] += _colsum(dh3 * t3hat)
        dt3 = _rms_bwd(t3hat, r3, gp_ref[...], dh3)
        de_ref[...] = (dt3 * pg).astype(BF16)
        dpre = (dt3 * e * pg * (1.0 - pg)).astype(BF16)
        dpre_ref[...] = dpre
        dh2_ref[...] = dh3 + _mm_nt(dpre, wpg[...])

    return pl.pallas_call(
        body, name="tail_fwd", grid=(seq // tm,),
        in_specs=[_row(tm, D_MODEL), _row(tm, D_MODEL), _row(tm, PLE_DIM), _fix((1, D_MODEL)), _fix((1, D_MODEL)),
                  _fix((1, D_MODEL)), ANY, ANY, ANY, ANY],
        out_specs=[_row(tm, D_MODEL), _row(tm, D_FF), _row(tm, D_MODEL), _row(tm, D_MODEL), _row(tm, PLE_DIM),
                   _row(tm, D_MODEL), _row(tm, D_MODEL), _row(tm, D_MODEL), _fix((8, 128)), _fix((1, D_MODEL))],
        out_shape=[_sds((seq, D_MODEL), BF16), _sds((seq, D_FF), BF16), _sds((seq, D_MODEL), F32),
                   _sds((seq, D_MODEL), BF16), _sds((seq, PLE_DIM), BF16), _sds((seq, D_MODEL), BF16),
                   _sds((seq, D_MODEL), BF16), _sds((seq, D_MODEL), F32), _sds((8, 128), F32), _sds((1, D_MODEL), F32)],
        scratch_shapes=[pltpu.VMEM(w_ff1.shape, BF16), pltpu.VMEM(w_ff2.shape, BF16),
                        pltpu.VMEM(w_pe.shape, BF16), pltpu.VMEM(w_pg.shape, BF16)],
        compiler_params=_params(1, 56),
    )(h1, target, p, g_pre_mlp, g_post_mlp, g_ple, w_ff1, w_ff2, w_pe, w_pg)


def _mlp_bwd(h1, f, zr, dh2, g_pre_mlp, g_post_mlp, w_ff1, w_ff2, tm):
    seq = h1.shape[0]

    def body(h1_ref, f_ref, zr_ref, dh2_ref, gm_ref, gf_ref, w1_hbm, w2_hbm,
             df_ref, dz_ref, dh1_ref, dgm_ref, dgf_ref, w1, w2):
        _load_once(((w1_hbm, w1), (w2_hbm, w2)))

        @pl.when(pl.program_id(0) == 0)
        def _():
            dgm_ref[...] = jnp.zeros_like(dgm_ref)
            dgf_ref[...] = jnp.zeros_like(dgf_ref)

        dh2 = dh2_ref[...]
        fv = f_ref[...]
        rf = _rms_r(fv)
        fhat = fv * rf
        dgf_ref[...] += _colsum(dh2 * fhat)
        df = _rms_bwd(fhat, rf, gf_ref[...], dh2).astype(BF16)
        df_ref[...] = df
        dz = (_mm_nt(df, w2[...]) * (2.0 * zr_ref[...].astype(F32))).astype(BF16)
        dz_ref[...] = dz
        dm = _mm_nt(dz, w1[...])
        h1v = h1_ref[...]
        r1 = _rms_r(h1v)
        h1hat = h1v * r1
        dgm_ref[...] += _colsum(dm * h1hat)
        dh1_ref[...] = dh2 + _rms_bwd(h1hat, r1, gm_ref[...], dm)

    return pl.pallas_call(
        body, name="mlp_bwd", grid=(seq // tm,),
        in_specs=[_row(tm, D_MODEL), _row(tm, D_MODEL), _row(tm, D_FF), _row(tm, D_MODEL),
                  _fix((1, D_MODEL)), _fix((1, D_MODEL)), ANY, ANY],
        out_specs=[_row(tm, D_MODEL), _row(tm, D_FF), _row(tm, D_MODEL), _fix((1, D_MODEL)), _fix((1, D_MODEL))],
        out_shape=[_sds((seq, D_MODEL), BF16), _sds((seq, D_FF), BF16), _sds((seq, D_MODEL), F32),
                   _sds((1, D_MODEL), F32), _sds((1, D_MODEL), F32)],
        scratch_shapes=[pltpu.VMEM(w_ff1.shape, BF16), pltpu.VMEM(w_ff2.shape, BF16)],
        compiler_params=_params(1, 56),
    )(h1, f, zr, dh2, g_pre_mlp, g_post_mlp, w_ff1, w_ff2)


def _merge_bwd(dh1, y, gates, ba, bb, d_pool, g_post_mix, pool_scale, w_out, w_ba, w_bb, w_pool_bf, tm):
    seq = dh1.shape[0]

    def body(dh1_ref, y_ref, gates_ref, ba_ref, bb_ref, d_ref, g_ref, ps_ref, wout_ref, wba_ref, wbb_ref, wp_ref,
             dy_ref, dba_ref, dbb_ref, dgpre_ref, dattn_ref, dd_ref, dg_ref, dbg_ref, dps_ref, dwp_ref):
        @pl.when(pl.program_id(0) == 0)
        def _():
            dg_ref[...] = jnp.zeros_like(dg_ref)
            dbg_ref[...] = jnp.zeros_like(dbg_ref)
            dps_ref[...] = jnp.zeros_like(dps_ref)
            dwp_ref[...] = jnp.zeros_like(dwp_ref)

        dh1v = dh1_ref[...]
        yv = y_ref[...]
        r = _rms_r(yv)
        yhat = yv * r
        dg_ref[...] += _colsum(dh1v * yhat)
        dy = _rms_bwd(yhat, r, g_ref[...], dh1v).astype(BF16)
        dy_ref[...] = dy
        dmerged = _mm_nt(dy, wout_ref[...])
        for half, branch_ref, dbranch_ref in ((0, ba_ref, dba_ref), (1, bb_ref, dbb_ref)):
            lanes = slice(D_MODEL * half, D_MODEL * (half + 1))
            gate = gates_ref[:, lanes].astype(F32)
            dpre = dmerged * branch_ref[...].astype(F32) * gate * (1.0 - gate)
            dbg_ref[:, lanes] += _colsum(dpre)
            dgpre_ref[:, lanes] = dpre.astype(BF16)
            dbranch_ref[...] = (dmerged * gate).astype(BF16)
        dattn = _mm_nt(dba_ref[...], wba_ref[...]).astype(BF16)
        for h in range(N_HEADS):
            dattn_ref[h] = dattn[:, HEAD_PAD * h:HEAD_PAD * (h + 1)]
        dpooled = _mm_nt(dbb_ref[...], wbb_ref[...])
        for g in range(len(POOL_WINDOWS)):
            lanes = slice(POOL_GROUP * g, POOL_GROUP * (g + 1))
            dpl = dpooled[:, lanes]
            d_g = d_ref[:, lanes]
            dps_ref[:, lanes] += _colsum(dpl * _mm(d_g, wp_ref[g]))
            dyp = (dpl * ps_ref[:, lanes]).astype(BF16)
            dwp_ref[g] += lax.dot_general(d_g, dyp, TN, preferred_element_type=F32)
            dd_ref[:, lanes] = _mm_nt(dyp, wp_ref[g])

    return pl.pallas_call(
        body, name="merge_bwd", grid=(seq // tm,),
        in_specs=[_row(tm, D_MODEL), _row(tm, D_MODEL), _row(tm, 2 * D_MODEL), _row(tm, D_MODEL), _row(tm, D_MODEL),
                  _row(tm, POOL_WIDTH), _fix((1, D_MODEL)), _fix((1, POOL_WIDTH)),
                  _fix(w_out.shape), _fix(w_ba.shape), _fix(w_bb.shape), _fix(w_pool_bf.shape)],
        out_specs=[_row(tm, D_MODEL), _row(tm, D_MODEL), _row(tm, D_MODEL), _row(tm, 2 * D_MODEL),
                   pl.BlockSpec((N_HEADS, tm, HEAD_PAD), lambda i: (0, i, 0)), _row(tm, POOL_WIDTH),
                   _fix((1, D_MODEL)), _fix((1, 2 * D_MODEL)), _fix((1, POOL_WIDTH)), _fix(w_pool_bf.shape)],
        out_shape=[_sds((seq, D_MODEL), BF16), _sds((seq, D_MODEL), BF16), _sds((seq, D_MODEL), BF16),
                   _sds((seq, 2 * D_MODEL), BF16), _sds((N_HEADS, seq, HEAD_PAD), BF16),
                   _sds((seq, POOL_WIDTH), F32), _sds((1, D_MODEL), F32), _sds((1, 2 * D_MODEL), F32),
                   _sds((1, POOL_WIDTH), F32), _sds(w_pool_bf.shape, F32)],
        compiler_params=_params(1, 48),
    )(dh1, y, gates, ba, bb, d_pool, g_post_mix, pool_scale, w_out, w_ba, w_bb, w_pool_bf)


def _proj_bwd(dq, dk, dv, qd, kvd, x, dh1, dgpre, dpin, cf, s1, s2, g_pre, g_q, g_kv,
              w_in_r, w_uq_r, w_k_exp, w_v, tm):
    seq = x.shape[0]

    def body(dq_ref, dk_ref, dv_ref, qd_ref, kvd_ref, x_ref, dh1_ref, dgpre_ref, dpin_ref, cf_ref, s1_ref, s2_ref,
             gpre_ref, gq_ref, gkv_ref, win_hbm, wuq_hbm, wk_hbm, wv_hbm,
             gx_ref, dproj_ref, dqb_ref, qn_ref, kvn_ref, dkb_ref, dvb_ref, dgpre_acc, dgq_acc, dgkv_acc,
             win, wuq, wk, wv):
        _load_once(((win_hbm, win), (wuq_hbm, wuq), (wk_hbm, wk), (wv_hbm, wv)))

        @pl.when(pl.program_id(0) == 0)
        def _():
            dgpre_acc[...] = jnp.zeros_like(dgpre_acc)
            dgq_acc[...] = jnp.zeros_like(dgq_acc)
            dgkv_acc[...] = jnp.zeros_like(dgkv_acc)

        cfv, s1v, s2v = cf_ref[...], s1_ref[...], s2_ref[...]
        ksum = jnp.zeros((tm, HEAD_PAD), F32)
        for h in range(N_HEADS):
            lanes = slice(HEAD_PAD * h, HEAD_PAD * (h + 1))
            dqb_ref[:, lanes] = (_rope_t(jnp.transpose(dq_ref[h]), cfv, s1v, s2v) * ATTN_SCALE).astype(BF16)
            dkh = dk_ref[h]
            dkb_ref[:, lanes] = dkh
            dvb_ref[:, lanes] = dv_ref[h]
            ksum = ksum + dkh.astype(F32)
        lane = lax.broadcasted_iota(jnp.int32, (tm, HEAD_PAD), 1)
        rope_lanes = (lane >= QK_NOPE) & (lane < QK_NOPE + QK_ROPE)
        dkr = _rope_t(jnp.where(rope_lanes, ksum, 0.0), cfv, s1v, s2v)

        qdv = qd_ref[...]
        rq = _rms_r(qdv)
        qhat = qdv * rq
        qn_ref[...] = (qhat * gq_ref[...]).astype(BF16)
        dqn = _mm_nt(dqb_ref[...], wuq[...])
        dgq_acc[...] += _colsum(dqn * qhat)
        dproj_ref[:, IN_Q0:IN_KV0] = _rms_bwd(qhat, rq, gq_ref[...], dqn).astype(BF16)

        kvdv = kvd_ref[...]
        rkv = _rms_r(kvdv)
        kvhat = kvdv * rkv
        kvn_ref[...] = (kvhat * gkv_ref[...]).astype(BF16)
        dkvn = _mm_nt(dkb_ref[...], wk[...]) + _mm_nt(dvb_ref[...], wv[...])
        dgkv_acc[...] += _colsum(dkvn * kvhat)
        dproj_ref[:, IN_KV0:IN_POOL0] = _rms_bwd(kvhat, rkv, gkv_ref[...], dkvn).astype(BF16)

        dproj_ref[:, IN_POOL0:IN_GATE0] = dpin_ref[...]
        dproj_ref[:, IN_GATE0:IN_KR0] = dgpre_ref[...]
        dproj_ref[:, IN_KR0:IN_R] = dkr.astype(BF16)

        da = _mm_nt(dproj_ref[...], win[...])
        xv = x_ref[...]
        r0 = _rms_r(xv)
        xhat = xv * r0
        dgpre_acc[...] += _colsum(da * xhat)
        gx_ref[...] = dh1_ref[...] + _rms_bwd(xhat, r0, gpre_ref[...], da)

    per_tile = ATTN_TILE // tm
    heads = pl.BlockSpec((N_HEADS, tm, HEAD_PAD), lambda i: (0, i, 0))
    heads_t = pl.BlockSpec((N_HEADS, None, HEAD_PAD, tm), lambda i: (0, i // per_tile, 0, i % per_tile))
    return pl.pallas_call(
        body, name="proj_bwd", grid=(seq // tm,),
        in_specs=[heads_t, heads, heads, _row(tm, Q_LORA), _row(tm, KV_LORA), _row(tm, D_MODEL),
                  _row(tm, D_MODEL), _row(tm, 2 * D_MODEL), _row(tm, POOL_WIDTH),
                  _row(tm, HEAD_PAD), _row(tm, HEAD_PAD), _row(tm, HEAD_PAD),
                  _fix((1, D_MODEL)), _fix((1, Q_LORA)), _fix((1, KV_LORA)), ANY, ANY, ANY, ANY],
        out_specs=[_row(tm, D_MODEL), _row(tm, IN_R), _row(tm, N_HEADS * HEAD_PAD), _row(tm, Q_LORA), _row(tm, KV_LORA),
                   _row(tm, N_HEADS * HEAD_PAD), _row(tm, N_HEADS * HEAD_PAD),
                   _fix((1, D_MODEL)), _fix((1, Q_LORA)), _fix((1, KV_LORA))],
        out_shape=[_sds((seq, D_MODEL), F32), _sds((seq, IN_R), BF16), _sds((seq, N_HEADS * HEAD_PAD), BF16),
                   _sds((seq, Q_LORA), BF16), _sds((seq, KV_LORA), BF16), _sds((seq, N_HEADS * HEAD_PAD), BF16),
                   _sds((seq, N_HEADS * HEAD_PAD), BF16),
                   _sds((1, D_MODEL), F32), _sds((1, Q_LORA), F32), _sds((1, KV_LORA), F32)],
        scratch_shapes=[pltpu.VMEM(w_in_r.shape, BF16), pltpu.VMEM(w_uq_r.shape, BF16),
                        pltpu.VMEM(w_k_exp.shape, BF16), pltpu.VMEM(w_v.shape, BF16)],
        compiler_params=_params(1, 48),
    )(dq, dk, dv, qd, kvd, x, dh1, dgpre, dpin, cf, s1, s2, g_pre, g_q, g_kv, w_in_r, w_uq_r, w_k_exp, w_v)


def _grad_w(a, b, name, square_a=False, by_device=False, with_bf16=False):
    seq, k_dim = a.shape
    n_dim = b.shape[1]
    tk = min(k_dim, 1024)
    tn = n_dim // 2 if n_dim == IN_R else min(n_dim, 1024)
    ts = min(seq, 1024)
    shard = n_dim // N_DEV
    per_tile = tn // shard
    if by_device:
        out_spec = pl.BlockSpec((per_tile, tk, shard), lambda i, j, s: (j, i, 0))
        out_shape = _sds((N_DEV, k_dim, shard), F32)
    else:
        out_spec = pl.BlockSpec((tk, tn), lambda i, j, s: (i, j))
        out_shape = _sds((k_dim, n_dim), F32)

    n_seq_steps = seq // ts

    def body(a_ref, b_ref, o_ref, *narrow):
        @pl.when(pl.program_id(2) == 0)
        def _():
            o_ref[...] = jnp.zeros_like(o_ref)

        at = a_ref[...]
        if square_a:
            at = at * at
        part = lax.dot_general(at, b_ref[...], TN, preferred_element_type=F32)
        if by_device:
            for d in range(per_tile):
                o_ref[d] += part[:, d * shard:(d + 1) * shard]
        else:
            o_ref[...] += part
        if with_bf16:
            @pl.when(pl.program_id(2) == n_seq_steps - 1)
            def _():
                narrow[0][...] = o_ref[...].astype(BF16)

    return pl.pallas_call(
        body, name=name, grid=(k_dim // tk, n_dim // tn, n_seq_steps),
        in_specs=[pl.BlockSpec((ts, tk), lambda i, j, s: (s, i)), pl.BlockSpec((ts, tn), lambda i, j, s: (s, j))],
        out_specs=[out_spec, out_spec] if with_bf16 else out_spec,
        out_shape=[out_shape, _sds(out_shape.shape, BF16)] if with_bf16 else out_shape,
        compiler_params=_params(3, 48),
    )(a, b)


def _position():
    return lax.axis_index("x"), lax.axis_index("y"), lax.axis_index("c")


def _gather_copies(x_ref, slot, send_sems, recv_sems, local_sem, phases=("send", "forward", "finish")):
    x, y, c = _position()
    me, sibling = (x, y, c), (x, y, 1 - c)
    chips = [(1 - x, y), (x, 1 - y), (1 - x, 1 - y)]

    def copy(k, block, to, src=None):
        return pltpu.make_async_remote_copy(
            src_ref=slot(*block) if src is None else src, dst_ref=slot(*block),
            send_sem=send_sems.at[k], recv_sem=recv_sems.at[k], device_id=to, device_id_type=MESH)

    mine = pltpu.make_async_copy(x_ref, slot(*me), local_sem)
    first = [copy(0, me, sibling, src=x_ref)]
    first += [copy(1 + j, me, (*chip, c), src=x_ref) for j, chip in enumerate(chips)]
    passed = [copy(4 + j, (*chip, c), sibling) for j, chip in enumerate(chips)]
    if "send" in phases:
        mine.start()
        for cp in first:
            cp.start()
    if "forward" in phases:
        for j, chip in enumerate(chips):
            copy(1 + j, (*chip, c), me).wait_recv()
            passed[j].start()
    if "finish" in phases:
        copy(0, sibling, me).wait_recv()
        for j, chip in enumerate(chips):
            copy(4 + j, (*chip, 1 - c), me).wait_recv()
        for cp in first + passed:
            cp.wait_send()
        mine.wait()


def _all_gather_hbm(block):
    def body(x_ref, out_ref, send_sems, recv_sems, local_sem):
        _gather_copies(x_ref, lambda px, py, pc: out_ref.at[4 * px + 2 * py + pc], send_sems, recv_sems, local_sem)

    return pl.pallas_call(
        body, name="gather_weights",
        in_specs=[ANY], out_specs=ANY,
        out_shape=_sds((N_DEV,) + block.shape, block.dtype),
        scratch_shapes=[pltpu.SemaphoreType.DMA((7,)), pltpu.SemaphoreType.DMA((7,)), pltpu.SemaphoreType.DMA],
    )(block)


def _all_reduce_small(block):
    def body(x_ref, out_ref, buf, send_sems, recv_sems, local_sem):
        _gather_copies(x_ref, lambda px, py, pc: buf.at[4 * px + 2 * py + pc], send_sems, recv_sems, local_sem)
        acc = buf[0]
        for k in range(1, N_DEV):
            acc = acc + buf[k]
        out_ref[...] = acc

    vmem = pl.BlockSpec(memory_space=pltpu.VMEM)
    return pl.pallas_call(
        body, name="all_reduce_small",
        in_specs=[vmem], out_specs=vmem,
        out_shape=_sds(block.shape, F32),
        scratch_shapes=[pltpu.VMEM((N_DEV,) + block.shape, F32), pltpu.SemaphoreType.DMA((7,)),
                        pltpu.SemaphoreType.DMA((7,)), pltpu.SemaphoreType.DMA],
        compiler_params=pltpu.CompilerParams(vmem_limit_bytes=32 * MIB),
    )(block)


def _exchange_pair(gs):
    n_w = len(gs)

    def body(*refs):
        g_refs, out_refs = refs[:n_w], refs[n_w:2 * n_w]
        send_sems, recv_sems = refs[2 * n_w:]
        x, y, c = _position()
        copies = []
        for w in range(n_w):
            for chip in range(4):
                cp = pltpu.make_async_remote_copy(
                    src_ref=g_refs[w].at[2 * chip + (1 - c)], dst_ref=out_refs[w].at[chip],
                    send_sem=send_sems.at[4 * w + chip], recv_sem=recv_sems.at[4 * w + chip],
                    device_id=(x, y, 1 - c), device_id_type=MESH)
                cp.start()
                copies.append(cp)
        for cp in copies:
            cp.wait_recv()
        for cp in copies:
            cp.wait_send()

    return pl.pallas_call(
        body, name="exchange_pair",
        in_specs=[ANY] * n_w, out_specs=[ANY] * n_w,
        out_shape=[_sds((4,) + g.shape[1:], g.dtype) for g in gs],
        scratch_shapes=[pltpu.SemaphoreType.DMA((4 * n_w,)), pltpu.SemaphoreType.DMA((4 * n_w,))],
    )(*gs)


def _exchange_chips(parts):
    n_w = len(parts)

    def body(*refs):
        p_refs, out_refs = refs[:n_w], refs[n_w:2 * n_w]
        send_sems, recv_sems = refs[2 * n_w:]
        x, y, c = _position()
        chips = [(1 - x, y), (x, 1 - y), (1 - x, 1 - y)]
        copies = []
        for w in range(n_w):
            for k, (px, py) in enumerate(chips):
                cp = pltpu.make_async_remote_copy(
                    src_ref=p_refs[w].at[2 * px + py], dst_ref=out_refs[w].at[k],
                    send_sem=send_sems.at[3 * w + k], recv_sem=recv_sems.at[3 * w + k],
                    device_id=(px, py, c), device_id_type=MESH)
                cp.start()
                copies.append(cp)
        for cp in copies:
            cp.wait_recv()
        for cp in copies:
            cp.wait_send()

    return pl.pallas_call(
        body, name="exchange_chips",
        in_specs=[ANY] * n_w, out_specs=[ANY] * n_w,
        out_shape=[_sds((3,) + p.shape[1:], p.dtype) for p in parts],
        scratch_shapes=[pltpu.SemaphoreType.DMA((3 * n_w,)), pltpu.SemaphoreType.DMA((3 * n_w,))],
    )(*parts)


def _row_tile(k):
    return 256 if k % 256 == 0 else 128


def _pair_sum(g, recv, place, name):
    _, k, n = g.shape
    tr = _row_tile(k)
    g4 = g.reshape(4, 2, k, n)

    def body(s_ref, g_ref, r_ref, o_ref):
        o_ref[...] = (g_ref[...] + r_ref[...]).astype(BF16)

    spec = pltpu.PrefetchScalarGridSpec(
        num_scalar_prefetch=1, grid=(4, k // tr),
        in_specs=[pl.BlockSpec((None, None, tr, n), lambda j, i, s: (j, s[2], i, 0)),
                  pl.BlockSpec((None, tr, n), lambda j, i, s: (j, i, 0))],
        out_specs=pl.BlockSpec((None, tr, n), lambda j, i, s: (j, i, 0)))
    return pl.pallas_call(
        body, name=name, grid_spec=spec, out_shape=_sds((4, k, n), BF16),
        compiler_params=_params(2, 32),
    )(place, g4, recv)


def _adamw_math(g, w, m, v):
    m = ADAM_B1 * m + (1.0 - ADAM_B1) * g
    v = ADAM_B2 * v + (1.0 - ADAM_B2) * (g * g)
    m_hat = m / (1.0 - ADAM_B1 ** ADAM_STEP)
    v_hat = v / (1.0 - ADAM_B2 ** ADAM_STEP)
    delta = -ADAM_LR * (m_hat / (jnp.sqrt(v_hat) + ADAM_EPS) + ADAM_WD * w)
    return delta, m, v


def _adamw_sharded(g, from_sibling, from_chips, place, w, m, v, name):
    _, k, n = g.shape
    tr = _row_tile(k)

    def body(s_ref, g_ref, sib_ref, r0_ref, r1_ref, r2_ref, w_ref, m_ref, v_ref, grad_ref, d_ref, nm_ref, nv_ref):
        grad = g_ref[...] + sib_ref[...]
        for r_ref in (r0_ref, r1_ref, r2_ref):
            grad = grad + r_ref[...].astype(F32)
        grad_ref[...] = grad
        d_ref[...], nm_ref[...], nv_ref[...] = _adamw_math(grad, w_ref[...], m_ref[...], v_ref[...])

    tile = pl.BlockSpec((None, tr, n), lambda i, s: (0, i, 0))

    def slot(j):
        return pl.BlockSpec((None, tr, n), lambda i, s: (j, i, 0))

    spec = pltpu.PrefetchScalarGridSpec(
        num_scalar_prefetch=1, grid=(k // tr,),
        in_specs=[pl.BlockSpec((None, tr, n), lambda i, s: (s[0], i, 0)),
                  pl.BlockSpec((None, tr, n), lambda i, s: (s[1], i, 0)),
                  slot(0), slot(1), slot(2), tile, tile, tile],
        out_specs=[tile] * 4)
    return pl.pallas_call(
        body, name=name, grid_spec=spec, out_shape=[_sds((1, k, n), F32)] * 4,
        compiler_params=_params(1, 48),
    )(place, g, from_sibling, from_chips, from_chips, from_chips, w, m, v)


def _adamw_direct(g, received, place, w, m, v, name):
    _, k, n = g.shape
    tr = _row_tile(k)

    def body(s_ref, g_ref, r_ref, w_ref, m_ref, v_ref, grad_ref, d_ref, nm_ref, nv_ref):
        grad = g_ref[...]
        for r in range(N_DEV - 1):
            grad = grad + r_ref[r].astype(F32)
        grad_ref[...] = grad
        d_ref[...], nm_ref[...], nv_ref[...] = _adamw_math(grad, w_ref[...], m_ref[...], v_ref[...])

    tile = pl.BlockSpec((None, tr, n), lambda i, s: (0, i, 0))
    spec = pltpu.PrefetchScalarGridSpec(
        num_scalar_prefetch=1, grid=(k // tr,),
        in_specs=[pl.BlockSpec((None, tr, n), lambda i, s: (s[0], i, 0)),
                  pl.BlockSpec((N_DEV - 1, tr, n), lambda i, s: (0, i, 0)), tile, tile, tile],
        out_specs=[tile] * 4)
    return pl.pallas_call(
        body, name=name, grid_spec=spec, out_shape=[_sds((1, k, n), F32)] * 4,
        compiler_params=_params(1, 48),
    )(place, g, received, w, m, v)


def _adamw_small(g, w, m, v):
    def body(g_ref, w_ref, m_ref, v_ref, d_ref, nm_ref, nv_ref):
        d_ref[...], nm_ref[...], nv_ref[...] = _adamw_math(g_ref[...], w_ref[...], m_ref[...], v_ref[...])

    return pl.pallas_call(body, name="adamw_small", out_shape=[_sds(g.shape, F32)] * 3)(g, w, m, v)


def _pack_rows(parts):
    parts = [a.reshape(-1, LANES) for a in parts]
    pad = (-sum(a.shape[0] for a in parts)) % PACK_ROW_TILE
    return jnp.concatenate(parts + [jnp.zeros((pad, LANES), parts[0].dtype)], axis=0)


def _pack_small(parts):
    flat = jnp.concatenate([a.reshape(-1) for a in parts])
    pad = (-flat.shape[0]) % (8 * LANES)
    return jnp.pad(flat, (0, pad)).reshape(-1, LANES)


def _unpack_small(packed, shapes):
    flat = packed.reshape(-1)
    out, off = [], 0
    for shape in shapes:
        size = 1
        for n in shape:
            size *= n
        out.append(flat[off:off + size].reshape(shape))
        off += size
    return out


def _full_from_gathered(gathered, entries, shard_shapes):
    out, off = {}, 0
    for (name, kind), (k, n) in zip(entries, shard_shapes):
        rows = k * n // LANES
        seg = gathered[:, off:off + rows].reshape(N_DEV, k, n)
        out[name] = jnp.transpose(seg, (1, 0, 2)).reshape(k, N_DEV * n) if kind == "col" else seg.reshape(N_DEV * k, n)
        off += rows
    return out


def _columns_by_device(a):
    k, n_all = a.shape
    return jnp.transpose(a.reshape(k, N_DEV, n_all // N_DEV), (1, 0, 2))


def _rows_by_device(a):
    k_all, n = a.shape
    return a.reshape(N_DEV, k_all // N_DEV, n)


def _rope_lane_frequencies():
    inv_freq = ROPE_THETA ** (-jnp.arange(0, QK_ROPE, 2, dtype=F32) / QK_ROPE)
    zeros = lambda n: jnp.zeros((n,), F32)
    return jnp.concatenate([zeros(QK_NOPE), inv_freq, inv_freq, zeros(HEAD_PAD - QK_NOPE - QK_ROPE)])[None, :]


def _rope_tables(pos_row, freq, tm):
    pos = jnp.transpose(jnp.broadcast_to(pos_row.astype(F32), (HEAD_PAD, tm)))
    ang = pos * freq
    cos, sin = jnp.cos(ang), jnp.sin(ang)
    lane = lax.broadcasted_iota(jnp.int32, (tm, HEAD_PAD), 1)
    first = (lane >= QK_NOPE) & (lane < QK_NOPE + QK_ROPE // 2)
    second = (lane >= QK_NOPE + QK_ROPE // 2) & (lane < QK_NOPE + QK_ROPE)
    cf = jnp.where(lane < QK_NOPE, 1.0, jnp.where(first | second, cos, 0.0))
    return cf, jnp.where(first, -sin, 0.0), jnp.where(second, sin, 0.0)


def _arrange_w_in(w):
    k = w.shape[0]
    zeros = lambda n: jnp.zeros((k, n), w.dtype)
    kr0 = Q_LORA + KV_LORA
    pool0 = kr0 + QK_ROPE
    return jnp.concatenate([w[:, :kr0], w[:, pool0:], zeros(QK_NOPE), w[:, kr0:pool0],
                            zeros(HEAD_PAD - QK_NOPE - QK_ROPE)], axis=1)


def _restore_w_in(d):
    kr = d[:, IN_KR0 + QK_NOPE:IN_KR0 + QK_NOPE + QK_ROPE]
    return jnp.concatenate([d[:, :IN_POOL0], kr, d[:, IN_POOL0:IN_KR0]], axis=1)


def _pad_heads(w, width):
    k = w.shape[0]
    w = w.reshape(k, N_HEADS, width)
    return jnp.pad(w, ((0, 0), (0, 0), (0, HEAD_PAD - width))).reshape(k, N_HEADS * HEAD_PAD)


def _unpad_heads(d, width):
    k = d.shape[0]
    return d.reshape(k, N_HEADS, HEAD_PAD)[:, :, :width]


def kernel(x, p, positions, g_pre_mix, w_in, b_gate, g_q, w_uq, g_kv, w_ukv, w_pool, pool_scale, w_branch_attn, w_branch_pool, w_out, g_post_mix, g_pre_mlp, w_ff1, w_ff2, g_post_mlp, w_ple_proj, w_ple_gate, g_ple, loss_target, m_g_pre_mix, m_w_in, m_b_gate, m_g_q, m_w_uq, m_g_kv, m_w_ukv, m_w_pool, m_pool_scale, m_w_branch_attn, m_w_branch_pool, m_w_out, m_g_post_mix, m_g_pre_mlp, m_w_ff1, m_w_ff2, m_g_post_mlp, m_w_ple_proj, m_w_ple_gate, m_g_ple, v_g_pre_mix, v_w_in, v_b_gate, v_g_q, v_w_uq, v_g_kv, v_w_ukv, v_w_pool, v_pool_scale, v_w_branch_attn, v_w_branch_pool, v_w_out, v_g_post_mix, v_g_pre_mlp, v_w_ff1, v_w_ff2, v_g_post_mlp, v_w_ple_proj, v_w_ple_gate, v_g_ple):
    given = dict(locals())
    weights = {n: given[n] for n in WEIGHT_ORDER}
    moments_m = {n: given["m_" + n] for n in WEIGHT_ORDER}
    moments_v = {n: given["v_" + n] for n in WEIGHT_ORDER}
    xs, ps, target = x[0], p[0, 0], loss_target[0]
    seq = xs.shape[0]
    tm = min(256, seq)
    tm_merge = min(512, seq)
    core = lax.axis_index("c")
    chip = 2 * lax.axis_index("x") + lax.axis_index("y")

    early, later = SHARDED[:N_EARLY], SHARDED[N_EARLY:]
    shapes_of = lambda entries: [weights[n].shape[1:] for n, _ in entries]
    pack_bf16 = lambda entries: _pack_rows([weights[n][0].astype(BF16) for n, _ in entries])
    full = _full_from_gathered(_all_gather_hbm(pack_bf16(early)), early, shapes_of(early))
    w_in_r = _arrange_w_in(full["w_in"])
    w_uq_r = _pad_heads(full["w_uq"], QK_NOPE + QK_ROPE)
    ukv = full["w_ukv"].reshape(KV_LORA, N_HEADS, QK_NOPE + V_HEAD)
    w_k_exp = _pad_heads(ukv[:, :, :QK_NOPE].reshape(KV_LORA, N_HEADS * QK_NOPE), QK_NOPE)
    w_v = _pad_heads(ukv[:, :, QK_NOPE:].reshape(KV_LORA, N_HEADS * V_HEAD), V_HEAD)
    w_pool_bf = w_pool[0].astype(BF16)

    a_bf, qd, kvd, pin, gates, q, k, v, k_t, cf, s1, s2, gathered_later = _proj_fwd(
        xs, g_pre_mix, b_gate, g_q, g_kv, positions, w_in_r, w_uq_r, w_k_exp, w_v, pack_bf16(later), tm)
    full.update(_full_from_gathered(gathered_later, later, shapes_of(later)))
    w_ba = jnp.pad(full["w_branch_attn"].reshape(N_HEADS, V_HEAD, D_MODEL),
                   ((0, 0), (0, HEAD_PAD - V_HEAD), (0, 0))).reshape(N_HEADS * HEAD_PAD, D_MODEL)
    d_pool, pooled = _pool_fwd(pin, w_pool_bf, pool_scale, tm)
    o_heads, lse = _attn_fwd(q, k, v)
    merged, ba, bb, y, h1, attn_rows = _merge_fwd(o_heads, pooled, gates, xs, g_post_mix, w_ba,
                                                  full["w_branch_pool"], full["w_out"], tm_merge)
    (m_bf, zr, f, h2_bf, p_bf, de, dpre, dh2, loss_acc, dg_ple) = _tail_fwd(
        h1, target, ps, g_pre_mlp, g_post_mlp, g_ple, full["w_ff1"], full["w_ff2"], full["w_ple_proj"],
        full["w_ple_gate"], tm)

    by_device, payload = {}, {}

    def early(name, pair, layout=lambda g: g):
        by_device[name], payload[name] = layout(pair[0]), layout(pair[1])

    early("w_ple_proj", _grad_w(p_bf, de, "grad_w_ple_proj", by_device=True, with_bf16=True))
    early("w_ple_gate", _grad_w(h2_bf, dpre, "grad_w_ple_gate", with_bf16=True), _rows_by_device)
    df, dz, dh1, dg_pre_mlp, dg_post_mlp = _mlp_bwd(h1, f, zr, dh2, g_pre_mlp, g_post_mlp, full["w_ff1"],
                                                    full["w_ff2"], tm)
    early("w_ff1", _grad_w(m_bf, dz, "grad_w_ff1", by_device=True, with_bf16=True))
    early("w_ff2", _grad_w(zr, df, "grad_w_ff2", square_a=True, with_bf16=True), _rows_by_device)
    (dy, dba, dbb, dgpre, do_heads, dd, dg_post_mix, db_gate, dpool_scale, dw_pool) = _merge_bwd(
        dh1, y, gates, ba, bb, d_pool, g_post_mix, pool_scale, full["w_out"], w_ba,
        full["w_branch_pool"], w_pool_bf, tm_merge)
    early("w_branch_attn", _grad_w(attn_rows, dba, "grad_w_branch_attn", with_bf16=True),
          lambda g: _columns_by_device(g.reshape(N_HEADS, HEAD_PAD, D_MODEL)[:, :V_HEAD].reshape(-1, D_MODEL)))
    early("w_branch_pool", _grad_w(pooled, dbb, "grad_w_branch_pool", by_device=True, with_bf16=True))
    early("w_out", _grad_w(merged, dy, "grad_w_out", with_bf16=True), _rows_by_device)
    dpin = _pool_bwd_window(dd, tm)
    delta = _attn_delta(o_heads, do_heads)
    direct = [n for n, _ in SHARDED[N_EARLY:]]
    outs = _attn_bwd(q, k, k_t, v, do_heads, lse, delta, [payload[n] for n in direct])
    dq, dk, dv = outs[:3]
    received = dict(zip(direct, outs[3:]))
    (grad_x, dproj, dq_bf, qn_bf, kvn_bf, dk_bf, dv_bf, dg_pre_mix, dg_q, dg_kv) = _proj_bwd(
        dq, dk, dv, qd, kvd, xs, dh1, dgpre, dpin, cf, s1, s2, g_pre_mix, g_q, g_kv, w_in_r, w_uq_r, w_k_exp, w_v, tm)
    d_k_exp = _unpad_heads(_grad_w(kvn_bf, dk_bf, "grad_w_uk"), QK_NOPE)
    d_w_v = _unpad_heads(_grad_w(kvn_bf, dv_bf, "grad_w_uv"), V_HEAD)
    by_device["w_in"] = _columns_by_device(_restore_w_in(_grad_w(a_bf, dproj, "grad_w_in")))
    by_device["w_uq"] = _columns_by_device(
        _unpad_heads(_grad_w(qn_bf, dq_bf, "grad_w_uq"), QK_NOPE + QK_ROPE).reshape(Q_LORA, -1))
    by_device["w_ukv"] = _columns_by_device(jnp.concatenate([d_k_exp, d_w_v], axis=2).reshape(KV_LORA, -1))
    grads_small = {
        "g_pre_mix": dg_pre_mix, "b_gate": db_gate, "g_q": dg_q, "g_kv": dg_kv,
        "w_pool": dw_pool, "pool_scale": dpool_scale, "g_post_mix": dg_post_mix,
        "g_pre_mlp": dg_pre_mlp, "g_post_mlp": dg_post_mlp, "g_ple": dg_ple,
    }

    names = [n for n, _ in SHARDED]
    place = jnp.stack([2 * chip + core, chip, core]).astype(jnp.int32)
    sharded = {n: _adamw_direct(by_device[n], received[n], place, weights[n], moments_m[n], moments_v[n],
                                "adamw_" + n) for n in direct}
    last = [n for n, _ in SHARDED[:N_EARLY]]
    own = [by_device[n] for n in last]
    from_sibling = _exchange_pair(own)
    pair = [_pair_sum(g, r, place, "pair_sum_" + n) for n, g, r in zip(last, own, from_sibling)]
    from_chips = _exchange_chips(pair)
    sharded.update({n: _adamw_sharded(g, r, rc, place, weights[n], moments_m[n], moments_v[n], "adamw_" + n)
                    for n, g, r, rc in zip(last, own, from_sibling, from_chips)})

    g_sm = _all_reduce_small(_pack_small([grads_small[n] for n in REPLICATED] + [loss_acc[0:1, 0:1]]))
    n_small = sum(weights[n].size for n in REPLICATED)
    d_sm, m_sm, v_sm = _adamw_small(g_sm, _pack_small([weights[n] for n in REPLICATED]),
                                    _pack_small([moments_m[n] for n in REPLICATED]),
                                    _pack_small([moments_v[n] for n in REPLICATED]))

    small_shapes = [weights[n].shape for n in REPLICATED]
    results = []
    for which, small in enumerate((g_sm, d_sm, m_sm, v_sm)):
        named = {n: sharded[n][which] for n in names}
        named.update(zip(REPLICATED, _unpack_small(small, small_shapes)))
        results.append([named[n] for n in WEIGHT_ORDER])

    loss = g_sm.reshape(-1)[n_small]
    return (loss, grad_x[None], *results[0], *results[1], *results[2], *results[3])
```

```python
import jax
import jax.numpy as jnp
from jax import lax
from jax.experimental import pallas as pl
from jax.experimental.pallas import tpu as pltpu

F32 = jnp.float32
BF16 = jnp.bfloat16

D_MODEL = 1024
PLE_DIM = 256
N_HEADS = 8
QK_NOPE = 64
QK_ROPE = 32
V_HEAD = 64
Q_LORA = 384
KV_LORA = 256
POOL_WINDOWS = (2, 4, 8, 16)
POOL_GROUP = 128
POOL_WIDTH = 512
D_FF = 4096
ROPE_THETA = 10000.0
EPS = 1e-6
HEAD_PAD = 128
ATTN_SCALE = (QK_NOPE + QK_ROPE) ** -0.5
LOG2E = 1.4426950408889634
Q_PRESCALE = ATTN_SCALE * LOG2E
ATTN_TILE = 512
FWD_ROWS = 512
FWD_CHAINS = 4

ADAM_LR = 0.001
ADAM_B1 = 0.9
ADAM_B2 = 0.999
ADAM_EPS = 1e-08
ADAM_WD = 0.01
ADAM_STEP = 10

N_DEV = 8
LANES = 1024
PACK_ROW_TILE = 480
POOL_HALO = 16
MIB = 2 ** 20

IN_Q0, IN_KV0, IN_POOL0, IN_GATE0, IN_KR0, IN_R = 0, 384, 640, 1152, 3200, 3328

SHARDED = (("w_in", "col"), ("w_uq", "col"), ("w_ukv", "col"), ("w_branch_attn", "col"),
           ("w_branch_pool", "col"), ("w_out", "row"), ("w_ff1", "col"), ("w_ff2", "row"),
           ("w_ple_proj", "col"), ("w_ple_gate", "row"))
N_EARLY = 3
REPLICATED = ("g_pre_mix", "b_gate", "g_q", "g_kv", "w_pool", "pool_scale", "g_post_mix",
              "g_pre_mlp", "g_post_mlp", "g_ple")
WEIGHT_ORDER = ("g_pre_mix", "w_in", "b_gate", "g_q", "w_uq", "g_kv", "w_ukv", "w_pool", "pool_scale",
                "w_branch_attn", "w_branch_pool", "w_out", "g_post_mix", "g_pre_mlp", "w_ff1", "w_ff2",
                "g_post_mlp", "w_ple_proj", "w_ple_gate", "g_ple")

NT = (((1,), (1,)), ((), ()))
TN = (((0,), (0,)), ((), ()))
MESH = pl.DeviceIdType.MESH
ANY = pl.BlockSpec(memory_space=pl.ANY)


def _params(n_axes, vmem_mib):
    return pltpu.CompilerParams(dimension_semantics=("arbitrary",) * n_axes, vmem_limit_bytes=vmem_mib * MIB)


def _row(tm, n):
    return pl.BlockSpec((tm, n), lambda i: (i, 0))


def _fix(shape):
    zeros = (0,) * len(shape)
    return pl.BlockSpec(shape, lambda i: zeros)


def _sds(shape, dtype):
    return jax.ShapeDtypeStruct(shape, dtype)


def _rms_r(v):
    return lax.rsqrt(jnp.mean(v * v, axis=-1, keepdims=True) + EPS)


def _rms_bwd(vhat, r, g, dy):
    gdy = dy * g
    return r * (gdy - vhat * jnp.mean(gdy * vhat, axis=-1, keepdims=True))


def _colsum(v):
    return jnp.sum(v, axis=0, keepdims=True)


def _sigmoid(v):
    return 1.0 / (1.0 + jnp.exp(-v))


def _mm(a, b):
    return jnp.dot(a, b, preferred_element_type=F32)


def _mm_nt(a, b):
    return lax.dot_general(a, b, NT, preferred_element_type=F32)


def _rope(c, cf, s1, s2):
    return c * cf + pltpu.roll(c, HEAD_PAD - 16, 1) * s1 + pltpu.roll(c, 16, 1) * s2


def _rope_t(c, cf, s1, s2):
    return c * cf + pltpu.roll(c * s1, 16, 1) + pltpu.roll(c * s2, HEAD_PAD - 16, 1)


def _row_chains(tm, rows=256):
    rows = min(rows, tm)
    return [slice(c * rows, (c + 1) * rows) for c in range(tm // rows)]


def _load_once(pairs):
    @pl.when(pl.program_id(0) == 0)
    def _():
        for src, dst in pairs:
            pltpu.sync_copy(src, dst)


def _proj_fwd(x, g_pre, b_gate, g_q, g_kv, positions, w_in_r, w_uq_r, w_k_exp, w_v, later_shards, tm):
    seq = x.shape[0]
    n_steps = seq // tm
    forward_step = (3 * n_steps) // 4

    def body(x_ref, gpre_ref, bg_ref, gq_ref, gkv_ref, pos_ref, freq_ref, win_hbm, wuq_hbm, wk_hbm, wv_hbm,
             later_ref, a_ref, qd_ref, kvd_ref, pin_ref, gates_ref, q_ref, k_ref, v_ref, kt_ref,
             cf_ref, s1_ref, s2_ref, gathered_ref, win, wuq, wk, wv, send_sems, recv_sems, local_sem):
        step = pl.program_id(0)

        def gather(phase):
            _gather_copies(later_ref, lambda px, py, pc: gathered_ref.at[4 * px + 2 * py + pc],
                           send_sems, recv_sems, local_sem, phases=(phase,))

        pl.when(step == 0)(lambda: gather("send"))
        pl.when(step == forward_step)(lambda: gather("forward"))
        _load_once(((win_hbm, win), (wuq_hbm, wuq), (wk_hbm, wk), (wv_hbm, wv)))
        for rows in _row_chains(tm):
            n_rows = rows.stop - rows.start
            xv = x_ref[rows, :]
            a = (xv * _rms_r(xv) * gpre_ref[...]).astype(BF16)
            a_ref[rows, :] = a
            proj = _mm(a, win[...])
            qd = proj[:, IN_Q0:IN_KV0]
            kvd = proj[:, IN_KV0:IN_POOL0]
            qd_ref[rows, :] = qd
            kvd_ref[rows, :] = kvd
            pin_ref[rows, :] = proj[:, IN_POOL0:IN_GATE0]
            gates_ref[rows, :] = _sigmoid(proj[:, IN_GATE0:IN_KR0] + bg_ref[...]).astype(BF16)
            cfv, s1v, s2v = _rope_tables(pos_ref[:, rows], freq_ref[...], n_rows)
            cf_ref[rows, :], s1_ref[rows, :], s2_ref[rows, :] = cfv, s1v, s2v
            krr = _rope(proj[:, IN_KR0:IN_R], cfv, s1v, s2v)
            qn = (qd * _rms_r(qd) * gq_ref[...]).astype(BF16)
            qf = _mm(qn, wuq[...])
            kvn = (kvd * _rms_r(kvd) * gkv_ref[...]).astype(BF16)
            kf = _mm(kvn, wk[...])
            vf = _mm(kvn, wv[...])
            one_lane = (lax.broadcasted_iota(jnp.int32, (n_rows, HEAD_PAD), 1) == V_HEAD).astype(F32)
            for h in range(N_HEADS):
                lanes = slice(HEAD_PAD * h, HEAD_PAD * (h + 1))
                q_ref[h, rows, :] = (_rope(qf[:, lanes], cfv, s1v, s2v) * Q_PRESCALE).astype(BF16)
                kh = kf[:, lanes] + krr
                vh = vf[:, lanes] + one_lane
                k_ref[h, rows, :] = kh.astype(BF16)
                v_ref[h, rows, :] = vh.astype(BF16)
                kt_ref[h, :, rows] = jnp.transpose(kh).astype(BF16)
        pl.when(step == n_steps - 1)(lambda: gather("finish"))

    per_tile = ATTN_TILE // tm
    heads = pl.BlockSpec((N_HEADS, tm, HEAD_PAD), lambda i: (0, i, 0))
    heads_t = pl.BlockSpec((N_HEADS, None, HEAD_PAD, tm), lambda i: (0, i // per_tile, 0, i % per_tile))
    heads_t_shape = _sds((N_HEADS, seq // ATTN_TILE, HEAD_PAD, ATTN_TILE), BF16)
    return pl.pallas_call(
        body, name="proj_fwd", grid=(seq // tm,),
        in_specs=[_row(tm, D_MODEL), _fix((1, D_MODEL)), _fix((1, 2 * D_MODEL)), _fix((1, Q_LORA)), _fix((1, KV_LORA)),
                  pl.BlockSpec((1, tm), lambda i: (0, i)), _fix((1, HEAD_PAD)), ANY, ANY, ANY, ANY, ANY],
        out_specs=[_row(tm, D_MODEL), _row(tm, Q_LORA), _row(tm, KV_LORA), _row(tm, POOL_WIDTH), _row(tm, 2 * D_MODEL),
                   heads, heads, heads, heads_t, _row(tm, HEAD_PAD), _row(tm, HEAD_PAD), _row(tm, HEAD_PAD), ANY],
        out_shape=[_sds((seq, D_MODEL), BF16), _sds((seq, Q_LORA), F32), _sds((seq, KV_LORA), F32),
                   _sds((seq, POOL_WIDTH), F32), _sds((seq, 2 * D_MODEL), BF16),
                   _sds((N_HEADS, seq, HEAD_PAD), BF16), _sds((N_HEADS, seq, HEAD_PAD), BF16),
                   _sds((N_HEADS, seq, HEAD_PAD), BF16), heads_t_shape,
                   _sds((seq, HEAD_PAD), F32), _sds((seq, HEAD_PAD), F32), _sds((seq, HEAD_PAD), F32),
                   _sds((N_DEV,) + later_shards.shape, later_shards.dtype)],
        scratch_shapes=[pltpu.VMEM(w_in_r.shape, BF16), pltpu.VMEM(w_uq_r.shape, BF16),
                        pltpu.VMEM(w_k_exp.shape, BF16), pltpu.VMEM(w_v.shape, BF16),
                        pltpu.SemaphoreType.DMA((7,)), pltpu.SemaphoreType.DMA((7,)), pltpu.SemaphoreType.DMA],
        compiler_params=_params(1, 48),
    )(x, g_pre, b_gate, g_q, g_kv, positions, _rope_lane_frequencies(), w_in_r, w_uq_r, w_k_exp, w_v, later_shards)


def _window_count(row0, n_rows):
    t = row0 + lax.broadcasted_iota(jnp.int32, (n_rows, POOL_GROUP), 0)
    return [jnp.minimum(t + 1, w).astype(F32) for w in POOL_WINDOWS]


def _pool_fwd(pin, w_pool_bf, pool_scale, tm):
    seq = pin.shape[0]
    ext_rows = tm + POOL_HALO

    def body(prev_ref, u_ref, wp_ref, ps_ref, d_ref, pooled_ref):
        i = pl.program_id(0)
        prev = jnp.where(i == 0, 0.0, prev_ref[...])
        u = u_ref[...]
        level = jnp.concatenate([prev, u], axis=0)
        counts = _window_count(i * tm, tm)
        shift = 1
        for g in range(len(POOL_WINDOWS)):
            level = level + pltpu.roll(level, shift, 0)
            shift *= 2
            lanes = slice(POOL_GROUP * g, POOL_GROUP * (g + 1))
            d = (level[POOL_HALO:, lanes] / counts[g] - u[:, lanes]).astype(BF16)
            d_ref[:, lanes] = d
            pooled_ref[:, lanes] = (_mm(d, wp_ref[g]) * ps_ref[:, lanes]).astype(BF16)

    halo = tm // POOL_HALO
    return pl.pallas_call(
        body, name="pool_fwd", grid=(seq // tm,),
        in_specs=[pl.BlockSpec((POOL_HALO, POOL_WIDTH), lambda i: (jnp.maximum(i * halo - 1, 0), 0)),
                  _row(tm, POOL_WIDTH), _fix(w_pool_bf.shape), _fix((1, POOL_WIDTH))],
        out_specs=[_row(tm, POOL_WIDTH), _row(tm, POOL_WIDTH)],
        out_shape=[_sds((seq, POOL_WIDTH), BF16), _sds((seq, POOL_WIDTH), BF16)],
        compiler_params=_params(1, 32),
    )(pin, pin, w_pool_bf, pool_scale)


def _pool_bwd_window(dd, tm):
    seq = dd.shape[0]
    n_tiles = seq // tm
    ext_rows = tm + POOL_HALO

    def body(dd_ref, next_ref, dpin_ref):
        i = pl.program_id(0)
        nxt = jnp.where(i == n_tiles - 1, 0.0, next_ref[...])
        dd_t = dd_ref[...]
        ext = jnp.concatenate([dd_t, nxt], axis=0)
        counts = _window_count(i * tm, ext_rows)
        shift = 1
        for g in range(len(POOL_WINDOWS)):
            lanes = slice(POOL_GROUP * g, POOL_GROUP * (g + 1))
            level = ext[:, lanes] / counts[g]
            s = 1
            while s <= shift:
                level = level + pltpu.roll(level, ext_rows - s, 0)
                s *= 2
            shift *= 2
            dpin_ref[:, lanes] = (level[:tm] - dd_t[:, lanes]).astype(BF16)

    halo = tm // POOL_HALO
    return pl.pallas_call(
        body, name="pool_bwd_window", grid=(n_tiles,),
        in_specs=[_row(tm, POOL_WIDTH),
                  pl.BlockSpec((POOL_HALO, POOL_WIDTH), lambda i: (jnp.minimum((i + 1) * halo, seq // POOL_HALO - 1), 0))],
        out_specs=_row(tm, POOL_WIDTH),
        out_shape=_sds((seq, POOL_WIDTH), BF16),
        compiler_params=_params(1, 32),
    )(dd, dd)


def _col_to_row(col, n):
    return jnp.transpose(jnp.broadcast_to(col, (n, HEAD_PAD)))[0:1, :]


def _attn_fwd(q, k, v):
    heads, seq, _ = q.shape
    r, n = FWD_ROWS, FWD_CHAINS
    block = r * n

    def body(q_ref, k_ref, v_ref, o_ref, lse_ref):
        qi = pl.program_id(1)
        q_tiles = [q_ref[c * r:(c + 1) * r, :] for c in range(n)]

        def tile(qt, j, m, acc, diagonal):
            start = pl.multiple_of(j * r, r)
            s = _mm_nt(qt, k_ref[pl.ds(start, r), :])
            if diagonal:
                row = lax.broadcasted_iota(jnp.int32, (r, r), 0)
                col = lax.broadcasted_iota(jnp.int32, (r, r), 1)
                s = jnp.where(col <= row, s, -jnp.inf)
            m_new = jnp.maximum(m, jnp.max(s, axis=1, keepdims=True))
            p = jnp.exp2((s - m_new).astype(BF16))
            acc = jnp.exp2(m - m_new) * acc + _mm(p, v_ref[pl.ds(start, r), :])
            return m_new, acc

        def all_chains(jj, carry):
            for u in range(n):
                carry = tuple(tile(q_tiles[c], n * jj + u, *carry[c], False) for c in range(n))
            return carry

        init = tuple((jnp.full((r, 1), -jnp.inf, F32), jnp.zeros((r, HEAD_PAD), F32)) for _ in range(n))
        state = list(lax.fori_loop(0, qi, all_chains, init))
        for d in range(n):
            for c in range(d, n):
                state[c] = tile(q_tiles[c], n * qi + d, *state[c], c == d)
        for c, (m, acc) in enumerate(state):
            l = acc[:, V_HEAD:V_HEAD + 1]
            o_ref[c * r:(c + 1) * r, :] = (acc / l).astype(BF16)
            row0 = c * r
            lse_ref[row0 // ATTN_TILE, :, row0 % ATTN_TILE:row0 % ATTN_TILE + r] = _col_to_row(m + jnp.log2(l), r)

    return pl.pallas_call(
        body, name="attn_fwd", grid=(heads, seq // block),
        in_specs=[pl.BlockSpec((None, block, HEAD_PAD), lambda h, i: (h, i, 0)),
                  pl.BlockSpec((None, seq, HEAD_PAD), lambda h, i: (h, 0, 0)),
                  pl.BlockSpec((None, seq, HEAD_PAD), lambda h, i: (h, 0, 0))],
        out_specs=[pl.BlockSpec((None, block, HEAD_PAD), lambda h, i: (h, i, 0)),
                   pl.BlockSpec((None, block // ATTN_TILE, 1, ATTN_TILE), lambda h, i: (h, i, 0, 0))],
        out_shape=[_sds((heads, seq, HEAD_PAD), BF16), _sds((heads, seq // ATTN_TILE, 1, ATTN_TILE), F32)],
        compiler_params=_params(2, 48),
    )(q, k, v)


def _attn_delta(o, do):
    heads, seq, _ = o.shape
    tq = ATTN_TILE
    nq = seq // tq

    per_step = min(4, nq)

    def body(o_ref, do_ref, delta_ref):
        for u in range(per_step):
            rows = slice(u * tq, (u + 1) * tq)
            prod = o_ref[rows, :].astype(F32) * do_ref[rows, :].astype(F32)
            delta_ref[u] = _col_to_row(jnp.sum(prod, axis=1, keepdims=True), tq)

    tile = pl.BlockSpec((None, per_step * tq, HEAD_PAD), lambda h, i: (h, i, 0))
    return pl.pallas_call(
        body, name="attn_delta", grid=(heads, nq // per_step),
        in_specs=[tile, tile],
        out_specs=pl.BlockSpec((None, per_step, 1, tq), lambda h, i: (h, i, 0, 0)),
        out_shape=_sds((heads, nq, 1, tq), F32),
        compiler_params=_params(2, 32),
    )(o, do)


def _peer_copies(src_refs, dst_refs, send_sems, recv_sems):
    x, y, c = _position()
    copies = []
    for w, (src, dst) in enumerate(zip(src_refs, dst_refs)):
        for r in range(1, N_DEV):
            px = 1 - x if r & 4 else x
            py = 1 - y if r & 2 else y
            pc = 1 - c if r & 1 else c
            copies.append(pltpu.make_async_remote_copy(
                src_ref=src.at[4 * px + 2 * py + pc], dst_ref=dst.at[r - 1],
                send_sem=send_sems.at[(N_DEV - 1) * w + r - 1], recv_sem=recv_sems.at[(N_DEV - 1) * w + r - 1],
                device_id=(px, py, pc), device_id_type=MESH))
    return copies


def _attn_bwd(q, k, k_t, v, do, lse, delta, early_grads):
    heads, seq, _ = q.shape
    t = ATTN_TILE
    nq = seq // t
    n_w = len(early_grads)

    def body(q_ref, k_ref, kt_ref, v_ref, do_ref, lse_ref, delta_ref, *rest):
        grad_refs, rest = rest[:n_w], rest[n_w:]
        dq_ref, dk_ref, dv_ref = rest[:3]
        recv_refs, (send_sems, recv_sems) = rest[3:3 + n_w], rest[3 + n_w:]
        jp = pl.program_id(1)
        head = pl.program_id(0)

        @pl.when((head == 0) & (jp == 0))
        def _():
            for cp in _peer_copies(grad_refs, recv_refs, send_sems, recv_sems):
                cp.start()

        @pl.when(jp == 0)
        def _():
            dq_ref[...] = jnp.zeros_like(dq_ref)

        k_a, k_b = k_ref[0:t, :], k_ref[t:2 * t, :]
        v_a, v_b = v_ref[0:t, :], v_ref[t:2 * t, :]

        def tile(kt, k_tr, vt, i, dk, dv, diagonal):
            start = pl.multiple_of(i * t, t)
            qt = q_ref[pl.ds(start, t), :]
            dot = do_ref[pl.ds(start, t), :]
            p_t = jnp.exp2(_mm_nt(kt, qt) - lse_ref[i])
            if diagonal:
                key = lax.broadcasted_iota(jnp.int32, (t, t), 0)
                query = lax.broadcasted_iota(jnp.int32, (t, t), 1)
                p_t = jnp.where(key <= query, p_t, 0.0)
            dv = dv + _mm(p_t.astype(BF16), dot)
            ds_t = (p_t * (_mm_nt(vt, dot) - delta_ref[i])).astype(BF16)
            dk = dk + _mm(ds_t, qt)
            return dk, dv, _mm(k_tr, ds_t)

        def add_dq(i, dq):
            dq_ref[i] += dq

        def both(ip, carry):
            dk_a, dv_a, dk_b, dv_b = carry
            for i in (2 * ip, 2 * ip + 1):
                dk_a, dv_a, dq_a = tile(k_a, kt_ref[0], v_a, i, dk_a, dv_a, False)
                dk_b, dv_b, dq_b = tile(k_b, kt_ref[1], v_b, i, dk_b, dv_b, False)
                add_dq(i, dq_a + dq_b)
            return dk_a, dv_a, dk_b, dv_b

        zero = jnp.zeros((t, HEAD_PAD), F32)
        dk_a, dv_a, dq_a = tile(k_a, kt_ref[0], v_a, 2 * jp, zero, zero, True)
        add_dq(2 * jp, dq_a)
        dk_a, dv_a, dq_a = tile(k_a, kt_ref[0], v_a, 2 * jp + 1, dk_a, dv_a, False)
        dk_b, dv_b, dq_b = tile(k_b, kt_ref[1], v_b, 2 * jp + 1, zero, zero, True)
        add_dq(2 * jp + 1, dq_a + dq_b)
        dk_a, dv_a, dk_b, dv_b = lax.fori_loop(jp + 1, nq // 2, both, (dk_a, dv_a, dk_b, dv_b))
        dk_ref[0:t, :] = (dk_a * (1.0 / LOG2E)).astype(BF16)
        dk_ref[t:2 * t, :] = (dk_b * (1.0 / LOG2E)).astype(BF16)
        dv_ref[0:t, :] = dv_a.astype(BF16)
        dv_ref[t:2 * t, :] = dv_b.astype(BF16)

        @pl.when((head == heads - 1) & (jp == nq // 2 - 1))
        def _():
            copies = _peer_copies(grad_refs, recv_refs, send_sems, recv_sems)
            for cp in copies:
                cp.wait_recv()
            for cp in copies:
                cp.wait_send()

    whole = pl.BlockSpec((None, seq, HEAD_PAD), lambda h, j: (h, 0, 0))
    whole_t = pl.BlockSpec((None, nq, HEAD_PAD, t), lambda h, j: (h, 0, 0, 0))
    pair = pl.BlockSpec((None, 2 * t, HEAD_PAD), lambda h, j: (h, j, 0))
    pair_t = pl.BlockSpec((None, 2, HEAD_PAD, t), lambda h, j: (h, j, 0, 0))
    stats = pl.BlockSpec((None, nq, 1, t), lambda h, j: (h, 0, 0, 0))
    return pl.pallas_call(
        body, name="attn_bwd", grid=(heads, nq // 2),
        in_specs=[whole, pair, pair_t, pair, whole, stats, stats] + [ANY] * n_w,
        out_specs=[whole_t, pair, pair] + [ANY] * n_w,
        out_shape=[_sds((heads, nq, HEAD_PAD, t), F32), _sds((heads, seq, HEAD_PAD), BF16),
                   _sds((heads, seq, HEAD_PAD), BF16)]
                  + [_sds((N_DEV - 1,) + g.shape[1:], g.dtype) for g in early_grads],
        scratch_shapes=[pltpu.SemaphoreType.DMA(((N_DEV - 1) * n_w,)), pltpu.SemaphoreType.DMA(((N_DEV - 1) * n_w,))],
        compiler_params=_params(2, 56),
    )(q, k, k_t, v, do, lse, delta, *early_grads)


def _merge_fwd(attn, pooled, gates, x, g_post_mix, w_ba, w_bb, w_out, tm):
    seq = x.shape[0]

    def body(attn_ref, pooled_ref, gates_ref, x_ref, g_ref, wba_ref, wbb_ref, wout_ref,
             merged_ref, ba_ref, bb_ref, y_ref, h1_ref, attn_rows_ref):
        for rows in _row_chains(tm):
            attn = jnp.concatenate([attn_ref[h, rows, :] for h in range(N_HEADS)], axis=1)
            attn_rows_ref[rows, :] = attn
            ba = _mm(attn, wba_ref[...])
            bb = _mm(pooled_ref[rows, :], wbb_ref[...])
            ba_ref[rows, :] = ba.astype(BF16)
            bb_ref[rows, :] = bb.astype(BF16)
            merged = (gates_ref[rows, :D_MODEL].astype(F32) * ba
                      + gates_ref[rows, D_MODEL:].astype(F32) * bb).astype(BF16)
            merged_ref[rows, :] = merged
            y = _mm(merged, wout_ref[...])
            y_ref[rows, :] = y
            h1_ref[rows, :] = x_ref[rows, :] + y * _rms_r(y) * g_ref[...]

    return pl.pallas_call(
        body, name="merge_fwd", grid=(seq // tm,),
        in_specs=[pl.BlockSpec((N_HEADS, tm, HEAD_PAD), lambda i: (0, i, 0)), _row(tm, POOL_WIDTH),
                  _row(tm, 2 * D_MODEL), _row(tm, D_MODEL),
                  _fix((1, D_MODEL)), _fix(w_ba.shape), _fix(w_bb.shape), _fix(w_out.shape)],
        out_specs=[_row(tm, D_MODEL)] * 5 + [_row(tm, N_HEADS * HEAD_PAD)],
        out_shape=[_sds((seq, D_MODEL), BF16), _sds((seq, D_MODEL), BF16), _sds((seq, D_MODEL), BF16),
                   _sds((seq, D_MODEL), F32), _sds((seq, D_MODEL), F32), _sds((seq, N_HEADS * HEAD_PAD), BF16)],
        compiler_params=_params(1, 48),
    )(attn, pooled, gates, x, g_post_mix, w_ba, w_bb, w_out)


def _tail_fwd(h1, target, p, g_pre_mlp, g_post_mlp, g_ple, w_ff1, w_ff2, w_pe, w_pg, tm):
    seq = h1.shape[0]

    def body(h1_ref, tgt_ref, p_ref, gm_ref, gf_ref, gp_ref, w1_hbm, w2_hbm, wpe_hbm, wpg_hbm,
             m_ref, zr_ref, f_ref, h2b_ref, pb_ref, de_ref, dpre_ref, dh2_ref, loss_ref, dgple_ref,
             w1, w2, wpe, wpg):
        _load_once(((w1_hbm, w1), (w2_hbm, w2), (wpe_hbm, wpe), (wpg_hbm, wpg)))

        @pl.when(pl.program_id(0) == 0)
        def _():
            loss_ref[...] = jnp.zeros_like(loss_ref)
            dgple_ref[...] = jnp.zeros_like(dgple_ref)

        h1v = h1_ref[...]
        m = (h1v * _rms_r(h1v) * gm_ref[...]).astype(BF16)
        m_ref[...] = m
        zr = jnp.maximum(_mm(m, w1[...]), 0.0)
        zr_ref[...] = zr.astype(BF16)
        f = _mm((zr * zr).astype(BF16), w2[...])
        f_ref[...] = f
        h2 = h1v + f * _rms_r(f) * gf_ref[...]
        h2b = h2.astype(BF16)
        h2b_ref[...] = h2b
        pb = p_ref[...].astype(BF16)
        pb_ref[...] = pb
        e = _mm(pb, wpe[...])
        pg = _sigmoid(_mm(h2b, wpg[...]))
        t3 = pg * e
        r3 = _rms_r(t3)
        t3hat = t3 * r3
        diff = h2 + t3hat * gp_ref[...] - tgt_ref[...]
        loss_ref[...] += jnp.sum(diff * diff) * (0.5 / D_MODEL)
        dh3 = diff * (1.0 / D_MODEL)
        dgple_ref[...] += _colsum(dh3 * t3hat)
        dt3 = _rms_bwd(t3hat, r3, gp_ref[...], dh3)
        de_ref[...] = (dt3 * pg).astype(BF16)
        dpre = (dt3 * e * pg * (1.0 - pg)).astype(BF16)
        dpre_ref[...] = dpre
        dh2_ref[...] = dh3 + _mm_nt(dpre, wpg[...])

    return pl.pallas_call(
        body, name="tail_fwd", grid=(seq // tm,),
        in_specs=[_row(tm, D_MODEL), _row(tm, D_MODEL), _row(tm, PLE_DIM), _fix((1, D_MODEL)), _fix((1, D_MODEL)),
                  _fix((1, D_MODEL)), ANY, ANY, ANY, ANY],
        out_specs=[_row(tm, D_MODEL), _row(tm, D_FF), _row(tm, D_MODEL), _row(tm, D_MODEL), _row(tm, PLE_DIM),
                   _row(tm, D_MODEL), _row(tm, D_MODEL), _row(tm, D_MODEL), _fix((8, 128)), _fix((1, D_MODEL))],
        out_shape=[_sds((seq, D_MODEL), BF16), _sds((seq, D_FF), BF16), _sds((seq, D_MODEL), F32),
                   _sds((seq, D_MODEL), BF16), _sds((seq, PLE_DIM), BF16), _sds((seq, D_MODEL), BF16),
                   _sds((seq, D_MODEL), BF16), _sds((seq, D_MODEL), F32), _sds((8, 128), F32), _sds((1, D_MODEL), F32)],
        scratch_shapes=[pltpu.VMEM(w_ff1.shape, BF16), pltpu.VMEM(w_ff2.shape, BF16),
                        pltpu.VMEM(w_pe.shape, BF16), pltpu.VMEM(w_pg.shape, BF16)],
        compiler_params=_params(1, 56),
    )(h1, target, p, g_pre_mlp, g_post_mlp, g_ple, w_ff1, w_ff2, w_pe, w_pg)


def _mlp_bwd(h1, f, zr, dh2, g_pre_mlp, g_post_mlp, w_ff1, w_ff2, tm):
    seq = h1.shape[0]

    def body(h1_ref, f_ref, zr_ref, dh2_ref, gm_ref, gf_ref, w1_hbm, w2_hbm,
             df_ref, dz_ref, dh1_ref, dgm_ref, dgf_ref, w1, w2):
        _load_once(((w1_hbm, w1), (w2_hbm, w2)))

        @pl.when(pl.program_id(0) == 0)
        def _():
            dgm_ref[...] = jnp.zeros_like(dgm_ref)
            dgf_ref[...] = jnp.zeros_like(dgf_ref)

        dh2 = dh2_ref[...]
        fv = f_ref[...]
        rf = _rms_r(fv)
        fhat = fv * rf
        dgf_ref[...] += _colsum(dh2 * fhat)
        df = _rms_bwd(fhat, rf, gf_ref[...], dh2).astype(BF16)
        df_ref[...] = df
        dz = (_mm_nt(df, w2[...]) * (2.0 * zr_ref[...].astype(F32))).astype(BF16)
        dz_ref[...] = dz
        dm = _mm_nt(dz, w1[...])
        h1v = h1_ref[...]
        r1 = _rms_r(h1v)
        h1hat = h1v * r1
        dgm_ref[...] += _colsum(dm * h1hat)
        dh1_ref[...] = dh2 + _rms_bwd(h1hat, r1, gm_ref[...], dm)

    return pl.pallas_call(
        body, name="mlp_bwd", grid=(seq // tm,),
        in_specs=[_row(tm, D_MODEL), _row(tm, D_MODEL), _row(tm, D_FF), _row(tm, D_MODEL),
                  _fix((1, D_MODEL)), _fix((1, D_MODEL)), ANY, ANY],
        out_specs=[_row(tm, D_MODEL), _row(tm, D_FF), _row(tm, D_MODEL), _fix((1, D_MODEL)), _fix((1, D_MODEL))],
        out_shape=[_sds((seq, D_MODEL), BF16), _sds((seq, D_FF), BF16), _sds((seq, D_MODEL), F32),
                   _sds((1, D_MODEL), F32), _sds((1, D_MODEL), F32)],
        scratch_shapes=[pltpu.VMEM(w_ff1.shape, BF16), pltpu.VMEM(w_ff2.shape, BF16)],
        compiler_params=_params(1, 56),
    )(h1, f, zr, dh2, g_pre_mlp, g_post_mlp, w_ff1, w_ff2)


def _merge_bwd(dh1, y, gates, ba, bb, d_pool, g_post_mix, pool_scale, w_out, w_ba, w_bb, w_pool_bf, tm):
    seq = dh1.shape[0]

    def body(dh1_ref, y_ref, gates_ref, ba_ref, bb_ref, d_ref, g_ref, ps_ref, wout_ref, wba_ref, wbb_ref, wp_ref,
             dy_ref, dba_ref, dbb_ref, dgpre_ref, dattn_ref, dd_ref, dg_ref, dbg_ref, dps_ref, dwp_ref):
        @pl.when(pl.program_id(0) == 0)
        def _():
            dg_ref[...] = jnp.zeros_like(dg_ref)
            dbg_ref[...] = jnp.zeros_like(dbg_ref)
            dps_ref[...] = jnp.zeros_like(dps_ref)
            dwp_ref[...] = jnp.zeros_like(dwp_ref)

        for rows in _row_chains(tm):
            dh1v = dh1_ref[rows, :]
            yv = y_ref[rows, :]
            r = _rms_r(yv)
            yhat = yv * r
            dg_ref[...] += _colsum(dh1v * yhat)
            dy = _rms_bwd(yhat, r, g_ref[...], dh1v).astype(BF16)
            dy_ref[rows, :] = dy
            dmerged = _mm_nt(dy, wout_ref[...])
            dbranch = []
            for half, branch_ref, dbranch_ref in ((0, ba_ref, dba_ref), (1, bb_ref, dbb_ref)):
                lanes = slice(D_MODEL * half, D_MODEL * (half + 1))
                gate = gates_ref[rows, lanes].astype(F32)
                dpre = dmerged * branch_ref[rows, :].astype(F32) * gate * (1.0 - gate)
                dbg_ref[:, lanes] += _colsum(dpre)
                dgpre_ref[rows, lanes] = dpre.astype(BF16)
                dbranch.append((dmerged * gate).astype(BF16))
                dbranch_ref[rows, :] = dbranch[-1]
            dattn = _mm_nt(dbranch[0], wba_ref[...]).astype(BF16)
            for h in range(N_HEADS):
                dattn_ref[h, rows, :] = dattn[:, HEAD_PAD * h:HEAD_PAD * (h + 1)]
            dpooled = _mm_nt(dbranch[1], wbb_ref[...])
            for g in range(len(POOL_WINDOWS)):
                lanes = slice(POOL_GROUP * g, POOL_GROUP * (g + 1))
                dpl = dpooled[:, lanes]
                d_g = d_ref[rows, lanes]
                dps_ref[:, lanes] += _colsum(dpl * _mm(d_g, wp_ref[g]))
                dyp = (dpl * ps_ref[:, lanes]).astype(BF16)
                dwp_ref[g] += lax.dot_general(d_g, dyp, TN, preferred_element_type=F32)
                dd_ref[rows, lanes] = _mm_nt(dyp, wp_ref[g])

    return pl.pallas_call(
        body, name="merge_bwd", grid=(seq // tm,),
        in_specs=[_row(tm, D_MODEL), _row(tm, D_MODEL), _row(tm, 2 * D_MODEL), _row(tm, D_MODEL), _row(tm, D_MODEL),
                  _row(tm, POOL_WIDTH), _fix((1, D_MODEL)), _fix((1, POOL_WIDTH)),
                  _fix(w_out.shape), _fix(w_ba.shape), _fix(w_bb.shape), _fix(w_pool_bf.shape)],
        out_specs=[_row(tm, D_MODEL), _row(tm, D_MODEL), _row(tm, D_MODEL), _row(tm, 2 * D_MODEL),
                   pl.BlockSpec((N_HEADS, tm, HEAD_PAD), lambda i: (0, i, 0)), _row(tm, POOL_WIDTH),
                   _fix((1, D_MODEL)), _fix((1, 2 * D_MODEL)), _fix((1, POOL_WIDTH)), _fix(w_pool_bf.shape)],
        out_shape=[_sds((seq, D_MODEL), BF16), _sds((seq, D_MODEL), BF16), _sds((seq, D_MODEL), BF16),
                   _sds((seq, 2 * D_MODEL), BF16), _sds((N_HEADS, seq, HEAD_PAD), BF16),
                   _sds((seq, POOL_WIDTH), F32), _sds((1, D_MODEL), F32), _sds((1, 2 * D_MODEL), F32),
                   _sds((1, POOL_WIDTH), F32), _sds(w_pool_bf.shape, F32)],
        compiler_params=_params(1, 48),
    )(dh1, y, gates, ba, bb, d_pool, g_post_mix, pool_scale, w_out, w_ba, w_bb, w_pool_bf)


def _proj_bwd(dq, dk, dv, qd, kvd, x, dh1, dgpre, dpin, cf, s1, s2, g_pre, g_q, g_kv,
              w_in_r, w_uq_r, w_k_exp, w_v, tm):
    seq = x.shape[0]

    def body(dq_ref, dk_ref, dv_ref, qd_ref, kvd_ref, x_ref, dh1_ref, dgpre_ref, dpin_ref, cf_ref, s1_ref, s2_ref,
             gpre_ref, gq_ref, gkv_ref, win_hbm, wuq_hbm, wk_hbm, wv_hbm,
             gx_ref, dproj_ref, dqb_ref, qn_ref, kvn_ref, dkb_ref, dvb_ref, dgpre_acc, dgq_acc, dgkv_acc,
             win, wuq, wk, wv):
        _load_once(((win_hbm, win), (wuq_hbm, wuq), (wk_hbm, wk), (wv_hbm, wv)))

        @pl.when(pl.program_id(0) == 0)
        def _():
            dgpre_acc[...] = jnp.zeros_like(dgpre_acc)
            dgq_acc[...] = jnp.zeros_like(dgq_acc)
            dgkv_acc[...] = jnp.zeros_like(dgkv_acc)

        for rows in _row_chains(tm):
            n_rows = rows.stop - rows.start
            cfv, s1v, s2v = cf_ref[rows, :], s1_ref[rows, :], s2_ref[rows, :]
            ksum = jnp.zeros((n_rows, HEAD_PAD), F32)
            for h in range(N_HEADS):
                lanes = slice(HEAD_PAD * h, HEAD_PAD * (h + 1))
                dqh = jnp.transpose(dq_ref[h, :, rows])
                dqb_ref[rows, lanes] = (_rope_t(dqh, cfv, s1v, s2v) * ATTN_SCALE).astype(BF16)
                dkh = dk_ref[h, rows, :]
                dkb_ref[rows, lanes] = dkh
                dvb_ref[rows, lanes] = dv_ref[h, rows, :]
                ksum = ksum + dkh.astype(F32)
            lane = lax.broadcasted_iota(jnp.int32, (n_rows, HEAD_PAD), 1)
            rope_lanes = (lane >= QK_NOPE) & (lane < QK_NOPE + QK_ROPE)
            dkr = _rope_t(jnp.where(rope_lanes, ksum, 0.0), cfv, s1v, s2v)

            qdv = qd_ref[rows, :]
            rq = _rms_r(qdv)
            qhat = qdv * rq
            qn_ref[rows, :] = (qhat * gq_ref[...]).astype(BF16)
            dqn = _mm_nt(dqb_ref[rows, :], wuq[...])
            dgq_acc[...] += _colsum(dqn * qhat)
            dproj_ref[rows, IN_Q0:IN_KV0] = _rms_bwd(qhat, rq, gq_ref[...], dqn).astype(BF16)

            kvdv = kvd_ref[rows, :]
            rkv = _rms_r(kvdv)
            kvhat = kvdv * rkv
            kvn_ref[rows, :] = (kvhat * gkv_ref[...]).astype(BF16)
            dkvn = _mm_nt(dkb_ref[rows, :], wk[...]) + _mm_nt(dvb_ref[rows, :], wv[...])
            dgkv_acc[...] += _colsum(dkvn * kvhat)
            dproj_ref[rows, IN_KV0:IN_POOL0] = _rms_bwd(kvhat, rkv, gkv_ref[...], dkvn).astype(BF16)

            dproj_ref[rows, IN_POOL0:IN_GATE0] = dpin_ref[rows, :]
            dproj_ref[rows, IN_GATE0:IN_KR0] = dgpre_ref[rows, :]
            dproj_ref[rows, IN_KR0:IN_R] = dkr.astype(BF16)

            da = _mm_nt(dproj_ref[rows, :], win[...])
            xv = x_ref[rows, :]
            r0 = _rms_r(xv)
            xhat = xv * r0
            dgpre_acc[...] += _colsum(da * xhat)
            gx_ref[rows, :] = dh1_ref[rows, :] + _rms_bwd(xhat, r0, gpre_ref[...], da)

    per_tile = ATTN_TILE // tm
    heads = pl.BlockSpec((N_HEADS, tm, HEAD_PAD), lambda i: (0, i, 0))
    heads_t = pl.BlockSpec((N_HEADS, None, HEAD_PAD, tm), lambda i: (0, i // per_tile, 0, i % per_tile))
    return pl.pallas_call(
        body, name="proj_bwd", grid=(seq // tm,),
        in_specs=[heads_t, heads, heads, _row(tm, Q_LORA), _row(tm, KV_LORA), _row(tm, D_MODEL),
                  _row(tm, D_MODEL), _row(tm, 2 * D_MODEL), _row(tm, POOL_WIDTH),
                  _row(tm, HEAD_PAD), _row(tm, HEAD_PAD), _row(tm, HEAD_PAD),
                  _fix((1, D_MODEL)), _fix((1, Q_LORA)), _fix((1, KV_LORA)), ANY, ANY, ANY, ANY],
        out_specs=[_row(tm, D_MODEL), _row(tm, IN_R), _row(tm, N_HEADS * HEAD_PAD), _row(tm, Q_LORA), _row(tm, KV_LORA),
                   _row(tm, N_HEADS * HEAD_PAD), _row(tm, N_HEADS * HEAD_PAD),
                   _fix((1, D_MODEL)), _fix((1, Q_LORA)), _fix((1, KV_LORA))],
        out_shape=[_sds((seq, D_MODEL), F32), _sds((seq, IN_R), BF16), _sds((seq, N_HEADS * HEAD_PAD), BF16),
                   _sds((seq, Q_LORA), BF16), _sds((seq, KV_LORA), BF16), _sds((seq, N_HEADS * HEAD_PAD), BF16),
                   _sds((seq, N_HEADS * HEAD_PAD), BF16),
                   _sds((1, D_MODEL), F32), _sds((1, Q_LORA), F32), _sds((1, KV_LORA), F32)],
        scratch_shapes=[pltpu.VMEM(w_in_r.shape, BF16), pltpu.VMEM(w_uq_r.shape, BF16),
                        pltpu.VMEM(w_k_exp.shape, BF16), pltpu.VMEM(w_v.shape, BF16)],
        compiler_params=_params(1, 58),
    )(dq, dk, dv, qd, kvd, x, dh1, dgpre, dpin, cf, s1, s2, g_pre, g_q, g_kv, w_in_r, w_uq_r, w_k_exp, w_v)


def _grad_w(a, b, name, square_a=False, by_device=False, with_bf16=False):
    seq, k_dim = a.shape
    n_dim = b.shape[1]
    tk = min(k_dim, 1024)
    tn = n_dim // 2 if n_dim == IN_R else min(n_dim, 1024)
    ts = min(seq, 1024)
    shard = n_dim // N_DEV
    per_tile = tn // shard
    if by_device:
        out_spec = pl.BlockSpec((per_tile, tk, shard), lambda i, j, s: (j, i, 0))
        out_shape = _sds((N_DEV, k_dim, shard), F32)
    else:
        out_spec = pl.BlockSpec((tk, tn), lambda i, j, s: (i, j))
        out_shape = _sds((k_dim, n_dim), F32)

    n_seq_steps = seq // ts

    def body(a_ref, b_ref, o_ref, *narrow):
        @pl.when(pl.program_id(2) == 0)
        def _():
            o_ref[...] = jnp.zeros_like(o_ref)

        at = a_ref[...]
        if square_a:
            at = at * at
        part = lax.dot_general(at, b_ref[...], TN, preferred_element_type=F32)
        if by_device:
            for d in range(per_tile):
                o_ref[d] += part[:, d * shard:(d + 1) * shard]
        else:
            o_ref[...] += part
        if with_bf16:
            @pl.when(pl.program_id(2) == n_seq_steps - 1)
            def _():
                narrow[0][...] = o_ref[...].astype(BF16)

    return pl.pallas_call(
        body, name=name, grid=(k_dim // tk, n_dim // tn, n_seq_steps),
        in_specs=[pl.BlockSpec((ts, tk), lambda i, j, s: (s, i)), pl.BlockSpec((ts, tn), lambda i, j, s: (s, j))],
        out_specs=[out_spec, out_spec] if with_bf16 else out_spec,
        out_shape=[out_shape, _sds(out_shape.shape, BF16)] if with_bf16 else out_shape,
        compiler_params=_params(3, 48),
    )(a, b)


def _position():
    return lax.axis_index("x"), lax.axis_index("y"), lax.axis_index("c")


def _gather_copies(x_ref, slot, send_sems, recv_sems, local_sem, phases=("send", "forward", "finish")):
    x, y, c = _position()
    me, sibling = (x, y, c), (x, y, 1 - c)
    chips = [(1 - x, y), (x, 1 - y), (1 - x, 1 - y)]

    def copy(k, block, to, src=None):
        return pltpu.make_async_remote_copy(
            src_ref=slot(*block) if src is None else src, dst_ref=slot(*block),
            send_sem=send_sems.at[k], recv_sem=recv_sems.at[k], device_id=to, device_id_type=MESH)

    mine = pltpu.make_async_copy(x_ref, slot(*me), local_sem)
    first = [copy(0, me, sibling, src=x_ref)]
    first += [copy(1 + j, me, (*chip, c), src=x_ref) for j, chip in enumerate(chips)]
    passed = [copy(4 + j, (*chip, c), sibling) for j, chip in enumerate(chips)]
    if "send" in phases:
        mine.start()
        for cp in first:
            cp.start()
    if "forward" in phases:
        for j, chip in enumerate(chips):
            copy(1 + j, (*chip, c), me).wait_recv()
            passed[j].start()
    if "finish" in phases:
        copy(0, sibling, me).wait_recv()
        for j, chip in enumerate(chips):
            copy(4 + j, (*chip, 1 - c), me).wait_recv()
        for cp in first + passed:
            cp.wait_send()
        mine.wait()


def _all_gather_hbm(block):
    def body(x_ref, out_ref, send_sems, recv_sems, local_sem):
        _gather_copies(x_ref, lambda px, py, pc: out_ref.at[4 * px + 2 * py + pc], send_sems, recv_sems, local_sem)

    return pl.pallas_call(
        body, name="gather_weights",
        in_specs=[ANY], out_specs=ANY,
        out_shape=_sds((N_DEV,) + block.shape, block.dtype),
        scratch_shapes=[pltpu.SemaphoreType.DMA((7,)), pltpu.SemaphoreType.DMA((7,)), pltpu.SemaphoreType.DMA],
    )(block)


def _all_reduce_small(block):
    def body(x_ref, out_ref, buf, send_sems, recv_sems, local_sem):
        _gather_copies(x_ref, lambda px, py, pc: buf.at[4 * px + 2 * py + pc], send_sems, recv_sems, local_sem)
        acc = buf[0]
        for k in range(1, N_DEV):
            acc = acc + buf[k]
        out_ref[...] = acc

    vmem = pl.BlockSpec(memory_space=pltpu.VMEM)
    return pl.pallas_call(
        body, name="all_reduce_small",
        in_specs=[vmem], out_specs=vmem,
        out_shape=_sds(block.shape, F32),
        scratch_shapes=[pltpu.VMEM((N_DEV,) + block.shape, F32), pltpu.SemaphoreType.DMA((7,)),
                        pltpu.SemaphoreType.DMA((7,)), pltpu.SemaphoreType.DMA],
        compiler_params=pltpu.CompilerParams(vmem_limit_bytes=32 * MIB),
    )(block)


def _exchange_pair(gs):
    n_w = len(gs)

    def body(*refs):
        g_refs, out_refs = refs[:n_w], refs[n_w:2 * n_w]
        send_sems, recv_sems = refs[2 * n_w:]
        x, y, c = _position()
        copies = []
        for w in range(n_w):
            for chip in range(4):
                cp = pltpu.make_async_remote_copy(
                    src_ref=g_refs[w].at[2 * chip + (1 - c)], dst_ref=out_refs[w].at[chip],
                    send_sem=send_sems.at[4 * w + chip], recv_sem=recv_sems.at[4 * w + chip],
                    device_id=(x, y, 1 - c), device_id_type=MESH)
                cp.start()
                copies.append(cp)
        for cp in copies:
            cp.wait_recv()
        for cp in copies:
            cp.wait_send()

    return pl.pallas_call(
        body, name="exchange_pair",
        in_specs=[ANY] * n_w, out_specs=[ANY] * n_w,
        out_shape=[_sds((4,) + g.shape[1:], g.dtype) for g in gs],
        scratch_shapes=[pltpu.SemaphoreType.DMA((4 * n_w,)), pltpu.SemaphoreType.DMA((4 * n_w,))],
    )(*gs)


def _exchange_chips(parts):
    n_w = len(parts)

    def body(*refs):
        p_refs, out_refs = refs[:n_w], refs[n_w:2 * n_w]
        send_sems, recv_sems = refs[2 * n_w:]
        x, y, c = _position()
        chips = [(1 - x, y), (x, 1 - y), (1 - x, 1 - y)]
        copies = []
        for w in range(n_w):
            for k, (px, py) in enumerate(chips):
                cp = pltpu.make_async_remote_copy(
                    src_ref=p_refs[w].at[2 * px + py], dst_ref=out_refs[w].at[k],
                    send_sem=send_sems.at[3 * w + k], recv_sem=recv_sems.at[3 * w + k],
                    device_id=(px, py, c), device_id_type=MESH)
                cp.start()
                copies.append(cp)
        for cp in copies:
            cp.wait_recv()
        for cp in copies:
            cp.wait_send()

    return pl.pallas_call(
        body, name="exchange_chips",
        in_specs=[ANY] * n_w, out_specs=[ANY] * n_w,
        out_shape=[_sds((3,) + p.shape[1:], p.dtype) for p in parts],
        scratch_shapes=[pltpu.SemaphoreType.DMA((3 * n_w,)), pltpu.SemaphoreType.DMA((3 * n_w,))],
    )(*parts)


def _row_tile(k):
    return 256 if k % 256 == 0 else 128


def _pair_sum(g, recv, place, name):
    _, k, n = g.shape
    tr = _row_tile(k)
    g4 = g.reshape(4, 2, k, n)

    def body(s_ref, g_ref, r_ref, o_ref):
        o_ref[...] = (g_ref[...] + r_ref[...]).astype(BF16)

    spec = pltpu.PrefetchScalarGridSpec(
        num_scalar_prefetch=1, grid=(4, k // tr),
        in_specs=[pl.BlockSpec((None, None, tr, n), lambda j, i, s: (j, s[2], i, 0)),
                  pl.BlockSpec((None, tr, n), lambda j, i, s: (j, i, 0))],
        out_specs=pl.BlockSpec((None, tr, n), lambda j, i, s: (j, i, 0)))
    return pl.pallas_call(
        body, name=name, grid_spec=spec, out_shape=_sds((4, k, n), BF16),
        compiler_params=_params(2, 32),
    )(place, g4, recv)


def _adamw_math(g, w, m, v):
    m = ADAM_B1 * m + (1.0 - ADAM_B1) * g
    v = ADAM_B2 * v + (1.0 - ADAM_B2) * (g * g)
    m_hat = m / (1.0 - ADAM_B1 ** ADAM_STEP)
    v_hat = v / (1.0 - ADAM_B2 ** ADAM_STEP)
    delta = -ADAM_LR * (m_hat / (jnp.sqrt(v_hat) + ADAM_EPS) + ADAM_WD * w)
    return delta, m, v


def _adamw_sharded(g, from_sibling, from_chips, place, w, m, v, name):
    _, k, n = g.shape
    tr = _row_tile(k)

    def body(s_ref, g_ref, sib_ref, r0_ref, r1_ref, r2_ref, w_ref, m_ref, v_ref, grad_ref, d_ref, nm_ref, nv_ref):
        grad = g_ref[...] + sib_ref[...]
        for r_ref in (r0_ref, r1_ref, r2_ref):
            grad = grad + r_ref[...].astype(F32)
        grad_ref[...] = grad
        d_ref[...], nm_ref[...], nv_ref[...] = _adamw_math(grad, w_ref[...], m_ref[...], v_ref[...])

    tile = pl.BlockSpec((None, tr, n), lambda i, s: (0, i, 0))

    def slot(j):
        return pl.BlockSpec((None, tr, n), lambda i, s: (j, i, 0))

    spec = pltpu.PrefetchScalarGridSpec(
        num_scalar_prefetch=1, grid=(k // tr,),
        in_specs=[pl.BlockSpec((None, tr, n), lambda i, s: (s[0], i, 0)),
                  pl.BlockSpec((None, tr, n), lambda i, s: (s[1], i, 0)),
                  slot(0), slot(1), slot(2), tile, tile, tile],
        out_specs=[tile] * 4)
    return pl.pallas_call(
        body, name=name, grid_spec=spec, out_shape=[_sds((1, k, n), F32)] * 4,
        compiler_params=_params(1, 48),
    )(place, g, from_sibling, from_chips, from_chips, from_chips, w, m, v)


def _adamw_direct(g, received, place, w, m, v, name):
    _, k, n = g.shape
    tr = _row_tile(k)

    def body(s_ref, g_ref, r_ref, w_ref, m_ref, v_ref, grad_ref, d_ref, nm_ref, nv_ref):
        grad = g_ref[...]
        for r in range(N_DEV - 1):
            grad = grad + r_ref[r].astype(F32)
        grad_ref[...] = grad
        d_ref[...], nm_ref[...], nv_ref[...] = _adamw_math(grad, w_ref[...], m_ref[...], v_ref[...])

    tile = pl.BlockSpec((None, tr, n), lambda i, s: (0, i, 0))
    spec = pltpu.PrefetchScalarGridSpec(
        num_scalar_prefetch=1, grid=(k // tr,),
        in_specs=[pl.BlockSpec((None, tr, n), lambda i, s: (s[0], i, 0)),
                  pl.BlockSpec((N_DEV - 1, tr, n), lambda i, s: (0, i, 0)), tile, tile, tile],
        out_specs=[tile] * 4)
    return pl.pallas_call(
        body, name=name, grid_spec=spec, out_shape=[_sds((1, k, n), F32)] * 4,
        compiler_params=_params(1, 48),
    )(place, g, received, w, m, v)


def _adamw_small(g, w, m, v):
    def body(g_ref, w_ref, m_ref, v_ref, d_ref, nm_ref, nv_ref):
        d_ref[...], nm_ref[...], nv_ref[...] = _adamw_math(g_ref[...], w_ref[...], m_ref[...], v_ref[...])

    return pl.pallas_call(body, name="adamw_small", out_shape=[_sds(g.shape, F32)] * 3)(g, w, m, v)


def _pack_rows(parts):
    parts = [a.reshape(-1, LANES) for a in parts]
    pad = (-sum(a.shape[0] for a in parts)) % PACK_ROW_TILE
    return jnp.concatenate(parts + [jnp.zeros((pad, LANES), parts[0].dtype)], axis=0)


def _pack_small(parts):
    flat = jnp.concatenate([a.reshape(-1) for a in parts])
    pad = (-flat.shape[0]) % (8 * LANES)
    return jnp.pad(flat, (0, pad)).reshape(-1, LANES)


def _unpack_small(packed, shapes):
    flat = packed.reshape(-1)
    out, off = [], 0
    for shape in shapes:
        size = 1
        for n in shape:
            size *= n
        out.append(flat[off:off + size].reshape(shape))
        off += size
    return out


def _full_from_gathered(gathered, entries, shard_shapes):
    out, off = {}, 0
    for (name, kind), (k, n) in zip(entries, shard_shapes):
        rows = k * n // LANES
        seg = gathered[:, off:off + rows].reshape(N_DEV, k, n)
        out[name] = jnp.transpose(seg, (1, 0, 2)).reshape(k, N_DEV * n) if kind == "col" else seg.reshape(N_DEV * k, n)
        off += rows
    return out


def _columns_by_device(a):
    k, n_all = a.shape
    return jnp.transpose(a.reshape(k, N_DEV, n_all // N_DEV), (1, 0, 2))


def _rows_by_device(a):
    k_all, n = a.shape
    return a.reshape(N_DEV, k_all // N_DEV, n)


def _rope_lane_frequencies():
    inv_freq = ROPE_THETA ** (-jnp.arange(0, QK_ROPE, 2, dtype=F32) / QK_ROPE)
    zeros = lambda n: jnp.zeros((n,), F32)
    return jnp.concatenate([zeros(QK_NOPE), inv_freq, inv_freq, zeros(HEAD_PAD - QK_NOPE - QK_ROPE)])[None, :]


def _rope_tables(pos_row, freq, tm):
    pos = jnp.transpose(jnp.broadcast_to(pos_row.astype(F32), (HEAD_PAD, tm)))
    ang = pos * freq
    cos, sin = jnp.cos(ang), jnp.sin(ang)
    lane = lax.broadcasted_iota(jnp.int32, (tm, HEAD_PAD), 1)
    first = (lane >= QK_NOPE) & (lane < QK_NOPE + QK_ROPE // 2)
    second = (lane >= QK_NOPE + QK_ROPE // 2) & (lane < QK_NOPE + QK_ROPE)
    cf = jnp.where(lane < QK_NOPE, 1.0, jnp.where(first | second, cos, 0.0))
    return cf, jnp.where(first, -sin, 0.0), jnp.where(second, sin, 0.0)


def _arrange_w_in(w):
    k = w.shape[0]
    zeros = lambda n: jnp.zeros((k, n), w.dtype)
    kr0 = Q_LORA + KV_LORA
    pool0 = kr0 + QK_ROPE
    return jnp.concatenate([w[:, :kr0], w[:, pool0:], zeros(QK_NOPE), w[:, kr0:pool0],
                            zeros(HEAD_PAD - QK_NOPE - QK_ROPE)], axis=1)


def _restore_w_in(d):
    kr = d[:, IN_KR0 + QK_NOPE:IN_KR0 + QK_NOPE + QK_ROPE]
    return jnp.concatenate([d[:, :IN_POOL0], kr, d[:, IN_POOL0:IN_KR0]], axis=1)


def _pad_heads(w, width):
    k = w.shape[0]
    w = w.reshape(k, N_HEADS, width)
    return jnp.pad(w, ((0, 0), (0, 0), (0, HEAD_PAD - width))).reshape(k, N_HEADS * HEAD_PAD)


def _unpad_heads(d, width):
    k = d.shape[0]
    return d.reshape(k, N_HEADS, HEAD_PAD)[:, :, :width]


def kernel(x, p, positions, g_pre_mix, w_in, b_gate, g_q, w_uq, g_kv, w_ukv, w_pool, pool_scale, w_branch_attn, w_branch_pool, w_out, g_post_mix, g_pre_mlp, w_ff1, w_ff2, g_post_mlp, w_ple_proj, w_ple_gate, g_ple, loss_target, m_g_pre_mix, m_w_in, m_b_gate, m_g_q, m_w_uq, m_g_kv, m_w_ukv, m_w_pool, m_pool_scale, m_w_branch_attn, m_w_branch_pool, m_w_out, m_g_post_mix, m_g_pre_mlp, m_w_ff1, m_w_ff2, m_g_post_mlp, m_w_ple_proj, m_w_ple_gate, m_g_ple, v_g_pre_mix, v_w_in, v_b_gate, v_g_q, v_w_uq, v_g_kv, v_w_ukv, v_w_pool, v_pool_scale, v_w_branch_attn, v_w_branch_pool, v_w_out, v_g_post_mix, v_g_pre_mlp, v_w_ff1, v_w_ff2, v_g_post_mlp, v_w_ple_proj, v_w_ple_gate, v_g_ple):
    given = dict(locals())
    weights = {n: given[n] for n in WEIGHT_ORDER}
    moments_m = {n: given["m_" + n] for n in WEIGHT_ORDER}
    moments_v = {n: given["v_" + n] for n in WEIGHT_ORDER}
    xs, ps, target = x[0], p[0, 0], loss_target[0]
    seq = xs.shape[0]
    tm = min(256, seq)
    tm_merge = min(512, seq)
    core = lax.axis_index("c")
    chip = 2 * lax.axis_index("x") + lax.axis_index("y")

    early, later = SHARDED[:N_EARLY], SHARDED[N_EARLY:]
    shapes_of = lambda entries: [weights[n].shape[1:] for n, _ in entries]
    pack_bf16 = lambda entries: _pack_rows([weights[n][0].astype(BF16) for n, _ in entries])
    full = _full_from_gathered(_all_gather_hbm(pack_bf16(early)), early, shapes_of(early))
    w_in_r = _arrange_w_in(full["w_in"])
    w_uq_r = _pad_heads(full["w_uq"], QK_NOPE + QK_ROPE)
    ukv = full["w_ukv"].reshape(KV_LORA, N_HEADS, QK_NOPE + V_HEAD)
    w_k_exp = _pad_heads(ukv[:, :, :QK_NOPE].reshape(KV_LORA, N_HEADS * QK_NOPE), QK_NOPE)
    w_v = _pad_heads(ukv[:, :, QK_NOPE:].reshape(KV_LORA, N_HEADS * V_HEAD), V_HEAD)
    w_pool_bf = w_pool[0].astype(BF16)

    a_bf, qd, kvd, pin, gates, q, k, v, k_t, cf, s1, s2, gathered_later = _proj_fwd(
        xs, g_pre_mix, b_gate, g_q, g_kv, positions, w_in_r, w_uq_r, w_k_exp, w_v, pack_bf16(later), tm_merge)
    full.update(_full_from_gathered(gathered_later, later, shapes_of(later)))
    w_ba = jnp.pad(full["w_branch_attn"].reshape(N_HEADS, V_HEAD, D_MODEL),
                   ((0, 0), (0, HEAD_PAD - V_HEAD), (0, 0))).reshape(N_HEADS * HEAD_PAD, D_MODEL)
    d_pool, pooled = _pool_fwd(pin, w_pool_bf, pool_scale, tm)
    o_heads, lse = _attn_fwd(q, k, v)
    merged, ba, bb, y, h1, attn_rows = _merge_fwd(o_heads, pooled, gates, xs, g_post_mix, w_ba,
                                                  full["w_branch_pool"], full["w_out"], tm_merge)
    (m_bf, zr, f, h2_bf, p_bf, de, dpre, dh2, loss_acc, dg_ple) = _tail_fwd(
        h1, target, ps, g_pre_mlp, g_post_mlp, g_ple, full["w_ff1"], full["w_ff2"], full["w_ple_proj"],
        full["w_ple_gate"], tm)

    by_device, payload = {}, {}

    def early(name, pair, layout=lambda g: g):
        by_device[name], payload[name] = layout(pair[0]), layout(pair[1])

    early("w_ple_proj", _grad_w(p_bf, de, "grad_w_ple_proj", by_device=True, with_bf16=True))
    early("w_ple_gate", _grad_w(h2_bf, dpre, "grad_w_ple_gate", with_bf16=True), _rows_by_device)
    df, dz, dh1, dg_pre_mlp, dg_post_mlp = _mlp_bwd(h1, f, zr, dh2, g_pre_mlp, g_post_mlp, full["w_ff1"],
                                                    full["w_ff2"], tm)
    early("w_ff1", _grad_w(m_bf, dz, "grad_w_ff1", by_device=True, with_bf16=True))
    early("w_ff2", _grad_w(zr, df, "grad_w_ff2", square_a=True, with_bf16=True), _rows_by_device)
    (dy, dba, dbb, dgpre, do_heads, dd, dg_post_mix, db_gate, dpool_scale, dw_pool) = _merge_bwd(
        dh1, y, gates, ba, bb, d_pool, g_post_mix, pool_scale, full["w_out"], w_ba,
        full["w_branch_pool"], w_pool_bf, tm_merge)
    early("w_branch_attn", _grad_w(attn_rows, dba, "grad_w_branch_attn", with_bf16=True),
          lambda g: _columns_by_device(g.reshape(N_HEADS, HEAD_PAD, D_MODEL)[:, :V_HEAD].reshape(-1, D_MODEL)))
    early("w_branch_pool", _grad_w(pooled, dbb, "grad_w_branch_pool", by_device=True, with_bf16=True))
    early("w_out", _grad_w(merged, dy, "grad_w_out", with_bf16=True), _rows_by_device)
    dpin = _pool_bwd_window(dd, tm)
    delta = _attn_delta(o_heads, do_heads)
    direct = [n for n, _ in SHARDED[N_EARLY:]]
    outs = _attn_bwd(q, k, k_t, v, do_heads, lse, delta, [payload[n] for n in direct])
    dq, dk, dv = outs[:3]
    received = dict(zip(direct, outs[3:]))
    (grad_x, dproj, dq_bf, qn_bf, kvn_bf, dk_bf, dv_bf, dg_pre_mix, dg_q, dg_kv) = _proj_bwd(
        dq, dk, dv, qd, kvd, xs, dh1, dgpre, dpin, cf, s1, s2, g_pre_mix, g_q, g_kv, w_in_r, w_uq_r, w_k_exp, w_v,
        tm_merge)
    d_k_exp = _unpad_heads(_grad_w(kvn_bf, dk_bf, "grad_w_uk"), QK_NOPE)
    d_w_v = _unpad_heads(_grad_w(kvn_bf, dv_bf, "grad_w_uv"), V_HEAD)
    by_device["w_in"] = _columns_by_device(_restore_w_in(_grad_w(a_bf, dproj, "grad_w_in")))
    by_device["w_uq"] = _columns_by_device(
        _unpad_heads(_grad_w(qn_bf, dq_bf, "grad_w_uq"), QK_NOPE + QK_ROPE).reshape(Q_LORA, -1))
    by_device["w_ukv"] = _columns_by_device(jnp.concatenate([d_k_exp, d_w_v], axis=2).reshape(KV_LORA, -1))
    grads_small = {
        "g_pre_mix": dg_pre_mix, "b_gate": db_gate, "g_q": dg_q, "g_kv": dg_kv,
        "w_pool": dw_pool, "pool_scale": dpool_scale, "g_post_mix": dg_post_mix,
        "g_pre_mlp": dg_pre_mlp, "g_post_mlp": dg_post_mlp, "g_ple": dg_ple,
    }

    names = [n for n, _ in SHARDED]
    place = jnp.stack([2 * chip + core, chip, core]).astype(jnp.int32)
    sharded = {n: _adamw_direct(by_device[n], received[n], place, weights[n], moments_m[n], moments_v[n],
                                "adamw_" + n) for n in direct}
    last = [n for n, _ in SHARDED[:N_EARLY]]
    own = [by_device[n] for n in last]
    from_sibling = _exchange_pair(own)
    pair = [_pair_sum(g, r, place, "pair_sum_" + n) for n, g, r in zip(last, own, from_sibling)]
    from_chips = _exchange_chips(pair)
    sharded.update({n: _adamw_sharded(g, r, rc, place, weights[n], moments_m[n], moments_v[n], "adamw_" + n)
                    for n, g, r, rc in zip(last, own, from_sibling, from_chips)})

    g_sm = _all_reduce_small(_pack_small([grads_small[n] for n in REPLICATED] + [loss_acc[0:1, 0:1]]))
    n_small = sum(weights[n].size for n in REPLICATED)
    d_sm, m_sm, v_sm = _adamw_small(g_sm, _pack_small([weights[n] for n in REPLICATED]),
                                    _pack_small([moments_m[n] for n in REPLICATED]),
                                    _pack_small([moments_v[n] for n in REPLICATED]))

    small_shapes = [weights[n].shape for n in REPLICATED]
    results = []
    for which, small in enumerate((g_sm, d_sm, m_sm, v_sm)):
        named = {n: sharded[n][which] for n in names}
        named.update(zip(REPLICATED, _unpack_small(small, small_shapes)))
        results.append([named[n] for n in WEIGHT_ORDER])

    loss = g_sm.reshape(-1)[n_small]
    return (loss, grad_x[None], *results[0], *results[1], *results[2], *results[3])
```

```python
import jax
import jax.numpy as jnp
from jax import lax
from jax.experimental import pallas as pl
from jax.experimental.pallas import tpu as pltpu

F32 = jnp.float32
BF16 = jnp.bfloat16

D_MODEL = 1024
PLE_DIM = 256
N_HEADS = 8
QK_NOPE = 64
QK_ROPE = 32
V_HEAD = 64
Q_LORA = 384
KV_LORA = 256
POOL_WINDOWS = (2, 4, 8, 16)
POOL_GROUP = 128
POOL_WIDTH = 512
D_FF = 4096
ROPE_THETA = 10000.0
EPS = 1e-6
HEAD_PAD = 128
ATTN_SCALE = (QK_NOPE + QK_ROPE) ** -0.5
LOG2E = 1.4426950408889634
Q_PRESCALE = ATTN_SCALE * LOG2E
ATTN_TILE = 512
FWD_ROWS = 512
FWD_CHAINS = 4

ADAM_LR = 0.001
ADAM_B1 = 0.9
ADAM_B2 = 0.999
ADAM_EPS = 1e-08
ADAM_WD = 0.01
ADAM_STEP = 10

N_DEV = 8
LANES = 1024
PACK_ROW_TILE = 480
POOL_HALO = 16
MIB = 2 ** 20

IN_Q0, IN_KV0, IN_POOL0, IN_GATE0, IN_KR0, IN_R = 0, 384, 640, 1152, 3200, 3328

SHARDED = (("w_in", "col"), ("w_uq", "col"), ("w_ukv", "col"), ("w_branch_attn", "col"),
           ("w_branch_pool", "col"), ("w_out", "row"), ("w_ff1", "col"), ("w_ff2", "row"),
           ("w_ple_proj", "col"), ("w_ple_gate", "row"))
N_EARLY = 3
REPLICATED = ("g_pre_mix", "b_gate", "g_q", "g_kv", "w_pool", "pool_scale", "g_post_mix",
              "g_pre_mlp", "g_post_mlp", "g_ple")
WEIGHT_ORDER = ("g_pre_mix", "w_in", "b_gate", "g_q", "w_uq", "g_kv", "w_ukv", "w_pool", "pool_scale",
                "w_branch_attn", "w_branch_pool", "w_out", "g_post_mix", "g_pre_mlp", "w_ff1", "w_ff2",
                "g_post_mlp", "w_ple_proj", "w_ple_gate", "g_ple")

NT = (((1,), (1,)), ((), ()))
TN = (((0,), (0,)), ((), ()))
MESH = pl.DeviceIdType.MESH
ANY = pl.BlockSpec(memory_space=pl.ANY)


def _params(n_axes, vmem_mib):
    return pltpu.CompilerParams(dimension_semantics=("arbitrary",) * n_axes, vmem_limit_bytes=vmem_mib * MIB)


def _row(tm, n):
    return pl.BlockSpec((tm, n), lambda i: (i, 0))


def _fix(shape):
    zeros = (0,) * len(shape)
    return pl.BlockSpec(shape, lambda i: zeros)


def _sds(shape, dtype):
    return jax.ShapeDtypeStruct(shape, dtype)


def _rms_r(v):
    return lax.rsqrt(jnp.mean(v * v, axis=-1, keepdims=True) + EPS)


def _rms_bwd(vhat, r, g, dy):
    gdy = dy * g
    return r * (gdy - vhat * jnp.mean(gdy * vhat, axis=-1, keepdims=True))


def _colsum(v):
    return jnp.sum(v, axis=0, keepdims=True)


def _sigmoid(v):
    return 1.0 / (1.0 + jnp.exp(-v))


def _mm(a, b):
    return jnp.dot(a, b, preferred_element_type=F32)


def _mm_nt(a, b):
    return lax.dot_general(a, b, NT, preferred_element_type=F32)


def _rope(c, cf, s1, s2):
    return c * cf + pltpu.roll(c, HEAD_PAD - 16, 1) * s1 + pltpu.roll(c, 16, 1) * s2


def _rope_t(c, cf, s1, s2):
    return c * cf + pltpu.roll(c * s1, 16, 1) + pltpu.roll(c * s2, HEAD_PAD - 16, 1)


def _row_chains(tm, rows=256):
    rows = min(rows, tm)
    return [slice(c * rows, (c + 1) * rows) for c in range(tm // rows)]


def _load_once(pairs):
    @pl.when(pl.program_id(0) == 0)
    def _():
        for src, dst in pairs:
            pltpu.sync_copy(src, dst)


def _proj_fwd(x, g_pre, b_gate, g_q, g_kv, positions, w_in_r, w_uq_r, w_k_exp, w_v, later_shards, tm):
    seq = x.shape[0]
    n_steps = seq // tm
    forward_step = (3 * n_steps) // 4

    def body(x_ref, gpre_ref, bg_ref, gq_ref, gkv_ref, pos_ref, freq_ref, win_hbm, wuq_hbm, wk_hbm, wv_hbm,
             later_ref, a_ref, qd_ref, kvd_ref, pin_ref, gates_ref, q_ref, k_ref, v_ref, kt_ref,
             cf_ref, s1_ref, s2_ref, gathered_ref, win, wuq, wk, wv, send_sems, recv_sems, local_sem):
        step = pl.program_id(0)

        def gather(phase):
            _gather_copies(later_ref, lambda px, py, pc: gathered_ref.at[4 * px + 2 * py + pc],
                           send_sems, recv_sems, local_sem, phases=(phase,))

        pl.when(step == 0)(lambda: gather("send"))
        pl.when(step == forward_step)(lambda: gather("forward"))
        _load_once(((win_hbm, win), (wuq_hbm, wuq), (wk_hbm, wk), (wv_hbm, wv)))
        for rows in _row_chains(tm):
            n_rows = rows.stop - rows.start
            xv = x_ref[rows, :]
            a = (xv * _rms_r(xv) * gpre_ref[...]).astype(BF16)
            a_ref[rows, :] = a
            proj = _mm(a, win[...])
            qd = proj[:, IN_Q0:IN_KV0]
            kvd = proj[:, IN_KV0:IN_POOL0]
            qd_ref[rows, :] = qd
            kvd_ref[rows, :] = kvd
            pin_ref[rows, :] = proj[:, IN_POOL0:IN_GATE0]
            gates_ref[rows, :] = _sigmoid(proj[:, IN_GATE0:IN_KR0] + bg_ref[...]).astype(BF16)
            cfv, s1v, s2v = _rope_tables(pos_ref[:, rows], freq_ref[...], n_rows)
            cf_ref[rows, :], s1_ref[rows, :], s2_ref[rows, :] = cfv, s1v, s2v
            krr = _rope(proj[:, IN_KR0:IN_R], cfv, s1v, s2v)
            qn = (qd * _rms_r(qd) * gq_ref[...]).astype(BF16)
            qf = _mm(qn, wuq[...])
            kvn = (kvd * _rms_r(kvd) * gkv_ref[...]).astype(BF16)
            kf = _mm(kvn, wk[...])
            vf = _mm(kvn, wv[...])
            one_lane = (lax.broadcasted_iota(jnp.int32, (n_rows, HEAD_PAD), 1) == V_HEAD).astype(F32)
            for h in range(N_HEADS):
                lanes = slice(HEAD_PAD * h, HEAD_PAD * (h + 1))
                q_ref[h, rows, :] = (_rope(qf[:, lanes], cfv, s1v, s2v) * Q_PRESCALE).astype(BF16)
                kh = kf[:, lanes] + krr
                vh = vf[:, lanes] + one_lane
                k_ref[h, rows, :] = kh.astype(BF16)
                v_ref[h, rows, :] = vh.astype(BF16)
                kt_ref[h, :, rows] = jnp.transpose(kh).astype(BF16)
        pl.when(step == n_steps - 1)(lambda: gather("finish"))

    per_tile = ATTN_TILE // tm
    heads = pl.BlockSpec((N_HEADS, tm, HEAD_PAD), lambda i: (0, i, 0))
    heads_t = pl.BlockSpec((N_HEADS, None, HEAD_PAD, tm), lambda i: (0, i // per_tile, 0, i % per_tile))
    heads_t_shape = _sds((N_HEADS, seq // ATTN_TILE, HEAD_PAD, ATTN_TILE), BF16)
    return pl.pallas_call(
        body, name="proj_fwd", grid=(seq // tm,),
        in_specs=[_row(tm, D_MODEL), _fix((1, D_MODEL)), _fix((1, 2 * D_MODEL)), _fix((1, Q_LORA)), _fix((1, KV_LORA)),
                  pl.BlockSpec((1, tm), lambda i: (0, i)), _fix((1, HEAD_PAD)), ANY, ANY, ANY, ANY, ANY],
        out_specs=[_row(tm, D_MODEL), _row(tm, Q_LORA), _row(tm, KV_LORA), _row(tm, POOL_WIDTH), _row(tm, 2 * D_MODEL),
                   heads, heads, heads, heads_t, _row(tm, HEAD_PAD), _row(tm, HEAD_PAD), _row(tm, HEAD_PAD), ANY],
        out_shape=[_sds((seq, D_MODEL), BF16), _sds((seq, Q_LORA), F32), _sds((seq, KV_LORA), F32),
                   _sds((seq, POOL_WIDTH), F32), _sds((seq, 2 * D_MODEL), BF16),
                   _sds((N_HEADS, seq, HEAD_PAD), BF16), _sds((N_HEADS, seq, HEAD_PAD), BF16),
                   _sds((N_HEADS, seq, HEAD_PAD), BF16), heads_t_shape,
                   _sds((seq, HEAD_PAD), F32), _sds((seq, HEAD_PAD), F32), _sds((seq, HEAD_PAD), F32),
                   _sds((N_DEV,) + later_shards.shape, later_shards.dtype)],
        scratch_shapes=[pltpu.VMEM(w_in_r.shape, BF16), pltpu.VMEM(w_uq_r.shape, BF16),
                        pltpu.VMEM(w_k_exp.shape, BF16), pltpu.VMEM(w_v.shape, BF16),
                        pltpu.SemaphoreType.DMA((7,)), pltpu.SemaphoreType.DMA((7,)), pltpu.SemaphoreType.DMA],
        compiler_params=_params(1, 48),
    )(x, g_pre, b_gate, g_q, g_kv, positions, _rope_lane_frequencies(), w_in_r, w_uq_r, w_k_exp, w_v, later_shards)


def _window_count(row0, n_rows):
    t = row0 + lax.broadcasted_iota(jnp.int32, (n_rows, POOL_GROUP), 0)
    return [jnp.minimum(t + 1, w).astype(F32) for w in POOL_WINDOWS]


def _pool_fwd(pin, w_pool_bf, pool_scale, tm):
    seq = pin.shape[0]
    ext_rows = tm + POOL_HALO

    def body(prev_ref, u_ref, wp_ref, ps_ref, d_ref, pooled_ref):
        i = pl.program_id(0)
        prev = jnp.where(i == 0, 0.0, prev_ref[...])
        u = u_ref[...]
        level = jnp.concatenate([prev, u], axis=0)
        counts = _window_count(i * tm, tm)
        shift = 1
        for g in range(len(POOL_WINDOWS)):
            level = level + pltpu.roll(level, shift, 0)
            shift *= 2
            lanes = slice(POOL_GROUP * g, POOL_GROUP * (g + 1))
            d = (level[POOL_HALO:, lanes] / counts[g] - u[:, lanes]).astype(BF16)
            d_ref[:, lanes] = d
            pooled_ref[:, lanes] = (_mm(d, wp_ref[g]) * ps_ref[:, lanes]).astype(BF16)

    halo = tm // POOL_HALO
    return pl.pallas_call(
        body, name="pool_fwd", grid=(seq // tm,),
        in_specs=[pl.BlockSpec((POOL_HALO, POOL_WIDTH), lambda i: (jnp.maximum(i * halo - 1, 0), 0)),
                  _row(tm, POOL_WIDTH), _fix(w_pool_bf.shape), _fix((1, POOL_WIDTH))],
        out_specs=[_row(tm, POOL_WIDTH), _row(tm, POOL_WIDTH)],
        out_shape=[_sds((seq, POOL_WIDTH), BF16), _sds((seq, POOL_WIDTH), BF16)],
        compiler_params=_params(1, 32),
    )(pin, pin, w_pool_bf, pool_scale)


def _pool_bwd_window(dd, tm):
    seq = dd.shape[0]
    n_tiles = seq // tm
    ext_rows = tm + POOL_HALO

    def body(dd_ref, next_ref, dpin_ref):
        i = pl.program_id(0)
        nxt = jnp.where(i == n_tiles - 1, 0.0, next_ref[...])
        dd_t = dd_ref[...]
        ext = jnp.concatenate([dd_t, nxt], axis=0)
        counts = _window_count(i * tm, ext_rows)
        shift = 1
        for g in range(len(POOL_WINDOWS)):
            lanes = slice(POOL_GROUP * g, POOL_GROUP * (g + 1))
            level = ext[:, lanes] / counts[g]
            s = 1
            while s <= shift:
                level = level + pltpu.roll(level, ext_rows - s, 0)
                s *= 2
            shift *= 2
            dpin_ref[:, lanes] = (level[:tm] - dd_t[:, lanes]).astype(BF16)

    halo = tm // POOL_HALO
    return pl.pallas_call(
        body, name="pool_bwd_window", grid=(n_tiles,),
        in_specs=[_row(tm, POOL_WIDTH),
                  pl.BlockSpec((POOL_HALO, POOL_WIDTH), lambda i: (jnp.minimum((i + 1) * halo, seq // POOL_HALO - 1), 0))],
        out_specs=_row(tm, POOL_WIDTH),
        out_shape=_sds((seq, POOL_WIDTH), BF16),
        compiler_params=_params(1, 32),
    )(dd, dd)


def _col_to_row(col, n):
    return jnp.transpose(jnp.broadcast_to(col, (n, HEAD_PAD)))[0:1, :]


def _attn_fwd(q, k, v):
    heads, seq, _ = q.shape
    r, n = FWD_ROWS, FWD_CHAINS
    block = r * n

    def body(q_ref, k_ref, v_ref, o_ref, lse_ref):
        qi = pl.program_id(1)
        q_tiles = [q_ref[c * r:(c + 1) * r, :] for c in range(n)]

        def tile(qt, j, m, acc, diagonal):
            start = pl.multiple_of(j * r, r)
            s = _mm_nt(qt, k_ref[pl.ds(start, r), :])
            if diagonal:
                row = lax.broadcasted_iota(jnp.int32, (r, r), 0)
                col = lax.broadcasted_iota(jnp.int32, (r, r), 1)
                s = jnp.where(col <= row, s, -jnp.inf)
            m_new = jnp.maximum(m, jnp.max(s, axis=1, keepdims=True))
            p = jnp.exp2((s - m_new).astype(BF16))
            acc = jnp.exp2(m - m_new) * acc + _mm(p, v_ref[pl.ds(start, r), :])
            return m_new, acc

        def all_chains(jj, carry):
            for u in range(n):
                carry = tuple(tile(q_tiles[c], n * jj + u, *carry[c], False) for c in range(n))
            return carry

        init = tuple((jnp.full((r, 1), -jnp.inf, F32), jnp.zeros((r, HEAD_PAD), F32)) for _ in range(n))
        state = list(lax.fori_loop(0, qi, all_chains, init))
        for d in range(n):
            for c in range(d, n):
                state[c] = tile(q_tiles[c], n * qi + d, *state[c], c == d)
        for c, (m, acc) in enumerate(state):
            l = acc[:, V_HEAD:V_HEAD + 1]
            o_ref[c * r:(c + 1) * r, :] = (acc / l).astype(BF16)
            row0 = c * r
            lse_ref[row0 // ATTN_TILE, :, row0 % ATTN_TILE:row0 % ATTN_TILE + r] = _col_to_row(m + jnp.log2(l), r)

    return pl.pallas_call(
        body, name="attn_fwd", grid=(heads, seq // block),
        in_specs=[pl.BlockSpec((None, block, HEAD_PAD), lambda h, i: (h, i, 0)),
                  pl.BlockSpec((None, seq, HEAD_PAD), lambda h, i: (h, 0, 0)),
                  pl.BlockSpec((None, seq, HEAD_PAD), lambda h, i: (h, 0, 0))],
        out_specs=[pl.BlockSpec((None, block, HEAD_PAD), lambda h, i: (h, i, 0)),
                   pl.BlockSpec((None, block // ATTN_TILE, 1, ATTN_TILE), lambda h, i: (h, i, 0, 0))],
        out_shape=[_sds((heads, seq, HEAD_PAD), BF16), _sds((heads, seq // ATTN_TILE, 1, ATTN_TILE), F32)],
        compiler_params=_params(2, 48),
    )(q, k, v)


def _attn_delta(o, do):
    heads, seq, _ = o.shape
    tq = ATTN_TILE
    nq = seq // tq

    per_step = min(8, nq)

    def body(o_ref, do_ref, delta_ref):
        for u in range(per_step):
            rows = slice(u * tq, (u + 1) * tq)
            prod = o_ref[rows, :].astype(F32) * do_ref[rows, :].astype(F32)
            delta_ref[u] = _col_to_row(jnp.sum(prod, axis=1, keepdims=True), tq)

    tile = pl.BlockSpec((None, per_step * tq, HEAD_PAD), lambda h, i: (h, i, 0))
    return pl.pallas_call(
        body, name="attn_delta", grid=(heads, nq // per_step),
        in_specs=[tile, tile],
        out_specs=pl.BlockSpec((None, per_step, 1, tq), lambda h, i: (h, i, 0, 0)),
        out_shape=_sds((heads, nq, 1, tq), F32),
        compiler_params=_params(2, 32),
    )(o, do)


def _peer_copies(src_refs, dst_refs, send_sems, recv_sems):
    x, y, c = _position()
    copies = []
    for w, (src, dst) in enumerate(zip(src_refs, dst_refs)):
        for r in range(1, N_DEV):
            px = 1 - x if r & 4 else x
            py = 1 - y if r & 2 else y
            pc = 1 - c if r & 1 else c
            copies.append(pltpu.make_async_remote_copy(
                src_ref=src.at[4 * px + 2 * py + pc], dst_ref=dst.at[r - 1],
                send_sem=send_sems.at[(N_DEV - 1) * w + r - 1], recv_sem=recv_sems.at[(N_DEV - 1) * w + r - 1],
                device_id=(px, py, pc), device_id_type=MESH))
    return copies


def _attn_bwd(q, k, k_t, v, do, lse, delta, early_grads):
    heads, seq, _ = q.shape
    t = ATTN_TILE
    nq = seq // t
    n_w = len(early_grads)

    def body(q_ref, k_ref, kt_ref, v_ref, do_ref, lse_ref, delta_ref, *rest):
        grad_refs, rest = rest[:n_w], rest[n_w:]
        dq_ref, dk_ref, dv_ref = rest[:3]
        recv_refs, (send_sems, recv_sems) = rest[3:3 + n_w], rest[3 + n_w:]
        jp = pl.program_id(1)
        head = pl.program_id(0)

        @pl.when((head == 0) & (jp == 0))
        def _():
            for cp in _peer_copies(grad_refs, recv_refs, send_sems, recv_sems):
                cp.start()

        @pl.when(jp == 0)
        def _():
            dq_ref[...] = jnp.zeros_like(dq_ref)

        k_a, k_b = k_ref[0:t, :], k_ref[t:2 * t, :]
        v_a, v_b = v_ref[0:t, :], v_ref[t:2 * t, :]

        def tile(kt, k_tr, vt, i, dk, dv, diagonal):
            start = pl.multiple_of(i * t, t)
            qt = q_ref[pl.ds(start, t), :]
            dot = do_ref[pl.ds(start, t), :]
            p_t = jnp.exp2(_mm_nt(kt, qt) - lse_ref[i])
            if diagonal:
                key = lax.broadcasted_iota(jnp.int32, (t, t), 0)
                query = lax.broadcasted_iota(jnp.int32, (t, t), 1)
                p_t = jnp.where(key <= query, p_t, 0.0)
            dv = dv + _mm(p_t.astype(BF16), dot)
            ds_t = (p_t * (_mm_nt(vt, dot) - delta_ref[i])).astype(BF16)
            dk = dk + _mm(ds_t, qt)
            return dk, dv, _mm(k_tr, ds_t)

        def add_dq(i, dq):
            dq_ref[i] += dq

        def both(ip, carry):
            dk_a, dv_a, dk_b, dv_b = carry
            for i in (2 * ip, 2 * ip + 1):
                dk_a, dv_a, dq_a = tile(k_a, kt_ref[0], v_a, i, dk_a, dv_a, False)
                dk_b, dv_b, dq_b = tile(k_b, kt_ref[1], v_b, i, dk_b, dv_b, False)
                add_dq(i, dq_a + dq_b)
            return dk_a, dv_a, dk_b, dv_b

        zero = jnp.zeros((t, HEAD_PAD), F32)
        dk_a, dv_a, dq_a = tile(k_a, kt_ref[0], v_a, 2 * jp, zero, zero, True)
        add_dq(2 * jp, dq_a)
        dk_a, dv_a, dq_a = tile(k_a, kt_ref[0], v_a, 2 * jp + 1, dk_a, dv_a, False)
        dk_b, dv_b, dq_b = tile(k_b, kt_ref[1], v_b, 2 * jp + 1, zero, zero, True)
        add_dq(2 * jp + 1, dq_a + dq_b)
        dk_a, dv_a, dk_b, dv_b = lax.fori_loop(jp + 1, nq // 2, both, (dk_a, dv_a, dk_b, dv_b))
        dk_ref[0:t, :] = (dk_a * (1.0 / LOG2E)).astype(BF16)
        dk_ref[t:2 * t, :] = (dk_b * (1.0 / LOG2E)).astype(BF16)
        dv_ref[0:t, :] = dv_a.astype(BF16)
        dv_ref[t:2 * t, :] = dv_b.astype(BF16)

        @pl.when((head == heads - 1) & (jp == nq // 2 - 1))
        def _():
            copies = _peer_copies(grad_refs, recv_refs, send_sems, recv_sems)
            for cp in copies:
                cp.wait_recv()
            for cp in copies:
                cp.wait_send()

    whole = pl.BlockSpec((None, seq, HEAD_PAD), lambda h, j: (h, 0, 0))
    whole_t = pl.BlockSpec((None, nq, HEAD_PAD, t), lambda h, j: (h, 0, 0, 0))
    pair = pl.BlockSpec((None, 2 * t, HEAD_PAD), lambda h, j: (h, j, 0))
    pair_t = pl.BlockSpec((None, 2, HEAD_PAD, t), lambda h, j: (h, j, 0, 0))
    stats = pl.BlockSpec((None, nq, 1, t), lambda h, j: (h, 0, 0, 0))
    return pl.pallas_call(
        body, name="attn_bwd", grid=(heads, nq // 2),
        in_specs=[whole, pair, pair_t, pair, whole, stats, stats] + [ANY] * n_w,
        out_specs=[whole_t, pair, pair] + [ANY] * n_w,
        out_shape=[_sds((heads, nq, HEAD_PAD, t), F32), _sds((heads, seq, HEAD_PAD), BF16),
                   _sds((heads, seq, HEAD_PAD), BF16)]
                  + [_sds((N_DEV - 1,) + g.shape[1:], g.dtype) for g in early_grads],
        scratch_shapes=[pltpu.SemaphoreType.DMA(((N_DEV - 1) * n_w,)), pltpu.SemaphoreType.DMA(((N_DEV - 1) * n_w,))],
        compiler_params=_params(2, 56),
    )(q, k, k_t, v, do, lse, delta, *early_grads)


def _merge_fwd(attn, pooled, gates, x, g_post_mix, w_ba, w_bb, w_out, tm):
    seq = x.shape[0]

    def body(attn_ref, pooled_ref, gates_ref, x_ref, g_ref, wba_ref, wbb_ref, wout_ref,
             merged_ref, ba_ref, bb_ref, y_ref, h1_ref, attn_rows_ref):
        for rows in _row_chains(tm):
            attn = jnp.concatenate([attn_ref[h, rows, :] for h in range(N_HEADS)], axis=1)
            attn_rows_ref[rows, :] = attn
            ba = _mm(attn, wba_ref[...])
            bb = _mm(pooled_ref[rows, :], wbb_ref[...])
            ba_ref[rows, :] = ba.astype(BF16)
            bb_ref[rows, :] = bb.astype(BF16)
            merged = (gates_ref[rows, :D_MODEL].astype(F32) * ba
                      + gates_ref[rows, D_MODEL:].astype(F32) * bb).astype(BF16)
            merged_ref[rows, :] = merged
            y = _mm(merged, wout_ref[...])
            y_ref[rows, :] = y
            h1_ref[rows, :] = x_ref[rows, :] + y * _rms_r(y) * g_ref[...]

    return pl.pallas_call(
        body, name="merge_fwd", grid=(seq // tm,),
        in_specs=[pl.BlockSpec((N_HEADS, tm, HEAD_PAD), lambda i: (0, i, 0)), _row(tm, POOL_WIDTH),
                  _row(tm, 2 * D_MODEL), _row(tm, D_MODEL),
                  _fix((1, D_MODEL)), _fix(w_ba.shape), _fix(w_bb.shape), _fix(w_out.shape)],
        out_specs=[_row(tm, D_MODEL)] * 5 + [_row(tm, N_HEADS * HEAD_PAD)],
        out_shape=[_sds((seq, D_MODEL), BF16), _sds((seq, D_MODEL), BF16), _sds((seq, D_MODEL), BF16),
                   _sds((seq, D_MODEL), F32), _sds((seq, D_MODEL), F32), _sds((seq, N_HEADS * HEAD_PAD), BF16)],
        compiler_params=_params(1, 48),
    )(attn, pooled, gates, x, g_post_mix, w_ba, w_bb, w_out)


def _tail_fwd(h1, target, p, g_pre_mlp, g_post_mlp, g_ple, w_ff1, w_ff2, w_pe, w_pg, tm):
    seq = h1.shape[0]

    def body(h1_ref, tgt_ref, p_ref, gm_ref, gf_ref, gp_ref, w1_hbm, w2_hbm, wpe_hbm, wpg_hbm,
             m_ref, zr_ref, f_ref, h2b_ref, pb_ref, de_ref, dpre_ref, dh2_ref, loss_ref, dgple_ref,
             w1, w2, wpe, wpg):
        _load_once(((w1_hbm, w1), (w2_hbm, w2), (wpe_hbm, wpe), (wpg_hbm, wpg)))

        @pl.when(pl.program_id(0) == 0)
        def _():
            loss_ref[...] = jnp.zeros_like(loss_ref)
            dgple_ref[...] = jnp.zeros_like(dgple_ref)

        h1v = h1_ref[...]
        m = (h1v * _rms_r(h1v) * gm_ref[...]).astype(BF16)
        m_ref[...] = m
        zr = jnp.maximum(_mm(m, w1[...]), 0.0)
        zr_ref[...] = zr.astype(BF16)
        f = _mm((zr * zr).astype(BF16), w2[...])
        f_ref[...] = f
        h2 = h1v + f * _rms_r(f) * gf_ref[...]
        h2b = h2.astype(BF16)
        h2b_ref[...] = h2b
        pb = p_ref[...].astype(BF16)
        pb_ref[...] = pb
        e = _mm(pb, wpe[...])
        pg = _sigmoid(_mm(h2b, wpg[...]))
        t3 = pg * e
        r3 = _rms_r(t3)
        t3hat = t3 * r3
        diff = h2 + t3hat * gp_ref[...] - tgt_ref[...]
        loss_ref[...] += jnp.sum(diff * diff) * (0.5 / D_MODEL)
        dh3 = diff * (1.0 / D_MODEL)
        dgple_ref[...] += _colsum(dh3 * t3hat)
        dt3 = _rms_bwd(t3hat, r3, gp_ref[...], dh3)
        de_ref[...] = (dt3 * pg).astype(BF16)
        dpre = (dt3 * e * pg * (1.0 - pg)).astype(BF16)
        dpre_ref[...] = dpre
        dh2_ref[...] = dh3 + _mm_nt(dpre, wpg[...])

    return pl.pallas_call(
        body, name="tail_fwd", grid=(seq // tm,),
        in_specs=[_row(tm, D_MODEL), _row(tm, D_MODEL), _row(tm, PLE_DIM), _fix((1, D_MODEL)), _fix((1, D_MODEL)),
                  _fix((1, D_MODEL)), ANY, ANY, ANY, ANY],
        out_specs=[_row(tm, D_MODEL), _row(tm, D_FF), _row(tm, D_MODEL), _row(tm, D_MODEL), _row(tm, PLE_DIM),
                   _row(tm, D_MODEL), _row(tm, D_MODEL), _row(tm, D_MODEL), _fix((8, 128)), _fix((1, D_MODEL))],
        out_shape=[_sds((seq, D_MODEL), BF16), _sds((seq, D_FF), BF16), _sds((seq, D_MODEL), F32),
                   _sds((seq, D_MODEL), BF16), _sds((seq, PLE_DIM), BF16), _sds((seq, D_MODEL), BF16),
                   _sds((seq, D_MODEL), BF16), _sds((seq, D_MODEL), F32), _sds((8, 128), F32), _sds((1, D_MODEL), F32)],
        scratch_shapes=[pltpu.VMEM(w_ff1.shape, BF16), pltpu.VMEM(w_ff2.shape, BF16),
                        pltpu.VMEM(w_pe.shape, BF16), pltpu.VMEM(w_pg.shape, BF16)],
        compiler_params=_params(1, 56),
    )(h1, target, p, g_pre_mlp, g_post_mlp, g_ple, w_ff1, w_ff2, w_pe, w_pg)


def _mlp_bwd(h1, f, zr, dh2, g_pre_mlp, g_post_mlp, w_ff1, w_ff2, tm):
    seq = h1.shape[0]

    def body(h1_ref, f_ref, zr_ref, dh2_ref, gm_ref, gf_ref, w1_hbm, w2_hbm,
             df_ref, dz_ref, dh1_ref, dgm_ref, dgf_ref, w1, w2):
        _load_once(((w1_hbm, w1), (w2_hbm, w2)))

        @pl.when(pl.program_id(0) == 0)
        def _():
            dgm_ref[...] = jnp.zeros_like(dgm_ref)
            dgf_ref[...] = jnp.zeros_like(dgf_ref)

        dh2 = dh2_ref[...]
        fv = f_ref[...]
        rf = _rms_r(fv)
        fhat = fv * rf
        dgf_ref[...] += _colsum(dh2 * fhat)
        df = _rms_bwd(fhat, rf, gf_ref[...], dh2).astype(BF16)
        df_ref[...] = df
        dz = (_mm_nt(df, w2[...]) * (2.0 * zr_ref[...].astype(F32))).astype(BF16)
        dz_ref[...] = dz
        dm = _mm_nt(dz, w1[...])
        h1v = h1_ref[...]
        r1 = _rms_r(h1v)
        h1hat = h1v * r1
        dgm_ref[...] += _colsum(dm * h1hat)
        dh1_ref[...] = dh2 + _rms_bwd(h1hat, r1, gm_ref[...], dm)

    return pl.pallas_call(
        body, name="mlp_bwd", grid=(seq // tm,),
        in_specs=[_row(tm, D_MODEL), _row(tm, D_MODEL), _row(tm, D_FF), _row(tm, D_MODEL),
                  _fix((1, D_MODEL)), _fix((1, D_MODEL)), ANY, ANY],
        out_specs=[_row(tm, D_MODEL), _row(tm, D_FF), _row(tm, D_MODEL), _fix((1, D_MODEL)), _fix((1, D_MODEL))],
        out_shape=[_sds((seq, D_MODEL), BF16), _sds((seq, D_FF), BF16), _sds((seq, D_MODEL), F32),
                   _sds((1, D_MODEL), F32), _sds((1, D_MODEL), F32)],
        scratch_shapes=[pltpu.VMEM(w_ff1.shape, BF16), pltpu.VMEM(w_ff2.shape, BF16)],
        compiler_params=_params(1, 56),
    )(h1, f, zr, dh2, g_pre_mlp, g_post_mlp, w_ff1, w_ff2)


def _merge_bwd(dh1, y, gates, ba, bb, d_pool, g_post_mix, pool_scale, w_out, w_ba, w_bb, w_pool_bf, tm):
    seq = dh1.shape[0]

    def body(dh1_ref, y_ref, gates_ref, ba_ref, bb_ref, d_ref, g_ref, ps_ref, wout_ref, wba_ref, wbb_ref, wp_ref,
             dy_ref, dba_ref, dbb_ref, dgpre_ref, dattn_ref, dd_ref, dg_ref, dbg_ref, dps_ref, dwp_ref):
        @pl.when(pl.program_id(0) == 0)
        def _():
            dg_ref[...] = jnp.zeros_like(dg_ref)
            dbg_ref[...] = jnp.zeros_like(dbg_ref)
            dps_ref[...] = jnp.zeros_like(dps_ref)
            dwp_ref[...] = jnp.zeros_like(dwp_ref)

        for rows in _row_chains(tm):
            dh1v = dh1_ref[rows, :]
            yv = y_ref[rows, :]
            r = _rms_r(yv)
            yhat = yv * r
            dg_ref[...] += _colsum(dh1v * yhat)
            dy = _rms_bwd(yhat, r, g_ref[...], dh1v).astype(BF16)
            dy_ref[rows, :] = dy
            dmerged = _mm_nt(dy, wout_ref[...])
            dbranch = []
            for half, branch_ref, dbranch_ref in ((0, ba_ref, dba_ref), (1, bb_ref, dbb_ref)):
                lanes = slice(D_MODEL * half, D_MODEL * (half + 1))
                gate = gates_ref[rows, lanes].astype(F32)
                dpre = dmerged * branch_ref[rows, :].astype(F32) * gate * (1.0 - gate)
                dbg_ref[:, lanes] += _colsum(dpre)
                dgpre_ref[rows, lanes] = dpre.astype(BF16)
                dbranch.append((dmerged * gate).astype(BF16))
                dbranch_ref[rows, :] = dbranch[-1]
            dattn = _mm_nt(dbranch[0], wba_ref[...]).astype(BF16)
            for h in range(N_HEADS):
                dattn_ref[h, rows, :] = dattn[:, HEAD_PAD * h:HEAD_PAD * (h + 1)]
            dpooled = _mm_nt(dbranch[1], wbb_ref[...])
            for g in range(len(POOL_WINDOWS)):
                lanes = slice(POOL_GROUP * g, POOL_GROUP * (g + 1))
                dpl = dpooled[:, lanes]
                d_g = d_ref[rows, lanes]
                dps_ref[:, lanes] += _colsum(dpl * _mm(d_g, wp_ref[g]))
                dyp = (dpl * ps_ref[:, lanes]).astype(BF16)
                dwp_ref[g] += lax.dot_general(d_g, dyp, TN, preferred_element_type=F32)
                dd_ref[rows, lanes] = _mm_nt(dyp, wp_ref[g])

    return pl.pallas_call(
        body, name="merge_bwd", grid=(seq // tm,),
        in_specs=[_row(tm, D_MODEL), _row(tm, D_MODEL), _row(tm, 2 * D_MODEL), _row(tm, D_MODEL), _row(tm, D_MODEL),
                  _row(tm, POOL_WIDTH), _fix((1, D_MODEL)), _fix((1, POOL_WIDTH)),
                  _fix(w_out.shape), _fix(w_ba.shape), _fix(w_bb.shape), _fix(w_pool_bf.shape)],
        out_specs=[_row(tm, D_MODEL), _row(tm, D_MODEL), _row(tm, D_MODEL), _row(tm, 2 * D_MODEL),
                   pl.BlockSpec((N_HEADS, tm, HEAD_PAD), lambda i: (0, i, 0)), _row(tm, POOL_WIDTH),
                   _fix((1, D_MODEL)), _fix((1, 2 * D_MODEL)), _fix((1, POOL_WIDTH)), _fix(w_pool_bf.shape)],
        out_shape=[_sds((seq, D_MODEL), BF16), _sds((seq, D_MODEL), BF16), _sds((seq, D_MODEL), BF16),
                   _sds((seq, 2 * D_MODEL), BF16), _sds((N_HEADS, seq, HEAD_PAD), BF16),
                   _sds((seq, POOL_WIDTH), F32), _sds((1, D_MODEL), F32), _sds((1, 2 * D_MODEL), F32),
                   _sds((1, POOL_WIDTH), F32), _sds(w_pool_bf.shape, F32)],
        compiler_params=_params(1, 48),
    )(dh1, y, gates, ba, bb, d_pool, g_post_mix, pool_scale, w_out, w_ba, w_bb, w_pool_bf)


def _proj_bwd(dq, dk, dv, qd, kvd, x, dh1, dgpre, dpin, cf, s1, s2, g_pre, g_q, g_kv,
              w_in_r, w_uq_r, w_k_exp, w_v, tm):
    seq = x.shape[0]

    def body(dq_ref, dk_ref, dv_ref, qd_ref, kvd_ref, x_ref, dh1_ref, dgpre_ref, dpin_ref, cf_ref, s1_ref, s2_ref,
             gpre_ref, gq_ref, gkv_ref, win_hbm, wuq_hbm, wk_hbm, wv_hbm,
             gx_ref, dproj_ref, dqb_ref, qn_ref, kvn_ref, dkb_ref, dvb_ref, dgpre_acc, dgq_acc, dgkv_acc,
             win, wuq, wk, wv):
        _load_once(((win_hbm, win), (wuq_hbm, wuq), (wk_hbm, wk), (wv_hbm, wv)))

        @pl.when(pl.program_id(0) == 0)
        def _():
            dgpre_acc[...] = jnp.zeros_like(dgpre_acc)
            dgq_acc[...] = jnp.zeros_like(dgq_acc)
            dgkv_acc[...] = jnp.zeros_like(dgkv_acc)

        for rows in _row_chains(tm):
            n_rows = rows.stop - rows.start
            cfv, s1v, s2v = cf_ref[rows, :], s1_ref[rows, :], s2_ref[rows, :]
            ksum = jnp.zeros((n_rows, HEAD_PAD), F32)
            for h in range(N_HEADS):
                lanes = slice(HEAD_PAD * h, HEAD_PAD * (h + 1))
                dqh = jnp.transpose(dq_ref[h, :, rows])
                dqb_ref[rows, lanes] = (_rope_t(dqh, cfv, s1v, s2v) * ATTN_SCALE).astype(BF16)
                dkh = dk_ref[h, rows, :]
                dkb_ref[rows, lanes] = dkh
                dvb_ref[rows, lanes] = dv_ref[h, rows, :]
                ksum = ksum + dkh.astype(F32)
            lane = lax.broadcasted_iota(jnp.int32, (n_rows, HEAD_PAD), 1)
            rope_lanes = (lane >= QK_NOPE) & (lane < QK_NOPE + QK_ROPE)
            dkr = _rope_t(jnp.where(rope_lanes, ksum, 0.0), cfv, s1v, s2v)

            qdv = qd_ref[rows, :]
            rq = _rms_r(qdv)
            qhat = qdv * rq
            qn_ref[rows, :] = (qhat * gq_ref[...]).astype(BF16)
            dqn = _mm_nt(dqb_ref[rows, :], wuq[...])
            dgq_acc[...] += _colsum(dqn * qhat)
            dproj_ref[rows, IN_Q0:IN_KV0] = _rms_bwd(qhat, rq, gq_ref[...], dqn).astype(BF16)

            kvdv = kvd_ref[rows, :]
            rkv = _rms_r(kvdv)
            kvhat = kvdv * rkv
            kvn_ref[rows, :] = (kvhat * gkv_ref[...]).astype(BF16)
            dkvn = _mm_nt(dkb_ref[rows, :], wk[...]) + _mm_nt(dvb_ref[rows, :], wv[...])
            dgkv_acc[...] += _colsum(dkvn * kvhat)
            dproj_ref[rows, IN_KV0:IN_POOL0] = _rms_bwd(kvhat, rkv, gkv_ref[...], dkvn).astype(BF16)

            dproj_ref[rows, IN_POOL0:IN_GATE0] = dpin_ref[rows, :]
            dproj_ref[rows, IN_GATE0:IN_KR0] = dgpre_ref[rows, :]
            dproj_ref[rows, IN_KR0:IN_R] = dkr.astype(BF16)

            da = _mm_nt(dproj_ref[rows, :], win[...])
            xv = x_ref[rows, :]
            r0 = _rms_r(xv)
            xhat = xv * r0
            dgpre_acc[...] += _colsum(da * xhat)
            gx_ref[rows, :] = dh1_ref[rows, :] + _rms_bwd(xhat, r0, gpre_ref[...], da)

    per_tile = ATTN_TILE // tm
    heads = pl.BlockSpec((N_HEADS, tm, HEAD_PAD), lambda i: (0, i, 0))
    heads_t = pl.BlockSpec((N_HEADS, None, HEAD_PAD, tm), lambda i: (0, i // per_tile, 0, i % per_tile))
    return pl.pallas_call(
        body, name="proj_bwd", grid=(seq // tm,),
        in_specs=[heads_t, heads, heads, _row(tm, Q_LORA), _row(tm, KV_LORA), _row(tm, D_MODEL),
                  _row(tm, D_MODEL), _row(tm, 2 * D_MODEL), _row(tm, POOL_WIDTH),
                  _row(tm, HEAD_PAD), _row(tm, HEAD_PAD), _row(tm, HEAD_PAD),
                  _fix((1, D_MODEL)), _fix((1, Q_LORA)), _fix((1, KV_LORA)), ANY, ANY, ANY, ANY],
        out_specs=[_row(tm, D_MODEL), _row(tm, IN_R), _row(tm, N_HEADS * HEAD_PAD), _row(tm, Q_LORA), _row(tm, KV_LORA),
                   _row(tm, N_HEADS * HEAD_PAD), _row(tm, N_HEADS * HEAD_PAD),
                   _fix((1, D_MODEL)), _fix((1, Q_LORA)), _fix((1, KV_LORA))],
        out_shape=[_sds((seq, D_MODEL), F32), _sds((seq, IN_R), BF16), _sds((seq, N_HEADS * HEAD_PAD), BF16),
                   _sds((seq, Q_LORA), BF16), _sds((seq, KV_LORA), BF16), _sds((seq, N_HEADS * HEAD_PAD), BF16),
                   _sds((seq, N_HEADS * HEAD_PAD), BF16),
                   _sds((1, D_MODEL), F32), _sds((1, Q_LORA), F32), _sds((1, KV_LORA), F32)],
        scratch_shapes=[pltpu.VMEM(w_in_r.shape, BF16), pltpu.VMEM(w_uq_r.shape, BF16),
                        pltpu.VMEM(w_k_exp.shape, BF16), pltpu.VMEM(w_v.shape, BF16)],
        compiler_params=_params(1, 58),
    )(dq, dk, dv, qd, kvd, x, dh1, dgpre, dpin, cf, s1, s2, g_pre, g_q, g_kv, w_in_r, w_uq_r, w_k_exp, w_v)


def _grad_w(a, b, name, square_a=False, by_device=False, with_bf16=False):
    seq, k_dim = a.shape
    n_dim = b.shape[1]
    tk = min(k_dim, 1024)
    tn = n_dim // 2 if n_dim == IN_R else min(n_dim, 1024)
    ts = min(seq, 2048)
    shard = n_dim // N_DEV
    per_tile = tn // shard
    if by_device:
        out_spec = pl.BlockSpec((per_tile, tk, shard), lambda i, j, s: (j, i, 0))
        out_shape = _sds((N_DEV, k_dim, shard), F32)
    else:
        out_spec = pl.BlockSpec((tk, tn), lambda i, j, s: (i, j))
        out_shape = _sds((k_dim, n_dim), F32)

    n_seq_steps = seq // ts

    def body(a_ref, b_ref, o_ref, *narrow):
        @pl.when(pl.program_id(2) == 0)
        def _():
            o_ref[...] = jnp.zeros_like(o_ref)

        at = a_ref[...]
        if square_a:
            at = at * at
        part = lax.dot_general(at, b_ref[...], TN, preferred_element_type=F32)
        if by_device:
            for d in range(per_tile):
                o_ref[d] += part[:, d * shard:(d + 1) * shard]
        else:
            o_ref[...] += part
        if with_bf16:
            @pl.when(pl.program_id(2) == n_seq_steps - 1)
            def _():
                narrow[0][...] = o_ref[...].astype(BF16)

    return pl.pallas_call(
        body, name=name, grid=(k_dim // tk, n_dim // tn, n_seq_steps),
        in_specs=[pl.BlockSpec((ts, tk), lambda i, j, s: (s, i)), pl.BlockSpec((ts, tn), lambda i, j, s: (s, j))],
        out_specs=[out_spec, out_spec] if with_bf16 else out_spec,
        out_shape=[out_shape, _sds(out_shape.shape, BF16)] if with_bf16 else out_shape,
        compiler_params=_params(3, 48),
    )(a, b)


def _position():
    return lax.axis_index("x"), lax.axis_index("y"), lax.axis_index("c")


def _gather_copies(x_ref, slot, send_sems, recv_sems, local_sem, phases=("send", "forward", "finish")):
    x, y, c = _position()
    me, sibling = (x, y, c), (x, y, 1 - c)
    chips = [(1 - x, y), (x, 1 - y), (1 - x, 1 - y)]

    def copy(k, block, to, src=None):
        return pltpu.make_async_remote_copy(
            src_ref=slot(*block) if src is None else src, dst_ref=slot(*block),
            send_sem=send_sems.at[k], recv_sem=recv_sems.at[k], device_id=to, device_id_type=MESH)

    mine = pltpu.make_async_copy(x_ref, slot(*me), local_sem)
    first = [copy(0, me, sibling, src=x_ref)]
    first += [copy(1 + j, me, (*chip, c), src=x_ref) for j, chip in enumerate(chips)]
    passed = [copy(4 + j, (*chip, c), sibling) for j, chip in enumerate(chips)]
    if "send" in phases:
        mine.start()
        for cp in first:
            cp.start()
    if "forward" in phases:
        for j, chip in enumerate(chips):
            copy(1 + j, (*chip, c), me).wait_recv()
            passed[j].start()
    if "finish" in phases:
        copy(0, sibling, me).wait_recv()
        for j, chip in enumerate(chips):
            copy(4 + j, (*chip, 1 - c), me).wait_recv()
        for cp in first + passed:
            cp.wait_send()
        mine.wait()


def _all_gather_hbm(block):
    def body(x_ref, out_ref, send_sems, recv_sems, local_sem):
        _gather_copies(x_ref, lambda px, py, pc: out_ref.at[4 * px + 2 * py + pc], send_sems, recv_sems, local_sem)

    return pl.pallas_call(
        body, name="gather_weights",
        in_specs=[ANY], out_specs=ANY,
        out_shape=_sds((N_DEV,) + block.shape, block.dtype),
        scratch_shapes=[pltpu.SemaphoreType.DMA((7,)), pltpu.SemaphoreType.DMA((7,)), pltpu.SemaphoreType.DMA],
    )(block)


def _replicated_update(grads, loss_block, ws, ms, vs):
    n_p = len(grads)
    sent = list(grads) + [loss_block]
    n_a = len(sent)

    def body(*refs):
        g_refs, refs = refs[:n_a], refs[n_a:]
        w_refs, m_refs, v_refs, refs = refs[:n_p], refs[n_p:2 * n_p], refs[2 * n_p:3 * n_p], refs[3 * n_p:]
        sum_refs, refs = refs[:n_a], refs[n_a:]
        d_refs, nm_refs, nv_refs, refs = refs[:n_p], refs[n_p:2 * n_p], refs[2 * n_p:3 * n_p], refs[3 * n_p:]
        bufs, (send_sems, recv_sems, local_sems) = refs[:n_a], refs[n_a:]
        x, y, c = _position()
        me = 4 * x + 2 * y + c
        local, remote = [], []
        for a in range(n_a):
            local.append(pltpu.make_async_copy(g_refs[a], bufs[a].at[me], local_sems.at[a]))
            for r in range(1, N_DEV):
                peer = (1 - x if r & 4 else x, 1 - y if r & 2 else y, 1 - c if r & 1 else c)
                remote.append(pltpu.make_async_remote_copy(
                    src_ref=g_refs[a], dst_ref=bufs[a].at[me],
                    send_sem=send_sems.at[(N_DEV - 1) * a + r - 1], recv_sem=recv_sems.at[(N_DEV - 1) * a + r - 1],
                    device_id=peer, device_id_type=MESH))
        for cp in local + remote:
            cp.start()
        for cp in remote:
            cp.wait_recv()
        for cp in remote:
            cp.wait_send()
        for cp in local:
            cp.wait()
        for a in range(n_a):
            acc = bufs[a][0]
            for d in range(1, N_DEV):
                acc = acc + bufs[a][d]
            if a == n_p:
                sum_refs[a][...] = acc
                continue
            delta, new_m, new_v = _adamw_math(acc, w_refs[a][...], m_refs[a][...], v_refs[a][...])
            sum_refs[a][...], d_refs[a][...], nm_refs[a][...], nv_refs[a][...] = acc, delta, new_m, new_v

    vmem = pl.BlockSpec(memory_space=pltpu.VMEM)
    like_w = [_sds(w.shape, F32) for w in ws]
    n_sem = (N_DEV - 1) * n_a
    outs = pl.pallas_call(
        body, name="replicated_update",
        in_specs=[vmem] * (n_a + 3 * n_p), out_specs=[vmem] * (n_a + 3 * n_p),
        out_shape=like_w + [_sds(loss_block.shape, F32)] + like_w * 3,
        scratch_shapes=[pltpu.VMEM((N_DEV,) + g.shape, F32) for g in sent]
                       + [pltpu.SemaphoreType.DMA((n_sem,)), pltpu.SemaphoreType.DMA((n_sem,)),
                          pltpu.SemaphoreType.DMA((n_a,))],
        compiler_params=pltpu.CompilerParams(vmem_limit_bytes=32 * MIB),
    )(*sent, *ws, *ms, *vs)
    return (outs[:n_p], outs[n_p], outs[n_a:n_a + n_p], outs[n_a + n_p:n_a + 2 * n_p], outs[n_a + 2 * n_p:])


def _exchange_pair(gs):
    n_w = len(gs)

    def body(*refs):
        g_refs, out_refs = refs[:n_w], refs[n_w:2 * n_w]
        send_sems, recv_sems = refs[2 * n_w:]
        x, y, c = _position()
        copies = []
        for w in range(n_w):
            for chip in range(4):
                cp = pltpu.make_async_remote_copy(
                    src_ref=g_refs[w].at[2 * chip + (1 - c)], dst_ref=out_refs[w].at[chip],
                    send_sem=send_sems.at[4 * w + chip], recv_sem=recv_sems.at[4 * w + chip],
                    device_id=(x, y, 1 - c), device_id_type=MESH)
                cp.start()
                copies.append(cp)
        for cp in copies:
            cp.wait_recv()
        for cp in copies:
            cp.wait_send()

    return pl.pallas_call(
        body, name="exchange_pair",
        in_specs=[ANY] * n_w, out_specs=[ANY] * n_w,
        out_shape=[_sds((4,) + g.shape[1:], g.dtype) for g in gs],
        scratch_shapes=[pltpu.SemaphoreType.DMA((4 * n_w,)), pltpu.SemaphoreType.DMA((4 * n_w,))],
    )(*gs)


def _exchange_chips(parts):
    n_w = len(parts)

    def body(*refs):
        p_refs, out_refs = refs[:n_w], refs[n_w:2 * n_w]
        send_sems, recv_sems = refs[2 * n_w:]
        x, y, c = _position()
        chips = [(1 - x, y), (x, 1 - y), (1 - x, 1 - y)]
        copies = []
        for w in range(n_w):
            for k, (px, py) in enumerate(chips):
                cp = pltpu.make_async_remote_copy(
                    src_ref=p_refs[w].at[2 * px + py], dst_ref=out_refs[w].at[k],
                    send_sem=send_sems.at[3 * w + k], recv_sem=recv_sems.at[3 * w + k],
                    device_id=(px, py, c), device_id_type=MESH)
                cp.start()
                copies.append(cp)
        for cp in copies:
            cp.wait_recv()
        for cp in copies:
            cp.wait_send()

    return pl.pallas_call(
        body, name="exchange_chips",
        in_specs=[ANY] * n_w, out_specs=[ANY] * n_w,
        out_shape=[_sds((3,) + p.shape[1:], p.dtype) for p in parts],
        scratch_shapes=[pltpu.SemaphoreType.DMA((3 * n_w,)), pltpu.SemaphoreType.DMA((3 * n_w,))],
    )(*parts)


def _row_tile(k):
    return 256 if k % 256 == 0 else 128


def _pair_sum(g, recv, place, name):
    _, k, n = g.shape
    tr = _row_tile(k)
    g4 = g.reshape(4, 2, k, n)

    def body(s_ref, g_ref, r_ref, o_ref):
        o_ref[...] = (g_ref[...] + r_ref[...]).astype(BF16)

    spec = pltpu.PrefetchScalarGridSpec(
        num_scalar_prefetch=1, grid=(4, k // tr),
        in_specs=[pl.BlockSpec((None, None, tr, n), lambda j, i, s: (j, s[2], i, 0)),
                  pl.BlockSpec((None, tr, n), lambda j, i, s: (j, i, 0))],
        out_specs=pl.BlockSpec((None, tr, n), lambda j, i, s: (j, i, 0)))
    return pl.pallas_call(
        body, name=name, grid_spec=spec, out_shape=_sds((4, k, n), BF16),
        compiler_params=_params(2, 32),
    )(place, g4, recv)


def _adamw_math(g, w, m, v):
    m = ADAM_B1 * m + (1.0 - ADAM_B1) * g
    v = ADAM_B2 * v + (1.0 - ADAM_B2) * (g * g)
    m_hat = m / (1.0 - ADAM_B1 ** ADAM_STEP)
    v_hat = v / (1.0 - ADAM_B2 ** ADAM_STEP)
    delta = -ADAM_LR * (m_hat / (jnp.sqrt(v_hat) + ADAM_EPS) + ADAM_WD * w)
    return delta, m, v


def _adamw_sharded(g, from_sibling, from_chips, place, w, m, v, name):
    _, k, n = g.shape
    tr = _row_tile(k)

    def body(s_ref, g_ref, sib_ref, r0_ref, r1_ref, r2_ref, w_ref, m_ref, v_ref, grad_ref, d_ref, nm_ref, nv_ref):
        grad = g_ref[...] + sib_ref[...]
        for r_ref in (r0_ref, r1_ref, r2_ref):
            grad = grad + r_ref[...].astype(F32)
        grad_ref[...] = grad
        d_ref[...], nm_ref[...], nv_ref[...] = _adamw_math(grad, w_ref[...], m_ref[...], v_ref[...])

    tile = pl.BlockSpec((None, tr, n), lambda i, s: (0, i, 0))

    def slot(j):
        return pl.BlockSpec((None, tr, n), lambda i, s: (j, i, 0))

    spec = pltpu.PrefetchScalarGridSpec(
        num_scalar_prefetch=1, grid=(k // tr,),
        in_specs=[pl.BlockSpec((None, tr, n), lambda i, s: (s[0], i, 0)),
                  pl.BlockSpec((None, tr, n), lambda i, s: (s[1], i, 0)),
                  slot(0), slot(1), slot(2), tile, tile, tile],
        out_specs=[tile] * 4)
    return pl.pallas_call(
        body, name=name, grid_spec=spec, out_shape=[_sds((1, k, n), F32)] * 4,
        compiler_params=_params(1, 48),
    )(place, g, from_sibling, from_chips, from_chips, from_chips, w, m, v)


def _adamw_direct(g, received, place, w, m, v, name):
    _, k, n = g.shape
    tr = _row_tile(k)

    def body(s_ref, g_ref, r_ref, w_ref, m_ref, v_ref, grad_ref, d_ref, nm_ref, nv_ref):
        grad = g_ref[...]
        for r in range(N_DEV - 1):
            grad = grad + r_ref[r].astype(F32)
        grad_ref[...] = grad
        d_ref[...], nm_ref[...], nv_ref[...] = _adamw_math(grad, w_ref[...], m_ref[...], v_ref[...])

    tile = pl.BlockSpec((None, tr, n), lambda i, s: (0, i, 0))
    spec = pltpu.PrefetchScalarGridSpec(
        num_scalar_prefetch=1, grid=(k // tr,),
        in_specs=[pl.BlockSpec((None, tr, n), lambda i, s: (s[0], i, 0)),
                  pl.BlockSpec((N_DEV - 1, tr, n), lambda i, s: (0, i, 0)), tile, tile, tile],
        out_specs=[tile] * 4)
    return pl.pallas_call(
        body, name=name, grid_spec=spec, out_shape=[_sds((1, k, n), F32)] * 4,
        compiler_params=_params(1, 48),
    )(place, g, received, w, m, v)


def _pack_rows(parts):
    parts = [a.reshape(-1, LANES) for a in parts]
    pad = (-sum(a.shape[0] for a in parts)) % PACK_ROW_TILE
    return jnp.concatenate(parts + [jnp.zeros((pad, LANES), parts[0].dtype)], axis=0)


def _full_from_gathered(gathered, entries, shard_shapes):
    out, off = {}, 0
    for (name, kind), (k, n) in zip(entries, shard_shapes):
        rows = k * n // LANES
        seg = gathered[:, off:off + rows].reshape(N_DEV, k, n)
        out[name] = jnp.transpose(seg, (1, 0, 2)).reshape(k, N_DEV * n) if kind == "col" else seg.reshape(N_DEV * k, n)
        off += rows
    return out


def _columns_by_device(a):
    k, n_all = a.shape
    return jnp.transpose(a.reshape(k, N_DEV, n_all // N_DEV), (1, 0, 2))


def _rows_by_device(a):
    k_all, n = a.shape
    return a.reshape(N_DEV, k_all // N_DEV, n)


def _rope_lane_frequencies():
    inv_freq = ROPE_THETA ** (-jnp.arange(0, QK_ROPE, 2, dtype=F32) / QK_ROPE)
    zeros = lambda n: jnp.zeros((n,), F32)
    return jnp.concatenate([zeros(QK_NOPE), inv_freq, inv_freq, zeros(HEAD_PAD - QK_NOPE - QK_ROPE)])[None, :]


def _rope_tables(pos_row, freq, tm):
    pos = jnp.transpose(jnp.broadcast_to(pos_row.astype(F32), (HEAD_PAD, tm)))
    ang = pos * freq
    cos, sin = jnp.cos(ang), jnp.sin(ang)
    lane = lax.broadcasted_iota(jnp.int32, (tm, HEAD_PAD), 1)
    first = (lane >= QK_NOPE) & (lane < QK_NOPE + QK_ROPE // 2)
    second = (lane >= QK_NOPE + QK_ROPE // 2) & (lane < QK_NOPE + QK_ROPE)
    cf = jnp.where(lane < QK_NOPE, 1.0, jnp.where(first | second, cos, 0.0))
    return cf, jnp.where(first, -sin, 0.0), jnp.where(second, sin, 0.0)


def _arrange_w_in(w):
    k = w.shape[0]
    zeros = lambda n: jnp.zeros((k, n), w.dtype)
    kr0 = Q_LORA + KV_LORA
    pool0 = kr0 + QK_ROPE
    return jnp.concatenate([w[:, :kr0], w[:, pool0:], zeros(QK_NOPE), w[:, kr0:pool0],
                            zeros(HEAD_PAD - QK_NOPE - QK_ROPE)], axis=1)


def _restore_w_in(d):
    kr = d[:, IN_KR0 + QK_NOPE:IN_KR0 + QK_NOPE + QK_ROPE]
    return jnp.concatenate([d[:, :IN_POOL0], kr, d[:, IN_POOL0:IN_KR0]], axis=1)


def _pad_heads(w, width):
    k = w.shape[0]
    w = w.reshape(k, N_HEADS, width)
    return jnp.pad(w, ((0, 0), (0, 0), (0, HEAD_PAD - width))).reshape(k, N_HEADS * HEAD_PAD)


def _unpad_heads(d, width):
    k = d.shape[0]
    return d.reshape(k, N_HEADS, HEAD_PAD)[:, :, :width]


def kernel(x, p, positions, g_pre_mix, w_in, b_gate, g_q, w_uq, g_kv, w_ukv, w_pool, pool_scale, w_branch_attn, w_branch_pool, w_out, g_post_mix, g_pre_mlp, w_ff1, w_ff2, g_post_mlp, w_ple_proj, w_ple_gate, g_ple, loss_target, m_g_pre_mix, m_w_in, m_b_gate, m_g_q, m_w_uq, m_g_kv, m_w_ukv, m_w_pool, m_pool_scale, m_w_branch_attn, m_w_branch_pool, m_w_out, m_g_post_mix, m_g_pre_mlp, m_w_ff1, m_w_ff2, m_g_post_mlp, m_w_ple_proj, m_w_ple_gate, m_g_ple, v_g_pre_mix, v_w_in, v_b_gate, v_g_q, v_w_uq, v_g_kv, v_w_ukv, v_w_pool, v_pool_scale, v_w_branch_attn, v_w_branch_pool, v_w_out, v_g_post_mix, v_g_pre_mlp, v_w_ff1, v_w_ff2, v_g_post_mlp, v_w_ple_proj, v_w_ple_gate, v_g_ple):
    given = dict(locals())
    weights = {n: given[n] for n in WEIGHT_ORDER}
    moments_m = {n: given["m_" + n] for n in WEIGHT_ORDER}
    moments_v = {n: given["v_" + n] for n in WEIGHT_ORDER}
    xs, ps, target = x[0], p[0, 0], loss_target[0]
    seq = xs.shape[0]
    tm = min(256, seq)
    tm_merge = min(512, seq)
    core = lax.axis_index("c")
    chip = 2 * lax.axis_index("x") + lax.axis_index("y")

    early, later = SHARDED[:N_EARLY], SHARDED[N_EARLY:]
    shapes_of = lambda entries: [weights[n].shape[1:] for n, _ in entries]
    pack_bf16 = lambda entries: _pack_rows([weights[n][0].astype(BF16) for n, _ in entries])
    full = _full_from_gathered(_all_gather_hbm(pack_bf16(early)), early, shapes_of(early))
    w_in_r = _arrange_w_in(full["w_in"])
    w_uq_r = _pad_heads(full["w_uq"], QK_NOPE + QK_ROPE)
    ukv = full["w_ukv"].reshape(KV_LORA, N_HEADS, QK_NOPE + V_HEAD)
    w_k_exp = _pad_heads(ukv[:, :, :QK_NOPE].reshape(KV_LORA, N_HEADS * QK_NOPE), QK_NOPE)
    w_v = _pad_heads(ukv[:, :, QK_NOPE:].reshape(KV_LORA, N_HEADS * V_HEAD), V_HEAD)
    w_pool_bf = w_pool[0].astype(BF16)

    a_bf, qd, kvd, pin, gates, q, k, v, k_t, cf, s1, s2, gathered_later = _proj_fwd(
        xs, g_pre_mix, b_gate, g_q, g_kv, positions, w_in_r, w_uq_r, w_k_exp, w_v, pack_bf16(later), tm_merge)
    full.update(_full_from_gathered(gathered_later, later, shapes_of(later)))
    w_ba = jnp.pad(full["w_branch_attn"].reshape(N_HEADS, V_HEAD, D_MODEL),
                   ((0, 0), (0, HEAD_PAD - V_HEAD), (0, 0))).reshape(N_HEADS * HEAD_PAD, D_MODEL)
    d_pool, pooled = _pool_fwd(pin, w_pool_bf, pool_scale, tm_merge)
    o_heads, lse = _attn_fwd(q, k, v)
    merged, ba, bb, y, h1, attn_rows = _merge_fwd(o_heads, pooled, gates, xs, g_post_mix, w_ba,
                                                  full["w_branch_pool"], full["w_out"], tm_merge)
    (m_bf, zr, f, h2_bf, p_bf, de, dpre, dh2, loss_acc, dg_ple) = _tail_fwd(
        h1, target, ps, g_pre_mlp, g_post_mlp, g_ple, full["w_ff1"], full["w_ff2"], full["w_ple_proj"],
        full["w_ple_gate"], tm)

    by_device, payload = {}, {}

    def early(name, pair, layout=lambda g: g):
        by_device[name], payload[name] = layout(pair[0]), layout(pair[1])

    early("w_ple_proj", _grad_w(p_bf, de, "grad_w_ple_proj", by_device=True, with_bf16=True))
    early("w_ple_gate", _grad_w(h2_bf, dpre, "grad_w_ple_gate", with_bf16=True), _rows_by_device)
    df, dz, dh1, dg_pre_mlp, dg_post_mlp = _mlp_bwd(h1, f, zr, dh2, g_pre_mlp, g_post_mlp, full["w_ff1"],
                                                    full["w_ff2"], tm)
    early("w_ff1", _grad_w(m_bf, dz, "grad_w_ff1", by_device=True, with_bf16=True))
    early("w_ff2", _grad_w(zr, df, "grad_w_ff2", square_a=True, with_bf16=True), _rows_by_device)
    (dy, dba, dbb, dgpre, do_heads, dd, dg_post_mix, db_gate, dpool_scale, dw_pool) = _merge_bwd(
        dh1, y, gates, ba, bb, d_pool, g_post_mix, pool_scale, full["w_out"], w_ba,
        full["w_branch_pool"], w_pool_bf, tm_merge)
    early("w_branch_attn", _grad_w(attn_rows, dba, "grad_w_branch_attn", with_bf16=True),
          lambda g: _columns_by_device(g.reshape(N_HEADS, HEAD_PAD, D_MODEL)[:, :V_HEAD].reshape(-1, D_MODEL)))
    early("w_branch_pool", _grad_w(pooled, dbb, "grad_w_branch_pool", by_device=True, with_bf16=True))
    early("w_out", _grad_w(merged, dy, "grad_w_out", with_bf16=True), _rows_by_device)
    dpin = _pool_bwd_window(dd, tm_merge)
    delta = _attn_delta(o_heads, do_heads)
    direct = [n for n, _ in SHARDED[N_EARLY:]]
    outs = _attn_bwd(q, k, k_t, v, do_heads, lse, delta, [payload[n] for n in direct])
    dq, dk, dv = outs[:3]
    received = dict(zip(direct, outs[3:]))
    (grad_x, dproj, dq_bf, qn_bf, kvn_bf, dk_bf, dv_bf, dg_pre_mix, dg_q, dg_kv) = _proj_bwd(
        dq, dk, dv, qd, kvd, xs, dh1, dgpre, dpin, cf, s1, s2, g_pre_mix, g_q, g_kv, w_in_r, w_uq_r, w_k_exp, w_v,
        tm_merge)
    d_k_exp = _unpad_heads(_grad_w(kvn_bf, dk_bf, "grad_w_uk"), QK_NOPE)
    d_w_v = _unpad_heads(_grad_w(kvn_bf, dv_bf, "grad_w_uv"), V_HEAD)
    by_device["w_in"] = _columns_by_device(_restore_w_in(_grad_w(a_bf, dproj, "grad_w_in")))
    by_device["w_uq"] = _columns_by_device(
        _unpad_heads(_grad_w(qn_bf, dq_bf, "grad_w_uq"), QK_NOPE + QK_ROPE).reshape(Q_LORA, -1))
    by_device["w_ukv"] = _columns_by_device(jnp.concatenate([d_k_exp, d_w_v], axis=2).reshape(KV_LORA, -1))
    grads_small = {
        "g_pre_mix": dg_pre_mix, "b_gate": db_gate, "g_q": dg_q, "g_kv": dg_kv,
        "w_pool": dw_pool, "pool_scale": dpool_scale, "g_post_mix": dg_post_mix,
        "g_pre_mlp": dg_pre_mlp, "g_post_mlp": dg_post_mlp, "g_ple": dg_ple,
    }

    names = [n for n, _ in SHARDED]
    place = jnp.stack([2 * chip + core, chip, core]).astype(jnp.int32)
    sharded = {n: _adamw_direct(by_device[n], received[n], place, weights[n], moments_m[n], moments_v[n],
                                "adamw_" + n) for n in direct}
    last = [n for n, _ in SHARDED[:N_EARLY]]
    own = [by_device[n] for n in last]
    from_sibling = _exchange_pair(own)
    pair = [_pair_sum(g, r, place, "pair_sum_" + n) for n, g, r in zip(last, own, from_sibling)]
    from_chips = _exchange_chips(pair)
    sharded.update({n: _adamw_sharded(g, r, rc, place, weights[n], moments_m[n], moments_v[n], "adamw_" + n)
                    for n, g, r, rc in zip(last, own, from_sibling, from_chips)})

    flat = lambda a: a.reshape(a.shape[-3:]) if a.ndim > 3 else a
    g_sm, loss_sum, d_sm, m_sm, v_sm = _replicated_update(
        [flat(grads_small[n]) for n in REPLICATED], loss_acc, [flat(weights[n]) for n in REPLICATED],
        [flat(moments_m[n]) for n in REPLICATED], [flat(moments_v[n]) for n in REPLICATED])

    results = []
    for which, small in enumerate((g_sm, d_sm, m_sm, v_sm)):
        named = {n: sharded[n][which] for n in names}
        named.update({n: a.reshape(weights[n].shape) for n, a in zip(REPLICATED, small)})
        results.append([named[n] for n in WEIGHT_ORDER])

    return (loss_sum[0, 0], grad_x[None], *results[0], *results[1], *results[2], *results[3])
```

```python
import jax
import jax.numpy as jnp
from jax import lax
from jax.experimental import pallas as pl
from jax.experimental.pallas import tpu as pltpu

F32 = jnp.float32
BF16 = jnp.bfloat16

D_MODEL = 1024
PLE_DIM = 256
N_HEADS = 8
QK_NOPE = 64
QK_ROPE = 32
V_HEAD = 64
Q_LORA = 384
KV_LORA = 256
POOL_WINDOWS = (2, 4, 8, 16)
POOL_GROUP = 128
POOL_WIDTH = 512
D_FF = 4096
ROPE_THETA = 10000.0
EPS = 1e-6
HEAD_PAD = 128
K_ALL = N_HEADS * HEAD_PAD
ATTN_SCALE = (QK_NOPE + QK_ROPE) ** -0.5
LOG2E = 1.4426950408889634
Q_PRESCALE = ATTN_SCALE * LOG2E
ATTN_TILE = 512
FWD_ROWS = 512
FWD_CHAINS = 4
BWD_CHAINS = 2
BWD_QUERIES = 2

ADAM_LR = 0.001
ADAM_B1 = 0.9
ADAM_B2 = 0.999
ADAM_EPS = 1e-08
ADAM_WD = 0.01
ADAM_STEP = 10

N_DEV = 8
LANES = 1024
PACK_ROW_TILE = 480
POOL_HALO = 16
MIB = 2 ** 20

IN_Q0, IN_KV0, IN_POOL0, IN_GATE0, IN_KR0, IN_R = 0, 384, 640, 1152, 3200, 3328

SHARDED = (("w_in", "col"), ("w_uq", "col"), ("w_ukv", "col"), ("w_branch_attn", "col"),
           ("w_branch_pool", "col"), ("w_out", "row"), ("w_ff1", "col"), ("w_ff2", "row"),
           ("w_ple_proj", "col"), ("w_ple_gate", "row"))
N_EARLY = 3
REPLICATED = ("g_pre_mix", "b_gate", "g_q", "g_kv", "w_pool", "pool_scale", "g_post_mix",
              "g_pre_mlp", "g_post_mlp", "g_ple")
WEIGHT_ORDER = ("g_pre_mix", "w_in", "b_gate", "g_q", "w_uq", "g_kv", "w_ukv", "w_pool", "pool_scale",
                "w_branch_attn", "w_branch_pool", "w_out", "g_post_mix", "g_pre_mlp", "w_ff1", "w_ff2",
                "g_post_mlp", "w_ple_proj", "w_ple_gate", "g_ple")

NT = (((1,), (1,)), ((), ()))
TN = (((0,), (0,)), ((), ()))
MESH = pl.DeviceIdType.MESH
ANY = pl.BlockSpec(memory_space=pl.ANY)


def _params(n_axes, vmem_mib):
    return pltpu.CompilerParams(dimension_semantics=("arbitrary",) * n_axes, vmem_limit_bytes=vmem_mib * MIB)


def _row(tm, n):
    return pl.BlockSpec((tm, n), lambda i: (i, 0))


def _fix(shape):
    zeros = (0,) * len(shape)
    return pl.BlockSpec(shape, lambda i: zeros)


def _sds(shape, dtype):
    return jax.ShapeDtypeStruct(shape, dtype)


def _rms_r(v):
    return lax.rsqrt(jnp.mean(v * v, axis=-1, keepdims=True) + EPS)


def _rms_bwd(vhat, r, g, dy):
    gdy = dy * g
    return r * (gdy - vhat * jnp.mean(gdy * vhat, axis=-1, keepdims=True))


def _colsum(v):
    return jnp.sum(v, axis=0, keepdims=True)


def _sigmoid(v):
    return 1.0 / (1.0 + jnp.exp(-v))


def _mm(a, b):
    return jnp.dot(a, b, preferred_element_type=F32)


def _mm_nt(a, b):
    return lax.dot_general(a, b, NT, preferred_element_type=F32)


def _rope(c, cf, s1, s2):
    return c * cf + pltpu.roll(c, HEAD_PAD - 16, 1) * s1 + pltpu.roll(c, 16, 1) * s2


def _rope_t(c, cf, s1, s2):
    return c * cf + pltpu.roll(c * s1, 16, 1) + pltpu.roll(c * s2, HEAD_PAD - 16, 1)


def _row_chains(tm, rows=256):
    rows = min(rows, tm)
    return [slice(c * rows, (c + 1) * rows) for c in range(tm // rows)]


def _load_once(pairs):
    @pl.when(pl.program_id(0) == 0)
    def _():
        for src, dst in pairs:
            pltpu.sync_copy(src, dst)


def _column_blocks(by_device_hbm, full_vmem):
    n = by_device_hbm.shape[2]
    return tuple((by_device_hbm.at[d], full_vmem.at[:, d * n:(d + 1) * n]) for d in range(N_DEV))


def _proj_fwd(x, g_pre, b_gate, g_q, g_kv, positions, w_in_r, w_uq_r, w_k_exp, w_v, later_shards, tm):
    seq = x.shape[0]
    n_steps = seq // tm
    forward_step = (3 * n_steps) // 4
    n_later = len(later_shards)

    def body(x_ref, gpre_ref, bg_ref, gq_ref, gkv_ref, pos_ref, freq_ref, win_hbm, wuq_hbm, wk_hbm, wv_hbm, *rest):
        later_refs, rest = rest[:n_later], rest[n_later:]
        (a_ref, qd_ref, kvd_ref, pin_ref, gates_ref, q_ref, k_ref, v_ref, kt_ref, cf_ref, s1_ref, s2_ref) = rest[:12]
        gathered_refs, (win, wuq, wk, wv, send_sems, recv_sems, local_sems) = rest[12:12 + n_later], rest[12 + n_later:]
        step = pl.program_id(0)

        def gather(phase):
            for a, (src, dst) in enumerate(zip(later_refs, gathered_refs)):
                _gather_copies(src, lambda px, py, pc, dst=dst: dst.at[4 * px + 2 * py + pc],
                               send_sems, recv_sems, local_sems.at[a], phases=(phase,), sem_base=7 * a)

        pl.when(step == 0)(lambda: gather("send"))
        pl.when(step == forward_step)(lambda: gather("forward"))
        _load_once(((win_hbm, win), (wuq_hbm, wuq), (wk_hbm, wk), (wv_hbm, wv)))
        for rows in _row_chains(tm):
            n_rows = rows.stop - rows.start
            xv = x_ref[rows, :]
            a = (xv * _rms_r(xv) * gpre_ref[...]).astype(BF16)
            a_ref[rows, :] = a
            proj = _mm(a, win[...])
            qd = proj[:, IN_Q0:IN_KV0]
            kvd = proj[:, IN_KV0:IN_POOL0]
            qd_ref[rows, :] = qd
            kvd_ref[rows, :] = kvd
            pin_ref[rows, :] = proj[:, IN_POOL0:IN_GATE0]
            gates_ref[rows, :] = _sigmoid(proj[:, IN_GATE0:IN_KR0] + bg_ref[...]).astype(BF16)
            cfv, s1v, s2v = _rope_tables(pos_ref[:, rows], freq_ref[...], n_rows)
            cf_ref[rows, :], s1_ref[rows, :], s2_ref[rows, :] = cfv, s1v, s2v
            krr = _rope(proj[:, IN_KR0:IN_R], cfv, s1v, s2v)
            qn = (qd * _rms_r(qd) * gq_ref[...]).astype(BF16)
            qf = _mm(qn, wuq[...])
            kvn = (kvd * _rms_r(kvd) * gkv_ref[...]).astype(BF16)
            kf = _mm(kvn, wk[...])
            vf = _mm(kvn, wv[...])
            one_lane = (lax.broadcasted_iota(jnp.int32, (n_rows, HEAD_PAD), 1) == V_HEAD).astype(F32)
            for h in range(N_HEADS):
                lanes = slice(HEAD_PAD * h, HEAD_PAD * (h + 1))
                q_ref[h, rows, :] = (_rope(qf[:, lanes], cfv, s1v, s2v) * Q_PRESCALE).astype(BF16)
                kh = kf[:, lanes] + krr
                vh = vf[:, lanes] + one_lane
                k_ref[h, rows, :] = kh.astype(BF16)
                v_ref[h, rows, :] = vh.astype(BF16)
                kt_ref[h, :, rows] = jnp.transpose(kh).astype(BF16)
        pl.when(step == n_steps - 1)(lambda: gather("finish"))

    per_tile = ATTN_TILE // tm
    heads = pl.BlockSpec((N_HEADS, tm, HEAD_PAD), lambda i: (0, i, 0))
    heads_t = pl.BlockSpec((N_HEADS, None, HEAD_PAD, tm), lambda i: (0, i // per_tile, 0, i % per_tile))
    heads_t_shape = _sds((N_HEADS, seq // ATTN_TILE, HEAD_PAD, ATTN_TILE), BF16)
    return pl.pallas_call(
        body, name="proj_fwd", grid=(seq // tm,),
        in_specs=[_row(tm, D_MODEL), _fix((1, D_MODEL)), _fix((1, 2 * D_MODEL)), _fix((1, Q_LORA)), _fix((1, KV_LORA)),
                  pl.BlockSpec((1, tm), lambda i: (0, i)), _fix((1, HEAD_PAD)), ANY, ANY, ANY, ANY] + [ANY] * n_later,
        out_specs=[_row(tm, D_MODEL), _row(tm, Q_LORA), _row(tm, KV_LORA), _row(tm, POOL_WIDTH), _row(tm, 2 * D_MODEL),
                   heads, heads, heads, heads_t, _row(tm, HEAD_PAD), _row(tm, HEAD_PAD), _row(tm, HEAD_PAD)]
                  + [ANY] * n_later,
        out_shape=[_sds((seq, D_MODEL), BF16), _sds((seq, Q_LORA), F32), _sds((seq, KV_LORA), F32),
                   _sds((seq, POOL_WIDTH), F32), _sds((seq, 2 * D_MODEL), BF16),
                   _sds((N_HEADS, seq, HEAD_PAD), BF16), _sds((N_HEADS, seq, HEAD_PAD), BF16),
                   _sds((N_HEADS, seq, HEAD_PAD), BF16), heads_t_shape,
                   _sds((seq, HEAD_PAD), F32), _sds((seq, HEAD_PAD), F32), _sds((seq, HEAD_PAD), F32)]
                  + [_sds((N_DEV,) + s.shape, s.dtype) for s in later_shards],
        scratch_shapes=[pltpu.VMEM(w_in_r.shape, BF16), pltpu.VMEM(w_uq_r.shape, BF16),
                        pltpu.VMEM(w_k_exp.shape, BF16), pltpu.VMEM(w_v.shape, BF16),
                        pltpu.SemaphoreType.DMA((7 * n_later,)), pltpu.SemaphoreType.DMA((7 * n_later,)),
                        pltpu.SemaphoreType.DMA((n_later,))],
        compiler_params=_params(1, 48),
    )(x, g_pre, b_gate, g_q, g_kv, positions, _rope_lane_frequencies(), w_in_r, w_uq_r, w_k_exp, w_v, *later_shards)


def _window_count(row0, n_rows):
    t = row0 + lax.broadcasted_iota(jnp.int32, (n_rows, POOL_GROUP), 0)
    return [jnp.minimum(t + 1, w).astype(F32) for w in POOL_WINDOWS]


def _pool_fwd(pin, w_pool_bf, pool_scale, tm):
    seq = pin.shape[0]
    ext_rows = tm + POOL_HALO

    def body(prev_ref, u_ref, wp_ref, ps_ref, d_ref, pooled_ref):
        i = pl.program_id(0)
        prev = jnp.where(i == 0, 0.0, prev_ref[...])
        u = u_ref[...]
        level = jnp.concatenate([prev, u], axis=0)
        counts = _window_count(i * tm, tm)
        shift = 1
        for g in range(len(POOL_WINDOWS)):
            level = level + pltpu.roll(level, shift, 0)
            shift *= 2
            lanes = slice(POOL_GROUP * g, POOL_GROUP * (g + 1))
            d = (level[POOL_HALO:, lanes] / counts[g] - u[:, lanes]).astype(BF16)
            d_ref[:, lanes] = d
            pooled_ref[:, lanes] = (_mm(d, wp_ref[g]) * ps_ref[:, lanes]).astype(BF16)

    halo = tm // POOL_HALO
    return pl.pallas_call(
        body, name="pool_fwd", grid=(seq // tm,),
        in_specs=[pl.BlockSpec((POOL_HALO, POOL_WIDTH), lambda i: (jnp.maximum(i * halo - 1, 0), 0)),
                  _row(tm, POOL_WIDTH), _fix(w_pool_bf.shape), _fix((1, POOL_WIDTH))],
        out_specs=[_row(tm, POOL_WIDTH), _row(tm, POOL_WIDTH)],
        out_shape=[_sds((seq, POOL_WIDTH), BF16), _sds((seq, POOL_WIDTH), BF16)],
        compiler_params=_params(1, 32),
    )(pin, pin, w_pool_bf, pool_scale)


def _pool_bwd_window(dd, tm):
    seq = dd.shape[0]
    n_tiles = seq // tm
    ext_rows = tm + POOL_HALO

    def body(dd_ref, next_ref, dpin_ref):
        i = pl.program_id(0)
        nxt = jnp.where(i == n_tiles - 1, 0.0, next_ref[...])
        dd_t = dd_ref[...]
        ext = jnp.concatenate([dd_t, nxt], axis=0)
        counts = _window_count(i * tm, ext_rows)
        shift = 1
        for g in range(len(POOL_WINDOWS)):
            lanes = slice(POOL_GROUP * g, POOL_GROUP * (g + 1))
            level = ext[:, lanes] / counts[g]
            s = 1
            while s <= shift:
                level = level + pltpu.roll(level, ext_rows - s, 0)
                s *= 2
            shift *= 2
            dpin_ref[:, lanes] = (level[:tm] - dd_t[:, lanes]).astype(BF16)

    halo = tm // POOL_HALO
    return pl.pallas_call(
        body, name="pool_bwd_window", grid=(n_tiles,),
        in_specs=[_row(tm, POOL_WIDTH),
                  pl.BlockSpec((POOL_HALO, POOL_WIDTH), lambda i: (jnp.minimum((i + 1) * halo, seq // POOL_HALO - 1), 0))],
        out_specs=_row(tm, POOL_WIDTH),
        out_shape=_sds((seq, POOL_WIDTH), BF16),
        compiler_params=_params(1, 32),
    )(dd, dd)


def _col_to_row(col, n):
    return jnp.transpose(jnp.broadcast_to(col, (n, HEAD_PAD)))[0:1, :]


def _attn_fwd(q, k, v):
    heads, seq, _ = q.shape
    r, n = FWD_ROWS, FWD_CHAINS
    block = r * n

    def body(q_ref, k_ref, v_ref, o_ref, lse_ref):
        qi = pl.program_id(1)
        q_tiles = [q_ref[c * r:(c + 1) * r, :] for c in range(n)]

        def tile(qt, j, m, acc, diagonal):
            start = pl.multiple_of(j * r, r)
            s = _mm_nt(qt, k_ref[pl.ds(start, r), :])
            if diagonal:
                row = lax.broadcasted_iota(jnp.int32, (r, r), 0)
                col = lax.broadcasted_iota(jnp.int32, (r, r), 1)
                s = jnp.where(col <= row, s, -jnp.inf)
            m_new = jnp.maximum(m, jnp.max(s, axis=1, keepdims=True))
            p = jnp.exp2((s - m_new).astype(BF16))
            acc = jnp.exp2(m - m_new) * acc + _mm(p, v_ref[pl.ds(start, r), :])
            return m_new, acc

        def all_chains(jj, carry):
            for u in range(n):
                carry = tuple(tile(q_tiles[c], n * jj + u, *carry[c], False) for c in range(n))
            return carry

        init = tuple((jnp.full((r, 1), -jnp.inf, F32), jnp.zeros((r, HEAD_PAD), F32)) for _ in range(n))
        state = list(lax.fori_loop(0, qi, all_chains, init))
        for d in range(n):
            for c in range(d, n):
                state[c] = tile(q_tiles[c], n * qi + d, *state[c], c == d)
        for c, (m, acc) in enumerate(state):
            l = acc[:, V_HEAD:V_HEAD + 1]
            o_ref[c * r:(c + 1) * r, :] = (acc / l).astype(BF16)
            row0 = c * r
            lse_ref[row0 // ATTN_TILE, :, row0 % ATTN_TILE:row0 % ATTN_TILE + r] = _col_to_row(m + jnp.log2(l), r)

    return pl.pallas_call(
        body, name="attn_fwd", grid=(heads, seq // block),
        in_specs=[pl.BlockSpec((None, block, HEAD_PAD), lambda h, i: (h, i, 0)),
                  pl.BlockSpec((None, seq, HEAD_PAD), lambda h, i: (h, 0, 0)),
                  pl.BlockSpec((None, seq, HEAD_PAD), lambda h, i: (h, 0, 0))],
        out_specs=[pl.BlockSpec((None, block, HEAD_PAD), lambda h, i: (h, i, 0)),
                   pl.BlockSpec((None, block // ATTN_TILE, 1, ATTN_TILE), lambda h, i: (h, i, 0, 0))],
        out_shape=[_sds((heads, seq, HEAD_PAD), BF16), _sds((heads, seq // ATTN_TILE, 1, ATTN_TILE), F32)],
        compiler_params=_params(2, 48),
    )(q, k, v)


def _attn_delta(o, do):
    heads, seq, _ = o.shape
    tq = ATTN_TILE
    nq = seq // tq

    per_step = min(8, nq)

    def body(o_ref, do_ref, delta_ref):
        for u in range(per_step):
            rows = slice(u * tq, (u + 1) * tq)
            prod = o_ref[rows, :].astype(F32) * do_ref[rows, :].astype(F32)
            delta_ref[u] = _col_to_row(jnp.sum(prod, axis=1, keepdims=True), tq)

    tile = pl.BlockSpec((None, per_step * tq, HEAD_PAD), lambda h, i: (h, i, 0))
    return pl.pallas_call(
        body, name="attn_delta", grid=(heads, nq // per_step),
        in_specs=[tile, tile],
        out_specs=pl.BlockSpec((None, per_step, 1, tq), lambda h, i: (h, i, 0, 0)),
        out_shape=_sds((heads, nq, 1, tq), F32),
        compiler_params=_params(2, 32),
    )(o, do)


def _peer_copies(src_refs, dst_refs, send_sems, recv_sems):
    x, y, c = _position()
    copies = []
    for w, (src, dst) in enumerate(zip(src_refs, dst_refs)):
        for r in range(1, N_DEV):
            px = 1 - x if r & 4 else x
            py = 1 - y if r & 2 else y
            pc = 1 - c if r & 1 else c
            copies.append(pltpu.make_async_remote_copy(
                src_ref=src.at[4 * px + 2 * py + pc], dst_ref=dst.at[r - 1],
                send_sem=send_sems.at[(N_DEV - 1) * w + r - 1], recv_sem=recv_sems.at[(N_DEV - 1) * w + r - 1],
                device_id=(px, py, pc), device_id_type=MESH))
    return copies


def _attn_bwd(q, k, k_t, v, do, lse, delta, early_grads):
    heads, seq, _ = q.shape
    t = ATTN_TILE
    nq = seq // t
    n = min(BWD_CHAINS, nq)
    n_w = len(early_grads)

    def body(q_ref, k_ref, kt_ref, v_ref, do_ref, lse_ref, delta_ref, *rest):
        grad_refs, rest = rest[:n_w], rest[n_w:]
        dq_ref, dk_ref, dv_ref = rest[:3]
        recv_refs, (send_sems, recv_sems) = rest[3:3 + n_w], rest[3 + n_w:]
        jp = pl.program_id(1)
        head = pl.program_id(0)

        @pl.when((head == 0) & (jp == 0))
        def _():
            for cp in _peer_copies(grad_refs, recv_refs, send_sems, recv_sems):
                cp.start()

        @pl.when(jp == 0)
        def _():
            dq_ref[...] = jnp.zeros_like(dq_ref)

        keys = [k_ref[c * t:(c + 1) * t, :] for c in range(n)]
        values = [v_ref[c * t:(c + 1) * t, :] for c in range(n)]

        def tile(c, i, dk, dv, diagonal):
            start = pl.multiple_of(i * t, t)
            qt = q_ref[pl.ds(start, t), :]
            dot = do_ref[pl.ds(start, t), :]
            p_t = jnp.exp2(_mm_nt(keys[c], qt) - lse_ref[i])
            if diagonal:
                key = lax.broadcasted_iota(jnp.int32, (t, t), 0)
                query = lax.broadcasted_iota(jnp.int32, (t, t), 1)
                p_t = jnp.where(key <= query, p_t, 0.0)
            dv = dv + _mm(p_t.astype(BF16), dot)
            ds_t = (p_t * (_mm_nt(values[c], dot) - delta_ref[i])).astype(BF16)
            dk = dk + _mm(ds_t, qt)
            return dk, dv, _mm(kt_ref[c], ds_t)

        def query_tile(i, state, first_rows):
            dq = None
            for c in range(n if first_rows is None else first_rows + 1):
                dk, dv, dq_c = tile(c, i, *state[c], diagonal=(c == first_rows))
                state[c] = (dk, dv)
                dq = dq_c if dq is None else dq + dq_c
            dq_ref[i] += dq

        def passes(ip, carry):
            state = list(carry)
            for u in range(BWD_QUERIES):
                query_tile(BWD_QUERIES * ip + u, state, None)
            return tuple(state)

        zero = jnp.zeros((t, HEAD_PAD), F32)
        state = [(zero, zero)] * n
        for offset in range(n):
            query_tile(n * jp + offset, state, offset)
        first_pass = (n * (jp + 1)) // BWD_QUERIES
        state = lax.fori_loop(first_pass, nq // BWD_QUERIES, passes, tuple(state))
        for c, (dk, dv) in enumerate(state):
            dk_ref[c * t:(c + 1) * t, :] = (dk * (1.0 / LOG2E)).astype(BF16)
            dv_ref[c * t:(c + 1) * t, :] = dv.astype(BF16)

        @pl.when((head == heads - 1) & (jp == nq // n - 1))
        def _():
            copies = _peer_copies(grad_refs, recv_refs, send_sems, recv_sems)
            for cp in copies:
                cp.wait_recv()
            for cp in copies:
                cp.wait_send()

    whole = pl.BlockSpec((None, seq, HEAD_PAD), lambda h, j: (h, 0, 0))
    whole_t = pl.BlockSpec((None, nq, HEAD_PAD, t), lambda h, j: (h, 0, 0, 0))
    pair = pl.BlockSpec((None, n * t, HEAD_PAD), lambda h, j: (h, j, 0))
    pair_t = pl.BlockSpec((None, n, HEAD_PAD, t), lambda h, j: (h, j, 0, 0))
    stats = pl.BlockSpec((None, nq, 1, t), lambda h, j: (h, 0, 0, 0))
    return pl.pallas_call(
        body, name="attn_bwd", grid=(heads, nq // n),
        in_specs=[whole, pair, pair_t, pair, whole, stats, stats] + [ANY] * n_w,
        out_specs=[whole_t, pair, pair] + [ANY] * n_w,
        out_shape=[_sds((heads, nq, HEAD_PAD, t), F32), _sds((heads, seq, HEAD_PAD), BF16),
                   _sds((heads, seq, HEAD_PAD), BF16)]
                  + [_sds((N_DEV - 1,) + g.shape[1:], g.dtype) for g in early_grads],
        scratch_shapes=[pltpu.SemaphoreType.DMA(((N_DEV - 1) * n_w,)), pltpu.SemaphoreType.DMA(((N_DEV - 1) * n_w,))],
        compiler_params=_params(2, 56),
    )(q, k, k_t, v, do, lse, delta, *early_grads)


def _merge_fwd(attn, pooled, gates, x, g_post_mix, w_ba, w_bb, w_out, tm):
    seq = x.shape[0]

    def body(attn_ref, pooled_ref, gates_ref, x_ref, g_ref, wba_ref, wbb_ref, wout_ref,
             merged_ref, ba_ref, bb_ref, y_ref, h1_ref, attn_rows_ref):
        for rows in _row_chains(tm):
            attn = jnp.concatenate([attn_ref[h, rows, :] for h in range(N_HEADS)], axis=1)
            attn_rows_ref[rows, :] = attn
            ba = _mm(attn, wba_ref[...])
            bb = _mm(pooled_ref[rows, :], wbb_ref[...])
            ba_ref[rows, :] = ba.astype(BF16)
            bb_ref[rows, :] = bb.astype(BF16)
            merged = (gates_ref[rows, :D_MODEL].astype(F32) * ba
                      + gates_ref[rows, D_MODEL:].astype(F32) * bb).astype(BF16)
            merged_ref[rows, :] = merged
            y = _mm(merged, wout_ref[...])
            y_ref[rows, :] = y
            h1_ref[rows, :] = x_ref[rows, :] + y * _rms_r(y) * g_ref[...]

    return pl.pallas_call(
        body, name="merge_fwd", grid=(seq // tm,),
        in_specs=[pl.BlockSpec((N_HEADS, tm, HEAD_PAD), lambda i: (0, i, 0)), _row(tm, POOL_WIDTH),
                  _row(tm, 2 * D_MODEL), _row(tm, D_MODEL),
                  _fix((1, D_MODEL)), _fix(w_ba.shape), _fix(w_bb.shape), _fix(w_out.shape)],
        out_specs=[_row(tm, D_MODEL)] * 5 + [_row(tm, N_HEADS * HEAD_PAD)],
        out_shape=[_sds((seq, D_MODEL), BF16), _sds((seq, D_MODEL), BF16), _sds((seq, D_MODEL), BF16),
                   _sds((seq, D_MODEL), F32), _sds((seq, D_MODEL), F32), _sds((seq, N_HEADS * HEAD_PAD), BF16)],
        compiler_params=_params(1, 48),
    )(attn, pooled, gates, x, g_post_mix, w_ba, w_bb, w_out)


def _tail_fwd(h1, target, p, g_pre_mlp, g_post_mlp, g_ple, w_ff1, w_ff2, w_pe, w_pg, tm):
    seq = h1.shape[0]

    def body(h1_ref, tgt_ref, p_ref, gm_ref, gf_ref, gp_ref, w1_hbm, w2_hbm, wpe_hbm, wpg_hbm,
             m_ref, zr_ref, f_ref, h2b_ref, pb_ref, de_ref, dpre_ref, dh2_ref, loss_ref, dgple_ref,
             w1, w2, wpe, wpg):
        _load_once(_column_blocks(w1_hbm, w1) + ((w2_hbm, w2), (wpe_hbm, wpe), (wpg_hbm, wpg)))

        @pl.when(pl.program_id(0) == 0)
        def _():
            loss_ref[...] = jnp.zeros_like(loss_ref)
            dgple_ref[...] = jnp.zeros_like(dgple_ref)

        h1v = h1_ref[...]
        m = (h1v * _rms_r(h1v) * gm_ref[...]).astype(BF16)
        m_ref[...] = m
        zr = jnp.maximum(_mm(m, w1[...]), 0.0)
        zr_ref[...] = zr.astype(BF16)
        f = _mm((zr * zr).astype(BF16), w2[...])
        f_ref[...] = f
        h2 = h1v + f * _rms_r(f) * gf_ref[...]
        h2b = h2.astype(BF16)
        h2b_ref[...] = h2b
        pb = p_ref[...].astype(BF16)
        pb_ref[...] = pb
        e = _mm(pb, wpe[...])
        pg = _sigmoid(_mm(h2b, wpg[...]))
        t3 = pg * e
        r3 = _rms_r(t3)
        t3hat = t3 * r3
        diff = h2 + t3hat * gp_ref[...] - tgt_ref[...]
        loss_ref[...] += jnp.sum(diff * diff) * (0.5 / D_MODEL)
        dh3 = diff * (1.0 / D_MODEL)
        dgple_ref[...] += _colsum(dh3 * t3hat)
        dt3 = _rms_bwd(t3hat, r3, gp_ref[...], dh3)
        de_ref[...] = (dt3 * pg).astype(BF16)
        dpre = (dt3 * e * pg * (1.0 - pg)).astype(BF16)
        dpre_ref[...] = dpre
        dh2_ref[...] = dh3 + _mm_nt(dpre, wpg[...])

    return pl.pallas_call(
        body, name="tail_fwd", grid=(seq // tm,),
        in_specs=[_row(tm, D_MODEL), _row(tm, D_MODEL), _row(tm, PLE_DIM), _fix((1, D_MODEL)), _fix((1, D_MODEL)),
                  _fix((1, D_MODEL)), ANY, ANY, ANY, ANY],
        out_specs=[_row(tm, D_MODEL), _row(tm, D_FF), _row(tm, D_MODEL), _row(tm, D_MODEL), _row(tm, PLE_DIM),
                   _row(tm, D_MODEL), _row(tm, D_MODEL), _row(tm, D_MODEL), _fix((8, 128)), _fix((1, D_MODEL))],
        out_shape=[_sds((seq, D_MODEL), BF16), _sds((seq, D_FF), BF16), _sds((seq, D_MODEL), F32),
                   _sds((seq, D_MODEL), BF16), _sds((seq, PLE_DIM), BF16), _sds((seq, D_MODEL), BF16),
                   _sds((seq, D_MODEL), BF16), _sds((seq, D_MODEL), F32), _sds((8, 128), F32), _sds((1, D_MODEL), F32)],
        scratch_shapes=[pltpu.VMEM((w_ff1.shape[1], N_DEV * w_ff1.shape[2]), BF16), pltpu.VMEM(w_ff2.shape, BF16),
                        pltpu.VMEM(w_pe.shape, BF16), pltpu.VMEM(w_pg.shape, BF16)],
        compiler_params=_params(1, 56),
    )(h1, target, p, g_pre_mlp, g_post_mlp, g_ple, w_ff1, w_ff2, w_pe, w_pg)


def _mlp_bwd(h1, f, zr, dh2, g_pre_mlp, g_post_mlp, w_ff1, w_ff2, tm):
    seq = h1.shape[0]

    def body(h1_ref, f_ref, zr_ref, dh2_ref, gm_ref, gf_ref, w1_hbm, w2_hbm,
             df_ref, dz_ref, dh1_ref, dgm_ref, dgf_ref, w1, w2):
        _load_once(_column_blocks(w1_hbm, w1) + ((w2_hbm, w2),))

        @pl.when(pl.program_id(0) == 0)
        def _():
            dgm_ref[...] = jnp.zeros_like(dgm_ref)
            dgf_ref[...] = jnp.zeros_like(dgf_ref)

        dh2 = dh2_ref[...]
        fv = f_ref[...]
        rf = _rms_r(fv)
        fhat = fv * rf
        dgf_ref[...] += _colsum(dh2 * fhat)
        df = _rms_bwd(fhat, rf, gf_ref[...], dh2).astype(BF16)
        df_ref[...] = df
        dz = (_mm_nt(df, w2[...]) * (2.0 * zr_ref[...].astype(F32))).astype(BF16)
        dz_ref[...] = dz
        dm = _mm_nt(dz, w1[...])
        h1v = h1_ref[...]
        r1 = _rms_r(h1v)
        h1hat = h1v * r1
        dgm_ref[...] += _colsum(dm * h1hat)
        dh1_ref[...] = dh2 + _rms_bwd(h1hat, r1, gm_ref[...], dm)

    return pl.pallas_call(
        body, name="mlp_bwd", grid=(seq // tm,),
        in_specs=[_row(tm, D_MODEL), _row(tm, D_MODEL), _row(tm, D_FF), _row(tm, D_MODEL),
                  _fix((1, D_MODEL)), _fix((1, D_MODEL)), ANY, ANY],
        out_specs=[_row(tm, D_MODEL), _row(tm, D_FF), _row(tm, D_MODEL), _fix((1, D_MODEL)), _fix((1, D_MODEL))],
        out_shape=[_sds((seq, D_MODEL), BF16), _sds((seq, D_FF), BF16), _sds((seq, D_MODEL), F32),
                   _sds((1, D_MODEL), F32), _sds((1, D_MODEL), F32)],
        scratch_shapes=[pltpu.VMEM((w_ff1.shape[1], N_DEV * w_ff1.shape[2]), BF16), pltpu.VMEM(w_ff2.shape, BF16)],
        compiler_params=_params(1, 56),
    )(h1, f, zr, dh2, g_pre_mlp, g_post_mlp, w_ff1, w_ff2)


def _merge_bwd(dh1, y, gates, ba, bb, d_pool, g_post_mix, pool_scale, w_out, w_ba, w_bb, w_pool_bf, tm):
    seq = dh1.shape[0]

    def body(dh1_ref, y_ref, gates_ref, ba_ref, bb_ref, d_ref, g_ref, ps_ref, wout_ref, wba_ref, wbb_ref, wp_ref,
             dy_ref, dba_ref, dbb_ref, dgpre_ref, dattn_ref, dd_ref, dg_ref, dbg_ref, dps_ref, dwp_ref):
        @pl.when(pl.program_id(0) == 0)
        def _():
            dg_ref[...] = jnp.zeros_like(dg_ref)
            dbg_ref[...] = jnp.zeros_like(dbg_ref)
            dps_ref[...] = jnp.zeros_like(dps_ref)
            dwp_ref[...] = jnp.zeros_like(dwp_ref)

        for rows in _row_chains(tm):
            dh1v = dh1_ref[rows, :]
            yv = y_ref[rows, :]
            r = _rms_r(yv)
            yhat = yv * r
            dg_ref[...] += _colsum(dh1v * yhat)
            dy = _rms_bwd(yhat, r, g_ref[...], dh1v).astype(BF16)
            dy_ref[rows, :] = dy
            dmerged = _mm_nt(dy, wout_ref[...])
            dbranch = []
            for half, branch_ref, dbranch_ref in ((0, ba_ref, dba_ref), (1, bb_ref, dbb_ref)):
                lanes = slice(D_MODEL * half, D_MODEL * (half + 1))
                gate = gates_ref[rows, lanes].astype(F32)
                dpre = dmerged * branch_ref[rows, :].astype(F32) * gate * (1.0 - gate)
                dbg_ref[:, lanes] += _colsum(dpre)
                dgpre_ref[rows, lanes] = dpre.astype(BF16)
                dbranch.append((dmerged * gate).astype(BF16))
                dbranch_ref[rows, :] = dbranch[-1]
            dattn = _mm_nt(dbranch[0], wba_ref[...]).astype(BF16)
            for h in range(N_HEADS):
                dattn_ref[h, rows, :] = dattn[:, HEAD_PAD * h:HEAD_PAD * (h + 1)]
            dpooled = _mm_nt(dbranch[1], wbb_ref[...])
            for g in range(len(POOL_WINDOWS)):
                lanes = slice(POOL_GROUP * g, POOL_GROUP * (g + 1))
                dpl = dpooled[:, lanes]
                d_g = d_ref[rows, lanes]
                dps_ref[:, lanes] += _colsum(dpl * _mm(d_g, wp_ref[g]))
                dyp = (dpl * ps_ref[:, lanes]).astype(BF16)
                dwp_ref[g] += lax.dot_general(d_g, dyp, TN, preferred_element_type=F32)
                dd_ref[rows, lanes] = _mm_nt(dyp, wp_ref[g])

    return pl.pallas_call(
        body, name="merge_bwd", grid=(seq // tm,),
        in_specs=[_row(tm, D_MODEL), _row(tm, D_MODEL), _row(tm, 2 * D_MODEL), _row(tm, D_MODEL), _row(tm, D_MODEL),
                  _row(tm, POOL_WIDTH), _fix((1, D_MODEL)), _fix((1, POOL_WIDTH)),
                  _fix(w_out.shape), _fix(w_ba.shape), _fix(w_bb.shape), _fix(w_pool_bf.shape)],
        out_specs=[_row(tm, D_MODEL), _row(tm, D_MODEL), _row(tm, D_MODEL), _row(tm, 2 * D_MODEL),
                   pl.BlockSpec((N_HEADS, tm, HEAD_PAD), lambda i: (0, i, 0)), _row(tm, POOL_WIDTH),
                   _fix((1, D_MODEL)), _fix((1, 2 * D_MODEL)), _fix((1, POOL_WIDTH)), _fix(w_pool_bf.shape)],
        out_shape=[_sds((seq, D_MODEL), BF16), _sds((seq, D_MODEL), BF16), _sds((seq, D_MODEL), BF16),
                   _sds((seq, 2 * D_MODEL), BF16), _sds((N_HEADS, seq, HEAD_PAD), BF16),
                   _sds((seq, POOL_WIDTH), F32), _sds((1, D_MODEL), F32), _sds((1, 2 * D_MODEL), F32),
                   _sds((1, POOL_WIDTH), F32), _sds(w_pool_bf.shape, F32)],
        compiler_params=_params(1, 48),
    )(dh1, y, gates, ba, bb, d_pool, g_post_mix, pool_scale, w_out, w_ba, w_bb, w_pool_bf)


def _proj_bwd(dq, dk, dv, qd, kvd, x, dh1, dgpre, dpin, cf, s1, s2, g_pre, g_q, g_kv,
              w_in_r, w_uq_r, w_k_exp, w_v, tm):
    seq = x.shape[0]

    def body(dq_ref, dk_ref, dv_ref, qd_ref, kvd_ref, x_ref, dh1_ref, dgpre_ref, dpin_ref, cf_ref, s1_ref, s2_ref,
             gpre_ref, gq_ref, gkv_ref, win_hbm, wuq_hbm, wk_hbm, wv_hbm,
             gx_ref, dproj_ref, dqb_ref, qn_ref, kvn_ref, dkvb_ref, dgpre_acc, dgq_acc, dgkv_acc,
             win, wuq, wk, wv):
        _load_once(((win_hbm, win), (wuq_hbm, wuq), (wk_hbm, wk), (wv_hbm, wv)))

        @pl.when(pl.program_id(0) == 0)
        def _():
            dgpre_acc[...] = jnp.zeros_like(dgpre_acc)
            dgq_acc[...] = jnp.zeros_like(dgq_acc)
            dgkv_acc[...] = jnp.zeros_like(dgkv_acc)

        for rows in _row_chains(tm):
            n_rows = rows.stop - rows.start
            cfv, s1v, s2v = cf_ref[rows, :], s1_ref[rows, :], s2_ref[rows, :]
            ksum = jnp.zeros((n_rows, HEAD_PAD), F32)
            for h in range(N_HEADS):
                lanes = slice(HEAD_PAD * h, HEAD_PAD * (h + 1))
                dqh = jnp.transpose(dq_ref[h, :, rows])
                dqb_ref[rows, lanes] = (_rope_t(dqh, cfv, s1v, s2v) * ATTN_SCALE).astype(BF16)
                dkh = dk_ref[h, rows, :]
                dkvb_ref[rows, lanes] = dkh
                dkvb_ref[rows, slice(K_ALL + lanes.start, K_ALL + lanes.stop)] = dv_ref[h, rows, :]
                ksum = ksum + dkh.astype(F32)
            lane = lax.broadcasted_iota(jnp.int32, (n_rows, HEAD_PAD), 1)
            rope_lanes = (lane >= QK_NOPE) & (lane < QK_NOPE + QK_ROPE)
            dkr = _rope_t(jnp.where(rope_lanes, ksum, 0.0), cfv, s1v, s2v)

            qdv = qd_ref[rows, :]
            rq = _rms_r(qdv)
            qhat = qdv * rq
            qn_ref[rows, :] = (qhat * gq_ref[...]).astype(BF16)
            dqn = _mm_nt(dqb_ref[rows, :], wuq[...])
            dgq_acc[...] += _colsum(dqn * qhat)
            dproj_ref[rows, IN_Q0:IN_KV0] = _rms_bwd(qhat, rq, gq_ref[...], dqn).astype(BF16)

            kvdv = kvd_ref[rows, :]
            rkv = _rms_r(kvdv)
            kvhat = kvdv * rkv
            kvn_ref[rows, :] = (kvhat * gkv_ref[...]).astype(BF16)
            dkvn = _mm_nt(dkvb_ref[rows, :K_ALL], wk[...]) + _mm_nt(dkvb_ref[rows, K_ALL:], wv[...])
            dgkv_acc[...] += _colsum(dkvn * kvhat)
            dproj_ref[rows, IN_KV0:IN_POOL0] = _rms_bwd(kvhat, rkv, gkv_ref[...], dkvn).astype(BF16)

            dproj_ref[rows, IN_POOL0:IN_GATE0] = dpin_ref[rows, :]
            dproj_ref[rows, IN_GATE0:IN_KR0] = dgpre_ref[rows, :]
            dproj_ref[rows, IN_KR0:IN_R] = dkr.astype(BF16)

            da = _mm_nt(dproj_ref[rows, :], win[...])
            xv = x_ref[rows, :]
            r0 = _rms_r(xv)
            xhat = xv * r0
            dgpre_acc[...] += _colsum(da * xhat)
            gx_ref[rows, :] = dh1_ref[rows, :] + _rms_bwd(xhat, r0, gpre_ref[...], da)

    per_tile = ATTN_TILE // tm
    heads = pl.BlockSpec((N_HEADS, tm, HEAD_PAD), lambda i: (0, i, 0))
    heads_t = pl.BlockSpec((N_HEADS, None, HEAD_PAD, tm), lambda i: (0, i // per_tile, 0, i % per_tile))
    return pl.pallas_call(
        body, name="proj_bwd", grid=(seq // tm,),
        in_specs=[heads_t, heads, heads, _row(tm, Q_LORA), _row(tm, KV_LORA), _row(tm, D_MODEL),
                  _row(tm, D_MODEL), _row(tm, 2 * D_MODEL), _row(tm, POOL_WIDTH),
                  _row(tm, HEAD_PAD), _row(tm, HEAD_PAD), _row(tm, HEAD_PAD),
                  _fix((1, D_MODEL)), _fix((1, Q_LORA)), _fix((1, KV_LORA)), ANY, ANY, ANY, ANY],
        out_specs=[_row(tm, D_MODEL), _row(tm, IN_R), _row(tm, N_HEADS * HEAD_PAD), _row(tm, Q_LORA), _row(tm, KV_LORA),
                   _row(tm, 2 * K_ALL),
                   _fix((1, D_MODEL)), _fix((1, Q_LORA)), _fix((1, KV_LORA))],
        out_shape=[_sds((seq, D_MODEL), F32), _sds((seq, IN_R), BF16), _sds((seq, N_HEADS * HEAD_PAD), BF16),
                   _sds((seq, Q_LORA), BF16), _sds((seq, KV_LORA), BF16), _sds((seq, 2 * K_ALL), BF16),
                   _sds((1, D_MODEL), F32), _sds((1, Q_LORA), F32), _sds((1, KV_LORA), F32)],
        scratch_shapes=[pltpu.VMEM(w_in_r.shape, BF16), pltpu.VMEM(w_uq_r.shape, BF16),
                        pltpu.VMEM(w_k_exp.shape, BF16), pltpu.VMEM(w_v.shape, BF16)],
        compiler_params=_params(1, 58),
    )(dq, dk, dv, qd, kvd, x, dh1, dgpre, dpin, cf, s1, s2, g_pre, g_q, g_kv, w_in_r, w_uq_r, w_k_exp, w_v)


def _grad_w(a, b, name, square_a=False, by_device=False, with_bf16=False):
    seq, k_dim = a.shape
    n_dim = b.shape[1]
    tk = min(k_dim, 1024)
    tn = n_dim // 2 if n_dim == IN_R else min(n_dim, 1024)
    ts = min(seq, 2048)
    shard = n_dim // N_DEV
    per_tile = tn // shard
    if by_device:
        out_spec = pl.BlockSpec((per_tile, tk, shard), lambda i, j, s: (j, i, 0))
        out_shape = _sds((N_DEV, k_dim, shard), F32)
    else:
        out_spec = pl.BlockSpec((tk, tn), lambda i, j, s: (i, j))
        out_shape = _sds((k_dim, n_dim), F32)

    n_seq_steps = seq // ts

    def body(a_ref, b_ref, o_ref, *narrow):
        @pl.when(pl.program_id(2) == 0)
        def _():
            o_ref[...] = jnp.zeros_like(o_ref)

        at = a_ref[...]
        if square_a:
            at = at * at
        part = lax.dot_general(at, b_ref[...], TN, preferred_element_type=F32)
        if by_device:
            for d in range(per_tile):
                o_ref[d] += part[:, d * shard:(d + 1) * shard]
        else:
            o_ref[...] += part
        if with_bf16:
            @pl.when(pl.program_id(2) == n_seq_steps - 1)
            def _():
                narrow[0][...] = o_ref[...].astype(BF16)

    return pl.pallas_call(
        body, name=name, grid=(k_dim // tk, n_dim // tn, n_seq_steps),
        in_specs=[pl.BlockSpec((ts, tk), lambda i, j, s: (s, i)), pl.BlockSpec((ts, tn), lambda i, j, s: (s, j))],
        out_specs=[out_spec, out_spec] if with_bf16 else out_spec,
        out_shape=[out_shape, _sds(out_shape.shape, BF16)] if with_bf16 else out_shape,
        compiler_params=_params(3, 48),
    )(a, b)


def _position():
    return lax.axis_index("x"), lax.axis_index("y"), lax.axis_index("c")


def _gather_copies(x_ref, slot, send_sems, recv_sems, local_sem, phases=("send", "forward", "finish"), sem_base=0):
    x, y, c = _position()
    me, sibling = (x, y, c), (x, y, 1 - c)
    chips = [(1 - x, y), (x, 1 - y), (1 - x, 1 - y)]

    def copy(k, block, to, src=None):
        return pltpu.make_async_remote_copy(
            src_ref=slot(*block) if src is None else src, dst_ref=slot(*block),
            send_sem=send_sems.at[sem_base + k], recv_sem=recv_sems.at[sem_base + k],
            device_id=to, device_id_type=MESH)

    mine = pltpu.make_async_copy(x_ref, slot(*me), local_sem)
    first = [copy(0, me, sibling, src=x_ref)]
    first += [copy(1 + j, me, (*chip, c), src=x_ref) for j, chip in enumerate(chips)]
    passed = [copy(4 + j, (*chip, c), sibling) for j, chip in enumerate(chips)]
    if "send" in phases:
        mine.start()
        for cp in first:
            cp.start()
    if "forward" in phases:
        for j, chip in enumerate(chips):
            copy(1 + j, (*chip, c), me).wait_recv()
            passed[j].start()
    if "finish" in phases:
        copy(0, sibling, me).wait_recv()
        for j, chip in enumerate(chips):
            copy(4 + j, (*chip, 1 - c), me).wait_recv()
        for cp in first + passed:
            cp.wait_send()
        mine.wait()


def _all_gather_hbm(block):
    def body(x_ref, out_ref, send_sems, recv_sems, local_sem):
        _gather_copies(x_ref, lambda px, py, pc: out_ref.at[4 * px + 2 * py + pc], send_sems, recv_sems, local_sem)

    return pl.pallas_call(
        body, name="gather_weights",
        in_specs=[ANY], out_specs=ANY,
        out_shape=_sds((N_DEV,) + block.shape, block.dtype),
        scratch_shapes=[pltpu.SemaphoreType.DMA((7,)), pltpu.SemaphoreType.DMA((7,)), pltpu.SemaphoreType.DMA],
    )(block)


def _replicated_update(grads, loss_block, ws, ms, vs):
    n_p = len(grads)
    sent = list(grads) + [loss_block]
    n_a = len(sent)

    def body(*refs):
        g_refs, refs = refs[:n_a], refs[n_a:]
        w_refs, m_refs, v_refs, refs = refs[:n_p], refs[n_p:2 * n_p], refs[2 * n_p:3 * n_p], refs[3 * n_p:]
        sum_refs, refs = refs[:n_a], refs[n_a:]
        d_refs, nm_refs, nv_refs, refs = refs[:n_p], refs[n_p:2 * n_p], refs[2 * n_p:3 * n_p], refs[3 * n_p:]
        bufs, (send_sems, recv_sems, local_sems) = refs[:n_a], refs[n_a:]
        x, y, c = _position()
        me = 4 * x + 2 * y + c
        local, remote = [], []
        for a in range(n_a):
            local.append(pltpu.make_async_copy(g_refs[a], bufs[a].at[me], local_sems.at[a]))
            for r in range(1, N_DEV):
                peer = (1 - x if r & 4 else x, 1 - y if r & 2 else y, 1 - c if r & 1 else c)
                remote.append(pltpu.make_async_remote_copy(
                    src_ref=g_refs[a], dst_ref=bufs[a].at[me],
                    send_sem=send_sems.at[(N_DEV - 1) * a + r - 1], recv_sem=recv_sems.at[(N_DEV - 1) * a + r - 1],
                    device_id=peer, device_id_type=MESH))
        for cp in local + remote:
            cp.start()
        for cp in remote:
            cp.wait_recv()
        for cp in remote:
            cp.wait_send()
        for cp in local:
            cp.wait()
        for a in range(n_a):
            acc = bufs[a][0]
            for d in range(1, N_DEV):
                acc = acc + bufs[a][d]
            if a == n_p:
                sum_refs[a][...] = acc
                continue
            delta, new_m, new_v = _adamw_math(acc, w_refs[a][...], m_refs[a][...], v_refs[a][...])
            sum_refs[a][...], d_refs[a][...], nm_refs[a][...], nv_refs[a][...] = acc, delta, new_m, new_v

    vmem = pl.BlockSpec(memory_space=pltpu.VMEM)
    like_w = [_sds(w.shape, F32) for w in ws]
    n_sem = (N_DEV - 1) * n_a
    outs = pl.pallas_call(
        body, name="replicated_update",
        in_specs=[vmem] * (n_a + 3 * n_p), out_specs=[vmem] * (n_a + 3 * n_p),
        out_shape=like_w + [_sds(loss_block.shape, F32)] + like_w * 3,
        scratch_shapes=[pltpu.VMEM((N_DEV,) + g.shape, F32) for g in sent]
                       + [pltpu.SemaphoreType.DMA((n_sem,)), pltpu.SemaphoreType.DMA((n_sem,)),
                          pltpu.SemaphoreType.DMA((n_a,))],
        compiler_params=pltpu.CompilerParams(vmem_limit_bytes=32 * MIB),
    )(*sent, *ws, *ms, *vs)
    return (outs[:n_p], outs[n_p], outs[n_a:n_a + n_p], outs[n_a + n_p:n_a + 2 * n_p], outs[n_a + 2 * n_p:])


def _exchange_pair(gs):
    n_w = len(gs)

    def body(*refs):
        g_refs, out_refs = refs[:n_w], refs[n_w:2 * n_w]
        send_sems, recv_sems = refs[2 * n_w:]
        x, y, c = _position()
        copies = []
        for w in range(n_w):
            for chip in range(4):
                cp = pltpu.make_async_remote_copy(
                    src_ref=g_refs[w].at[2 * chip + (1 - c)], dst_ref=out_refs[w].at[chip],
                    send_sem=send_sems.at[4 * w + chip], recv_sem=recv_sems.at[4 * w + chip],
                    device_id=(x, y, 1 - c), device_id_type=MESH)
                cp.start()
                copies.append(cp)
        for cp in copies:
            cp.wait_recv()
        for cp in copies:
            cp.wait_send()

    return pl.pallas_call(
        body, name="exchange_pair",
        in_specs=[ANY] * n_w, out_specs=[ANY] * n_w,
        out_shape=[_sds((4,) + g.shape[1:], g.dtype) for g in gs],
        scratch_shapes=[pltpu.SemaphoreType.DMA((4 * n_w,)), pltpu.SemaphoreType.DMA((4 * n_w,))],
    )(*gs)


def _exchange_chips(parts):
    n_w = len(parts)

    def body(*refs):
        p_refs, out_refs = refs[:n_w], refs[n_w:2 * n_w]
        send_sems, recv_sems = refs[2 * n_w:]
        x, y, c = _position()
        chips = [(1 - x, y), (x, 1 - y), (1 - x, 1 - y)]
        copies = []
        for w in range(n_w):
            for k, (px, py) in enumerate(chips):
                cp = pltpu.make_async_remote_copy(
                    src_ref=p_refs[w].at[2 * px + py], dst_ref=out_refs[w].at[k],
                    send_sem=send_sems.at[3 * w + k], recv_sem=recv_sems.at[3 * w + k],
                    device_id=(px, py, c), device_id_type=MESH)
                cp.start()
                copies.append(cp)
        for cp in copies:
            cp.wait_recv()
        for cp in copies:
            cp.wait_send()

    return pl.pallas_call(
        body, name="exchange_chips",
        in_specs=[ANY] * n_w, out_specs=[ANY] * n_w,
        out_shape=[_sds((3,) + p.shape[1:], p.dtype) for p in parts],
        scratch_shapes=[pltpu.SemaphoreType.DMA((3 * n_w,)), pltpu.SemaphoreType.DMA((3 * n_w,))],
    )(*parts)


def _row_tile(k):
    return 256 if k % 256 == 0 else 128


def _pair_sum(g, recv, place, name):
    _, k, n = g.shape
    tr = _row_tile(k)
    g4 = g.reshape(4, 2, k, n)

    def body(s_ref, g_ref, r_ref, o_ref):
        o_ref[...] = (g_ref[...] + r_ref[...]).astype(BF16)

    spec = pltpu.PrefetchScalarGridSpec(
        num_scalar_prefetch=1, grid=(4, k // tr),
        in_specs=[pl.BlockSpec((None, None, tr, n), lambda j, i, s: (j, s[2], i, 0)),
                  pl.BlockSpec((None, tr, n), lambda j, i, s: (j, i, 0))],
        out_specs=pl.BlockSpec((None, tr, n), lambda j, i, s: (j, i, 0)))
    return pl.pallas_call(
        body, name=name, grid_spec=spec, out_shape=_sds((4, k, n), BF16),
        compiler_params=_params(2, 32),
    )(place, g4, recv)


def _adamw_math(g, w, m, v):
    m = ADAM_B1 * m + (1.0 - ADAM_B1) * g
    v = ADAM_B2 * v + (1.0 - ADAM_B2) * (g * g)
    m_hat = m / (1.0 - ADAM_B1 ** ADAM_STEP)
    v_hat = v / (1.0 - ADAM_B2 ** ADAM_STEP)
    delta = -ADAM_LR * (m_hat / (jnp.sqrt(v_hat) + ADAM_EPS) + ADAM_WD * w)
    return delta, m, v


def _adamw_sharded(g, from_sibling, from_chips, place, w, m, v, name):
    _, k, n = g.shape
    tr = _row_tile(k)

    def body(s_ref, g_ref, sib_ref, r0_ref, r1_ref, r2_ref, w_ref, m_ref, v_ref, grad_ref, d_ref, nm_ref, nv_ref):
        grad = g_ref[...] + sib_ref[...]
        for r_ref in (r0_ref, r1_ref, r2_ref):
            grad = grad + r_ref[...].astype(F32)
        grad_ref[...] = grad
        d_ref[...], nm_ref[...], nv_ref[...] = _adamw_math(grad, w_ref[...], m_ref[...], v_ref[...])

    tile = pl.BlockSpec((None, tr, n), lambda i, s: (0, i, 0))

    def slot(j):
        return pl.BlockSpec((None, tr, n), lambda i, s: (j, i, 0))

    spec = pltpu.PrefetchScalarGridSpec(
        num_scalar_prefetch=1, grid=(k // tr,),
        in_specs=[pl.BlockSpec((None, tr, n), lambda i, s: (s[0], i, 0)),
                  pl.BlockSpec((None, tr, n), lambda i, s: (s[1], i, 0)),
                  slot(0), slot(1), slot(2), tile, tile, tile],
        out_specs=[tile] * 4)
    return pl.pallas_call(
        body, name=name, grid_spec=spec, out_shape=[_sds((1, k, n), F32)] * 4,
        compiler_params=_params(1, 48),
    )(place, g, from_sibling, from_chips, from_chips, from_chips, w, m, v)


def _adamw_direct(g, received, place, w, m, v, name):
    _, k, n = g.shape
    tr = _row_tile(k)

    def body(s_ref, g_ref, r_ref, w_ref, m_ref, v_ref, grad_ref, d_ref, nm_ref, nv_ref):
        grad = g_ref[...]
        for r in range(N_DEV - 1):
            grad = grad + r_ref[r].astype(F32)
        grad_ref[...] = grad
        d_ref[...], nm_ref[...], nv_ref[...] = _adamw_math(grad, w_ref[...], m_ref[...], v_ref[...])

    tile = pl.BlockSpec((None, tr, n), lambda i, s: (0, i, 0))
    spec = pltpu.PrefetchScalarGridSpec(
        num_scalar_prefetch=1, grid=(k // tr,),
        in_specs=[pl.BlockSpec((None, tr, n), lambda i, s: (s[0], i, 0)),
                  pl.BlockSpec((N_DEV - 1, tr, n), lambda i, s: (0, i, 0)), tile, tile, tile],
        out_specs=[tile] * 4)
    return pl.pallas_call(
        body, name=name, grid_spec=spec, out_shape=[_sds((1, k, n), F32)] * 4,
        compiler_params=_params(1, 48),
    )(place, g, received, w, m, v)


def _pack_rows(parts):
    parts = [a.reshape(-1, LANES) for a in parts]
    pad = (-sum(a.shape[0] for a in parts)) % PACK_ROW_TILE
    return jnp.concatenate(parts + [jnp.zeros((pad, LANES), parts[0].dtype)], axis=0)


def _full_from_gathered(gathered, entries, shard_shapes):
    out, off = {}, 0
    for (name, kind), (k, n) in zip(entries, shard_shapes):
        rows = k * n // LANES
        seg = gathered[:, off:off + rows].reshape(N_DEV, k, n)
        out[name] = jnp.transpose(seg, (1, 0, 2)).reshape(k, N_DEV * n) if kind == "col" else seg.reshape(N_DEV * k, n)
        off += rows
    return out


def _columns_by_device(a):
    k, n_all = a.shape
    return jnp.transpose(a.reshape(k, N_DEV, n_all // N_DEV), (1, 0, 2))


def _rows_by_device(a):
    k_all, n = a.shape
    return a.reshape(N_DEV, k_all // N_DEV, n)


def _rope_lane_frequencies():
    inv_freq = ROPE_THETA ** (-jnp.arange(0, QK_ROPE, 2, dtype=F32) / QK_ROPE)
    zeros = lambda n: jnp.zeros((n,), F32)
    return jnp.concatenate([zeros(QK_NOPE), inv_freq, inv_freq, zeros(HEAD_PAD - QK_NOPE - QK_ROPE)])[None, :]


def _rope_tables(pos_row, freq, tm):
    pos = jnp.transpose(jnp.broadcast_to(pos_row.astype(F32), (HEAD_PAD, tm)))
    ang = pos * freq
    cos, sin = jnp.cos(ang), jnp.sin(ang)
    lane = lax.broadcasted_iota(jnp.int32, (tm, HEAD_PAD), 1)
    first = (lane >= QK_NOPE) & (lane < QK_NOPE + QK_ROPE // 2)
    second = (lane >= QK_NOPE + QK_ROPE // 2) & (lane < QK_NOPE + QK_ROPE)
    cf = jnp.where(lane < QK_NOPE, 1.0, jnp.where(first | second, cos, 0.0))
    return cf, jnp.where(first, -sin, 0.0), jnp.where(second, sin, 0.0)


def _arrange_w_in(w):
    k = w.shape[0]
    zeros = lambda n: jnp.zeros((k, n), w.dtype)
    kr0 = Q_LORA + KV_LORA
    pool0 = kr0 + QK_ROPE
    return jnp.concatenate([w[:, :kr0], w[:, pool0:], zeros(QK_NOPE), w[:, kr0:pool0],
                            zeros(HEAD_PAD - QK_NOPE - QK_ROPE)], axis=1)


def _restore_w_in(d):
    kr = d[:, IN_KR0 + QK_NOPE:IN_KR0 + QK_NOPE + QK_ROPE]
    return jnp.concatenate([d[:, :IN_POOL0], kr, d[:, IN_POOL0:IN_KR0]], axis=1)


def _pad_heads(w, width):
    k = w.shape[0]
    w = w.reshape(k, N_HEADS, width)
    return jnp.pad(w, ((0, 0), (0, 0), (0, HEAD_PAD - width))).reshape(k, N_HEADS * HEAD_PAD)


def _unpad_heads(d, width):
    k = d.shape[0]
    return d.reshape(k, N_HEADS, HEAD_PAD)[:, :, :width]


def kernel(x, p, positions, g_pre_mix, w_in, b_gate, g_q, w_uq, g_kv, w_ukv, w_pool, pool_scale, w_branch_attn, w_branch_pool, w_out, g_post_mix, g_pre_mlp, w_ff1, w_ff2, g_post_mlp, w_ple_proj, w_ple_gate, g_ple, loss_target, m_g_pre_mix, m_w_in, m_b_gate, m_g_q, m_w_uq, m_g_kv, m_w_ukv, m_w_pool, m_pool_scale, m_w_branch_attn, m_w_branch_pool, m_w_out, m_g_post_mix, m_g_pre_mlp, m_w_ff1, m_w_ff2, m_g_post_mlp, m_w_ple_proj, m_w_ple_gate, m_g_ple, v_g_pre_mix, v_w_in, v_b_gate, v_g_q, v_w_uq, v_g_kv, v_w_ukv, v_w_pool, v_pool_scale, v_w_branch_attn, v_w_branch_pool, v_w_out, v_g_post_mix, v_g_pre_mlp, v_w_ff1, v_w_ff2, v_g_post_mlp, v_w_ple_proj, v_w_ple_gate, v_g_ple):
    given = dict(locals())
    weights = {n: given[n] for n in WEIGHT_ORDER}
    moments_m = {n: given["m_" + n] for n in WEIGHT_ORDER}
    moments_v = {n: given["v_" + n] for n in WEIGHT_ORDER}
    xs, ps, target = x[0], p[0, 0], loss_target[0]
    seq = xs.shape[0]
    tm = min(256, seq)
    tm_merge = min(512, seq)
    core = lax.axis_index("c")
    chip = 2 * lax.axis_index("x") + lax.axis_index("y")

    early, later = SHARDED[:N_EARLY], SHARDED[N_EARLY:]
    shapes_of = lambda entries: [weights[n].shape[1:] for n, _ in entries]
    pack_bf16 = lambda entries: _pack_rows([weights[n][0].astype(BF16) for n, _ in entries])
    full = _full_from_gathered(_all_gather_hbm(pack_bf16(early)), early, shapes_of(early))
    w_in_r = _arrange_w_in(full["w_in"])
    w_uq_r = _pad_heads(full["w_uq"], QK_NOPE + QK_ROPE)
    ukv = full["w_ukv"].reshape(KV_LORA, N_HEADS, QK_NOPE + V_HEAD)
    w_k_exp = _pad_heads(ukv[:, :, :QK_NOPE].reshape(KV_LORA, N_HEADS * QK_NOPE), QK_NOPE)
    w_v = _pad_heads(ukv[:, :, QK_NOPE:].reshape(KV_LORA, N_HEADS * V_HEAD), V_HEAD)
    w_pool_bf = w_pool[0].astype(BF16)

    packed_later = [e for e in later if e[0] != "w_ff1"]
    a_bf, qd, kvd, pin, gates, q, k, v, k_t, cf, s1, s2, gathered_later, gathered_ff1 = _proj_fwd(
        xs, g_pre_mix, b_gate, g_q, g_kv, positions, w_in_r, w_uq_r, w_k_exp, w_v,
        [pack_bf16(packed_later), w_ff1[0].astype(BF16)], tm_merge)
    full.update(_full_from_gathered(gathered_later, packed_later, shapes_of(packed_later)))
    full["w_ff1"] = gathered_ff1
    w_ba = jnp.pad(full["w_branch_attn"].reshape(N_HEADS, V_HEAD, D_MODEL),
                   ((0, 0), (0, HEAD_PAD - V_HEAD), (0, 0))).reshape(N_HEADS * HEAD_PAD, D_MODEL)
    d_pool, pooled = _pool_fwd(pin, w_pool_bf, pool_scale, tm_merge)
    o_heads, lse = _attn_fwd(q, k, v)
    merged, ba, bb, y, h1, attn_rows = _merge_fwd(o_heads, pooled, gates, xs, g_post_mix, w_ba,
                                                  full["w_branch_pool"], full["w_out"], tm_merge)
    (m_bf, zr, f, h2_bf, p_bf, de, dpre, dh2, loss_acc, dg_ple) = _tail_fwd(
        h1, target, ps, g_pre_mlp, g_post_mlp, g_ple, full["w_ff1"], full["w_ff2"], full["w_ple_proj"],
        full["w_ple_gate"], tm)

    by_device, payload = {}, {}

    def keep(name, pair, layout=lambda g: g):
        by_device[name], payload[name] = layout(pair[0]), layout(pair[1])

    keep("w_ple_proj", _grad_w(p_bf, de, "grad_w_ple_proj", by_device=True, with_bf16=True))
    keep("w_ple_gate", _grad_w(h2_bf, dpre, "grad_w_ple_gate", with_bf16=True), _rows_by_device)
    df, dz, dh1, dg_pre_mlp, dg_post_mlp = _mlp_bwd(h1, f, zr, dh2, g_pre_mlp, g_post_mlp, full["w_ff1"],
                                                    full["w_ff2"], tm)
    keep("w_ff1", _grad_w(m_bf, dz, "grad_w_ff1", by_device=True, with_bf16=True))
    keep("w_ff2", _grad_w(zr, df, "grad_w_ff2", square_a=True, with_bf16=True), _rows_by_device)
    (dy, dba, dbb, dgpre, do_heads, dd, dg_post_mix, db_gate, dpool_scale, dw_pool) = _merge_bwd(
        dh1, y, gates, ba, bb, d_pool, g_post_mix, pool_scale, full["w_out"], w_ba,
        full["w_branch_pool"], w_pool_bf, tm_merge)
    keep("w_branch_attn", _grad_w(attn_rows, dba, "grad_w_branch_attn", with_bf16=True),
          lambda g: _columns_by_device(g.reshape(N_HEADS, HEAD_PAD, D_MODEL)[:, :V_HEAD].reshape(-1, D_MODEL)))
    keep("w_branch_pool", _grad_w(pooled, dbb, "grad_w_branch_pool", by_device=True, with_bf16=True))
    keep("w_out", _grad_w(merged, dy, "grad_w_out", with_bf16=True), _rows_by_device)
    dpin = _pool_bwd_window(dd, tm_merge)
    delta = _attn_delta(o_heads, do_heads)
    direct = [n for n, _ in SHARDED[N_EARLY:]]
    outs = _attn_bwd(q, k, k_t, v, do_heads, lse, delta, [payload[n] for n in direct])
    dq, dk, dv = outs[:3]
    received = dict(zip(direct, outs[3:]))
    (grad_x, dproj, dq_bf, qn_bf, kvn_bf, dkv_bf, dg_pre_mix, dg_q, dg_kv) = _proj_bwd(
        dq, dk, dv, qd, kvd, xs, dh1, dgpre, dpin, cf, s1, s2, g_pre_mix, g_q, g_kv, w_in_r, w_uq_r, w_k_exp, w_v,
        tm_merge)
    d_w_kv = _grad_w(kvn_bf, dkv_bf, "grad_w_ukv")
    d_k_exp = _unpad_heads(d_w_kv[:, :K_ALL], QK_NOPE)
    d_w_v = _unpad_heads(d_w_kv[:, K_ALL:], V_HEAD)
    by_device["w_in"] = _columns_by_device(_restore_w_in(_grad_w(a_bf, dproj, "grad_w_in")))
    by_device["w_uq"] = _columns_by_device(
        _unpad_heads(_grad_w(qn_bf, dq_bf, "grad_w_uq"), QK_NOPE + QK_ROPE).reshape(Q_LORA, -1))
    by_device["w_ukv"] = _columns_by_device(jnp.concatenate([d_k_exp, d_w_v], axis=2).reshape(KV_LORA, -1))
    grads_small = {
        "g_pre_mix": dg_pre_mix, "b_gate": db_gate, "g_q": dg_q, "g_kv": dg_kv,
        "w_pool": dw_pool, "pool_scale": dpool_scale, "g_post_mix": dg_post_mix,
        "g_pre_mlp": dg_pre_mlp, "g_post_mlp": dg_post_mlp, "g_ple": dg_ple,
    }

    names = [n for n, _ in SHARDED]
    place = jnp.stack([2 * chip + core, chip, core]).astype(jnp.int32)
    sharded = {n: _adamw_direct(by_device[n], received[n], place, weights[n], moments_m[n], moments_v[n],
                                "adamw_" + n) for n in direct}
    last = [n for n, _ in SHARDED[:N_EARLY]]
    own = [by_device[n] for n in last]
    from_sibling = _exchange_pair(own)
    pair = [_pair_sum(g, r, place, "pair_sum_" + n) for n, g, r in zip(last, own, from_sibling)]
    from_chips = _exchange_chips(pair)
    sharded.update({n: _adamw_sharded(g, r, rc, place, weights[n], moments_m[n], moments_v[n], "adamw_" + n)
                    for n, g, r, rc in zip(last, own, from_sibling, from_chips)})

    flat = lambda a: a.reshape(a.shape[-3:]) if a.ndim > 3 else a
    g_sm, loss_sum, d_sm, m_sm, v_sm = _replicated_update(
        [flat(grads_small[n]) for n in REPLICATED], loss_acc, [flat(weights[n]) for n in REPLICATED],
        [flat(moments_m[n]) for n in REPLICATED], [flat(moments_v[n]) for n in REPLICATED])

    results = []
    for which, small in enumerate((g_sm, d_sm, m_sm, v_sm)):
        named = {n: sharded[n][which] for n in names}
        named.update({n: a.reshape(weights[n].shape) for n, a in zip(REPLICATED, small)})
        results.append([named[n] for n in WEIGHT_ORDER])

    return (loss_sum[0, 0], grad_x[None], *results[0], *results[1], *results[2], *results[3])
```

```python
import jax
import jax.numpy as jnp
from jax import lax
from jax.experimental import pallas as pl
from jax.experimental.pallas import tpu as pltpu

F32 = jnp.float32
BF16 = jnp.bfloat16

D_MODEL = 1024
PLE_DIM = 256
N_HEADS = 8
QK_NOPE = 64
QK_ROPE = 32
V_HEAD = 64
Q_LORA = 384
KV_LORA = 256
POOL_WINDOWS = (2, 4, 8, 16)
POOL_GROUP = 128
POOL_WIDTH = 512
D_FF = 4096
ROPE_THETA = 10000.0
EPS = 1e-6
HEAD_PAD = 128
K_ALL = N_HEADS * HEAD_PAD
ATTN_SCALE = (QK_NOPE + QK_ROPE) ** -0.5
LOG2E = 1.4426950408889634
Q_PRESCALE = ATTN_SCALE * LOG2E
ATTN_TILE = 512
FWD_ROWS = 512
FWD_CHAINS = 4
BWD_CHAINS = 2
BWD_QUERIES = 2

ADAM_LR = 0.001
ADAM_B1 = 0.9
ADAM_B2 = 0.999
ADAM_EPS = 1e-08
ADAM_WD = 0.01
ADAM_STEP = 10

N_DEV = 8
LANES = 1024
PACK_ROW_TILE = 480
POOL_HALO = 16
MIB = 2 ** 20

IN_Q0, IN_KV0, IN_POOL0, IN_GATE0, IN_KR0, IN_R = 0, 384, 640, 1152, 3200, 3328

SHARDED = (("w_in", "col"), ("w_uq", "col"), ("w_ukv", "col"), ("w_branch_attn", "col"),
           ("w_branch_pool", "col"), ("w_out", "row"), ("w_ff1", "col"), ("w_ff2", "row"),
           ("w_ple_proj", "col"), ("w_ple_gate", "row"))
N_EARLY = 3
REPLICATED = ("g_pre_mix", "b_gate", "g_q", "g_kv", "w_pool", "pool_scale", "g_post_mix",
              "g_pre_mlp", "g_post_mlp", "g_ple")
WEIGHT_ORDER = ("g_pre_mix", "w_in", "b_gate", "g_q", "w_uq", "g_kv", "w_ukv", "w_pool", "pool_scale",
                "w_branch_attn", "w_branch_pool", "w_out", "g_post_mix", "g_pre_mlp", "w_ff1", "w_ff2",
                "g_post_mlp", "w_ple_proj", "w_ple_gate", "g_ple")

NT = (((1,), (1,)), ((), ()))
TN = (((0,), (0,)), ((), ()))
MESH = pl.DeviceIdType.MESH
ANY = pl.BlockSpec(memory_space=pl.ANY)


def _params(n_axes, vmem_mib):
    return pltpu.CompilerParams(dimension_semantics=("arbitrary",) * n_axes, vmem_limit_bytes=vmem_mib * MIB)


def _row(tm, n):
    return pl.BlockSpec((tm, n), lambda i: (i, 0))


def _fix(shape):
    zeros = (0,) * len(shape)
    return pl.BlockSpec(shape, lambda i: zeros)


def _sds(shape, dtype):
    return jax.ShapeDtypeStruct(shape, dtype)


def _rms_r(v):
    return lax.rsqrt(jnp.mean(v * v, axis=-1, keepdims=True) + EPS)


def _rms_bwd(vhat, r, g, dy):
    gdy = dy * g
    return r * (gdy - vhat * jnp.mean(gdy * vhat, axis=-1, keepdims=True))


def _colsum(v):
    return jnp.sum(v, axis=0, keepdims=True)


def _sigmoid(v):
    return 1.0 / (1.0 + jnp.exp(-v))


def _mm(a, b):
    return jnp.dot(a, b, preferred_element_type=F32)


def _mm_nt(a, b):
    return lax.dot_general(a, b, NT, preferred_element_type=F32)


def _rope(c, cf, s1, s2):
    return c * cf + pltpu.roll(c, HEAD_PAD - 16, 1) * s1 + pltpu.roll(c, 16, 1) * s2


def _rope_t(c, cf, s1, s2):
    return c * cf + pltpu.roll(c * s1, 16, 1) + pltpu.roll(c * s2, HEAD_PAD - 16, 1)


def _row_chains(tm, rows=256):
    rows = min(rows, tm)
    return [slice(c * rows, (c + 1) * rows) for c in range(tm // rows)]


def _load_once(pairs):
    @pl.when(pl.program_id(0) == 0)
    def _():
        for src, dst in pairs:
            pltpu.sync_copy(src, dst)


def _column_blocks(by_device_hbm, full_vmem):
    n = by_device_hbm.shape[2]
    return tuple((by_device_hbm.at[d], full_vmem.at[:, d * n:(d + 1) * n]) for d in range(N_DEV))


def _proj_fwd(x, g_pre, b_gate, g_q, g_kv, positions, w_pool_bf, pool_scale, w_in_r, w_uq_r, w_k_exp, w_v,
              later_shards, tm):
    seq = x.shape[0]
    n_steps = seq // tm
    forward_step = (3 * n_steps) // 4
    n_later = len(later_shards)

    def body(x_ref, gpre_ref, bg_ref, gq_ref, gkv_ref, pos_ref, freq_ref, wpool_ref, pscale_ref,
             win_hbm, wuq_hbm, wk_hbm, wv_hbm, *rest):
        later_refs, rest = rest[:n_later], rest[n_later:]
        (a_ref, qd_ref, kvd_ref, dpool_ref, pooled_ref, gates_ref, q_ref, k_ref, v_ref, kt_ref,
         cf_ref, s1_ref, s2_ref) = rest[:13]
        gathered_refs, rest = rest[13:13 + n_later], rest[13 + n_later:]
        win, wuq, wk, wv, halo_ref, send_sems, recv_sems, local_sems = rest
        step = pl.program_id(0)

        def gather(phase):
            for a, (src, dst) in enumerate(zip(later_refs, gathered_refs)):
                _gather_copies(src, lambda px, py, pc, dst=dst: dst.at[4 * px + 2 * py + pc],
                               send_sems, recv_sems, local_sems.at[a], phases=(phase,), sem_base=7 * a)

        pl.when(step == 0)(lambda: gather("send"))
        pl.when(step == forward_step)(lambda: gather("forward"))
        _load_once(((win_hbm, win), (wuq_hbm, wuq), (wk_hbm, wk), (wv_hbm, wv)))
        for rows in _row_chains(tm):
            n_rows = rows.stop - rows.start
            xv = x_ref[rows, :]
            a = (xv * _rms_r(xv) * gpre_ref[...]).astype(BF16)
            a_ref[rows, :] = a
            proj = _mm(a, win[...])
            qd = proj[:, IN_Q0:IN_KV0]
            kvd = proj[:, IN_KV0:IN_POOL0]
            qd_ref[rows, :] = qd
            kvd_ref[rows, :] = kvd
            gates_ref[rows, :] = _sigmoid(proj[:, IN_GATE0:IN_KR0] + bg_ref[...]).astype(BF16)

            u = proj[:, IN_POOL0:IN_GATE0]
            before = jnp.where(step == 0, 0.0, halo_ref[...]) if rows.start == 0 else tail
            tail = u[n_rows - POOL_HALO:, :]
            level = jnp.concatenate([before, u], axis=0)
            counts = _window_count(step * tm + rows.start, n_rows)
            shift = 1
            for g in range(len(POOL_WINDOWS)):
                level = level + pltpu.roll(level, shift, 0)
                shift *= 2
                lanes = slice(POOL_GROUP * g, POOL_GROUP * (g + 1))
                d = (level[POOL_HALO:, lanes] / counts[g] - u[:, lanes]).astype(BF16)
                dpool_ref[rows, lanes] = d
                pooled_ref[rows, lanes] = (_mm(d, wpool_ref[g]) * pscale_ref[:, lanes]).astype(BF16)
            if rows.stop == tm:
                halo_ref[...] = tail
            cfv, s1v, s2v = _rope_tables(pos_ref[:, rows], freq_ref[...], n_rows)
            cf_ref[rows, :], s1_ref[rows, :], s2_ref[rows, :] = cfv, s1v, s2v
            krr = _rope(proj[:, IN_KR0:IN_R], cfv, s1v, s2v)
            qn = (qd * _rms_r(qd) * gq_ref[...]).astype(BF16)
            qf = _mm(qn, wuq[...])
            kvn = (kvd * _rms_r(kvd) * gkv_ref[...]).astype(BF16)
            kf = _mm(kvn, wk[...])
            vf = _mm(kvn, wv[...])
            one_lane = (lax.broadcasted_iota(jnp.int32, (n_rows, HEAD_PAD), 1) == V_HEAD).astype(F32)
            for h in range(N_HEADS):
                lanes = slice(HEAD_PAD * h, HEAD_PAD * (h + 1))
                q_ref[h, rows, :] = (_rope(qf[:, lanes], cfv, s1v, s2v) * Q_PRESCALE).astype(BF16)
                kh = kf[:, lanes] + krr
                vh = vf[:, lanes] + one_lane
                k_ref[h, rows, :] = kh.astype(BF16)
                v_ref[h, rows, :] = vh.astype(BF16)
                kt_ref[h, :, rows] = jnp.transpose(kh).astype(BF16)
        pl.when(step == n_steps - 1)(lambda: gather("finish"))

    per_tile = ATTN_TILE // tm
    heads = pl.BlockSpec((N_HEADS, tm, HEAD_PAD), lambda i: (0, i, 0))
    heads_t = pl.BlockSpec((N_HEADS, None, HEAD_PAD, tm), lambda i: (0, i // per_tile, 0, i % per_tile))
    heads_t_shape = _sds((N_HEADS, seq // ATTN_TILE, HEAD_PAD, ATTN_TILE), BF16)
    return pl.pallas_call(
        body, name="proj_fwd", grid=(seq // tm,),
        in_specs=[_row(tm, D_MODEL), _fix((1, D_MODEL)), _fix((1, 2 * D_MODEL)), _fix((1, Q_LORA)), _fix((1, KV_LORA)),
                  pl.BlockSpec((1, tm), lambda i: (0, i)), _fix((1, HEAD_PAD)), _fix(w_pool_bf.shape), _fix((1, POOL_WIDTH)),
                  ANY, ANY, ANY, ANY] + [ANY] * n_later,
        out_specs=[_row(tm, D_MODEL), _row(tm, Q_LORA), _row(tm, KV_LORA), _row(tm, POOL_WIDTH), _row(tm, POOL_WIDTH),
                   _row(tm, 2 * D_MODEL), heads, heads, heads, heads_t,
                   _row(tm, HEAD_PAD), _row(tm, HEAD_PAD), _row(tm, HEAD_PAD)] + [ANY] * n_later,
        out_shape=[_sds((seq, D_MODEL), BF16), _sds((seq, Q_LORA), F32), _sds((seq, KV_LORA), F32),
                   _sds((seq, POOL_WIDTH), BF16), _sds((seq, POOL_WIDTH), BF16), _sds((seq, 2 * D_MODEL), BF16),
                   _sds((N_HEADS, seq, HEAD_PAD), BF16), _sds((N_HEADS, seq, HEAD_PAD), BF16),
                   _sds((N_HEADS, seq, HEAD_PAD), BF16), heads_t_shape,
                   _sds((seq, HEAD_PAD), F32), _sds((seq, HEAD_PAD), F32), _sds((seq, HEAD_PAD), F32)]
                  + [_sds((N_DEV,) + s.shape, s.dtype) for s in later_shards],
        scratch_shapes=[pltpu.VMEM(w_in_r.shape, BF16), pltpu.VMEM(w_uq_r.shape, BF16),
                        pltpu.VMEM(w_k_exp.shape, BF16), pltpu.VMEM(w_v.shape, BF16),
                        pltpu.VMEM((POOL_HALO, POOL_WIDTH), F32),
                        pltpu.SemaphoreType.DMA((7 * n_later,)), pltpu.SemaphoreType.DMA((7 * n_later,)),
                        pltpu.SemaphoreType.DMA((n_later,))],
        compiler_params=_params(1, 48),
    )(x, g_pre, b_gate, g_q, g_kv, positions, _rope_lane_frequencies(), w_pool_bf, pool_scale,
      w_in_r, w_uq_r, w_k_exp, w_v, *later_shards)


def _window_count(row0, n_rows):
    t = row0 + lax.broadcasted_iota(jnp.int32, (n_rows, POOL_GROUP), 0)
    return [jnp.minimum(t + 1, w).astype(F32) for w in POOL_WINDOWS]


def _pool_bwd_window(dd, tm):
    seq = dd.shape[0]
    n_tiles = seq // tm
    ext_rows = tm + POOL_HALO

    def body(dd_ref, next_ref, dpin_ref):
        i = pl.program_id(0)
        nxt = jnp.where(i == n_tiles - 1, 0.0, next_ref[...])
        dd_t = dd_ref[...]
        ext = jnp.concatenate([dd_t, nxt], axis=0)
        counts = _window_count(i * tm, ext_rows)
        shift = 1
        for g in range(len(POOL_WINDOWS)):
            lanes = slice(POOL_GROUP * g, POOL_GROUP * (g + 1))
            level = ext[:, lanes] / counts[g]
            s = 1
            while s <= shift:
                level = level + pltpu.roll(level, ext_rows - s, 0)
                s *= 2
            shift *= 2
            dpin_ref[:, lanes] = (level[:tm] - dd_t[:, lanes]).astype(BF16)

    halo = tm // POOL_HALO
    return pl.pallas_call(
        body, name="pool_bwd_window", grid=(n_tiles,),
        in_specs=[_row(tm, POOL_WIDTH),
                  pl.BlockSpec((POOL_HALO, POOL_WIDTH), lambda i: (jnp.minimum((i + 1) * halo, seq // POOL_HALO - 1), 0))],
        out_specs=_row(tm, POOL_WIDTH),
        out_shape=_sds((seq, POOL_WIDTH), BF16),
        compiler_params=_params(1, 32),
    )(dd, dd)


def _col_to_row(col, n):
    return jnp.transpose(jnp.broadcast_to(col, (n, HEAD_PAD)))[0:1, :]


def _attn_fwd(q, k, v):
    heads, seq, _ = q.shape
    r, n = FWD_ROWS, FWD_CHAINS
    block = r * n

    def body(q_ref, k_ref, v_ref, o_ref, lse_ref):
        qi = pl.program_id(1)
        q_tiles = [q_ref[c * r:(c + 1) * r, :] for c in range(n)]

        def tile(qt, j, m, acc, diagonal):
            start = pl.multiple_of(j * r, r)
            s = _mm_nt(qt, k_ref[pl.ds(start, r), :])
            if diagonal:
                row = lax.broadcasted_iota(jnp.int32, (r, r), 0)
                col = lax.broadcasted_iota(jnp.int32, (r, r), 1)
                s = jnp.where(col <= row, s, -jnp.inf)
            m_new = jnp.maximum(m, jnp.max(s, axis=1, keepdims=True))
            p = jnp.exp2((s - m_new).astype(BF16))
            acc = jnp.exp2(m - m_new) * acc + _mm(p, v_ref[pl.ds(start, r), :])
            return m_new, acc

        def all_chains(jj, carry):
            for u in range(n):
                carry = tuple(tile(q_tiles[c], n * jj + u, *carry[c], False) for c in range(n))
            return carry

        init = tuple((jnp.full((r, 1), -jnp.inf, F32), jnp.zeros((r, HEAD_PAD), F32)) for _ in range(n))
        state = list(lax.fori_loop(0, qi, all_chains, init))
        for d in range(n):
            for c in range(d, n):
                state[c] = tile(q_tiles[c], n * qi + d, *state[c], c == d)
        for c, (m, acc) in enumerate(state):
            l = acc[:, V_HEAD:V_HEAD + 1]
            o_ref[c * r:(c + 1) * r, :] = (acc / l).astype(BF16)
            row0 = c * r
            lse_ref[row0 // ATTN_TILE, :, row0 % ATTN_TILE:row0 % ATTN_TILE + r] = _col_to_row(m + jnp.log2(l), r)

    return pl.pallas_call(
        body, name="attn_fwd", grid=(heads, seq // block),
        in_specs=[pl.BlockSpec((None, block, HEAD_PAD), lambda h, i: (h, i, 0)),
                  pl.BlockSpec((None, seq, HEAD_PAD), lambda h, i: (h, 0, 0)),
                  pl.BlockSpec((None, seq, HEAD_PAD), lambda h, i: (h, 0, 0))],
        out_specs=[pl.BlockSpec((None, block, HEAD_PAD), lambda h, i: (h, i, 0)),
                   pl.BlockSpec((None, block // ATTN_TILE, 1, ATTN_TILE), lambda h, i: (h, i, 0, 0))],
        out_shape=[_sds((heads, seq, HEAD_PAD), BF16), _sds((heads, seq // ATTN_TILE, 1, ATTN_TILE), F32)],
        compiler_params=_params(2, 48),
    )(q, k, v)


def _peer_copies(src_refs, dst_refs, send_sems, recv_sems):
    x, y, c = _position()
    copies = []
    for w, (src, dst) in enumerate(zip(src_refs, dst_refs)):
        for r in range(1, N_DEV):
            px = 1 - x if r & 4 else x
            py = 1 - y if r & 2 else y
            pc = 1 - c if r & 1 else c
            copies.append(pltpu.make_async_remote_copy(
                src_ref=src.at[4 * px + 2 * py + pc], dst_ref=dst.at[r - 1],
                send_sem=send_sems.at[(N_DEV - 1) * w + r - 1], recv_sem=recv_sems.at[(N_DEV - 1) * w + r - 1],
                device_id=(px, py, pc), device_id_type=MESH))
    return copies


def _attn_bwd(q, k, k_t, v, do, lse, delta, early_grads):
    heads, seq, _ = q.shape
    t = ATTN_TILE
    nq = seq // t
    n = min(BWD_CHAINS, nq)
    n_w = len(early_grads)

    def body(q_ref, k_ref, kt_ref, v_ref, do_ref, lse_ref, delta_ref, *rest):
        grad_refs, rest = rest[:n_w], rest[n_w:]
        dq_ref, dk_ref, dv_ref = rest[:3]
        recv_refs, (send_sems, recv_sems) = rest[3:3 + n_w], rest[3 + n_w:]
        jp = pl.program_id(1)
        head = pl.program_id(0)

        @pl.when((head == 0) & (jp == 0))
        def _():
            for cp in _peer_copies(grad_refs, recv_refs, send_sems, recv_sems):
                cp.start()

        @pl.when(jp == 0)
        def _():
            dq_ref[...] = jnp.zeros_like(dq_ref)

        keys = [k_ref[c * t:(c + 1) * t, :] for c in range(n)]
        values = [v_ref[c * t:(c + 1) * t, :] for c in range(n)]

        def tile(c, i, dk, dv, diagonal):
            start = pl.multiple_of(i * t, t)
            qt = q_ref[pl.ds(start, t), :]
            dot = do_ref[pl.ds(start, t), :]
            p_t = jnp.exp2(_mm_nt(keys[c], qt) - lse_ref[i])
            if diagonal:
                key = lax.broadcasted_iota(jnp.int32, (t, t), 0)
                query = lax.broadcasted_iota(jnp.int32, (t, t), 1)
                p_t = jnp.where(key <= query, p_t, 0.0)
            dv = dv + _mm(p_t.astype(BF16), dot)
            ds_t = (p_t * (_mm_nt(values[c], dot) - delta_ref[i])).astype(BF16)
            dk = dk + _mm(ds_t, qt)
            return dk, dv, _mm(kt_ref[c], ds_t)

        def query_tile(i, state, first_rows):
            dq = None
            for c in range(n if first_rows is None else first_rows + 1):
                dk, dv, dq_c = tile(c, i, *state[c], diagonal=(c == first_rows))
                state[c] = (dk, dv)
                dq = dq_c if dq is None else dq + dq_c
            dq_ref[i] += dq

        def passes(ip, carry):
            state = list(carry)
            for u in range(BWD_QUERIES):
                query_tile(BWD_QUERIES * ip + u, state, None)
            return tuple(state)

        zero = jnp.zeros((t, HEAD_PAD), F32)
        state = [(zero, zero)] * n
        for offset in range(n):
            query_tile(n * jp + offset, state, offset)
        first_pass = (n * (jp + 1)) // BWD_QUERIES
        state = lax.fori_loop(first_pass, nq // BWD_QUERIES, passes, tuple(state))
        for c, (dk, dv) in enumerate(state):
            dk_ref[c * t:(c + 1) * t, :] = (dk * (1.0 / LOG2E)).astype(BF16)
            dv_ref[c * t:(c + 1) * t, :] = dv.astype(BF16)

        @pl.when((head == heads - 1) & (jp == nq // n - 1))
        def _():
            copies = _peer_copies(grad_refs, recv_refs, send_sems, recv_sems)
            for cp in copies:
                cp.wait_recv()
            for cp in copies:
                cp.wait_send()

    whole = pl.BlockSpec((None, seq, HEAD_PAD), lambda h, j: (h, 0, 0))
    whole_t = pl.BlockSpec((None, nq, HEAD_PAD, t), lambda h, j: (h, 0, 0, 0))
    pair = pl.BlockSpec((None, n * t, HEAD_PAD), lambda h, j: (h, j, 0))
    pair_t = pl.BlockSpec((None, n, HEAD_PAD, t), lambda h, j: (h, j, 0, 0))
    stats = pl.BlockSpec((None, nq, 1, t), lambda h, j: (h, 0, 0, 0))
    return pl.pallas_call(
        body, name="attn_bwd", grid=(heads, nq // n),
        in_specs=[whole, pair, pair_t, pair, whole, stats, stats] + [ANY] * n_w,
        out_specs=[whole_t, pair, pair] + [ANY] * n_w,
        out_shape=[_sds((heads, nq, HEAD_PAD, t), F32), _sds((heads, seq, HEAD_PAD), BF16),
                   _sds((heads, seq, HEAD_PAD), BF16)]
                  + [_sds((N_DEV - 1,) + g.shape[1:], g.dtype) for g in early_grads],
        scratch_shapes=[pltpu.SemaphoreType.DMA(((N_DEV - 1) * n_w,)), pltpu.SemaphoreType.DMA(((N_DEV - 1) * n_w,))],
        compiler_params=_params(2, 56),
    )(q, k, k_t, v, do, lse, delta, *early_grads)


def _merge_fwd(attn, pooled, gates, x, g_post_mix, w_ba, w_bb, w_out, tm):
    seq = x.shape[0]

    def body(attn_ref, pooled_ref, gates_ref, x_ref, g_ref, wba_ref, wbb_ref, wout_ref,
             merged_ref, ba_ref, bb_ref, y_ref, h1_ref, attn_rows_ref):
        for rows in _row_chains(tm):
            attn = jnp.concatenate([attn_ref[h, rows, :] for h in range(N_HEADS)], axis=1)
            attn_rows_ref[rows, :] = attn
            ba = _mm(attn, wba_ref[...])
            bb = _mm(pooled_ref[rows, :], wbb_ref[...])
            ba_ref[rows, :] = ba.astype(BF16)
            bb_ref[rows, :] = bb.astype(BF16)
            merged = (gates_ref[rows, :D_MODEL].astype(F32) * ba
                      + gates_ref[rows, D_MODEL:].astype(F32) * bb).astype(BF16)
            merged_ref[rows, :] = merged
            y = _mm(merged, wout_ref[...])
            y_ref[rows, :] = y
            h1_ref[rows, :] = x_ref[rows, :] + y * _rms_r(y) * g_ref[...]

    return pl.pallas_call(
        body, name="merge_fwd", grid=(seq // tm,),
        in_specs=[pl.BlockSpec((N_HEADS, tm, HEAD_PAD), lambda i: (0, i, 0)), _row(tm, POOL_WIDTH),
                  _row(tm, 2 * D_MODEL), _row(tm, D_MODEL),
                  _fix((1, D_MODEL)), _fix(w_ba.shape), _fix(w_bb.shape), _fix(w_out.shape)],
        out_specs=[_row(tm, D_MODEL)] * 5 + [_row(tm, N_HEADS * HEAD_PAD)],
        out_shape=[_sds((seq, D_MODEL), BF16), _sds((seq, D_MODEL), BF16), _sds((seq, D_MODEL), BF16),
                   _sds((seq, D_MODEL), F32), _sds((seq, D_MODEL), F32), _sds((seq, N_HEADS * HEAD_PAD), BF16)],
        compiler_params=_params(1, 48),
    )(attn, pooled, gates, x, g_post_mix, w_ba, w_bb, w_out)


def _tail_fwd(h1, target, p, g_pre_mlp, g_post_mlp, g_ple, w_ff1, w_ff2, w_pe, w_pg, tm):
    seq = h1.shape[0]

    def body(h1_ref, tgt_ref, p_ref, gm_ref, gf_ref, gp_ref, w1_hbm, w2_hbm, wpe_hbm, wpg_hbm,
             m_ref, zr_ref, f_ref, h2b_ref, pb_ref, de_ref, dpre_ref, dh2_ref, loss_ref, dgple_ref,
             w1, w2, wpe, wpg):
        _load_once(_column_blocks(w1_hbm, w1) + ((w2_hbm, w2), (wpe_hbm, wpe), (wpg_hbm, wpg)))

        @pl.when(pl.program_id(0) == 0)
        def _():
            loss_ref[...] = jnp.zeros_like(loss_ref)
            dgple_ref[...] = jnp.zeros_like(dgple_ref)

        h1v = h1_ref[...]
        m = (h1v * _rms_r(h1v) * gm_ref[...]).astype(BF16)
        m_ref[...] = m
        zr = jnp.maximum(_mm(m, w1[...]), 0.0)
        zr_ref[...] = zr.astype(BF16)
        f = _mm((zr * zr).astype(BF16), w2[...])
        f_ref[...] = f
        h2 = h1v + f * _rms_r(f) * gf_ref[...]
        h2b = h2.astype(BF16)
        h2b_ref[...] = h2b
        pb = p_ref[...].astype(BF16)
        pb_ref[...] = pb
        e = _mm(pb, wpe[...])
        pg = _sigmoid(_mm(h2b, wpg[...]))
        t3 = pg * e
        r3 = _rms_r(t3)
        t3hat = t3 * r3
        diff = h2 + t3hat * gp_ref[...] - tgt_ref[...]
        loss_ref[...] += jnp.sum(diff * diff) * (0.5 / D_MODEL)
        dh3 = diff * (1.0 / D_MODEL)
        dgple_ref[...] += _colsum(dh3 * t3hat)
        dt3 = _rms_bwd(t3hat, r3, gp_ref[...], dh3)
        de_ref[...] = (dt3 * pg).astype(BF16)
        dpre = (dt3 * e * pg * (1.0 - pg)).astype(BF16)
        dpre_ref[...] = dpre
        dh2_ref[...] = dh3 + _mm_nt(dpre, wpg[...])

    return pl.pallas_call(
        body, name="tail_fwd", grid=(seq // tm,),
        in_specs=[_row(tm, D_MODEL), _row(tm, D_MODEL), _row(tm, PLE_DIM), _fix((1, D_MODEL)), _fix((1, D_MODEL)),
                  _fix((1, D_MODEL)), ANY, ANY, ANY, ANY],
        out_specs=[_row(tm, D_MODEL), _row(tm, D_FF), _row(tm, D_MODEL), _row(tm, D_MODEL), _row(tm, PLE_DIM),
                   _row(tm, D_MODEL), _row(tm, D_MODEL), _row(tm, D_MODEL), _fix((8, 128)), _fix((1, D_MODEL))],
        out_shape=[_sds((seq, D_MODEL), BF16), _sds((seq, D_FF), BF16), _sds((seq, D_MODEL), F32),
                   _sds((seq, D_MODEL), BF16), _sds((seq, PLE_DIM), BF16), _sds((seq, D_MODEL), BF16),
                   _sds((seq, D_MODEL), BF16), _sds((seq, D_MODEL), F32), _sds((8, 128), F32), _sds((1, D_MODEL), F32)],
        scratch_shapes=[pltpu.VMEM((w_ff1.shape[1], N_DEV * w_ff1.shape[2]), BF16), pltpu.VMEM(w_ff2.shape, BF16),
                        pltpu.VMEM(w_pe.shape, BF16), pltpu.VMEM(w_pg.shape, BF16)],
        compiler_params=_params(1, 56),
    )(h1, target, p, g_pre_mlp, g_post_mlp, g_ple, w_ff1, w_ff2, w_pe, w_pg)


def _mlp_bwd(h1, f, zr, dh2, g_pre_mlp, g_post_mlp, w_ff1, w_ff2, tm):
    seq = h1.shape[0]

    def body(h1_ref, f_ref, zr_ref, dh2_ref, gm_ref, gf_ref, w1_hbm, w2_hbm,
             df_ref, dz_ref, dh1_ref, dgm_ref, dgf_ref, w1, w2):
        _load_once(_column_blocks(w1_hbm, w1) + ((w2_hbm, w2),))

        @pl.when(pl.program_id(0) == 0)
        def _():
            dgm_ref[...] = jnp.zeros_like(dgm_ref)
            dgf_ref[...] = jnp.zeros_like(dgf_ref)

        dh2 = dh2_ref[...]
        fv = f_ref[...]
        rf = _rms_r(fv)
        fhat = fv * rf
        dgf_ref[...] += _colsum(dh2 * fhat)
        df = _rms_bwd(fhat, rf, gf_ref[...], dh2).astype(BF16)
        df_ref[...] = df
        dz = (_mm_nt(df, w2[...]) * (2.0 * zr_ref[...].astype(F32))).astype(BF16)
        dz_ref[...] = dz
        dm = _mm_nt(dz, w1[...])
        h1v = h1_ref[...]
        r1 = _rms_r(h1v)
        h1hat = h1v * r1
        dgm_ref[...] += _colsum(dm * h1hat)
        dh1_ref[...] = dh2 + _rms_bwd(h1hat, r1, gm_ref[...], dm)

    return pl.pallas_call(
        body, name="mlp_bwd", grid=(seq // tm,),
        in_specs=[_row(tm, D_MODEL), _row(tm, D_MODEL), _row(tm, D_FF), _row(tm, D_MODEL),
                  _fix((1, D_MODEL)), _fix((1, D_MODEL)), ANY, ANY],
        out_specs=[_row(tm, D_MODEL), _row(tm, D_FF), _row(tm, D_MODEL), _fix((1, D_MODEL)), _fix((1, D_MODEL))],
        out_shape=[_sds((seq, D_MODEL), BF16), _sds((seq, D_FF), BF16), _sds((seq, D_MODEL), F32),
                   _sds((1, D_MODEL), F32), _sds((1, D_MODEL), F32)],
        scratch_shapes=[pltpu.VMEM((w_ff1.shape[1], N_DEV * w_ff1.shape[2]), BF16), pltpu.VMEM(w_ff2.shape, BF16)],
        compiler_params=_params(1, 56),
    )(h1, f, zr, dh2, g_pre_mlp, g_post_mlp, w_ff1, w_ff2)


def _merge_bwd(dh1, y, gates, ba, bb, d_pool, o_heads, g_post_mix, pool_scale, w_out, w_ba, w_bb, w_pool_bf, tm):
    seq = dh1.shape[0]
    assert tm == ATTN_TILE

    def body(dh1_ref, y_ref, gates_ref, ba_ref, bb_ref, d_ref, o_ref, g_ref, ps_ref, wout_ref, wba_ref, wbb_ref, wp_ref,
             dy_ref, dba_ref, dbb_ref, dgpre_ref, dattn_ref, dd_ref, delta_ref, dg_ref, dbg_ref, dps_ref, dwp_ref):
        @pl.when(pl.program_id(0) == 0)
        def _():
            dg_ref[...] = jnp.zeros_like(dg_ref)
            dbg_ref[...] = jnp.zeros_like(dbg_ref)
            dps_ref[...] = jnp.zeros_like(dps_ref)
            dwp_ref[...] = jnp.zeros_like(dwp_ref)

        for rows in _row_chains(tm):
            dh1v = dh1_ref[rows, :]
            yv = y_ref[rows, :]
            r = _rms_r(yv)
            yhat = yv * r
            dg_ref[...] += _colsum(dh1v * yhat)
            dy = _rms_bwd(yhat, r, g_ref[...], dh1v).astype(BF16)
            dy_ref[rows, :] = dy
            dmerged = _mm_nt(dy, wout_ref[...])
            dbranch = []
            for half, branch_ref, dbranch_ref in ((0, ba_ref, dba_ref), (1, bb_ref, dbb_ref)):
                lanes = slice(D_MODEL * half, D_MODEL * (half + 1))
                gate = gates_ref[rows, lanes].astype(F32)
                dpre = dmerged * branch_ref[rows, :].astype(F32) * gate * (1.0 - gate)
                dbg_ref[:, lanes] += _colsum(dpre)
                dgpre_ref[rows, lanes] = dpre.astype(BF16)
                dbranch.append((dmerged * gate).astype(BF16))
                dbranch_ref[rows, :] = dbranch[-1]
            dattn = _mm_nt(dbranch[0], wba_ref[...]).astype(BF16)
            for h in range(N_HEADS):
                do_h = dattn[:, HEAD_PAD * h:HEAD_PAD * (h + 1)]
                dattn_ref[h, rows, :] = do_h
                row_term = jnp.sum(o_ref[h, rows, :].astype(F32) * do_h.astype(F32), axis=1, keepdims=True)
                delta_ref[h, :, rows] = _col_to_row(row_term, rows.stop - rows.start)
            dpooled = _mm_nt(dbranch[1], wbb_ref[...])
            for g in range(len(POOL_WINDOWS)):
                lanes = slice(POOL_GROUP * g, POOL_GROUP * (g + 1))
                dpl = dpooled[:, lanes]
                d_g = d_ref[rows, lanes]
                dps_ref[:, lanes] += _colsum(dpl * _mm(d_g, wp_ref[g]))
                dyp = (dpl * ps_ref[:, lanes]).astype(BF16)
                dwp_ref[g] += lax.dot_general(d_g, dyp, TN, preferred_element_type=F32)
                dd_ref[rows, lanes] = _mm_nt(dyp, wp_ref[g])

    heads = pl.BlockSpec((N_HEADS, tm, HEAD_PAD), lambda i: (0, i, 0))
    return pl.pallas_call(
        body, name="merge_bwd", grid=(seq // tm,),
        in_specs=[_row(tm, D_MODEL), _row(tm, D_MODEL), _row(tm, 2 * D_MODEL), _row(tm, D_MODEL), _row(tm, D_MODEL),
                  _row(tm, POOL_WIDTH), heads, _fix((1, D_MODEL)), _fix((1, POOL_WIDTH)),
                  _fix(w_out.shape), _fix(w_ba.shape), _fix(w_bb.shape), _fix(w_pool_bf.shape)],
        out_specs=[_row(tm, D_MODEL), _row(tm, D_MODEL), _row(tm, D_MODEL), _row(tm, 2 * D_MODEL),
                   heads, _row(tm, POOL_WIDTH), pl.BlockSpec((N_HEADS, None, 1, tm), lambda i: (0, i, 0, 0)),
                   _fix((1, D_MODEL)), _fix((1, 2 * D_MODEL)), _fix((1, POOL_WIDTH)), _fix(w_pool_bf.shape)],
        out_shape=[_sds((seq, D_MODEL), BF16), _sds((seq, D_MODEL), BF16), _sds((seq, D_MODEL), BF16),
                   _sds((seq, 2 * D_MODEL), BF16), _sds((N_HEADS, seq, HEAD_PAD), BF16),
                   _sds((seq, POOL_WIDTH), F32), _sds((N_HEADS, seq // tm, 1, tm), F32),
                   _sds((1, D_MODEL), F32), _sds((1, 2 * D_MODEL), F32),
                   _sds((1, POOL_WIDTH), F32), _sds(w_pool_bf.shape, F32)],
        compiler_params=_params(1, 48),
    )(dh1, y, gates, ba, bb, d_pool, o_heads, g_post_mix, pool_scale, w_out, w_ba, w_bb, w_pool_bf)


def _proj_bwd(dq, dk, dv, qd, kvd, x, dh1, dgpre, dpin, cf, s1, s2, g_pre, g_q, g_kv,
              w_in_r, w_uq_r, w_k_exp, w_v, tm):
    seq = x.shape[0]

    def body(dq_ref, dk_ref, dv_ref, qd_ref, kvd_ref, x_ref, dh1_ref, dgpre_ref, dpin_ref, cf_ref, s1_ref, s2_ref,
             gpre_ref, gq_ref, gkv_ref, win_hbm, wuq_hbm, wk_hbm, wv_hbm,
             gx_ref, dproj_ref, dqb_ref, qn_ref, kvn_ref, dkvb_ref, dgpre_acc, dgq_acc, dgkv_acc,
             win, wuq, wk, wv):
        _load_once(((win_hbm, win), (wuq_hbm, wuq), (wk_hbm, wk), (wv_hbm, wv)))

        @pl.when(pl.program_id(0) == 0)
        def _():
            dgpre_acc[...] = jnp.zeros_like(dgpre_acc)
            dgq_acc[...] = jnp.zeros_like(dgq_acc)
            dgkv_acc[...] = jnp.zeros_like(dgkv_acc)

        for rows in _row_chains(tm):
            n_rows = rows.stop - rows.start
            cfv, s1v, s2v = cf_ref[rows, :], s1_ref[rows, :], s2_ref[rows, :]
            ksum = jnp.zeros((n_rows, HEAD_PAD), F32)
            for h in range(N_HEADS):
                lanes = slice(HEAD_PAD * h, HEAD_PAD * (h + 1))
                dqh = jnp.transpose(dq_ref[h, :, rows])
                dqb_ref[rows, lanes] = (_rope_t(dqh, cfv, s1v, s2v) * ATTN_SCALE).astype(BF16)
                dkh = dk_ref[h, rows, :]
                dkvb_ref[rows, lanes] = dkh
                dkvb_ref[rows, slice(K_ALL + lanes.start, K_ALL + lanes.stop)] = dv_ref[h, rows, :]
                ksum = ksum + dkh.astype(F32)
            lane = lax.broadcasted_iota(jnp.int32, (n_rows, HEAD_PAD), 1)
            rope_lanes = (lane >= QK_NOPE) & (lane < QK_NOPE + QK_ROPE)
            dkr = _rope_t(jnp.where(rope_lanes, ksum, 0.0), cfv, s1v, s2v)

            qdv = qd_ref[rows, :]
            rq = _rms_r(qdv)
            qhat = qdv * rq
            qn_ref[rows, :] = (qhat * gq_ref[...]).astype(BF16)
            dqn = _mm_nt(dqb_ref[rows, :], wuq[...])
            dgq_acc[...] += _colsum(dqn * qhat)
            dproj_ref[rows, IN_Q0:IN_KV0] = _rms_bwd(qhat, rq, gq_ref[...], dqn).astype(BF16)

            kvdv = kvd_ref[rows, :]
            rkv = _rms_r(kvdv)
            kvhat = kvdv * rkv
            kvn_ref[rows, :] = (kvhat * gkv_ref[...]).astype(BF16)
            dkvn = _mm_nt(dkvb_ref[rows, :K_ALL], wk[...]) + _mm_nt(dkvb_ref[rows, K_ALL:], wv[...])
            dgkv_acc[...] += _colsum(dkvn * kvhat)
            dproj_ref[rows, IN_KV0:IN_POOL0] = _rms_bwd(kvhat, rkv, gkv_ref[...], dkvn).astype(BF16)

            dproj_ref[rows, IN_POOL0:IN_GATE0] = dpin_ref[rows, :]
            dproj_ref[rows, IN_GATE0:IN_KR0] = dgpre_ref[rows, :]
            dproj_ref[rows, IN_KR0:IN_R] = dkr.astype(BF16)

            da = _mm_nt(dproj_ref[rows, :], win[...])
            xv = x_ref[rows, :]
            r0 = _rms_r(xv)
            xhat = xv * r0
            dgpre_acc[...] += _colsum(da * xhat)
            gx_ref[rows, :] = dh1_ref[rows, :] + _rms_bwd(xhat, r0, gpre_ref[...], da)

    per_tile = ATTN_TILE // tm
    heads = pl.BlockSpec((N_HEADS, tm, HEAD_PAD), lambda i: (0, i, 0))
    heads_t = pl.BlockSpec((N_HEADS, None, HEAD_PAD, tm), lambda i: (0, i // per_tile, 0, i % per_tile))
    return pl.pallas_call(
        body, name="proj_bwd", grid=(seq // tm,),
        in_specs=[heads_t, heads, heads, _row(tm, Q_LORA), _row(tm, KV_LORA), _row(tm, D_MODEL),
                  _row(tm, D_MODEL), _row(tm, 2 * D_MODEL), _row(tm, POOL_WIDTH),
                  _row(tm, HEAD_PAD), _row(tm, HEAD_PAD), _row(tm, HEAD_PAD),
                  _fix((1, D_MODEL)), _fix((1, Q_LORA)), _fix((1, KV_LORA)), ANY, ANY, ANY, ANY],
        out_specs=[_row(tm, D_MODEL), _row(tm, IN_R), _row(tm, N_HEADS * HEAD_PAD), _row(tm, Q_LORA), _row(tm, KV_LORA),
                   _row(tm, 2 * K_ALL),
                   _fix((1, D_MODEL)), _fix((1, Q_LORA)), _fix((1, KV_LORA))],
        out_shape=[_sds((seq, D_MODEL), F32), _sds((seq, IN_R), BF16), _sds((seq, N_HEADS * HEAD_PAD), BF16),
                   _sds((seq, Q_LORA), BF16), _sds((seq, KV_LORA), BF16), _sds((seq, 2 * K_ALL), BF16),
                   _sds((1, D_MODEL), F32), _sds((1, Q_LORA), F32), _sds((1, KV_LORA), F32)],
        scratch_shapes=[pltpu.VMEM(w_in_r.shape, BF16), pltpu.VMEM(w_uq_r.shape, BF16),
                        pltpu.VMEM(w_k_exp.shape, BF16), pltpu.VMEM(w_v.shape, BF16)],
        compiler_params=_params(1, 58),
    )(dq, dk, dv, qd, kvd, x, dh1, dgpre, dpin, cf, s1, s2, g_pre, g_q, g_kv, w_in_r, w_uq_r, w_k_exp, w_v)


def _grad_w(a, b, name, square_a=False, by_device=False, with_bf16=False):
    seq, k_dim = a.shape
    n_dim = b.shape[1]
    tk = min(k_dim, 1024)
    tn = n_dim // 2 if n_dim == IN_R else min(n_dim, 1024)
    ts = min(seq, 2048)
    shard = n_dim // N_DEV
    per_tile = tn // shard
    if by_device:
        out_spec = pl.BlockSpec((per_tile, tk, shard), lambda i, j, s: (j, i, 0))
        out_shape = _sds((N_DEV, k_dim, shard), F32)
    else:
        out_spec = pl.BlockSpec((tk, tn), lambda i, j, s: (i, j))
        out_shape = _sds((k_dim, n_dim), F32)

    n_seq_steps = seq // ts

    def body(a_ref, b_ref, o_ref, *narrow):
        @pl.when(pl.program_id(2) == 0)
        def _():
            o_ref[...] = jnp.zeros_like(o_ref)

        at = a_ref[...]
        if square_a:
            at = at * at
        part = lax.dot_general(at, b_ref[...], TN, preferred_element_type=F32)
        if by_device:
            for d in range(per_tile):
                o_ref[d] += part[:, d * shard:(d + 1) * shard]
        else:
            o_ref[...] += part
        if with_bf16:
            @pl.when(pl.program_id(2) == n_seq_steps - 1)
            def _():
                narrow[0][...] = o_ref[...].astype(BF16)

    return pl.pallas_call(
        body, name=name, grid=(k_dim // tk, n_dim // tn, n_seq_steps),
        in_specs=[pl.BlockSpec((ts, tk), lambda i, j, s: (s, i)), pl.BlockSpec((ts, tn), lambda i, j, s: (s, j))],
        out_specs=[out_spec, out_spec] if with_bf16 else out_spec,
        out_shape=[out_shape, _sds(out_shape.shape, BF16)] if with_bf16 else out_shape,
        compiler_params=_params(3, 48),
    )(a, b)


def _position():
    return lax.axis_index("x"), lax.axis_index("y"), lax.axis_index("c")


def _gather_copies(x_ref, slot, send_sems, recv_sems, local_sem, phases=("send", "forward", "finish"), sem_base=0):
    x, y, c = _position()
    me, sibling = (x, y, c), (x, y, 1 - c)
    chips = [(1 - x, y), (x, 1 - y), (1 - x, 1 - y)]

    def copy(k, block, to, src=None):
        return pltpu.make_async_remote_copy(
            src_ref=slot(*block) if src is None else src, dst_ref=slot(*block),
            send_sem=send_sems.at[sem_base + k], recv_sem=recv_sems.at[sem_base + k],
            device_id=to, device_id_type=MESH)

    mine = pltpu.make_async_copy(x_ref, slot(*me), local_sem)
    first = [copy(0, me, sibling, src=x_ref)]
    first += [copy(1 + j, me, (*chip, c), src=x_ref) for j, chip in enumerate(chips)]
    passed = [copy(4 + j, (*chip, c), sibling) for j, chip in enumerate(chips)]
    if "send" in phases:
        mine.start()
        for cp in first:
            cp.start()
    if "forward" in phases:
        for j, chip in enumerate(chips):
            copy(1 + j, (*chip, c), me).wait_recv()
            passed[j].start()
    if "finish" in phases:
        copy(0, sibling, me).wait_recv()
        for j, chip in enumerate(chips):
            copy(4 + j, (*chip, 1 - c), me).wait_recv()
        for cp in first + passed:
            cp.wait_send()
        mine.wait()


def _all_gather_hbm(block):
    def body(x_ref, out_ref, send_sems, recv_sems, local_sem):
        _gather_copies(x_ref, lambda px, py, pc: out_ref.at[4 * px + 2 * py + pc], send_sems, recv_sems, local_sem)

    return pl.pallas_call(
        body, name="gather_weights",
        in_specs=[ANY], out_specs=ANY,
        out_shape=_sds((N_DEV,) + block.shape, block.dtype),
        scratch_shapes=[pltpu.SemaphoreType.DMA((7,)), pltpu.SemaphoreType.DMA((7,)), pltpu.SemaphoreType.DMA],
    )(block)


def _replicated_update(grads, loss_block, ws, ms, vs):
    n_p = len(grads)
    sent = list(grads) + [loss_block]
    n_a = len(sent)

    def body(*refs):
        g_refs, refs = refs[:n_a], refs[n_a:]
        w_refs, m_refs, v_refs, refs = refs[:n_p], refs[n_p:2 * n_p], refs[2 * n_p:3 * n_p], refs[3 * n_p:]
        sum_refs, refs = refs[:n_a], refs[n_a:]
        d_refs, nm_refs, nv_refs, refs = refs[:n_p], refs[n_p:2 * n_p], refs[2 * n_p:3 * n_p], refs[3 * n_p:]
        bufs, (send_sems, recv_sems, local_sems) = refs[:n_a], refs[n_a:]
        x, y, c = _position()
        me = 4 * x + 2 * y + c
        local, remote = [], []
        for a in range(n_a):
            local.append(pltpu.make_async_copy(g_refs[a], bufs[a].at[me], local_sems.at[a]))
            for r in range(1, N_DEV):
                peer = (1 - x if r & 4 else x, 1 - y if r & 2 else y, 1 - c if r & 1 else c)
                remote.append(pltpu.make_async_remote_copy(
                    src_ref=g_refs[a], dst_ref=bufs[a].at[me],
                    send_sem=send_sems.at[(N_DEV - 1) * a + r - 1], recv_sem=recv_sems.at[(N_DEV - 1) * a + r - 1],
                    device_id=peer, device_id_type=MESH))
        for cp in local + remote:
            cp.start()
        for cp in remote:
            cp.wait_recv()
        for cp in remote:
            cp.wait_send()
        for cp in local:
            cp.wait()
        for a in range(n_a):
            acc = bufs[a][0]
            for d in range(1, N_DEV):
                acc = acc + bufs[a][d]
            if a == n_p:
                sum_refs[a][...] = acc
                continue
            delta, new_m, new_v = _adamw_math(acc, w_refs[a][...], m_refs[a][...], v_refs[a][...])
            sum_refs[a][...], d_refs[a][...], nm_refs[a][...], nv_refs[a][...] = acc, delta, new_m, new_v

    vmem = pl.BlockSpec(memory_space=pltpu.VMEM)
    like_w = [_sds(w.shape, F32) for w in ws]
    n_sem = (N_DEV - 1) * n_a
    outs = pl.pallas_call(
        body, name="replicated_update",
        in_specs=[vmem] * (n_a + 3 * n_p), out_specs=[vmem] * (n_a + 3 * n_p),
        out_shape=like_w + [_sds(loss_block.shape, F32)] + like_w * 3,
        scratch_shapes=[pltpu.VMEM((N_DEV,) + g.shape, F32) for g in sent]
                       + [pltpu.SemaphoreType.DMA((n_sem,)), pltpu.SemaphoreType.DMA((n_sem,)),
                          pltpu.SemaphoreType.DMA((n_a,))],
        compiler_params=pltpu.CompilerParams(vmem_limit_bytes=32 * MIB),
    )(*sent, *ws, *ms, *vs)
    return (outs[:n_p], outs[n_p], outs[n_a:n_a + n_p], outs[n_a + n_p:n_a + 2 * n_p], outs[n_a + 2 * n_p:])


def _exchange_pair(gs):
    n_w = len(gs)

    def body(*refs):
        g_refs, out_refs = refs[:n_w], refs[n_w:2 * n_w]
        send_sems, recv_sems = refs[2 * n_w:]
        x, y, c = _position()
        copies = []
        for w in range(n_w):
            for chip in range(4):
                cp = pltpu.make_async_remote_copy(
                    src_ref=g_refs[w].at[2 * chip + (1 - c)], dst_ref=out_refs[w].at[chip],
                    send_sem=send_sems.at[4 * w + chip], recv_sem=recv_sems.at[4 * w + chip],
                    device_id=(x, y, 1 - c), device_id_type=MESH)
                cp.start()
                copies.append(cp)
        for cp in copies:
            cp.wait_recv()
        for cp in copies:
            cp.wait_send()

    return pl.pallas_call(
        body, name="exchange_pair",
        in_specs=[ANY] * n_w, out_specs=[ANY] * n_w,
        out_shape=[_sds((4,) + g.shape[1:], g.dtype) for g in gs],
        scratch_shapes=[pltpu.SemaphoreType.DMA((4 * n_w,)), pltpu.SemaphoreType.DMA((4 * n_w,))],
    )(*gs)


def _exchange_chips(parts):
    n_w = len(parts)

    def body(*refs):
        p_refs, out_refs = refs[:n_w], refs[n_w:2 * n_w]
        send_sems, recv_sems = refs[2 * n_w:]
        x, y, c = _position()
        chips = [(1 - x, y), (x, 1 - y), (1 - x, 1 - y)]
        copies = []
        for w in range(n_w):
            for k, (px, py) in enumerate(chips):
                cp = pltpu.make_async_remote_copy(
                    src_ref=p_refs[w].at[2 * px + py], dst_ref=out_refs[w].at[k],
                    send_sem=send_sems.at[3 * w + k], recv_sem=recv_sems.at[3 * w + k],
                    device_id=(px, py, c), device_id_type=MESH)
                cp.start()
                copies.append(cp)
        for cp in copies:
            cp.wait_recv()
        for cp in copies:
            cp.wait_send()

    return pl.pallas_call(
        body, name="exchange_chips",
        in_specs=[ANY] * n_w, out_specs=[ANY] * n_w,
        out_shape=[_sds((3,) + p.shape[1:], p.dtype) for p in parts],
        scratch_shapes=[pltpu.SemaphoreType.DMA((3 * n_w,)), pltpu.SemaphoreType.DMA((3 * n_w,))],
    )(*parts)


def _row_tile(k):
    return 256 if k % 256 == 0 else 128


def _pair_sum(g, recv, place, name):
    _, k, n = g.shape
    tr = _row_tile(k)
    g4 = g.reshape(4, 2, k, n)

    def body(s_ref, g_ref, r_ref, o_ref):
        o_ref[...] = (g_ref[...] + r_ref[...]).astype(BF16)

    spec = pltpu.PrefetchScalarGridSpec(
        num_scalar_prefetch=1, grid=(4, k // tr),
        in_specs=[pl.BlockSpec((None, None, tr, n), lambda j, i, s: (j, s[2], i, 0)),
                  pl.BlockSpec((None, tr, n), lambda j, i, s: (j, i, 0))],
        out_specs=pl.BlockSpec((None, tr, n), lambda j, i, s: (j, i, 0)))
    return pl.pallas_call(
        body, name=name, grid_spec=spec, out_shape=_sds((4, k, n), BF16),
        compiler_params=_params(2, 32),
    )(place, g4, recv)


def _adamw_math(g, w, m, v):
    m = ADAM_B1 * m + (1.0 - ADAM_B1) * g
    v = ADAM_B2 * v + (1.0 - ADAM_B2) * (g * g)
    m_hat = m / (1.0 - ADAM_B1 ** ADAM_STEP)
    v_hat = v / (1.0 - ADAM_B2 ** ADAM_STEP)
    delta = -ADAM_LR * (m_hat / (jnp.sqrt(v_hat) + ADAM_EPS) + ADAM_WD * w)
    return delta, m, v


def _adamw_sharded(g, from_sibling, from_chips, place, w, m, v, name):
    _, k, n = g.shape
    tr = _row_tile(k)

    def body(s_ref, g_ref, sib_ref, r0_ref, r1_ref, r2_ref, w_ref, m_ref, v_ref, grad_ref, d_ref, nm_ref, nv_ref):
        grad = g_ref[...] + sib_ref[...]
        for r_ref in (r0_ref, r1_ref, r2_ref):
            grad = grad + r_ref[...].astype(F32)
        grad_ref[...] = grad
        d_ref[...], nm_ref[...], nv_ref[...] = _adamw_math(grad, w_ref[...], m_ref[...], v_ref[...])

    tile = pl.BlockSpec((None, tr, n), lambda i, s: (0, i, 0))

    def slot(j):
        return pl.BlockSpec((None, tr, n), lambda i, s: (j, i, 0))

    spec = pltpu.PrefetchScalarGridSpec(
        num_scalar_prefetch=1, grid=(k // tr,),
        in_specs=[pl.BlockSpec((None, tr, n), lambda i, s: (s[0], i, 0)),
                  pl.BlockSpec((None, tr, n), lambda i, s: (s[1], i, 0)),
                  slot(0), slot(1), slot(2), tile, tile, tile],
        out_specs=[tile] * 4)
    return pl.pallas_call(
        body, name=name, grid_spec=spec, out_shape=[_sds((1, k, n), F32)] * 4,
        compiler_params=_params(1, 48),
    )(place, g, from_sibling, from_chips, from_chips, from_chips, w, m, v)


def _adamw_direct(g, received, place, w, m, v, name):
    _, k, n = g.shape
    tr = _row_tile(k)

    def body(s_ref, g_ref, r_ref, w_ref, m_ref, v_ref, grad_ref, d_ref, nm_ref, nv_ref):
        grad = g_ref[...]
        for r in range(N_DEV - 1):
            grad = grad + r_ref[r].astype(F32)
        grad_ref[...] = grad
        d_ref[...], nm_ref[...], nv_ref[...] = _adamw_math(grad, w_ref[...], m_ref[...], v_ref[...])

    tile = pl.BlockSpec((None, tr, n), lambda i, s: (0, i, 0))
    spec = pltpu.PrefetchScalarGridSpec(
        num_scalar_prefetch=1, grid=(k // tr,),
        in_specs=[pl.BlockSpec((None, tr, n), lambda i, s: (s[0], i, 0)),
                  pl.BlockSpec((N_DEV - 1, tr, n), lambda i, s: (0, i, 0)), tile, tile, tile],
        out_specs=[tile] * 4)
    return pl.pallas_call(
        body, name=name, grid_spec=spec, out_shape=[_sds((1, k, n), F32)] * 4,
        compiler_params=_params(1, 48),
    )(place, g, received, w, m, v)


def _pack_rows(parts):
    parts = [a.reshape(-1, LANES) for a in parts]
    pad = (-sum(a.shape[0] for a in parts)) % PACK_ROW_TILE
    return jnp.concatenate(parts + [jnp.zeros((pad, LANES), parts[0].dtype)], axis=0)


def _full_from_gathered(gathered, entries, shard_shapes):
    out, off = {}, 0
    for (name, kind), (k, n) in zip(entries, shard_shapes):
        rows = k * n // LANES
        seg = gathered[:, off:off + rows].reshape(N_DEV, k, n)
        out[name] = jnp.transpose(seg, (1, 0, 2)).reshape(k, N_DEV * n) if kind == "col" else seg.reshape(N_DEV * k, n)
        off += rows
    return out


def _columns_by_device(a):
    k, n_all = a.shape
    return jnp.transpose(a.reshape(k, N_DEV, n_all // N_DEV), (1, 0, 2))


def _rows_by_device(a):
    k_all, n = a.shape
    return a.reshape(N_DEV, k_all // N_DEV, n)


def _rope_lane_frequencies():
    inv_freq = ROPE_THETA ** (-jnp.arange(0, QK_ROPE, 2, dtype=F32) / QK_ROPE)
    zeros = lambda n: jnp.zeros((n,), F32)
    return jnp.concatenate([zeros(QK_NOPE), inv_freq, inv_freq, zeros(HEAD_PAD - QK_NOPE - QK_ROPE)])[None, :]


def _rope_tables(pos_row, freq, tm):
    pos = jnp.transpose(jnp.broadcast_to(pos_row.astype(F32), (HEAD_PAD, tm)))
    ang = pos * freq
    cos, sin = jnp.cos(ang), jnp.sin(ang)
    lane = lax.broadcasted_iota(jnp.int32, (tm, HEAD_PAD), 1)
    first = (lane >= QK_NOPE) & (lane < QK_NOPE + QK_ROPE // 2)
    second = (lane >= QK_NOPE + QK_ROPE // 2) & (lane < QK_NOPE + QK_ROPE)
    cf = jnp.where(lane < QK_NOPE, 1.0, jnp.where(first | second, cos, 0.0))
    return cf, jnp.where(first, -sin, 0.0), jnp.where(second, sin, 0.0)


def _arrange_w_in(w):
    k = w.shape[0]
    zeros = lambda n: jnp.zeros((k, n), w.dtype)
    kr0 = Q_LORA + KV_LORA
    pool0 = kr0 + QK_ROPE
    return jnp.concatenate([w[:, :kr0], w[:, pool0:], zeros(QK_NOPE), w[:, kr0:pool0],
                            zeros(HEAD_PAD - QK_NOPE - QK_ROPE)], axis=1)


def _restore_w_in(d):
    kr = d[:, IN_KR0 + QK_NOPE:IN_KR0 + QK_NOPE + QK_ROPE]
    return jnp.concatenate([d[:, :IN_POOL0], kr, d[:, IN_POOL0:IN_KR0]], axis=1)


def _pad_heads(w, width):
    k = w.shape[0]
    w = w.reshape(k, N_HEADS, width)
    return jnp.pad(w, ((0, 0), (0, 0), (0, HEAD_PAD - width))).reshape(k, N_HEADS * HEAD_PAD)


def _unpad_heads(d, width):
    k = d.shape[0]
    return d.reshape(k, N_HEADS, HEAD_PAD)[:, :, :width]


def kernel(x, p, positions, g_pre_mix, w_in, b_gate, g_q, w_uq, g_kv, w_ukv, w_pool, pool_scale, w_branch_attn, w_branch_pool, w_out, g_post_mix, g_pre_mlp, w_ff1, w_ff2, g_post_mlp, w_ple_proj, w_ple_gate, g_ple, loss_target, m_g_pre_mix, m_w_in, m_b_gate, m_g_q, m_w_uq, m_g_kv, m_w_ukv, m_w_pool, m_pool_scale, m_w_branch_attn, m_w_branch_pool, m_w_out, m_g_post_mix, m_g_pre_mlp, m_w_ff1, m_w_ff2, m_g_post_mlp, m_w_ple_proj, m_w_ple_gate, m_g_ple, v_g_pre_mix, v_w_in, v_b_gate, v_g_q, v_w_uq, v_g_kv, v_w_ukv, v_w_pool, v_pool_scale, v_w_branch_attn, v_w_branch_pool, v_w_out, v_g_post_mix, v_g_pre_mlp, v_w_ff1, v_w_ff2, v_g_post_mlp, v_w_ple_proj, v_w_ple_gate, v_g_ple):
    given = dict(locals())
    weights = {n: given[n] for n in WEIGHT_ORDER}
    moments_m = {n: given["m_" + n] for n in WEIGHT_ORDER}
    moments_v = {n: given["v_" + n] for n in WEIGHT_ORDER}
    xs, ps, target = x[0], p[0, 0], loss_target[0]
    seq = xs.shape[0]
    tm = min(256, seq)
    tm_merge = min(512, seq)
    core = lax.axis_index("c")
    chip = 2 * lax.axis_index("x") + lax.axis_index("y")

    early, later = SHARDED[:N_EARLY], SHARDED[N_EARLY:]
    shapes_of = lambda entries: [weights[n].shape[1:] for n, _ in entries]
    pack_bf16 = lambda entries: _pack_rows([weights[n][0].astype(BF16) for n, _ in entries])
    full = _full_from_gathered(_all_gather_hbm(pack_bf16(early)), early, shapes_of(early))
    w_in_r = _arrange_w_in(full["w_in"])
    w_uq_r = _pad_heads(full["w_uq"], QK_NOPE + QK_ROPE)
    ukv = full["w_ukv"].reshape(KV_LORA, N_HEADS, QK_NOPE + V_HEAD)
    w_k_exp = _pad_heads(ukv[:, :, :QK_NOPE].reshape(KV_LORA, N_HEADS * QK_NOPE), QK_NOPE)
    w_v = _pad_heads(ukv[:, :, QK_NOPE:].reshape(KV_LORA, N_HEADS * V_HEAD), V_HEAD)
    w_pool_bf = w_pool[0].astype(BF16)

    packed_later = [e for e in later if e[0] != "w_ff1"]
    a_bf, qd, kvd, d_pool, pooled, gates, q, k, v, k_t, cf, s1, s2, gathered_later, gathered_ff1 = _proj_fwd(
        xs, g_pre_mix, b_gate, g_q, g_kv, positions, w_pool_bf, pool_scale, w_in_r, w_uq_r, w_k_exp, w_v,
        [pack_bf16(packed_later), w_ff1[0].astype(BF16)], tm_merge)
    full.update(_full_from_gathered(gathered_later, packed_later, shapes_of(packed_later)))
    full["w_ff1"] = gathered_ff1
    w_ba = jnp.pad(full["w_branch_attn"].reshape(N_HEADS, V_HEAD, D_MODEL),
                   ((0, 0), (0, HEAD_PAD - V_HEAD), (0, 0))).reshape(N_HEADS * HEAD_PAD, D_MODEL)
    o_heads, lse = _attn_fwd(q, k, v)
    merged, ba, bb, y, h1, attn_rows = _merge_fwd(o_heads, pooled, gates, xs, g_post_mix, w_ba,
                                                  full["w_branch_pool"], full["w_out"], tm_merge)
    (m_bf, zr, f, h2_bf, p_bf, de, dpre, dh2, loss_acc, dg_ple) = _tail_fwd(
        h1, target, ps, g_pre_mlp, g_post_mlp, g_ple, full["w_ff1"], full["w_ff2"], full["w_ple_proj"],
        full["w_ple_gate"], tm)

    by_device, payload = {}, {}

    def keep(name, pair, layout=lambda g: g):
        by_device[name], payload[name] = layout(pair[0]), layout(pair[1])

    keep("w_ple_proj", _grad_w(p_bf, de, "grad_w_ple_proj", by_device=True, with_bf16=True))
    keep("w_ple_gate", _grad_w(h2_bf, dpre, "grad_w_ple_gate", with_bf16=True), _rows_by_device)
    df, dz, dh1, dg_pre_mlp, dg_post_mlp = _mlp_bwd(h1, f, zr, dh2, g_pre_mlp, g_post_mlp, full["w_ff1"],
                                                    full["w_ff2"], tm)
    keep("w_ff1", _grad_w(m_bf, dz, "grad_w_ff1", by_device=True, with_bf16=True))
    keep("w_ff2", _grad_w(zr, df, "grad_w_ff2", square_a=True, with_bf16=True), _rows_by_device)
    (dy, dba, dbb, dgpre, do_heads, dd, delta, dg_post_mix, db_gate, dpool_scale, dw_pool) = _merge_bwd(
        dh1, y, gates, ba, bb, d_pool, o_heads, g_post_mix, pool_scale, full["w_out"], w_ba,
        full["w_branch_pool"], w_pool_bf, tm_merge)
    keep("w_branch_attn", _grad_w(attn_rows, dba, "grad_w_branch_attn", with_bf16=True),
          lambda g: _columns_by_device(g.reshape(N_HEADS, HEAD_PAD, D_MODEL)[:, :V_HEAD].reshape(-1, D_MODEL)))
    keep("w_branch_pool", _grad_w(pooled, dbb, "grad_w_branch_pool", by_device=True, with_bf16=True))
    keep("w_out", _grad_w(merged, dy, "grad_w_out", with_bf16=True), _rows_by_device)
    dpin = _pool_bwd_window(dd, tm_merge)
    direct = [n for n, _ in SHARDED[N_EARLY:]]
    outs = _attn_bwd(q, k, k_t, v, do_heads, lse, delta, [payload[n] for n in direct])
    dq, dk, dv = outs[:3]
    received = dict(zip(direct, outs[3:]))
    (grad_x, dproj, dq_bf, qn_bf, kvn_bf, dkv_bf, dg_pre_mix, dg_q, dg_kv) = _proj_bwd(
        dq, dk, dv, qd, kvd, xs, dh1, dgpre, dpin, cf, s1, s2, g_pre_mix, g_q, g_kv, w_in_r, w_uq_r, w_k_exp, w_v,
        tm_merge)
    d_w_kv = _grad_w(kvn_bf, dkv_bf, "grad_w_ukv")
    d_k_exp = _unpad_heads(d_w_kv[:, :K_ALL], QK_NOPE)
    d_w_v = _unpad_heads(d_w_kv[:, K_ALL:], V_HEAD)
    by_device["w_in"] = _columns_by_device(_restore_w_in(_grad_w(a_bf, dproj, "grad_w_in")))
    by_device["w_uq"] = _columns_by_device(
        _unpad_heads(_grad_w(qn_bf, dq_bf, "grad_w_uq"), QK_NOPE + QK_ROPE).reshape(Q_LORA, -1))
    by_device["w_ukv"] = _columns_by_device(jnp.concatenate([d_k_exp, d_w_v], axis=2).reshape(KV_LORA, -1))
    grads_small = {
        "g_pre_mix": dg_pre_mix, "b_gate": db_gate, "g_q": dg_q, "g_kv": dg_kv,
        "w_pool": dw_pool, "pool_scale": dpool_scale, "g_post_mix": dg_post_mix,
        "g_pre_mlp": dg_pre_mlp, "g_post_mlp": dg_post_mlp, "g_ple": dg_ple,
    }

    names = [n for n, _ in SHARDED]
    place = jnp.stack([2 * chip + core, chip, core]).astype(jnp.int32)
    sharded = {n: _adamw_direct(by_device[n], received[n], place, weights[n], moments_m[n], moments_v[n],
                                "adamw_" + n) for n in direct}
    last = [n for n, _ in SHARDED[:N_EARLY]]
    own = [by_device[n] for n in last]
    from_sibling = _exchange_pair(own)
    pair = [_pair_sum(g, r, place, "pair_sum_" + n) for n, g, r in zip(last, own, from_sibling)]
    from_chips = _exchange_chips(pair)
    sharded.update({n: _adamw_sharded(g, r, rc, place, weights[n], moments_m[n], moments_v[n], "adamw_" + n)
                    for n, g, r, rc in zip(last, own, from_sibling, from_chips)})

    flat = lambda a: a.reshape(a.shape[-3:]) if a.ndim > 3 else a
    g_sm, loss_sum, d_sm, m_sm, v_sm = _replicated_update(
        [flat(grads_small[n]) for n in REPLICATED], loss_acc, [flat(weights[n]) for n in REPLICATED],
        [flat(moments_m[n]) for n in REPLICATED], [flat(moments_v[n]) for n in REPLICATED])

    results = []
    for which, small in enumerate((g_sm, d_sm, m_sm, v_sm)):
        named = {n: sharded[n][which] for n in names}
        named.update({n: a.reshape(weights[n].shape) for n, a in zip(REPLICATED, small)})
        results.append([named[n] for n in WEIGHT_ORDER])

    return (loss_sum[0, 0], grad_x[None], *results[0], *results[1], *results[2], *results[3])
```

```python
import jax
import jax.numpy as jnp
from jax import lax
from jax.experimental import pallas as pl
from jax.experimental.pallas import tpu as pltpu

F32 = jnp.float32
BF16 = jnp.bfloat16

D_MODEL = 1024
PLE_DIM = 256
N_HEADS = 8
QK_NOPE = 64
QK_ROPE = 32
V_HEAD = 64
Q_LORA = 384
KV_LORA = 256
POOL_WINDOWS = (2, 4, 8, 16)
POOL_GROUP = 128
POOL_WIDTH = 512
D_FF = 4096
ROPE_THETA = 10000.0
EPS = 1e-6
HEAD_PAD = 128
K_ALL = N_HEADS * HEAD_PAD
ATTN_SCALE = (QK_NOPE + QK_ROPE) ** -0.5
LOG2E = 1.4426950408889634
Q_PRESCALE = ATTN_SCALE * LOG2E
ATTN_TILE = 512
FWD_ROWS = 512
FWD_CHAINS = 8
BWD_CHAINS = 2
BWD_QUERIES = 2

ADAM_LR = 0.001
ADAM_B1 = 0.9
ADAM_B2 = 0.999
ADAM_EPS = 1e-08
ADAM_WD = 0.01
ADAM_STEP = 10

N_DEV = 8
LANES = 1024
PACK_ROW_TILE = 480
POOL_HALO = 16
MIB = 2 ** 20

IN_Q0, IN_KV0, IN_POOL0, IN_GATE0, IN_KR0, IN_R = 0, 384, 640, 1152, 3200, 3328

SHARDED = (("w_in", "col"), ("w_uq", "col"), ("w_ukv", "col"), ("w_branch_attn", "col"),
           ("w_branch_pool", "col"), ("w_out", "row"), ("w_ff1", "col"), ("w_ff2", "row"),
           ("w_ple_proj", "col"), ("w_ple_gate", "row"))
N_EARLY = 3
REPLICATED = ("g_pre_mix", "b_gate", "g_q", "g_kv", "w_pool", "pool_scale", "g_post_mix",
              "g_pre_mlp", "g_post_mlp", "g_ple")
WEIGHT_ORDER = ("g_pre_mix", "w_in", "b_gate", "g_q", "w_uq", "g_kv", "w_ukv", "w_pool", "pool_scale",
                "w_branch_attn", "w_branch_pool", "w_out", "g_post_mix", "g_pre_mlp", "w_ff1", "w_ff2",
                "g_post_mlp", "w_ple_proj", "w_ple_gate", "g_ple")

NT = (((1,), (1,)), ((), ()))
TN = (((0,), (0,)), ((), ()))
MESH = pl.DeviceIdType.MESH
ANY = pl.BlockSpec(memory_space=pl.ANY)


def _params(n_axes, vmem_mib):
    return pltpu.CompilerParams(dimension_semantics=("arbitrary",) * n_axes, vmem_limit_bytes=vmem_mib * MIB)


def _row(tm, n):
    return pl.BlockSpec((tm, n), lambda i: (i, 0))


def _fix(shape):
    zeros = (0,) * len(shape)
    return pl.BlockSpec(shape, lambda i: zeros)


def _sds(shape, dtype):
    return jax.ShapeDtypeStruct(shape, dtype)


def _rms_r(v):
    return lax.rsqrt(jnp.mean(v * v, axis=-1, keepdims=True) + EPS)


def _rms_bwd(vhat, r, g, dy):
    gdy = dy * g
    return r * (gdy - vhat * jnp.mean(gdy * vhat, axis=-1, keepdims=True))


def _colsum(v):
    return jnp.sum(v, axis=0, keepdims=True)


def _sigmoid(v):
    return 1.0 / (1.0 + jnp.exp(-v))


def _mm(a, b):
    return jnp.dot(a, b, preferred_element_type=F32)


def _mm_nt(a, b):
    return lax.dot_general(a, b, NT, preferred_element_type=F32)


def _rope(c, cf, s1, s2):
    return c * cf + pltpu.roll(c, HEAD_PAD - 16, 1) * s1 + pltpu.roll(c, 16, 1) * s2


def _rope_t(c, cf, s1, s2):
    return c * cf + pltpu.roll(c * s1, 16, 1) + pltpu.roll(c * s2, HEAD_PAD - 16, 1)


def _row_chains(tm, rows=256):
    rows = min(rows, tm)
    return [slice(c * rows, (c + 1) * rows) for c in range(tm // rows)]


def _load_once(pairs, sems):
    @pl.when(pl.program_id(0) == 0)
    def _():
        copies = [pltpu.make_async_copy(src, dst, sems.at[i]) for i, (src, dst) in enumerate(pairs)]
        for cp in copies:
            cp.start()
        for cp in copies:
            cp.wait()


def _column_blocks(by_device_hbm, full_vmem):
    n = by_device_hbm.shape[2]
    return tuple((by_device_hbm.at[d], full_vmem.at[:, d * n:(d + 1) * n]) for d in range(N_DEV))


def _proj_fwd(x, g_pre, b_gate, g_q, g_kv, positions, w_pool_bf, pool_scale, w_in_r, w_uq_r, w_k_exp, w_v,
              later_shards, tm):
    seq = x.shape[0]
    n_steps = seq // tm
    forward_step = (3 * n_steps) // 4
    n_later = len(later_shards)

    def body(x_ref, gpre_ref, bg_ref, gq_ref, gkv_ref, pos_ref, freq_ref, wpool_ref, pscale_ref,
             win_hbm, wuq_hbm, wk_hbm, wv_hbm, *rest):
        later_refs, rest = rest[:n_later], rest[n_later:]
        (a_ref, qd_ref, kvd_ref, dpool_ref, pooled_ref, gates_ref, q_ref, k_ref, v_ref, kt_ref,
         cf_ref, s1_ref, s2_ref) = rest[:13]
        gathered_refs, rest = rest[13:13 + n_later], rest[13 + n_later:]
        win, wuq, wk, wv, halo_ref, send_sems, recv_sems, local_sems, load_sems = rest
        step = pl.program_id(0)

        def gather(phase):
            for a, (src, dst) in enumerate(zip(later_refs, gathered_refs)):
                _gather_copies(src, lambda px, py, pc, dst=dst: dst.at[4 * px + 2 * py + pc],
                               send_sems, recv_sems, local_sems.at[a], phases=(phase,), sem_base=7 * a)

        pl.when(step == 0)(lambda: gather("send"))
        pl.when(step == forward_step)(lambda: gather("forward"))
        _load_once(((win_hbm, win), (wuq_hbm, wuq), (wk_hbm, wk), (wv_hbm, wv)), load_sems)
        for rows in _row_chains(tm):
            n_rows = rows.stop - rows.start
            xv = x_ref[rows, :]
            a = (xv * _rms_r(xv) * gpre_ref[...]).astype(BF16)
            a_ref[rows, :] = a
            proj = _mm(a, win[...])
            qd = proj[:, IN_Q0:IN_KV0]
            kvd = proj[:, IN_KV0:IN_POOL0]
            qd_ref[rows, :] = qd
            kvd_ref[rows, :] = kvd
            gates_ref[rows, :] = _sigmoid(proj[:, IN_GATE0:IN_KR0] + bg_ref[...]).astype(BF16)

            u = proj[:, IN_POOL0:IN_GATE0]
            before = jnp.where(step == 0, 0.0, halo_ref[...]) if rows.start == 0 else tail
            tail = u[n_rows - POOL_HALO:, :]
            level = jnp.concatenate([before, u], axis=0)
            counts = _window_count(step * tm + rows.start, n_rows)
            shift = 1
            for g in range(len(POOL_WINDOWS)):
                level = level + pltpu.roll(level, shift, 0)
                shift *= 2
                lanes = slice(POOL_GROUP * g, POOL_GROUP * (g + 1))
                d = (level[POOL_HALO:, lanes] / counts[g] - u[:, lanes]).astype(BF16)
                dpool_ref[rows, lanes] = d
                pooled_ref[rows, lanes] = (_mm(d, wpool_ref[g]) * pscale_ref[:, lanes]).astype(BF16)
            if rows.stop == tm:
                halo_ref[...] = tail
            cfv, s1v, s2v = _rope_tables(pos_ref[:, rows], freq_ref[...], n_rows)
            cf_ref[rows, :], s1_ref[rows, :], s2_ref[rows, :] = cfv, s1v, s2v
            krr = _rope(proj[:, IN_KR0:IN_R], cfv, s1v, s2v)
            qn = (qd * _rms_r(qd) * gq_ref[...]).astype(BF16)
            qf = _mm(qn, wuq[...])
            kvn = (kvd * _rms_r(kvd) * gkv_ref[...]).astype(BF16)
            kf = _mm(kvn, wk[...])
            vf = _mm(kvn, wv[...])
            one_lane = (lax.broadcasted_iota(jnp.int32, (n_rows, HEAD_PAD), 1) == V_HEAD).astype(F32)
            for h in range(N_HEADS):
                lanes = slice(HEAD_PAD * h, HEAD_PAD * (h + 1))
                q_ref[h, rows, :] = (_rope(qf[:, lanes], cfv, s1v, s2v) * Q_PRESCALE).astype(BF16)
                kh = kf[:, lanes] + krr
                vh = vf[:, lanes] + one_lane
                k_ref[h, rows, :] = kh.astype(BF16)
                v_ref[h, rows, :] = vh.astype(BF16)
                kt_ref[h, :, rows] = jnp.transpose(kh).astype(BF16)
        pl.when(step == n_steps - 1)(lambda: gather("finish"))

    per_tile = ATTN_TILE // tm
    heads = pl.BlockSpec((N_HEADS, tm, HEAD_PAD), lambda i: (0, i, 0))
    heads_t = pl.BlockSpec((N_HEADS, None, HEAD_PAD, tm), lambda i: (0, i // per_tile, 0, i % per_tile))
    heads_t_shape = _sds((N_HEADS, seq // ATTN_TILE, HEAD_PAD, ATTN_TILE), BF16)
    return pl.pallas_call(
        body, name="proj_fwd", grid=(seq // tm,),
        in_specs=[_row(tm, D_MODEL), _fix((1, D_MODEL)), _fix((1, 2 * D_MODEL)), _fix((1, Q_LORA)), _fix((1, KV_LORA)),
                  pl.BlockSpec((1, tm), lambda i: (0, i)), _fix((1, HEAD_PAD)), _fix(w_pool_bf.shape), _fix((1, POOL_WIDTH)),
                  ANY, ANY, ANY, ANY] + [ANY] * n_later,
        out_specs=[_row(tm, D_MODEL), _row(tm, Q_LORA), _row(tm, KV_LORA), _row(tm, POOL_WIDTH), _row(tm, POOL_WIDTH),
                   _row(tm, 2 * D_MODEL), heads, heads, heads, heads_t,
                   _row(tm, HEAD_PAD), _row(tm, HEAD_PAD), _row(tm, HEAD_PAD)] + [ANY] * n_later,
        out_shape=[_sds((seq, D_MODEL), BF16), _sds((seq, Q_LORA), F32), _sds((seq, KV_LORA), F32),
                   _sds((seq, POOL_WIDTH), BF16), _sds((seq, POOL_WIDTH), BF16), _sds((seq, 2 * D_MODEL), BF16),
                   _sds((N_HEADS, seq, HEAD_PAD), BF16), _sds((N_HEADS, seq, HEAD_PAD), BF16),
                   _sds((N_HEADS, seq, HEAD_PAD), BF16), heads_t_shape,
                   _sds((seq, HEAD_PAD), F32), _sds((seq, HEAD_PAD), F32), _sds((seq, HEAD_PAD), F32)]
                  + [_sds((N_DEV,) + s.shape, s.dtype) for s in later_shards],
        scratch_shapes=[pltpu.VMEM(w_in_r.shape, BF16), pltpu.VMEM(w_uq_r.shape, BF16),
                        pltpu.VMEM(w_k_exp.shape, BF16), pltpu.VMEM(w_v.shape, BF16),
                        pltpu.VMEM((POOL_HALO, POOL_WIDTH), F32),
                        pltpu.SemaphoreType.DMA((7 * n_later,)), pltpu.SemaphoreType.DMA((7 * n_later,)),
                        pltpu.SemaphoreType.DMA((n_later,)), pltpu.SemaphoreType.DMA((4,))],
        compiler_params=_params(1, 48),
    )(x, g_pre, b_gate, g_q, g_kv, positions, _rope_lane_frequencies(), w_pool_bf, pool_scale,
      w_in_r, w_uq_r, w_k_exp, w_v, *later_shards)


def _window_count(row0, n_rows):
    t = row0 + lax.broadcasted_iota(jnp.int32, (n_rows, POOL_GROUP), 0)
    return [jnp.minimum(t + 1, w).astype(F32) for w in POOL_WINDOWS]


def _col_to_row(col, n):
    return jnp.transpose(jnp.broadcast_to(col, (n, HEAD_PAD)))[0:1, :]


def _attn_fwd(q, k, v):
    heads, seq, _ = q.shape
    r = FWD_ROWS
    n = min(FWD_CHAINS, seq // r)
    block = r * n

    def body(q_ref, k_ref, v_ref, o_ref, lse_ref):
        qi = pl.program_id(1)
        q_tiles = [q_ref[c * r:(c + 1) * r, :] for c in range(n)]

        def tile(qt, j, m, acc, diagonal):
            start = pl.multiple_of(j * r, r)
            s = _mm_nt(qt, k_ref[pl.ds(start, r), :])
            if diagonal:
                row = lax.broadcasted_iota(jnp.int32, (r, r), 0)
                col = lax.broadcasted_iota(jnp.int32, (r, r), 1)
                s = jnp.where(col <= row, s, -jnp.inf)
            m_new = jnp.maximum(m, jnp.max(s, axis=1, keepdims=True))
            p = jnp.exp2((s - m_new).astype(BF16))
            acc = jnp.exp2(m - m_new) * acc + _mm(p, v_ref[pl.ds(start, r), :])
            return m_new, acc

        def all_chains(jj, carry):
            for u in range(n):
                carry = tuple(tile(q_tiles[c], n * jj + u, *carry[c], False) for c in range(n))
            return carry

        init = tuple((jnp.full((r, 1), -jnp.inf, F32), jnp.zeros((r, HEAD_PAD), F32)) for _ in range(n))
        state = list(lax.fori_loop(0, qi, all_chains, init))
        for d in range(n):
            for c in range(d, n):
                state[c] = tile(q_tiles[c], n * qi + d, *state[c], c == d)
        for c, (m, acc) in enumerate(state):
            l = acc[:, V_HEAD:V_HEAD + 1]
            o_ref[c * r:(c + 1) * r, :] = (acc / l).astype(BF16)
            row0 = c * r
            lse_ref[row0 // ATTN_TILE, :, row0 % ATTN_TILE:row0 % ATTN_TILE + r] = _col_to_row(m + jnp.log2(l), r)

    return pl.pallas_call(
        body, name="attn_fwd", grid=(heads, seq // block),
        in_specs=[pl.BlockSpec((None, block, HEAD_PAD), lambda h, i: (h, i, 0)),
                  pl.BlockSpec((None, seq, HEAD_PAD), lambda h, i: (h, 0, 0)),
                  pl.BlockSpec((None, seq, HEAD_PAD), lambda h, i: (h, 0, 0))],
        out_specs=[pl.BlockSpec((None, block, HEAD_PAD), lambda h, i: (h, i, 0)),
                   pl.BlockSpec((None, block // ATTN_TILE, 1, ATTN_TILE), lambda h, i: (h, i, 0, 0))],
        out_shape=[_sds((heads, seq, HEAD_PAD), BF16), _sds((heads, seq // ATTN_TILE, 1, ATTN_TILE), F32)],
        compiler_params=_params(2, 48),
    )(q, k, v)


def _peer_copies(src_refs, dst_refs, send_sems, recv_sems):
    x, y, c = _position()
    copies = []
    for w, (src, dst) in enumerate(zip(src_refs, dst_refs)):
        for r in range(1, N_DEV):
            px = 1 - x if r & 4 else x
            py = 1 - y if r & 2 else y
            pc = 1 - c if r & 1 else c
            copies.append(pltpu.make_async_remote_copy(
                src_ref=src.at[4 * px + 2 * py + pc], dst_ref=dst.at[r - 1],
                send_sem=send_sems.at[(N_DEV - 1) * w + r - 1], recv_sem=recv_sems.at[(N_DEV - 1) * w + r - 1],
                device_id=(px, py, pc), device_id_type=MESH))
    return copies


def _attn_bwd(q, k, k_t, v, do, lse, delta, early_grads):
    heads, seq, _ = q.shape
    t = ATTN_TILE
    nq = seq // t
    n = min(BWD_CHAINS, nq)
    n_w = len(early_grads)

    def body(q_ref, k_ref, kt_ref, v_ref, do_ref, lse_ref, delta_ref, *rest):
        grad_refs, rest = rest[:n_w], rest[n_w:]
        dq_ref, dk_ref, dv_ref = rest[:3]
        recv_refs, (send_sems, recv_sems) = rest[3:3 + n_w], rest[3 + n_w:]
        jp = pl.program_id(1)
        head = pl.program_id(0)

        @pl.when((head == 0) & (jp == 0))
        def _():
            for cp in _peer_copies(grad_refs, recv_refs, send_sems, recv_sems):
                cp.start()

        @pl.when(jp == 0)
        def _():
            dq_ref[...] = jnp.zeros_like(dq_ref)

        keys = [k_ref[c * t:(c + 1) * t, :] for c in range(n)]
        values = [v_ref[c * t:(c + 1) * t, :] for c in range(n)]

        def tile(c, i, dk, dv, diagonal):
            start = pl.multiple_of(i * t, t)
            qt = q_ref[pl.ds(start, t), :]
            dot = do_ref[pl.ds(start, t), :]
            p_t = jnp.exp2(_mm_nt(keys[c], qt) - lse_ref[i])
            if diagonal:
                key = lax.broadcasted_iota(jnp.int32, (t, t), 0)
                query = lax.broadcasted_iota(jnp.int32, (t, t), 1)
                p_t = jnp.where(key <= query, p_t, 0.0)
            dv = dv + _mm(p_t.astype(BF16), dot)
            ds_t = (p_t * (_mm_nt(values[c], dot) - delta_ref[i])).astype(BF16)
            dk = dk + _mm(ds_t, qt)
            return dk, dv, _mm(kt_ref[c], ds_t)

        def query_tile(i, state, first_rows):
            dq = None
            for c in range(n if first_rows is None else first_rows + 1):
                dk, dv, dq_c = tile(c, i, *state[c], diagonal=(c == first_rows))
                state[c] = (dk, dv)
                dq = dq_c if dq is None else dq + dq_c
            dq_ref[i] += dq

        def passes(ip, carry):
            state = list(carry)
            for u in range(BWD_QUERIES):
                query_tile(BWD_QUERIES * ip + u, state, None)
            return tuple(state)

        zero = jnp.zeros((t, HEAD_PAD), F32)
        state = [(zero, zero)] * n
        for offset in range(n):
            query_tile(n * jp + offset, state, offset)
        first_pass = (n * (jp + 1)) // BWD_QUERIES
        state = lax.fori_loop(first_pass, nq // BWD_QUERIES, passes, tuple(state))
        for c, (dk, dv) in enumerate(state):
            dk_ref[c * t:(c + 1) * t, :] = (dk * (1.0 / LOG2E)).astype(BF16)
            dv_ref[c * t:(c + 1) * t, :] = dv.astype(BF16)

        @pl.when((head == heads - 1) & (jp == nq // n - 1))
        def _():
            copies = _peer_copies(grad_refs, recv_refs, send_sems, recv_sems)
            for cp in copies:
                cp.wait_recv()
            for cp in copies:
                cp.wait_send()

    whole = pl.BlockSpec((None, seq, HEAD_PAD), lambda h, j: (h, 0, 0))
    whole_t = pl.BlockSpec((None, nq, HEAD_PAD, t), lambda h, j: (h, 0, 0, 0))
    pair = pl.BlockSpec((None, n * t, HEAD_PAD), lambda h, j: (h, j, 0))
    pair_t = pl.BlockSpec((None, n, HEAD_PAD, t), lambda h, j: (h, j, 0, 0))
    stats = pl.BlockSpec((None, nq, 1, t), lambda h, j: (h, 0, 0, 0))
    return pl.pallas_call(
        body, name="attn_bwd", grid=(heads, nq // n),
        in_specs=[whole, pair, pair_t, pair, whole, stats, stats] + [ANY] * n_w,
        out_specs=[whole_t, pair, pair] + [ANY] * n_w,
        out_shape=[_sds((heads, nq, HEAD_PAD, t), F32), _sds((heads, seq, HEAD_PAD), BF16),
                   _sds((heads, seq, HEAD_PAD), BF16)]
                  + [_sds((N_DEV - 1,) + g.shape[1:], g.dtype) for g in early_grads],
        scratch_shapes=[pltpu.SemaphoreType.DMA(((N_DEV - 1) * n_w,)), pltpu.SemaphoreType.DMA(((N_DEV - 1) * n_w,))],
        compiler_params=_params(2, 56),
    )(q, k, k_t, v, do, lse, delta, *early_grads)


def _merge_fwd(attn, pooled, gates, x, g_post_mix, w_ba, w_bb, w_out, tm):
    seq = x.shape[0]

    def body(attn_ref, pooled_ref, gates_ref, x_ref, g_ref, wba_ref, wbb_ref, wout_ref,
             merged_ref, ba_ref, bb_ref, y_ref, h1_ref, attn_rows_ref):
        for rows in _row_chains(tm):
            attn = jnp.concatenate([attn_ref[h, rows, :] for h in range(N_HEADS)], axis=1)
            attn_rows_ref[rows, :] = attn
            ba = _mm(attn, wba_ref[...])
            bb = _mm(pooled_ref[rows, :], wbb_ref[...])
            ba_ref[rows, :] = ba.astype(BF16)
            bb_ref[rows, :] = bb.astype(BF16)
            merged = (gates_ref[rows, :D_MODEL].astype(F32) * ba
                      + gates_ref[rows, D_MODEL:].astype(F32) * bb).astype(BF16)
            merged_ref[rows, :] = merged
            y = _mm(merged, wout_ref[...])
            y_ref[rows, :] = y
            h1_ref[rows, :] = x_ref[rows, :] + y * _rms_r(y) * g_ref[...]

    return pl.pallas_call(
        body, name="merge_fwd", grid=(seq // tm,),
        in_specs=[pl.BlockSpec((N_HEADS, tm, HEAD_PAD), lambda i: (0, i, 0)), _row(tm, POOL_WIDTH),
                  _row(tm, 2 * D_MODEL), _row(tm, D_MODEL),
                  _fix((1, D_MODEL)), _fix(w_ba.shape), _fix(w_bb.shape), _fix(w_out.shape)],
        out_specs=[_row(tm, D_MODEL)] * 5 + [_row(tm, N_HEADS * HEAD_PAD)],
        out_shape=[_sds((seq, D_MODEL), BF16), _sds((seq, D_MODEL), BF16), _sds((seq, D_MODEL), BF16),
                   _sds((seq, D_MODEL), F32), _sds((seq, D_MODEL), F32), _sds((seq, N_HEADS * HEAD_PAD), BF16)],
        compiler_params=_params(1, 48),
    )(attn, pooled, gates, x, g_post_mix, w_ba, w_bb, w_out)


def _tail_fwd(h1, target, p, g_pre_mlp, g_post_mlp, g_ple, w_ff1, w_ff2, w_pe, w_pg, tm):
    seq = h1.shape[0]

    def body(h1_ref, tgt_ref, p_ref, gm_ref, gf_ref, gp_ref, w1_hbm, w2_hbm, wpe_hbm, wpg_hbm,
             m_ref, zr_ref, f_ref, h2b_ref, pb_ref, de_ref, dpre_ref, dh2_ref, loss_ref, dgple_ref,
             w1, w2, wpe, wpg, load_sems):
        _load_once(_column_blocks(w1_hbm, w1) + ((w2_hbm, w2), (wpe_hbm, wpe), (wpg_hbm, wpg)), load_sems)

        @pl.when(pl.program_id(0) == 0)
        def _():
            loss_ref[...] = jnp.zeros_like(loss_ref)
            dgple_ref[...] = jnp.zeros_like(dgple_ref)

        h1v = h1_ref[...]
        m = (h1v * _rms_r(h1v) * gm_ref[...]).astype(BF16)
        m_ref[...] = m
        zr = jnp.maximum(_mm(m, w1[...]), 0.0)
        zr_ref[...] = zr.astype(BF16)
        f = _mm((zr * zr).astype(BF16), w2[...])
        f_ref[...] = f
        h2 = h1v + f * _rms_r(f) * gf_ref[...]
        h2b = h2.astype(BF16)
        h2b_ref[...] = h2b
        pb = p_ref[...].astype(BF16)
        pb_ref[...] = pb
        e = _mm(pb, wpe[...])
        pg = _sigmoid(_mm(h2b, wpg[...]))
        t3 = pg * e
        r3 = _rms_r(t3)
        t3hat = t3 * r3
        diff = h2 + t3hat * gp_ref[...] - tgt_ref[...]
        loss_ref[...] += jnp.sum(diff * diff) * (0.5 / D_MODEL)
        dh3 = diff * (1.0 / D_MODEL)
        dgple_ref[...] += _colsum(dh3 * t3hat)
        dt3 = _rms_bwd(t3hat, r3, gp_ref[...], dh3)
        de_ref[...] = (dt3 * pg).astype(BF16)
        dpre = (dt3 * e * pg * (1.0 - pg)).astype(BF16)
        dpre_ref[...] = dpre
        dh2_ref[...] = dh3 + _mm_nt(dpre, wpg[...])

    return pl.pallas_call(
        body, name="tail_fwd", grid=(seq // tm,),
        in_specs=[_row(tm, D_MODEL), _row(tm, D_MODEL), _row(tm, PLE_DIM), _fix((1, D_MODEL)), _fix((1, D_MODEL)),
                  _fix((1, D_MODEL)), ANY, ANY, ANY, ANY],
        out_specs=[_row(tm, D_MODEL), _row(tm, D_FF), _row(tm, D_MODEL), _row(tm, D_MODEL), _row(tm, PLE_DIM),
                   _row(tm, D_MODEL), _row(tm, D_MODEL), _row(tm, D_MODEL), _fix((8, 128)), _fix((1, D_MODEL))],
        out_shape=[_sds((seq, D_MODEL), BF16), _sds((seq, D_FF), BF16), _sds((seq, D_MODEL), F32),
                   _sds((seq, D_MODEL), BF16), _sds((seq, PLE_DIM), BF16), _sds((seq, D_MODEL), BF16),
                   _sds((seq, D_MODEL), BF16), _sds((seq, D_MODEL), F32), _sds((8, 128), F32), _sds((1, D_MODEL), F32)],
        scratch_shapes=[pltpu.VMEM((w_ff1.shape[1], N_DEV * w_ff1.shape[2]), BF16), pltpu.VMEM(w_ff2.shape, BF16),
                        pltpu.VMEM(w_pe.shape, BF16), pltpu.VMEM(w_pg.shape, BF16),
                        pltpu.SemaphoreType.DMA((N_DEV + 3,))],
        compiler_params=_params(1, 56),
    )(h1, target, p, g_pre_mlp, g_post_mlp, g_ple, w_ff1, w_ff2, w_pe, w_pg)


def _mlp_bwd(h1, f, zr, dh2, g_pre_mlp, g_post_mlp, w_ff1, w_ff2, tm):
    seq = h1.shape[0]

    def body(h1_ref, f_ref, zr_ref, dh2_ref, gm_ref, gf_ref, w1_hbm, w2_hbm,
             df_ref, dz_ref, dh1_ref, dgm_ref, dgf_ref, w1, w2, load_sems):
        _load_once(_column_blocks(w1_hbm, w1) + ((w2_hbm, w2),), load_sems)

        @pl.when(pl.program_id(0) == 0)
        def _():
            dgm_ref[...] = jnp.zeros_like(dgm_ref)
            dgf_ref[...] = jnp.zeros_like(dgf_ref)

        dh2 = dh2_ref[...]
        fv = f_ref[...]
        rf = _rms_r(fv)
        fhat = fv * rf
        dgf_ref[...] += _colsum(dh2 * fhat)
        df = _rms_bwd(fhat, rf, gf_ref[...], dh2).astype(BF16)
        df_ref[...] = df
        dz = (_mm_nt(df, w2[...]) * (2.0 * zr_ref[...].astype(F32))).astype(BF16)
        dz_ref[...] = dz
        dm = _mm_nt(dz, w1[...])
        h1v = h1_ref[...]
        r1 = _rms_r(h1v)
        h1hat = h1v * r1
        dgm_ref[...] += _colsum(dm * h1hat)
        dh1_ref[...] = dh2 + _rms_bwd(h1hat, r1, gm_ref[...], dm)

    return pl.pallas_call(
        body, name="mlp_bwd", grid=(seq // tm,),
        in_specs=[_row(tm, D_MODEL), _row(tm, D_MODEL), _row(tm, D_FF), _row(tm, D_MODEL),
                  _fix((1, D_MODEL)), _fix((1, D_MODEL)), ANY, ANY],
        out_specs=[_row(tm, D_MODEL), _row(tm, D_FF), _row(tm, D_MODEL), _fix((1, D_MODEL)), _fix((1, D_MODEL))],
        out_shape=[_sds((seq, D_MODEL), BF16), _sds((seq, D_FF), BF16), _sds((seq, D_MODEL), F32),
                   _sds((1, D_MODEL), F32), _sds((1, D_MODEL), F32)],
        scratch_shapes=[pltpu.VMEM((w_ff1.shape[1], N_DEV * w_ff1.shape[2]), BF16), pltpu.VMEM(w_ff2.shape, BF16),
                        pltpu.SemaphoreType.DMA((N_DEV + 1,))],
        compiler_params=_params(1, 56),
    )(h1, f, zr, dh2, g_pre_mlp, g_post_mlp, w_ff1, w_ff2)


def _merge_bwd(dh1, y, gates, ba, bb, d_pool, o_heads, g_post_mix, pool_scale, w_out, w_ba, w_bb, w_pool_bf, tm):
    seq = dh1.shape[0]
    assert tm == ATTN_TILE

    def body(dh1_ref, y_ref, gates_ref, ba_ref, bb_ref, d_ref, o_ref, g_ref, ps_ref, wout_ref, wba_ref, wbb_ref, wp_ref,
             dy_ref, dba_ref, dbb_ref, dgpre_ref, dattn_ref, dd_ref, delta_ref, dg_ref, dbg_ref, dps_ref, dwp_ref):
        @pl.when(pl.program_id(0) == 0)
        def _():
            dg_ref[...] = jnp.zeros_like(dg_ref)
            dbg_ref[...] = jnp.zeros_like(dbg_ref)
            dps_ref[...] = jnp.zeros_like(dps_ref)
            dwp_ref[...] = jnp.zeros_like(dwp_ref)

        for rows in _row_chains(tm):
            dh1v = dh1_ref[rows, :]
            yv = y_ref[rows, :]
            r = _rms_r(yv)
            yhat = yv * r
            dg_ref[...] += _colsum(dh1v * yhat)
            dy = _rms_bwd(yhat, r, g_ref[...], dh1v).astype(BF16)
            dy_ref[rows, :] = dy
            dmerged = _mm_nt(dy, wout_ref[...])
            dbranch = []
            for half, branch_ref, dbranch_ref in ((0, ba_ref, dba_ref), (1, bb_ref, dbb_ref)):
                lanes = slice(D_MODEL * half, D_MODEL * (half + 1))
                gate = gates_ref[rows, lanes].astype(F32)
                dpre = dmerged * branch_ref[rows, :].astype(F32) * gate * (1.0 - gate)
                dbg_ref[:, lanes] += _colsum(dpre)
                dgpre_ref[rows, lanes] = dpre.astype(BF16)
                dbranch.append((dmerged * gate).astype(BF16))
                dbranch_ref[rows, :] = dbranch[-1]
            dattn = _mm_nt(dbranch[0], wba_ref[...]).astype(BF16)
            for h in range(N_HEADS):
                do_h = dattn[:, HEAD_PAD * h:HEAD_PAD * (h + 1)]
                dattn_ref[h, rows, :] = do_h
                row_term = jnp.sum(o_ref[h, rows, :].astype(F32) * do_h.astype(F32), axis=1, keepdims=True)
                delta_ref[h, :, rows] = _col_to_row(row_term, rows.stop - rows.start)
            dpooled = _mm_nt(dbranch[1], wbb_ref[...])
            for g in range(len(POOL_WINDOWS)):
                lanes = slice(POOL_GROUP * g, POOL_GROUP * (g + 1))
                dpl = dpooled[:, lanes]
                d_g = d_ref[rows, lanes]
                dps_ref[:, lanes] += _colsum(dpl * _mm(d_g, wp_ref[g]))
                dyp = (dpl * ps_ref[:, lanes]).astype(BF16)
                dwp_ref[g] += lax.dot_general(d_g, dyp, TN, preferred_element_type=F32)
                dd_ref[rows, lanes] = _mm_nt(dyp, wp_ref[g])

    heads = pl.BlockSpec((N_HEADS, tm, HEAD_PAD), lambda i: (0, i, 0))
    return pl.pallas_call(
        body, name="merge_bwd", grid=(seq // tm,),
        in_specs=[_row(tm, D_MODEL), _row(tm, D_MODEL), _row(tm, 2 * D_MODEL), _row(tm, D_MODEL), _row(tm, D_MODEL),
                  _row(tm, POOL_WIDTH), heads, _fix((1, D_MODEL)), _fix((1, POOL_WIDTH)),
                  _fix(w_out.shape), _fix(w_ba.shape), _fix(w_bb.shape), _fix(w_pool_bf.shape)],
        out_specs=[_row(tm, D_MODEL), _row(tm, D_MODEL), _row(tm, D_MODEL), _row(tm, 2 * D_MODEL),
                   heads, _row(tm, POOL_WIDTH), pl.BlockSpec((N_HEADS, None, 1, tm), lambda i: (0, i, 0, 0)),
                   _fix((1, D_MODEL)), _fix((1, 2 * D_MODEL)), _fix((1, POOL_WIDTH)), _fix(w_pool_bf.shape)],
        out_shape=[_sds((seq, D_MODEL), BF16), _sds((seq, D_MODEL), BF16), _sds((seq, D_MODEL), BF16),
                   _sds((seq, 2 * D_MODEL), BF16), _sds((N_HEADS, seq, HEAD_PAD), BF16),
                   _sds((seq, POOL_WIDTH), F32), _sds((N_HEADS, seq // tm, 1, tm), F32),
                   _sds((1, D_MODEL), F32), _sds((1, 2 * D_MODEL), F32),
                   _sds((1, POOL_WIDTH), F32), _sds(w_pool_bf.shape, F32)],
        compiler_params=_params(1, 48),
    )(dh1, y, gates, ba, bb, d_pool, o_heads, g_post_mix, pool_scale, w_out, w_ba, w_bb, w_pool_bf)


def _proj_bwd(dq, dk, dv, qd, kvd, x, dh1, dgpre, dd, cf, s1, s2, g_pre, g_q, g_kv,
              w_in_r, w_uq_r, w_k_exp, w_v, tm):
    seq = x.shape[0]
    n_steps = seq // tm

    def body(dq_ref, dk_ref, dv_ref, qd_ref, kvd_ref, x_ref, dh1_ref, dgpre_ref, dd_ref, next_ref,
             cf_ref, s1_ref, s2_ref, gpre_ref, gq_ref, gkv_ref, win_hbm, wuq_hbm, wk_hbm, wv_hbm,
             gx_ref, dproj_ref, dqb_ref, qn_ref, kvn_ref, dkvb_ref, dgpre_acc, dgq_acc, dgkv_acc,
             win, wuq, wk, wv, load_sems):
        step = pl.program_id(0)
        _load_once(((win_hbm, win), (wuq_hbm, wuq), (wk_hbm, wk), (wv_hbm, wv)), load_sems)

        @pl.when(pl.program_id(0) == 0)
        def _():
            dgpre_acc[...] = jnp.zeros_like(dgpre_acc)
            dgq_acc[...] = jnp.zeros_like(dgq_acc)
            dgkv_acc[...] = jnp.zeros_like(dgkv_acc)

        for rows in _row_chains(tm):
            n_rows = rows.stop - rows.start
            cfv, s1v, s2v = cf_ref[rows, :], s1_ref[rows, :], s2_ref[rows, :]
            ksum = jnp.zeros((n_rows, HEAD_PAD), F32)
            for h in range(N_HEADS):
                lanes = slice(HEAD_PAD * h, HEAD_PAD * (h + 1))
                dqh = jnp.transpose(dq_ref[h, :, rows])
                dqb_ref[rows, lanes] = (_rope_t(dqh, cfv, s1v, s2v) * ATTN_SCALE).astype(BF16)
                dkh = dk_ref[h, rows, :]
                dkvb_ref[rows, lanes] = dkh
                dkvb_ref[rows, slice(K_ALL + lanes.start, K_ALL + lanes.stop)] = dv_ref[h, rows, :]
                ksum = ksum + dkh.astype(F32)
            lane = lax.broadcasted_iota(jnp.int32, (n_rows, HEAD_PAD), 1)
            rope_lanes = (lane >= QK_NOPE) & (lane < QK_NOPE + QK_ROPE)
            dkr = _rope_t(jnp.where(rope_lanes, ksum, 0.0), cfv, s1v, s2v)

            qdv = qd_ref[rows, :]
            rq = _rms_r(qdv)
            qhat = qdv * rq
            qn_ref[rows, :] = (qhat * gq_ref[...]).astype(BF16)
            dqn = _mm_nt(dqb_ref[rows, :], wuq[...])
            dgq_acc[...] += _colsum(dqn * qhat)
            dproj_ref[rows, IN_Q0:IN_KV0] = _rms_bwd(qhat, rq, gq_ref[...], dqn).astype(BF16)

            kvdv = kvd_ref[rows, :]
            rkv = _rms_r(kvdv)
            kvhat = kvdv * rkv
            kvn_ref[rows, :] = (kvhat * gkv_ref[...]).astype(BF16)
            dkvn = _mm_nt(dkvb_ref[rows, :K_ALL], wk[...]) + _mm_nt(dkvb_ref[rows, K_ALL:], wv[...])
            dgkv_acc[...] += _colsum(dkvn * kvhat)
            dproj_ref[rows, IN_KV0:IN_POOL0] = _rms_bwd(kvhat, rkv, gkv_ref[...], dkvn).astype(BF16)

            dd_t = dd_ref[rows, :]
            if rows.stop < tm:
                after = dd_ref[rows.stop:rows.stop + POOL_HALO, :]
            else:
                after = jnp.where(step == n_steps - 1, 0.0, next_ref[...])
            ext = jnp.concatenate([dd_t, after], axis=0)
            ext_rows = n_rows + POOL_HALO
            counts = _window_count(step * tm + rows.start, ext_rows)
            for g, window in enumerate(POOL_WINDOWS):
                lanes = slice(POOL_GROUP * g, POOL_GROUP * (g + 1))
                level = ext[:, lanes] / counts[g]
                reach = 1
                while reach < window:
                    level = level + pltpu.roll(level, ext_rows - reach, 0)
                    reach *= 2
                dproj_ref[rows, IN_POOL0 + lanes.start:IN_POOL0 + lanes.stop] = (
                    level[:n_rows] - dd_t[:, lanes]).astype(BF16)
            dproj_ref[rows, IN_GATE0:IN_KR0] = dgpre_ref[rows, :]
            dproj_ref[rows, IN_KR0:IN_R] = dkr.astype(BF16)

            da = _mm_nt(dproj_ref[rows, :], win[...])
            xv = x_ref[rows, :]
            r0 = _rms_r(xv)
            xhat = xv * r0
            dgpre_acc[...] += _colsum(da * xhat)
            gx_ref[rows, :] = dh1_ref[rows, :] + _rms_bwd(xhat, r0, gpre_ref[...], da)

    per_tile = ATTN_TILE // tm
    heads = pl.BlockSpec((N_HEADS, tm, HEAD_PAD), lambda i: (0, i, 0))
    heads_t = pl.BlockSpec((N_HEADS, None, HEAD_PAD, tm), lambda i: (0, i // per_tile, 0, i % per_tile))
    return pl.pallas_call(
        body, name="proj_bwd", grid=(seq // tm,),
        in_specs=[heads_t, heads, heads, _row(tm, Q_LORA), _row(tm, KV_LORA), _row(tm, D_MODEL),
                  _row(tm, D_MODEL), _row(tm, 2 * D_MODEL), _row(tm, POOL_WIDTH),
                  pl.BlockSpec((POOL_HALO, POOL_WIDTH),
                               lambda i: (jnp.minimum((i + 1) * (tm // POOL_HALO), seq // POOL_HALO - 1), 0)),
                  _row(tm, HEAD_PAD), _row(tm, HEAD_PAD), _row(tm, HEAD_PAD),
                  _fix((1, D_MODEL)), _fix((1, Q_LORA)), _fix((1, KV_LORA)), ANY, ANY, ANY, ANY],
        out_specs=[_row(tm, D_MODEL), _row(tm, IN_R), _row(tm, N_HEADS * HEAD_PAD), _row(tm, Q_LORA), _row(tm, KV_LORA),
                   _row(tm, 2 * K_ALL),
                   _fix((1, D_MODEL)), _fix((1, Q_LORA)), _fix((1, KV_LORA))],
        out_shape=[_sds((seq, D_MODEL), F32), _sds((seq, IN_R), BF16), _sds((seq, N_HEADS * HEAD_PAD), BF16),
                   _sds((seq, Q_LORA), BF16), _sds((seq, KV_LORA), BF16), _sds((seq, 2 * K_ALL), BF16),
                   _sds((1, D_MODEL), F32), _sds((1, Q_LORA), F32), _sds((1, KV_LORA), F32)],
        scratch_shapes=[pltpu.VMEM(w_in_r.shape, BF16), pltpu.VMEM(w_uq_r.shape, BF16),
                        pltpu.VMEM(w_k_exp.shape, BF16), pltpu.VMEM(w_v.shape, BF16), pltpu.SemaphoreType.DMA((4,))],
        compiler_params=_params(1, 60),
    )(dq, dk, dv, qd, kvd, x, dh1, dgpre, dd, dd, cf, s1, s2, g_pre, g_q, g_kv, w_in_r, w_uq_r, w_k_exp, w_v)


def _grad_w(a, b, name, square_a=False, by_device=False, with_bf16=False):
    seq, k_dim = a.shape
    n_dim = b.shape[1]
    tk = min(k_dim, 1024)
    tn = n_dim // 2 if n_dim == IN_R else min(n_dim, 1024)
    ts = min(seq, 2048)
    shard = n_dim // N_DEV
    per_tile = tn // shard
    if by_device:
        out_spec = pl.BlockSpec((per_tile, tk, shard), lambda i, j, s: (j, i, 0))
        out_shape = _sds((N_DEV, k_dim, shard), F32)
    else:
        out_spec = pl.BlockSpec((tk, tn), lambda i, j, s: (i, j))
        out_shape = _sds((k_dim, n_dim), F32)

    n_seq_steps = seq // ts

    def body(a_ref, b_ref, o_ref, *narrow):
        @pl.when(pl.program_id(2) == 0)
        def _():
            o_ref[...] = jnp.zeros_like(o_ref)

        at = a_ref[...]
        if square_a:
            at = at * at
        part = lax.dot_general(at, b_ref[...], TN, preferred_element_type=F32)
        if by_device:
            for d in range(per_tile):
                o_ref[d] += part[:, d * shard:(d + 1) * shard]
        else:
            o_ref[...] += part
        if with_bf16:
            @pl.when(pl.program_id(2) == n_seq_steps - 1)
            def _():
                narrow[0][...] = o_ref[...].astype(BF16)

    return pl.pallas_call(
        body, name=name, grid=(k_dim // tk, n_dim // tn, n_seq_steps),
        in_specs=[pl.BlockSpec((ts, tk), lambda i, j, s: (s, i)), pl.BlockSpec((ts, tn), lambda i, j, s: (s, j))],
        out_specs=[out_spec, out_spec] if with_bf16 else out_spec,
        out_shape=[out_shape, _sds(out_shape.shape, BF16)] if with_bf16 else out_shape,
        compiler_params=_params(3, 48),
    )(a, b)


def _position():
    return lax.axis_index("x"), lax.axis_index("y"), lax.axis_index("c")


def _gather_copies(x_ref, slot, send_sems, recv_sems, local_sem, phases=("send", "forward", "finish"), sem_base=0):
    x, y, c = _position()
    me, sibling = (x, y, c), (x, y, 1 - c)
    chips = [(1 - x, y), (x, 1 - y), (1 - x, 1 - y)]

    def copy(k, block, to, src=None):
        return pltpu.make_async_remote_copy(
            src_ref=slot(*block) if src is None else src, dst_ref=slot(*block),
            send_sem=send_sems.at[sem_base + k], recv_sem=recv_sems.at[sem_base + k],
            device_id=to, device_id_type=MESH)

    mine = pltpu.make_async_copy(x_ref, slot(*me), local_sem)
    first = [copy(0, me, sibling, src=x_ref)]
    first += [copy(1 + j, me, (*chip, c), src=x_ref) for j, chip in enumerate(chips)]
    passed = [copy(4 + j, (*chip, c), sibling) for j, chip in enumerate(chips)]
    if "send" in phases:
        mine.start()
        for cp in first:
            cp.start()
    if "forward" in phases:
        for j, chip in enumerate(chips):
            copy(1 + j, (*chip, c), me).wait_recv()
            passed[j].start()
    if "finish" in phases:
        copy(0, sibling, me).wait_recv()
        for j, chip in enumerate(chips):
            copy(4 + j, (*chip, 1 - c), me).wait_recv()
        for cp in first + passed:
            cp.wait_send()
        mine.wait()


def _all_gather_hbm(block):
    def body(x_ref, out_ref, send_sems, recv_sems, local_sem):
        _gather_copies(x_ref, lambda px, py, pc: out_ref.at[4 * px + 2 * py + pc], send_sems, recv_sems, local_sem)

    return pl.pallas_call(
        body, name="gather_weights",
        in_specs=[ANY], out_specs=ANY,
        out_shape=_sds((N_DEV,) + block.shape, block.dtype),
        scratch_shapes=[pltpu.SemaphoreType.DMA((7,)), pltpu.SemaphoreType.DMA((7,)), pltpu.SemaphoreType.DMA],
    )(block)


def _replicated_update(grads, loss_block, ws, ms, vs):
    n_p = len(grads)
    sent = list(grads) + [loss_block]
    n_a = len(sent)

    def body(*refs):
        g_refs, refs = refs[:n_a], refs[n_a:]
        w_refs, m_refs, v_refs, refs = refs[:n_p], refs[n_p:2 * n_p], refs[2 * n_p:3 * n_p], refs[3 * n_p:]
        sum_refs, refs = refs[:n_a], refs[n_a:]
        d_refs, nm_refs, nv_refs, refs = refs[:n_p], refs[n_p:2 * n_p], refs[2 * n_p:3 * n_p], refs[3 * n_p:]
        bufs, (send_sems, recv_sems, local_sems) = refs[:n_a], refs[n_a:]
        x, y, c = _position()
        me = 4 * x + 2 * y + c
        local, remote = [], []
        for a in range(n_a):
            local.append(pltpu.make_async_copy(g_refs[a], bufs[a].at[me], local_sems.at[a]))
            for r in range(1, N_DEV):
                peer = (1 - x if r & 4 else x, 1 - y if r & 2 else y, 1 - c if r & 1 else c)
                remote.append(pltpu.make_async_remote_copy(
                    src_ref=g_refs[a], dst_ref=bufs[a].at[me],
                    send_sem=send_sems.at[(N_DEV - 1) * a + r - 1], recv_sem=recv_sems.at[(N_DEV - 1) * a + r - 1],
                    device_id=peer, device_id_type=MESH))
        for cp in local + remote:
            cp.start()
        for cp in remote:
            cp.wait_recv()
        for cp in remote:
            cp.wait_send()
        for cp in local:
            cp.wait()
        for a in range(n_a):
            acc = bufs[a][0]
            for d in range(1, N_DEV):
                acc = acc + bufs[a][d]
            if a == n_p:
                sum_refs[a][...] = acc
                continue
            delta, new_m, new_v = _adamw_math(acc, w_refs[a][...], m_refs[a][...], v_refs[a][...])
            sum_refs[a][...], d_refs[a][...], nm_refs[a][...], nv_refs[a][...] = acc, delta, new_m, new_v

    vmem = pl.BlockSpec(memory_space=pltpu.VMEM)
    like_w = [_sds(w.shape, F32) for w in ws]
    n_sem = (N_DEV - 1) * n_a
    outs = pl.pallas_call(
        body, name="replicated_update",
        in_specs=[vmem] * (n_a + 3 * n_p), out_specs=[vmem] * (n_a + 3 * n_p),
        out_shape=like_w + [_sds(loss_block.shape, F32)] + like_w * 3,
        scratch_shapes=[pltpu.VMEM((N_DEV,) + g.shape, F32) for g in sent]
                       + [pltpu.SemaphoreType.DMA((n_sem,)), pltpu.SemaphoreType.DMA((n_sem,)),
                          pltpu.SemaphoreType.DMA((n_a,))],
        compiler_params=pltpu.CompilerParams(vmem_limit_bytes=32 * MIB),
    )(*sent, *ws, *ms, *vs)
    return (outs[:n_p], outs[n_p], outs[n_a:n_a + n_p], outs[n_a + n_p:n_a + 2 * n_p], outs[n_a + 2 * n_p:])


def _exchange_pair(gs):
    n_w = len(gs)

    def body(*refs):
        g_refs, out_refs = refs[:n_w], refs[n_w:2 * n_w]
        send_sems, recv_sems = refs[2 * n_w:]
        x, y, c = _position()
        copies = []
        for w in range(n_w):
            for chip in range(4):
                cp = pltpu.make_async_remote_copy(
                    src_ref=g_refs[w].at[2 * chip + (1 - c)], dst_ref=out_refs[w].at[chip],
                    send_sem=send_sems.at[4 * w + chip], recv_sem=recv_sems.at[4 * w + chip],
                    device_id=(x, y, 1 - c), device_id_type=MESH)
                cp.start()
                copies.append(cp)
        for cp in copies:
            cp.wait_recv()
        for cp in copies:
            cp.wait_send()

    return pl.pallas_call(
        body, name="exchange_pair",
        in_specs=[ANY] * n_w, out_specs=[ANY] * n_w,
        out_shape=[_sds((4,) + g.shape[1:], g.dtype) for g in gs],
        scratch_shapes=[pltpu.SemaphoreType.DMA((4 * n_w,)), pltpu.SemaphoreType.DMA((4 * n_w,))],
    )(*gs)


def _exchange_chips(parts):
    n_w = len(parts)

    def body(*refs):
        p_refs, out_refs = refs[:n_w], refs[n_w:2 * n_w]
        send_sems, recv_sems = refs[2 * n_w:]
        x, y, c = _position()
        chips = [(1 - x, y), (x, 1 - y), (1 - x, 1 - y)]
        copies = []
        for w in range(n_w):
            for k, (px, py) in enumerate(chips):
                cp = pltpu.make_async_remote_copy(
                    src_ref=p_refs[w].at[2 * px + py], dst_ref=out_refs[w].at[k],
                    send_sem=send_sems.at[3 * w + k], recv_sem=recv_sems.at[3 * w + k],
                    device_id=(px, py, c), device_id_type=MESH)
                cp.start()
                copies.append(cp)
        for cp in copies:
            cp.wait_recv()
        for cp in copies:
            cp.wait_send()

    return pl.pallas_call(
        body, name="exchange_chips",
        in_specs=[ANY] * n_w, out_specs=[ANY] * n_w,
        out_shape=[_sds((3,) + p.shape[1:], p.dtype) for p in parts],
        scratch_shapes=[pltpu.SemaphoreType.DMA((3 * n_w,)), pltpu.SemaphoreType.DMA((3 * n_w,))],
    )(*parts)


def _row_tile(k):
    return 256 if k % 256 == 0 else 128


def _pair_sum(g, recv, place, name):
    _, k, n = g.shape
    tr = _row_tile(k)
    g4 = g.reshape(4, 2, k, n)

    def body(s_ref, g_ref, r_ref, o_ref):
        o_ref[...] = (g_ref[...] + r_ref[...]).astype(BF16)

    spec = pltpu.PrefetchScalarGridSpec(
        num_scalar_prefetch=1, grid=(4, k // tr),
        in_specs=[pl.BlockSpec((None, None, tr, n), lambda j, i, s: (j, s[2], i, 0)),
                  pl.BlockSpec((None, tr, n), lambda j, i, s: (j, i, 0))],
        out_specs=pl.BlockSpec((None, tr, n), lambda j, i, s: (j, i, 0)))
    return pl.pallas_call(
        body, name=name, grid_spec=spec, out_shape=_sds((4, k, n), BF16),
        compiler_params=_params(2, 32),
    )(place, g4, recv)


def _adamw_math(g, w, m, v):
    m = ADAM_B1 * m + (1.0 - ADAM_B1) * g
    v = ADAM_B2 * v + (1.0 - ADAM_B2) * (g * g)
    m_hat = m / (1.0 - ADAM_B1 ** ADAM_STEP)
    v_hat = v / (1.0 - ADAM_B2 ** ADAM_STEP)
    delta = -ADAM_LR * (m_hat / (jnp.sqrt(v_hat) + ADAM_EPS) + ADAM_WD * w)
    return delta, m, v


def _adamw_sharded(g, from_sibling, from_chips, place, w, m, v, name):
    _, k, n = g.shape
    tr = _row_tile(k)

    def body(s_ref, g_ref, sib_ref, r0_ref, r1_ref, r2_ref, w_ref, m_ref, v_ref, grad_ref, d_ref, nm_ref, nv_ref):
        grad = g_ref[...] + sib_ref[...]
        for r_ref in (r0_ref, r1_ref, r2_ref):
            grad = grad + r_ref[...].astype(F32)
        grad_ref[...] = grad
        d_ref[...], nm_ref[...], nv_ref[...] = _adamw_math(grad, w_ref[...], m_ref[...], v_ref[...])

    tile = pl.BlockSpec((None, tr, n), lambda i, s: (0, i, 0))

    def slot(j):
        return pl.BlockSpec((None, tr, n), lambda i, s: (j, i, 0))

    spec = pltpu.PrefetchScalarGridSpec(
        num_scalar_prefetch=1, grid=(k // tr,),
        in_specs=[pl.BlockSpec((None, tr, n), lambda i, s: (s[0], i, 0)),
                  pl.BlockSpec((None, tr, n), lambda i, s: (s[1], i, 0)),
                  slot(0), slot(1), slot(2), tile, tile, tile],
        out_specs=[tile] * 4)
    return pl.pallas_call(
        body, name=name, grid_spec=spec, out_shape=[_sds((1, k, n), F32)] * 4,
        compiler_params=_params(1, 48),
    )(place, g, from_sibling, from_chips, from_chips, from_chips, w, m, v)


def _adamw_direct(g, received, place, w, m, v, name):
    _, k, n = g.shape
    tr = _row_tile(k)

    def body(s_ref, g_ref, r_ref, w_ref, m_ref, v_ref, grad_ref, d_ref, nm_ref, nv_ref):
        grad = g_ref[...]
        for r in range(N_DEV - 1):
            grad = grad + r_ref[r].astype(F32)
        grad_ref[...] = grad
        d_ref[...], nm_ref[...], nv_ref[...] = _adamw_math(grad, w_ref[...], m_ref[...], v_ref[...])

    tile = pl.BlockSpec((None, tr, n), lambda i, s: (0, i, 0))
    spec = pltpu.PrefetchScalarGridSpec(
        num_scalar_prefetch=1, grid=(k // tr,),
        in_specs=[pl.BlockSpec((None, tr, n), lambda i, s: (s[0], i, 0)),
                  pl.BlockSpec((N_DEV - 1, tr, n), lambda i, s: (0, i, 0)), tile, tile, tile],
        out_specs=[tile] * 4)
    return pl.pallas_call(
        body, name=name, grid_spec=spec, out_shape=[_sds((1, k, n), F32)] * 4,
        compiler_params=_params(1, 48),
    )(place, g, received, w, m, v)


def _pack_rows(parts):
    parts = [a.reshape(-1, LANES) for a in parts]
    pad = (-sum(a.shape[0] for a in parts)) % PACK_ROW_TILE
    return jnp.concatenate(parts + [jnp.zeros((pad, LANES), parts[0].dtype)], axis=0)


def _full_from_gathered(gathered, entries, shard_shapes):
    out, off = {}, 0
    for (name, kind), (k, n) in zip(entries, shard_shapes):
        rows = k * n // LANES
        seg = gathered[:, off:off + rows].reshape(N_DEV, k, n)
        out[name] = jnp.transpose(seg, (1, 0, 2)).reshape(k, N_DEV * n) if kind == "col" else seg.reshape(N_DEV * k, n)
        off += rows
    return out


def _columns_by_device(a):
    k, n_all = a.shape
    return jnp.transpose(a.reshape(k, N_DEV, n_all // N_DEV), (1, 0, 2))


def _rows_by_device(a):
    k_all, n = a.shape
    return a.reshape(N_DEV, k_all // N_DEV, n)


def _rope_lane_frequencies():
    inv_freq = ROPE_THETA ** (-jnp.arange(0, QK_ROPE, 2, dtype=F32) / QK_ROPE)
    zeros = lambda n: jnp.zeros((n,), F32)
    return jnp.concatenate([zeros(QK_NOPE), inv_freq, inv_freq, zeros(HEAD_PAD - QK_NOPE - QK_ROPE)])[None, :]


def _rope_tables(pos_row, freq, tm):
    pos = jnp.transpose(jnp.broadcast_to(pos_row.astype(F32), (HEAD_PAD, tm)))
    ang = pos * freq
    cos, sin = jnp.cos(ang), jnp.sin(ang)
    lane = lax.broadcasted_iota(jnp.int32, (tm, HEAD_PAD), 1)
    first = (lane >= QK_NOPE) & (lane < QK_NOPE + QK_ROPE // 2)
    second = (lane >= QK_NOPE + QK_ROPE // 2) & (lane < QK_NOPE + QK_ROPE)
    cf = jnp.where(lane < QK_NOPE, 1.0, jnp.where(first | second, cos, 0.0))
    return cf, jnp.where(first, -sin, 0.0), jnp.where(second, sin, 0.0)


def _arrange_w_in(w):
    k = w.shape[0]
    zeros = lambda n: jnp.zeros((k, n), w.dtype)
    kr0 = Q_LORA + KV_LORA
    pool0 = kr0 + QK_ROPE
    return jnp.concatenate([w[:, :kr0], w[:, pool0:], zeros(QK_NOPE), w[:, kr0:pool0],
                            zeros(HEAD_PAD - QK_NOPE - QK_ROPE)], axis=1)


def _restore_w_in(d):
    kr = d[:, IN_KR0 + QK_NOPE:IN_KR0 + QK_NOPE + QK_ROPE]
    return jnp.concatenate([d[:, :IN_POOL0], kr, d[:, IN_POOL0:IN_KR0]], axis=1)


def _pad_heads(w, width):
    k = w.shape[0]
    w = w.reshape(k, N_HEADS, width)
    return jnp.pad(w, ((0, 0), (0, 0), (0, HEAD_PAD - width))).reshape(k, N_HEADS * HEAD_PAD)


def _unpad_heads(d, width):
    k = d.shape[0]
    return d.reshape(k, N_HEADS, HEAD_PAD)[:, :, :width]


def kernel(x, p, positions, g_pre_mix, w_in, b_gate, g_q, w_uq, g_kv, w_ukv, w_pool, pool_scale, w_branch_attn, w_branch_pool, w_out, g_post_mix, g_pre_mlp, w_ff1, w_ff2, g_post_mlp, w_ple_proj, w_ple_gate, g_ple, loss_target, m_g_pre_mix, m_w_in, m_b_gate, m_g_q, m_w_uq, m_g_kv, m_w_ukv, m_w_pool, m_pool_scale, m_w_branch_attn, m_w_branch_pool, m_w_out, m_g_post_mix, m_g_pre_mlp, m_w_ff1, m_w_ff2, m_g_post_mlp, m_w_ple_proj, m_w_ple_gate, m_g_ple, v_g_pre_mix, v_w_in, v_b_gate, v_g_q, v_w_uq, v_g_kv, v_w_ukv, v_w_pool, v_pool_scale, v_w_branch_attn, v_w_branch_pool, v_w_out, v_g_post_mix, v_g_pre_mlp, v_w_ff1, v_w_ff2, v_g_post_mlp, v_w_ple_proj, v_w_ple_gate, v_g_ple):
    given = dict(locals())
    weights = {n: given[n] for n in WEIGHT_ORDER}
    moments_m = {n: given["m_" + n] for n in WEIGHT_ORDER}
    moments_v = {n: given["v_" + n] for n in WEIGHT_ORDER}
    xs, ps, target = x[0], p[0, 0], loss_target[0]
    seq = xs.shape[0]
    tm = min(256, seq)
    tm_merge = min(512, seq)
    core = lax.axis_index("c")
    chip = 2 * lax.axis_index("x") + lax.axis_index("y")

    early, later = SHARDED[:N_EARLY], SHARDED[N_EARLY:]
    shapes_of = lambda entries: [weights[n].shape[1:] for n, _ in entries]
    pack_bf16 = lambda entries: _pack_rows([weights[n][0].astype(BF16) for n, _ in entries])
    full = _full_from_gathered(_all_gather_hbm(pack_bf16(early)), early, shapes_of(early))
    w_in_r = _arrange_w_in(full["w_in"])
    w_uq_r = _pad_heads(full["w_uq"], QK_NOPE + QK_ROPE)
    ukv = full["w_ukv"].reshape(KV_LORA, N_HEADS, QK_NOPE + V_HEAD)
    w_k_exp = _pad_heads(ukv[:, :, :QK_NOPE].reshape(KV_LORA, N_HEADS * QK_NOPE), QK_NOPE)
    w_v = _pad_heads(ukv[:, :, QK_NOPE:].reshape(KV_LORA, N_HEADS * V_HEAD), V_HEAD)
    w_pool_bf = w_pool[0].astype(BF16)

    packed_later = [e for e in later if e[0] != "w_ff1"]
    a_bf, qd, kvd, d_pool, pooled, gates, q, k, v, k_t, cf, s1, s2, gathered_later, gathered_ff1 = _proj_fwd(
        xs, g_pre_mix, b_gate, g_q, g_kv, positions, w_pool_bf, pool_scale, w_in_r, w_uq_r, w_k_exp, w_v,
        [pack_bf16(packed_later), w_ff1[0].astype(BF16)], tm_merge)
    full.update(_full_from_gathered(gathered_later, packed_later, shapes_of(packed_later)))
    full["w_ff1"] = gathered_ff1
    w_ba = jnp.pad(full["w_branch_attn"].reshape(N_HEADS, V_HEAD, D_MODEL),
                   ((0, 0), (0, HEAD_PAD - V_HEAD), (0, 0))).reshape(N_HEADS * HEAD_PAD, D_MODEL)
    o_heads, lse = _attn_fwd(q, k, v)
    merged, ba, bb, y, h1, attn_rows = _merge_fwd(o_heads, pooled, gates, xs, g_post_mix, w_ba,
                                                  full["w_branch_pool"], full["w_out"], tm_merge)
    (m_bf, zr, f, h2_bf, p_bf, de, dpre, dh2, loss_acc, dg_ple) = _tail_fwd(
        h1, target, ps, g_pre_mlp, g_post_mlp, g_ple, full["w_ff1"], full["w_ff2"], full["w_ple_proj"],
        full["w_ple_gate"], tm)

    by_device, payload = {}, {}

    def keep(name, pair, layout=lambda g: g):
        by_device[name], payload[name] = layout(pair[0]), layout(pair[1])

    keep("w_ple_proj", _grad_w(p_bf, de, "grad_w_ple_proj", by_device=True, with_bf16=True))
    keep("w_ple_gate", _grad_w(h2_bf, dpre, "grad_w_ple_gate", with_bf16=True), _rows_by_device)
    df, dz, dh1, dg_pre_mlp, dg_post_mlp = _mlp_bwd(h1, f, zr, dh2, g_pre_mlp, g_post_mlp, full["w_ff1"],
                                                    full["w_ff2"], tm)
    keep("w_ff1", _grad_w(m_bf, dz, "grad_w_ff1", by_device=True, with_bf16=True))
    keep("w_ff2", _grad_w(zr, df, "grad_w_ff2", square_a=True, with_bf16=True), _rows_by_device)
    (dy, dba, dbb, dgpre, do_heads, dd, delta, dg_post_mix, db_gate, dpool_scale, dw_pool) = _merge_bwd(
        dh1, y, gates, ba, bb, d_pool, o_heads, g_post_mix, pool_scale, full["w_out"], w_ba,
        full["w_branch_pool"], w_pool_bf, tm_merge)
    keep("w_branch_attn", _grad_w(attn_rows, dba, "grad_w_branch_attn", with_bf16=True),
          lambda g: _columns_by_device(g.reshape(N_HEADS, HEAD_PAD, D_MODEL)[:, :V_HEAD].reshape(-1, D_MODEL)))
    keep("w_branch_pool", _grad_w(pooled, dbb, "grad_w_branch_pool", by_device=True, with_bf16=True))
    keep("w_out", _grad_w(merged, dy, "grad_w_out", with_bf16=True), _rows_by_device)
    direct = [n for n, _ in SHARDED[N_EARLY:]]
    outs = _attn_bwd(q, k, k_t, v, do_heads, lse, delta, [payload[n] for n in direct])
    dq, dk, dv = outs[:3]
    received = dict(zip(direct, outs[3:]))
    (grad_x, dproj, dq_bf, qn_bf, kvn_bf, dkv_bf, dg_pre_mix, dg_q, dg_kv) = _proj_bwd(
        dq, dk, dv, qd, kvd, xs, dh1, dgpre, dd, cf, s1, s2, g_pre_mix, g_q, g_kv, w_in_r, w_uq_r, w_k_exp, w_v,
        tm_merge)
    d_w_kv = _grad_w(kvn_bf, dkv_bf, "grad_w_ukv")
    d_k_exp = _unpad_heads(d_w_kv[:, :K_ALL], QK_NOPE)
    d_w_v = _unpad_heads(d_w_kv[:, K_ALL:], V_HEAD)
    by_device["w_in"] = _columns_by_device(_restore_w_in(_grad_w(a_bf, dproj, "grad_w_in")))
    by_device["w_uq"] = _columns_by_device(
        _unpad_heads(_grad_w(qn_bf, dq_bf, "grad_w_uq"), QK_NOPE + QK_ROPE).reshape(Q_LORA, -1))
    by_device["w_ukv"] = _columns_by_device(jnp.concatenate([d_k_exp, d_w_v], axis=2).reshape(KV_LORA, -1))
    grads_small = {
        "g_pre_mix": dg_pre_mix, "b_gate": db_gate, "g_q": dg_q, "g_kv": dg_kv,
        "w_pool": dw_pool, "pool_scale": dpool_scale, "g_post_mix": dg_post_mix,
        "g_pre_mlp": dg_pre_mlp, "g_post_mlp": dg_post_mlp, "g_ple": dg_ple,
    }

    names = [n for n, _ in SHARDED]
    place = jnp.stack([2 * chip + core, chip, core]).astype(jnp.int32)
    sharded = {n: _adamw_direct(by_device[n], received[n], place, weights[n], moments_m[n], moments_v[n],
                                "adamw_" + n) for n in direct}
    last = [n for n, _ in SHARDED[:N_EARLY]]
    own = [by_device[n] for n in last]
    from_sibling = _exchange_pair(own)
    pair = [_pair_sum(g, r, place, "pair_sum_" + n) for n, g, r in zip(last, own, from_sibling)]
    from_chips = _exchange_chips(pair)
    sharded.update({n: _adamw_sharded(g, r, rc, place, weights[n], moments_m[n], moments_v[n], "adamw_" + n)
                    for n, g, r, rc in zip(last, own, from_sibling, from_chips)})

    flat = lambda a: a.reshape(a.shape[-3:]) if a.ndim > 3 else a
    g_sm, loss_sum, d_sm, m_sm, v_sm = _replicated_update(
        [flat(grads_small[n]) for n in REPLICATED], loss_acc, [flat(weights[n]) for n in REPLICATED],
        [flat(moments_m[n]) for n in REPLICATED], [flat(moments_v[n]) for n in REPLICATED])

    results = []
    for which, small in enumerate((g_sm, d_sm, m_sm, v_sm)):
        named = {n: sharded[n][which] for n in names}
        named.update({n: a.reshape(weights[n].shape) for n, a in zip(REPLICATED, small)})
        results.append([named[n] for n in WEIGHT_ORDER])

    return (loss_sum[0, 0], grad_x[None], *results[0], *results[1], *results[2], *results[3])
```

```python
import jax
import jax.numpy as jnp
from jax import lax
from jax.experimental import pallas as pl
from jax.experimental.pallas import tpu as pltpu

F32 = jnp.float32
BF16 = jnp.bfloat16

D_MODEL = 1024
PLE_DIM = 256
N_HEADS = 8
QK_NOPE = 64
QK_ROPE = 32
V_HEAD = 64
Q_LORA = 384
KV_LORA = 256
POOL_WINDOWS = (2, 4, 8, 16)
POOL_GROUP = 128
POOL_WIDTH = 512
D_FF = 4096
ROPE_THETA = 10000.0
EPS = 1e-6
HEAD_PAD = 128
K_ALL = N_HEADS * HEAD_PAD
ATTN_SCALE = (QK_NOPE + QK_ROPE) ** -0.5
LOG2E = 1.4426950408889634
Q_PRESCALE = ATTN_SCALE * LOG2E
ATTN_TILE = 512
FWD_ROWS = 512
FWD_CHAINS = 4
BWD_CHAINS = 2
BWD_QUERIES = 2

ADAM_LR = 0.001
ADAM_B1 = 0.9
ADAM_B2 = 0.999
ADAM_EPS = 1e-08
ADAM_WD = 0.01
ADAM_STEP = 10

N_DEV = 8
LANES = 1024
PACK_ROW_TILE = 480
POOL_HALO = 16
MIB = 2 ** 20

IN_Q0, IN_KV0, IN_POOL0, IN_GATE0, IN_KR0, IN_R = 0, 384, 640, 1152, 3200, 3328

SHARDED = (("w_in", "col"), ("w_uq", "col"), ("w_ukv", "col"), ("w_branch_attn", "col"),
           ("w_branch_pool", "col"), ("w_out", "row"), ("w_ff1", "col"), ("w_ff2", "row"),
           ("w_ple_proj", "col"), ("w_ple_gate", "row"))
N_EARLY = 3
REPLICATED = ("g_pre_mix", "b_gate", "g_q", "g_kv", "w_pool", "pool_scale", "g_post_mix",
              "g_pre_mlp", "g_post_mlp", "g_ple")
WEIGHT_ORDER = ("g_pre_mix", "w_in", "b_gate", "g_q", "w_uq", "g_kv", "w_ukv", "w_pool", "pool_scale",
                "w_branch_attn", "w_branch_pool", "w_out", "g_post_mix", "g_pre_mlp", "w_ff1", "w_ff2",
                "g_post_mlp", "w_ple_proj", "w_ple_gate", "g_ple")

NT = (((1,), (1,)), ((), ()))
TN = (((0,), (0,)), ((), ()))
MESH = pl.DeviceIdType.MESH
ANY = pl.BlockSpec(memory_space=pl.ANY)


def _params(n_axes, vmem_mib):
    return pltpu.CompilerParams(dimension_semantics=("arbitrary",) * n_axes, vmem_limit_bytes=vmem_mib * MIB)


def _row(tm, n):
    return pl.BlockSpec((tm, n), lambda i: (i, 0))


def _fix(shape):
    zeros = (0,) * len(shape)
    return pl.BlockSpec(shape, lambda i: zeros)


def _sds(shape, dtype):
    return jax.ShapeDtypeStruct(shape, dtype)


def _rms_r(v):
    return lax.rsqrt(jnp.mean(v * v, axis=-1, keepdims=True) + EPS)


def _rms_bwd(vhat, r, g, dy):
    gdy = dy * g
    return r * (gdy - vhat * jnp.mean(gdy * vhat, axis=-1, keepdims=True))


def _colsum(v):
    return jnp.sum(v, axis=0, keepdims=True)


def _sigmoid(v):
    return 1.0 / (1.0 + jnp.exp(-v))


def _mm(a, b):
    return jnp.dot(a, b, preferred_element_type=F32)


def _mm_nt(a, b):
    return lax.dot_general(a, b, NT, preferred_element_type=F32)


def _rope(c, cf, s1, s2):
    return c * cf + pltpu.roll(c, HEAD_PAD - 16, 1) * s1 + pltpu.roll(c, 16, 1) * s2


def _rope_t(c, cf, s1, s2):
    return c * cf + pltpu.roll(c * s1, 16, 1) + pltpu.roll(c * s2, HEAD_PAD - 16, 1)


def _row_chains(tm, rows=256):
    rows = min(rows, tm)
    return [slice(c * rows, (c + 1) * rows) for c in range(tm // rows)]


def _load_once(pairs, sems):
    @pl.when(pl.program_id(0) == 0)
    def _():
        copies = [pltpu.make_async_copy(src, dst, sems.at[i]) for i, (src, dst) in enumerate(pairs)]
        for cp in copies:
            cp.start()
        for cp in copies:
            cp.wait()


def _column_blocks(by_device_hbm, full_vmem):
    n = by_device_hbm.shape[2]
    return tuple((by_device_hbm.at[d], full_vmem.at[:, d * n:(d + 1) * n]) for d in range(N_DEV))


def _proj_fwd(x, g_pre, b_gate, g_q, g_kv, positions, w_pool_bf, pool_scale, w_in_r, w_uq_r, w_k_exp, w_v,
              later_shards, tm):
    seq = x.shape[0]
    n_steps = seq // tm
    forward_step = (3 * n_steps) // 4
    n_later = len(later_shards)

    def body(x_ref, gpre_ref, bg_ref, gq_ref, gkv_ref, pos_ref, freq_ref, wpool_ref, pscale_ref,
             win_hbm, wuq_hbm, wk_hbm, wv_hbm, *rest):
        later_refs, rest = rest[:n_later], rest[n_later:]
        (a_ref, qd_ref, kvd_ref, dpool_ref, pooled_ref, gates_ref, q_ref, k_ref, v_ref, kt_ref,
         cf_ref, s1_ref, s2_ref) = rest[:13]
        gathered_refs, rest = rest[13:13 + n_later], rest[13 + n_later:]
        win, wuq, wk, wv, halo_ref, send_sems, recv_sems, local_sems, load_sems = rest
        step = pl.program_id(0)

        def gather(phase):
            for a, (src, dst) in enumerate(zip(later_refs, gathered_refs)):
                _gather_copies(src, lambda px, py, pc, dst=dst: dst.at[4 * px + 2 * py + pc],
                               send_sems, recv_sems, local_sems.at[a], phases=(phase,), sem_base=7 * a)

        pl.when(step == 0)(lambda: gather("send"))
        pl.when(step == forward_step)(lambda: gather("forward"))
        _load_once(((win_hbm, win), (wuq_hbm, wuq), (wk_hbm, wk), (wv_hbm, wv)), load_sems)
        for rows in _row_chains(tm):
            n_rows = rows.stop - rows.start
            xv = x_ref[rows, :]
            a = (xv * _rms_r(xv) * gpre_ref[...]).astype(BF16)
            a_ref[rows, :] = a
            proj = _mm(a, win[...])
            qd = proj[:, IN_Q0:IN_KV0]
            kvd = proj[:, IN_KV0:IN_POOL0]
            qd_ref[rows, :] = qd
            kvd_ref[rows, :] = kvd
            gates_ref[rows, :] = _sigmoid(proj[:, IN_GATE0:IN_KR0] + bg_ref[...]).astype(BF16)

            u = proj[:, IN_POOL0:IN_GATE0]
            before = jnp.where(step == 0, 0.0, halo_ref[...]) if rows.start == 0 else tail
            tail = u[n_rows - POOL_HALO:, :]
            level = jnp.concatenate([before, u], axis=0)
            counts = _window_count(step * tm + rows.start, n_rows)
            shift = 1
            for g in range(len(POOL_WINDOWS)):
                level = level + pltpu.roll(level, shift, 0)
                shift *= 2
                lanes = slice(POOL_GROUP * g, POOL_GROUP * (g + 1))
                d = (level[POOL_HALO:, lanes] / counts[g] - u[:, lanes]).astype(BF16)
                dpool_ref[rows, lanes] = d
                pooled_ref[rows, lanes] = (_mm(d, wpool_ref[g]) * pscale_ref[:, lanes]).astype(BF16)
            if rows.stop == tm:
                halo_ref[...] = tail
            cfv, s1v, s2v = _rope_tables(pos_ref[:, rows], freq_ref[...], n_rows)
            cf_ref[rows, :], s1_ref[rows, :], s2_ref[rows, :] = cfv, s1v, s2v
            krr = _rope(proj[:, IN_KR0:IN_R], cfv, s1v, s2v)
            qn = (qd * _rms_r(qd) * gq_ref[...]).astype(BF16)
            qf = _mm(qn, wuq[...])
            kvn = (kvd * _rms_r(kvd) * gkv_ref[...]).astype(BF16)
            kf = _mm(kvn, wk[...])
            vf = _mm(kvn, wv[...])
            one_lane = (lax.broadcasted_iota(jnp.int32, (n_rows, HEAD_PAD), 1) == V_HEAD).astype(F32)
            for h in range(N_HEADS):
                lanes = slice(HEAD_PAD * h, HEAD_PAD * (h + 1))
                q_ref[h, rows, :] = (_rope(qf[:, lanes], cfv, s1v, s2v) * Q_PRESCALE).astype(BF16)
                kh = kf[:, lanes] + krr
                vh = vf[:, lanes] + one_lane
                k_ref[h, rows, :] = kh.astype(BF16)
                v_ref[h, rows, :] = vh.astype(BF16)
                kt_ref[h, :, rows] = jnp.transpose(kh).astype(BF16)
        pl.when(step == n_steps - 1)(lambda: gather("finish"))

    per_tile = ATTN_TILE // tm
    heads = pl.BlockSpec((N_HEADS, tm, HEAD_PAD), lambda i: (0, i, 0))
    heads_t = pl.BlockSpec((N_HEADS, None, HEAD_PAD, tm), lambda i: (0, i // per_tile, 0, i % per_tile))
    heads_t_shape = _sds((N_HEADS, seq // ATTN_TILE, HEAD_PAD, ATTN_TILE), BF16)
    return pl.pallas_call(
        body, name="proj_fwd", grid=(seq // tm,),
        in_specs=[_row(tm, D_MODEL), _fix((1, D_MODEL)), _fix((1, 2 * D_MODEL)), _fix((1, Q_LORA)), _fix((1, KV_LORA)),
                  pl.BlockSpec((1, tm), lambda i: (0, i)), _fix((1, HEAD_PAD)), _fix(w_pool_bf.shape), _fix((1, POOL_WIDTH)),
                  ANY, ANY, ANY, ANY] + [ANY] * n_later,
        out_specs=[_row(tm, D_MODEL), _row(tm, Q_LORA), _row(tm, KV_LORA), _row(tm, POOL_WIDTH), _row(tm, POOL_WIDTH),
                   _row(tm, 2 * D_MODEL), heads, heads, heads, heads_t,
                   _row(tm, HEAD_PAD), _row(tm, HEAD_PAD), _row(tm, HEAD_PAD)] + [ANY] * n_later,
        out_shape=[_sds((seq, D_MODEL), BF16), _sds((seq, Q_LORA), F32), _sds((seq, KV_LORA), F32),
                   _sds((seq, POOL_WIDTH), BF16), _sds((seq, POOL_WIDTH), BF16), _sds((seq, 2 * D_MODEL), BF16),
                   _sds((N_HEADS, seq, HEAD_PAD), BF16), _sds((N_HEADS, seq, HEAD_PAD), BF16),
                   _sds((N_HEADS, seq, HEAD_PAD), BF16), heads_t_shape,
                   _sds((seq, HEAD_PAD), F32), _sds((seq, HEAD_PAD), F32), _sds((seq, HEAD_PAD), F32)]
                  + [_sds((N_DEV,) + s.shape, s.dtype) for s in later_shards],
        scratch_shapes=[pltpu.VMEM(w_in_r.shape, BF16), pltpu.VMEM(w_uq_r.shape, BF16),
                        pltpu.VMEM(w_k_exp.shape, BF16), pltpu.VMEM(w_v.shape, BF16),
                        pltpu.VMEM((POOL_HALO, POOL_WIDTH), F32),
                        pltpu.SemaphoreType.DMA((7 * n_later,)), pltpu.SemaphoreType.DMA((7 * n_later,)),
                        pltpu.SemaphoreType.DMA((n_later,)), pltpu.SemaphoreType.DMA((4,))],
        compiler_params=_params(1, 48),
    )(x, g_pre, b_gate, g_q, g_kv, positions, _rope_lane_frequencies(), w_pool_bf, pool_scale,
      w_in_r, w_uq_r, w_k_exp, w_v, *later_shards)


def _window_count(row0, n_rows):
    t = row0 + lax.broadcasted_iota(jnp.int32, (n_rows, POOL_GROUP), 0)
    return [jnp.minimum(t + 1, w).astype(F32) for w in POOL_WINDOWS]


def _col_to_row(col, n):
    return jnp.transpose(jnp.broadcast_to(col, (n, HEAD_PAD)))[0:1, :]


def _attn_fwd(q, k, v):
    heads, seq, _ = q.shape
    r = FWD_ROWS
    n = min(FWD_CHAINS, seq // r)
    block = r * n

    def body(q_ref, k_ref, v_ref, o_ref, lse_ref):
        qi = pl.program_id(1)
        q_tiles = [q_ref[c * r:(c + 1) * r, :] for c in range(n)]

        def tile(qt, j, m, acc, diagonal):
            start = pl.multiple_of(j * r, r)
            s = _mm_nt(qt, k_ref[pl.ds(start, r), :])
            if diagonal:
                row = lax.broadcasted_iota(jnp.int32, (r, r), 0)
                col = lax.broadcasted_iota(jnp.int32, (r, r), 1)
                s = jnp.where(col <= row, s, -jnp.inf)
            m_new = jnp.maximum(m, jnp.max(s, axis=1, keepdims=True))
            p = jnp.exp2((s - m_new).astype(BF16))
            acc = jnp.exp2(m - m_new) * acc + _mm(p, v_ref[pl.ds(start, r), :])
            return m_new, acc

        def all_chains(jj, carry):
            for u in range(n):
                carry = tuple(tile(q_tiles[c], n * jj + u, *carry[c], False) for c in range(n))
            return carry

        init = tuple((jnp.full((r, 1), -jnp.inf, F32), jnp.zeros((r, HEAD_PAD), F32)) for _ in range(n))
        state = list(lax.fori_loop(0, qi, all_chains, init))
        for d in range(n):
            for c in range(d, n):
                state[c] = tile(q_tiles[c], n * qi + d, *state[c], c == d)
        for c, (m, acc) in enumerate(state):
            l = acc[:, V_HEAD:V_HEAD + 1]
            o_ref[c * r:(c + 1) * r, :] = (acc / l).astype(BF16)
            row0 = c * r
            lse_ref[row0 // ATTN_TILE, :, row0 % ATTN_TILE:row0 % ATTN_TILE + r] = _col_to_row(m + jnp.log2(l), r)

    return pl.pallas_call(
        body, name="attn_fwd", grid=(heads, seq // block),
        in_specs=[pl.BlockSpec((None, block, HEAD_PAD), lambda h, i: (h, i, 0)),
                  pl.BlockSpec((None, seq, HEAD_PAD), lambda h, i: (h, 0, 0)),
                  pl.BlockSpec((None, seq, HEAD_PAD), lambda h, i: (h, 0, 0))],
        out_specs=[pl.BlockSpec((None, block, HEAD_PAD), lambda h, i: (h, i, 0)),
                   pl.BlockSpec((None, block // ATTN_TILE, 1, ATTN_TILE), lambda h, i: (h, i, 0, 0))],
        out_shape=[_sds((heads, seq, HEAD_PAD), BF16), _sds((heads, seq // ATTN_TILE, 1, ATTN_TILE), F32)],
        compiler_params=_params(2, 48),
    )(q, k, v)


def _peer_copies(src_refs, dst_refs, send_sems, recv_sems):
    x, y, c = _position()
    copies = []
    for w, (src, dst) in enumerate(zip(src_refs, dst_refs)):
        for r in range(1, N_DEV):
            px = 1 - x if r & 4 else x
            py = 1 - y if r & 2 else y
            pc = 1 - c if r & 1 else c
            copies.append(pltpu.make_async_remote_copy(
                src_ref=src.at[4 * px + 2 * py + pc], dst_ref=dst.at[r - 1],
                send_sem=send_sems.at[(N_DEV - 1) * w + r - 1], recv_sem=recv_sems.at[(N_DEV - 1) * w + r - 1],
                device_id=(px, py, pc), device_id_type=MESH))
    return copies


def _attn_bwd(q, k, k_t, v, do, lse, delta, early_grads):
    heads, seq, _ = q.shape
    t = ATTN_TILE
    nq = seq // t
    n = min(BWD_CHAINS, nq)
    n_w = len(early_grads)

    def body(q_ref, k_ref, kt_ref, v_ref, do_ref, lse_ref, delta_ref, *rest):
        grad_refs, rest = rest[:n_w], rest[n_w:]
        dq_ref, dk_ref, dv_ref = rest[:3]
        recv_refs, (send_sems, recv_sems) = rest[3:3 + n_w], rest[3 + n_w:]
        jp = pl.program_id(1)
        head = pl.program_id(0)

        @pl.when((head == 0) & (jp == 0))
        def _():
            for cp in _peer_copies(grad_refs, recv_refs, send_sems, recv_sems):
                cp.start()

        @pl.when(jp == 0)
        def _():
            dq_ref[...] = jnp.zeros_like(dq_ref)

        keys = [k_ref[c * t:(c + 1) * t, :] for c in range(n)]
        values = [v_ref[c * t:(c + 1) * t, :] for c in range(n)]

        def tile(c, i, dk, dv, diagonal):
            start = pl.multiple_of(i * t, t)
            qt = q_ref[pl.ds(start, t), :]
            dot = do_ref[pl.ds(start, t), :]
            p_t = jnp.exp2(_mm_nt(keys[c], qt) - lse_ref[i])
            if diagonal:
                key = lax.broadcasted_iota(jnp.int32, (t, t), 0)
                query = lax.broadcasted_iota(jnp.int32, (t, t), 1)
                p_t = jnp.where(key <= query, p_t, 0.0)
            dv = dv + _mm(p_t.astype(BF16), dot)
            ds_t = (p_t * (_mm_nt(values[c], dot) - delta_ref[i])).astype(BF16)
            dk = dk + _mm(ds_t, qt)
            return dk, dv, _mm(kt_ref[c], ds_t)

        def query_tile(i, state, first_rows):
            dq = None
            for c in range(n if first_rows is None else first_rows + 1):
                dk, dv, dq_c = tile(c, i, *state[c], diagonal=(c == first_rows))
                state[c] = (dk, dv)
                dq = dq_c if dq is None else dq + dq_c
            dq_ref[i] += dq

        def passes(ip, carry):
            state = list(carry)
            for u in range(BWD_QUERIES):
                query_tile(BWD_QUERIES * ip + u, state, None)
            return tuple(state)

        zero = jnp.zeros((t, HEAD_PAD), F32)
        state = [(zero, zero)] * n
        for offset in range(n):
            query_tile(n * jp + offset, state, offset)
        first_pass = (n * (jp + 1)) // BWD_QUERIES
        state = lax.fori_loop(first_pass, nq // BWD_QUERIES, passes, tuple(state))
        for c, (dk, dv) in enumerate(state):
            dk_ref[c * t:(c + 1) * t, :] = (dk * (1.0 / LOG2E)).astype(BF16)
            dv_ref[c * t:(c + 1) * t, :] = dv.astype(BF16)

        @pl.when((head == heads - 1) & (jp == nq // n - 1))
        def _():
            copies = _peer_copies(grad_refs, recv_refs, send_sems, recv_sems)
            for cp in copies:
                cp.wait_recv()
            for cp in copies:
                cp.wait_send()

    whole = pl.BlockSpec((None, seq, HEAD_PAD), lambda h, j: (h, 0, 0))
    whole_t = pl.BlockSpec((None, nq, HEAD_PAD, t), lambda h, j: (h, 0, 0, 0))
    pair = pl.BlockSpec((None, n * t, HEAD_PAD), lambda h, j: (h, j, 0))
    pair_t = pl.BlockSpec((None, n, HEAD_PAD, t), lambda h, j: (h, j, 0, 0))
    stats = pl.BlockSpec((None, nq, 1, t), lambda h, j: (h, 0, 0, 0))
    return pl.pallas_call(
        body, name="attn_bwd", grid=(heads, nq // n),
        in_specs=[whole, pair, pair_t, pair, whole, stats, stats] + [ANY] * n_w,
        out_specs=[whole_t, pair, pair] + [ANY] * n_w,
        out_shape=[_sds((heads, nq, HEAD_PAD, t), F32), _sds((heads, seq, HEAD_PAD), BF16),
                   _sds((heads, seq, HEAD_PAD), BF16)]
                  + [_sds((N_DEV - 1,) + g.shape[1:], g.dtype) for g in early_grads],
        scratch_shapes=[pltpu.SemaphoreType.DMA(((N_DEV - 1) * n_w,)), pltpu.SemaphoreType.DMA(((N_DEV - 1) * n_w,))],
        compiler_params=_params(2, 56),
    )(q, k, k_t, v, do, lse, delta, *early_grads)


def _merge_fwd(attn, pooled, gates, x, g_post_mix, w_ba, w_bb, w_out, tm):
    seq = x.shape[0]

    def body(attn_ref, pooled_ref, gates_ref, x_ref, g_ref, wba_ref, wbb_ref, wout_ref,
             merged_ref, ba_ref, bb_ref, y_ref, h1_ref, attn_rows_ref):
        for rows in _row_chains(tm):
            attn = jnp.concatenate([attn_ref[h, rows, :] for h in range(N_HEADS)], axis=1)
            attn_rows_ref[rows, :] = attn
            ba = _mm(attn, wba_ref[...])
            bb = _mm(pooled_ref[rows, :], wbb_ref[...])
            ba_ref[rows, :] = ba.astype(BF16)
            bb_ref[rows, :] = bb.astype(BF16)
            merged = (gates_ref[rows, :D_MODEL].astype(F32) * ba
                      + gates_ref[rows, D_MODEL:].astype(F32) * bb).astype(BF16)
            merged_ref[rows, :] = merged
            y = _mm(merged, wout_ref[...])
            y_ref[rows, :] = y
            h1_ref[rows, :] = x_ref[rows, :] + y * _rms_r(y) * g_ref[...]

    return pl.pallas_call(
        body, name="merge_fwd", grid=(seq // tm,),
        in_specs=[pl.BlockSpec((N_HEADS, tm, HEAD_PAD), lambda i: (0, i, 0)), _row(tm, POOL_WIDTH),
                  _row(tm, 2 * D_MODEL), _row(tm, D_MODEL),
                  _fix((1, D_MODEL)), _fix(w_ba.shape), _fix(w_bb.shape), _fix(w_out.shape)],
        out_specs=[_row(tm, D_MODEL)] * 5 + [_row(tm, N_HEADS * HEAD_PAD)],
        out_shape=[_sds((seq, D_MODEL), BF16), _sds((seq, D_MODEL), BF16), _sds((seq, D_MODEL), BF16),
                   _sds((seq, D_MODEL), F32), _sds((seq, D_MODEL), F32), _sds((seq, N_HEADS * HEAD_PAD), BF16)],
        compiler_params=_params(1, 48),
    )(attn, pooled, gates, x, g_post_mix, w_ba, w_bb, w_out)


def _tail_fwd(h1, target, p, g_pre_mlp, g_post_mlp, g_ple, w_ff1, w_ff2, w_pe, w_pg, tm):
    seq = h1.shape[0]

    def body(h1_ref, tgt_ref, p_ref, gm_ref, gf_ref, gp_ref, w1_hbm, w2_hbm, wpe_hbm, wpg_hbm,
             m_ref, zr_ref, f_ref, h2b_ref, pb_ref, de_ref, dpre_ref, dh2_ref, loss_ref, dgple_ref,
             w1, w2, wpe, wpg, load_sems):
        _load_once(_column_blocks(w1_hbm, w1) + ((w2_hbm, w2), (wpe_hbm, wpe), (wpg_hbm, wpg)), load_sems)

        @pl.when(pl.program_id(0) == 0)
        def _():
            loss_ref[...] = jnp.zeros_like(loss_ref)
            dgple_ref[...] = jnp.zeros_like(dgple_ref)

        h1v = h1_ref[...]
        m = (h1v * _rms_r(h1v) * gm_ref[...]).astype(BF16)
        m_ref[...] = m
        zr = jnp.maximum(_mm(m, w1[...]), 0.0)
        zr_ref[...] = zr.astype(BF16)
        f = _mm((zr * zr).astype(BF16), w2[...])
        f_ref[...] = f
        h2 = h1v + f * _rms_r(f) * gf_ref[...]
        h2b = h2.astype(BF16)
        h2b_ref[...] = h2b
        pb = p_ref[...].astype(BF16)
        pb_ref[...] = pb
        e = _mm(pb, wpe[...])
        pg = _sigmoid(_mm(h2b, wpg[...]))
        t3 = pg * e
        r3 = _rms_r(t3)
        t3hat = t3 * r3
        diff = h2 + t3hat * gp_ref[...] - tgt_ref[...]
        loss_ref[...] += jnp.sum(diff * diff) * (0.5 / D_MODEL)
        dh3 = diff * (1.0 / D_MODEL)
        dgple_ref[...] += _colsum(dh3 * t3hat)
        dt3 = _rms_bwd(t3hat, r3, gp_ref[...], dh3)
        de_ref[...] = (dt3 * pg).astype(BF16)
        dpre = (dt3 * e * pg * (1.0 - pg)).astype(BF16)
        dpre_ref[...] = dpre
        dh2_ref[...] = dh3 + _mm_nt(dpre, wpg[...])

    return pl.pallas_call(
        body, name="tail_fwd", grid=(seq // tm,),
        in_specs=[_row(tm, D_MODEL), _row(tm, D_MODEL), _row(tm, PLE_DIM), _fix((1, D_MODEL)), _fix((1, D_MODEL)),
                  _fix((1, D_MODEL)), ANY, ANY, ANY, ANY],
        out_specs=[_row(tm, D_MODEL), _row(tm, D_FF), _row(tm, D_MODEL), _row(tm, D_MODEL), _row(tm, PLE_DIM),
                   _row(tm, D_MODEL), _row(tm, D_MODEL), _row(tm, D_MODEL), _fix((8, 128)), _fix((1, D_MODEL))],
        out_shape=[_sds((seq, D_MODEL), BF16), _sds((seq, D_FF), BF16), _sds((seq, D_MODEL), F32),
                   _sds((seq, D_MODEL), BF16), _sds((seq, PLE_DIM), BF16), _sds((seq, D_MODEL), BF16),
                   _sds((seq, D_MODEL), BF16), _sds((seq, D_MODEL), F32), _sds((8, 128), F32), _sds((1, D_MODEL), F32)],
        scratch_shapes=[pltpu.VMEM((w_ff1.shape[1], N_DEV * w_ff1.shape[2]), BF16), pltpu.VMEM(w_ff2.shape, BF16),
                        pltpu.VMEM(w_pe.shape, BF16), pltpu.VMEM(w_pg.shape, BF16),
                        pltpu.SemaphoreType.DMA((N_DEV + 3,))],
        compiler_params=_params(1, 56),
    )(h1, target, p, g_pre_mlp, g_post_mlp, g_ple, w_ff1, w_ff2, w_pe, w_pg)


def _mlp_bwd(h1, f, zr, dh2, g_pre_mlp, g_post_mlp, w_ff1, w_ff2, tm):
    seq = h1.shape[0]

    def body(h1_ref, f_ref, zr_ref, dh2_ref, gm_ref, gf_ref, w1_hbm, w2_hbm,
             df_ref, dz_ref, dh1_ref, dgm_ref, dgf_ref, w1, w2, load_sems):
        _load_once(_column_blocks(w1_hbm, w1) + ((w2_hbm, w2),), load_sems)

        @pl.when(pl.program_id(0) == 0)
        def _():
            dgm_ref[...] = jnp.zeros_like(dgm_ref)
            dgf_ref[...] = jnp.zeros_like(dgf_ref)

        dh2 = dh2_ref[...]
        fv = f_ref[...]
        rf = _rms_r(fv)
        fhat = fv * rf
        dgf_ref[...] += _colsum(dh2 * fhat)
        df = _rms_bwd(fhat, rf, gf_ref[...], dh2).astype(BF16)
        df_ref[...] = df
        dz = (_mm_nt(df, w2[...]) * (2.0 * zr_ref[...].astype(F32))).astype(BF16)
        dz_ref[...] = dz
        dm = _mm_nt(dz, w1[...])
        h1v = h1_ref[...]
        r1 = _rms_r(h1v)
        h1hat = h1v * r1
        dgm_ref[...] += _colsum(dm * h1hat)
        dh1_ref[...] = dh2 + _rms_bwd(h1hat, r1, gm_ref[...], dm)

    return pl.pallas_call(
        body, name="mlp_bwd", grid=(seq // tm,),
        in_specs=[_row(tm, D_MODEL), _row(tm, D_MODEL), _row(tm, D_FF), _row(tm, D_MODEL),
                  _fix((1, D_MODEL)), _fix((1, D_MODEL)), ANY, ANY],
        out_specs=[_row(tm, D_MODEL), _row(tm, D_FF), _row(tm, D_MODEL), _fix((1, D_MODEL)), _fix((1, D_MODEL))],
        out_shape=[_sds((seq, D_MODEL), BF16), _sds((seq, D_FF), BF16), _sds((seq, D_MODEL), F32),
                   _sds((1, D_MODEL), F32), _sds((1, D_MODEL), F32)],
        scratch_shapes=[pltpu.VMEM((w_ff1.shape[1], N_DEV * w_ff1.shape[2]), BF16), pltpu.VMEM(w_ff2.shape, BF16),
                        pltpu.SemaphoreType.DMA((N_DEV + 1,))],
        compiler_params=_params(1, 56),
    )(h1, f, zr, dh2, g_pre_mlp, g_post_mlp, w_ff1, w_ff2)


def _merge_bwd(dh1, y, gates, ba, bb, d_pool, o_heads, g_post_mix, pool_scale, w_out, w_ba, w_bb, w_pool_bf, tm):
    seq = dh1.shape[0]
    assert tm == ATTN_TILE

    def body(dh1_ref, y_ref, gates_ref, ba_ref, bb_ref, d_ref, o_ref, g_ref, ps_ref, wout_ref, wba_ref, wbb_ref, wp_ref,
             dy_ref, dba_ref, dbb_ref, dgpre_ref, dattn_ref, dd_ref, delta_ref, dg_ref, dbg_ref, dps_ref, dwp_ref):
        @pl.when(pl.program_id(0) == 0)
        def _():
            dg_ref[...] = jnp.zeros_like(dg_ref)
            dbg_ref[...] = jnp.zeros_like(dbg_ref)
            dps_ref[...] = jnp.zeros_like(dps_ref)
            dwp_ref[...] = jnp.zeros_like(dwp_ref)

        for rows in _row_chains(tm):
            dh1v = dh1_ref[rows, :]
            yv = y_ref[rows, :]
            r = _rms_r(yv)
            yhat = yv * r
            dg_ref[...] += _colsum(dh1v * yhat)
            dy = _rms_bwd(yhat, r, g_ref[...], dh1v).astype(BF16)
            dy_ref[rows, :] = dy
            dmerged = _mm_nt(dy, wout_ref[...])
            dbranch = []
            for half, branch_ref, dbranch_ref in ((0, ba_ref, dba_ref), (1, bb_ref, dbb_ref)):
                lanes = slice(D_MODEL * half, D_MODEL * (half + 1))
                gate = gates_ref[rows, lanes].astype(F32)
                dpre = dmerged * branch_ref[rows, :].astype(F32) * gate * (1.0 - gate)
                dbg_ref[:, lanes] += _colsum(dpre)
                dgpre_ref[rows, lanes] = dpre.astype(BF16)
                dbranch.append((dmerged * gate).astype(BF16))
                dbranch_ref[rows, :] = dbranch[-1]
            dattn = _mm_nt(dbranch[0], wba_ref[...]).astype(BF16)
            for h in range(N_HEADS):
                do_h = dattn[:, HEAD_PAD * h:HEAD_PAD * (h + 1)]
                dattn_ref[h, rows, :] = do_h
                row_term = jnp.sum(o_ref[h, rows, :].astype(F32) * do_h.astype(F32), axis=1, keepdims=True)
                delta_ref[h, :, rows] = _col_to_row(row_term, rows.stop - rows.start)
            dpooled = _mm_nt(dbranch[1], wbb_ref[...])
            for g in range(len(POOL_WINDOWS)):
                lanes = slice(POOL_GROUP * g, POOL_GROUP * (g + 1))
                dpl = dpooled[:, lanes]
                d_g = d_ref[rows, lanes]
                dps_ref[:, lanes] += _colsum(dpl * _mm(d_g, wp_ref[g]))
                dyp = (dpl * ps_ref[:, lanes]).astype(BF16)
                dwp_ref[g] += lax.dot_general(d_g, dyp, TN, preferred_element_type=F32)
                dd_ref[rows, lanes] = _mm_nt(dyp, wp_ref[g])

    heads = pl.BlockSpec((N_HEADS, tm, HEAD_PAD), lambda i: (0, i, 0))
    return pl.pallas_call(
        body, name="merge_bwd", grid=(seq // tm,),
        in_specs=[_row(tm, D_MODEL), _row(tm, D_MODEL), _row(tm, 2 * D_MODEL), _row(tm, D_MODEL), _row(tm, D_MODEL),
                  _row(tm, POOL_WIDTH), heads, _fix((1, D_MODEL)), _fix((1, POOL_WIDTH)),
                  _fix(w_out.shape), _fix(w_ba.shape), _fix(w_bb.shape), _fix(w_pool_bf.shape)],
        out_specs=[_row(tm, D_MODEL), _row(tm, D_MODEL), _row(tm, D_MODEL), _row(tm, 2 * D_MODEL),
                   heads, _row(tm, POOL_WIDTH), pl.BlockSpec((N_HEADS, None, 1, tm), lambda i: (0, i, 0, 0)),
                   _fix((1, D_MODEL)), _fix((1, 2 * D_MODEL)), _fix((1, POOL_WIDTH)), _fix(w_pool_bf.shape)],
        out_shape=[_sds((seq, D_MODEL), BF16), _sds((seq, D_MODEL), BF16), _sds((seq, D_MODEL), BF16),
                   _sds((seq, 2 * D_MODEL), BF16), _sds((N_HEADS, seq, HEAD_PAD), BF16),
                   _sds((seq, POOL_WIDTH), F32), _sds((N_HEADS, seq // tm, 1, tm), F32),
                   _sds((1, D_MODEL), F32), _sds((1, 2 * D_MODEL), F32),
                   _sds((1, POOL_WIDTH), F32), _sds(w_pool_bf.shape, F32)],
        compiler_params=_params(1, 48),
    )(dh1, y, gates, ba, bb, d_pool, o_heads, g_post_mix, pool_scale, w_out, w_ba, w_bb, w_pool_bf)


def _proj_bwd(dq, dk, dv, qd, kvd, x, dh1, dgpre, dd, cf, s1, s2, g_pre, g_q, g_kv,
              w_in_r, w_uq_r, w_k_exp, w_v, tm):
    seq = x.shape[0]
    n_steps = seq // tm

    def body(dq_ref, dk_ref, dv_ref, qd_ref, kvd_ref, x_ref, dh1_ref, dgpre_ref, dd_ref, next_ref,
             cf_ref, s1_ref, s2_ref, gpre_ref, gq_ref, gkv_ref, win_hbm, wuq_hbm, wk_hbm, wv_hbm,
             gx_ref, dproj_ref, dqb_ref, qn_ref, kvn_ref, dkvb_ref, dgpre_acc, dgq_acc, dgkv_acc,
             win, wuq, wk, wv, load_sems):
        step = pl.program_id(0)
        _load_once(((win_hbm, win), (wuq_hbm, wuq), (wk_hbm, wk), (wv_hbm, wv)), load_sems)

        @pl.when(pl.program_id(0) == 0)
        def _():
            dgpre_acc[...] = jnp.zeros_like(dgpre_acc)
            dgq_acc[...] = jnp.zeros_like(dgq_acc)
            dgkv_acc[...] = jnp.zeros_like(dgkv_acc)

        for rows in _row_chains(tm):
            n_rows = rows.stop - rows.start
            cfv, s1v, s2v = cf_ref[rows, :], s1_ref[rows, :], s2_ref[rows, :]
            ksum = jnp.zeros((n_rows, HEAD_PAD), F32)
            for h in range(N_HEADS):
                lanes = slice(HEAD_PAD * h, HEAD_PAD * (h + 1))
                dqh = jnp.transpose(dq_ref[h, :, rows])
                dqb_ref[rows, lanes] = (_rope_t(dqh, cfv, s1v, s2v) * ATTN_SCALE).astype(BF16)
                dkh = dk_ref[h, rows, :]
                dkvb_ref[rows, lanes] = dkh
                dkvb_ref[rows, slice(K_ALL + lanes.start, K_ALL + lanes.stop)] = dv_ref[h, rows, :]
                ksum = ksum + dkh.astype(F32)
            lane = lax.broadcasted_iota(jnp.int32, (n_rows, HEAD_PAD), 1)
            rope_lanes = (lane >= QK_NOPE) & (lane < QK_NOPE + QK_ROPE)
            dkr = _rope_t(jnp.where(rope_lanes, ksum, 0.0), cfv, s1v, s2v)

            qdv = qd_ref[rows, :]
            rq = _rms_r(qdv)
            qhat = qdv * rq
            qn_ref[rows, :] = (qhat * gq_ref[...]).astype(BF16)
            dqn = _mm_nt(dqb_ref[rows, :], wuq[...])
            dgq_acc[...] += _colsum(dqn * qhat)
            dproj_ref[rows, IN_Q0:IN_KV0] = _rms_bwd(qhat, rq, gq_ref[...], dqn).astype(BF16)

            kvdv = kvd_ref[rows, :]
            rkv = _rms_r(kvdv)
            kvhat = kvdv * rkv
            kvn_ref[rows, :] = (kvhat * gkv_ref[...]).astype(BF16)
            dkvn = _mm_nt(dkvb_ref[rows, :K_ALL], wk[...]) + _mm_nt(dkvb_ref[rows, K_ALL:], wv[...])
            dgkv_acc[...] += _colsum(dkvn * kvhat)
            dproj_ref[rows, IN_KV0:IN_POOL0] = _rms_bwd(kvhat, rkv, gkv_ref[...], dkvn).astype(BF16)

            dd_t = dd_ref[rows, :]
            if rows.stop < tm:
                after = dd_ref[rows.stop:rows.stop + POOL_HALO, :]
            else:
                after = jnp.where(step == n_steps - 1, 0.0, next_ref[...])
            ext = jnp.concatenate([dd_t, after], axis=0)
            ext_rows = n_rows + POOL_HALO
            counts = _window_count(step * tm + rows.start, ext_rows)
            for g, window in enumerate(POOL_WINDOWS):
                lanes = slice(POOL_GROUP * g, POOL_GROUP * (g + 1))
                level = ext[:, lanes] / counts[g]
                reach = 1
                while reach < window:
                    level = level + pltpu.roll(level, ext_rows - reach, 0)
                    reach *= 2
                dproj_ref[rows, IN_POOL0 + lanes.start:IN_POOL0 + lanes.stop] = (
                    level[:n_rows] - dd_t[:, lanes]).astype(BF16)
            dproj_ref[rows, IN_GATE0:IN_KR0] = dgpre_ref[rows, :]
            dproj_ref[rows, IN_KR0:IN_R] = dkr.astype(BF16)

            da = _mm_nt(dproj_ref[rows, :], win[...])
            xv = x_ref[rows, :]
            r0 = _rms_r(xv)
            xhat = xv * r0
            dgpre_acc[...] += _colsum(da * xhat)
            gx_ref[rows, :] = dh1_ref[rows, :] + _rms_bwd(xhat, r0, gpre_ref[...], da)

    per_tile = ATTN_TILE // tm
    heads = pl.BlockSpec((N_HEADS, tm, HEAD_PAD), lambda i: (0, i, 0))
    heads_t = pl.BlockSpec((N_HEADS, None, HEAD_PAD, tm), lambda i: (0, i // per_tile, 0, i % per_tile))
    return pl.pallas_call(
        body, name="proj_bwd", grid=(seq // tm,),
        in_specs=[heads_t, heads, heads, _row(tm, Q_LORA), _row(tm, KV_LORA), _row(tm, D_MODEL),
                  _row(tm, D_MODEL), _row(tm, 2 * D_MODEL), _row(tm, POOL_WIDTH),
                  pl.BlockSpec((POOL_HALO, POOL_WIDTH),
                               lambda i: (jnp.minimum((i + 1) * (tm // POOL_HALO), seq // POOL_HALO - 1), 0)),
                  _row(tm, HEAD_PAD), _row(tm, HEAD_PAD), _row(tm, HEAD_PAD),
                  _fix((1, D_MODEL)), _fix((1, Q_LORA)), _fix((1, KV_LORA)), ANY, ANY, ANY, ANY],
        out_specs=[_row(tm, D_MODEL), _row(tm, IN_R), _row(tm, N_HEADS * HEAD_PAD), _row(tm, Q_LORA), _row(tm, KV_LORA),
                   _row(tm, 2 * K_ALL),
                   _fix((1, D_MODEL)), _fix((1, Q_LORA)), _fix((1, KV_LORA))],
        out_shape=[_sds((seq, D_MODEL), F32), _sds((seq, IN_R), BF16), _sds((seq, N_HEADS * HEAD_PAD), BF16),
                   _sds((seq, Q_LORA), BF16), _sds((seq, KV_LORA), BF16), _sds((seq, 2 * K_ALL), BF16),
                   _sds((1, D_MODEL), F32), _sds((1, Q_LORA), F32), _sds((1, KV_LORA), F32)],
        scratch_shapes=[pltpu.VMEM(w_in_r.shape, BF16), pltpu.VMEM(w_uq_r.shape, BF16),
                        pltpu.VMEM(w_k_exp.shape, BF16), pltpu.VMEM(w_v.shape, BF16), pltpu.SemaphoreType.DMA((4,))],
        compiler_params=_params(1, 60),
    )(dq, dk, dv, qd, kvd, x, dh1, dgpre, dd, dd, cf, s1, s2, g_pre, g_q, g_kv, w_in_r, w_uq_r, w_k_exp, w_v)


def _grad_w(a, b, name, square_a=False, by_device=False, with_bf16=False):
    seq, k_dim = a.shape
    n_dim = b.shape[1]
    tk = min(k_dim, 1024)
    tn = n_dim // 2 if n_dim == IN_R else min(n_dim, 1024)
    ts = min(seq, 2048)
    shard = n_dim // N_DEV
    per_tile = tn // shard
    if by_device:
        out_spec = pl.BlockSpec((per_tile, tk, shard), lambda i, j, s: (j, i, 0))
        out_shape = _sds((N_DEV, k_dim, shard), F32)
    else:
        out_spec = pl.BlockSpec((tk, tn), lambda i, j, s: (i, j))
        out_shape = _sds((k_dim, n_dim), F32)

    n_seq_steps = seq // ts

    def body(a_ref, b_ref, o_ref, *narrow):
        @pl.when(pl.program_id(2) == 0)
        def _():
            o_ref[...] = jnp.zeros_like(o_ref)

        at = a_ref[...]
        if square_a:
            at = at * at
        part = lax.dot_general(at, b_ref[...], TN, preferred_element_type=F32)
        if by_device:
            for d in range(per_tile):
                o_ref[d] += part[:, d * shard:(d + 1) * shard]
        else:
            o_ref[...] += part
        if with_bf16:
            @pl.when(pl.program_id(2) == n_seq_steps - 1)
            def _():
                narrow[0][...] = o_ref[...].astype(BF16)

    return pl.pallas_call(
        body, name=name, grid=(k_dim // tk, n_dim // tn, n_seq_steps),
        in_specs=[pl.BlockSpec((ts, tk), lambda i, j, s: (s, i)), pl.BlockSpec((ts, tn), lambda i, j, s: (s, j))],
        out_specs=[out_spec, out_spec] if with_bf16 else out_spec,
        out_shape=[out_shape, _sds(out_shape.shape, BF16)] if with_bf16 else out_shape,
        compiler_params=_params(3, 48),
    )(a, b)


def _position():
    return lax.axis_index("x"), lax.axis_index("y"), lax.axis_index("c")


def _gather_copies(x_ref, slot, send_sems, recv_sems, local_sem, phases=("send", "forward", "finish"), sem_base=0):
    x, y, c = _position()
    me, sibling = (x, y, c), (x, y, 1 - c)
    chips = [(1 - x, y), (x, 1 - y), (1 - x, 1 - y)]

    def copy(k, block, to, src=None):
        return pltpu.make_async_remote_copy(
            src_ref=slot(*block) if src is None else src, dst_ref=slot(*block),
            send_sem=send_sems.at[sem_base + k], recv_sem=recv_sems.at[sem_base + k],
            device_id=to, device_id_type=MESH)

    mine = pltpu.make_async_copy(x_ref, slot(*me), local_sem)
    first = [copy(0, me, sibling, src=x_ref)]
    first += [copy(1 + j, me, (*chip, c), src=x_ref) for j, chip in enumerate(chips)]
    passed = [copy(4 + j, (*chip, c), sibling) for j, chip in enumerate(chips)]
    if "send" in phases:
        mine.start()
        for cp in first:
            cp.start()
    if "forward" in phases:
        for j, chip in enumerate(chips):
            copy(1 + j, (*chip, c), me).wait_recv()
            passed[j].start()
    if "finish" in phases:
        copy(0, sibling, me).wait_recv()
        for j, chip in enumerate(chips):
            copy(4 + j, (*chip, 1 - c), me).wait_recv()
        for cp in first + passed:
            cp.wait_send()
        mine.wait()


def _all_gather_hbm(block):
    def body(x_ref, out_ref, send_sems, recv_sems, local_sem):
        _gather_copies(x_ref, lambda px, py, pc: out_ref.at[4 * px + 2 * py + pc], send_sems, recv_sems, local_sem)

    return pl.pallas_call(
        body, name="gather_weights",
        in_specs=[ANY], out_specs=ANY,
        out_shape=_sds((N_DEV,) + block.shape, block.dtype),
        scratch_shapes=[pltpu.SemaphoreType.DMA((7,)), pltpu.SemaphoreType.DMA((7,)), pltpu.SemaphoreType.DMA],
    )(block)


def _replicated_update(grads, loss_block, ws, ms, vs):
    n_p = len(grads)
    sent = list(grads) + [loss_block]
    n_a = len(sent)

    def body(*refs):
        g_refs, refs = refs[:n_a], refs[n_a:]
        w_refs, m_refs, v_refs, refs = refs[:n_p], refs[n_p:2 * n_p], refs[2 * n_p:3 * n_p], refs[3 * n_p:]
        sum_refs, refs = refs[:n_a], refs[n_a:]
        d_refs, nm_refs, nv_refs, refs = refs[:n_p], refs[n_p:2 * n_p], refs[2 * n_p:3 * n_p], refs[3 * n_p:]
        bufs, (send_sems, recv_sems, local_sems) = refs[:n_a], refs[n_a:]
        x, y, c = _position()
        me = 4 * x + 2 * y + c
        local, remote = [], []
        for a in range(n_a):
            local.append(pltpu.make_async_copy(g_refs[a], bufs[a].at[me], local_sems.at[a]))
            for r in range(1, N_DEV):
                peer = (1 - x if r & 4 else x, 1 - y if r & 2 else y, 1 - c if r & 1 else c)
                remote.append(pltpu.make_async_remote_copy(
                    src_ref=g_refs[a], dst_ref=bufs[a].at[me],
                    send_sem=send_sems.at[(N_DEV - 1) * a + r - 1], recv_sem=recv_sems.at[(N_DEV - 1) * a + r - 1],
                    device_id=peer, device_id_type=MESH))
        for cp in local + remote:
            cp.start()
        for cp in remote:
            cp.wait_recv()
        for cp in remote:
            cp.wait_send()
        for cp in local:
            cp.wait()
        for a in range(n_a):
            acc = bufs[a][0]
            for d in range(1, N_DEV):
                acc = acc + bufs[a][d]
            if a == n_p:
                sum_refs[a][...] = acc
                continue
            delta, new_m, new_v = _adamw_math(acc, w_refs[a][...], m_refs[a][...], v_refs[a][...])
            sum_refs[a][...], d_refs[a][...], nm_refs[a][...], nv_refs[a][...] = acc, delta, new_m, new_v

    vmem = pl.BlockSpec(memory_space=pltpu.VMEM)
    like_w = [_sds(w.shape, F32) for w in ws]
    n_sem = (N_DEV - 1) * n_a
    outs = pl.pallas_call(
        body, name="replicated_update",
        in_specs=[vmem] * (n_a + 3 * n_p), out_specs=[vmem] * (n_a + 3 * n_p),
        out_shape=like_w + [_sds(loss_block.shape, F32)] + like_w * 3,
        scratch_shapes=[pltpu.VMEM((N_DEV,) + g.shape, F32) for g in sent]
                       + [pltpu.SemaphoreType.DMA((n_sem,)), pltpu.SemaphoreType.DMA((n_sem,)),
                          pltpu.SemaphoreType.DMA((n_a,))],
        compiler_params=pltpu.CompilerParams(vmem_limit_bytes=32 * MIB),
    )(*sent, *ws, *ms, *vs)
    return (outs[:n_p], outs[n_p], outs[n_a:n_a + n_p], outs[n_a + n_p:n_a + 2 * n_p], outs[n_a + 2 * n_p:])


def _exchange_pair(gs):
    n_w = len(gs)

    def body(*refs):
        g_refs, out_refs = refs[:n_w], refs[n_w:2 * n_w]
        send_sems, recv_sems = refs[2 * n_w:]
        x, y, c = _position()
        copies = []
        for w in range(n_w):
            for chip in range(4):
                cp = pltpu.make_async_remote_copy(
                    src_ref=g_refs[w].at[2 * chip + (1 - c)], dst_ref=out_refs[w].at[chip],
                    send_sem=send_sems.at[4 * w + chip], recv_sem=recv_sems.at[4 * w + chip],
                    device_id=(x, y, 1 - c), device_id_type=MESH)
                cp.start()
                copies.append(cp)
        for cp in copies:
            cp.wait_recv()
        for cp in copies:
            cp.wait_send()

    return pl.pallas_call(
        body, name="exchange_pair",
        in_specs=[ANY] * n_w, out_specs=[ANY] * n_w,
        out_shape=[_sds((4,) + g.shape[1:], g.dtype) for g in gs],
        scratch_shapes=[pltpu.SemaphoreType.DMA((4 * n_w,)), pltpu.SemaphoreType.DMA((4 * n_w,))],
    )(*gs)


def _exchange_chips(parts):
    n_w = len(parts)

    def body(*refs):
        p_refs, out_refs = refs[:n_w], refs[n_w:2 * n_w]
        send_sems, recv_sems = refs[2 * n_w:]
        x, y, c = _position()
        chips = [(1 - x, y), (x, 1 - y), (1 - x, 1 - y)]
        copies = []
        for w in range(n_w):
            for k, (px, py) in enumerate(chips):
                cp = pltpu.make_async_remote_copy(
                    src_ref=p_refs[w].at[2 * px + py], dst_ref=out_refs[w].at[k],
                    send_sem=send_sems.at[3 * w + k], recv_sem=recv_sems.at[3 * w + k],
                    device_id=(px, py, c), device_id_type=MESH)
                cp.start()
                copies.append(cp)
        for cp in copies:
            cp.wait_recv()
        for cp in copies:
            cp.wait_send()

    return pl.pallas_call(
        body, name="exchange_chips",
        in_specs=[ANY] * n_w, out_specs=[ANY] * n_w,
        out_shape=[_sds((3,) + p.shape[1:], p.dtype) for p in parts],
        scratch_shapes=[pltpu.SemaphoreType.DMA((3 * n_w,)), pltpu.SemaphoreType.DMA((3 * n_w,))],
    )(*parts)


def _row_tile(k):
    return 256 if k % 256 == 0 else 128


def _pair_sum(g, recv, place, name):
    _, k, n = g.shape
    tr = _row_tile(k)
    g4 = g.reshape(4, 2, k, n)

    def body(s_ref, g_ref, r_ref, o_ref):
        o_ref[...] = (g_ref[...] + r_ref[...]).astype(BF16)

    spec = pltpu.PrefetchScalarGridSpec(
        num_scalar_prefetch=1, grid=(4, k // tr),
        in_specs=[pl.BlockSpec((None, None, tr, n), lambda j, i, s: (j, s[2], i, 0)),
                  pl.BlockSpec((None, tr, n), lambda j, i, s: (j, i, 0))],
        out_specs=pl.BlockSpec((None, tr, n), lambda j, i, s: (j, i, 0)))
    return pl.pallas_call(
        body, name=name, grid_spec=spec, out_shape=_sds((4, k, n), BF16),
        compiler_params=_params(2, 32),
    )(place, g4, recv)


def _adamw_math(g, w, m, v):
    m = ADAM_B1 * m + (1.0 - ADAM_B1) * g
    v = ADAM_B2 * v + (1.0 - ADAM_B2) * (g * g)
    m_hat = m / (1.0 - ADAM_B1 ** ADAM_STEP)
    v_hat = v / (1.0 - ADAM_B2 ** ADAM_STEP)
    delta = -ADAM_LR * (m_hat / (jnp.sqrt(v_hat) + ADAM_EPS) + ADAM_WD * w)
    return delta, m, v


def _adamw_sharded(g, from_sibling, from_chips, place, w, m, v, name):
    _, k, n = g.shape
    tr = _row_tile(k)

    def body(s_ref, g_ref, sib_ref, r0_ref, r1_ref, r2_ref, w_ref, m_ref, v_ref, grad_ref, d_ref, nm_ref, nv_ref):
        grad = g_ref[...] + sib_ref[...]
        for r_ref in (r0_ref, r1_ref, r2_ref):
            grad = grad + r_ref[...].astype(F32)
        grad_ref[...] = grad
        d_ref[...], nm_ref[...], nv_ref[...] = _adamw_math(grad, w_ref[...], m_ref[...], v_ref[...])

    tile = pl.BlockSpec((None, tr, n), lambda i, s: (0, i, 0))

    def slot(j):
        return pl.BlockSpec((None, tr, n), lambda i, s: (j, i, 0))

    spec = pltpu.PrefetchScalarGridSpec(
        num_scalar_prefetch=1, grid=(k // tr,),
        in_specs=[pl.BlockSpec((None, tr, n), lambda i, s: (s[0], i, 0)),
                  pl.BlockSpec((None, tr, n), lambda i, s: (s[1], i, 0)),
                  slot(0), slot(1), slot(2), tile, tile, tile],
        out_specs=[tile] * 4)
    return pl.pallas_call(
        body, name=name, grid_spec=spec, out_shape=[_sds((1, k, n), F32)] * 4,
        compiler_params=_params(1, 48),
    )(place, g, from_sibling, from_chips, from_chips, from_chips, w, m, v)


def _adamw_direct(g, received, place, w, m, v, name):
    _, k, n = g.shape
    tr = _row_tile(k)

    def body(s_ref, g_ref, r_ref, w_ref, m_ref, v_ref, grad_ref, d_ref, nm_ref, nv_ref):
        grad = g_ref[...]
        for r in range(N_DEV - 1):
            grad = grad + r_ref[r].astype(F32)
        grad_ref[...] = grad
        d_ref[...], nm_ref[...], nv_ref[...] = _adamw_math(grad, w_ref[...], m_ref[...], v_ref[...])

    tile = pl.BlockSpec((None, tr, n), lambda i, s: (0, i, 0))
    spec = pltpu.PrefetchScalarGridSpec(
        num_scalar_prefetch=1, grid=(k // tr,),
        in_specs=[pl.BlockSpec((None, tr, n), lambda i, s: (s[0], i, 0)),
                  pl.BlockSpec((N_DEV - 1, tr, n), lambda i, s: (0, i, 0)), tile, tile, tile],
        out_specs=[tile] * 4)
    return pl.pallas_call(
        body, name=name, grid_spec=spec, out_shape=[_sds((1, k, n), F32)] * 4,
        compiler_params=_params(1, 48),
    )(place, g, received, w, m, v)


def _pack_rows(parts):
    parts = [a.reshape(-1, LANES) for a in parts]
    pad = (-sum(a.shape[0] for a in parts)) % PACK_ROW_TILE
    return jnp.concatenate(parts + [jnp.zeros((pad, LANES), parts[0].dtype)], axis=0)


def _full_from_gathered(gathered, entries, shard_shapes):
    out, off = {}, 0
    for (name, kind), (k, n) in zip(entries, shard_shapes):
        rows = k * n // LANES
        seg = gathered[:, off:off + rows].reshape(N_DEV, k, n)
        out[name] = jnp.transpose(seg, (1, 0, 2)).reshape(k, N_DEV * n) if kind == "col" else seg.reshape(N_DEV * k, n)
        off += rows
    return out


def _columns_by_device(a):
    k, n_all = a.shape
    return jnp.transpose(a.reshape(k, N_DEV, n_all // N_DEV), (1, 0, 2))


def _rows_by_device(a):
    k_all, n = a.shape
    return a.reshape(N_DEV, k_all // N_DEV, n)


def _rope_lane_frequencies():
    inv_freq = ROPE_THETA ** (-jnp.arange(0, QK_ROPE, 2, dtype=F32) / QK_ROPE)
    zeros = lambda n: jnp.zeros((n,), F32)
    return jnp.concatenate([zeros(QK_NOPE), inv_freq, inv_freq, zeros(HEAD_PAD - QK_NOPE - QK_ROPE)])[None, :]


def _rope_tables(pos_row, freq, tm):
    pos = jnp.transpose(jnp.broadcast_to(pos_row.astype(F32), (HEAD_PAD, tm)))
    ang = pos * freq
    cos, sin = jnp.cos(ang), jnp.sin(ang)
    lane = lax.broadcasted_iota(jnp.int32, (tm, HEAD_PAD), 1)
    first = (lane >= QK_NOPE) & (lane < QK_NOPE + QK_ROPE // 2)
    second = (lane >= QK_NOPE + QK_ROPE // 2) & (lane < QK_NOPE + QK_ROPE)
    cf = jnp.where(lane < QK_NOPE, 1.0, jnp.where(first | second, cos, 0.0))
    return cf, jnp.where(first, -sin, 0.0), jnp.where(second, sin, 0.0)


def _arrange_w_in(w):
    k = w.shape[0]
    zeros = lambda n: jnp.zeros((k, n), w.dtype)
    kr0 = Q_LORA + KV_LORA
    pool0 = kr0 + QK_ROPE
    return jnp.concatenate([w[:, :kr0], w[:, pool0:], zeros(QK_NOPE), w[:, kr0:pool0],
                            zeros(HEAD_PAD - QK_NOPE - QK_ROPE)], axis=1)


def _restore_w_in(d):
    kr = d[:, IN_KR0 + QK_NOPE:IN_KR0 + QK_NOPE + QK_ROPE]
    return jnp.concatenate([d[:, :IN_POOL0], kr, d[:, IN_POOL0:IN_KR0]], axis=1)


def _pad_heads(w, width):
    k = w.shape[0]
    w = w.reshape(k, N_HEADS, width)
    return jnp.pad(w, ((0, 0), (0, 0), (0, HEAD_PAD - width))).reshape(k, N_HEADS * HEAD_PAD)


def _unpad_heads(d, width):
    k = d.shape[0]
    return d.reshape(k, N_HEADS, HEAD_PAD)[:, :, :width]


def kernel(x, p, positions, g_pre_mix, w_in, b_gate, g_q, w_uq, g_kv, w_ukv, w_pool, pool_scale, w_branch_attn, w_branch_pool, w_out, g_post_mix, g_pre_mlp, w_ff1, w_ff2, g_post_mlp, w_ple_proj, w_ple_gate, g_ple, loss_target, m_g_pre_mix, m_w_in, m_b_gate, m_g_q, m_w_uq, m_g_kv, m_w_ukv, m_w_pool, m_pool_scale, m_w_branch_attn, m_w_branch_pool, m_w_out, m_g_post_mix, m_g_pre_mlp, m_w_ff1, m_w_ff2, m_g_post_mlp, m_w_ple_proj, m_w_ple_gate, m_g_ple, v_g_pre_mix, v_w_in, v_b_gate, v_g_q, v_w_uq, v_g_kv, v_w_ukv, v_w_pool, v_pool_scale, v_w_branch_attn, v_w_branch_pool, v_w_out, v_g_post_mix, v_g_pre_mlp, v_w_ff1, v_w_ff2, v_g_post_mlp, v_w_ple_proj, v_w_ple_gate, v_g_ple):
    given = dict(locals())
    weights = {n: given[n] for n in WEIGHT_ORDER}
    moments_m = {n: given["m_" + n] for n in WEIGHT_ORDER}
    moments_v = {n: given["v_" + n] for n in WEIGHT_ORDER}
    xs, ps, target = x[0], p[0, 0], loss_target[0]
    seq = xs.shape[0]
    tm = min(256, seq)
    tm_merge = min(512, seq)
    core = lax.axis_index("c")
    chip = 2 * lax.axis_index("x") + lax.axis_index("y")

    early, later = SHARDED[:N_EARLY], SHARDED[N_EARLY:]
    shapes_of = lambda entries: [weights[n].shape[1:] for n, _ in entries]
    pack_bf16 = lambda entries: _pack_rows([weights[n][0].astype(BF16) for n, _ in entries])
    full = _full_from_gathered(_all_gather_hbm(pack_bf16(early)), early, shapes_of(early))
    w_in_r = _arrange_w_in(full["w_in"])
    w_uq_r = _pad_heads(full["w_uq"], QK_NOPE + QK_ROPE)
    ukv = full["w_ukv"].reshape(KV_LORA, N_HEADS, QK_NOPE + V_HEAD)
    w_k_exp = _pad_heads(ukv[:, :, :QK_NOPE].reshape(KV_LORA, N_HEADS * QK_NOPE), QK_NOPE)
    w_v = _pad_heads(ukv[:, :, QK_NOPE:].reshape(KV_LORA, N_HEADS * V_HEAD), V_HEAD)
    w_pool_bf = w_pool[0].astype(BF16)

    packed_later = [e for e in later if e[0] != "w_ff1"]
    a_bf, qd, kvd, d_pool, pooled, gates, q, k, v, k_t, cf, s1, s2, gathered_later, gathered_ff1 = _proj_fwd(
        xs, g_pre_mix, b_gate, g_q, g_kv, positions, w_pool_bf, pool_scale, w_in_r, w_uq_r, w_k_exp, w_v,
        [pack_bf16(packed_later), w_ff1[0].astype(BF16)], tm_merge)
    full.update(_full_from_gathered(gathered_later, packed_later, shapes_of(packed_later)))
    full["w_ff1"] = gathered_ff1
    w_ba = jnp.pad(full["w_branch_attn"].reshape(N_HEADS, V_HEAD, D_MODEL),
                   ((0, 0), (0, HEAD_PAD - V_HEAD), (0, 0))).reshape(N_HEADS * HEAD_PAD, D_MODEL)
    o_heads, lse = _attn_fwd(q, k, v)
    merged, ba, bb, y, h1, attn_rows = _merge_fwd(o_heads, pooled, gates, xs, g_post_mix, w_ba,
                                                  full["w_branch_pool"], full["w_out"], tm_merge)
    (m_bf, zr, f, h2_bf, p_bf, de, dpre, dh2, loss_acc, dg_ple) = _tail_fwd(
        h1, target, ps, g_pre_mlp, g_post_mlp, g_ple, full["w_ff1"], full["w_ff2"], full["w_ple_proj"],
        full["w_ple_gate"], tm)

    by_device, payload = {}, {}

    def keep(name, pair, layout=lambda g: g):
        by_device[name], payload[name] = layout(pair[0]), layout(pair[1])

    keep("w_ple_proj", _grad_w(p_bf, de, "grad_w_ple_proj", by_device=True, with_bf16=True))
    keep("w_ple_gate", _grad_w(h2_bf, dpre, "grad_w_ple_gate", with_bf16=True), _rows_by_device)
    df, dz, dh1, dg_pre_mlp, dg_post_mlp = _mlp_bwd(h1, f, zr, dh2, g_pre_mlp, g_post_mlp, full["w_ff1"],
                                                    full["w_ff2"], tm)
    keep("w_ff1", _grad_w(m_bf, dz, "grad_w_ff1", by_device=True, with_bf16=True))
    keep("w_ff2", _grad_w(zr, df, "grad_w_ff2", square_a=True, with_bf16=True), _rows_by_device)
    (dy, dba, dbb, dgpre, do_heads, dd, delta, dg_post_mix, db_gate, dpool_scale, dw_pool) = _merge_bwd(
        dh1, y, gates, ba, bb, d_pool, o_heads, g_post_mix, pool_scale, full["w_out"], w_ba,
        full["w_branch_pool"], w_pool_bf, tm_merge)
    keep("w_branch_attn", _grad_w(attn_rows, dba, "grad_w_branch_attn", with_bf16=True),
          lambda g: _columns_by_device(g.reshape(N_HEADS, HEAD_PAD, D_MODEL)[:, :V_HEAD].reshape(-1, D_MODEL)))
    keep("w_branch_pool", _grad_w(pooled, dbb, "grad_w_branch_pool", by_device=True, with_bf16=True))
    keep("w_out", _grad_w(merged, dy, "grad_w_out", with_bf16=True), _rows_by_device)
    direct = [n for n, _ in SHARDED[N_EARLY:]]
    outs = _attn_bwd(q, k, k_t, v, do_heads, lse, delta, [payload[n] for n in direct])
    dq, dk, dv = outs[:3]
    received = dict(zip(direct, outs[3:]))
    (grad_x, dproj, dq_bf, qn_bf, kvn_bf, dkv_bf, dg_pre_mix, dg_q, dg_kv) = _proj_bwd(
        dq, dk, dv, qd, kvd, xs, dh1, dgpre, dd, cf, s1, s2, g_pre_mix, g_q, g_kv, w_in_r, w_uq_r, w_k_exp, w_v,
        tm_merge)
    d_w_kv = _grad_w(kvn_bf, dkv_bf, "grad_w_ukv")
    d_k_exp = _unpad_heads(d_w_kv[:, :K_ALL], QK_NOPE)
    d_w_v = _unpad_heads(d_w_kv[:, K_ALL:], V_HEAD)
    by_device["w_in"] = _columns_by_device(_restore_w_in(_grad_w(a_bf, dproj, "grad_w_in")))
    by_device["w_uq"] = _columns_by_device(
        _unpad_heads(_grad_w(qn_bf, dq_bf, "grad_w_uq"), QK_NOPE + QK_ROPE).reshape(Q_LORA, -1))
    by_device["w_ukv"] = _columns_by_device(jnp.concatenate([d_k_exp, d_w_v], axis=2).reshape(KV_LORA, -1))
    grads_small = {
        "g_pre_mix": dg_pre_mix, "b_gate": db_gate, "g_q": dg_q, "g_kv": dg_kv,
        "w_pool": dw_pool, "pool_scale": dpool_scale, "g_post_mix": dg_post_mix,
        "g_pre_mlp": dg_pre_mlp, "g_post_mlp": dg_post_mlp, "g_ple": dg_ple,
    }

    names = [n for n, _ in SHARDED]
    place = jnp.stack([2 * chip + core, chip, core]).astype(jnp.int32)
    sharded = {n: _adamw_direct(by_device[n], received[n], place, weights[n], moments_m[n], moments_v[n],
                                "adamw_" + n) for n in direct}
    last = [n for n, _ in SHARDED[:N_EARLY]]
    own = [by_device[n] for n in last]
    from_sibling = _exchange_pair(own)
    pair = [_pair_sum(g, r, place, "pair_sum_" + n) for n, g, r in zip(last, own, from_sibling)]
    from_chips = _exchange_chips(pair)
    sharded.update({n: _adamw_sharded(g, r, rc, place, weights[n], moments_m[n], moments_v[n], "adamw_" + n)
                    for n, g, r, rc in zip(last, own, from_sibling, from_chips)})

    flat = lambda a: a.reshape(a.shape[-3:]) if a.ndim > 3 else a
    g_sm, loss_sum, d_sm, m_sm, v_sm = _replicated_update(
        [flat(grads_small[n]) for n in REPLICATED], loss_acc, [flat(weights[n]) for n in REPLICATED],
        [flat(moments_m[n]) for n in REPLICATED], [flat(moments_v[n]) for n in REPLICATED])

    results = []
    for which, small in enumerate((g_sm, d_sm, m_sm, v_sm)):
        named = {n: sharded[n][which] for n in names}
        named.update({n: a.reshape(weights[n].shape) for n, a in zip(REPLICATED, small)})
        results.append([named[n] for n in WEIGHT_ORDER])

    return (loss_sum[0, 0], grad_x[None], *results[0], *results[1], *results[2], *results[3])
```

```python
import jax
import jax.numpy as jnp
from jax import lax
from jax.experimental import pallas as pl
from jax.experimental.pallas import tpu as pltpu

F32 = jnp.float32
BF16 = jnp.bfloat16

D_MODEL = 1024
PLE_DIM = 256
N_HEADS = 8
QK_NOPE = 64
QK_ROPE = 32
V_HEAD = 64
Q_LORA = 384
KV_LORA = 256
POOL_WINDOWS = (2, 4, 8, 16)
POOL_GROUP = 128
POOL_WIDTH = 512
D_FF = 4096
ROPE_THETA = 10000.0
EPS = 1e-6
HEAD_PAD = 128
K_ALL = N_HEADS * HEAD_PAD
ATTN_SCALE = (QK_NOPE + QK_ROPE) ** -0.5
LOG2E = 1.4426950408889634
Q_PRESCALE = ATTN_SCALE * LOG2E
ATTN_TILE = 512
FWD_ROWS = 512
FWD_CHAINS = 4
BWD_CHAINS = 2
BWD_QUERIES = 4

ADAM_LR = 0.001
ADAM_B1 = 0.9
ADAM_B2 = 0.999
ADAM_EPS = 1e-08
ADAM_WD = 0.01
ADAM_STEP = 10

N_DEV = 8
LANES = 1024
PACK_ROW_TILE = 480
POOL_HALO = 16
MIB = 2 ** 20

IN_Q0, IN_KV0, IN_POOL0, IN_GATE0, IN_KR0, IN_R = 0, 384, 640, 1152, 3200, 3328

SHARDED = (("w_in", "col"), ("w_uq", "col"), ("w_ukv", "col"), ("w_branch_attn", "col"),
           ("w_branch_pool", "col"), ("w_out", "row"), ("w_ff1", "col"), ("w_ff2", "row"),
           ("w_ple_proj", "col"), ("w_ple_gate", "row"))
N_EARLY = 3
REPLICATED = ("g_pre_mix", "b_gate", "g_q", "g_kv", "w_pool", "pool_scale", "g_post_mix",
              "g_pre_mlp", "g_post_mlp", "g_ple")
WEIGHT_ORDER = ("g_pre_mix", "w_in", "b_gate", "g_q", "w_uq", "g_kv", "w_ukv", "w_pool", "pool_scale",
                "w_branch_attn", "w_branch_pool", "w_out", "g_post_mix", "g_pre_mlp", "w_ff1", "w_ff2",
                "g_post_mlp", "w_ple_proj", "w_ple_gate", "g_ple")

NT = (((1,), (1,)), ((), ()))
TN = (((0,), (0,)), ((), ()))
MESH = pl.DeviceIdType.MESH
ANY = pl.BlockSpec(memory_space=pl.ANY)


def _params(n_axes, vmem_mib):
    return pltpu.CompilerParams(dimension_semantics=("arbitrary",) * n_axes, vmem_limit_bytes=vmem_mib * MIB)


def _row(tm, n):
    return pl.BlockSpec((tm, n), lambda i: (i, 0))


def _fix(shape):
    zeros = (0,) * len(shape)
    return pl.BlockSpec(shape, lambda i: zeros)


def _sds(shape, dtype):
    return jax.ShapeDtypeStruct(shape, dtype)


def _rms_r(v):
    return lax.rsqrt(jnp.mean(v * v, axis=-1, keepdims=True) + EPS)


def _rms_bwd(vhat, r, g, dy):
    gdy = dy * g
    return r * (gdy - vhat * jnp.mean(gdy * vhat, axis=-1, keepdims=True))


def _colsum(v):
    return jnp.sum(v, axis=0, keepdims=True)


def _sigmoid(v):
    return 1.0 / (1.0 + jnp.exp(-v))


def _mm(a, b):
    return jnp.dot(a, b, preferred_element_type=F32)


def _mm_nt(a, b):
    return lax.dot_general(a, b, NT, preferred_element_type=F32)


def _rope(c, cf, s1, s2):
    return c * cf + pltpu.roll(c, HEAD_PAD - 16, 1) * s1 + pltpu.roll(c, 16, 1) * s2


def _rope_t(c, cf, s1, s2):
    return c * cf + pltpu.roll(c * s1, 16, 1) + pltpu.roll(c * s2, HEAD_PAD - 16, 1)


def _row_chains(tm, rows=256):
    rows = min(rows, tm)
    return [slice(c * rows, (c + 1) * rows) for c in range(tm // rows)]


def _load_once(pairs, sems):
    @pl.when(pl.program_id(0) == 0)
    def _():
        copies = [pltpu.make_async_copy(src, dst, sems.at[i]) for i, (src, dst) in enumerate(pairs)]
        for cp in copies:
            cp.start()
        for cp in copies:
            cp.wait()


def _column_blocks(by_device_hbm, full_vmem):
    n = by_device_hbm.shape[2]
    return tuple((by_device_hbm.at[d], full_vmem.at[:, d * n:(d + 1) * n]) for d in range(N_DEV))


def _proj_fwd(x, g_pre, b_gate, g_q, g_kv, positions, w_pool_bf, pool_scale, w_in_r, w_uq_r, w_k_exp, w_v,
              later_shards, tm):
    seq = x.shape[0]
    n_steps = seq // tm
    forward_step = (3 * n_steps) // 4
    n_later = len(later_shards)

    def body(x_ref, gpre_ref, bg_ref, gq_ref, gkv_ref, pos_ref, freq_ref, wpool_ref, pscale_ref,
             win_hbm, wuq_hbm, wk_hbm, wv_hbm, *rest):
        later_refs, rest = rest[:n_later], rest[n_later:]
        (a_ref, qd_ref, kvd_ref, dpool_ref, pooled_ref, gates_ref, q_ref, k_ref, v_ref, kt_ref,
         cf_ref, s1_ref, s2_ref) = rest[:13]
        gathered_refs, rest = rest[13:13 + n_later], rest[13 + n_later:]
        win, wuq, wk, wv, halo_ref, send_sems, recv_sems, local_sems, load_sems = rest
        step = pl.program_id(0)

        def gather(phase):
            for a, (src, dst) in enumerate(zip(later_refs, gathered_refs)):
                _gather_copies(src, lambda px, py, pc, dst=dst: dst.at[4 * px + 2 * py + pc],
                               send_sems, recv_sems, local_sems.at[a], phases=(phase,), sem_base=7 * a)

        pl.when(step == 0)(lambda: gather("send"))
        pl.when(step == forward_step)(lambda: gather("forward"))
        _load_once(((win_hbm, win), (wuq_hbm, wuq), (wk_hbm, wk), (wv_hbm, wv)), load_sems)
        for rows in _row_chains(tm):
            n_rows = rows.stop - rows.start
            xv = x_ref[rows, :]
            a = (xv * _rms_r(xv) * gpre_ref[...]).astype(BF16)
            a_ref[rows, :] = a
            proj = _mm(a, win[...])
            qd = proj[:, IN_Q0:IN_KV0]
            kvd = proj[:, IN_KV0:IN_POOL0]
            qd_ref[rows, :] = qd
            kvd_ref[rows, :] = kvd
            gates_ref[rows, :] = _sigmoid(proj[:, IN_GATE0:IN_KR0] + bg_ref[...]).astype(BF16)

            u = proj[:, IN_POOL0:IN_GATE0]
            before = jnp.where(step == 0, 0.0, halo_ref[...]) if rows.start == 0 else tail
            tail = u[n_rows - POOL_HALO:, :]
            level = jnp.concatenate([before, u], axis=0)
            counts = _window_count(step * tm + rows.start, n_rows)
            shift = 1
            for g in range(len(POOL_WINDOWS)):
                level = level + pltpu.roll(level, shift, 0)
                shift *= 2
                lanes = slice(POOL_GROUP * g, POOL_GROUP * (g + 1))
                d = (level[POOL_HALO:, lanes] / counts[g] - u[:, lanes]).astype(BF16)
                dpool_ref[rows, lanes] = d
                pooled_ref[rows, lanes] = (_mm(d, wpool_ref[g]) * pscale_ref[:, lanes]).astype(BF16)
            if rows.stop == tm:
                halo_ref[...] = tail
            cfv, s1v, s2v = _rope_tables(pos_ref[:, rows], freq_ref[...], n_rows)
            cf_ref[rows, :], s1_ref[rows, :], s2_ref[rows, :] = cfv, s1v, s2v
            krr = _rope(proj[:, IN_KR0:IN_R], cfv, s1v, s2v)
            qn = (qd * _rms_r(qd) * gq_ref[...]).astype(BF16)
            qf = _mm(qn, wuq[...])
            kvn = (kvd * _rms_r(kvd) * gkv_ref[...]).astype(BF16)
            kf = _mm(kvn, wk[...])
            vf = _mm(kvn, wv[...])
            one_lane = (lax.broadcasted_iota(jnp.int32, (n_rows, HEAD_PAD), 1) == V_HEAD).astype(F32)
            for h in range(N_HEADS):
                lanes = slice(HEAD_PAD * h, HEAD_PAD * (h + 1))
                q_ref[h, rows, :] = (_rope(qf[:, lanes], cfv, s1v, s2v) * Q_PRESCALE).astype(BF16)
                kh = kf[:, lanes] + krr
                vh = vf[:, lanes] + one_lane
                k_ref[h, rows, :] = kh.astype(BF16)
                v_ref[h, rows, :] = vh.astype(BF16)
                kt_ref[h, :, rows] = jnp.transpose(kh).astype(BF16)
        pl.when(step == n_steps - 1)(lambda: gather("finish"))

    per_tile = ATTN_TILE // tm
    heads = pl.BlockSpec((N_HEADS, tm, HEAD_PAD), lambda i: (0, i, 0))
    heads_t = pl.BlockSpec((N_HEADS, None, HEAD_PAD, tm), lambda i: (0, i // per_tile, 0, i % per_tile))
    heads_t_shape = _sds((N_HEADS, seq // ATTN_TILE, HEAD_PAD, ATTN_TILE), BF16)
    return pl.pallas_call(
        body, name="proj_fwd", grid=(seq // tm,),
        in_specs=[_row(tm, D_MODEL), _fix((1, D_MODEL)), _fix((1, 2 * D_MODEL)), _fix((1, Q_LORA)), _fix((1, KV_LORA)),
                  pl.BlockSpec((1, tm), lambda i: (0, i)), _fix((1, HEAD_PAD)), _fix(w_pool_bf.shape), _fix((1, POOL_WIDTH)),
                  ANY, ANY, ANY, ANY] + [ANY] * n_later,
        out_specs=[_row(tm, D_MODEL), _row(tm, Q_LORA), _row(tm, KV_LORA), _row(tm, POOL_WIDTH), _row(tm, POOL_WIDTH),
                   _row(tm, 2 * D_MODEL), heads, heads, heads, heads_t,
                   _row(tm, HEAD_PAD), _row(tm, HEAD_PAD), _row(tm, HEAD_PAD)] + [ANY] * n_later,
        out_shape=[_sds((seq, D_MODEL), BF16), _sds((seq, Q_LORA), F32), _sds((seq, KV_LORA), F32),
                   _sds((seq, POOL_WIDTH), BF16), _sds((seq, POOL_WIDTH), BF16), _sds((seq, 2 * D_MODEL), BF16),
                   _sds((N_HEADS, seq, HEAD_PAD), BF16), _sds((N_HEADS, seq, HEAD_PAD), BF16),
                   _sds((N_HEADS, seq, HEAD_PAD), BF16), heads_t_shape,
                   _sds((seq, HEAD_PAD), F32), _sds((seq, HEAD_PAD), F32), _sds((seq, HEAD_PAD), F32)]
                  + [_sds((N_DEV,) + s.shape, s.dtype) for s in later_shards],
        scratch_shapes=[pltpu.VMEM(w_in_r.shape, BF16), pltpu.VMEM(w_uq_r.shape, BF16),
                        pltpu.VMEM(w_k_exp.shape, BF16), pltpu.VMEM(w_v.shape, BF16),
                        pltpu.VMEM((POOL_HALO, POOL_WIDTH), F32),
                        pltpu.SemaphoreType.DMA((7 * n_later,)), pltpu.SemaphoreType.DMA((7 * n_later,)),
                        pltpu.SemaphoreType.DMA((n_later,)), pltpu.SemaphoreType.DMA((4,))],
        compiler_params=_params(1, 48),
    )(x, g_pre, b_gate, g_q, g_kv, positions, _rope_lane_frequencies(), w_pool_bf, pool_scale,
      w_in_r, w_uq_r, w_k_exp, w_v, *later_shards)


def _window_count(row0, n_rows):
    t = row0 + lax.broadcasted_iota(jnp.int32, (n_rows, POOL_GROUP), 0)
    return [jnp.minimum(t + 1, w).astype(F32) for w in POOL_WINDOWS]


def _col_to_row(col, n):
    return jnp.transpose(jnp.broadcast_to(col, (n, HEAD_PAD)))[0:1, :]


def _attn_fwd(q, k, v):
    heads, seq, _ = q.shape
    r = FWD_ROWS
    n = min(FWD_CHAINS, seq // r)
    block = r * n

    def body(q_ref, k_ref, v_ref, o_ref, lse_ref):
        qi = pl.program_id(1)
        q_tiles = [q_ref[c * r:(c + 1) * r, :] for c in range(n)]

        def tile(qt, j, m, acc, diagonal):
            start = pl.multiple_of(j * r, r)
            s = _mm_nt(qt, k_ref[pl.ds(start, r), :])
            if diagonal:
                row = lax.broadcasted_iota(jnp.int32, (r, r), 0)
                col = lax.broadcasted_iota(jnp.int32, (r, r), 1)
                s = jnp.where(col <= row, s, -jnp.inf)
            m_new = jnp.maximum(m, jnp.max(s, axis=1, keepdims=True))
            p = jnp.exp2((s - m_new).astype(BF16))
            acc = jnp.exp2(m - m_new) * acc + _mm(p, v_ref[pl.ds(start, r), :])
            return m_new, acc

        def all_chains(jj, carry):
            for u in range(n):
                carry = tuple(tile(q_tiles[c], n * jj + u, *carry[c], False) for c in range(n))
            return carry

        init = tuple((jnp.full((r, 1), -jnp.inf, F32), jnp.zeros((r, HEAD_PAD), F32)) for _ in range(n))
        state = list(lax.fori_loop(0, qi, all_chains, init))
        for d in range(n):
            for c in range(d, n):
                state[c] = tile(q_tiles[c], n * qi + d, *state[c], c == d)
        for c, (m, acc) in enumerate(state):
            l = acc[:, V_HEAD:V_HEAD + 1]
            o_ref[c * r:(c + 1) * r, :] = (acc / l).astype(BF16)
            row0 = c * r
            lse_ref[row0 // ATTN_TILE, :, row0 % ATTN_TILE:row0 % ATTN_TILE + r] = _col_to_row(m + jnp.log2(l), r)

    return pl.pallas_call(
        body, name="attn_fwd", grid=(heads, seq // block),
        in_specs=[pl.BlockSpec((None, block, HEAD_PAD), lambda h, i: (h, i, 0)),
                  pl.BlockSpec((None, seq, HEAD_PAD), lambda h, i: (h, 0, 0)),
                  pl.BlockSpec((None, seq, HEAD_PAD), lambda h, i: (h, 0, 0))],
        out_specs=[pl.BlockSpec((None, block, HEAD_PAD), lambda h, i: (h, i, 0)),
                   pl.BlockSpec((None, block // ATTN_TILE, 1, ATTN_TILE), lambda h, i: (h, i, 0, 0))],
        out_shape=[_sds((heads, seq, HEAD_PAD), BF16), _sds((heads, seq // ATTN_TILE, 1, ATTN_TILE), F32)],
        compiler_params=_params(2, 48),
    )(q, k, v)


def _peer_copies(src_refs, dst_refs, send_sems, recv_sems):
    x, y, c = _position()
    copies = []
    for w, (src, dst) in enumerate(zip(src_refs, dst_refs)):
        for r in range(1, N_DEV):
            px = 1 - x if r & 4 else x
            py = 1 - y if r & 2 else y
            pc = 1 - c if r & 1 else c
            copies.append(pltpu.make_async_remote_copy(
                src_ref=src.at[4 * px + 2 * py + pc], dst_ref=dst.at[r - 1],
                send_sem=send_sems.at[(N_DEV - 1) * w + r - 1], recv_sem=recv_sems.at[(N_DEV - 1) * w + r - 1],
                device_id=(px, py, pc), device_id_type=MESH))
    return copies


def _attn_bwd(q, k, k_t, v, do, lse, delta, early_grads):
    heads, seq, _ = q.shape
    t = ATTN_TILE
    nq = seq // t
    n = min(BWD_CHAINS, nq)
    n_w = len(early_grads)

    def body(q_ref, k_ref, kt_ref, v_ref, do_ref, lse_ref, delta_ref, *rest):
        grad_refs, rest = rest[:n_w], rest[n_w:]
        dq_ref, dk_ref, dv_ref = rest[:3]
        recv_refs, (send_sems, recv_sems) = rest[3:3 + n_w], rest[3 + n_w:]
        jp = pl.program_id(1)
        head = pl.program_id(0)

        @pl.when((head == 0) & (jp == 0))
        def _():
            for cp in _peer_copies(grad_refs, recv_refs, send_sems, recv_sems):
                cp.start()

        @pl.when(jp == 0)
        def _():
            dq_ref[...] = jnp.zeros_like(dq_ref)

        keys = [k_ref[c * t:(c + 1) * t, :] for c in range(n)]
        values = [v_ref[c * t:(c + 1) * t, :] for c in range(n)]

        def tile(c, i, dk, dv, diagonal):
            start = pl.multiple_of(i * t, t)
            qt = q_ref[pl.ds(start, t), :]
            dot = do_ref[pl.ds(start, t), :]
            p_t = jnp.exp2(_mm_nt(keys[c], qt) - lse_ref[i])
            if diagonal:
                key = lax.broadcasted_iota(jnp.int32, (t, t), 0)
                query = lax.broadcasted_iota(jnp.int32, (t, t), 1)
                p_t = jnp.where(key <= query, p_t, 0.0)
            dv = dv + _mm(p_t.astype(BF16), dot)
            ds_t = (p_t * (_mm_nt(values[c], dot) - delta_ref[i])).astype(BF16)
            dk = dk + _mm(ds_t, qt)
            return dk, dv, _mm(kt_ref[c], ds_t)

        def query_tile(i, state, first_rows):
            dq = None
            for c in range(n if first_rows is None else first_rows + 1):
                dk, dv, dq_c = tile(c, i, *state[c], diagonal=(c == first_rows))
                state[c] = (dk, dv)
                dq = dq_c if dq is None else dq + dq_c
            dq_ref[i] += dq

        def passes(width):
            def run(ip, carry):
                state = list(carry)
                for u in range(width):
                    query_tile(width * ip + u, state, None)
                return tuple(state)
            return run

        zero = jnp.zeros((t, HEAD_PAD), F32)
        state = [(zero, zero)] * n
        for offset in range(n):
            query_tile(n * jp + offset, state, offset)
        half = BWD_QUERIES // 2
        first = n * (jp + 1)
        odd = ((nq - first) // half) % 2 == 1
        state = lax.cond(odd, lambda s: passes(half)(first // half, s), lambda s: s, tuple(state))
        first_pass = (first + jnp.where(odd, half, 0)) // BWD_QUERIES
        state = lax.fori_loop(first_pass, nq // BWD_QUERIES, passes(BWD_QUERIES), state)
        for c, (dk, dv) in enumerate(state):
            dk_ref[c * t:(c + 1) * t, :] = (dk * (1.0 / LOG2E)).astype(BF16)
            dv_ref[c * t:(c + 1) * t, :] = dv.astype(BF16)

        @pl.when((head == heads - 1) & (jp == nq // n - 1))
        def _():
            copies = _peer_copies(grad_refs, recv_refs, send_sems, recv_sems)
            for cp in copies:
                cp.wait_recv()
            for cp in copies:
                cp.wait_send()

    whole = pl.BlockSpec((None, seq, HEAD_PAD), lambda h, j: (h, 0, 0))
    whole_t = pl.BlockSpec((None, nq, HEAD_PAD, t), lambda h, j: (h, 0, 0, 0))
    pair = pl.BlockSpec((None, n * t, HEAD_PAD), lambda h, j: (h, j, 0))
    pair_t = pl.BlockSpec((None, n, HEAD_PAD, t), lambda h, j: (h, j, 0, 0))
    stats = pl.BlockSpec((None, nq, 1, t), lambda h, j: (h, 0, 0, 0))
    return pl.pallas_call(
        body, name="attn_bwd", grid=(heads, nq // n),
        in_specs=[whole, pair, pair_t, pair, whole, stats, stats] + [ANY] * n_w,
        out_specs=[whole_t, pair, pair] + [ANY] * n_w,
        out_shape=[_sds((heads, nq, HEAD_PAD, t), F32), _sds((heads, seq, HEAD_PAD), BF16),
                   _sds((heads, seq, HEAD_PAD), BF16)]
                  + [_sds((N_DEV - 1,) + g.shape[1:], g.dtype) for g in early_grads],
        scratch_shapes=[pltpu.SemaphoreType.DMA(((N_DEV - 1) * n_w,)), pltpu.SemaphoreType.DMA(((N_DEV - 1) * n_w,))],
        compiler_params=_params(2, 56),
    )(q, k, k_t, v, do, lse, delta, *early_grads)


def _merge_fwd(attn, pooled, gates, x, g_post_mix, w_ba, w_bb, w_out, tm):
    seq = x.shape[0]

    def body(attn_ref, pooled_ref, gates_ref, x_ref, g_ref, wba_ref, wbb_ref, wout_ref,
             merged_ref, ba_ref, bb_ref, y_ref, h1_ref, attn_rows_ref):
        for rows in _row_chains(tm):
            attn = jnp.concatenate([attn_ref[h, rows, :] for h in range(N_HEADS)], axis=1)
            attn_rows_ref[rows, :] = attn
            ba = _mm(attn, wba_ref[...])
            bb = _mm(pooled_ref[rows, :], wbb_ref[...])
            ba_ref[rows, :] = ba.astype(BF16)
            bb_ref[rows, :] = bb.astype(BF16)
            merged = (gates_ref[rows, :D_MODEL].astype(F32) * ba
                      + gates_ref[rows, D_MODEL:].astype(F32) * bb).astype(BF16)
            merged_ref[rows, :] = merged
            y = _mm(merged, wout_ref[...])
            y_ref[rows, :] = y
            h1_ref[rows, :] = x_ref[rows, :] + y * _rms_r(y) * g_ref[...]

    return pl.pallas_call(
        body, name="merge_fwd", grid=(seq // tm,),
        in_specs=[pl.BlockSpec((N_HEADS, tm, HEAD_PAD), lambda i: (0, i, 0)), _row(tm, POOL_WIDTH),
                  _row(tm, 2 * D_MODEL), _row(tm, D_MODEL),
                  _fix((1, D_MODEL)), _fix(w_ba.shape), _fix(w_bb.shape), _fix(w_out.shape)],
        out_specs=[_row(tm, D_MODEL)] * 5 + [_row(tm, N_HEADS * HEAD_PAD)],
        out_shape=[_sds((seq, D_MODEL), BF16), _sds((seq, D_MODEL), BF16), _sds((seq, D_MODEL), BF16),
                   _sds((seq, D_MODEL), F32), _sds((seq, D_MODEL), F32), _sds((seq, N_HEADS * HEAD_PAD), BF16)],
        compiler_params=_params(1, 48),
    )(attn, pooled, gates, x, g_post_mix, w_ba, w_bb, w_out)


def _tail_fwd(h1, target, p, g_pre_mlp, g_post_mlp, g_ple, w_ff1, w_ff2, w_pe, w_pg, tm):
    seq = h1.shape[0]

    def body(h1_ref, tgt_ref, p_ref, gm_ref, gf_ref, gp_ref, w1_hbm, w2_hbm, wpe_hbm, wpg_hbm,
             m_ref, zr_ref, f_ref, h2b_ref, pb_ref, de_ref, dpre_ref, dh2_ref, loss_ref, dgple_ref,
             w1, w2, wpe, wpg, load_sems):
        _load_once(_column_blocks(w1_hbm, w1) + ((w2_hbm, w2), (wpe_hbm, wpe), (wpg_hbm, wpg)), load_sems)

        @pl.when(pl.program_id(0) == 0)
        def _():
            loss_ref[...] = jnp.zeros_like(loss_ref)
            dgple_ref[...] = jnp.zeros_like(dgple_ref)

        h1v = h1_ref[...]
        m = (h1v * _rms_r(h1v) * gm_ref[...]).astype(BF16)
        m_ref[...] = m
        zr = jnp.maximum(_mm(m, w1[...]), 0.0)
        zr_ref[...] = zr.astype(BF16)
        f = _mm((zr * zr).astype(BF16), w2[...])
        f_ref[...] = f
        h2 = h1v + f * _rms_r(f) * gf_ref[...]
        h2b = h2.astype(BF16)
        h2b_ref[...] = h2b
        pb = p_ref[...].astype(BF16)
        pb_ref[...] = pb
        e = _mm(pb, wpe[...])
        pg = _sigmoid(_mm(h2b, wpg[...]))
        t3 = pg * e
        r3 = _rms_r(t3)
        t3hat = t3 * r3
        diff = h2 + t3hat * gp_ref[...] - tgt_ref[...]
        loss_ref[...] += jnp.sum(diff * diff) * (0.5 / D_MODEL)
        dh3 = diff * (1.0 / D_MODEL)
        dgple_ref[...] += _colsum(dh3 * t3hat)
        dt3 = _rms_bwd(t3hat, r3, gp_ref[...], dh3)
        de_ref[...] = (dt3 * pg).astype(BF16)
        dpre = (dt3 * e * pg * (1.0 - pg)).astype(BF16)
        dpre_ref[...] = dpre
        dh2_ref[...] = dh3 + _mm_nt(dpre, wpg[...])

    return pl.pallas_call(
        body, name="tail_fwd", grid=(seq // tm,),
        in_specs=[_row(tm, D_MODEL), _row(tm, D_MODEL), _row(tm, PLE_DIM), _fix((1, D_MODEL)), _fix((1, D_MODEL)),
                  _fix((1, D_MODEL)), ANY, ANY, ANY, ANY],
        out_specs=[_row(tm, D_MODEL), _row(tm, D_FF), _row(tm, D_MODEL), _row(tm, D_MODEL), _row(tm, PLE_DIM),
                   _row(tm, D_MODEL), _row(tm, D_MODEL), _row(tm, D_MODEL), _fix((8, 128)), _fix((1, D_MODEL))],
        out_shape=[_sds((seq, D_MODEL), BF16), _sds((seq, D_FF), BF16), _sds((seq, D_MODEL), F32),
                   _sds((seq, D_MODEL), BF16), _sds((seq, PLE_DIM), BF16), _sds((seq, D_MODEL), BF16),
                   _sds((seq, D_MODEL), BF16), _sds((seq, D_MODEL), F32), _sds((8, 128), F32), _sds((1, D_MODEL), F32)],
        scratch_shapes=[pltpu.VMEM((w_ff1.shape[1], N_DEV * w_ff1.shape[2]), BF16), pltpu.VMEM(w_ff2.shape, BF16),
                        pltpu.VMEM(w_pe.shape, BF16), pltpu.VMEM(w_pg.shape, BF16),
                        pltpu.SemaphoreType.DMA((N_DEV + 3,))],
        compiler_params=_params(1, 56),
    )(h1, target, p, g_pre_mlp, g_post_mlp, g_ple, w_ff1, w_ff2, w_pe, w_pg)


def _mlp_bwd(h1, f, zr, dh2, g_pre_mlp, g_post_mlp, w_ff1, w_ff2, tm):
    seq = h1.shape[0]

    def body(h1_ref, f_ref, zr_ref, dh2_ref, gm_ref, gf_ref, w1_hbm, w2_hbm,
             df_ref, dz_ref, dh1_ref, dgm_ref, dgf_ref, w1, w2, load_sems):
        _load_once(_column_blocks(w1_hbm, w1) + ((w2_hbm, w2),), load_sems)

        @pl.when(pl.program_id(0) == 0)
        def _():
            dgm_ref[...] = jnp.zeros_like(dgm_ref)
            dgf_ref[...] = jnp.zeros_like(dgf_ref)

        dh2 = dh2_ref[...]
        fv = f_ref[...]
        rf = _rms_r(fv)
        fhat = fv * rf
        dgf_ref[...] += _colsum(dh2 * fhat)
        df = _rms_bwd(fhat, rf, gf_ref[...], dh2).astype(BF16)
        df_ref[...] = df
        dz = (_mm_nt(df, w2[...]) * (2.0 * zr_ref[...].astype(F32))).astype(BF16)
        dz_ref[...] = dz
        dm = _mm_nt(dz, w1[...])
        h1v = h1_ref[...]
        r1 = _rms_r(h1v)
        h1hat = h1v * r1
        dgm_ref[...] += _colsum(dm * h1hat)
        dh1_ref[...] = dh2 + _rms_bwd(h1hat, r1, gm_ref[...], dm)

    return pl.pallas_call(
        body, name="mlp_bwd", grid=(seq // tm,),
        in_specs=[_row(tm, D_MODEL), _row(tm, D_MODEL), _row(tm, D_FF), _row(tm, D_MODEL),
                  _fix((1, D_MODEL)), _fix((1, D_MODEL)), ANY, ANY],
        out_specs=[_row(tm, D_MODEL), _row(tm, D_FF), _row(tm, D_MODEL), _fix((1, D_MODEL)), _fix((1, D_MODEL))],
        out_shape=[_sds((seq, D_MODEL), BF16), _sds((seq, D_FF), BF16), _sds((seq, D_MODEL), F32),
                   _sds((1, D_MODEL), F32), _sds((1, D_MODEL), F32)],
        scratch_shapes=[pltpu.VMEM((w_ff1.shape[1], N_DEV * w_ff1.shape[2]), BF16), pltpu.VMEM(w_ff2.shape, BF16),
                        pltpu.SemaphoreType.DMA((N_DEV + 1,))],
        compiler_params=_params(1, 56),
    )(h1, f, zr, dh2, g_pre_mlp, g_post_mlp, w_ff1, w_ff2)


def _merge_bwd(dh1, y, gates, ba, bb, d_pool, o_heads, g_post_mix, pool_scale, w_out, w_ba, w_bb, w_pool_bf, tm):
    seq = dh1.shape[0]
    assert tm == ATTN_TILE

    def body(dh1_ref, y_ref, gates_ref, ba_ref, bb_ref, d_ref, o_ref, g_ref, ps_ref, wout_ref, wba_ref, wbb_ref, wp_ref,
             dy_ref, dba_ref, dbb_ref, dgpre_ref, dattn_ref, dd_ref, delta_ref, dg_ref, dbg_ref, dps_ref, dwp_ref):
        @pl.when(pl.program_id(0) == 0)
        def _():
            dg_ref[...] = jnp.zeros_like(dg_ref)
            dbg_ref[...] = jnp.zeros_like(dbg_ref)
            dps_ref[...] = jnp.zeros_like(dps_ref)
            dwp_ref[...] = jnp.zeros_like(dwp_ref)

        for rows in _row_chains(tm):
            dh1v = dh1_ref[rows, :]
            yv = y_ref[rows, :]
            r = _rms_r(yv)
            yhat = yv * r
            dg_ref[...] += _colsum(dh1v * yhat)
            dy = _rms_bwd(yhat, r, g_ref[...], dh1v).astype(BF16)
            dy_ref[rows, :] = dy
            dmerged = _mm_nt(dy, wout_ref[...])
            dbranch = []
            for half, branch_ref, dbranch_ref in ((0, ba_ref, dba_ref), (1, bb_ref, dbb_ref)):
                lanes = slice(D_MODEL * half, D_MODEL * (half + 1))
                gate = gates_ref[rows, lanes].astype(F32)
                dpre = dmerged * branch_ref[rows, :].astype(F32) * gate * (1.0 - gate)
                dbg_ref[:, lanes] += _colsum(dpre)
                dgpre_ref[rows, lanes] = dpre.astype(BF16)
                dbranch.append((dmerged * gate).astype(BF16))
                dbranch_ref[rows, :] = dbranch[-1]
            dattn = _mm_nt(dbranch[0], wba_ref[...]).astype(BF16)
            for h in range(N_HEADS):
                do_h = dattn[:, HEAD_PAD * h:HEAD_PAD * (h + 1)]
                dattn_ref[h, rows, :] = do_h
                row_term = jnp.sum(o_ref[h, rows, :].astype(F32) * do_h.astype(F32), axis=1, keepdims=True)
                delta_ref[h, :, rows] = _col_to_row(row_term, rows.stop - rows.start)
            dpooled = _mm_nt(dbranch[1], wbb_ref[...])
            for g in range(len(POOL_WINDOWS)):
                lanes = slice(POOL_GROUP * g, POOL_GROUP * (g + 1))
                dpl = dpooled[:, lanes]
                d_g = d_ref[rows, lanes]
                dps_ref[:, lanes] += _colsum(dpl * _mm(d_g, wp_ref[g]))
                dyp = (dpl * ps_ref[:, lanes]).astype(BF16)
                dwp_ref[g] += lax.dot_general(d_g, dyp, TN, preferred_element_type=F32)
                dd_ref[rows, lanes] = _mm_nt(dyp, wp_ref[g])

    heads = pl.BlockSpec((N_HEADS, tm, HEAD_PAD), lambda i: (0, i, 0))
    return pl.pallas_call(
        body, name="merge_bwd", grid=(seq // tm,),
        in_specs=[_row(tm, D_MODEL), _row(tm, D_MODEL), _row(tm, 2 * D_MODEL), _row(tm, D_MODEL), _row(tm, D_MODEL),
                  _row(tm, POOL_WIDTH), heads, _fix((1, D_MODEL)), _fix((1, POOL_WIDTH)),
                  _fix(w_out.shape), _fix(w_ba.shape), _fix(w_bb.shape), _fix(w_pool_bf.shape)],
        out_specs=[_row(tm, D_MODEL), _row(tm, D_MODEL), _row(tm, D_MODEL), _row(tm, 2 * D_MODEL),
                   heads, _row(tm, POOL_WIDTH), pl.BlockSpec((N_HEADS, None, 1, tm), lambda i: (0, i, 0, 0)),
                   _fix((1, D_MODEL)), _fix((1, 2 * D_MODEL)), _fix((1, POOL_WIDTH)), _fix(w_pool_bf.shape)],
        out_shape=[_sds((seq, D_MODEL), BF16), _sds((seq, D_MODEL), BF16), _sds((seq, D_MODEL), BF16),
                   _sds((seq, 2 * D_MODEL), BF16), _sds((N_HEADS, seq, HEAD_PAD), BF16),
                   _sds((seq, POOL_WIDTH), F32), _sds((N_HEADS, seq // tm, 1, tm), F32),
                   _sds((1, D_MODEL), F32), _sds((1, 2 * D_MODEL), F32),
                   _sds((1, POOL_WIDTH), F32), _sds(w_pool_bf.shape, F32)],
        compiler_params=_params(1, 48),
    )(dh1, y, gates, ba, bb, d_pool, o_heads, g_post_mix, pool_scale, w_out, w_ba, w_bb, w_pool_bf)


def _proj_bwd(dq, dk, dv, qd, kvd, x, dh1, dgpre, dd, cf, s1, s2, g_pre, g_q, g_kv,
              w_in_r, w_uq_r, w_k_exp, w_v, tm):
    seq = x.shape[0]
    n_steps = seq // tm

    def body(dq_ref, dk_ref, dv_ref, qd_ref, kvd_ref, x_ref, dh1_ref, dgpre_ref, dd_ref, next_ref,
             cf_ref, s1_ref, s2_ref, gpre_ref, gq_ref, gkv_ref, win_hbm, wuq_hbm, wk_hbm, wv_hbm,
             gx_ref, dproj_ref, dqb_ref, qn_ref, kvn_ref, dkvb_ref, dgpre_acc, dgq_acc, dgkv_acc,
             win, wuq, wk, wv, load_sems):
        step = pl.program_id(0)
        _load_once(((win_hbm, win), (wuq_hbm, wuq), (wk_hbm, wk), (wv_hbm, wv)), load_sems)

        @pl.when(pl.program_id(0) == 0)
        def _():
            dgpre_acc[...] = jnp.zeros_like(dgpre_acc)
            dgq_acc[...] = jnp.zeros_like(dgq_acc)
            dgkv_acc[...] = jnp.zeros_like(dgkv_acc)

        for rows in _row_chains(tm):
            n_rows = rows.stop - rows.start
            cfv, s1v, s2v = cf_ref[rows, :], s1_ref[rows, :], s2_ref[rows, :]
            ksum = jnp.zeros((n_rows, HEAD_PAD), F32)
            for h in range(N_HEADS):
                lanes = slice(HEAD_PAD * h, HEAD_PAD * (h + 1))
                dqh = jnp.transpose(dq_ref[h, :, rows])
                dqb_ref[rows, lanes] = (_rope_t(dqh, cfv, s1v, s2v) * ATTN_SCALE).astype(BF16)
                dkh = dk_ref[h, rows, :]
                dkvb_ref[rows, lanes] = dkh
                dkvb_ref[rows, slice(K_ALL + lanes.start, K_ALL + lanes.stop)] = dv_ref[h, rows, :]
                ksum = ksum + dkh.astype(F32)
            lane = lax.broadcasted_iota(jnp.int32, (n_rows, HEAD_PAD), 1)
            rope_lanes = (lane >= QK_NOPE) & (lane < QK_NOPE + QK_ROPE)
            dkr = _rope_t(jnp.where(rope_lanes, ksum, 0.0), cfv, s1v, s2v)

            qdv = qd_ref[rows, :]
            rq = _rms_r(qdv)
            qhat = qdv * rq
            qn_ref[rows, :] = (qhat * gq_ref[...]).astype(BF16)
            dqn = _mm_nt(dqb_ref[rows, :], wuq[...])
            dgq_acc[...] += _colsum(dqn * qhat)
            dproj_ref[rows, IN_Q0:IN_KV0] = _rms_bwd(qhat, rq, gq_ref[...], dqn).astype(BF16)

            kvdv = kvd_ref[rows, :]
            rkv = _rms_r(kvdv)
            kvhat = kvdv * rkv
            kvn_ref[rows, :] = (kvhat * gkv_ref[...]).astype(BF16)
            dkvn = _mm_nt(dkvb_ref[rows, :K_ALL], wk[...]) + _mm_nt(dkvb_ref[rows, K_ALL:], wv[...])
            dgkv_acc[...] += _colsum(dkvn * kvhat)
            dproj_ref[rows, IN_KV0:IN_POOL0] = _rms_bwd(kvhat, rkv, gkv_ref[...], dkvn).astype(BF16)

            dd_t = dd_ref[rows, :]
            if rows.stop < tm:
                after = dd_ref[rows.stop:rows.stop + POOL_HALO, :]
            else:
                after = jnp.where(step == n_steps - 1, 0.0, next_ref[...])
            ext = jnp.concatenate([dd_t, after], axis=0)
            ext_rows = n_rows + POOL_HALO
            counts = _window_count(step * tm + rows.start, ext_rows)
            for g, window in enumerate(POOL_WINDOWS):
                lanes = slice(POOL_GROUP * g, POOL_GROUP * (g + 1))
                level = ext[:, lanes] / counts[g]
                reach = 1
                while reach < window:
                    level = level + pltpu.roll(level, ext_rows - reach, 0)
                    reach *= 2
                dproj_ref[rows, IN_POOL0 + lanes.start:IN_POOL0 + lanes.stop] = (
                    level[:n_rows] - dd_t[:, lanes]).astype(BF16)
            dproj_ref[rows, IN_GATE0:IN_KR0] = dgpre_ref[rows, :]
            dproj_ref[rows, IN_KR0:IN_R] = dkr.astype(BF16)

            da = _mm_nt(dproj_ref[rows, :], win[...])
            xv = x_ref[rows, :]
            r0 = _rms_r(xv)
            xhat = xv * r0
            dgpre_acc[...] += _colsum(da * xhat)
            gx_ref[rows, :] = dh1_ref[rows, :] + _rms_bwd(xhat, r0, gpre_ref[...], da)

    per_tile = ATTN_TILE // tm
    heads = pl.BlockSpec((N_HEADS, tm, HEAD_PAD), lambda i: (0, i, 0))
    heads_t = pl.BlockSpec((N_HEADS, None, HEAD_PAD, tm), lambda i: (0, i // per_tile, 0, i % per_tile))
    return pl.pallas_call(
        body, name="proj_bwd", grid=(seq // tm,),
        in_specs=[heads_t, heads, heads, _row(tm, Q_LORA), _row(tm, KV_LORA), _row(tm, D_MODEL),
                  _row(tm, D_MODEL), _row(tm, 2 * D_MODEL), _row(tm, POOL_WIDTH),
                  pl.BlockSpec((POOL_HALO, POOL_WIDTH),
                               lambda i: (jnp.minimum((i + 1) * (tm // POOL_HALO), seq // POOL_HALO - 1), 0)),
                  _row(tm, HEAD_PAD), _row(tm, HEAD_PAD), _row(tm, HEAD_PAD),
                  _fix((1, D_MODEL)), _fix((1, Q_LORA)), _fix((1, KV_LORA)), ANY, ANY, ANY, ANY],
        out_specs=[_row(tm, D_MODEL), _row(tm, IN_R), _row(tm, N_HEADS * HEAD_PAD), _row(tm, Q_LORA), _row(tm, KV_LORA),
                   _row(tm, 2 * K_ALL),
                   _fix((1, D_MODEL)), _fix((1, Q_LORA)), _fix((1, KV_LORA))],
        out_shape=[_sds((seq, D_MODEL), F32), _sds((seq, IN_R), BF16), _sds((seq, N_HEADS * HEAD_PAD), BF16),
                   _sds((seq, Q_LORA), BF16), _sds((seq, KV_LORA), BF16), _sds((seq, 2 * K_ALL), BF16),
                   _sds((1, D_MODEL), F32), _sds((1, Q_LORA), F32), _sds((1, KV_LORA), F32)],
        scratch_shapes=[pltpu.VMEM(w_in_r.shape, BF16), pltpu.VMEM(w_uq_r.shape, BF16),
                        pltpu.VMEM(w_k_exp.shape, BF16), pltpu.VMEM(w_v.shape, BF16), pltpu.SemaphoreType.DMA((4,))],
        compiler_params=_params(1, 60),
    )(dq, dk, dv, qd, kvd, x, dh1, dgpre, dd, dd, cf, s1, s2, g_pre, g_q, g_kv, w_in_r, w_uq_r, w_k_exp, w_v)


def _grad_w(a, b, name, square_a=False, by_device=False, with_bf16=False):
    seq, k_dim = a.shape
    n_dim = b.shape[1]
    tk = min(k_dim, 1024)
    tn = n_dim // 2 if n_dim == IN_R else min(n_dim, 1024)
    ts = min(seq, 2048)
    shard = n_dim // N_DEV
    per_tile = tn // shard
    if by_device:
        out_spec = pl.BlockSpec((per_tile, tk, shard), lambda i, j, s: (j, i, 0))
        out_shape = _sds((N_DEV, k_dim, shard), F32)
    else:
        out_spec = pl.BlockSpec((tk, tn), lambda i, j, s: (i, j))
        out_shape = _sds((k_dim, n_dim), F32)

    n_seq_steps = seq // ts

    def body(a_ref, b_ref, o_ref, *narrow):
        @pl.when(pl.program_id(2) == 0)
        def _():
            o_ref[...] = jnp.zeros_like(o_ref)

        at = a_ref[...]
        if square_a:
            at = at * at
        part = lax.dot_general(at, b_ref[...], TN, preferred_element_type=F32)
        if by_device:
            for d in range(per_tile):
                o_ref[d] += part[:, d * shard:(d + 1) * shard]
        else:
            o_ref[...] += part
        if with_bf16:
            @pl.when(pl.program_id(2) == n_seq_steps - 1)
            def _():
                narrow[0][...] = o_ref[...].astype(BF16)

    return pl.pallas_call(
        body, name=name, grid=(k_dim // tk, n_dim // tn, n_seq_steps),
        in_specs=[pl.BlockSpec((ts, tk), lambda i, j, s: (s, i)), pl.BlockSpec((ts, tn), lambda i, j, s: (s, j))],
        out_specs=[out_spec, out_spec] if with_bf16 else out_spec,
        out_shape=[out_shape, _sds(out_shape.shape, BF16)] if with_bf16 else out_shape,
        compiler_params=_params(3, 48),
    )(a, b)


def _position():
    return lax.axis_index("x"), lax.axis_index("y"), lax.axis_index("c")


def _gather_copies(x_ref, slot, send_sems, recv_sems, local_sem, phases=("send", "forward", "finish"), sem_base=0):
    x, y, c = _position()
    me, sibling = (x, y, c), (x, y, 1 - c)
    chips = [(1 - x, y), (x, 1 - y), (1 - x, 1 - y)]

    def copy(k, block, to, src=None):
        return pltpu.make_async_remote_copy(
            src_ref=slot(*block) if src is None else src, dst_ref=slot(*block),
            send_sem=send_sems.at[sem_base + k], recv_sem=recv_sems.at[sem_base + k],
            device_id=to, device_id_type=MESH)

    mine = pltpu.make_async_copy(x_ref, slot(*me), local_sem)
    first = [copy(0, me, sibling, src=x_ref)]
    first += [copy(1 + j, me, (*chip, c), src=x_ref) for j, chip in enumerate(chips)]
    passed = [copy(4 + j, (*chip, c), sibling) for j, chip in enumerate(chips)]
    if "send" in phases:
        mine.start()
        for cp in first:
            cp.start()
    if "forward" in phases:
        for j, chip in enumerate(chips):
            copy(1 + j, (*chip, c), me).wait_recv()
            passed[j].start()
    if "finish" in phases:
        copy(0, sibling, me).wait_recv()
        for j, chip in enumerate(chips):
            copy(4 + j, (*chip, 1 - c), me).wait_recv()
        for cp in first + passed:
            cp.wait_send()
        mine.wait()


def _all_gather_hbm(block):
    def body(x_ref, out_ref, send_sems, recv_sems, local_sem):
        _gather_copies(x_ref, lambda px, py, pc: out_ref.at[4 * px + 2 * py + pc], send_sems, recv_sems, local_sem)

    return pl.pallas_call(
        body, name="gather_weights",
        in_specs=[ANY], out_specs=ANY,
        out_shape=_sds((N_DEV,) + block.shape, block.dtype),
        scratch_shapes=[pltpu.SemaphoreType.DMA((7,)), pltpu.SemaphoreType.DMA((7,)), pltpu.SemaphoreType.DMA],
    )(block)


def _replicated_update(grads, loss_block, ws, ms, vs):
    n_p = len(grads)
    sent = list(grads) + [loss_block]
    n_a = len(sent)

    def body(*refs):
        g_refs, refs = refs[:n_a], refs[n_a:]
        w_refs, m_refs, v_refs, refs = refs[:n_p], refs[n_p:2 * n_p], refs[2 * n_p:3 * n_p], refs[3 * n_p:]
        sum_refs, refs = refs[:n_a], refs[n_a:]
        d_refs, nm_refs, nv_refs, refs = refs[:n_p], refs[n_p:2 * n_p], refs[2 * n_p:3 * n_p], refs[3 * n_p:]
        bufs, (send_sems, recv_sems, local_sems) = refs[:n_a], refs[n_a:]
        x, y, c = _position()
        me = 4 * x + 2 * y + c
        local, remote = [], []
        for a in range(n_a):
            local.append(pltpu.make_async_copy(g_refs[a], bufs[a].at[me], local_sems.at[a]))
            for r in range(1, N_DEV):
                peer = (1 - x if r & 4 else x, 1 - y if r & 2 else y, 1 - c if r & 1 else c)
                remote.append(pltpu.make_async_remote_copy(
                    src_ref=g_refs[a], dst_ref=bufs[a].at[me],
                    send_sem=send_sems.at[(N_DEV - 1) * a + r - 1], recv_sem=recv_sems.at[(N_DEV - 1) * a + r - 1],
                    device_id=peer, device_id_type=MESH))
        for cp in local + remote:
            cp.start()
        for cp in remote:
            cp.wait_recv()
        for cp in remote:
            cp.wait_send()
        for cp in local:
            cp.wait()
        for a in range(n_a):
            acc = bufs[a][0]
            for d in range(1, N_DEV):
                acc = acc + bufs[a][d]
            if a == n_p:
                sum_refs[a][...] = acc
                continue
            delta, new_m, new_v = _adamw_math(acc, w_refs[a][...], m_refs[a][...], v_refs[a][...])
            sum_refs[a][...], d_refs[a][...], nm_refs[a][...], nv_refs[a][...] = acc, delta, new_m, new_v

    vmem = pl.BlockSpec(memory_space=pltpu.VMEM)
    like_w = [_sds(w.shape, F32) for w in ws]
    n_sem = (N_DEV - 1) * n_a
    outs = pl.pallas_call(
        body, name="replicated_update",
        in_specs=[vmem] * (n_a + 3 * n_p), out_specs=[vmem] * (n_a + 3 * n_p),
        out_shape=like_w + [_sds(loss_block.shape, F32)] + like_w * 3,
        scratch_shapes=[pltpu.VMEM((N_DEV,) + g.shape, F32) for g in sent]
                       + [pltpu.SemaphoreType.DMA((n_sem,)), pltpu.SemaphoreType.DMA((n_sem,)),
                          pltpu.SemaphoreType.DMA((n_a,))],
        compiler_params=pltpu.CompilerParams(vmem_limit_bytes=32 * MIB),
    )(*sent, *ws, *ms, *vs)
    return (outs[:n_p], outs[n_p], outs[n_a:n_a + n_p], outs[n_a + n_p:n_a + 2 * n_p], outs[n_a + 2 * n_p:])


def _exchange_pair(gs):
    n_w = len(gs)

    def body(*refs):
        g_refs, out_refs = refs[:n_w], refs[n_w:2 * n_w]
        send_sems, recv_sems = refs[2 * n_w:]
        x, y, c = _position()
        copies = []
        for w in range(n_w):
            for chip in range(4):
                cp = pltpu.make_async_remote_copy(
                    src_ref=g_refs[w].at[2 * chip + (1 - c)], dst_ref=out_refs[w].at[chip],
                    send_sem=send_sems.at[4 * w + chip], recv_sem=recv_sems.at[4 * w + chip],
                    device_id=(x, y, 1 - c), device_id_type=MESH)
                cp.start()
                copies.append(cp)
        for cp in copies:
            cp.wait_recv()
        for cp in copies:
            cp.wait_send()

    return pl.pallas_call(
        body, name="exchange_pair",
        in_specs=[ANY] * n_w, out_specs=[ANY] * n_w,
        out_shape=[_sds((4,) + g.shape[1:], g.dtype) for g in gs],
        scratch_shapes=[pltpu.SemaphoreType.DMA((4 * n_w,)), pltpu.SemaphoreType.DMA((4 * n_w,))],
    )(*gs)


def _exchange_chips(parts):
    n_w = len(parts)

    def body(*refs):
        p_refs, out_refs = refs[:n_w], refs[n_w:2 * n_w]
        send_sems, recv_sems = refs[2 * n_w:]
        x, y, c = _position()
        chips = [(1 - x, y), (x, 1 - y), (1 - x, 1 - y)]
        copies = []
        for w in range(n_w):
            for k, (px, py) in enumerate(chips):
                cp = pltpu.make_async_remote_copy(
                    src_ref=p_refs[w].at[2 * px + py], dst_ref=out_refs[w].at[k],
                    send_sem=send_sems.at[3 * w + k], recv_sem=recv_sems.at[3 * w + k],
                    device_id=(px, py, c), device_id_type=MESH)
                cp.start()
                copies.append(cp)
        for cp in copies:
            cp.wait_recv()
        for cp in copies:
            cp.wait_send()

    return pl.pallas_call(
        body, name="exchange_chips",
        in_specs=[ANY] * n_w, out_specs=[ANY] * n_w,
        out_shape=[_sds((3,) + p.shape[1:], p.dtype) for p in parts],
        scratch_shapes=[pltpu.SemaphoreType.DMA((3 * n_w,)), pltpu.SemaphoreType.DMA((3 * n_w,))],
    )(*parts)


def _row_tile(k):
    return 256 if k % 256 == 0 else 128


def _pair_sum(g, recv, place, name):
    _, k, n = g.shape
    tr = _row_tile(k)
    g4 = g.reshape(4, 2, k, n)

    def body(s_ref, g_ref, r_ref, o_ref):
        o_ref[...] = (g_ref[...] + r_ref[...]).astype(BF16)

    spec = pltpu.PrefetchScalarGridSpec(
        num_scalar_prefetch=1, grid=(4, k // tr),
        in_specs=[pl.BlockSpec((None, None, tr, n), lambda j, i, s: (j, s[2], i, 0)),
                  pl.BlockSpec((None, tr, n), lambda j, i, s: (j, i, 0))],
        out_specs=pl.BlockSpec((None, tr, n), lambda j, i, s: (j, i, 0)))
    return pl.pallas_call(
        body, name=name, grid_spec=spec, out_shape=_sds((4, k, n), BF16),
        compiler_params=_params(2, 32),
    )(place, g4, recv)


def _adamw_math(g, w, m, v):
    m = ADAM_B1 * m + (1.0 - ADAM_B1) * g
    v = ADAM_B2 * v + (1.0 - ADAM_B2) * (g * g)
    m_hat = m / (1.0 - ADAM_B1 ** ADAM_STEP)
    v_hat = v / (1.0 - ADAM_B2 ** ADAM_STEP)
    delta = -ADAM_LR * (m_hat / (jnp.sqrt(v_hat) + ADAM_EPS) + ADAM_WD * w)
    return delta, m, v


def _adamw_sharded(g, from_sibling, from_chips, place, w, m, v, name):
    _, k, n = g.shape
    tr = _row_tile(k)

    def body(s_ref, g_ref, sib_ref, r0_ref, r1_ref, r2_ref, w_ref, m_ref, v_ref, grad_ref, d_ref, nm_ref, nv_ref):
        grad = g_ref[...] + sib_ref[...]
        for r_ref in (r0_ref, r1_ref, r2_ref):
            grad = grad + r_ref[...].astype(F32)
        grad_ref[...] = grad
        d_ref[...], nm_ref[...], nv_ref[...] = _adamw_math(grad, w_ref[...], m_ref[...], v_ref[...])

    tile = pl.BlockSpec((None, tr, n), lambda i, s: (0, i, 0))

    def slot(j):
        return pl.BlockSpec((None, tr, n), lambda i, s: (j, i, 0))

    spec = pltpu.PrefetchScalarGridSpec(
        num_scalar_prefetch=1, grid=(k // tr,),
        in_specs=[pl.BlockSpec((None, tr, n), lambda i, s: (s[0], i, 0)),
                  pl.BlockSpec((None, tr, n), lambda i, s: (s[1], i, 0)),
                  slot(0), slot(1), slot(2), tile, tile, tile],
        out_specs=[tile] * 4)
    return pl.pallas_call(
        body, name=name, grid_spec=spec, out_shape=[_sds((1, k, n), F32)] * 4,
        compiler_params=_params(1, 48),
    )(place, g, from_sibling, from_chips, from_chips, from_chips, w, m, v)


def _adamw_direct(g, received, place, w, m, v, name):
    _, k, n = g.shape
    tr = _row_tile(k)

    def body(s_ref, g_ref, r_ref, w_ref, m_ref, v_ref, grad_ref, d_ref, nm_ref, nv_ref):
        grad = g_ref[...]
        for r in range(N_DEV - 1):
            grad = grad + r_ref[r].astype(F32)
        grad_ref[...] = grad
        d_ref[...], nm_ref[...], nv_ref[...] = _adamw_math(grad, w_ref[...], m_ref[...], v_ref[...])

    tile = pl.BlockSpec((None, tr, n), lambda i, s: (0, i, 0))
    spec = pltpu.PrefetchScalarGridSpec(
        num_scalar_prefetch=1, grid=(k // tr,),
        in_specs=[pl.BlockSpec((None, tr, n), lambda i, s: (s[0], i, 0)),
                  pl.BlockSpec((N_DEV - 1, tr, n), lambda i, s: (0, i, 0)), tile, tile, tile],
        out_specs=[tile] * 4)
    return pl.pallas_call(
        body, name=name, grid_spec=spec, out_shape=[_sds((1, k, n), F32)] * 4,
        compiler_params=_params(1, 48),
    )(place, g, received, w, m, v)


def _pack_rows(parts):
    parts = [a.reshape(-1, LANES) for a in parts]
    pad = (-sum(a.shape[0] for a in parts)) % PACK_ROW_TILE
    return jnp.concatenate(parts + [jnp.zeros((pad, LANES), parts[0].dtype)], axis=0)


def _full_from_gathered(gathered, entries, shard_shapes):
    out, off = {}, 0
    for (name, kind), (k, n) in zip(entries, shard_shapes):
        rows = k * n // LANES
        seg = gathered[:, off:off + rows].reshape(N_DEV, k, n)
        out[name] = jnp.transpose(seg, (1, 0, 2)).reshape(k, N_DEV * n) if kind == "col" else seg.reshape(N_DEV * k, n)
        off += rows
    return out


def _columns_by_device(a):
    k, n_all = a.shape
    return jnp.transpose(a.reshape(k, N_DEV, n_all // N_DEV), (1, 0, 2))


def _rows_by_device(a):
    k_all, n = a.shape
    return a.reshape(N_DEV, k_all // N_DEV, n)


def _rope_lane_frequencies():
    inv_freq = ROPE_THETA ** (-jnp.arange(0, QK_ROPE, 2, dtype=F32) / QK_ROPE)
    zeros = lambda n: jnp.zeros((n,), F32)
    return jnp.concatenate([zeros(QK_NOPE), inv_freq, inv_freq, zeros(HEAD_PAD - QK_NOPE - QK_ROPE)])[None, :]


def _rope_tables(pos_row, freq, tm):
    pos = jnp.transpose(jnp.broadcast_to(pos_row.astype(F32), (HEAD_PAD, tm)))
    ang = pos * freq
    cos, sin = jnp.cos(ang), jnp.sin(ang)
    lane = lax.broadcasted_iota(jnp.int32, (tm, HEAD_PAD), 1)
    first = (lane >= QK_NOPE) & (lane < QK_NOPE + QK_ROPE // 2)
    second = (lane >= QK_NOPE + QK_ROPE // 2) & (lane < QK_NOPE + QK_ROPE)
    cf = jnp.where(lane < QK_NOPE, 1.0, jnp.where(first | second, cos, 0.0))
    return cf, jnp.where(first, -sin, 0.0), jnp.where(second, sin, 0.0)


def _arrange_w_in(w):
    k = w.shape[0]
    zeros = lambda n: jnp.zeros((k, n), w.dtype)
    kr0 = Q_LORA + KV_LORA
    pool0 = kr0 + QK_ROPE
    return jnp.concatenate([w[:, :kr0], w[:, pool0:], zeros(QK_NOPE), w[:, kr0:pool0],
                            zeros(HEAD_PAD - QK_NOPE - QK_ROPE)], axis=1)


def _restore_w_in(d):
    kr = d[:, IN_KR0 + QK_NOPE:IN_KR0 + QK_NOPE + QK_ROPE]
    return jnp.concatenate([d[:, :IN_POOL0], kr, d[:, IN_POOL0:IN_KR0]], axis=1)


def _pad_heads(w, width):
    k = w.shape[0]
    w = w.reshape(k, N_HEADS, width)
    return jnp.pad(w, ((0, 0), (0, 0), (0, HEAD_PAD - width))).reshape(k, N_HEADS * HEAD_PAD)


def _unpad_heads(d, width):
    k = d.shape[0]
    return d.reshape(k, N_HEADS, HEAD_PAD)[:, :, :width]


def kernel(x, p, positions, g_pre_mix, w_in, b_gate, g_q, w_uq, g_kv, w_ukv, w_pool, pool_scale, w_branch_attn, w_branch_pool, w_out, g_post_mix, g_pre_mlp, w_ff1, w_ff2, g_post_mlp, w_ple_proj, w_ple_gate, g_ple, loss_target, m_g_pre_mix, m_w_in, m_b_gate, m_g_q, m_w_uq, m_g_kv, m_w_ukv, m_w_pool, m_pool_scale, m_w_branch_attn, m_w_branch_pool, m_w_out, m_g_post_mix, m_g_pre_mlp, m_w_ff1, m_w_ff2, m_g_post_mlp, m_w_ple_proj, m_w_ple_gate, m_g_ple, v_g_pre_mix, v_w_in, v_b_gate, v_g_q, v_w_uq, v_g_kv, v_w_ukv, v_w_pool, v_pool_scale, v_w_branch_attn, v_w_branch_pool, v_w_out, v_g_post_mix, v_g_pre_mlp, v_w_ff1, v_w_ff2, v_g_post_mlp, v_w_ple_proj, v_w_ple_gate, v_g_ple):
    given = dict(locals())
    weights = {n: given[n] for n in WEIGHT_ORDER}
    moments_m = {n: given["m_" + n] for n in WEIGHT_ORDER}
    moments_v = {n: given["v_" + n] for n in WEIGHT_ORDER}
    xs, ps, target = x[0], p[0, 0], loss_target[0]
    seq = xs.shape[0]
    tm = min(256, seq)
    tm_merge = min(512, seq)
    core = lax.axis_index("c")
    chip = 2 * lax.axis_index("x") + lax.axis_index("y")

    early, later = SHARDED[:N_EARLY], SHARDED[N_EARLY:]
    shapes_of = lambda entries: [weights[n].shape[1:] for n, _ in entries]
    pack_bf16 = lambda entries: _pack_rows([weights[n][0].astype(BF16) for n, _ in entries])
    full = _full_from_gathered(_all_gather_hbm(pack_bf16(early)), early, shapes_of(early))
    w_in_r = _arrange_w_in(full["w_in"])
    w_uq_r = _pad_heads(full["w_uq"], QK_NOPE + QK_ROPE)
    ukv = full["w_ukv"].reshape(KV_LORA, N_HEADS, QK_NOPE + V_HEAD)
    w_k_exp = _pad_heads(ukv[:, :, :QK_NOPE].reshape(KV_LORA, N_HEADS * QK_NOPE), QK_NOPE)
    w_v = _pad_heads(ukv[:, :, QK_NOPE:].reshape(KV_LORA, N_HEADS * V_HEAD), V_HEAD)
    w_pool_bf = w_pool[0].astype(BF16)

    packed_later = [e for e in later if e[0] != "w_ff1"]
    a_bf, qd, kvd, d_pool, pooled, gates, q, k, v, k_t, cf, s1, s2, gathered_later, gathered_ff1 = _proj_fwd(
        xs, g_pre_mix, b_gate, g_q, g_kv, positions, w_pool_bf, pool_scale, w_in_r, w_uq_r, w_k_exp, w_v,
        [pack_bf16(packed_later), w_ff1[0].astype(BF16)], tm_merge)
    full.update(_full_from_gathered(gathered_later, packed_later, shapes_of(packed_later)))
    full["w_ff1"] = gathered_ff1
    w_ba = jnp.pad(full["w_branch_attn"].reshape(N_HEADS, V_HEAD, D_MODEL),
                   ((0, 0), (0, HEAD_PAD - V_HEAD), (0, 0))).reshape(N_HEADS * HEAD_PAD, D_MODEL)
    o_heads, lse = _attn_fwd(q, k, v)
    merged, ba, bb, y, h1, attn_rows = _merge_fwd(o_heads, pooled, gates, xs, g_post_mix, w_ba,
                                                  full["w_branch_pool"], full["w_out"], tm_merge)
    (m_bf, zr, f, h2_bf, p_bf, de, dpre, dh2, loss_acc, dg_ple) = _tail_fwd(
        h1, target, ps, g_pre_mlp, g_post_mlp, g_ple, full["w_ff1"], full["w_ff2"], full["w_ple_proj"],
        full["w_ple_gate"], tm)

    by_device, payload = {}, {}

    def keep(name, pair, layout=lambda g: g):
        by_device[name], payload[name] = layout(pair[0]), layout(pair[1])

    keep("w_ple_proj", _grad_w(p_bf, de, "grad_w_ple_proj", by_device=True, with_bf16=True))
    keep("w_ple_gate", _grad_w(h2_bf, dpre, "grad_w_ple_gate", with_bf16=True), _rows_by_device)
    df, dz, dh1, dg_pre_mlp, dg_post_mlp = _mlp_bwd(h1, f, zr, dh2, g_pre_mlp, g_post_mlp, full["w_ff1"],
                                                    full["w_ff2"], tm)
    keep("w_ff1", _grad_w(m_bf, dz, "grad_w_ff1", by_device=True, with_bf16=True))
    keep("w_ff2", _grad_w(zr, df, "grad_w_ff2", square_a=True, with_bf16=True), _rows_by_device)
    (dy, dba, dbb, dgpre, do_heads, dd, delta, dg_post_mix, db_gate, dpool_scale, dw_pool) = _merge_bwd(
        dh1, y, gates, ba, bb, d_pool, o_heads, g_post_mix, pool_scale, full["w_out"], w_ba,
        full["w_branch_pool"], w_pool_bf, tm_merge)
    keep("w_branch_attn", _grad_w(attn_rows, dba, "grad_w_branch_attn", with_bf16=True),
          lambda g: _columns_by_device(g.reshape(N_HEADS, HEAD_PAD, D_MODEL)[:, :V_HEAD].reshape(-1, D_MODEL)))
    keep("w_branch_pool", _grad_w(pooled, dbb, "grad_w_branch_pool", by_device=True, with_bf16=True))
    keep("w_out", _grad_w(merged, dy, "grad_w_out", with_bf16=True), _rows_by_device)
    direct = [n for n, _ in SHARDED[N_EARLY:]]
    outs = _attn_bwd(q, k, k_t, v, do_heads, lse, delta, [payload[n] for n in direct])
    dq, dk, dv = outs[:3]
    received = dict(zip(direct, outs[3:]))
    (grad_x, dproj, dq_bf, qn_bf, kvn_bf, dkv_bf, dg_pre_mix, dg_q, dg_kv) = _proj_bwd(
        dq, dk, dv, qd, kvd, xs, dh1, dgpre, dd, cf, s1, s2, g_pre_mix, g_q, g_kv, w_in_r, w_uq_r, w_k_exp, w_v,
        tm_merge)
    d_w_kv = _grad_w(kvn_bf, dkv_bf, "grad_w_ukv")
    d_k_exp = _unpad_heads(d_w_kv[:, :K_ALL], QK_NOPE)
    d_w_v = _unpad_heads(d_w_kv[:, K_ALL:], V_HEAD)
    by_device["w_in"] = _columns_by_device(_restore_w_in(_grad_w(a_bf, dproj, "grad_w_in")))
    by_device["w_uq"] = _columns_by_device(
        _unpad_heads(_grad_w(qn_bf, dq_bf, "grad_w_uq"), QK_NOPE + QK_ROPE).reshape(Q_LORA, -1))
    by_device["w_ukv"] = _columns_by_device(jnp.concatenate([d_k_exp, d_w_v], axis=2).reshape(KV_LORA, -1))
    grads_small = {
        "g_pre_mix": dg_pre_mix, "b_gate": db_gate, "g_q": dg_q, "g_kv": dg_kv,
        "w_pool": dw_pool, "pool_scale": dpool_scale, "g_post_mix": dg_post_mix,
        "g_pre_mlp": dg_pre_mlp, "g_post_mlp": dg_post_mlp, "g_ple": dg_ple,
    }

    names = [n for n, _ in SHARDED]
    place = jnp.stack([2 * chip + core, chip, core]).astype(jnp.int32)
    sharded = {n: _adamw_direct(by_device[n], received[n], place, weights[n], moments_m[n], moments_v[n],
                                "adamw_" + n) for n in direct}
    last = [n for n, _ in SHARDED[:N_EARLY]]
    own = [by_device[n] for n in last]
    from_sibling = _exchange_pair(own)
    pair = [_pair_sum(g, r, place, "pair_sum_" + n) for n, g, r in zip(last, own, from_sibling)]
    from_chips = _exchange_chips(pair)
    sharded.update({n: _adamw_sharded(g, r, rc, place, weights[n], moments_m[n], moments_v[n], "adamw_" + n)
                    for n, g, r, rc in zip(last, own, from_sibling, from_chips)})

    flat = lambda a: a.reshape(a.shape[-3:]) if a.ndim > 3 else a
    g_sm, loss_sum, d_sm, m_sm, v_sm = _replicated_update(
        [flat(grads_small[n]) for n in REPLICATED], loss_acc, [flat(weights[n]) for n in REPLICATED],
        [flat(moments_m[n]) for n in REPLICATED], [flat(moments_v[n]) for n in REPLICATED])

    results = []
    for which, small in enumerate((g_sm, d_sm, m_sm, v_sm)):
        named = {n: sharded[n][which] for n in names}
        named.update({n: a.reshape(weights[n].shape) for n, a in zip(REPLICATED, small)})
        results.append([named[n] for n in WEIGHT_ORDER])

    return (loss_sum[0, 0], grad_x[None], *results[0], *results[1], *results[2], *results[3])
```

```python
import jax
import jax.numpy as jnp
from jax import lax
from jax.experimental import pallas as pl
from jax.experimental.pallas import tpu as pltpu

F32 = jnp.float32
BF16 = jnp.bfloat16

D_MODEL = 1024
PLE_DIM = 256
N_HEADS = 8
QK_NOPE = 64
QK_ROPE = 32
V_HEAD = 64
Q_LORA = 384
KV_LORA = 256
POOL_WINDOWS = (2, 4, 8, 16)
POOL_GROUP = 128
POOL_WIDTH = 512
D_FF = 4096
ROPE_THETA = 10000.0
EPS = 1e-6
HEAD_PAD = 128
K_ALL = N_HEADS * HEAD_PAD
ATTN_SCALE = (QK_NOPE + QK_ROPE) ** -0.5
LOG2E = 1.4426950408889634
Q_PRESCALE = ATTN_SCALE * LOG2E
ATTN_TILE = 512
FWD_ROWS = 512
FWD_CHAINS = 4
BWD_CHAINS = 2
BWD_QUERIES = 4

ADAM_LR = 0.001
ADAM_B1 = 0.9
ADAM_B2 = 0.999
ADAM_EPS = 1e-08
ADAM_WD = 0.01
ADAM_STEP = 10

N_DEV = 8
LANES = 1024
PACK_ROW_TILE = 480
POOL_HALO = 16
MIB = 2 ** 20

IN_Q0, IN_KV0, IN_POOL0, IN_GATE0, IN_KR0, IN_R = 0, 384, 640, 1152, 3200, 3328

SHARDED = (("w_in", "col"), ("w_uq", "col"), ("w_ukv", "col"), ("w_branch_attn", "col"),
           ("w_branch_pool", "col"), ("w_out", "row"), ("w_ff1", "col"), ("w_ff2", "row"),
           ("w_ple_proj", "col"), ("w_ple_gate", "row"))
N_EARLY = 3
REPLICATED = ("g_pre_mix", "b_gate", "g_q", "g_kv", "w_pool", "pool_scale", "g_post_mix",
              "g_pre_mlp", "g_post_mlp", "g_ple")
WEIGHT_ORDER = ("g_pre_mix", "w_in", "b_gate", "g_q", "w_uq", "g_kv", "w_ukv", "w_pool", "pool_scale",
                "w_branch_attn", "w_branch_pool", "w_out", "g_post_mix", "g_pre_mlp", "w_ff1", "w_ff2",
                "g_post_mlp", "w_ple_proj", "w_ple_gate", "g_ple")

NT = (((1,), (1,)), ((), ()))
TN = (((0,), (0,)), ((), ()))
MESH = pl.DeviceIdType.MESH
ANY = pl.BlockSpec(memory_space=pl.ANY)


def _params(n_axes, vmem_mib):
    return pltpu.CompilerParams(dimension_semantics=("arbitrary",) * n_axes, vmem_limit_bytes=vmem_mib * MIB)


def _row(tm, n):
    return pl.BlockSpec((tm, n), lambda i: (i, 0))


def _fix(shape):
    zeros = (0,) * len(shape)
    return pl.BlockSpec(shape, lambda i: zeros)


def _sds(shape, dtype):
    return jax.ShapeDtypeStruct(shape, dtype)


def _rms_r(v):
    return lax.rsqrt(jnp.mean(v * v, axis=-1, keepdims=True) + EPS)


def _rms_bwd(vhat, r, g, dy):
    gdy = dy * g
    return r * (gdy - vhat * jnp.mean(gdy * vhat, axis=-1, keepdims=True))


def _colsum(v):
    return jnp.sum(v, axis=0, keepdims=True)


def _sigmoid(v):
    return 1.0 / (1.0 + jnp.exp(-v))


def _mm(a, b):
    return jnp.dot(a, b, preferred_element_type=F32)


def _mm_nt(a, b):
    return lax.dot_general(a, b, NT, preferred_element_type=F32)


def _rope(c, cf, s1, s2):
    return c * cf + pltpu.roll(c, HEAD_PAD - 16, 1) * s1 + pltpu.roll(c, 16, 1) * s2


def _rope_t(c, cf, s1, s2):
    return c * cf + pltpu.roll(c * s1, 16, 1) + pltpu.roll(c * s2, HEAD_PAD - 16, 1)


def _row_chains(tm, rows=256):
    rows = min(rows, tm)
    return [slice(c * rows, (c + 1) * rows) for c in range(tm // rows)]


def _load_once(pairs, sems):
    @pl.when(pl.program_id(0) == 0)
    def _():
        copies = [pltpu.make_async_copy(src, dst, sems.at[i]) for i, (src, dst) in enumerate(pairs)]
        for cp in copies:
            cp.start()
        for cp in copies:
            cp.wait()


def _column_blocks(by_device_hbm, full_vmem):
    n = by_device_hbm.shape[2]
    return tuple((by_device_hbm.at[d], full_vmem.at[:, d * n:(d + 1) * n]) for d in range(N_DEV))


def _proj_fwd(x, g_pre, b_gate, g_q, g_kv, positions, w_pool_bf, pool_scale, w_in_r, w_uq_r, w_k_exp, w_v,
              later_shards, tm):
    seq = x.shape[0]
    n_steps = seq // tm
    forward_step = (3 * n_steps) // 4
    n_later = len(later_shards)

    def body(x_ref, gpre_ref, bg_ref, gq_ref, gkv_ref, pos_ref, freq_ref, wpool_ref, pscale_ref,
             win_hbm, wuq_hbm, wk_hbm, wv_hbm, *rest):
        later_refs, rest = rest[:n_later], rest[n_later:]
        (a_ref, qd_ref, kvd_ref, dpool_ref, pooled_ref, gates_ref, q_ref, k_ref, v_ref, kt_ref,
         cf_ref, s1_ref, s2_ref) = rest[:13]
        gathered_refs, rest = rest[13:13 + n_later], rest[13 + n_later:]
        win, wuq, wk, wv, halo_ref, send_sems, recv_sems, local_sems, load_sems = rest
        step = pl.program_id(0)

        def gather(phase):
            for a, (src, dst) in enumerate(zip(later_refs, gathered_refs)):
                _gather_copies(src, lambda px, py, pc, dst=dst: dst.at[4 * px + 2 * py + pc],
                               send_sems, recv_sems, local_sems.at[a], phases=(phase,), sem_base=7 * a)

        pl.when(step == 0)(lambda: gather("send"))
        pl.when(step == forward_step)(lambda: gather("forward"))
        _load_once(((win_hbm, win), (wuq_hbm, wuq), (wk_hbm, wk), (wv_hbm, wv)), load_sems)
        for rows in _row_chains(tm):
            n_rows = rows.stop - rows.start
            xv = x_ref[rows, :]
            a = (xv * _rms_r(xv) * gpre_ref[...]).astype(BF16)
            a_ref[rows, :] = a
            proj = _mm(a, win[...])
            qd = proj[:, IN_Q0:IN_KV0]
            kvd = proj[:, IN_KV0:IN_POOL0]
            qd_ref[rows, :] = qd
            kvd_ref[rows, :] = kvd
            gates_ref[rows, :] = _sigmoid(proj[:, IN_GATE0:IN_KR0] + bg_ref[...]).astype(BF16)

            u = proj[:, IN_POOL0:IN_GATE0]
            before = jnp.where(step == 0, 0.0, halo_ref[...]) if rows.start == 0 else tail
            tail = u[n_rows - POOL_HALO:, :]
            level = jnp.concatenate([before, u], axis=0)
            counts = _window_count(step * tm + rows.start, n_rows)
            shift = 1
            for g in range(len(POOL_WINDOWS)):
                level = level + pltpu.roll(level, shift, 0)
                shift *= 2
                lanes = slice(POOL_GROUP * g, POOL_GROUP * (g + 1))
                d = (level[POOL_HALO:, lanes] / counts[g] - u[:, lanes]).astype(BF16)
                dpool_ref[rows, lanes] = d
                pooled_ref[rows, lanes] = (_mm(d, wpool_ref[g]) * pscale_ref[:, lanes]).astype(BF16)
            if rows.stop == tm:
                halo_ref[...] = tail
            cfv, s1v, s2v = _rope_tables(pos_ref[:, rows], freq_ref[...], n_rows)
            cf_ref[rows, :], s1_ref[rows, :], s2_ref[rows, :] = cfv, s1v, s2v
            krr = _rope(proj[:, IN_KR0:IN_R], cfv, s1v, s2v)
            qn = (qd * _rms_r(qd) * gq_ref[...]).astype(BF16)
            qf = _mm(qn, wuq[...])
            kvn = (kvd * _rms_r(kvd) * gkv_ref[...]).astype(BF16)
            kf = _mm(kvn, wk[...])
            vf = _mm(kvn, wv[...])
            one_lane = (lax.broadcasted_iota(jnp.int32, (n_rows, HEAD_PAD), 1) == V_HEAD).astype(F32)
            for h in range(N_HEADS):
                lanes = slice(HEAD_PAD * h, HEAD_PAD * (h + 1))
                q_ref[h, rows, :] = (_rope(qf[:, lanes], cfv, s1v, s2v) * Q_PRESCALE).astype(BF16)
                kh = kf[:, lanes] + krr
                vh = vf[:, lanes] + one_lane
                k_ref[h, rows, :] = kh.astype(BF16)
                v_ref[h, rows, :] = vh.astype(BF16)
                kt_ref[h, :, rows] = jnp.transpose(kh).astype(BF16)
        pl.when(step == n_steps - 1)(lambda: gather("finish"))

    per_tile = ATTN_TILE // tm
    heads = pl.BlockSpec((N_HEADS, tm, HEAD_PAD), lambda i: (0, i, 0))
    heads_t = pl.BlockSpec((N_HEADS, None, HEAD_PAD, tm), lambda i: (0, i // per_tile, 0, i % per_tile))
    heads_t_shape = _sds((N_HEADS, seq // ATTN_TILE, HEAD_PAD, ATTN_TILE), BF16)
    return pl.pallas_call(
        body, name="proj_fwd", grid=(seq // tm,),
        in_specs=[_row(tm, D_MODEL), _fix((1, D_MODEL)), _fix((1, 2 * D_MODEL)), _fix((1, Q_LORA)), _fix((1, KV_LORA)),
                  pl.BlockSpec((1, tm), lambda i: (0, i)), _fix((1, HEAD_PAD)), _fix(w_pool_bf.shape), _fix((1, POOL_WIDTH)),
                  ANY, ANY, ANY, ANY] + [ANY] * n_later,
        out_specs=[_row(tm, D_MODEL), _row(tm, Q_LORA), _row(tm, KV_LORA), _row(tm, POOL_WIDTH), _row(tm, POOL_WIDTH),
                   _row(tm, 2 * D_MODEL), heads, heads, heads, heads_t,
                   _row(tm, HEAD_PAD), _row(tm, HEAD_PAD), _row(tm, HEAD_PAD)] + [ANY] * n_later,
        out_shape=[_sds((seq, D_MODEL), BF16), _sds((seq, Q_LORA), F32), _sds((seq, KV_LORA), F32),
                   _sds((seq, POOL_WIDTH), BF16), _sds((seq, POOL_WIDTH), BF16), _sds((seq, 2 * D_MODEL), BF16),
                   _sds((N_HEADS, seq, HEAD_PAD), BF16), _sds((N_HEADS, seq, HEAD_PAD), BF16),
                   _sds((N_HEADS, seq, HEAD_PAD), BF16), heads_t_shape,
                   _sds((seq, HEAD_PAD), F32), _sds((seq, HEAD_PAD), F32), _sds((seq, HEAD_PAD), F32)]
                  + [_sds((N_DEV,) + s.shape, s.dtype) for s in later_shards],
        scratch_shapes=[pltpu.VMEM(w_in_r.shape, BF16), pltpu.VMEM(w_uq_r.shape, BF16),
                        pltpu.VMEM(w_k_exp.shape, BF16), pltpu.VMEM(w_v.shape, BF16),
                        pltpu.VMEM((POOL_HALO, POOL_WIDTH), F32),
                        pltpu.SemaphoreType.DMA((7 * n_later,)), pltpu.SemaphoreType.DMA((7 * n_later,)),
                        pltpu.SemaphoreType.DMA((n_later,)), pltpu.SemaphoreType.DMA((4,))],
        compiler_params=_params(1, 48),
    )(x, g_pre, b_gate, g_q, g_kv, positions, _rope_lane_frequencies(), w_pool_bf, pool_scale,
      w_in_r, w_uq_r, w_k_exp, w_v, *later_shards)


def _window_count(row0, n_rows):
    t = row0 + lax.broadcasted_iota(jnp.int32, (n_rows, POOL_GROUP), 0)
    return [jnp.minimum(t + 1, w).astype(F32) for w in POOL_WINDOWS]


def _col_to_row(col, n):
    return jnp.transpose(jnp.broadcast_to(col, (n, HEAD_PAD)))[0:1, :]


def _attn_fwd(q, k, v):
    heads, seq, _ = q.shape
    r = FWD_ROWS
    n = min(FWD_CHAINS, seq // r)
    block = r * n

    def body(q_ref, k_ref, v_ref, o_ref, lse_ref):
        qi = pl.program_id(1)
        q_tiles = [q_ref[c * r:(c + 1) * r, :] for c in range(n)]

        def tile(qt, j, m, acc, diagonal):
            start = pl.multiple_of(j * r, r)
            s = _mm_nt(qt, k_ref[pl.ds(start, r), :])
            if diagonal:
                row = lax.broadcasted_iota(jnp.int32, (r, r), 0)
                col = lax.broadcasted_iota(jnp.int32, (r, r), 1)
                s = jnp.where(col <= row, s, -jnp.inf)
            m_new = jnp.maximum(m, jnp.max(s, axis=1, keepdims=True))
            p = jnp.exp2((s - m_new).astype(BF16))
            acc = jnp.exp2(m - m_new) * acc + _mm(p, v_ref[pl.ds(start, r), :])
            return m_new, acc

        def all_chains(jj, carry):
            for u in range(n):
                carry = tuple(tile(q_tiles[c], n * jj + u, *carry[c], False) for c in range(n))
            return carry

        init = tuple((jnp.full((r, 1), -jnp.inf, F32), jnp.zeros((r, HEAD_PAD), F32)) for _ in range(n))
        state = list(lax.fori_loop(0, qi, all_chains, init))
        for d in range(n):
            for c in range(d, n):
                state[c] = tile(q_tiles[c], n * qi + d, *state[c], c == d)
        for c, (m, acc) in enumerate(state):
            l = acc[:, V_HEAD:V_HEAD + 1]
            o_ref[c * r:(c + 1) * r, :] = (acc / l).astype(BF16)
            row0 = c * r
            lse_ref[row0 // ATTN_TILE, :, row0 % ATTN_TILE:row0 % ATTN_TILE + r] = _col_to_row(m + jnp.log2(l), r)

    return pl.pallas_call(
        body, name="attn_fwd", grid=(heads, seq // block),
        in_specs=[pl.BlockSpec((None, block, HEAD_PAD), lambda h, i: (h, i, 0)),
                  pl.BlockSpec((None, seq, HEAD_PAD), lambda h, i: (h, 0, 0)),
                  pl.BlockSpec((None, seq, HEAD_PAD), lambda h, i: (h, 0, 0))],
        out_specs=[pl.BlockSpec((None, block, HEAD_PAD), lambda h, i: (h, i, 0)),
                   pl.BlockSpec((None, block // ATTN_TILE, 1, ATTN_TILE), lambda h, i: (h, i, 0, 0))],
        out_shape=[_sds((heads, seq, HEAD_PAD), BF16), _sds((heads, seq // ATTN_TILE, 1, ATTN_TILE), F32)],
        compiler_params=_params(2, 48),
    )(q, k, v)


def _peer_copies(src_refs, dst_refs, send_sems, recv_sems):
    x, y, c = _position()
    copies = []
    for w, (src, dst) in enumerate(zip(src_refs, dst_refs)):
        for r in range(1, N_DEV):
            px = 1 - x if r & 4 else x
            py = 1 - y if r & 2 else y
            pc = 1 - c if r & 1 else c
            copies.append(pltpu.make_async_remote_copy(
                src_ref=src.at[4 * px + 2 * py + pc], dst_ref=dst.at[r - 1],
                send_sem=send_sems.at[(N_DEV - 1) * w + r - 1], recv_sem=recv_sems.at[(N_DEV - 1) * w + r - 1],
                device_id=(px, py, pc), device_id_type=MESH))
    return copies


def _attn_bwd(q, k, k_t, v, do, lse, delta, early_grads):
    heads, seq, _ = q.shape
    t = ATTN_TILE
    nq = seq // t
    n = min(BWD_CHAINS, nq)
    n_w = len(early_grads)

    def body(q_ref, k_ref, kt_ref, v_ref, do_ref, lse_ref, delta_ref, *rest):
        grad_refs, rest = rest[:n_w], rest[n_w:]
        dq_ref, dk_ref, dv_ref = rest[:3]
        recv_refs, (send_sems, recv_sems) = rest[3:3 + n_w], rest[3 + n_w:]
        jp = pl.program_id(1)
        head = pl.program_id(0)

        @pl.when((head == 0) & (jp == 0))
        def _():
            for cp in _peer_copies(grad_refs, recv_refs, send_sems, recv_sems):
                cp.start()

        @pl.when(jp == 0)
        def _():
            dq_ref[...] = jnp.zeros_like(dq_ref)

        keys = [k_ref[c * t:(c + 1) * t, :] for c in range(n)]
        values = [v_ref[c * t:(c + 1) * t, :] for c in range(n)]

        def tile(c, i, dk, dv, diagonal):
            start = pl.multiple_of(i * t, t)
            qt = q_ref[pl.ds(start, t), :]
            dot = do_ref[pl.ds(start, t), :]
            p_t = jnp.exp2(_mm_nt(keys[c], qt) - lse_ref[i])
            if diagonal:
                key = lax.broadcasted_iota(jnp.int32, (t, t), 0)
                query = lax.broadcasted_iota(jnp.int32, (t, t), 1)
                p_t = jnp.where(key <= query, p_t, 0.0)
            dv = dv + _mm(p_t.astype(BF16), dot)
            ds_t = (p_t * (_mm_nt(values[c], dot) - delta_ref[i])).astype(BF16)
            dk = dk + _mm(ds_t, qt)
            return dk, dv, _mm(kt_ref[c], ds_t)

        def query_tile(i, state, first_rows):
            dq = None
            for c in range(n if first_rows is None else first_rows + 1):
                dk, dv, dq_c = tile(c, i, *state[c], diagonal=(c == first_rows))
                state[c] = (dk, dv)
                dq = dq_c if dq is None else dq + dq_c
            dq_ref[i] += dq

        def passes(width):
            def run(ip, carry):
                state = list(carry)
                for u in range(width):
                    query_tile(width * ip + u, state, None)
                return tuple(state)
            return run

        zero = jnp.zeros((t, HEAD_PAD), F32)
        state = [(zero, zero)] * n
        for offset in range(n):
            query_tile(n * jp + offset, state, offset)
        half = BWD_QUERIES // 2
        first = n * (jp + 1)
        odd = ((nq - first) // half) % 2 == 1
        state = lax.cond(odd, lambda s: passes(half)(first // half, s), lambda s: s, tuple(state))
        first_pass = (first + jnp.where(odd, half, 0)) // BWD_QUERIES
        state = lax.fori_loop(first_pass, nq // BWD_QUERIES, passes(BWD_QUERIES), state)
        for c, (dk, dv) in enumerate(state):
            dk_ref[c * t:(c + 1) * t, :] = (dk * (1.0 / LOG2E)).astype(BF16)
            dv_ref[c * t:(c + 1) * t, :] = dv.astype(BF16)

        @pl.when((head == heads - 1) & (jp == nq // n - 1))
        def _():
            copies = _peer_copies(grad_refs, recv_refs, send_sems, recv_sems)
            for cp in copies:
                cp.wait_recv()
            for cp in copies:
                cp.wait_send()

    whole = pl.BlockSpec((None, seq, HEAD_PAD), lambda h, j: (h, 0, 0))
    whole_t = pl.BlockSpec((None, nq, HEAD_PAD, t), lambda h, j: (h, 0, 0, 0))
    pair = pl.BlockSpec((None, n * t, HEAD_PAD), lambda h, j: (h, j, 0))
    pair_t = pl.BlockSpec((None, n, HEAD_PAD, t), lambda h, j: (h, j, 0, 0))
    stats = pl.BlockSpec((None, nq, 1, t), lambda h, j: (h, 0, 0, 0))
    return pl.pallas_call(
        body, name="attn_bwd", grid=(heads, nq // n),
        in_specs=[whole, pair, pair_t, pair, whole, stats, stats] + [ANY] * n_w,
        out_specs=[whole_t, pair, pair] + [ANY] * n_w,
        out_shape=[_sds((heads, nq, HEAD_PAD, t), F32), _sds((heads, seq, HEAD_PAD), BF16),
                   _sds((heads, seq, HEAD_PAD), BF16)]
                  + [_sds((N_DEV - 1,) + g.shape[1:], g.dtype) for g in early_grads],
        scratch_shapes=[pltpu.SemaphoreType.DMA(((N_DEV - 1) * n_w,)), pltpu.SemaphoreType.DMA(((N_DEV - 1) * n_w,))],
        compiler_params=_params(2, 56),
    )(q, k, k_t, v, do, lse, delta, *early_grads)


def _merge_fwd(attn, pooled, gates, x, g_post_mix, w_ba, w_bb, w_out, tm):
    seq = x.shape[0]

    def body(attn_ref, pooled_ref, gates_ref, x_ref, g_ref, wba_ref, wbb_ref, wout_ref,
             merged_ref, ba_ref, bb_ref, y_ref, h1_ref):
        for rows in _row_chains(tm):
            attn = jnp.concatenate([attn_ref[h, rows, :] for h in range(N_HEADS)], axis=1)
            ba = _mm(attn, wba_ref[...])
            bb = _mm(pooled_ref[rows, :], wbb_ref[...])
            ba_ref[rows, :] = ba.astype(BF16)
            bb_ref[rows, :] = bb.astype(BF16)
            merged = (gates_ref[rows, :D_MODEL].astype(F32) * ba
                      + gates_ref[rows, D_MODEL:].astype(F32) * bb).astype(BF16)
            merged_ref[rows, :] = merged
            y = _mm(merged, wout_ref[...])
            y_ref[rows, :] = y
            h1_ref[rows, :] = x_ref[rows, :] + y * _rms_r(y) * g_ref[...]

    return pl.pallas_call(
        body, name="merge_fwd", grid=(seq // tm,),
        in_specs=[pl.BlockSpec((N_HEADS, tm, HEAD_PAD), lambda i: (0, i, 0)), _row(tm, POOL_WIDTH),
                  _row(tm, 2 * D_MODEL), _row(tm, D_MODEL),
                  _fix((1, D_MODEL)), _fix(w_ba.shape), _fix(w_bb.shape), _fix(w_out.shape)],
        out_specs=[_row(tm, D_MODEL)] * 5,
        out_shape=[_sds((seq, D_MODEL), BF16), _sds((seq, D_MODEL), BF16), _sds((seq, D_MODEL), BF16),
                   _sds((seq, D_MODEL), F32), _sds((seq, D_MODEL), F32)],
        compiler_params=_params(1, 48),
    )(attn, pooled, gates, x, g_post_mix, w_ba, w_bb, w_out)


def _tail_fwd(h1, target, p, g_pre_mlp, g_post_mlp, g_ple, w_ff1, w_ff2, w_pe, w_pg, tm):
    seq = h1.shape[0]

    def body(h1_ref, tgt_ref, p_ref, gm_ref, gf_ref, gp_ref, w1_hbm, w2_hbm, wpe_hbm, wpg_hbm,
             m_ref, zr_ref, f_ref, h2b_ref, pb_ref, de_ref, dpre_ref, dh2_ref, loss_ref, dgple_ref,
             w1, w2, wpe, wpg, load_sems):
        _load_once(_column_blocks(w1_hbm, w1) + ((w2_hbm, w2), (wpe_hbm, wpe), (wpg_hbm, wpg)), load_sems)

        @pl.when(pl.program_id(0) == 0)
        def _():
            loss_ref[...] = jnp.zeros_like(loss_ref)
            dgple_ref[...] = jnp.zeros_like(dgple_ref)

        h1v = h1_ref[...]
        m = (h1v * _rms_r(h1v) * gm_ref[...]).astype(BF16)
        m_ref[...] = m
        zr = jnp.maximum(_mm(m, w1[...]), 0.0)
        zr_ref[...] = zr.astype(BF16)
        f = _mm((zr * zr).astype(BF16), w2[...])
        f_ref[...] = f
        h2 = h1v + f * _rms_r(f) * gf_ref[...]
        h2b = h2.astype(BF16)
        h2b_ref[...] = h2b
        pb = p_ref[...].astype(BF16)
        pb_ref[...] = pb
        e = _mm(pb, wpe[...])
        pg = _sigmoid(_mm(h2b, wpg[...]))
        t3 = pg * e
        r3 = _rms_r(t3)
        t3hat = t3 * r3
        diff = h2 + t3hat * gp_ref[...] - tgt_ref[...]
        loss_ref[...] += jnp.sum(diff * diff) * (0.5 / D_MODEL)
        dh3 = diff * (1.0 / D_MODEL)
        dgple_ref[...] += _colsum(dh3 * t3hat)
        dt3 = _rms_bwd(t3hat, r3, gp_ref[...], dh3)
        de_ref[...] = (dt3 * pg).astype(BF16)
        dpre = (dt3 * e * pg * (1.0 - pg)).astype(BF16)
        dpre_ref[...] = dpre
        dh2_ref[...] = (dh3 + _mm_nt(dpre, wpg[...])).astype(BF16)

    return pl.pallas_call(
        body, name="tail_fwd", grid=(seq // tm,),
        in_specs=[_row(tm, D_MODEL), _row(tm, D_MODEL), _row(tm, PLE_DIM), _fix((1, D_MODEL)), _fix((1, D_MODEL)),
                  _fix((1, D_MODEL)), ANY, ANY, ANY, ANY],
        out_specs=[_row(tm, D_MODEL), _row(tm, D_FF), _row(tm, D_MODEL), _row(tm, D_MODEL), _row(tm, PLE_DIM),
                   _row(tm, D_MODEL), _row(tm, D_MODEL), _row(tm, D_MODEL), _fix((8, 128)), _fix((1, D_MODEL))],
        out_shape=[_sds((seq, D_MODEL), BF16), _sds((seq, D_FF), BF16), _sds((seq, D_MODEL), F32),
                   _sds((seq, D_MODEL), BF16), _sds((seq, PLE_DIM), BF16), _sds((seq, D_MODEL), BF16),
                   _sds((seq, D_MODEL), BF16), _sds((seq, D_MODEL), BF16), _sds((8, 128), F32), _sds((1, D_MODEL), F32)],
        scratch_shapes=[pltpu.VMEM((w_ff1.shape[1], N_DEV * w_ff1.shape[2]), BF16), pltpu.VMEM(w_ff2.shape, BF16),
                        pltpu.VMEM(w_pe.shape, BF16), pltpu.VMEM(w_pg.shape, BF16),
                        pltpu.SemaphoreType.DMA((N_DEV + 3,))],
        compiler_params=_params(1, 56),
    )(h1, target, p, g_pre_mlp, g_post_mlp, g_ple, w_ff1, w_ff2, w_pe, w_pg)


def _mlp_bwd(h1, f, zr, dh2, g_pre_mlp, g_post_mlp, w_ff1, w_ff2, tm):
    seq = h1.shape[0]

    def body(h1_ref, f_ref, zr_ref, dh2_ref, gm_ref, gf_ref, w1_hbm, w2_hbm,
             df_ref, dz_ref, dh1_ref, dgm_ref, dgf_ref, w1, w2, load_sems):
        _load_once(_column_blocks(w1_hbm, w1) + ((w2_hbm, w2),), load_sems)

        @pl.when(pl.program_id(0) == 0)
        def _():
            dgm_ref[...] = jnp.zeros_like(dgm_ref)
            dgf_ref[...] = jnp.zeros_like(dgf_ref)

        dh2 = dh2_ref[...].astype(F32)
        fv = f_ref[...]
        rf = _rms_r(fv)
        fhat = fv * rf
        dgf_ref[...] += _colsum(dh2 * fhat)
        df = _rms_bwd(fhat, rf, gf_ref[...], dh2).astype(BF16)
        df_ref[...] = df
        dz = (_mm_nt(df, w2[...]) * (2.0 * zr_ref[...].astype(F32))).astype(BF16)
        dz_ref[...] = dz
        dm = _mm_nt(dz, w1[...])
        h1v = h1_ref[...]
        r1 = _rms_r(h1v)
        h1hat = h1v * r1
        dgm_ref[...] += _colsum(dm * h1hat)
        dh1_ref[...] = (dh2 + _rms_bwd(h1hat, r1, gm_ref[...], dm)).astype(BF16)

    return pl.pallas_call(
        body, name="mlp_bwd", grid=(seq // tm,),
        in_specs=[_row(tm, D_MODEL), _row(tm, D_MODEL), _row(tm, D_FF), _row(tm, D_MODEL),
                  _fix((1, D_MODEL)), _fix((1, D_MODEL)), ANY, ANY],
        out_specs=[_row(tm, D_MODEL), _row(tm, D_FF), _row(tm, D_MODEL), _fix((1, D_MODEL)), _fix((1, D_MODEL))],
        out_shape=[_sds((seq, D_MODEL), BF16), _sds((seq, D_FF), BF16), _sds((seq, D_MODEL), BF16),
                   _sds((1, D_MODEL), F32), _sds((1, D_MODEL), F32)],
        scratch_shapes=[pltpu.VMEM((w_ff1.shape[1], N_DEV * w_ff1.shape[2]), BF16), pltpu.VMEM(w_ff2.shape, BF16),
                        pltpu.SemaphoreType.DMA((N_DEV + 1,))],
        compiler_params=_params(1, 56),
    )(h1, f, zr, dh2, g_pre_mlp, g_post_mlp, w_ff1, w_ff2)


def _merge_bwd(dh1, y, gates, ba, bb, d_pool, o_heads, g_post_mix, pool_scale, w_out, w_ba, w_bb, w_pool_bf, tm):
    seq = dh1.shape[0]
    assert tm == ATTN_TILE

    def body(dh1_ref, y_ref, gates_ref, ba_ref, bb_ref, d_ref, o_ref, g_ref, ps_ref, wout_ref, wba_ref, wbb_ref, wp_ref,
             dy_ref, dba_ref, dbb_ref, dgpre_ref, dattn_ref, dd_ref, delta_ref, dg_ref, dbg_ref, dps_ref, dwp_ref):
        @pl.when(pl.program_id(0) == 0)
        def _():
            dg_ref[...] = jnp.zeros_like(dg_ref)
            dbg_ref[...] = jnp.zeros_like(dbg_ref)
            dps_ref[...] = jnp.zeros_like(dps_ref)
            dwp_ref[...] = jnp.zeros_like(dwp_ref)

        for rows in _row_chains(tm):
            dh1v = dh1_ref[rows, :].astype(F32)
            yv = y_ref[rows, :]
            r = _rms_r(yv)
            yhat = yv * r
            dg_ref[...] += _colsum(dh1v * yhat)
            dy = _rms_bwd(yhat, r, g_ref[...], dh1v).astype(BF16)
            dy_ref[rows, :] = dy
            dmerged = _mm_nt(dy, wout_ref[...])
            dbranch = []
            for half, branch_ref, dbranch_ref in ((0, ba_ref, dba_ref), (1, bb_ref, dbb_ref)):
                lanes = slice(D_MODEL * half, D_MODEL * (half + 1))
                gate = gates_ref[rows, lanes].astype(F32)
                dpre = dmerged * branch_ref[rows, :].astype(F32) * gate * (1.0 - gate)
                dbg_ref[:, lanes] += _colsum(dpre)
                dgpre_ref[rows, lanes] = dpre.astype(BF16)
                dbranch.append((dmerged * gate).astype(BF16))
                dbranch_ref[rows, :] = dbranch[-1]
            dattn = _mm_nt(dbranch[0], wba_ref[...]).astype(BF16)
            for h in range(N_HEADS):
                do_h = dattn[:, HEAD_PAD * h:HEAD_PAD * (h + 1)]
                dattn_ref[h, rows, :] = do_h
                row_term = jnp.sum(o_ref[h, rows, :].astype(F32) * do_h.astype(F32), axis=1, keepdims=True)
                delta_ref[h, :, rows] = _col_to_row(row_term, rows.stop - rows.start)
            dpooled = _mm_nt(dbranch[1], wbb_ref[...])
            for g in range(len(POOL_WINDOWS)):
                lanes = slice(POOL_GROUP * g, POOL_GROUP * (g + 1))
                dpl = dpooled[:, lanes]
                d_g = d_ref[rows, lanes]
                dps_ref[:, lanes] += _colsum(dpl * _mm(d_g, wp_ref[g]))
                dyp = (dpl * ps_ref[:, lanes]).astype(BF16)
                dwp_ref[g] += lax.dot_general(d_g, dyp, TN, preferred_element_type=F32)
                dd_ref[rows, lanes] = _mm_nt(dyp, wp_ref[g])

    heads = pl.BlockSpec((N_HEADS, tm, HEAD_PAD), lambda i: (0, i, 0))
    return pl.pallas_call(
        body, name="merge_bwd", grid=(seq // tm,),
        in_specs=[_row(tm, D_MODEL), _row(tm, D_MODEL), _row(tm, 2 * D_MODEL), _row(tm, D_MODEL), _row(tm, D_MODEL),
                  _row(tm, POOL_WIDTH), heads, _fix((1, D_MODEL)), _fix((1, POOL_WIDTH)),
                  _fix(w_out.shape), _fix(w_ba.shape), _fix(w_bb.shape), _fix(w_pool_bf.shape)],
        out_specs=[_row(tm, D_MODEL), _row(tm, D_MODEL), _row(tm, D_MODEL), _row(tm, 2 * D_MODEL),
                   heads, _row(tm, POOL_WIDTH), pl.BlockSpec((N_HEADS, None, 1, tm), lambda i: (0, i, 0, 0)),
                   _fix((1, D_MODEL)), _fix((1, 2 * D_MODEL)), _fix((1, POOL_WIDTH)), _fix(w_pool_bf.shape)],
        out_shape=[_sds((seq, D_MODEL), BF16), _sds((seq, D_MODEL), BF16), _sds((seq, D_MODEL), BF16),
                   _sds((seq, 2 * D_MODEL), BF16), _sds((N_HEADS, seq, HEAD_PAD), BF16),
                   _sds((seq, POOL_WIDTH), F32), _sds((N_HEADS, seq // tm, 1, tm), F32),
                   _sds((1, D_MODEL), F32), _sds((1, 2 * D_MODEL), F32),
                   _sds((1, POOL_WIDTH), F32), _sds(w_pool_bf.shape, F32)],
        compiler_params=_params(1, 48),
    )(dh1, y, gates, ba, bb, d_pool, o_heads, g_post_mix, pool_scale, w_out, w_ba, w_bb, w_pool_bf)


def _proj_bwd(dq, dk, dv, qd, kvd, x, dh1, dgpre, dd, cf, s1, s2, g_pre, g_q, g_kv,
              w_in_r, w_uq_r, w_k_exp, w_v, tm):
    seq = x.shape[0]
    n_steps = seq // tm

    def body(dq_ref, dk_ref, dv_ref, qd_ref, kvd_ref, x_ref, dh1_ref, dgpre_ref, dd_ref, next_ref,
             cf_ref, s1_ref, s2_ref, gpre_ref, gq_ref, gkv_ref, win_hbm, wuq_hbm, wk_hbm, wv_hbm,
             gx_ref, dproj_ref, dqb_ref, qn_ref, kvn_ref, dkvb_ref, dgpre_acc, dgq_acc, dgkv_acc,
             win, wuq, wk, wv, load_sems):
        step = pl.program_id(0)
        _load_once(((win_hbm, win), (wuq_hbm, wuq), (wk_hbm, wk), (wv_hbm, wv)), load_sems)

        @pl.when(pl.program_id(0) == 0)
        def _():
            dgpre_acc[...] = jnp.zeros_like(dgpre_acc)
            dgq_acc[...] = jnp.zeros_like(dgq_acc)
            dgkv_acc[...] = jnp.zeros_like(dgkv_acc)

        for rows in _row_chains(tm):
            n_rows = rows.stop - rows.start
            cfv, s1v, s2v = cf_ref[rows, :], s1_ref[rows, :], s2_ref[rows, :]
            ksum = jnp.zeros((n_rows, HEAD_PAD), F32)
            for h in range(N_HEADS):
                lanes = slice(HEAD_PAD * h, HEAD_PAD * (h + 1))
                dqh = jnp.transpose(dq_ref[h, :, rows])
                dqb_ref[rows, lanes] = (_rope_t(dqh, cfv, s1v, s2v) * ATTN_SCALE).astype(BF16)
                dkh = dk_ref[h, rows, :]
                dkvb_ref[rows, lanes] = dkh
                dkvb_ref[rows, slice(K_ALL + lanes.start, K_ALL + lanes.stop)] = dv_ref[h, rows, :]
                ksum = ksum + dkh.astype(F32)
            lane = lax.broadcasted_iota(jnp.int32, (n_rows, HEAD_PAD), 1)
            rope_lanes = (lane >= QK_NOPE) & (lane < QK_NOPE + QK_ROPE)
            dkr = _rope_t(jnp.where(rope_lanes, ksum, 0.0), cfv, s1v, s2v)

            qdv = qd_ref[rows, :]
            rq = _rms_r(qdv)
            qhat = qdv * rq
            qn_ref[rows, :] = (qhat * gq_ref[...]).astype(BF16)
            dqn = _mm_nt(dqb_ref[rows, :], wuq[...])
            dgq_acc[...] += _colsum(dqn * qhat)
            dproj_ref[rows, IN_Q0:IN_KV0] = _rms_bwd(qhat, rq, gq_ref[...], dqn).astype(BF16)

            kvdv = kvd_ref[rows, :]
            rkv = _rms_r(kvdv)
            kvhat = kvdv * rkv
            kvn_ref[rows, :] = (kvhat * gkv_ref[...]).astype(BF16)
            dkvn = _mm_nt(dkvb_ref[rows, :K_ALL], wk[...]) + _mm_nt(dkvb_ref[rows, K_ALL:], wv[...])
            dgkv_acc[...] += _colsum(dkvn * kvhat)
            dproj_ref[rows, IN_KV0:IN_POOL0] = _rms_bwd(kvhat, rkv, gkv_ref[...], dkvn).astype(BF16)

            dd_t = dd_ref[rows, :]
            if rows.stop < tm:
                after = dd_ref[rows.stop:rows.stop + POOL_HALO, :]
            else:
                after = jnp.where(step == n_steps - 1, 0.0, next_ref[...])
            ext = jnp.concatenate([dd_t, after], axis=0)
            ext_rows = n_rows + POOL_HALO
            counts = _window_count(step * tm + rows.start, ext_rows)
            for g, window in enumerate(POOL_WINDOWS):
                lanes = slice(POOL_GROUP * g, POOL_GROUP * (g + 1))
                level = ext[:, lanes] / counts[g]
                reach = 1
                while reach < window:
                    level = level + pltpu.roll(level, ext_rows - reach, 0)
                    reach *= 2
                dproj_ref[rows, IN_POOL0 + lanes.start:IN_POOL0 + lanes.stop] = (
                    level[:n_rows] - dd_t[:, lanes]).astype(BF16)
            dproj_ref[rows, IN_GATE0:IN_KR0] = dgpre_ref[rows, :]
            dproj_ref[rows, IN_KR0:IN_R] = dkr.astype(BF16)

            da = _mm_nt(dproj_ref[rows, :], win[...])
            xv = x_ref[rows, :]
            r0 = _rms_r(xv)
            xhat = xv * r0
            dgpre_acc[...] += _colsum(da * xhat)
            gx_ref[rows, :] = dh1_ref[rows, :].astype(F32) + _rms_bwd(xhat, r0, gpre_ref[...], da)

    per_tile = ATTN_TILE // tm
    heads = pl.BlockSpec((N_HEADS, tm, HEAD_PAD), lambda i: (0, i, 0))
    heads_t = pl.BlockSpec((N_HEADS, None, HEAD_PAD, tm), lambda i: (0, i // per_tile, 0, i % per_tile))
    return pl.pallas_call(
        body, name="proj_bwd", grid=(seq // tm,),
        in_specs=[heads_t, heads, heads, _row(tm, Q_LORA), _row(tm, KV_LORA), _row(tm, D_MODEL),
                  _row(tm, D_MODEL), _row(tm, 2 * D_MODEL), _row(tm, POOL_WIDTH),
                  pl.BlockSpec((POOL_HALO, POOL_WIDTH),
                               lambda i: (jnp.minimum((i + 1) * (tm // POOL_HALO), seq // POOL_HALO - 1), 0)),
                  _row(tm, HEAD_PAD), _row(tm, HEAD_PAD), _row(tm, HEAD_PAD),
                  _fix((1, D_MODEL)), _fix((1, Q_LORA)), _fix((1, KV_LORA)), ANY, ANY, ANY, ANY],
        out_specs=[_row(tm, D_MODEL), _row(tm, IN_R), _row(tm, N_HEADS * HEAD_PAD), _row(tm, Q_LORA), _row(tm, KV_LORA),
                   _row(tm, 2 * K_ALL),
                   _fix((1, D_MODEL)), _fix((1, Q_LORA)), _fix((1, KV_LORA))],
        out_shape=[_sds((seq, D_MODEL), F32), _sds((seq, IN_R), BF16), _sds((seq, N_HEADS * HEAD_PAD), BF16),
                   _sds((seq, Q_LORA), BF16), _sds((seq, KV_LORA), BF16), _sds((seq, 2 * K_ALL), BF16),
                   _sds((1, D_MODEL), F32), _sds((1, Q_LORA), F32), _sds((1, KV_LORA), F32)],
        scratch_shapes=[pltpu.VMEM(w_in_r.shape, BF16), pltpu.VMEM(w_uq_r.shape, BF16),
                        pltpu.VMEM(w_k_exp.shape, BF16), pltpu.VMEM(w_v.shape, BF16), pltpu.SemaphoreType.DMA((4,))],
        compiler_params=_params(1, 60),
    )(dq, dk, dv, qd, kvd, x, dh1, dgpre, dd, dd, cf, s1, s2, g_pre, g_q, g_kv, w_in_r, w_uq_r, w_k_exp, w_v)


def _grad_w(a, b, name, square_a=False, by_device=False, with_bf16=False):
    by_heads = a.ndim == 3
    seq, k_dim = (a.shape[1], a.shape[0] * a.shape[2]) if by_heads else a.shape
    n_dim = b.shape[1]
    tk = min(k_dim, 1024)
    tn = n_dim // 2 if n_dim == IN_R else min(n_dim, 1024)
    ts = min(seq, 2048)
    shard = n_dim // N_DEV
    per_tile = tn // shard
    if by_device:
        out_spec = pl.BlockSpec((per_tile, tk, shard), lambda i, j, s: (j, i, 0))
        out_shape = _sds((N_DEV, k_dim, shard), F32)
    else:
        out_spec = pl.BlockSpec((tk, tn), lambda i, j, s: (i, j))
        out_shape = _sds((k_dim, n_dim), F32)

    n_seq_steps = seq // ts

    def body(a_ref, b_ref, o_ref, *narrow):
        @pl.when(pl.program_id(2) == 0)
        def _():
            o_ref[...] = jnp.zeros_like(o_ref)

        at = jnp.concatenate([a_ref[h] for h in range(a.shape[0])], axis=1) if by_heads else a_ref[...]
        if square_a:
            at = at * at
        part = lax.dot_general(at, b_ref[...], TN, preferred_element_type=F32)
        if by_device:
            for d in range(per_tile):
                o_ref[d] += part[:, d * shard:(d + 1) * shard]
        else:
            o_ref[...] += part
        if with_bf16:
            @pl.when(pl.program_id(2) == n_seq_steps - 1)
            def _():
                narrow[0][...] = o_ref[...].astype(BF16)

    return pl.pallas_call(
        body, name=name, grid=(k_dim // tk, n_dim // tn, n_seq_steps),
        in_specs=[pl.BlockSpec(a.shape[:1] + (ts, HEAD_PAD), lambda i, j, s: (0, s, 0)) if by_heads
                  else pl.BlockSpec((ts, tk), lambda i, j, s: (s, i)),
                  pl.BlockSpec((ts, tn), lambda i, j, s: (s, j))],
        out_specs=[out_spec, out_spec] if with_bf16 else out_spec,
        out_shape=[out_shape, _sds(out_shape.shape, BF16)] if with_bf16 else out_shape,
        compiler_params=_params(3, 48),
    )(a, b)


def _position():
    return lax.axis_index("x"), lax.axis_index("y"), lax.axis_index("c")


def _gather_copies(x_ref, slot, send_sems, recv_sems, local_sem, phases=("send", "forward", "finish"), sem_base=0):
    x, y, c = _position()
    me, sibling = (x, y, c), (x, y, 1 - c)
    chips = [(1 - x, y), (x, 1 - y), (1 - x, 1 - y)]

    def copy(k, block, to, src=None):
        return pltpu.make_async_remote_copy(
            src_ref=slot(*block) if src is None else src, dst_ref=slot(*block),
            send_sem=send_sems.at[sem_base + k], recv_sem=recv_sems.at[sem_base + k],
            device_id=to, device_id_type=MESH)

    mine = pltpu.make_async_copy(x_ref, slot(*me), local_sem)
    first = [copy(0, me, sibling, src=x_ref)]
    first += [copy(1 + j, me, (*chip, c), src=x_ref) for j, chip in enumerate(chips)]
    passed = [copy(4 + j, (*chip, c), sibling) for j, chip in enumerate(chips)]
    if "send" in phases:
        mine.start()
        for cp in first:
            cp.start()
    if "forward" in phases:
        for j, chip in enumerate(chips):
            copy(1 + j, (*chip, c), me).wait_recv()
            passed[j].start()
    if "finish" in phases:
        copy(0, sibling, me).wait_recv()
        for j, chip in enumerate(chips):
            copy(4 + j, (*chip, 1 - c), me).wait_recv()
        for cp in first + passed:
            cp.wait_send()
        mine.wait()


def _all_gather_hbm(block):
    def body(x_ref, out_ref, send_sems, recv_sems, local_sem):
        _gather_copies(x_ref, lambda px, py, pc: out_ref.at[4 * px + 2 * py + pc], send_sems, recv_sems, local_sem)

    return pl.pallas_call(
        body, name="gather_weights",
        in_specs=[ANY], out_specs=ANY,
        out_shape=_sds((N_DEV,) + block.shape, block.dtype),
        scratch_shapes=[pltpu.SemaphoreType.DMA((7,)), pltpu.SemaphoreType.DMA((7,)), pltpu.SemaphoreType.DMA],
    )(block)


def _replicated_update(grads, loss_block, ws, ms, vs):
    n_p = len(grads)
    sent = list(grads) + [loss_block]
    n_a = len(sent)

    def body(*refs):
        g_refs, refs = refs[:n_a], refs[n_a:]
        w_refs, m_refs, v_refs, refs = refs[:n_p], refs[n_p:2 * n_p], refs[2 * n_p:3 * n_p], refs[3 * n_p:]
        sum_refs, refs = refs[:n_a], refs[n_a:]
        d_refs, nm_refs, nv_refs, refs = refs[:n_p], refs[n_p:2 * n_p], refs[2 * n_p:3 * n_p], refs[3 * n_p:]
        bufs, (send_sems, recv_sems, local_sems) = refs[:n_a], refs[n_a:]
        x, y, c = _position()
        me = 4 * x + 2 * y + c
        local, remote = [], []
        for a in range(n_a):
            local.append(pltpu.make_async_copy(g_refs[a], bufs[a].at[me], local_sems.at[a]))
            for r in range(1, N_DEV):
                peer = (1 - x if r & 4 else x, 1 - y if r & 2 else y, 1 - c if r & 1 else c)
                remote.append(pltpu.make_async_remote_copy(
                    src_ref=g_refs[a], dst_ref=bufs[a].at[me],
                    send_sem=send_sems.at[(N_DEV - 1) * a + r - 1], recv_sem=recv_sems.at[(N_DEV - 1) * a + r - 1],
                    device_id=peer, device_id_type=MESH))
        for cp in local + remote:
            cp.start()
        for cp in remote:
            cp.wait_recv()
        for cp in remote:
            cp.wait_send()
        for cp in local:
            cp.wait()
        for a in range(n_a):
            acc = bufs[a][0]
            for d in range(1, N_DEV):
                acc = acc + bufs[a][d]
            if a == n_p:
                sum_refs[a][...] = acc
                continue
            delta, new_m, new_v = _adamw_math(acc, w_refs[a][...], m_refs[a][...], v_refs[a][...])
            sum_refs[a][...], d_refs[a][...], nm_refs[a][...], nv_refs[a][...] = acc, delta, new_m, new_v

    vmem = pl.BlockSpec(memory_space=pltpu.VMEM)
    like_w = [_sds(w.shape, F32) for w in ws]
    n_sem = (N_DEV - 1) * n_a
    outs = pl.pallas_call(
        body, name="replicated_update",
        in_specs=[vmem] * (n_a + 3 * n_p), out_specs=[vmem] * (n_a + 3 * n_p),
        out_shape=like_w + [_sds(loss_block.shape, F32)] + like_w * 3,
        scratch_shapes=[pltpu.VMEM((N_DEV,) + g.shape, F32) for g in sent]
                       + [pltpu.SemaphoreType.DMA((n_sem,)), pltpu.SemaphoreType.DMA((n_sem,)),
                          pltpu.SemaphoreType.DMA((n_a,))],
        compiler_params=pltpu.CompilerParams(vmem_limit_bytes=32 * MIB),
    )(*sent, *ws, *ms, *vs)
    return (outs[:n_p], outs[n_p], outs[n_a:n_a + n_p], outs[n_a + n_p:n_a + 2 * n_p], outs[n_a + 2 * n_p:])


def _exchange_pair(gs):
    n_w = len(gs)

    def body(*refs):
        g_refs, out_refs = refs[:n_w], refs[n_w:2 * n_w]
        send_sems, recv_sems = refs[2 * n_w:]
        x, y, c = _position()
        copies = []
        for w in range(n_w):
            for chip in range(4):
                cp = pltpu.make_async_remote_copy(
                    src_ref=g_refs[w].at[2 * chip + (1 - c)], dst_ref=out_refs[w].at[chip],
                    send_sem=send_sems.at[4 * w + chip], recv_sem=recv_sems.at[4 * w + chip],
                    device_id=(x, y, 1 - c), device_id_type=MESH)
                cp.start()
                copies.append(cp)
        for cp in copies:
            cp.wait_recv()
        for cp in copies:
            cp.wait_send()

    return pl.pallas_call(
        body, name="exchange_pair",
        in_specs=[ANY] * n_w, out_specs=[ANY] * n_w,
        out_shape=[_sds((4,) + g.shape[1:], g.dtype) for g in gs],
        scratch_shapes=[pltpu.SemaphoreType.DMA((4 * n_w,)), pltpu.SemaphoreType.DMA((4 * n_w,))],
    )(*gs)


def _exchange_chips(parts):
    n_w = len(parts)

    def body(*refs):
        p_refs, out_refs = refs[:n_w], refs[n_w:2 * n_w]
        send_sems, recv_sems = refs[2 * n_w:]
        x, y, c = _position()
        chips = [(1 - x, y), (x, 1 - y), (1 - x, 1 - y)]
        copies = []
        for w in range(n_w):
            for k, (px, py) in enumerate(chips):
                cp = pltpu.make_async_remote_copy(
                    src_ref=p_refs[w].at[2 * px + py], dst_ref=out_refs[w].at[k],
                    send_sem=send_sems.at[3 * w + k], recv_sem=recv_sems.at[3 * w + k],
                    device_id=(px, py, c), device_id_type=MESH)
                cp.start()
                copies.append(cp)
        for cp in copies:
            cp.wait_recv()
        for cp in copies:
            cp.wait_send()

    return pl.pallas_call(
        body, name="exchange_chips",
        in_specs=[ANY] * n_w, out_specs=[ANY] * n_w,
        out_shape=[_sds((3,) + p.shape[1:], p.dtype) for p in parts],
        scratch_shapes=[pltpu.SemaphoreType.DMA((3 * n_w,)), pltpu.SemaphoreType.DMA((3 * n_w,))],
    )(*parts)


def _row_tile(k):
    return 256 if k % 256 == 0 else 128


def _pair_sum(g, recv, place, name):
    _, k, n = g.shape
    tr = _row_tile(k)
    g4 = g.reshape(4, 2, k, n)

    def body(s_ref, g_ref, r_ref, o_ref):
        o_ref[...] = (g_ref[...] + r_ref[...]).astype(BF16)

    spec = pltpu.PrefetchScalarGridSpec(
        num_scalar_prefetch=1, grid=(4, k // tr),
        in_specs=[pl.BlockSpec((None, None, tr, n), lambda j, i, s: (j, s[2], i, 0)),
                  pl.BlockSpec((None, tr, n), lambda j, i, s: (j, i, 0))],
        out_specs=pl.BlockSpec((None, tr, n), lambda j, i, s: (j, i, 0)))
    return pl.pallas_call(
        body, name=name, grid_spec=spec, out_shape=_sds((4, k, n), BF16),
        compiler_params=_params(2, 32),
    )(place, g4, recv)


def _adamw_math(g, w, m, v):
    m = ADAM_B1 * m + (1.0 - ADAM_B1) * g
    v = ADAM_B2 * v + (1.0 - ADAM_B2) * (g * g)
    m_hat = m / (1.0 - ADAM_B1 ** ADAM_STEP)
    v_hat = v / (1.0 - ADAM_B2 ** ADAM_STEP)
    delta = -ADAM_LR * (m_hat / (jnp.sqrt(v_hat) + ADAM_EPS) + ADAM_WD * w)
    return delta, m, v


def _adamw_sharded(g, from_sibling, from_chips, place, w, m, v, name):
    _, k, n = g.shape
    tr = _row_tile(k)

    def body(s_ref, g_ref, sib_ref, r0_ref, r1_ref, r2_ref, w_ref, m_ref, v_ref, grad_ref, d_ref, nm_ref, nv_ref):
        grad = g_ref[...] + sib_ref[...]
        for r_ref in (r0_ref, r1_ref, r2_ref):
            grad = grad + r_ref[...].astype(F32)
        grad_ref[...] = grad
        d_ref[...], nm_ref[...], nv_ref[...] = _adamw_math(grad, w_ref[...], m_ref[...], v_ref[...])

    tile = pl.BlockSpec((None, tr, n), lambda i, s: (0, i, 0))

    def slot(j):
        return pl.BlockSpec((None, tr, n), lambda i, s: (j, i, 0))

    spec = pltpu.PrefetchScalarGridSpec(
        num_scalar_prefetch=1, grid=(k // tr,),
        in_specs=[pl.BlockSpec((None, tr, n), lambda i, s: (s[0], i, 0)),
                  pl.BlockSpec((None, tr, n), lambda i, s: (s[1], i, 0)),
                  slot(0), slot(1), slot(2), tile, tile, tile],
        out_specs=[tile] * 4)
    return pl.pallas_call(
        body, name=name, grid_spec=spec, out_shape=[_sds((1, k, n), F32)] * 4,
        compiler_params=_params(1, 48),
    )(place, g, from_sibling, from_chips, from_chips, from_chips, w, m, v)


def _adamw_direct(g, received, place, w, m, v, name):
    _, k, n = g.shape
    tr = _row_tile(k)

    def body(s_ref, g_ref, r_ref, w_ref, m_ref, v_ref, grad_ref, d_ref, nm_ref, nv_ref):
        grad = g_ref[...]
        for r in range(N_DEV - 1):
            grad = grad + r_ref[r].astype(F32)
        grad_ref[...] = grad
        d_ref[...], nm_ref[...], nv_ref[...] = _adamw_math(grad, w_ref[...], m_ref[...], v_ref[...])

    tile = pl.BlockSpec((None, tr, n), lambda i, s: (0, i, 0))
    spec = pltpu.PrefetchScalarGridSpec(
        num_scalar_prefetch=1, grid=(k // tr,),
        in_specs=[pl.BlockSpec((None, tr, n), lambda i, s: (s[0], i, 0)),
                  pl.BlockSpec((N_DEV - 1, tr, n), lambda i, s: (0, i, 0)), tile, tile, tile],
        out_specs=[tile] * 4)
    return pl.pallas_call(
        body, name=name, grid_spec=spec, out_shape=[_sds((1, k, n), F32)] * 4,
        compiler_params=_params(1, 48),
    )(place, g, received, w, m, v)


def _pack_rows(parts):
    parts = [a.reshape(-1, LANES) for a in parts]
    pad = (-sum(a.shape[0] for a in parts)) % PACK_ROW_TILE
    return jnp.concatenate(parts + [jnp.zeros((pad, LANES), parts[0].dtype)], axis=0)


def _full_from_gathered(gathered, entries, shard_shapes):
    out, off = {}, 0
    for (name, kind), (k, n) in zip(entries, shard_shapes):
        rows = k * n // LANES
        seg = gathered[:, off:off + rows].reshape(N_DEV, k, n)
        out[name] = jnp.transpose(seg, (1, 0, 2)).reshape(k, N_DEV * n) if kind == "col" else seg.reshape(N_DEV * k, n)
        off += rows
    return out


def _columns_by_device(a):
    k, n_all = a.shape
    return jnp.transpose(a.reshape(k, N_DEV, n_all // N_DEV), (1, 0, 2))


def _rows_by_device(a):
    k_all, n = a.shape
    return a.reshape(N_DEV, k_all // N_DEV, n)


def _rope_lane_frequencies():
    inv_freq = ROPE_THETA ** (-jnp.arange(0, QK_ROPE, 2, dtype=F32) / QK_ROPE)
    zeros = lambda n: jnp.zeros((n,), F32)
    return jnp.concatenate([zeros(QK_NOPE), inv_freq, inv_freq, zeros(HEAD_PAD - QK_NOPE - QK_ROPE)])[None, :]


def _rope_tables(pos_row, freq, tm):
    pos = jnp.transpose(jnp.broadcast_to(pos_row.astype(F32), (HEAD_PAD, tm)))
    ang = pos * freq
    cos, sin = jnp.cos(ang), jnp.sin(ang)
    lane = lax.broadcasted_iota(jnp.int32, (tm, HEAD_PAD), 1)
    first = (lane >= QK_NOPE) & (lane < QK_NOPE + QK_ROPE // 2)
    second = (lane >= QK_NOPE + QK_ROPE // 2) & (lane < QK_NOPE + QK_ROPE)
    cf = jnp.where(lane < QK_NOPE, 1.0, jnp.where(first | second, cos, 0.0))
    return cf, jnp.where(first, -sin, 0.0), jnp.where(second, sin, 0.0)


def _arrange_w_in(w):
    k = w.shape[0]
    zeros = lambda n: jnp.zeros((k, n), w.dtype)
    kr0 = Q_LORA + KV_LORA
    pool0 = kr0 + QK_ROPE
    return jnp.concatenate([w[:, :kr0], w[:, pool0:], zeros(QK_NOPE), w[:, kr0:pool0],
                            zeros(HEAD_PAD - QK_NOPE - QK_ROPE)], axis=1)


def _restore_w_in(d):
    kr = d[:, IN_KR0 + QK_NOPE:IN_KR0 + QK_NOPE + QK_ROPE]
    return jnp.concatenate([d[:, :IN_POOL0], kr, d[:, IN_POOL0:IN_KR0]], axis=1)


def _pad_heads(w, width):
    k = w.shape[0]
    w = w.reshape(k, N_HEADS, width)
    return jnp.pad(w, ((0, 0), (0, 0), (0, HEAD_PAD - width))).reshape(k, N_HEADS * HEAD_PAD)


def _unpad_heads(d, width):
    k = d.shape[0]
    return d.reshape(k, N_HEADS, HEAD_PAD)[:, :, :width]


def kernel(x, p, positions, g_pre_mix, w_in, b_gate, g_q, w_uq, g_kv, w_ukv, w_pool, pool_scale, w_branch_attn, w_branch_pool, w_out, g_post_mix, g_pre_mlp, w_ff1, w_ff2, g_post_mlp, w_ple_proj, w_ple_gate, g_ple, loss_target, m_g_pre_mix, m_w_in, m_b_gate, m_g_q, m_w_uq, m_g_kv, m_w_ukv, m_w_pool, m_pool_scale, m_w_branch_attn, m_w_branch_pool, m_w_out, m_g_post_mix, m_g_pre_mlp, m_w_ff1, m_w_ff2, m_g_post_mlp, m_w_ple_proj, m_w_ple_gate, m_g_ple, v_g_pre_mix, v_w_in, v_b_gate, v_g_q, v_w_uq, v_g_kv, v_w_ukv, v_w_pool, v_pool_scale, v_w_branch_attn, v_w_branch_pool, v_w_out, v_g_post_mix, v_g_pre_mlp, v_w_ff1, v_w_ff2, v_g_post_mlp, v_w_ple_proj, v_w_ple_gate, v_g_ple):
    given = dict(locals())
    weights = {n: given[n] for n in WEIGHT_ORDER}
    moments_m = {n: given["m_" + n] for n in WEIGHT_ORDER}
    moments_v = {n: given["v_" + n] for n in WEIGHT_ORDER}
    xs, ps, target = x[0], p[0, 0], loss_target[0]
    seq = xs.shape[0]
    tm = min(256, seq)
    tm_merge = min(512, seq)
    core = lax.axis_index("c")
    chip = 2 * lax.axis_index("x") + lax.axis_index("y")

    early, later = SHARDED[:N_EARLY], SHARDED[N_EARLY:]
    shapes_of = lambda entries: [weights[n].shape[1:] for n, _ in entries]
    pack_bf16 = lambda entries: _pack_rows([weights[n][0].astype(BF16) for n, _ in entries])
    full = _full_from_gathered(_all_gather_hbm(pack_bf16(early)), early, shapes_of(early))
    w_in_r = _arrange_w_in(full["w_in"])
    w_uq_r = _pad_heads(full["w_uq"], QK_NOPE + QK_ROPE)
    ukv = full["w_ukv"].reshape(KV_LORA, N_HEADS, QK_NOPE + V_HEAD)
    w_k_exp = _pad_heads(ukv[:, :, :QK_NOPE].reshape(KV_LORA, N_HEADS * QK_NOPE), QK_NOPE)
    w_v = _pad_heads(ukv[:, :, QK_NOPE:].reshape(KV_LORA, N_HEADS * V_HEAD), V_HEAD)
    w_pool_bf = w_pool[0].astype(BF16)

    packed_later = [e for e in later if e[0] != "w_ff1"]
    a_bf, qd, kvd, d_pool, pooled, gates, q, k, v, k_t, cf, s1, s2, gathered_later, gathered_ff1 = _proj_fwd(
        xs, g_pre_mix, b_gate, g_q, g_kv, positions, w_pool_bf, pool_scale, w_in_r, w_uq_r, w_k_exp, w_v,
        [pack_bf16(packed_later), w_ff1[0].astype(BF16)], tm_merge)
    full.update(_full_from_gathered(gathered_later, packed_later, shapes_of(packed_later)))
    full["w_ff1"] = gathered_ff1
    w_ba = jnp.pad(full["w_branch_attn"].reshape(N_HEADS, V_HEAD, D_MODEL),
                   ((0, 0), (0, HEAD_PAD - V_HEAD), (0, 0))).reshape(N_HEADS * HEAD_PAD, D_MODEL)
    o_heads, lse = _attn_fwd(q, k, v)
    merged, ba, bb, y, h1 = _merge_fwd(o_heads, pooled, gates, xs, g_post_mix, w_ba,
                                                  full["w_branch_pool"], full["w_out"], tm_merge)
    (m_bf, zr, f, h2_bf, p_bf, de, dpre, dh2, loss_acc, dg_ple) = _tail_fwd(
        h1, target, ps, g_pre_mlp, g_post_mlp, g_ple, full["w_ff1"], full["w_ff2"], full["w_ple_proj"],
        full["w_ple_gate"], tm)

    by_device, payload = {}, {}

    def keep(name, pair, layout=lambda g: g):
        by_device[name], payload[name] = layout(pair[0]), layout(pair[1])

    keep("w_ple_proj", _grad_w(p_bf, de, "grad_w_ple_proj", by_device=True, with_bf16=True))
    keep("w_ple_gate", _grad_w(h2_bf, dpre, "grad_w_ple_gate", with_bf16=True), _rows_by_device)
    df, dz, dh1, dg_pre_mlp, dg_post_mlp = _mlp_bwd(h1, f, zr, dh2, g_pre_mlp, g_post_mlp, full["w_ff1"],
                                                    full["w_ff2"], tm)
    keep("w_ff1", _grad_w(m_bf, dz, "grad_w_ff1", by_device=True, with_bf16=True))
    keep("w_ff2", _grad_w(zr, df, "grad_w_ff2", square_a=True, with_bf16=True), _rows_by_device)
    (dy, dba, dbb, dgpre, do_heads, dd, delta, dg_post_mix, db_gate, dpool_scale, dw_pool) = _merge_bwd(
        dh1, y, gates, ba, bb, d_pool, o_heads, g_post_mix, pool_scale, full["w_out"], w_ba,
        full["w_branch_pool"], w_pool_bf, tm_merge)
    keep("w_branch_attn", _grad_w(o_heads, dba, "grad_w_branch_attn", with_bf16=True),
          lambda g: _columns_by_device(g.reshape(N_HEADS, HEAD_PAD, D_MODEL)[:, :V_HEAD].reshape(-1, D_MODEL)))
    keep("w_branch_pool", _grad_w(pooled, dbb, "grad_w_branch_pool", by_device=True, with_bf16=True))
    keep("w_out", _grad_w(merged, dy, "grad_w_out", with_bf16=True), _rows_by_device)
    direct = [n for n, _ in SHARDED[N_EARLY:]]
    outs = _attn_bwd(q, k, k_t, v, do_heads, lse, delta, [payload[n] for n in direct])
    dq, dk, dv = outs[:3]
    received = dict(zip(direct, outs[3:]))
    (grad_x, dproj, dq_bf, qn_bf, kvn_bf, dkv_bf, dg_pre_mix, dg_q, dg_kv) = _proj_bwd(
        dq, dk, dv, qd, kvd, xs, dh1, dgpre, dd, cf, s1, s2, g_pre_mix, g_q, g_kv, w_in_r, w_uq_r, w_k_exp, w_v,
        tm_merge)
    d_w_kv = _grad_w(kvn_bf, dkv_bf, "grad_w_ukv")
    d_k_exp = _unpad_heads(d_w_kv[:, :K_ALL], QK_NOPE)
    d_w_v = _unpad_heads(d_w_kv[:, K_ALL:], V_HEAD)
    by_device["w_in"] = _columns_by_device(_restore_w_in(_grad_w(a_bf, dproj, "grad_w_in")))
    by_device["w_uq"] = _columns_by_device(
        _unpad_heads(_grad_w(qn_bf, dq_bf, "grad_w_uq"), QK_NOPE + QK_ROPE).reshape(Q_LORA, -1))
    by_device["w_ukv"] = _columns_by_device(jnp.concatenate([d_k_exp, d_w_v], axis=2).reshape(KV_LORA, -1))
    grads_small = {
        "g_pre_mix": dg_pre_mix, "b_gate": db_gate, "g_q": dg_q, "g_kv": dg_kv,
        "w_pool": dw_pool, "pool_scale": dpool_scale, "g_post_mix": dg_post_mix,
        "g_pre_mlp": dg_pre_mlp, "g_post_mlp": dg_post_mlp, "g_ple": dg_ple,
    }

    names = [n for n, _ in SHARDED]
    place = jnp.stack([2 * chip + core, chip, core]).astype(jnp.int32)
    sharded = {n: _adamw_direct(by_device[n], received[n], place, weights[n], moments_m[n], moments_v[n],
                                "adamw_" + n) for n in direct}
    last = [n for n, _ in SHARDED[:N_EARLY]]
    own = [by_device[n] for n in last]
    from_sibling = _exchange_pair(own)
    pair = [_pair_sum(g, r, place, "pair_sum_" + n) for n, g, r in zip(last, own, from_sibling)]
    from_chips = _exchange_chips(pair)
    sharded.update({n: _adamw_sharded(g, r, rc, place, weights[n], moments_m[n], moments_v[n], "adamw_" + n)
                    for n, g, r, rc in zip(last, own, from_sibling, from_chips)})

    flat = lambda a: a.reshape(a.shape[-3:]) if a.ndim > 3 else a
    g_sm, loss_sum, d_sm, m_sm, v_sm = _replicated_update(
        [flat(grads_small[n]) for n in REPLICATED], loss_acc, [flat(weights[n]) for n in REPLICATED],
        [flat(moments_m[n]) for n in REPLICATED], [flat(moments_v[n]) for n in REPLICATED])

    results = []
    for which, small in enumerate((g_sm, d_sm, m_sm, v_sm)):
        named = {n: sharded[n][which] for n in names}
        named.update({n: a.reshape(weights[n].shape) for n, a in zip(REPLICATED, small)})
        results.append([named[n] for n in WEIGHT_ORDER])

    return (loss_sum[0, 0], grad_x[None], *results[0], *results[1], *results[2], *results[3])
```

```python
import jax
import jax.numpy as jnp
from jax import lax
from jax.experimental import pallas as pl
from jax.experimental.pallas import tpu as pltpu

F32 = jnp.float32
BF16 = jnp.bfloat16

D_MODEL = 1024
PLE_DIM = 256
N_HEADS = 8
QK_NOPE = 64
QK_ROPE = 32
V_HEAD = 64
Q_LORA = 384
KV_LORA = 256
POOL_WINDOWS = (2, 4, 8, 16)
POOL_GROUP = 128
POOL_WIDTH = 512
D_FF = 4096
ROPE_THETA = 10000.0
EPS = 1e-6
HEAD_PAD = 128
K_ALL = N_HEADS * HEAD_PAD
ATTN_SCALE = (QK_NOPE + QK_ROPE) ** -0.5
LOG2E = 1.4426950408889634
Q_PRESCALE = ATTN_SCALE * LOG2E
ATTN_TILE = 512
FWD_ROWS = 512
FWD_CHAINS = 4
BWD_CHAINS = 2
BWD_QUERIES = 4

ADAM_LR = 0.001
ADAM_B1 = 0.9
ADAM_B2 = 0.999
ADAM_EPS = 1e-08
ADAM_WD = 0.01
ADAM_STEP = 10

N_DEV = 8
LANES = 1024
PACK_ROW_TILE = 480
POOL_HALO = 16
MIB = 2 ** 20

IN_Q0, IN_KV0, IN_POOL0, IN_GATE0, IN_KR0, IN_R = 0, 384, 640, 1152, 3200, 3328

SHARDED = (("w_in", "col"), ("w_uq", "col"), ("w_ukv", "col"), ("w_branch_attn", "col"),
           ("w_branch_pool", "col"), ("w_out", "row"), ("w_ff1", "col"), ("w_ff2", "row"),
           ("w_ple_proj", "col"), ("w_ple_gate", "row"))
N_EARLY = 3
REPLICATED = ("g_pre_mix", "b_gate", "g_q", "g_kv", "w_pool", "pool_scale", "g_post_mix",
              "g_pre_mlp", "g_post_mlp", "g_ple")
WEIGHT_ORDER = ("g_pre_mix", "w_in", "b_gate", "g_q", "w_uq", "g_kv", "w_ukv", "w_pool", "pool_scale",
                "w_branch_attn", "w_branch_pool", "w_out", "g_post_mix", "g_pre_mlp", "w_ff1", "w_ff2",
                "g_post_mlp", "w_ple_proj", "w_ple_gate", "g_ple")

NT = (((1,), (1,)), ((), ()))
TN = (((0,), (0,)), ((), ()))
MESH = pl.DeviceIdType.MESH
ANY = pl.BlockSpec(memory_space=pl.ANY)


def _params(n_axes, vmem_mib):
    return pltpu.CompilerParams(dimension_semantics=("arbitrary",) * n_axes, vmem_limit_bytes=vmem_mib * MIB)


def _row(tm, n):
    return pl.BlockSpec((tm, n), lambda i: (i, 0))


def _fix(shape):
    zeros = (0,) * len(shape)
    return pl.BlockSpec(shape, lambda i: zeros)


def _sds(shape, dtype):
    return jax.ShapeDtypeStruct(shape, dtype)


def _rms_r(v):
    return lax.rsqrt(jnp.mean(v * v, axis=-1, keepdims=True) + EPS)


def _rms_bwd(vhat, r, g, dy):
    gdy = dy * g
    return r * (gdy - vhat * jnp.mean(gdy * vhat, axis=-1, keepdims=True))


def _colsum(v):
    return jnp.sum(v, axis=0, keepdims=True)


def _sigmoid(v):
    return 1.0 / (1.0 + jnp.exp(-v))


def _mm(a, b):
    return jnp.dot(a, b, preferred_element_type=F32)


def _mm_nt(a, b):
    return lax.dot_general(a, b, NT, preferred_element_type=F32)


def _rope(c, cf, s1, s2):
    return c * cf + pltpu.roll(c, HEAD_PAD - 16, 1) * s1 + pltpu.roll(c, 16, 1) * s2


def _rope_t(c, cf, s1, s2):
    return c * cf + pltpu.roll(c * s1, 16, 1) + pltpu.roll(c * s2, HEAD_PAD - 16, 1)


def _row_chains(tm, rows=256):
    rows = min(rows, tm)
    return [slice(c * rows, (c + 1) * rows) for c in range(tm // rows)]


def _load_once(pairs, sems):
    @pl.when(pl.program_id(0) == 0)
    def _():
        copies = [pltpu.make_async_copy(src, dst, sems.at[i]) for i, (src, dst) in enumerate(pairs)]
        for i, cp in enumerate(copies):
            cp.start(priority=i % 2)
        for cp in copies:
            cp.wait()


def _column_blocks(by_device_hbm, full_vmem):
    n = by_device_hbm.shape[2]
    return tuple((by_device_hbm.at[d], full_vmem.at[:, d * n:(d + 1) * n]) for d in range(N_DEV))


def _proj_fwd(x, g_pre, b_gate, g_q, g_kv, positions, w_pool_bf, pool_scale, w_in_r, w_uq_r, w_k_exp, w_v,
              later_shards, tm):
    seq = x.shape[0]
    n_steps = seq // tm
    forward_step = (3 * n_steps) // 4
    n_later = len(later_shards)

    def body(x_ref, gpre_ref, bg_ref, gq_ref, gkv_ref, pos_ref, freq_ref, wpool_ref, pscale_ref,
             win_hbm, wuq_hbm, wk_hbm, wv_hbm, *rest):
        later_refs, rest = rest[:n_later], rest[n_later:]
        (a_ref, qd_ref, kvd_ref, dpool_ref, pooled_ref, gates_ref, q_ref, k_ref, v_ref, kt_ref,
         cf_ref, s1_ref, s2_ref) = rest[:13]
        gathered_refs, rest = rest[13:13 + n_later], rest[13 + n_later:]
        win, wuq, wk, wv, halo_ref, send_sems, recv_sems, local_sems, load_sems = rest
        step = pl.program_id(0)

        def gather(phase):
            for a, (src, dst) in enumerate(zip(later_refs, gathered_refs)):
                _gather_copies(src, lambda px, py, pc, dst=dst: dst.at[4 * px + 2 * py + pc],
                               send_sems, recv_sems, local_sems.at[a], phases=(phase,), sem_base=7 * a)

        pl.when(step == 0)(lambda: gather("send"))
        pl.when(step == forward_step)(lambda: gather("forward"))
        _load_once(((win_hbm, win), (wuq_hbm, wuq), (wk_hbm, wk), (wv_hbm, wv)), load_sems)
        for rows in _row_chains(tm):
            n_rows = rows.stop - rows.start
            xv = x_ref[rows, :]
            a = (xv * _rms_r(xv) * gpre_ref[...]).astype(BF16)
            a_ref[rows, :] = a
            proj = _mm(a, win[...])
            qd = proj[:, IN_Q0:IN_KV0]
            kvd = proj[:, IN_KV0:IN_POOL0]
            qd_ref[rows, :] = qd
            kvd_ref[rows, :] = kvd
            gates_ref[rows, :] = _sigmoid(proj[:, IN_GATE0:IN_KR0] + bg_ref[...]).astype(BF16)

            u = proj[:, IN_POOL0:IN_GATE0]
            before = jnp.where(step == 0, 0.0, halo_ref[...]) if rows.start == 0 else tail
            tail = u[n_rows - POOL_HALO:, :]
            level = jnp.concatenate([before, u], axis=0)
            counts = _window_count(step * tm + rows.start, n_rows)
            shift = 1
            for g in range(len(POOL_WINDOWS)):
                level = level + pltpu.roll(level, shift, 0)
                shift *= 2
                lanes = slice(POOL_GROUP * g, POOL_GROUP * (g + 1))
                d = (level[POOL_HALO:, lanes] / counts[g] - u[:, lanes]).astype(BF16)
                dpool_ref[rows, lanes] = d
                pooled_ref[rows, lanes] = (_mm(d, wpool_ref[g]) * pscale_ref[:, lanes]).astype(BF16)
            if rows.stop == tm:
                halo_ref[...] = tail
            cfv, s1v, s2v = _rope_tables(pos_ref[:, rows], freq_ref[...], n_rows)
            cf_ref[rows, :], s1_ref[rows, :], s2_ref[rows, :] = cfv, s1v, s2v
            krr = _rope(proj[:, IN_KR0:IN_R], cfv, s1v, s2v)
            qn = (qd * _rms_r(qd) * gq_ref[...]).astype(BF16)
            qf = _mm(qn, wuq[...])
            kvn = (kvd * _rms_r(kvd) * gkv_ref[...]).astype(BF16)
            kf = _mm(kvn, wk[...])
            vf = _mm(kvn, wv[...])
            one_lane = (lax.broadcasted_iota(jnp.int32, (n_rows, HEAD_PAD), 1) == V_HEAD).astype(F32)
            for h in range(N_HEADS):
                lanes = slice(HEAD_PAD * h, HEAD_PAD * (h + 1))
                q_ref[h, rows, :] = (_rope(qf[:, lanes], cfv, s1v, s2v) * Q_PRESCALE).astype(BF16)
                kh = kf[:, lanes] + krr
                vh = vf[:, lanes] + one_lane
                k_ref[h, rows, :] = kh.astype(BF16)
                v_ref[h, rows, :] = vh.astype(BF16)
                kt_ref[h, :, rows] = jnp.transpose(kh).astype(BF16)
        pl.when(step == n_steps - 1)(lambda: gather("finish"))

    per_tile = ATTN_TILE // tm
    heads = pl.BlockSpec((N_HEADS, tm, HEAD_PAD), lambda i: (0, i, 0))
    heads_t = pl.BlockSpec((N_HEADS, None, HEAD_PAD, tm), lambda i: (0, i // per_tile, 0, i % per_tile))
    heads_t_shape = _sds((N_HEADS, seq // ATTN_TILE, HEAD_PAD, ATTN_TILE), BF16)
    return pl.pallas_call(
        body, name="proj_fwd", grid=(seq // tm,),
        in_specs=[_row(tm, D_MODEL), _fix((1, D_MODEL)), _fix((1, 2 * D_MODEL)), _fix((1, Q_LORA)), _fix((1, KV_LORA)),
                  pl.BlockSpec((1, tm), lambda i: (0, i)), _fix((1, HEAD_PAD)), _fix(w_pool_bf.shape), _fix((1, POOL_WIDTH)),
                  ANY, ANY, ANY, ANY] + [ANY] * n_later,
        out_specs=[_row(tm, D_MODEL), _row(tm, Q_LORA), _row(tm, KV_LORA), _row(tm, POOL_WIDTH), _row(tm, POOL_WIDTH),
                   _row(tm, 2 * D_MODEL), heads, heads, heads, heads_t,
                   _row(tm, HEAD_PAD), _row(tm, HEAD_PAD), _row(tm, HEAD_PAD)] + [ANY] * n_later,
        out_shape=[_sds((seq, D_MODEL), BF16), _sds((seq, Q_LORA), F32), _sds((seq, KV_LORA), F32),
                   _sds((seq, POOL_WIDTH), BF16), _sds((seq, POOL_WIDTH), BF16), _sds((seq, 2 * D_MODEL), BF16),
                   _sds((N_HEADS, seq, HEAD_PAD), BF16), _sds((N_HEADS, seq, HEAD_PAD), BF16),
                   _sds((N_HEADS, seq, HEAD_PAD), BF16), heads_t_shape,
                   _sds((seq, HEAD_PAD), F32), _sds((seq, HEAD_PAD), F32), _sds((seq, HEAD_PAD), F32)]
                  + [_sds((N_DEV,) + s.shape, s.dtype) for s in later_shards],
        scratch_shapes=[pltpu.VMEM(w_in_r.shape, BF16), pltpu.VMEM(w_uq_r.shape, BF16),
                        pltpu.VMEM(w_k_exp.shape, BF16), pltpu.VMEM(w_v.shape, BF16),
                        pltpu.VMEM((POOL_HALO, POOL_WIDTH), F32),
                        pltpu.SemaphoreType.DMA((7 * n_later,)), pltpu.SemaphoreType.DMA((7 * n_later,)),
                        pltpu.SemaphoreType.DMA((n_later,)), pltpu.SemaphoreType.DMA((4,))],
        compiler_params=_params(1, 48),
    )(x, g_pre, b_gate, g_q, g_kv, positions, _rope_lane_frequencies(), w_pool_bf, pool_scale,
      w_in_r, w_uq_r, w_k_exp, w_v, *later_shards)


def _window_count(row0, n_rows):
    t = row0 + lax.broadcasted_iota(jnp.int32, (n_rows, POOL_GROUP), 0)
    return [jnp.minimum(t + 1, w).astype(F32) for w in POOL_WINDOWS]


def _col_to_row(col, n):
    return jnp.transpose(jnp.broadcast_to(col, (n, HEAD_PAD)))[0:1, :]


def _attn_fwd(q, k, v):
    heads, seq, _ = q.shape
    r = FWD_ROWS
    n = min(FWD_CHAINS, seq // r)
    block = r * n

    def body(q_ref, k_ref, v_ref, o_ref, lse_ref):
        qi = pl.program_id(1)
        q_tiles = [q_ref[c * r:(c + 1) * r, :] for c in range(n)]

        def tile(qt, j, m, acc, diagonal):
            start = pl.multiple_of(j * r, r)
            s = _mm_nt(qt, k_ref[pl.ds(start, r), :])
            if diagonal:
                row = lax.broadcasted_iota(jnp.int32, (r, r), 0)
                col = lax.broadcasted_iota(jnp.int32, (r, r), 1)
                s = jnp.where(col <= row, s, -jnp.inf)
            m_new = jnp.maximum(m, jnp.max(s, axis=1, keepdims=True))
            p = jnp.exp2((s - m_new).astype(BF16))
            acc = jnp.exp2(m - m_new) * acc + _mm(p, v_ref[pl.ds(start, r), :])
            return m_new, acc

        def all_chains(jj, carry):
            for u in range(n):
                carry = tuple(tile(q_tiles[c], n * jj + u, *carry[c], False) for c in range(n))
            return carry

        init = tuple((jnp.full((r, 1), -jnp.inf, F32), jnp.zeros((r, HEAD_PAD), F32)) for _ in range(n))
        state = list(lax.fori_loop(0, qi, all_chains, init))
        for d in range(n):
            for c in range(d, n):
                state[c] = tile(q_tiles[c], n * qi + d, *state[c], c == d)
        for c, (m, acc) in enumerate(state):
            l = acc[:, V_HEAD:V_HEAD + 1]
            o_ref[c * r:(c + 1) * r, :] = (acc / l).astype(BF16)
            row0 = c * r
            lse_ref[row0 // ATTN_TILE, :, row0 % ATTN_TILE:row0 % ATTN_TILE + r] = _col_to_row(m + jnp.log2(l), r)

    return pl.pallas_call(
        body, name="attn_fwd", grid=(heads, seq // block),
        in_specs=[pl.BlockSpec((None, block, HEAD_PAD), lambda h, i: (h, i, 0)),
                  pl.BlockSpec((None, seq, HEAD_PAD), lambda h, i: (h, 0, 0)),
                  pl.BlockSpec((None, seq, HEAD_PAD), lambda h, i: (h, 0, 0))],
        out_specs=[pl.BlockSpec((None, block, HEAD_PAD), lambda h, i: (h, i, 0)),
                   pl.BlockSpec((None, block // ATTN_TILE, 1, ATTN_TILE), lambda h, i: (h, i, 0, 0))],
        out_shape=[_sds((heads, seq, HEAD_PAD), BF16), _sds((heads, seq // ATTN_TILE, 1, ATTN_TILE), F32)],
        compiler_params=_params(2, 48),
    )(q, k, v)


def _peer_copies(src_refs, dst_refs, send_sems, recv_sems):
    x, y, c = _position()
    copies = []
    for w, (src, dst) in enumerate(zip(src_refs, dst_refs)):
        for r in range(1, N_DEV):
            px = 1 - x if r & 4 else x
            py = 1 - y if r & 2 else y
            pc = 1 - c if r & 1 else c
            copies.append(pltpu.make_async_remote_copy(
                src_ref=src.at[4 * px + 2 * py + pc], dst_ref=dst.at[r - 1],
                send_sem=send_sems.at[(N_DEV - 1) * w + r - 1], recv_sem=recv_sems.at[(N_DEV - 1) * w + r - 1],
                device_id=(px, py, pc), device_id_type=MESH))
    return copies


def _attn_bwd(q, k, k_t, v, do, lse, delta, early_grads):
    heads, seq, _ = q.shape
    t = ATTN_TILE
    nq = seq // t
    n = min(BWD_CHAINS, nq)
    n_w = len(early_grads)

    def body(q_ref, k_ref, kt_ref, v_ref, do_ref, lse_ref, delta_ref, *rest):
        grad_refs, rest = rest[:n_w], rest[n_w:]
        dq_ref, dk_ref, dv_ref = rest[:3]
        recv_refs, (send_sems, recv_sems) = rest[3:3 + n_w], rest[3 + n_w:]
        jp = pl.program_id(1)
        head = pl.program_id(0)

        @pl.when((head == 0) & (jp == 0))
        def _():
            for cp in _peer_copies(grad_refs, recv_refs, send_sems, recv_sems):
                cp.start()

        @pl.when(jp == 0)
        def _():
            dq_ref[...] = jnp.zeros_like(dq_ref)

        keys = [k_ref[c * t:(c + 1) * t, :] for c in range(n)]
        values = [v_ref[c * t:(c + 1) * t, :] for c in range(n)]

        def tile(c, i, dk, dv, diagonal):
            start = pl.multiple_of(i * t, t)
            qt = q_ref[pl.ds(start, t), :]
            dot = do_ref[pl.ds(start, t), :]
            p_t = jnp.exp2(_mm_nt(keys[c], qt) - lse_ref[i])
            if diagonal:
                key = lax.broadcasted_iota(jnp.int32, (t, t), 0)
                query = lax.broadcasted_iota(jnp.int32, (t, t), 1)
                p_t = jnp.where(key <= query, p_t, 0.0)
            dv = dv + _mm(p_t.astype(BF16), dot)
            ds_t = (p_t * (_mm_nt(values[c], dot) - delta_ref[i])).astype(BF16)
            dk = dk + _mm(ds_t, qt)
            return dk, dv, _mm(kt_ref[c], ds_t)

        def query_tile(i, state, first_rows):
            dq = None
            for c in range(n if first_rows is None else first_rows + 1):
                dk, dv, dq_c = tile(c, i, *state[c], diagonal=(c == first_rows))
                state[c] = (dk, dv)
                dq = dq_c if dq is None else dq + dq_c
            dq_ref[i] += dq

        def passes(width):
            def run(ip, carry):
                state = list(carry)
                for u in range(width):
                    query_tile(width * ip + u, state, None)
                return tuple(state)
            return run

        zero = jnp.zeros((t, HEAD_PAD), F32)
        state = [(zero, zero)] * n
        for offset in range(n):
            query_tile(n * jp + offset, state, offset)
        half = BWD_QUERIES // 2
        first = n * (jp + 1)
        odd = ((nq - first) // half) % 2 == 1
        state = lax.cond(odd, lambda s: passes(half)(first // half, s), lambda s: s, tuple(state))
        first_pass = (first + jnp.where(odd, half, 0)) // BWD_QUERIES
        state = lax.fori_loop(first_pass, nq // BWD_QUERIES, passes(BWD_QUERIES), state)
        for c, (dk, dv) in enumerate(state):
            dk_ref[c * t:(c + 1) * t, :] = (dk * (1.0 / LOG2E)).astype(BF16)
            dv_ref[c * t:(c + 1) * t, :] = dv.astype(BF16)

        @pl.when((head == heads - 1) & (jp == nq // n - 1))
        def _():
            copies = _peer_copies(grad_refs, recv_refs, send_sems, recv_sems)
            for cp in copies:
                cp.wait_recv()
            for cp in copies:
                cp.wait_send()

    whole = pl.BlockSpec((None, seq, HEAD_PAD), lambda h, j: (h, 0, 0))
    whole_t = pl.BlockSpec((None, nq, HEAD_PAD, t), lambda h, j: (h, 0, 0, 0))
    pair = pl.BlockSpec((None, n * t, HEAD_PAD), lambda h, j: (h, j, 0))
    pair_t = pl.BlockSpec((None, n, HEAD_PAD, t), lambda h, j: (h, j, 0, 0))
    stats = pl.BlockSpec((None, nq, 1, t), lambda h, j: (h, 0, 0, 0))
    return pl.pallas_call(
        body, name="attn_bwd", grid=(heads, nq // n),
        in_specs=[whole, pair, pair_t, pair, whole, stats, stats] + [ANY] * n_w,
        out_specs=[whole_t, pair, pair] + [ANY] * n_w,
        out_shape=[_sds((heads, nq, HEAD_PAD, t), F32), _sds((heads, seq, HEAD_PAD), BF16),
                   _sds((heads, seq, HEAD_PAD), BF16)]
                  + [_sds((N_DEV - 1,) + g.shape[1:], g.dtype) for g in early_grads],
        scratch_shapes=[pltpu.SemaphoreType.DMA(((N_DEV - 1) * n_w,)), pltpu.SemaphoreType.DMA(((N_DEV - 1) * n_w,))],
        compiler_params=_params(2, 56),
    )(q, k, k_t, v, do, lse, delta, *early_grads)


def _merge_fwd(attn, pooled, gates, x, g_post_mix, w_ba, w_bb, w_out, tm):
    seq = x.shape[0]

    def body(attn_ref, pooled_ref, gates_ref, x_ref, g_ref, wba_ref, wbb_ref, wout_ref,
             merged_ref, ba_ref, bb_ref, y_ref, h1_ref):
        for rows in _row_chains(tm):
            attn = jnp.concatenate([attn_ref[h, rows, :] for h in range(N_HEADS)], axis=1)
            ba = _mm(attn, wba_ref[...])
            bb = _mm(pooled_ref[rows, :], wbb_ref[...])
            ba_ref[rows, :] = ba.astype(BF16)
            bb_ref[rows, :] = bb.astype(BF16)
            merged = (gates_ref[rows, :D_MODEL].astype(F32) * ba
                      + gates_ref[rows, D_MODEL:].astype(F32) * bb).astype(BF16)
            merged_ref[rows, :] = merged
            y = _mm(merged, wout_ref[...])
            y_ref[rows, :] = y
            h1_ref[rows, :] = x_ref[rows, :] + y * _rms_r(y) * g_ref[...]

    return pl.pallas_call(
        body, name="merge_fwd", grid=(seq // tm,),
        in_specs=[pl.BlockSpec((N_HEADS, tm, HEAD_PAD), lambda i: (0, i, 0)), _row(tm, POOL_WIDTH),
                  _row(tm, 2 * D_MODEL), _row(tm, D_MODEL),
                  _fix((1, D_MODEL)), _fix(w_ba.shape), _fix(w_bb.shape), _fix(w_out.shape)],
        out_specs=[_row(tm, D_MODEL)] * 5,
        out_shape=[_sds((seq, D_MODEL), BF16), _sds((seq, D_MODEL), BF16), _sds((seq, D_MODEL), BF16),
                   _sds((seq, D_MODEL), F32), _sds((seq, D_MODEL), F32)],
        compiler_params=_params(1, 48),
    )(attn, pooled, gates, x, g_post_mix, w_ba, w_bb, w_out)


def _tail_fwd(h1, target, p, g_pre_mlp, g_post_mlp, g_ple, w_ff1, w_ff2, w_pe, w_pg, tm):
    seq = h1.shape[0]

    def body(h1_ref, tgt_ref, p_ref, gm_ref, gf_ref, gp_ref, w1_hbm, w2_hbm, wpe_hbm, wpg_hbm,
             m_ref, zr_ref, f_ref, h2b_ref, pb_ref, de_ref, dpre_ref, dh2_ref, loss_ref, dgple_ref,
             w1, w2, wpe, wpg, load_sems):
        _load_once(_column_blocks(w1_hbm, w1) + ((w2_hbm, w2), (wpe_hbm, wpe), (wpg_hbm, wpg)), load_sems)

        @pl.when(pl.program_id(0) == 0)
        def _():
            loss_ref[...] = jnp.zeros_like(loss_ref)
            dgple_ref[...] = jnp.zeros_like(dgple_ref)

        h1v = h1_ref[...]
        m = (h1v * _rms_r(h1v) * gm_ref[...]).astype(BF16)
        m_ref[...] = m
        zr = jnp.maximum(_mm(m, w1[...]), 0.0)
        zr_ref[...] = zr.astype(BF16)
        f = _mm((zr * zr).astype(BF16), w2[...])
        f_ref[...] = f
        h2 = h1v + f * _rms_r(f) * gf_ref[...]
        h2b = h2.astype(BF16)
        h2b_ref[...] = h2b
        pb = p_ref[...].astype(BF16)
        pb_ref[...] = pb
        e = _mm(pb, wpe[...])
        pg = _sigmoid(_mm(h2b, wpg[...]))
        t3 = pg * e
        r3 = _rms_r(t3)
        t3hat = t3 * r3
        diff = h2 + t3hat * gp_ref[...] - tgt_ref[...]
        loss_ref[...] += jnp.sum(diff * diff) * (0.5 / D_MODEL)
        dh3 = diff * (1.0 / D_MODEL)
        dgple_ref[...] += _colsum(dh3 * t3hat)
        dt3 = _rms_bwd(t3hat, r3, gp_ref[...], dh3)
        de_ref[...] = (dt3 * pg).astype(BF16)
        dpre = (dt3 * e * pg * (1.0 - pg)).astype(BF16)
        dpre_ref[...] = dpre
        dh2_ref[...] = (dh3 + _mm_nt(dpre, wpg[...])).astype(BF16)

    return pl.pallas_call(
        body, name="tail_fwd", grid=(seq // tm,),
        in_specs=[_row(tm, D_MODEL), _row(tm, D_MODEL), _row(tm, PLE_DIM), _fix((1, D_MODEL)), _fix((1, D_MODEL)),
                  _fix((1, D_MODEL)), ANY, ANY, ANY, ANY],
        out_specs=[_row(tm, D_MODEL), _row(tm, D_FF), _row(tm, D_MODEL), _row(tm, D_MODEL), _row(tm, PLE_DIM),
                   _row(tm, D_MODEL), _row(tm, D_MODEL), _row(tm, D_MODEL), _fix((8, 128)), _fix((1, D_MODEL))],
        out_shape=[_sds((seq, D_MODEL), BF16), _sds((seq, D_FF), BF16), _sds((seq, D_MODEL), F32),
                   _sds((seq, D_MODEL), BF16), _sds((seq, PLE_DIM), BF16), _sds((seq, D_MODEL), BF16),
                   _sds((seq, D_MODEL), BF16), _sds((seq, D_MODEL), BF16), _sds((8, 128), F32), _sds((1, D_MODEL), F32)],
        scratch_shapes=[pltpu.VMEM((w_ff1.shape[1], N_DEV * w_ff1.shape[2]), BF16), pltpu.VMEM(w_ff2.shape, BF16),
                        pltpu.VMEM(w_pe.shape, BF16), pltpu.VMEM(w_pg.shape, BF16),
                        pltpu.SemaphoreType.DMA((N_DEV + 3,))],
        compiler_params=_params(1, 56),
    )(h1, target, p, g_pre_mlp, g_post_mlp, g_ple, w_ff1, w_ff2, w_pe, w_pg)


def _mlp_bwd(h1, f, zr, dh2, g_pre_mlp, g_post_mlp, w_ff1, w_ff2, tm):
    seq = h1.shape[0]

    def body(h1_ref, f_ref, zr_ref, dh2_ref, gm_ref, gf_ref, w1_hbm, w2_hbm,
             df_ref, dz_ref, dh1_ref, dgm_ref, dgf_ref, w1, w2, load_sems):
        _load_once(_column_blocks(w1_hbm, w1) + ((w2_hbm, w2),), load_sems)

        @pl.when(pl.program_id(0) == 0)
        def _():
            dgm_ref[...] = jnp.zeros_like(dgm_ref)
            dgf_ref[...] = jnp.zeros_like(dgf_ref)

        dh2 = dh2_ref[...].astype(F32)
        fv = f_ref[...]
        rf = _rms_r(fv)
        fhat = fv * rf
        dgf_ref[...] += _colsum(dh2 * fhat)
        df = _rms_bwd(fhat, rf, gf_ref[...], dh2).astype(BF16)
        df_ref[...] = df
        dz = (_mm_nt(df, w2[...]) * (2.0 * zr_ref[...].astype(F32))).astype(BF16)
        dz_ref[...] = dz
        dm = _mm_nt(dz, w1[...])
        h1v = h1_ref[...]
        r1 = _rms_r(h1v)
        h1hat = h1v * r1
        dgm_ref[...] += _colsum(dm * h1hat)
        dh1_ref[...] = (dh2 + _rms_bwd(h1hat, r1, gm_ref[...], dm)).astype(BF16)

    return pl.pallas_call(
        body, name="mlp_bwd", grid=(seq // tm,),
        in_specs=[_row(tm, D_MODEL), _row(tm, D_MODEL), _row(tm, D_FF), _row(tm, D_MODEL),
                  _fix((1, D_MODEL)), _fix((1, D_MODEL)), ANY, ANY],
        out_specs=[_row(tm, D_MODEL), _row(tm, D_FF), _row(tm, D_MODEL), _fix((1, D_MODEL)), _fix((1, D_MODEL))],
        out_shape=[_sds((seq, D_MODEL), BF16), _sds((seq, D_FF), BF16), _sds((seq, D_MODEL), BF16),
                   _sds((1, D_MODEL), F32), _sds((1, D_MODEL), F32)],
        scratch_shapes=[pltpu.VMEM((w_ff1.shape[1], N_DEV * w_ff1.shape[2]), BF16), pltpu.VMEM(w_ff2.shape, BF16),
                        pltpu.SemaphoreType.DMA((N_DEV + 1,))],
        compiler_params=_params(1, 56),
    )(h1, f, zr, dh2, g_pre_mlp, g_post_mlp, w_ff1, w_ff2)


def _merge_bwd(dh1, y, gates, ba, bb, d_pool, o_heads, g_post_mix, pool_scale, w_out, w_ba, w_bb, w_pool_bf, tm):
    seq = dh1.shape[0]
    assert tm == ATTN_TILE

    def body(dh1_ref, y_ref, gates_ref, ba_ref, bb_ref, d_ref, o_ref, g_ref, ps_ref, wout_ref, wba_ref, wbb_ref, wp_ref,
             dy_ref, dba_ref, dbb_ref, dgpre_ref, dattn_ref, dd_ref, delta_ref, dg_ref, dbg_ref, dps_ref, dwp_ref):
        @pl.when(pl.program_id(0) == 0)
        def _():
            dg_ref[...] = jnp.zeros_like(dg_ref)
            dbg_ref[...] = jnp.zeros_like(dbg_ref)
            dps_ref[...] = jnp.zeros_like(dps_ref)
            dwp_ref[...] = jnp.zeros_like(dwp_ref)

        for rows in _row_chains(tm):
            dh1v = dh1_ref[rows, :].astype(F32)
            yv = y_ref[rows, :]
            r = _rms_r(yv)
            yhat = yv * r
            dg_ref[...] += _colsum(dh1v * yhat)
            dy = _rms_bwd(yhat, r, g_ref[...], dh1v).astype(BF16)
            dy_ref[rows, :] = dy
            dmerged = _mm_nt(dy, wout_ref[...])
            dbranch = []
            for half, branch_ref, dbranch_ref in ((0, ba_ref, dba_ref), (1, bb_ref, dbb_ref)):
                lanes = slice(D_MODEL * half, D_MODEL * (half + 1))
                gate = gates_ref[rows, lanes].astype(F32)
                dpre = dmerged * branch_ref[rows, :].astype(F32) * gate * (1.0 - gate)
                dbg_ref[:, lanes] += _colsum(dpre)
                dgpre_ref[rows, lanes] = dpre.astype(BF16)
                dbranch.append((dmerged * gate).astype(BF16))
                dbranch_ref[rows, :] = dbranch[-1]
            dattn = _mm_nt(dbranch[0], wba_ref[...]).astype(BF16)
            for h in range(N_HEADS):
                do_h = dattn[:, HEAD_PAD * h:HEAD_PAD * (h + 1)]
                dattn_ref[h, rows, :] = do_h
                row_term = jnp.sum(o_ref[h, rows, :].astype(F32) * do_h.astype(F32), axis=1, keepdims=True)
                delta_ref[h, :, rows] = _col_to_row(row_term, rows.stop - rows.start)
            dpooled = _mm_nt(dbranch[1], wbb_ref[...])
            for g in range(len(POOL_WINDOWS)):
                lanes = slice(POOL_GROUP * g, POOL_GROUP * (g + 1))
                dpl = dpooled[:, lanes]
                d_g = d_ref[rows, lanes]
                dps_ref[:, lanes] += _colsum(dpl * _mm(d_g, wp_ref[g]))
                dyp = (dpl * ps_ref[:, lanes]).astype(BF16)
                dwp_ref[g] += lax.dot_general(d_g, dyp, TN, preferred_element_type=F32)
                dd_ref[rows, lanes] = _mm_nt(dyp, wp_ref[g]).astype(BF16)

    heads = pl.BlockSpec((N_HEADS, tm, HEAD_PAD), lambda i: (0, i, 0))
    return pl.pallas_call(
        body, name="merge_bwd", grid=(seq // tm,),
        in_specs=[_row(tm, D_MODEL), _row(tm, D_MODEL), _row(tm, 2 * D_MODEL), _row(tm, D_MODEL), _row(tm, D_MODEL),
                  _row(tm, POOL_WIDTH), heads, _fix((1, D_MODEL)), _fix((1, POOL_WIDTH)),
                  _fix(w_out.shape), _fix(w_ba.shape), _fix(w_bb.shape), _fix(w_pool_bf.shape)],
        out_specs=[_row(tm, D_MODEL), _row(tm, D_MODEL), _row(tm, D_MODEL), _row(tm, 2 * D_MODEL),
                   heads, _row(tm, POOL_WIDTH), pl.BlockSpec((N_HEADS, None, 1, tm), lambda i: (0, i, 0, 0)),
                   _fix((1, D_MODEL)), _fix((1, 2 * D_MODEL)), _fix((1, POOL_WIDTH)), _fix(w_pool_bf.shape)],
        out_shape=[_sds((seq, D_MODEL), BF16), _sds((seq, D_MODEL), BF16), _sds((seq, D_MODEL), BF16),
                   _sds((seq, 2 * D_MODEL), BF16), _sds((N_HEADS, seq, HEAD_PAD), BF16),
                   _sds((seq, POOL_WIDTH), BF16), _sds((N_HEADS, seq // tm, 1, tm), F32),
                   _sds((1, D_MODEL), F32), _sds((1, 2 * D_MODEL), F32),
                   _sds((1, POOL_WIDTH), F32), _sds(w_pool_bf.shape, F32)],
        compiler_params=_params(1, 48),
    )(dh1, y, gates, ba, bb, d_pool, o_heads, g_post_mix, pool_scale, w_out, w_ba, w_bb, w_pool_bf)


def _proj_bwd(dq, dk, dv, qd, kvd, x, dh1, dgpre, dd, cf, s1, s2, g_pre, g_q, g_kv,
              w_in_r, w_uq_r, w_k_exp, w_v, tm):
    seq = x.shape[0]
    n_steps = seq // tm

    def body(dq_ref, dk_ref, dv_ref, qd_ref, kvd_ref, x_ref, dh1_ref, dgpre_ref, dd_ref, next_ref,
             cf_ref, s1_ref, s2_ref, gpre_ref, gq_ref, gkv_ref, win_hbm, wuq_hbm, wk_hbm, wv_hbm,
             gx_ref, dproj_ref, dqb_ref, qn_ref, kvn_ref, dkvb_ref, dgpre_acc, dgq_acc, dgkv_acc,
             win, wuq, wk, wv, load_sems):
        step = pl.program_id(0)
        _load_once(((win_hbm, win), (wuq_hbm, wuq), (wk_hbm, wk), (wv_hbm, wv)), load_sems)

        @pl.when(pl.program_id(0) == 0)
        def _():
            dgpre_acc[...] = jnp.zeros_like(dgpre_acc)
            dgq_acc[...] = jnp.zeros_like(dgq_acc)
            dgkv_acc[...] = jnp.zeros_like(dgkv_acc)

        for rows in _row_chains(tm):
            n_rows = rows.stop - rows.start
            cfv, s1v, s2v = cf_ref[rows, :], s1_ref[rows, :], s2_ref[rows, :]
            ksum = jnp.zeros((n_rows, HEAD_PAD), F32)
            for h in range(N_HEADS):
                lanes = slice(HEAD_PAD * h, HEAD_PAD * (h + 1))
                dqh = jnp.transpose(dq_ref[h, :, rows])
                dqb_ref[rows, lanes] = (_rope_t(dqh, cfv, s1v, s2v) * ATTN_SCALE).astype(BF16)
                dkh = dk_ref[h, rows, :]
                dkvb_ref[rows, lanes] = dkh
                dkvb_ref[rows, slice(K_ALL + lanes.start, K_ALL + lanes.stop)] = dv_ref[h, rows, :]
                ksum = ksum + dkh.astype(F32)
            lane = lax.broadcasted_iota(jnp.int32, (n_rows, HEAD_PAD), 1)
            rope_lanes = (lane >= QK_NOPE) & (lane < QK_NOPE + QK_ROPE)
            dkr = _rope_t(jnp.where(rope_lanes, ksum, 0.0), cfv, s1v, s2v)

            qdv = qd_ref[rows, :]
            rq = _rms_r(qdv)
            qhat = qdv * rq
            qn_ref[rows, :] = (qhat * gq_ref[...]).astype(BF16)
            dqn = _mm_nt(dqb_ref[rows, :], wuq[...])
            dgq_acc[...] += _colsum(dqn * qhat)
            dproj_ref[rows, IN_Q0:IN_KV0] = _rms_bwd(qhat, rq, gq_ref[...], dqn).astype(BF16)

            kvdv = kvd_ref[rows, :]
            rkv = _rms_r(kvdv)
            kvhat = kvdv * rkv
            kvn_ref[rows, :] = (kvhat * gkv_ref[...]).astype(BF16)
            dkvn = _mm_nt(dkvb_ref[rows, :K_ALL], wk[...]) + _mm_nt(dkvb_ref[rows, K_ALL:], wv[...])
            dgkv_acc[...] += _colsum(dkvn * kvhat)
            dproj_ref[rows, IN_KV0:IN_POOL0] = _rms_bwd(kvhat, rkv, gkv_ref[...], dkvn).astype(BF16)

            dd_t = dd_ref[rows, :].astype(F32)
            if rows.stop < tm:
                after = dd_ref[rows.stop:rows.stop + POOL_HALO, :].astype(F32)
            else:
                after = jnp.where(step == n_steps - 1, 0.0, next_ref[...].astype(F32))
            ext = jnp.concatenate([dd_t, after], axis=0)
            ext_rows = n_rows + POOL_HALO
            counts = _window_count(step * tm + rows.start, ext_rows)
            for g, window in enumerate(POOL_WINDOWS):
                lanes = slice(POOL_GROUP * g, POOL_GROUP * (g + 1))
                level = ext[:, lanes] / counts[g]
                reach = 1
                while reach < window:
                    level = level + pltpu.roll(level, ext_rows - reach, 0)
                    reach *= 2
                dproj_ref[rows, IN_POOL0 + lanes.start:IN_POOL0 + lanes.stop] = (
                    level[:n_rows] - dd_t[:, lanes]).astype(BF16)
            dproj_ref[rows, IN_GATE0:IN_KR0] = dgpre_ref[rows, :]
            dproj_ref[rows, IN_KR0:IN_R] = dkr.astype(BF16)

            da = _mm_nt(dproj_ref[rows, :], win[...])
            xv = x_ref[rows, :]
            r0 = _rms_r(xv)
            xhat = xv * r0
            dgpre_acc[...] += _colsum(da * xhat)
            gx_ref[rows, :] = dh1_ref[rows, :].astype(F32) + _rms_bwd(xhat, r0, gpre_ref[...], da)

    per_tile = ATTN_TILE // tm
    heads = pl.BlockSpec((N_HEADS, tm, HEAD_PAD), lambda i: (0, i, 0))
    heads_t = pl.BlockSpec((N_HEADS, None, HEAD_PAD, tm), lambda i: (0, i // per_tile, 0, i % per_tile))
    return pl.pallas_call(
        body, name="proj_bwd", grid=(seq // tm,),
        in_specs=[heads_t, heads, heads, _row(tm, Q_LORA), _row(tm, KV_LORA), _row(tm, D_MODEL),
                  _row(tm, D_MODEL), _row(tm, 2 * D_MODEL), _row(tm, POOL_WIDTH),
                  pl.BlockSpec((POOL_HALO, POOL_WIDTH),
                               lambda i: (jnp.minimum((i + 1) * (tm // POOL_HALO), seq // POOL_HALO - 1), 0)),
                  _row(tm, HEAD_PAD), _row(tm, HEAD_PAD), _row(tm, HEAD_PAD),
                  _fix((1, D_MODEL)), _fix((1, Q_LORA)), _fix((1, KV_LORA)), ANY, ANY, ANY, ANY],
        out_specs=[_row(tm, D_MODEL), _row(tm, IN_R), _row(tm, N_HEADS * HEAD_PAD), _row(tm, Q_LORA), _row(tm, KV_LORA),
                   _row(tm, 2 * K_ALL),
                   _fix((1, D_MODEL)), _fix((1, Q_LORA)), _fix((1, KV_LORA))],
        out_shape=[_sds((seq, D_MODEL), F32), _sds((seq, IN_R), BF16), _sds((seq, N_HEADS * HEAD_PAD), BF16),
                   _sds((seq, Q_LORA), BF16), _sds((seq, KV_LORA), BF16), _sds((seq, 2 * K_ALL), BF16),
                   _sds((1, D_MODEL), F32), _sds((1, Q_LORA), F32), _sds((1, KV_LORA), F32)],
        scratch_shapes=[pltpu.VMEM(w_in_r.shape, BF16), pltpu.VMEM(w_uq_r.shape, BF16),
                        pltpu.VMEM(w_k_exp.shape, BF16), pltpu.VMEM(w_v.shape, BF16), pltpu.SemaphoreType.DMA((4,))],
        compiler_params=_params(1, 60),
    )(dq, dk, dv, qd, kvd, x, dh1, dgpre, dd, dd, cf, s1, s2, g_pre, g_q, g_kv, w_in_r, w_uq_r, w_k_exp, w_v)


def _grad_w(a, b, name, square_a=False, by_device=False, with_bf16=False):
    by_heads = a.ndim == 3
    seq, k_dim = (a.shape[1], a.shape[0] * a.shape[2]) if by_heads else a.shape
    n_dim = b.shape[1]
    tk = min(k_dim, 1024)
    tn = n_dim // 2 if n_dim == IN_R else min(n_dim, 1024)
    ts = min(seq, 2048)
    shard = n_dim // N_DEV
    per_tile = tn // shard
    if by_device:
        out_spec = pl.BlockSpec((per_tile, tk, shard), lambda i, j, s: (j, i, 0))
        out_shape = _sds((N_DEV, k_dim, shard), F32)
    else:
        out_spec = pl.BlockSpec((tk, tn), lambda i, j, s: (i, j))
        out_shape = _sds((k_dim, n_dim), F32)

    n_seq_steps = seq // ts

    def body(a_ref, b_ref, o_ref, *narrow):
        @pl.when(pl.program_id(2) == 0)
        def _():
            o_ref[...] = jnp.zeros_like(o_ref)

        at = jnp.concatenate([a_ref[h] for h in range(a.shape[0])], axis=1) if by_heads else a_ref[...]
        if square_a:
            at = at * at
        part = lax.dot_general(at, b_ref[...], TN, preferred_element_type=F32)
        if by_device:
            for d in range(per_tile):
                o_ref[d] += part[:, d * shard:(d + 1) * shard]
        else:
            o_ref[...] += part
        if with_bf16:
            @pl.when(pl.program_id(2) == n_seq_steps - 1)
            def _():
                narrow[0][...] = o_ref[...].astype(BF16)

    return pl.pallas_call(
        body, name=name, grid=(k_dim // tk, n_dim // tn, n_seq_steps),
        in_specs=[pl.BlockSpec(a.shape[:1] + (ts, HEAD_PAD), lambda i, j, s: (0, s, 0)) if by_heads
                  else pl.BlockSpec((ts, tk), lambda i, j, s: (s, i)),
                  pl.BlockSpec((ts, tn), lambda i, j, s: (s, j))],
        out_specs=[out_spec, out_spec] if with_bf16 else out_spec,
        out_shape=[out_shape, _sds(out_shape.shape, BF16)] if with_bf16 else out_shape,
        compiler_params=_params(3, 48),
    )(a, b)


def _position():
    return lax.axis_index("x"), lax.axis_index("y"), lax.axis_index("c")


def _gather_copies(x_ref, slot, send_sems, recv_sems, local_sem, phases=("send", "forward", "finish"), sem_base=0):
    x, y, c = _position()
    me, sibling = (x, y, c), (x, y, 1 - c)
    chips = [(1 - x, y), (x, 1 - y), (1 - x, 1 - y)]

    def copy(k, block, to, src=None):
        return pltpu.make_async_remote_copy(
            src_ref=slot(*block) if src is None else src, dst_ref=slot(*block),
            send_sem=send_sems.at[sem_base + k], recv_sem=recv_sems.at[sem_base + k],
            device_id=to, device_id_type=MESH)

    mine = pltpu.make_async_copy(x_ref, slot(*me), local_sem)
    first = [copy(0, me, sibling, src=x_ref)]
    first += [copy(1 + j, me, (*chip, c), src=x_ref) for j, chip in enumerate(chips)]
    passed = [copy(4 + j, (*chip, c), sibling) for j, chip in enumerate(chips)]
    if "send" in phases:
        mine.start()
        for cp in first:
            cp.start()
    if "forward" in phases:
        for j, chip in enumerate(chips):
            copy(1 + j, (*chip, c), me).wait_recv()
            passed[j].start()
    if "finish" in phases:
        copy(0, sibling, me).wait_recv()
        for j, chip in enumerate(chips):
            copy(4 + j, (*chip, 1 - c), me).wait_recv()
        for cp in first + passed:
            cp.wait_send()
        mine.wait()


def _all_gather_hbm(block):
    def body(x_ref, out_ref, send_sems, recv_sems, local_sem):
        _gather_copies(x_ref, lambda px, py, pc: out_ref.at[4 * px + 2 * py + pc], send_sems, recv_sems, local_sem)

    return pl.pallas_call(
        body, name="gather_weights",
        in_specs=[ANY], out_specs=ANY,
        out_shape=_sds((N_DEV,) + block.shape, block.dtype),
        scratch_shapes=[pltpu.SemaphoreType.DMA((7,)), pltpu.SemaphoreType.DMA((7,)), pltpu.SemaphoreType.DMA],
    )(block)


def _replicated_update(grads, loss_block, ws, ms, vs):
    n_p = len(grads)
    sent = list(grads) + [loss_block]
    n_a = len(sent)

    def body(*refs):
        g_refs, refs = refs[:n_a], refs[n_a:]
        w_refs, m_refs, v_refs, refs = refs[:n_p], refs[n_p:2 * n_p], refs[2 * n_p:3 * n_p], refs[3 * n_p:]
        sum_refs, refs = refs[:n_a], refs[n_a:]
        d_refs, nm_refs, nv_refs, refs = refs[:n_p], refs[n_p:2 * n_p], refs[2 * n_p:3 * n_p], refs[3 * n_p:]
        bufs, (send_sems, recv_sems, local_sems) = refs[:n_a], refs[n_a:]
        x, y, c = _position()
        me = 4 * x + 2 * y + c
        local, remote = [], []
        for a in range(n_a):
            local.append(pltpu.make_async_copy(g_refs[a], bufs[a].at[me], local_sems.at[a]))
            for r in range(1, N_DEV):
                peer = (1 - x if r & 4 else x, 1 - y if r & 2 else y, 1 - c if r & 1 else c)
                remote.append(pltpu.make_async_remote_copy(
                    src_ref=g_refs[a], dst_ref=bufs[a].at[me],
                    send_sem=send_sems.at[(N_DEV - 1) * a + r - 1], recv_sem=recv_sems.at[(N_DEV - 1) * a + r - 1],
                    device_id=peer, device_id_type=MESH))
        for cp in local + remote:
            cp.start()
        for cp in remote:
            cp.wait_recv()
        for cp in remote:
            cp.wait_send()
        for cp in local:
            cp.wait()
        for a in range(n_a):
            acc = bufs[a][0]
            for d in range(1, N_DEV):
                acc = acc + bufs[a][d]
            if a == n_p:
                sum_refs[a][...] = acc
                continue
            delta, new_m, new_v = _adamw_math(acc, w_refs[a][...], m_refs[a][...], v_refs[a][...])
            sum_refs[a][...], d_refs[a][...], nm_refs[a][...], nv_refs[a][...] = acc, delta, new_m, new_v

    vmem = pl.BlockSpec(memory_space=pltpu.VMEM)
    like_w = [_sds(w.shape, F32) for w in ws]
    n_sem = (N_DEV - 1) * n_a
    outs = pl.pallas_call(
        body, name="replicated_update",
        in_specs=[vmem] * (n_a + 3 * n_p), out_specs=[vmem] * (n_a + 3 * n_p),
        out_shape=like_w + [_sds(loss_block.shape, F32)] + like_w * 3,
        scratch_shapes=[pltpu.VMEM((N_DEV,) + g.shape, F32) for g in sent]
                       + [pltpu.SemaphoreType.DMA((n_sem,)), pltpu.SemaphoreType.DMA((n_sem,)),
                          pltpu.SemaphoreType.DMA((n_a,))],
        compiler_params=pltpu.CompilerParams(vmem_limit_bytes=32 * MIB),
    )(*sent, *ws, *ms, *vs)
    return (outs[:n_p], outs[n_p], outs[n_a:n_a + n_p], outs[n_a + n_p:n_a + 2 * n_p], outs[n_a + 2 * n_p:])


def _exchange_pair(gs):
    n_w = len(gs)

    def body(*refs):
        g_refs, out_refs = refs[:n_w], refs[n_w:2 * n_w]
        send_sems, recv_sems = refs[2 * n_w:]
        x, y, c = _position()
        copies = []
        for w in range(n_w):
            for chip in range(4):
                cp = pltpu.make_async_remote_copy(
                    src_ref=g_refs[w].at[2 * chip + (1 - c)], dst_ref=out_refs[w].at[chip],
                    send_sem=send_sems.at[4 * w + chip], recv_sem=recv_sems.at[4 * w + chip],
                    device_id=(x, y, 1 - c), device_id_type=MESH)
                cp.start()
                copies.append(cp)
        for cp in copies:
            cp.wait_recv()
        for cp in copies:
            cp.wait_send()

    return pl.pallas_call(
        body, name="exchange_pair",
        in_specs=[ANY] * n_w, out_specs=[ANY] * n_w,
        out_shape=[_sds((4,) + g.shape[1:], g.dtype) for g in gs],
        scratch_shapes=[pltpu.SemaphoreType.DMA((4 * n_w,)), pltpu.SemaphoreType.DMA((4 * n_w,))],
    )(*gs)


def _exchange_chips(parts):
    n_w = len(parts)

    def body(*refs):
        p_refs, out_refs = refs[:n_w], refs[n_w:2 * n_w]
        send_sems, recv_sems = refs[2 * n_w:]
        x, y, c = _position()
        chips = [(1 - x, y), (x, 1 - y), (1 - x, 1 - y)]
        copies = []
        for w in range(n_w):
            for k, (px, py) in enumerate(chips):
                cp = pltpu.make_async_remote_copy(
                    src_ref=p_refs[w].at[2 * px + py], dst_ref=out_refs[w].at[k],
                    send_sem=send_sems.at[3 * w + k], recv_sem=recv_sems.at[3 * w + k],
                    device_id=(px, py, c), device_id_type=MESH)
                cp.start()
                copies.append(cp)
        for cp in copies:
            cp.wait_recv()
        for cp in copies:
            cp.wait_send()

    return pl.pallas_call(
        body, name="exchange_chips",
        in_specs=[ANY] * n_w, out_specs=[ANY] * n_w,
        out_shape=[_sds((3,) + p.shape[1:], p.dtype) for p in parts],
        scratch_shapes=[pltpu.SemaphoreType.DMA((3 * n_w,)), pltpu.SemaphoreType.DMA((3 * n_w,))],
    )(*parts)


def _row_tile(k):
    return 256 if k % 256 == 0 else 128


def _pair_sum(g, recv, place, name):
    _, k, n = g.shape
    tr = _row_tile(k)
    g4 = g.reshape(4, 2, k, n)

    def body(s_ref, g_ref, r_ref, o_ref):
        o_ref[...] = (g_ref[...] + r_ref[...]).astype(BF16)

    spec = pltpu.PrefetchScalarGridSpec(
        num_scalar_prefetch=1, grid=(4, k // tr),
        in_specs=[pl.BlockSpec((None, None, tr, n), lambda j, i, s: (j, s[2], i, 0)),
                  pl.BlockSpec((None, tr, n), lambda j, i, s: (j, i, 0))],
        out_specs=pl.BlockSpec((None, tr, n), lambda j, i, s: (j, i, 0)))
    return pl.pallas_call(
        body, name=name, grid_spec=spec, out_shape=_sds((4, k, n), BF16),
        compiler_params=_params(2, 32),
    )(place, g4, recv)


def _adamw_math(g, w, m, v):
    m = ADAM_B1 * m + (1.0 - ADAM_B1) * g
    v = ADAM_B2 * v + (1.0 - ADAM_B2) * (g * g)
    m_hat = m / (1.0 - ADAM_B1 ** ADAM_STEP)
    v_hat = v / (1.0 - ADAM_B2 ** ADAM_STEP)
    delta = -ADAM_LR * (m_hat / (jnp.sqrt(v_hat) + ADAM_EPS) + ADAM_WD * w)
    return delta, m, v


def _adamw_sharded(g, from_sibling, from_chips, place, w, m, v, name):
    _, k, n = g.shape
    tr = _row_tile(k)

    def body(s_ref, g_ref, sib_ref, r0_ref, r1_ref, r2_ref, w_ref, m_ref, v_ref, grad_ref, d_ref, nm_ref, nv_ref):
        grad = g_ref[...] + sib_ref[...]
        for r_ref in (r0_ref, r1_ref, r2_ref):
            grad = grad + r_ref[...].astype(F32)
        grad_ref[...] = grad
        d_ref[...], nm_ref[...], nv_ref[...] = _adamw_math(grad, w_ref[...], m_ref[...], v_ref[...])

    tile = pl.BlockSpec((None, tr, n), lambda i, s: (0, i, 0))

    def slot(j):
        return pl.BlockSpec((None, tr, n), lambda i, s: (j, i, 0))

    spec = pltpu.PrefetchScalarGridSpec(
        num_scalar_prefetch=1, grid=(k // tr,),
        in_specs=[pl.BlockSpec((None, tr, n), lambda i, s: (s[0], i, 0)),
                  pl.BlockSpec((None, tr, n), lambda i, s: (s[1], i, 0)),
                  slot(0), slot(1), slot(2), tile, tile, tile],
        out_specs=[tile] * 4)
    return pl.pallas_call(
        body, name=name, grid_spec=spec, out_shape=[_sds((1, k, n), F32)] * 4,
        compiler_params=_params(1, 48),
    )(place, g, from_sibling, from_chips, from_chips, from_chips, w, m, v)


def _adamw_direct(g, received, place, w, m, v, name):
    _, k, n = g.shape
    tr = _row_tile(k)

    def body(s_ref, g_ref, r_ref, w_ref, m_ref, v_ref, grad_ref, d_ref, nm_ref, nv_ref):
        grad = g_ref[...]
        for r in range(N_DEV - 1):
            grad = grad + r_ref[r].astype(F32)
        grad_ref[...] = grad
        d_ref[...], nm_ref[...], nv_ref[...] = _adamw_math(grad, w_ref[...], m_ref[...], v_ref[...])

    tile = pl.BlockSpec((None, tr, n), lambda i, s: (0, i, 0))
    spec = pltpu.PrefetchScalarGridSpec(
        num_scalar_prefetch=1, grid=(k // tr,),
        in_specs=[pl.BlockSpec((None, tr, n), lambda i, s: (s[0], i, 0)),
                  pl.BlockSpec((N_DEV - 1, tr, n), lambda i, s: (0, i, 0)), tile, tile, tile],
        out_specs=[tile] * 4)
    return pl.pallas_call(
        body, name=name, grid_spec=spec, out_shape=[_sds((1, k, n), F32)] * 4,
        compiler_params=_params(1, 48),
    )(place, g, received, w, m, v)


def _pack_rows(parts):
    parts = [a.reshape(-1, LANES) for a in parts]
    pad = (-sum(a.shape[0] for a in parts)) % PACK_ROW_TILE
    return jnp.concatenate(parts + [jnp.zeros((pad, LANES), parts[0].dtype)], axis=0)


def _full_from_gathered(gathered, entries, shard_shapes):
    out, off = {}, 0
    for (name, kind), (k, n) in zip(entries, shard_shapes):
        rows = k * n // LANES
        seg = gathered[:, off:off + rows].reshape(N_DEV, k, n)
        out[name] = jnp.transpose(seg, (1, 0, 2)).reshape(k, N_DEV * n) if kind == "col" else seg.reshape(N_DEV * k, n)
        off += rows
    return out


def _columns_by_device(a):
    k, n_all = a.shape
    return jnp.transpose(a.reshape(k, N_DEV, n_all // N_DEV), (1, 0, 2))


def _rows_by_device(a):
    k_all, n = a.shape
    return a.reshape(N_DEV, k_all // N_DEV, n)


def _rope_lane_frequencies():
    inv_freq = ROPE_THETA ** (-jnp.arange(0, QK_ROPE, 2, dtype=F32) / QK_ROPE)
    zeros = lambda n: jnp.zeros((n,), F32)
    return jnp.concatenate([zeros(QK_NOPE), inv_freq, inv_freq, zeros(HEAD_PAD - QK_NOPE - QK_ROPE)])[None, :]


def _rope_tables(pos_row, freq, tm):
    pos = jnp.transpose(jnp.broadcast_to(pos_row.astype(F32), (HEAD_PAD, tm)))
    ang = pos * freq
    cos, sin = jnp.cos(ang), jnp.sin(ang)
    lane = lax.broadcasted_iota(jnp.int32, (tm, HEAD_PAD), 1)
    first = (lane >= QK_NOPE) & (lane < QK_NOPE + QK_ROPE // 2)
    second = (lane >= QK_NOPE + QK_ROPE // 2) & (lane < QK_NOPE + QK_ROPE)
    cf = jnp.where(lane < QK_NOPE, 1.0, jnp.where(first | second, cos, 0.0))
    return cf, jnp.where(first, -sin, 0.0), jnp.where(second, sin, 0.0)


def _arrange_w_in(w):
    k = w.shape[0]
    zeros = lambda n: jnp.zeros((k, n), w.dtype)
    kr0 = Q_LORA + KV_LORA
    pool0 = kr0 + QK_ROPE
    return jnp.concatenate([w[:, :kr0], w[:, pool0:], zeros(QK_NOPE), w[:, kr0:pool0],
                            zeros(HEAD_PAD - QK_NOPE - QK_ROPE)], axis=1)


def _restore_w_in(d):
    kr = d[:, IN_KR0 + QK_NOPE:IN_KR0 + QK_NOPE + QK_ROPE]
    return jnp.concatenate([d[:, :IN_POOL0], kr, d[:, IN_POOL0:IN_KR0]], axis=1)


def _pad_heads(w, width):
    k = w.shape[0]
    w = w.reshape(k, N_HEADS, width)
    return jnp.pad(w, ((0, 0), (0, 0), (0, HEAD_PAD - width))).reshape(k, N_HEADS * HEAD_PAD)


def _unpad_heads(d, width):
    k = d.shape[0]
    return d.reshape(k, N_HEADS, HEAD_PAD)[:, :, :width]


def kernel(x, p, positions, g_pre_mix, w_in, b_gate, g_q, w_uq, g_kv, w_ukv, w_pool, pool_scale, w_branch_attn, w_branch_pool, w_out, g_post_mix, g_pre_mlp, w_ff1, w_ff2, g_post_mlp, w_ple_proj, w_ple_gate, g_ple, loss_target, m_g_pre_mix, m_w_in, m_b_gate, m_g_q, m_w_uq, m_g_kv, m_w_ukv, m_w_pool, m_pool_scale, m_w_branch_attn, m_w_branch_pool, m_w_out, m_g_post_mix, m_g_pre_mlp, m_w_ff1, m_w_ff2, m_g_post_mlp, m_w_ple_proj, m_w_ple_gate, m_g_ple, v_g_pre_mix, v_w_in, v_b_gate, v_g_q, v_w_uq, v_g_kv, v_w_ukv, v_w_pool, v_pool_scale, v_w_branch_attn, v_w_branch_pool, v_w_out, v_g_post_mix, v_g_pre_mlp, v_w_ff1, v_w_ff2, v_g_post_mlp, v_w_ple_proj, v_w_ple_gate, v_g_ple):
    given = dict(locals())
    weights = {n: given[n] for n in WEIGHT_ORDER}
    moments_m = {n: given["m_" + n] for n in WEIGHT_ORDER}
    moments_v = {n: given["v_" + n] for n in WEIGHT_ORDER}
    xs, ps, target = x[0], p[0, 0], loss_target[0]
    seq = xs.shape[0]
    tm = min(256, seq)
    tm_merge = min(512, seq)
    core = lax.axis_index("c")
    chip = 2 * lax.axis_index("x") + lax.axis_index("y")

    early, later = SHARDED[:N_EARLY], SHARDED[N_EARLY:]
    shapes_of = lambda entries: [weights[n].shape[1:] for n, _ in entries]
    pack_bf16 = lambda entries: _pack_rows([weights[n][0].astype(BF16) for n, _ in entries])
    full = _full_from_gathered(_all_gather_hbm(pack_bf16(early)), early, shapes_of(early))
    w_in_r = _arrange_w_in(full["w_in"])
    w_uq_r = _pad_heads(full["w_uq"], QK_NOPE + QK_ROPE)
    ukv = full["w_ukv"].reshape(KV_LORA, N_HEADS, QK_NOPE + V_HEAD)
    w_k_exp = _pad_heads(ukv[:, :, :QK_NOPE].reshape(KV_LORA, N_HEADS * QK_NOPE), QK_NOPE)
    w_v = _pad_heads(ukv[:, :, QK_NOPE:].reshape(KV_LORA, N_HEADS * V_HEAD), V_HEAD)
    w_pool_bf = w_pool[0].astype(BF16)

    packed_later = [e for e in later if e[0] != "w_ff1"]
    a_bf, qd, kvd, d_pool, pooled, gates, q, k, v, k_t, cf, s1, s2, gathered_later, gathered_ff1 = _proj_fwd(
        xs, g_pre_mix, b_gate, g_q, g_kv, positions, w_pool_bf, pool_scale, w_in_r, w_uq_r, w_k_exp, w_v,
        [pack_bf16(packed_later), w_ff1[0].astype(BF16)], tm_merge)
    full.update(_full_from_gathered(gathered_later, packed_later, shapes_of(packed_later)))
    full["w_ff1"] = gathered_ff1
    w_ba = jnp.pad(full["w_branch_attn"].reshape(N_HEADS, V_HEAD, D_MODEL),
                   ((0, 0), (0, HEAD_PAD - V_HEAD), (0, 0))).reshape(N_HEADS * HEAD_PAD, D_MODEL)
    o_heads, lse = _attn_fwd(q, k, v)
    merged, ba, bb, y, h1 = _merge_fwd(o_heads, pooled, gates, xs, g_post_mix, w_ba,
                                                  full["w_branch_pool"], full["w_out"], tm_merge)
    (m_bf, zr, f, h2_bf, p_bf, de, dpre, dh2, loss_acc, dg_ple) = _tail_fwd(
        h1, target, ps, g_pre_mlp, g_post_mlp, g_ple, full["w_ff1"], full["w_ff2"], full["w_ple_proj"],
        full["w_ple_gate"], tm)

    by_device, payload = {}, {}

    def keep(name, pair, layout=lambda g: g):
        by_device[name], payload[name] = layout(pair[0]), layout(pair[1])

    keep("w_ple_proj", _grad_w(p_bf, de, "grad_w_ple_proj", by_device=True, with_bf16=True))
    keep("w_ple_gate", _grad_w(h2_bf, dpre, "grad_w_ple_gate", with_bf16=True), _rows_by_device)
    df, dz, dh1, dg_pre_mlp, dg_post_mlp = _mlp_bwd(h1, f, zr, dh2, g_pre_mlp, g_post_mlp, full["w_ff1"],
                                                    full["w_ff2"], tm)
    keep("w_ff1", _grad_w(m_bf, dz, "grad_w_ff1", by_device=True, with_bf16=True))
    keep("w_ff2", _grad_w(zr, df, "grad_w_ff2", square_a=True, with_bf16=True), _rows_by_device)
    (dy, dba, dbb, dgpre, do_heads, dd, delta, dg_post_mix, db_gate, dpool_scale, dw_pool) = _merge_bwd(
        dh1, y, gates, ba, bb, d_pool, o_heads, g_post_mix, pool_scale, full["w_out"], w_ba,
        full["w_branch_pool"], w_pool_bf, tm_merge)
    keep("w_branch_attn", _grad_w(o_heads, dba, "grad_w_branch_attn", with_bf16=True),
          lambda g: _columns_by_device(g.reshape(N_HEADS, HEAD_PAD, D_MODEL)[:, :V_HEAD].reshape(-1, D_MODEL)))
    keep("w_branch_pool", _grad_w(pooled, dbb, "grad_w_branch_pool", by_device=True, with_bf16=True))
    keep("w_out", _grad_w(merged, dy, "grad_w_out", with_bf16=True), _rows_by_device)
    direct = [n for n, _ in SHARDED[N_EARLY:]]
    outs = _attn_bwd(q, k, k_t, v, do_heads, lse, delta, [payload[n] for n in direct])
    dq, dk, dv = outs[:3]
    received = dict(zip(direct, outs[3:]))
    (grad_x, dproj, dq_bf, qn_bf, kvn_bf, dkv_bf, dg_pre_mix, dg_q, dg_kv) = _proj_bwd(
        dq, dk, dv, qd, kvd, xs, dh1, dgpre, dd, cf, s1, s2, g_pre_mix, g_q, g_kv, w_in_r, w_uq_r, w_k_exp, w_v,
        tm_merge)
    d_w_kv = _grad_w(kvn_bf, dkv_bf, "grad_w_ukv")
    d_k_exp = _unpad_heads(d_w_kv[:, :K_ALL], QK_NOPE)
    d_w_v = _unpad_heads(d_w_kv[:, K_ALL:], V_HEAD)
    by_device["w_in"] = _columns_by_device(_restore_w_in(_grad_w(a_bf, dproj, "grad_w_in")))
    by_device["w_uq"] = _columns_by_device(
        _unpad_heads(_grad_w(qn_bf, dq_bf, "grad_w_uq"), QK_NOPE + QK_ROPE).reshape(Q_LORA, -1))
    by_device["w_ukv"] = _columns_by_device(jnp.concatenate([d_k_exp, d_w_v], axis=2).reshape(KV_LORA, -1))
    grads_small = {
        "g_pre_mix": dg_pre_mix, "b_gate": db_gate, "g_q": dg_q, "g_kv": dg_kv,
        "w_pool": dw_pool, "pool_scale": dpool_scale, "g_post_mix": dg_post_mix,
        "g_pre_mlp": dg_pre_mlp, "g_post_mlp": dg_post_mlp, "g_ple": dg_ple,
    }

    names = [n for n, _ in SHARDED]
    place = jnp.stack([2 * chip + core, chip, core]).astype(jnp.int32)
    sharded = {n: _adamw_direct(by_device[n], received[n], place, weights[n], moments_m[n], moments_v[n],
                                "adamw_" + n) for n in direct}
    last = [n for n, _ in SHARDED[:N_EARLY]]
    own = [by_device[n] for n in last]
    from_sibling = _exchange_pair(own)
    pair = [_pair_sum(g, r, place, "pair_sum_" + n) for n, g, r in zip(last, own, from_sibling)]
    from_chips = _exchange_chips(pair)
    sharded.update({n: _adamw_sharded(g, r, rc, place, weights[n], moments_m[n], moments_v[n], "adamw_" + n)
                    for n, g, r, rc in zip(last, own, from_sibling, from_chips)})

    flat = lambda a: a.reshape(a.shape[-3:]) if a.ndim > 3 else a
    g_sm, loss_sum, d_sm, m_sm, v_sm = _replicated_update(
        [flat(grads_small[n]) for n in REPLICATED], loss_acc, [flat(weights[n]) for n in REPLICATED],
        [flat(moments_m[n]) for n in REPLICATED], [flat(moments_v[n]) for n in REPLICATED])

    results = []
    for which, small in enumerate((g_sm, d_sm, m_sm, v_sm)):
        named = {n: sharded[n][which] for n in names}
        named.update({n: a.reshape(weights[n].shape) for n, a in zip(REPLICATED, small)})
        results.append([named[n] for n in WEIGHT_ORDER])

    return (loss_sum[0, 0], grad_x[None], *results[0], *results[1], *results[2], *results[3])
```
